```python
import jax, jax.numpy as jnp
from jax import lax
import numpy as np

D_MODEL = 1024
BATCH = 8
SEQ = 8192
DEPTH = 1

HG_HEADS = 8
HG_DIM = 128
HG_WIDTH = HG_HEADS * HG_DIM
HG_CHUNK = 32
MLA_HEADS = 8
QK_NOPE = 128
QK_ROPE = 64
QK_DIM = QK_NOPE + QK_ROPE
V_DIM = 128
Q_LORA = 3 * D_MODEL // 8
KV_LORA = D_MODEL // 4
MLA_WIDTH = MLA_HEADS * V_DIM
Q_BLOCK = 128
ROPE_THETA = 10000.0
N_BRANCH = 2
EPS = 1e-6
IN_SPLITS = (HG_WIDTH, HG_WIDTH, HG_WIDTH, HG_WIDTH,
             Q_LORA, KV_LORA, QK_ROPE, MLA_WIDTH,
             N_BRANCH * D_MODEL)
IN_COLS = sum(IN_SPLITS)

kernel_name = "hgrn2_mla_gated_parallel_hybrid"


def _split_points():
    pts, acc = [], 0
    for w in IN_SPLITS[:-1]:
        acc += w
        pts.append(acc)
    return pts


def rms_norm(x, g):
    xf = x.astype(jnp.float32)
    y = xf * lax.rsqrt(jnp.mean(xf * xf, axis=-1, keepdims=True) + EPS)
    return (y * g.astype(jnp.float32)).astype(x.dtype)


def forget_lower_bounds(lb_logits):
    return jnp.cumsum(jax.nn.softmax(lb_logits.astype(jnp.float32), axis=0), axis=0)[:DEPTH]


def rope_tables(seq):
    inv = ROPE_THETA ** (-jnp.arange(0, QK_ROPE, 2, dtype=jnp.float32) / QK_ROPE)
    ang = jnp.arange(seq, dtype=jnp.float32)[:, None] * inv[None, :]
    return jnp.cos(ang), jnp.sin(ang)


def apply_rope(x, cos, sin):
    xf = x.astype(jnp.float32)
    x1, x2 = xf[..., : QK_ROPE // 2], xf[..., QK_ROPE // 2:]
    out = jnp.concatenate([x1 * cos - x2 * sin, x2 * cos + x1 * sin], axis=-1)
    return out.astype(x.dtype)


def hgrn2_recurrence(q, k, v, log_f):
    B, S, H, dk = q.shape
    dv = v.shape[-1]
    C = HG_CHUNK
    N = S // C

    def to_chunks(t):
        return t.reshape(B, N, C, H, t.shape[-1]).transpose(1, 0, 3, 2, 4)

    q, k, v, log_f = to_chunks(q), to_chunks(k), to_chunks(v), to_chunks(log_f)
    b = jnp.cumsum(log_f, axis=3)
    b_last = b[:, :, :, -1:, :]
    q_in = q * jnp.exp(b)
    k_in = k * jnp.exp(-b)
    k_out = k * jnp.exp(b_last - b)
    chunk_decay = jnp.exp(b_last[:, :, :, 0, :])

    causal = jnp.tril(jnp.ones((C, C), dtype=bool))
    scores = jnp.einsum('nbhtk,nbhsk->nbhts', q_in, k_in)
    scores = jnp.where(causal, scores, 0.0)
    o_intra = jnp.einsum('nbhts,nbhsv->nbhtv', scores, v)

    def step(state, inp):
        q_n, k_n, v_n, dec_n = inp
        o_inter = jnp.einsum('bhtk,bhkv->bhtv', q_n, state)
        state = state * dec_n[..., None] + jnp.einsum('bhsk,bhsv->bhkv', k_n, v_n)
        return state, o_inter

    state0 = jnp.zeros((B, H, dk, dv), jnp.float32)
    _, o_inter = lax.scan(step, state0, (q_in, k_out, v, chunk_decay))
    o = o_intra + o_inter
    return o.transpose(1, 0, 3, 2, 4).reshape(B, S, H, dv)


def hgrn2_branch(hq, hf, hi, hz, lb, hg_norm_g):
    B, S, _ = hq.shape
    dt = hq.dtype
    f = lb + (1.0 - lb) * jax.nn.sigmoid(hf.astype(jnp.float32))
    q = jax.nn.silu(hq.astype(jnp.float32)).reshape(B, S, HG_HEADS, HG_DIM)
    k = (1.0 - f).reshape(B, S, HG_HEADS, HG_DIM)
    v = hi.astype(jnp.float32).reshape(B, S, HG_HEADS, HG_DIM)
    log_f = jnp.log(f).reshape(B, S, HG_HEADS, HG_DIM)
    o = hgrn2_recurrence(q, k, v, log_f).astype(dt)
    o = rms_norm(o, hg_norm_g)
    o = o * jax.nn.silu(hz).reshape(B, S, HG_HEADS, HG_DIM)
    return o.reshape(B, S, HG_WIDTH)


def mla_branch(cq, ckv, kr, mz, q_a_g, w_uq, kv_a_g, w_ukv):
    B, S, _ = cq.shape
    cos, sin = rope_tables(S)
    q = (rms_norm(cq, q_a_g) @ w_uq).reshape(B, S, MLA_HEADS, QK_DIM)
    q_nope = q[..., :QK_NOPE]
    q_pe = apply_rope(q[..., QK_NOPE:], cos[:, None, :], sin[:, None, :])
    kv = (rms_norm(ckv, kv_a_g) @ w_ukv).reshape(B, S, MLA_HEADS, QK_NOPE + V_DIM)
    k_nope, v = kv[..., :QK_NOPE], kv[..., QK_NOPE:]
    k_pe = apply_rope(kr, cos, sin)
    scale = QK_DIM ** -0.5
    key_pos = jnp.arange(S)

    def attend_block(blk):
        start = blk * Q_BLOCK
        qn = lax.dynamic_slice_in_dim(q_nope, start, Q_BLOCK, axis=1)
        qp = lax.dynamic_slice_in_dim(q_pe, start, Q_BLOCK, axis=1)
        s = (jnp.einsum('bqhd,bkhd->bhqk', qn, k_nope)
             + jnp.einsum('bqhr,bkr->bhqk', qp, k_pe)).astype(jnp.float32) * scale
        q_pos = start + jnp.arange(Q_BLOCK)
        s = jnp.where(q_pos[:, None] >= key_pos[None, :], s, -jnp.inf)
        p = jax.nn.softmax(s, axis=-1).astype(v.dtype)
        return jnp.einsum('bhqk,bkhd->bqhd', p, v)

    out = lax.map(attend_block, jnp.arange(S // Q_BLOCK))
    out = out.transpose(1, 0, 2, 3, 4).reshape(B, S, MLA_WIDTH)
    return out * jax.nn.silu(mz)


def _fwd_setup_inputs(seed: int = 0) -> dict:
    key = jax.random.key(seed)
    ks = jax.random.split(key, 16)
    f32 = jnp.float32

    def w(k, shape, fan_in):
        return jax.random.normal(k, shape, f32) * fan_in ** -0.5

    def gain(k, shape):
        return 1.0 + 0.02 * jax.random.normal(k, shape, f32)

    return {
        "x": jax.random.normal(ks[0], (BATCH, SEQ, D_MODEL), f32),
        "norm_g": gain(ks[1], (DEPTH, D_MODEL)),
        "w_in": w(ks[2], (DEPTH, D_MODEL, IN_COLS), D_MODEL),
        "b_gate": 0.02 * jax.random.normal(ks[3], (DEPTH, N_BRANCH * D_MODEL), f32),
        "lb_logits": 0.1 * jax.random.normal(ks[4], (DEPTH + 1, HG_WIDTH), f32),
        "hg_norm_g": gain(ks[5], (DEPTH, HG_DIM)),
        "q_a_g": gain(ks[6], (DEPTH, Q_LORA)),
        "w_uq": w(ks[7], (DEPTH, Q_LORA, MLA_HEADS * QK_DIM), Q_LORA),
        "kv_a_g": gain(ks[8], (DEPTH, KV_LORA)),
        "w_ukv": w(ks[9], (DEPTH, KV_LORA, MLA_HEADS * (QK_NOPE + V_DIM)), KV_LORA),
        "w_proj_a": w(ks[10], (DEPTH, HG_WIDTH, D_MODEL), HG_WIDTH),
        "w_proj_b": w(ks[11], (DEPTH, MLA_WIDTH, D_MODEL), MLA_WIDTH),
        "w_out": w(ks[12], (DEPTH, D_MODEL, D_MODEL), D_MODEL),
        "final_norm_g": gain(ks[13], (D_MODEL,)),
    }


def _fwd_reference(x, norm_g, w_in, b_gate, lb_logits, hg_norm_g, q_a_g, w_uq, kv_a_g, w_ukv,
              w_proj_a, w_proj_b, w_out, final_norm_g):
    B, S, _ = x.shape
    lower_bounds = forget_lower_bounds(lb_logits)
    pts = _split_points()
    for l in range(DEPTH):
        h = rms_norm(x, norm_g[l])
        proj = h @ w_in[l]
        hq, hf, hi, hz, cq, ckv, kr, mz, glog = jnp.split(proj, pts, axis=-1)
        y_a = hgrn2_branch(hq, hf, hi, hz, lower_bounds[l], hg_norm_g[l])
        y_b = mla_branch(cq, ckv, kr, mz, q_a_g[l], w_uq[l], kv_a_g[l], w_ukv[l])
        gates = jax.nn.sigmoid((glog + b_gate[l]).astype(jnp.float32)).astype(x.dtype)
        gates = gates.reshape(B, S, N_BRANCH, D_MODEL)
        merged = gates[:, :, 0] * (y_a @ w_proj_a[l]) + gates[:, :, 1] * (y_b @ w_proj_b[l])
        x = x + merged @ w_out[l]
    return rms_norm(x, final_norm_g)


import jax as _jax
import jax.numpy as _jnp

TWIN_FORMAT = 'train_step'
FWD_PARAMS = ['x', 'norm_g', 'w_in', 'b_gate', 'lb_logits', 'hg_norm_g', 'q_a_g', 'w_uq', 'kv_a_g', 'w_ukv', 'w_proj_a', 'w_proj_b', 'w_out', 'final_norm_g']
TWIN_WEIGHTS = ['norm_g', 'w_in', 'b_gate', 'lb_logits', 'hg_norm_g', 'q_a_g', 'w_uq', 'kv_a_g', 'w_ukv', 'w_proj_a', 'w_proj_b', 'w_out', 'final_norm_g']
TWIN_DIFF_INPUT = 'x'
TWIN_INPUTS = ['x', 'norm_g', 'w_in', 'b_gate', 'lb_logits', 'hg_norm_g', 'q_a_g', 'w_uq', 'kv_a_g', 'w_ukv', 'w_proj_a', 'w_proj_b', 'w_out', 'final_norm_g', 'loss_target', 'm_norm_g', 'm_w_in', 'm_b_gate', 'm_lb_logits', 'm_hg_norm_g', 'm_q_a_g', 'm_w_uq', 'm_kv_a_g', 'm_w_ukv', 'm_w_proj_a', 'm_w_proj_b', 'm_w_out', 'm_final_norm_g', 'v_norm_g', 'v_w_in', 'v_b_gate', 'v_lb_logits', 'v_hg_norm_g', 'v_q_a_g', 'v_w_uq', 'v_kv_a_g', 'v_w_ukv', 'v_w_proj_a', 'v_w_proj_b', 'v_w_out', 'v_final_norm_g']
TWIN_OUTPUTS = ['loss', 'grad_x', 'grad_norm_g', 'grad_w_in', 'grad_b_gate', 'grad_lb_logits', 'grad_hg_norm_g', 'grad_q_a_g', 'grad_w_uq', 'grad_kv_a_g', 'grad_w_ukv', 'grad_w_proj_a', 'grad_w_proj_b', 'grad_w_out', 'grad_final_norm_g', 'delta_norm_g', 'delta_w_in', 'delta_b_gate', 'delta_lb_logits', 'delta_hg_norm_g', 'delta_q_a_g', 'delta_w_uq', 'delta_kv_a_g', 'delta_w_ukv', 'delta_w_proj_a', 'delta_w_proj_b', 'delta_w_out', 'delta_final_norm_g', 'new_m_norm_g', 'new_m_w_in', 'new_m_b_gate', 'new_m_lb_logits', 'new_m_hg_norm_g', 'new_m_q_a_g', 'new_m_w_uq', 'new_m_kv_a_g', 'new_m_w_ukv', 'new_m_w_proj_a', 'new_m_w_proj_b', 'new_m_w_out', 'new_m_final_norm_g', 'new_v_norm_g', 'new_v_w_in', 'new_v_b_gate', 'new_v_lb_logits', 'new_v_hg_norm_g', 'new_v_q_a_g', 'new_v_w_uq', 'new_v_kv_a_g', 'new_v_w_ukv', 'new_v_w_proj_a', 'new_v_w_proj_b', 'new_v_w_out', 'new_v_final_norm_g']
TWIN_LEAF_KINDS = {'loss': 'loss', 'grad_x': 'grad_x', 'grad_norm_g': 'grad_w', 'grad_w_in': 'grad_w', 'grad_b_gate': 'grad_w', 'grad_lb_logits': 'grad_w', 'grad_hg_norm_g': 'grad_w', 'grad_q_a_g': 'grad_w', 'grad_w_uq': 'grad_w', 'grad_kv_a_g': 'grad_w', 'grad_w_ukv': 'grad_w', 'grad_w_proj_a': 'grad_w', 'grad_w_proj_b': 'grad_w', 'grad_w_out': 'grad_w', 'grad_final_norm_g': 'grad_w', 'delta_norm_g': 'delta_w', 'delta_w_in': 'delta_w', 'delta_b_gate': 'delta_w', 'delta_lb_logits': 'delta_w', 'delta_hg_norm_g': 'delta_w', 'delta_q_a_g': 'delta_w', 'delta_w_uq': 'delta_w', 'delta_kv_a_g': 'delta_w', 'delta_w_ukv': 'delta_w', 'delta_w_proj_a': 'delta_w', 'delta_w_proj_b': 'delta_w', 'delta_w_out': 'delta_w', 'delta_final_norm_g': 'delta_w', 'new_m_norm_g': 'new_m', 'new_m_w_in': 'new_m', 'new_m_b_gate': 'new_m', 'new_m_lb_logits': 'new_m', 'new_m_hg_norm_g': 'new_m', 'new_m_q_a_g': 'new_m', 'new_m_w_uq': 'new_m', 'new_m_kv_a_g': 'new_m', 'new_m_w_ukv': 'new_m', 'new_m_w_proj_a': 'new_m', 'new_m_w_proj_b': 'new_m', 'new_m_w_out': 'new_m', 'new_m_final_norm_g': 'new_m', 'new_v_norm_g': 'new_v', 'new_v_w_in': 'new_v', 'new_v_b_gate': 'new_v', 'new_v_lb_logits': 'new_v', 'new_v_hg_norm_g': 'new_v', 'new_v_q_a_g': 'new_v', 'new_v_w_uq': 'new_v', 'new_v_kv_a_g': 'new_v', 'new_v_w_ukv': 'new_v', 'new_v_w_proj_a': 'new_v', 'new_v_w_proj_b': 'new_v', 'new_v_w_out': 'new_v', 'new_v_final_norm_g': 'new_v'}


def _forward(args):
    return _fwd_reference(*[args[k] for k in FWD_PARAMS])


def _output_shape():
    out = _jax.eval_shape(lambda: _forward(_fwd_setup_inputs(0)))
    return out.shape, out.dtype

N_MICROBATCH = 1
ADAM_LR = 0.001
ADAM_B1 = 0.9
ADAM_B2 = 0.999
ADAM_EPS = 1e-08
ADAM_WD = 0.01
ADAM_STEP = 10
PER_EXAMPLE_BATCH_AXIS = {'x': 0, 'loss_target': 0}
SHARED_INPUTS = []
_WEIGHT_DTYPES = {'norm_g': _jnp.float32, 'w_in': _jnp.float32, 'b_gate': _jnp.float32, 'lb_logits': _jnp.float32, 'hg_norm_g': _jnp.float32, 'q_a_g': _jnp.float32, 'w_uq': _jnp.float32, 'kv_a_g': _jnp.float32, 'w_ukv': _jnp.float32, 'w_proj_a': _jnp.float32, 'w_proj_b': _jnp.float32, 'w_out': _jnp.float32, 'final_norm_g': _jnp.float32}
MOMENT_SCALE = {'norm_g': 1.156174e-01, 'w_in': 4.315888e-02, 'b_gate': 2.250798e-02, 'lb_logits': 7.233551e-03, 'hg_norm_g': 2.194686e-01, 'q_a_g': 2.336998e-02, 'w_uq': 1.167582e-02, 'kv_a_g': 4.400831e-02, 'w_ukv': 1.468468e-02, 'w_proj_a': 7.665569e-02, 'w_proj_b': 1.683889e-02, 'w_out': 7.834921e-02, 'final_norm_g': 6.395048e+01}


def _to_microbatches(a, axis):
    t = _jnp.moveaxis(a, axis, 0)
    t = t.reshape((N_MICROBATCH, t.shape[0] // N_MICROBATCH) + t.shape[1:])
    return _jnp.moveaxis(t, 1, axis + 1)


def setup_inputs(seed: int = 0) -> dict:
    inp = _fwd_setup_inputs(seed)
    key = _jax.random.fold_in(_jax.random.key(seed), 7919)
    shape, _ = _output_shape()
    out = dict(inp)
    out["loss_target"] = _jax.random.normal(_jax.random.fold_in(key, 0), shape, _jnp.float32)
    for i, name in enumerate(TWIN_WEIGHTS):
        w = inp[name].astype(_jnp.float32)
        if MOMENT_SCALE is None:
            s = _jnp.sqrt(_jnp.mean(_jnp.square(w)) + 1e-30)
        else:
            s = MOMENT_SCALE[name]
        km, kv = _jax.random.split(_jax.random.fold_in(key, i + 1))
        out[name] = w
        out["m_" + name] = s * _jax.random.normal(km, w.shape, _jnp.float32)
        out["v_" + name] = (s * s) * _jax.random.uniform(kv, w.shape, _jnp.float32, 0.5, 1.5)
    if N_MICROBATCH > 1:
        for name, axis in PER_EXAMPLE_BATCH_AXIS.items():
            out[name] = _to_microbatches(out[name], axis)
    return {'x': out['x'], 'norm_g': out['norm_g'], 'w_in': out['w_in'], 'b_gate': out['b_gate'], 'lb_logits': out['lb_logits'], 'hg_norm_g': out['hg_norm_g'], 'q_a_g': out['q_a_g'], 'w_uq': out['w_uq'], 'kv_a_g': out['kv_a_g'], 'w_ukv': out['w_ukv'], 'w_proj_a': out['w_proj_a'], 'w_proj_b': out['w_proj_b'], 'w_out': out['w_out'], 'final_norm_g': out['final_norm_g'], 'loss_target': out['loss_target'], 'm_norm_g': out['m_norm_g'], 'm_w_in': out['m_w_in'], 'm_b_gate': out['m_b_gate'], 'm_lb_logits': out['m_lb_logits'], 'm_hg_norm_g': out['m_hg_norm_g'], 'm_q_a_g': out['m_q_a_g'], 'm_w_uq': out['m_w_uq'], 'm_kv_a_g': out['m_kv_a_g'], 'm_w_ukv': out['m_w_ukv'], 'm_w_proj_a': out['m_w_proj_a'], 'm_w_proj_b': out['m_w_proj_b'], 'm_w_out': out['m_w_out'], 'm_final_norm_g': out['m_final_norm_g'], 'v_norm_g': out['v_norm_g'], 'v_w_in': out['v_w_in'], 'v_b_gate': out['v_b_gate'], 'v_lb_logits': out['v_lb_logits'], 'v_hg_norm_g': out['v_hg_norm_g'], 'v_q_a_g': out['v_q_a_g'], 'v_w_uq': out['v_w_uq'], 'v_kv_a_g': out['v_kv_a_g'], 'v_w_ukv': out['v_w_ukv'], 'v_w_proj_a': out['v_w_proj_a'], 'v_w_proj_b': out['v_w_proj_b'], 'v_w_out': out['v_w_out'], 'v_final_norm_g': out['v_final_norm_g']}


def _loss(weights, diff, rest, loss_target):
    with _jax.named_scope("forward"):
        args = {**rest, TWIN_DIFF_INPUT: diff, **{k: w.astype(_WEIGHT_DTYPES[k]) for k, w in weights.items()}}
        y = _forward(args)
    with _jax.named_scope("loss_head"):
        err = _jnp.square(y.astype(_jnp.float32) - loss_target)
        return 0.5 * _jnp.sum(_jnp.mean(err, axis=-1)) if err.ndim else 0.5 * err


def _adamw(w, g, m, v):
    m = ADAM_B1 * m + (1.0 - ADAM_B1) * g
    v = ADAM_B2 * v + (1.0 - ADAM_B2) * _jnp.square(g)
    m_hat = m / (1.0 - ADAM_B1 ** ADAM_STEP)
    v_hat = v / (1.0 - ADAM_B2 ** ADAM_STEP)
    delta = -ADAM_LR * (m_hat / (_jnp.sqrt(v_hat) + ADAM_EPS) + ADAM_WD * w)
    return delta, m, v


def reference(x, norm_g, w_in, b_gate, lb_logits, hg_norm_g, q_a_g, w_uq, kv_a_g, w_ukv, w_proj_a, w_proj_b, w_out, final_norm_g, loss_target, m_norm_g, m_w_in, m_b_gate, m_lb_logits, m_hg_norm_g, m_q_a_g, m_w_uq, m_kv_a_g, m_w_ukv, m_w_proj_a, m_w_proj_b, m_w_out, m_final_norm_g, v_norm_g, v_w_in, v_b_gate, v_lb_logits, v_hg_norm_g, v_q_a_g, v_w_uq, v_kv_a_g, v_w_ukv, v_w_proj_a, v_w_proj_b, v_w_out, v_final_norm_g):
    given = dict(x=x, norm_g=norm_g, w_in=w_in, b_gate=b_gate, lb_logits=lb_logits, hg_norm_g=hg_norm_g, q_a_g=q_a_g, w_uq=w_uq, kv_a_g=kv_a_g, w_ukv=w_ukv, w_proj_a=w_proj_a, w_proj_b=w_proj_b, w_out=w_out, final_norm_g=final_norm_g, loss_target=loss_target, m_norm_g=m_norm_g, m_w_in=m_w_in, m_b_gate=m_b_gate, m_lb_logits=m_lb_logits, m_hg_norm_g=m_hg_norm_g, m_q_a_g=m_q_a_g, m_w_uq=m_w_uq, m_kv_a_g=m_kv_a_g, m_w_ukv=m_w_ukv, m_w_proj_a=m_w_proj_a, m_w_proj_b=m_w_proj_b, m_w_out=m_w_out, m_final_norm_g=m_final_norm_g, v_norm_g=v_norm_g, v_w_in=v_w_in, v_b_gate=v_b_gate, v_lb_logits=v_lb_logits, v_hg_norm_g=v_hg_norm_g, v_q_a_g=v_q_a_g, v_w_uq=v_w_uq, v_kv_a_g=v_kv_a_g, v_w_ukv=v_w_ukv, v_w_proj_a=v_w_proj_a, v_w_proj_b=v_w_proj_b, v_w_out=v_w_out, v_final_norm_g=v_final_norm_g)
    weights = {n: given[n] for n in TWIN_WEIGHTS}
    shared = {n: given[n] for n in SHARED_INPUTS}
    per_example = {n: given[n] for n in ['x']}
    grad_fn = _jax.value_and_grad(_loss, argnums=(0, 1))

    def one_microbatch(ex, loss_target):
        ex = dict(ex)
        diff = ex.pop(TWIN_DIFF_INPUT)
        return grad_fn(weights, diff, {**shared, **ex}, loss_target)

    if N_MICROBATCH == 1:
        loss, (grad_w, grad_x) = one_microbatch(per_example, given["loss_target"])
    else:
        def body(carry, xs):
            loss_sum, grad_sum = carry
            l_k, (gw_k, gx_k) = one_microbatch(xs[0], xs[1])
            with _jax.named_scope("update"):
                return (loss_sum + l_k, _jax.tree.map(_jnp.add, grad_sum, gw_k)), gx_k

        init = (_jnp.zeros((), _jnp.float32), _jax.tree.map(_jnp.zeros_like, weights))
        (loss, grad_w), grad_x = _jax.lax.scan(body, init, (per_example, given["loss_target"]))
    with _jax.named_scope("update"):
        delta_w, new_m, new_v = {}, {}, {}
        for n in TWIN_WEIGHTS:
            delta_w[n], new_m[n], new_v[n] = _adamw(weights[n], grad_w[n], given["m_" + n], given["v_" + n])
    return (loss, grad_x, *[grad_w[n] for n in TWIN_WEIGHTS], *[delta_w[n] for n in TWIN_WEIGHTS],
            *[new_m[n] for n in TWIN_WEIGHTS], *[new_v[n] for n in TWIN_WEIGHTS])
```

```python
import functools

import jax
import jax.numpy as jnp
from jax import lax
from jax.experimental import pallas as pl
from jax.experimental.pallas import tpu as pltpu

F32 = jnp.float32
BF16 = jnp.bfloat16

D_MODEL = 1024
HEADS = 8
HEAD_DIM = 128
HG_CHUNK = 32
QK_NOPE = 128
QK_ROPE = 64
QK_DIM = QK_NOPE + QK_ROPE
QK_PAD = 256
Q_LORA = 384
KV_LORA = 256
MS_COLS = 768
ROPE_THETA = 10000.0
EPS = 1e-6
ATT_SCALE = QK_DIM ** -0.5

ADAM_LR = 0.001
ADAM_B1 = 0.9
ADAM_B2 = 0.999
ADAM_EPS = 1e-08
ADAM_WD = 0.01
ADAM_STEP = 10

N_CHIPS = 4
N_DEV = 8
W_IN_COLS = 7872
W_IN_BLK = W_IN_COLS // N_CHIPS
PACK_ROWS = 1968 + 144 + 128 + 3 * 256
HALF_ROWS = PACK_ROWS // 2
SMALL_COLS = 7168

TM_MM = 512
TM_FUSED = 256
HG_ROWS = 256
TQ = 512
VMEM_LIMIT = 56 * 1024 * 1024


def _dot(a, b):
    return lax.dot_general(a, b, (((1,), (0,)), ((), ())), preferred_element_type=F32)


def _dot_nt(a, b):
    return lax.dot_general(a, b, (((1,), (1,)), ((), ())), preferred_element_type=F32)


def _dot_tn(a, b):
    return lax.dot_general(a, b, (((0,), (0,)), ((), ())), preferred_element_type=F32)


def _params(n_axes):
    return pltpu.CompilerParams(dimension_semantics=("arbitrary",) * n_axes, vmem_limit_bytes=VMEM_LIMIT)


def _rms(x, g):
    r = lax.rsqrt(jnp.mean(x * x, axis=-1, keepdims=True) + EPS)
    return x * r * g


def _rms_bwd(x, g, dy):
    r = lax.rsqrt(jnp.mean(x * x, axis=-1, keepdims=True) + EPS)
    xh = x * r
    dyg = dy * g
    dx = r * (dyg - xh * jnp.mean(dyg * xh, axis=-1, keepdims=True))
    return dx, dy * xh


def _silu_parts(z):
    s = jax.nn.sigmoid(z)
    return z * s, s * (1.0 + z * (1.0 - s))


def _rope(x, c, sa, sb):
    return x * c + pltpu.roll(x, 32, 1) * sa + pltpu.roll(x, 96, 1) * sb


def _rope_bwd(dy, c, sa, sb):
    return dy * c + pltpu.roll(dy * sa, 96, 1) + pltpu.roll(dy * sb, 32, 1)


def _rope_tables(seq):
    inv = ROPE_THETA ** (-jnp.arange(0, QK_ROPE, 2, dtype=F32) / QK_ROPE)
    ang = jnp.arange(seq, dtype=F32)[:, None] * inv[None, :]
    cos, sin = jnp.cos(ang), jnp.sin(ang)
    z32 = jnp.zeros_like(cos)
    z64 = jnp.zeros((seq, 64), F32)
    c = jnp.concatenate([cos, cos, z64], axis=1)
    sa = jnp.concatenate([z32, sin, z64], axis=1)
    sb = jnp.concatenate([-sin, z32, z64], axis=1)
    return c, sa, sb


def _mm(a, b, *, name, trans_b=False, add=None, out_dtype=F32, tm=TM_MM, tn=1024, tk=1024):
    m, k = a.shape
    n = b.shape[0] if trans_b else b.shape[1]
    tm, tn, tk = min(tm, m), min(tn, n), min(tk, k)
    assert m % tm == 0 and n % tn == 0 and k % tk == 0
    nk = k // tk
    has_add = add is not None

    def body(*refs):
        if has_add:
            a_ref, b_ref, add_ref, o_ref, acc_ref = refs
        else:
            a_ref, b_ref, o_ref, acc_ref = refs
        kk = pl.program_id(2)

        @pl.when(kk == 0)
        def _():
            acc_ref[...] = add_ref[...] if has_add else jnp.zeros_like(acc_ref)

        if trans_b:
            acc_ref[...] += _dot_nt(a_ref[...], b_ref[...])
        else:
            acc_ref[...] += _dot(a_ref[...], b_ref[...])

        @pl.when(kk == nk - 1)
        def _():
            o_ref[...] = acc_ref[...].astype(out_dtype)

    in_specs = [pl.BlockSpec((tm, tk), lambda i, j, kk: (i, kk))]
    if trans_b:
        in_specs.append(pl.BlockSpec((tn, tk), lambda i, j, kk: (j, kk)))
    else:
        in_specs.append(pl.BlockSpec((tk, tn), lambda i, j, kk: (kk, j)))
    args = [a, b]
    if has_add:
        in_specs.append(pl.BlockSpec((tm, tn), lambda i, j, kk: (i, j)))
        args.append(add)
    return pl.pallas_call(
        body, name=name, grid=(m // tm, n // tn, nk),
        in_specs=in_specs, out_specs=pl.BlockSpec((tm, tn), lambda i, j, kk: (i, j)),
        out_shape=jax.ShapeDtypeStruct((m, n), out_dtype),
        scratch_shapes=[pltpu.VMEM((tm, tn), F32)], compiler_params=_params(3),
    )(*args)


def _mm_tn(a, b, *, name, tm=TM_MM, tn=1024):
    m, k = a.shape
    n = b.shape[1]
    tm, tn = min(tm, m), min(tn, n)
    assert m % tm == 0 and n % tn == 0

    def body(a_ref, b_ref, o_ref):
        @pl.when(pl.program_id(1) == 0)
        def _():
            o_ref[...] = jnp.zeros_like(o_ref)

        o_ref[...] += _dot_tn(a_ref[...], b_ref[...])

    return pl.pallas_call(
        body, name=name, grid=(n // tn, m // tm),
        in_specs=[pl.BlockSpec((tm, k), lambda j, i: (i, 0)), pl.BlockSpec((tm, tn), lambda j, i: (i, j))],
        out_specs=pl.BlockSpec((k, tn), lambda j, i: (0, j)),
        out_shape=jax.ShapeDtypeStruct((k, n), F32), compiler_params=_params(2),
    )(a, b)


def _norm_in(x, g):
    s = x.shape[0]
    tm = min(TM_MM, s)

    def body(x_ref, g_ref, h_ref):
        h_ref[...] = _rms(x_ref[...], g_ref[...]).astype(BF16)

    return pl.pallas_call(
        body, name="norm_in", grid=(s // tm,),
        in_specs=[pl.BlockSpec((tm, D_MODEL), lambda i: (i, 0)), pl.BlockSpec((1, D_MODEL), lambda i: (0, 0))],
        out_specs=pl.BlockSpec((tm, D_MODEL), lambda i: (i, 0)),
        out_shape=jax.ShapeDtypeStruct((s, D_MODEL), BF16), compiler_params=_params(1),
    )(x, g)


def _norm_in_bwd(x, g, dh, dx2):
    s = x.shape[0]
    tm = min(TM_MM, s)

    def body(x_ref, g_ref, dh_ref, dx2_ref, dx_ref, dg_ref):
        @pl.when(pl.program_id(0) == 0)
        def _():
            dg_ref[...] = jnp.zeros_like(dg_ref)

        dx, dg_rows = _rms_bwd(x_ref[...], g_ref[...], dh_ref[...])
        dx_ref[...] = dx + dx2_ref[...]
        dg_ref[...] += jnp.sum(dg_rows, axis=0, keepdims=True)

    row = pl.BlockSpec((tm, D_MODEL), lambda i: (i, 0))
    vec = pl.BlockSpec((1, D_MODEL), lambda i: (0, 0))
    return pl.pallas_call(
        body, name="norm_in_bwd", grid=(s // tm,),
        in_specs=[row, vec, row, row], out_specs=[row, vec],
        out_shape=[jax.ShapeDtypeStruct((s, D_MODEL), F32), jax.ShapeDtypeStruct((1, D_MODEL), F32)],
        compiler_params=_params(1),
    )(x, g, dh, dx2)


def _chunk_rows(rows):
    return lax.broadcasted_iota(jnp.int32, (rows, HEAD_DIM), 0) & (HG_CHUNK - 1)


def _chunk_cumsum(x, rows):
    pos = _chunk_rows(rows)
    shift = 1
    while shift < HG_CHUNK:
        x = x + jnp.where(pos >= shift, pltpu.roll(x, shift, 0), 0.0)
        shift *= 2
    return x


def _chunk_revcumsum(x, rows):
    pos = _chunk_rows(rows)
    shift = 1
    while shift < HG_CHUNK:
        x = x + jnp.where(pos + shift < HG_CHUNK, pltpu.roll(x, rows - shift, 0), 0.0)
        shift *= 2
    return x


def _lower_bound(lbl):
    mx = jnp.maximum(lbl[0:1, :], lbl[1:2, :])
    e0 = jnp.exp(lbl[0:1, :] - mx)
    e1 = jnp.exp(lbl[1:2, :] - mx)
    p0 = e0 / (e0 + e1)
    return p0, p0 * (e1 / (e0 + e1))


def _tril():
    r = lax.broadcasted_iota(jnp.int32, (HG_CHUNK, HG_CHUNK), 0)
    c = lax.broadcasted_iota(jnp.int32, (HG_CHUNK, HG_CHUNK), 1)
    return r >= c


def _hgrn_fwd(hg, lb_logits, norm_g):
    s = hg.shape[0]
    rows = min(HG_ROWS, s)
    nblk = s // rows
    nch = rows // HG_CHUNK

    def body(hg_ref, lbl_ref, g_ref, o_ref, ya_ref, st0_ref, st_s):
        @pl.when(pl.program_id(1) == 0)
        def _():
            st_s[...] = jnp.zeros_like(st_s)

        hq = hg_ref[:, 0:128]
        hf = hg_ref[:, 128:256]
        hi = hg_ref[:, 256:384]
        hz = hg_ref[:, 384:512]
        lb, _ = _lower_bound(lbl_ref[...])
        f = lb + (1.0 - lb) * jax.nn.sigmoid(hf)
        q = hq * jax.nn.sigmoid(hq)
        k = 1.0 - f
        b = _chunk_cumsum(jnp.log(f), rows)
        q_in = (q * jnp.exp(b)).astype(BF16)
        k_in = (k * jnp.exp(-b)).astype(BF16)
        vb = hi.astype(BF16)
        tril = _tril()
        st = st_s[...]
        st0_ref[...] = st
        for c in range(nch):
            sl = slice(c * HG_CHUNK, (c + 1) * HG_CHUNK)
            b_c = b[sl]
            bl = b_c[HG_CHUNK - 1:HG_CHUNK, :]
            k_out = (k[sl] * jnp.exp(bl - b_c)).astype(BF16)
            sc = jnp.where(tril, _dot_nt(q_in[sl], k_in[sl]), 0.0)
            o_ref[sl, :] = _dot(sc.astype(BF16), vb[sl]) + _dot_nt(q_in[sl], st.astype(BF16))
            st = st * jnp.exp(bl) + _dot_tn(vb[sl], k_out)
        st_s[...] = st
        o = o_ref[...]
        silu_z, _ = _silu_parts(hz)
        ya_ref[...] = (_rms(o, g_ref[...]) * silu_z).astype(BF16)

    return pl.pallas_call(
        body, name="hgrn_fwd", grid=(HEADS, nblk),
        in_specs=[pl.BlockSpec((rows, 512), lambda h, i: (i, h)),
                  pl.BlockSpec((2, HEAD_DIM), lambda h, i: (0, h)),
                  pl.BlockSpec((1, HEAD_DIM), lambda h, i: (0, 0))],
        out_specs=[pl.BlockSpec((rows, HEAD_DIM), lambda h, i: (i, h)),
                   pl.BlockSpec((rows, HEAD_DIM), lambda h, i: (i, h)),
                   pl.BlockSpec((None, None, HEAD_DIM, HEAD_DIM), lambda h, i: (h, i, 0, 0))],
        out_shape=[jax.ShapeDtypeStruct((s, D_MODEL), F32), jax.ShapeDtypeStruct((s, D_MODEL), BF16),
                   jax.ShapeDtypeStruct((HEADS, nblk, HEAD_DIM, HEAD_DIM), F32)],
        scratch_shapes=[pltpu.VMEM((HEAD_DIM, HEAD_DIM), F32)], compiler_params=_params(2),
    )(hg, lb_logits, norm_g)


def _hgrn_bwd(hg, o_pre, dya, st0, lb_logits, norm_g):
    s = hg.shape[0]
    rows = min(HG_ROWS, s)
    nblk = s // rows
    nch = rows // HG_CHUNK

    def body(hg_ref, o_ref, dya_ref, st0_ref, lbl_ref, g_ref, dhg_ref, dlb_ref, dg_ref,
             dst_s, stp_s, ebl_s, dqin_s, dkin_s, dkout_s, dv_s, dbl_s):
        @pl.when(pl.program_id(1) == 0)
        def _():
            dst_s[...] = jnp.zeros_like(dst_s)
            dlb_ref[...] = jnp.zeros_like(dlb_ref)
            dg_ref[...] = jnp.zeros_like(dg_ref)

        hq = hg_ref[:, 0:128]
        hf = hg_ref[:, 128:256]
        hi = hg_ref[:, 256:384]
        hz = hg_ref[:, 384:512]
        lb, _ = _lower_bound(lbl_ref[...])
        sg = jax.nn.sigmoid(hf)
        f = lb + (1.0 - lb) * sg
        q, dsilu_q = _silu_parts(hq)
        k = 1.0 - f
        b = _chunk_cumsum(jnp.log(f), rows)
        eb = jnp.exp(b)
        enb = jnp.exp(-b)
        q_in32 = q * eb
        k_in32 = k * enb
        q_in = q_in32.astype(BF16)
        k_in = k_in32.astype(BF16)
        vb = hi.astype(BF16)
        tril = _tril()

        st = st0_ref[...]
        decs = []
        for c in range(nch):
            sl = slice(c * HG_CHUNK, (c + 1) * HG_CHUNK)
            b_c = b[sl]
            bl = b_c[HG_CHUNK - 1:HG_CHUNK, :]
            ebl = jnp.exp(bl - b_c)
            ebl_s[sl, :] = ebl
            decs.append(jnp.exp(bl))
            stp_s[c] = st
            st = st * decs[c] + _dot_tn(vb[sl], (k[sl] * ebl).astype(BF16))

        g = g_ref[...]
        o = o_ref[...]
        rstd = lax.rsqrt(jnp.mean(o * o, axis=-1, keepdims=True) + EPS)
        oh = o * rstd
        silu_z, dsilu_z = _silu_parts(hz)
        dya_v = dya_ref[...]
        dn = dya_v * silu_z
        dhz = dya_v * (oh * g) * dsilu_z
        dg_ref[...] += jnp.sum(dn * oh, axis=0, keepdims=True)
        doh = dn * g
        do = (rstd * (doh - oh * jnp.mean(doh * oh, axis=-1, keepdims=True))).astype(BF16)

        k_out32 = k * ebl_s[...]
        k_out = k_out32.astype(BF16)
        dst = dst_s[...]
        for c in reversed(range(nch)):
            sl = slice(c * HG_CHUNK, (c + 1) * HG_CHUNK)
            stp = stp_s[c]
            dstb = dst.astype(BF16)
            sc = jnp.where(tril, _dot_nt(q_in[sl], k_in[sl]), 0.0)
            dkout = _dot(vb[sl], dstb)
            dv_s[sl, :] = _dot_nt(k_out[sl], dstb) + _dot_tn(sc.astype(BF16), do[sl])
            dsc = jnp.where(tril, _dot_nt(do[sl], vb[sl]), 0.0).astype(BF16)
            dqin_s[sl, :] = _dot(dsc, k_in[sl]) + _dot(do[sl], stp.astype(BF16))
            dkin_s[sl, :] = _dot_tn(dsc, q_in[sl])
            dkout_s[sl, :] = dkout
            ddec = jnp.sum(dst * stp, axis=0, keepdims=True)
            dbl = jnp.sum(dkout * k_out32[sl], axis=0, keepdims=True) + ddec * decs[c]
            dbl_s[sl, :] = jnp.broadcast_to(dbl, (HG_CHUNK, HEAD_DIM))
            dst = dst * decs[c] + _dot_tn(do[sl], q_in[sl])
        dst_s[...] = dst

        dqin = dqin_s[...]
        dkin = dkin_s[...]
        dkout = dkout_s[...]
        dq = dqin * eb
        dk = dkin * enb + dkout * ebl_s[...]
        db = dqin * q_in32 - dkin * k_in32 - dkout * k_out32
        dlogf = _chunk_revcumsum(db, rows) + dbl_s[...]
        df = dlogf / f - dk
        dlb_ref[...] += jnp.sum(df * (1.0 - sg), axis=0, keepdims=True)
        dhg_ref[:, 0:128] = (dq * dsilu_q).astype(BF16)
        dhg_ref[:, 128:256] = (df * (1.0 - lb) * sg * (1.0 - sg)).astype(BF16)
        dhg_ref[:, 256:384] = dv_s[...].astype(BF16)
        dhg_ref[:, 384:512] = dhz.astype(BF16)

    last = nblk - 1
    tile = pltpu.VMEM((rows, HEAD_DIM), F32)
    return pl.pallas_call(
        body, name="hgrn_bwd", grid=(HEADS, nblk),
        in_specs=[pl.BlockSpec((rows, 512), lambda h, i: (last - i, h)),
                  pl.BlockSpec((rows, HEAD_DIM), lambda h, i: (last - i, h)),
                  pl.BlockSpec((rows, HEAD_DIM), lambda h, i: (last - i, h)),
                  pl.BlockSpec((None, None, HEAD_DIM, HEAD_DIM), lambda h, i: (h, last - i, 0, 0)),
                  pl.BlockSpec((2, HEAD_DIM), lambda h, i: (0, h)),
                  pl.BlockSpec((1, HEAD_DIM), lambda h, i: (0, 0))],
        out_specs=[pl.BlockSpec((rows, 512), lambda h, i: (last - i, h)),
                   pl.BlockSpec((1, HEAD_DIM), lambda h, i: (0, h)),
                   pl.BlockSpec((None, 1, HEAD_DIM), lambda h, i: (h, 0, 0))],
        out_shape=[jax.ShapeDtypeStruct((s, 4 * D_MODEL), BF16), jax.ShapeDtypeStruct((1, D_MODEL), F32),
                   jax.ShapeDtypeStruct((HEADS, 1, HEAD_DIM), F32)],
        scratch_shapes=[pltpu.VMEM((HEAD_DIM, HEAD_DIM), F32), pltpu.VMEM((nch, HEAD_DIM, HEAD_DIM), F32),
                        tile, tile, tile, tile, tile, tile],
        compiler_params=_params(2),
    )(hg, o_pre, dya, st0, lb_logits, norm_g)


def _mla_pre(ms, q_a_g, kv_a_g, wuq3, wukv3, tabs):
    s = ms.shape[0]
    tm = min(TM_FUSED, s)

    def body(ms_ref, qg_ref, kvg_ref, wuq_ref, wukv_ref, c_ref, sa_ref, sb_ref,
             q_ref, k_ref, v_ref, cqn_ref, ckvn_ref):
        c, sa, sb = c_ref[...], sa_ref[...], sb_ref[...]
        cqn = _rms(ms_ref[:, 0:Q_LORA], qg_ref[...]).astype(BF16)
        ckvn = _rms(ms_ref[:, Q_LORA:Q_LORA + KV_LORA], kvg_ref[...]).astype(BF16)
        cqn_ref[...] = cqn
        ckvn_ref[...] = ckvn
        k_pe = _rope(ms_ref[:, Q_LORA + KV_LORA:MS_COLS], c, sa, sb).astype(BF16)
        for h in range(HEADS):
            qh = _dot(cqn, wuq_ref[h])
            q_ref[h, :, 0:128] = qh[:, 0:128].astype(BF16)
            q_ref[h, :, 128:256] = _rope(qh[:, 128:256], c, sa, sb).astype(BF16)
            kvh = _dot(ckvn, wukv_ref[h])
            k_ref[h, :, 0:128] = kvh[:, 0:128].astype(BF16)
            k_ref[h, :, 128:256] = k_pe
            v_ref[h] = kvh[:, 128:256].astype(BF16)

    tab = pl.BlockSpec((tm, 128), lambda i: (i, 0))
    return pl.pallas_call(
        body, name="mla_pre", grid=(s // tm,),
        in_specs=[pl.BlockSpec((tm, MS_COLS), lambda i: (i, 0)),
                  pl.BlockSpec((1, Q_LORA), lambda i: (0, 0)), pl.BlockSpec((1, KV_LORA), lambda i: (0, 0)),
                  pl.BlockSpec((HEADS, Q_LORA, QK_PAD), lambda i: (0, 0, 0)),
                  pl.BlockSpec((HEADS, KV_LORA, 256), lambda i: (0, 0, 0)), tab, tab, tab],
        out_specs=[pl.BlockSpec((HEADS, tm, QK_PAD), lambda i: (0, i, 0)),
                   pl.BlockSpec((HEADS, tm, QK_PAD), lambda i: (0, i, 0)),
                   pl.BlockSpec((HEADS, tm, HEAD_DIM), lambda i: (0, i, 0)),
                   pl.BlockSpec((tm, Q_LORA), lambda i: (i, 0)), pl.BlockSpec((tm, KV_LORA), lambda i: (i, 0))],
        out_shape=[jax.ShapeDtypeStruct((HEADS, s, QK_PAD), BF16), jax.ShapeDtypeStruct((HEADS, s, QK_PAD), BF16),
                   jax.ShapeDtypeStruct((HEADS, s, HEAD_DIM), BF16),
                   jax.ShapeDtypeStruct((s, Q_LORA), BF16), jax.ShapeDtypeStruct((s, KV_LORA), BF16)],
        compiler_params=_params(1),
    )(ms, q_a_g, kv_a_g, wuq3, wukv3, *tabs)


def _causal_mask(t):
    r = lax.broadcasted_iota(jnp.int32, (t, t), 0)
    c = lax.broadcasted_iota(jnp.int32, (t, t), 1)
    return r >= c


def _flash_fwd(q, k, v, mz):
    s = q.shape[1]
    t = min(TQ, s)

    def body(q_ref, k_ref, v_ref, mz_ref, o_ref, yb_ref, lse_ref, m_s, l_s, acc_s):
        i = pl.program_id(1)
        qv = q_ref[...]
        m_s[...] = jnp.full_like(m_s, -jnp.inf)
        l_s[...] = jnp.zeros_like(l_s)
        acc_s[...] = jnp.zeros_like(acc_s)

        def step(j, masked):
            rows = pl.ds(pl.multiple_of(j * t, t), t)
            sc = _dot_nt(qv, k_ref[rows, :]) * ATT_SCALE
            if masked:
                sc = jnp.where(_causal_mask(t), sc, -jnp.inf)
            m_prev = m_s[...]
            m_new = jnp.maximum(m_prev, jnp.max(sc, axis=-1, keepdims=True))
            alpha = jnp.exp(m_prev - m_new)
            p = jnp.exp(sc - m_new)
            l_s[...] = alpha * l_s[...] + jnp.sum(p, axis=-1, keepdims=True)
            acc_s[...] = alpha * acc_s[...] + _dot(p.astype(BF16), v_ref[rows, :])
            m_s[...] = m_new

        def loop_body(j, carry):
            step(j, False)
            return carry

        lax.fori_loop(0, i, loop_body, 0)
        step(i, True)
        out = acc_s[...] / l_s[...]
        o_ref[...] = out
        silu_z, _ = _silu_parts(mz_ref[...])
        yb_ref[...] = (out * silu_z).astype(BF16)
        lse_ref[...] = jnp.broadcast_to(m_s[...] + jnp.log(l_s[...]), (t, 128))

    col = pl.BlockSpec((t, HEAD_DIM), lambda h, i: (i, h))
    return pl.pallas_call(
        body, name="flash_fwd", grid=(HEADS, s // t),
        in_specs=[pl.BlockSpec((None, t, QK_PAD), lambda h, i: (h, i, 0)),
                  pl.BlockSpec((None, s, QK_PAD), lambda h, i: (h, 0, 0)),
                  pl.BlockSpec((None, s, HEAD_DIM), lambda h, i: (h, 0, 0)), col],
        out_specs=[col, col, pl.BlockSpec((None, t, 128), lambda h, i: (h, i, 0))],
        out_shape=[jax.ShapeDtypeStruct((s, D_MODEL), F32), jax.ShapeDtypeStruct((s, D_MODEL), BF16),
                   jax.ShapeDtypeStruct((HEADS, s, 128), F32)],
        scratch_shapes=[pltpu.VMEM((t, 1), F32), pltpu.VMEM((t, 1), F32), pltpu.VMEM((t, HEAD_DIM), F32)],
        compiler_params=_params(2),
    )(q, k, v, mz)


def _flash_bwd(q, k, v, dyb, mz, o_att, lse, tabs):
    s = q.shape[1]
    t = min(TQ, s)

    def body(q_ref, k_ref, v_ref, dyb_ref, mz_ref, o_ref, lse_ref, c_ref, sa_ref, sb_ref,
             dq_ref, dk_ref, dv_ref, dmz_ref, dq_s):
        i = pl.program_id(1)

        @pl.when(i == 0)
        def _():
            dk_ref[...] = jnp.zeros_like(dk_ref)
            dv_ref[...] = jnp.zeros_like(dv_ref)

        silu_z, dsilu_z = _silu_parts(mz_ref[...])
        dyb_v = dyb_ref[...]
        out = o_ref[...]
        do32 = dyb_v * silu_z
        dmz_ref[...] = (dyb_v * out * dsilu_z).astype(BF16)
        delta = jnp.sum(do32 * out, axis=-1, keepdims=True)
        do = do32.astype(BF16)
        qv = q_ref[...]
        lse_v = lse_ref[:, 0:1]
        dq_s[...] = jnp.zeros_like(dq_s)

        def step(j, masked):
            rows = pl.ds(pl.multiple_of(j * t, t), t)
            kj = k_ref[rows, :]
            vj = v_ref[rows, :]
            sc = _dot_nt(qv, kj) * ATT_SCALE
            if masked:
                sc = jnp.where(_causal_mask(t), sc, -jnp.inf)
            p = jnp.exp(sc - lse_v)
            dp = _dot_nt(do, vj)
            ds = (p * (dp - delta) * ATT_SCALE).astype(BF16)
            dv_ref[rows, :] += _dot_tn(p.astype(BF16), do)
            dk_ref[rows, :] += _dot_tn(ds, qv)
            dq_s[...] += _dot(ds, kj)

        def loop_body(j, carry):
            step(j, False)
            return carry

        lax.fori_loop(0, i, loop_body, 0)
        step(i, True)
        dq = dq_s[...]
        dq_ref[:, 0:128] = dq[:, 0:128].astype(BF16)
        dq_ref[:, 128:256] = _rope_bwd(dq[:, 128:256], c_ref[...], sa_ref[...], sb_ref[...]).astype(BF16)

    col = pl.BlockSpec((t, HEAD_DIM), lambda h, i: (i, h))
    tab = pl.BlockSpec((t, 128), lambda h, i: (i, 0))
    return pl.pallas_call(
        body, name="flash_bwd", grid=(HEADS, s // t),
        in_specs=[pl.BlockSpec((None, t, QK_PAD), lambda h, i: (h, i, 0)),
                  pl.BlockSpec((None, s, QK_PAD), lambda h, i: (h, 0, 0)),
                  pl.BlockSpec((None, s, HEAD_DIM), lambda h, i: (h, 0, 0)),
                  col, col, col, pl.BlockSpec((None, t, 128), lambda h, i: (h, i, 0)), tab, tab, tab],
        out_specs=[pl.BlockSpec((None, t, QK_PAD), lambda h, i: (h, i, 0)),
                   pl.BlockSpec((None, s, QK_PAD), lambda h, i: (h, 0, 0)),
                   pl.BlockSpec((None, s, HEAD_DIM), lambda h, i: (h, 0, 0)), col],
        out_shape=[jax.ShapeDtypeStruct((HEADS, s, QK_PAD), BF16), jax.ShapeDtypeStruct((HEADS, s, QK_PAD), F32),
                   jax.ShapeDtypeStruct((HEADS, s, HEAD_DIM), F32), jax.ShapeDtypeStruct((s, D_MODEL), BF16)],
        scratch_shapes=[pltpu.VMEM((t, QK_PAD), F32)], compiler_params=_params(2),
    )(q, k, v, dyb, mz, o_att, lse, *tabs)


def _mla_bwd_proj(dq, dk, dv, cqn, ckvn, ms, q_a_g, kv_a_g, wuq3, wukv3, tabs):
    s = ms.shape[0]
    tm = min(TM_FUSED, s)

    def body(dq_ref, dk_ref, dv_ref, cqn_ref, ckvn_ref, ms_ref, qg_ref, kvg_ref, wuq_ref, wukv_ref,
             c_ref, sa_ref, sb_ref, dms_ref, dwuq_ref, dwukv_ref, dqg_ref, dkvg_ref):
        @pl.when(pl.program_id(0) == 0)
        def _():
            dwuq_ref[...] = jnp.zeros_like(dwuq_ref)
            dwukv_ref[...] = jnp.zeros_like(dwukv_ref)
            dqg_ref[...] = jnp.zeros_like(dqg_ref)
            dkvg_ref[...] = jnp.zeros_like(dkvg_ref)

        cqn = cqn_ref[...]
        ckvn = ckvn_ref[...]
        dcqn = jnp.zeros((tm, Q_LORA), F32)
        dckvn = jnp.zeros((tm, KV_LORA), F32)
        dkpe = jnp.zeros((tm, 128), F32)
        for h in range(HEADS):
            dqh = dq_ref[h]
            dcqn += _dot_nt(dqh, wuq_ref[h])
            dwuq_ref[h] += _dot_tn(cqn, dqh)
            dkh = dk_ref[h]
            dkvh = jnp.concatenate([dkh[:, 0:128], dv_ref[h]], axis=1).astype(BF16)
            dckvn += _dot_nt(dkvh, wukv_ref[h])
            dwukv_ref[h] += _dot_tn(ckvn, dkvh)
            dkpe += dkh[:, 128:256]
        dcq, dqg_rows = _rms_bwd(ms_ref[:, 0:Q_LORA], qg_ref[...], dcqn)
        dckv, dkvg_rows = _rms_bwd(ms_ref[:, Q_LORA:Q_LORA + KV_LORA], kvg_ref[...], dckvn)
        dqg_ref[...] += jnp.sum(dqg_rows, axis=0, keepdims=True)
        dkvg_ref[...] += jnp.sum(dkvg_rows, axis=0, keepdims=True)
        dms_ref[:, 0:Q_LORA] = dcq.astype(BF16)
        dms_ref[:, Q_LORA:Q_LORA + KV_LORA] = dckv.astype(BF16)
        dms_ref[:, Q_LORA + KV_LORA:MS_COLS] = _rope_bwd(dkpe, c_ref[...], sa_ref[...], sb_ref[...]).astype(BF16)

    tab = pl.BlockSpec((tm, 128), lambda i: (i, 0))
    wq = pl.BlockSpec((HEADS, Q_LORA, QK_PAD), lambda i: (0, 0, 0))
    wkv = pl.BlockSpec((HEADS, KV_LORA, 256), lambda i: (0, 0, 0))
    qg = pl.BlockSpec((1, Q_LORA), lambda i: (0, 0))
    kvg = pl.BlockSpec((1, KV_LORA), lambda i: (0, 0))
    return pl.pallas_call(
        body, name="mla_bwd_proj", grid=(s // tm,),
        in_specs=[pl.BlockSpec((HEADS, tm, QK_PAD), lambda i: (0, i, 0)),
                  pl.BlockSpec((HEADS, tm, QK_PAD), lambda i: (0, i, 0)),
                  pl.BlockSpec((HEADS, tm, HEAD_DIM), lambda i: (0, i, 0)),
                  pl.BlockSpec((tm, Q_LORA), lambda i: (i, 0)), pl.BlockSpec((tm, KV_LORA), lambda i: (i, 0)),
                  pl.BlockSpec((tm, MS_COLS), lambda i: (i, 0)), qg, kvg, wq, wkv, tab, tab, tab],
        out_specs=[pl.BlockSpec((tm, MS_COLS), lambda i: (i, 0)), wq, wkv, qg, kvg],
        out_shape=[jax.ShapeDtypeStruct((s, MS_COLS), BF16), jax.ShapeDtypeStruct((HEADS, Q_LORA, QK_PAD), F32),
                   jax.ShapeDtypeStruct((HEADS, KV_LORA, 256), F32),
                   jax.ShapeDtypeStruct((1, Q_LORA), F32), jax.ShapeDtypeStruct((1, KV_LORA), F32)],
        compiler_params=_params(1),
    )(dq, dk, dv, cqn, ckvn, ms, q_a_g, kv_a_g, wuq3, wukv3, *tabs)


def _merge_loss(ya, yb, glog, b_gate, x, tgt, fg, wpa, wpb, wout):
    s = x.shape[0]
    tm = min(TM_FUSED, s)

    def body(ya_ref, yb_ref, g0_ref, g1_ref, b0_ref, b1_ref, x_ref, t_ref, fg_ref, wpa_ref, wpb_ref, wout_ref,
             mg_ref, pa_ref, pb_ref, dx2_ref, loss_ref, dfg_ref):
        @pl.when(pl.program_id(0) == 0)
        def _():
            loss_ref[...] = jnp.zeros_like(loss_ref)
            dfg_ref[...] = jnp.zeros_like(dfg_ref)

        pa = _dot(ya_ref[...], wpa_ref[...])
        pb = _dot(yb_ref[...], wpb_ref[...])
        pa_ref[...] = pa
        pb_ref[...] = pb
        merged = (jax.nn.sigmoid(g0_ref[...] + b0_ref[...]) * pa
                  + jax.nn.sigmoid(g1_ref[...] + b1_ref[...]) * pb).astype(BF16)
        mg_ref[...] = merged
        x2 = x_ref[...] + _dot(merged, wout_ref[...])
        fg_v = fg_ref[...]
        err = _rms(x2, fg_v) - t_ref[...]
        loss_ref[...] += 0.5 * jnp.sum(jnp.mean(err * err, axis=-1, keepdims=True), axis=0, keepdims=True)
        dx2, dfg_rows = _rms_bwd(x2, fg_v, err * (1.0 / D_MODEL))
        dx2_ref[...] = dx2
        dfg_ref[...] += jnp.sum(dfg_rows, axis=0, keepdims=True)

    row = pl.BlockSpec((tm, D_MODEL), lambda i: (i, 0))
    row1 = pl.BlockSpec((tm, D_MODEL), lambda i: (i, 1))
    vec = pl.BlockSpec((1, D_MODEL), lambda i: (0, 0))
    vec1 = pl.BlockSpec((1, D_MODEL), lambda i: (0, 1))
    wsp = pl.BlockSpec((D_MODEL, D_MODEL), lambda i: (0, 0))
    return pl.pallas_call(
        body, name="merge_loss", grid=(s // tm,),
        in_specs=[row, row, row, row1, vec, vec1, row, row, vec, wsp, wsp, wsp],
        out_specs=[row, row, row, row, pl.BlockSpec((1, 128), lambda i: (0, 0)), vec],
        out_shape=[jax.ShapeDtypeStruct((s, D_MODEL), BF16), jax.ShapeDtypeStruct((s, D_MODEL), F32),
                   jax.ShapeDtypeStruct((s, D_MODEL), F32), jax.ShapeDtypeStruct((s, D_MODEL), F32),
                   jax.ShapeDtypeStruct((1, 128), F32), jax.ShapeDtypeStruct((1, D_MODEL), F32)],
        compiler_params=_params(1),
    )(ya, yb, glog, glog, b_gate, b_gate, x, tgt, fg, wpa, wpb, wout)


def _merge_bwd(dx2, pa, pb, glog, b_gate, wpa, wpb, wout):
    s = dx2.shape[0]
    tm = min(TM_FUSED, s)

    def body(dx2_ref, pa_ref, pb_ref, g0_ref, g1_ref, b0_ref, b1_ref, wpa_ref, wpb_ref, wout_ref,
             dya_ref, dyb_ref, dgl_ref, dpa_ref, dpb_ref, dx2b_ref, dbg_ref):
        @pl.when(pl.program_id(0) == 0)
        def _():
            dbg_ref[...] = jnp.zeros_like(dbg_ref)

        dx2b = dx2_ref[...].astype(BF16)
        dx2b_ref[...] = dx2b
        dmg = _dot_nt(dx2b, wout_ref[...])
        g0 = jax.nn.sigmoid(g0_ref[...] + b0_ref[...])
        g1 = jax.nn.sigmoid(g1_ref[...] + b1_ref[...])
        dpa = (dmg * g0).astype(BF16)
        dpb = (dmg * g1).astype(BF16)
        dpa_ref[...] = dpa
        dpb_ref[...] = dpb
        dgl0 = dmg * pa_ref[...] * g0 * (1.0 - g0)
        dgl1 = dmg * pb_ref[...] * g1 * (1.0 - g1)
        dgl_ref[:, 0:D_MODEL] = dgl0.astype(BF16)
        dgl_ref[:, D_MODEL:2 * D_MODEL] = dgl1.astype(BF16)
        dbg_ref[:, 0:D_MODEL] += jnp.sum(dgl0, axis=0, keepdims=True)
        dbg_ref[:, D_MODEL:2 * D_MODEL] += jnp.sum(dgl1, axis=0, keepdims=True)
        dya_ref[...] = _dot_nt(dpa, wpa_ref[...])
        dyb_ref[...] = _dot_nt(dpb, wpb_ref[...])

    row = pl.BlockSpec((tm, D_MODEL), lambda i: (i, 0))
    row1 = pl.BlockSpec((tm, D_MODEL), lambda i: (i, 1))
    row2 = pl.BlockSpec((tm, 2 * D_MODEL), lambda i: (i, 0))
    vec = pl.BlockSpec((1, D_MODEL), lambda i: (0, 0))
    vec1 = pl.BlockSpec((1, D_MODEL), lambda i: (0, 1))
    vec2 = pl.BlockSpec((1, 2 * D_MODEL), lambda i: (0, 0))
    wsp = pl.BlockSpec((D_MODEL, D_MODEL), lambda i: (0, 0))
    return pl.pallas_call(
        body, name="merge_bwd", grid=(s // tm,),
        in_specs=[row, row, row, row, row1, vec, vec1, wsp, wsp, wsp],
        out_specs=[row, row, row2, row, row, row, vec2],
        out_shape=[jax.ShapeDtypeStruct((s, D_MODEL), F32), jax.ShapeDtypeStruct((s, D_MODEL), F32),
                   jax.ShapeDtypeStruct((s, 2 * D_MODEL), BF16), jax.ShapeDtypeStruct((s, D_MODEL), BF16),
                   jax.ShapeDtypeStruct((s, D_MODEL), BF16), jax.ShapeDtypeStruct((s, D_MODEL), BF16),
                   jax.ShapeDtypeStruct((1, 2 * D_MODEL), F32)],
        compiler_params=_params(1),
    )(dx2, pa, pb, glog, glog, b_gate, b_gate, wpa, wpb, wout)


def _hg_perm(w):
    r = w.shape[0]
    return w.reshape(r, 4, HEADS, HEAD_DIM).transpose(0, 2, 1, 3).reshape(r, 4 * D_MODEL)


def _hg_unperm(w):
    r = w.shape[0]
    return w.reshape(r, HEADS, 4, HEAD_DIM).transpose(0, 2, 1, 3).reshape(r, 4 * D_MODEL)


def _local_step(x, tgt, w_in, w_uq, w_ukv, wpa, wpb, wout, norm_g, b_gate, lb_logits, hg_norm_g, q_a_g, kv_a_g, fg):
    s = x.shape[0]
    zpad = jnp.zeros((D_MODEL, 64), BF16)
    w_hg = _hg_perm(w_in[:, 0:4096])
    w_ms = jnp.concatenate([w_in[:, 4096:4800], zpad], axis=1)
    w_mz = w_in[:, 4800:5824]
    w_gl = w_in[:, 5824:7872]
    wuq3 = jnp.pad(w_uq.reshape(Q_LORA, HEADS, QK_DIM).transpose(1, 0, 2), ((0, 0), (0, 0), (0, QK_PAD - QK_DIM)))
    wukv3 = w_ukv.reshape(KV_LORA, HEADS, 256).transpose(1, 0, 2)
    tabs = _rope_tables(s)

    h = _norm_in(x, norm_g)
    hg = _mm(h, w_hg, name="proj_hg")
    ms = _mm(h, w_ms, name="proj_ms")
    mz = _mm(h, w_mz, name="proj_mz")
    glog = _mm(h, w_gl, name="proj_gate")
    o_pre, ya, st0 = _hgrn_fwd(hg, lb_logits, hg_norm_g)
    q, k, v, cqn, ckvn = _mla_pre(ms, q_a_g, kv_a_g, wuq3, wukv3, tabs)
    o_att, yb, lse = _flash_fwd(q, k, v, mz)
    merged, pa, pb, dx2, loss, dfg = _merge_loss(ya, yb, glog, b_gate, x, tgt, fg, wpa, wpb, wout)

    dya, dyb, dglog, dpa, dpb, dx2b, dbg = _merge_bwd(dx2, pa, pb, glog, b_gate, wpa, wpb, wout)
    d_wout = _mm_tn(merged, dx2b, name="dw_out")
    d_wpa = _mm_tn(ya, dpa, name="dw_proj_a")
    d_wpb = _mm_tn(yb, dpb, name="dw_proj_b")
    dhg, dlb, dhgg = _hgrn_bwd(hg, o_pre, dya, st0, lb_logits, hg_norm_g)
    dq, dk, dv, dmz = _flash_bwd(q, k, v, dyb, mz, o_att, lse, tabs)
    dms, d_wuq3, d_wukv3, dqg, dkvg = _mla_bwd_proj(dq, dk, dv, cqn, ckvn, ms, q_a_g, kv_a_g, wuq3, wukv3, tabs)
    d_w_hg = _mm_tn(h, dhg, name="dw_in_hg")
    d_w_ms = _mm_tn(h, dms, name="dw_in_ms")
    d_w_mz = _mm_tn(h, dmz, name="dw_in_mz")
    d_w_gl = _mm_tn(h, dglog, name="dw_in_gate")
    dh = _mm(dhg, w_hg, trans_b=True, name="dh_hg")
    dh = _mm(dms, w_ms, trans_b=True, add=dh, name="dh_ms")
    dh = _mm(dmz, w_mz, trans_b=True, add=dh, name="dh_mz")
    dh = _mm(dglog, w_gl, trans_b=True, add=dh, name="dh_gate")
    grad_x, dng = _norm_in_bwd(x, norm_g, dh, dx2)

    big = {
        "w_in": jnp.concatenate([_hg_unperm(d_w_hg), d_w_ms[:, 0:704], d_w_mz, d_w_gl], axis=1),
        "w_uq": d_wuq3.transpose(1, 0, 2)[:, :, 0:QK_DIM].reshape(Q_LORA, HEADS * QK_DIM),
        "w_ukv": d_wukv3.transpose(1, 0, 2).reshape(KV_LORA, HEADS * 256),
        "w_proj_a": d_wpa, "w_proj_b": d_wpb, "w_out": d_wout,
    }
    small = {"norm_g": dng, "b_gate": dbg, "lb": dlb, "hg_norm_g": dhgg, "q_a_g": dqg, "kv_a_g": dkvg,
             "final_norm_g": dfg}
    return loss, grad_x, big, small


def _pack_block(w_in_b, w_uq_b, w_ukv_b, wpa_b, wpb_b, wout_b):
    return jnp.concatenate([w_in_b.reshape(1968, D_MODEL), w_uq_b.reshape(144, D_MODEL),
                            w_ukv_b.reshape(128, D_MODEL), wpa_b, wpb_b, wout_b], axis=0)


def _unpack_block(p):
    return (p[0:1968].reshape(D_MODEL, W_IN_BLK), p[1968:2112].reshape(Q_LORA, 384),
            p[2112:2240].reshape(KV_LORA, 512), p[2240:2496], p[2496:2752], p[2752:3008])


def _pack_full(big):
    blocks = []
    for b in range(N_CHIPS):
        blocks.append(_pack_block(
            big["w_in"][:, b * W_IN_BLK:(b + 1) * W_IN_BLK], big["w_uq"][:, b * 384:(b + 1) * 384],
            big["w_ukv"][:, b * 512:(b + 1) * 512], big["w_proj_a"][b * 256:(b + 1) * 256],
            big["w_proj_b"][b * 256:(b + 1) * 256], big["w_out"][b * 256:(b + 1) * 256]))
    return jnp.stack(blocks, axis=0)


def _unpack_full(g):
    parts = [_unpack_block(g[b]) for b in range(N_CHIPS)]
    w_in, w_uq, w_ukv = (jnp.concatenate([p[n] for p in parts], axis=1) for n in range(3))
    wpa, wpb, wout = (jnp.concatenate([p[n] for p in parts], axis=0) for n in range(3, 6))
    return w_in, w_uq, w_ukv, wpa, wpb, wout


MESH_ID = pl.DeviceIdType.MESH
ANY = pl.BlockSpec(memory_space=pl.ANY)


def _me():
    return lax.axis_index("x"), lax.axis_index("y"), lax.axis_index("c")


def _other_chips(x, y):
    return [(1 - x, y), (x, 1 - y), (1 - x, 1 - y)]


def _gather_weights(block):
    def body(in_ref, out_ref, send_sems, recv_sems, local_sem):
        x, y, c = _me()
        chips = _other_chips(x, y)
        half = pl.ds(c * HALF_ROWS, HALF_ROWS)
        other_half = pl.ds((1 - c) * HALF_ROWS, HALF_ROWS)

        def copy(k, src, dst, to):
            return pltpu.make_async_remote_copy(src_ref=src, dst_ref=dst, send_sem=send_sems.at[k],
                                                recv_sem=recv_sems.at[k], device_id=to, device_id_type=MESH_ID)

        mine = pltpu.make_async_copy(in_ref, out_ref.at[2 * x + y], local_sem)
        mine.start()
        first = [copy(j, in_ref.at[half, :], out_ref.at[2 * x + y, half, :], (cx, cy, c))
                 for j, (cx, cy) in enumerate(chips)]
        for cp in first:
            cp.start()
        passed = []
        for j, (cx, cy) in enumerate(chips):
            landed = out_ref.at[2 * cx + cy, half, :]
            copy(j, landed, landed, (cx, cy, c)).wait_recv()
            fwd = copy(3 + j, landed, landed, (x, y, 1 - c))
            fwd.start()
            passed.append(fwd)
        for j, (cx, cy) in enumerate(chips):
            theirs = out_ref.at[2 * cx + cy, other_half, :]
            copy(3 + j, theirs, theirs, (x, y, 1 - c)).wait_recv()
        for cp in first + passed:
            cp.wait_send()
        mine.wait()

    return pl.pallas_call(
        body, name="gather_weights", in_specs=[ANY], out_specs=ANY,
        out_shape=jax.ShapeDtypeStruct((N_CHIPS, PACK_ROWS, D_MODEL), block.dtype),
        scratch_shapes=[pltpu.SemaphoreType.DMA((6,)), pltpu.SemaphoreType.DMA((6,)), pltpu.SemaphoreType.DMA],
    )(block)


def _swap_halves(g):
    def body(g_ref, out_ref, send_sem, recv_sem):
        x, y, c = _me()
        cp = pltpu.make_async_remote_copy(
            src_ref=g_ref.at[:, pl.ds((1 - c) * HALF_ROWS, HALF_ROWS), :], dst_ref=out_ref,
            send_sem=send_sem, recv_sem=recv_sem, device_id=(x, y, 1 - c), device_id_type=MESH_ID)
        cp.start()
        cp.wait()

    return pl.pallas_call(
        body, name="grad_swap_halves", in_specs=[ANY], out_specs=ANY,
        out_shape=jax.ShapeDtypeStruct((N_CHIPS, HALF_ROWS, D_MODEL), g.dtype),
        scratch_shapes=[pltpu.SemaphoreType.DMA, pltpu.SemaphoreType.DMA],
    )(g)


def _scatter_blocks(hsum):
    def body(h_ref, out_ref, send_sems, recv_sems):
        x, y, c = _me()
        chips = _other_chips(x, y)
        cps = []
        for j, (cx, cy) in enumerate(chips):
            cps.append(pltpu.make_async_remote_copy(
                src_ref=h_ref.at[2 * cx + cy], dst_ref=out_ref.at[j], send_sem=send_sems.at[j],
                recv_sem=recv_sems.at[j], device_id=(cx, cy, c), device_id_type=MESH_ID))
        for cp in cps:
            cp.start()
        for cp in cps:
            cp.wait()

    return pl.pallas_call(
        body, name="grad_scatter_blocks", in_specs=[ANY], out_specs=ANY,
        out_shape=jax.ShapeDtypeStruct((3, HALF_ROWS, D_MODEL), hsum.dtype),
        scratch_shapes=[pltpu.SemaphoreType.DMA((3,)), pltpu.SemaphoreType.DMA((3,))],
    )(hsum)


def _join_halves(rhalf):
    def body(r_ref, out_ref, send_sem, recv_sem, local_sem):
        x, y, c = _me()
        mine = pltpu.make_async_copy(r_ref, out_ref.at[c], local_sem)
        mine.start()
        cp = pltpu.make_async_remote_copy(src_ref=r_ref, dst_ref=out_ref.at[c], send_sem=send_sem,
                                          recv_sem=recv_sem, device_id=(x, y, 1 - c), device_id_type=MESH_ID)
        cp.start()
        cp.wait()
        mine.wait()

    return pl.pallas_call(
        body, name="grad_join_halves", in_specs=[ANY], out_specs=ANY,
        out_shape=jax.ShapeDtypeStruct((2, HALF_ROWS, D_MODEL), rhalf.dtype),
        scratch_shapes=[pltpu.SemaphoreType.DMA, pltpu.SemaphoreType.DMA, pltpu.SemaphoreType.DMA],
    )(rhalf)


def _gather_small(vec):
    def body(v_ref, out_ref, send_sems, recv_sems, local_sem):
        x, y, c = _me()
        my_id = 4 * x + 2 * y + c
        mine = pltpu.make_async_copy(v_ref, out_ref.at[my_id], local_sem)
        mine.start()
        cps = []
        for r in range(1, N_DEV):
            peer = (x ^ (r >> 2), y ^ ((r >> 1) & 1), c ^ (r & 1))
            cps.append(pltpu.make_async_remote_copy(
                src_ref=v_ref, dst_ref=out_ref.at[my_id], send_sem=send_sems.at[r - 1],
                recv_sem=recv_sems.at[r - 1], device_id=peer, device_id_type=MESH_ID))
        for cp in cps:
            cp.start()
        for cp in cps:
            cp.wait()
        mine.wait()

    return pl.pallas_call(
        body, name="gather_small", in_specs=[ANY], out_specs=ANY,
        out_shape=jax.ShapeDtypeStruct((N_DEV, 1, SMALL_COLS), vec.dtype),
        scratch_shapes=[pltpu.SemaphoreType.DMA((N_DEV - 1,)), pltpu.SemaphoreType.DMA((N_DEV - 1,)),
                        pltpu.SemaphoreType.DMA],
    )(vec)


def _add_rows(terms, *, name):
    r = terms[0].shape[0]
    tm = 376 if r % 376 == 0 else r
    n = len(terms)

    def body(*refs):
        acc = refs[0][...]
        for t in refs[1:n]:
            acc = acc + t[...]
        refs[n][...] = acc

    row = pl.BlockSpec((tm, D_MODEL), lambda i: (i, 0))
    return pl.pallas_call(
        body, name=name, grid=(r // tm,), in_specs=[row] * n, out_specs=row,
        out_shape=jax.ShapeDtypeStruct((r, D_MODEL), F32), compiler_params=_params(1),
    )(*terms)


def _pack_small(small, lb_logits, loss):
    def body(ng_ref, bg_ref, dlb_ref, lbl_ref, hgg_ref, qg_ref, kvg_ref, fg_ref, loss_ref, out_ref):
        out_ref[...] = jnp.zeros_like(out_ref)
        out_ref[:, 0:1024] = ng_ref[...]
        out_ref[:, 1024:3072] = bg_ref[...]
        _, p0p1 = _lower_bound(lbl_ref[...])
        dl0 = dlb_ref[...] * p0p1
        out_ref[:, 3072:4096] = dl0
        out_ref[:, 4096:5120] = -dl0
        hgg = hgg_ref[0]
        for h in range(1, HEADS):
            hgg = hgg + hgg_ref[h]
        out_ref[:, 5120:5248] = hgg
        out_ref[:, 5248:5632] = qg_ref[...]
        out_ref[:, 5632:5888] = kvg_ref[...]
        out_ref[:, 5888:6912] = fg_ref[...]
        out_ref[:, 6912:7040] = loss_ref[...]

    return pl.pallas_call(
        body, name="pack_small", out_shape=jax.ShapeDtypeStruct((1, SMALL_COLS), F32),
    )(small["norm_g"], small["b_gate"], small["lb"], lb_logits, small["hg_norm_g"], small["q_a_g"],
      small["kv_a_g"], small["final_norm_g"], loss)


def _sum_small(gathered):
    def body(g_ref, out_ref):
        acc = g_ref[0]
        for d in range(1, N_DEV):
            acc = acc + g_ref[d]
        out_ref[...] = acc

    return pl.pallas_call(
        body, name="sum_small", out_shape=jax.ShapeDtypeStruct((1, SMALL_COLS), F32),
    )(gathered)


def _adamw(w, g, m, v, *, name):
    r, cols = w.shape
    tm = 376 if r % 376 == 0 else r
    c1 = 1.0 - ADAM_B1 ** ADAM_STEP
    c2 = 1.0 - ADAM_B2 ** ADAM_STEP

    def body(w_ref, g_ref, m_ref, v_ref, d_ref, nm_ref, nv_ref):
        gv = g_ref[...]
        nm = ADAM_B1 * m_ref[...] + (1.0 - ADAM_B1) * gv
        nv = ADAM_B2 * v_ref[...] + (1.0 - ADAM_B2) * (gv * gv)
        nm_ref[...] = nm
        nv_ref[...] = nv
        d_ref[...] = -ADAM_LR * ((nm / c1) / (jnp.sqrt(nv / c2) + ADAM_EPS) + ADAM_WD * w_ref[...])

    row = pl.BlockSpec((tm, cols), lambda i: (i, 0))
    shp = jax.ShapeDtypeStruct((r, cols), F32)
    return pl.pallas_call(
        body, name=name, grid=(r // tm,), in_specs=[row] * 4, out_specs=[row] * 3, out_shape=[shp] * 3,
        compiler_params=_params(1),
    )(w, g, m, v)


def _small_vec(norm_g, b_gate, lb_logits, hg_norm_g, q_a_g, kv_a_g, fg):
    parts = [norm_g.reshape(1, -1), b_gate.reshape(1, -1), lb_logits.reshape(1, -1), hg_norm_g.reshape(1, -1),
             q_a_g.reshape(1, -1), kv_a_g.reshape(1, -1), fg.reshape(1, -1), jnp.zeros((1, SMALL_COLS - 6912), F32)]
    return jnp.concatenate(parts, axis=1)


def _split_small(vec):
    v = vec.reshape(-1)
    return (v[0:1024].reshape(1, 1024), v[1024:3072].reshape(1, 2048), v[3072:5120].reshape(2, 1024),
            v[5120:5248].reshape(1, 128), v[5248:5632].reshape(1, 384), v[5632:5888].reshape(1, 256), v[5888:6912])


def kernel(x, norm_g, w_in, b_gate, lb_logits, hg_norm_g, q_a_g, w_uq, kv_a_g, w_ukv, w_proj_a, w_proj_b, w_out, final_norm_g, loss_target, m_norm_g, m_w_in, m_b_gate, m_lb_logits, m_hg_norm_g, m_q_a_g, m_w_uq, m_kv_a_g, m_w_ukv, m_w_proj_a, m_w_proj_b, m_w_out, m_final_norm_g, v_norm_g, v_w_in, v_b_gate, v_lb_logits, v_hg_norm_g, v_q_a_g, v_w_uq, v_kv_a_g, v_w_ukv, v_w_proj_a, v_w_proj_b, v_w_out, v_final_norm_g):
    c = lax.axis_index("c")
    chip = 2 * lax.axis_index("x") + lax.axis_index("y")

    w_pack = _pack_block(w_in[0], w_uq[0], w_ukv[0], w_proj_a[0], w_proj_b[0], w_out[0])
    gathered = _gather_weights(w_pack.astype(BF16))
    fw_in, fw_uq, fw_ukv, fwpa, fwpb, fwout = _unpack_full(gathered)

    loss, grad_x, big, small = _local_step(
        x[0], loss_target[0], fw_in, fw_uq, fw_ukv, fwpa, fwpb, fwout,
        norm_g, b_gate, lb_logits, hg_norm_g, q_a_g, kv_a_g, final_norm_g.reshape(1, D_MODEL))

    g_full = _pack_full(big)
    from_sibling = _swap_halves(g_full)
    my_half = lax.dynamic_slice_in_dim(g_full, c * HALF_ROWS, HALF_ROWS, axis=1)
    hsum = _add_rows([my_half.reshape(N_CHIPS * HALF_ROWS, D_MODEL), from_sibling.reshape(N_CHIPS * HALF_ROWS, D_MODEL)],
                     name="grad_add_cores").reshape(N_CHIPS, HALF_ROWS, D_MODEL)
    landed = _scatter_blocks(hsum)
    own = lax.dynamic_index_in_dim(hsum, chip, axis=0, keepdims=False)
    rhalf = _add_rows([own, landed[0], landed[1], landed[2]], name="grad_add_chips")
    g_block = _join_halves(rhalf).reshape(PACK_ROWS, D_MODEL)

    small_sum = _sum_small(_gather_small(_pack_small(small, lb_logits, loss)))

    m_pack = _pack_block(m_w_in[0], m_w_uq[0], m_w_ukv[0], m_w_proj_a[0], m_w_proj_b[0], m_w_out[0])
    v_pack = _pack_block(v_w_in[0], v_w_uq[0], v_w_ukv[0], v_w_proj_a[0], v_w_proj_b[0], v_w_out[0])
    d_pack, nm_pack, nv_pack = _adamw(w_pack, g_block, m_pack, v_pack, name="adamw_big")
    ws = _small_vec(norm_g, b_gate, lb_logits, hg_norm_g, q_a_g, kv_a_g, final_norm_g).reshape(7, 1024)
    ms_ = _small_vec(m_norm_g, m_b_gate, m_lb_logits, m_hg_norm_g, m_q_a_g, m_kv_a_g, m_final_norm_g).reshape(7, 1024)
    vs = _small_vec(v_norm_g, v_b_gate, v_lb_logits, v_hg_norm_g, v_q_a_g, v_kv_a_g, v_final_norm_g).reshape(7, 1024)
    d_small, nm_small, nv_small = _adamw(ws, small_sum.reshape(7, 1024), ms_, vs, name="adamw_small")

    def outputs(pack, vec):
        b_in, b_uq, b_ukv, b_pa, b_pb, b_out = _unpack_block(pack)
        s_ng, s_bg, s_lb, s_hg, s_qg, s_kvg, s_fg = _split_small(vec)
        return (s_ng, b_in[None], s_bg, s_lb, s_hg, s_qg, b_uq[None], s_kvg, b_ukv[None], b_pa[None], b_pb[None],
                b_out[None], s_fg)

    total_loss = small_sum[0, 6912]
    return (total_loss, grad_x[None], *outputs(g_block, small_sum), *outputs(d_pack, d_small),
            *outputs(nm_pack, nm_small), *outputs(nv_pack, nv_small))
```

```python
import functools

import jax
import jax.numpy as jnp
from jax import lax
from jax.experimental import pallas as pl
from jax.experimental.pallas import tpu as pltpu

F32 = jnp.float32
BF16 = jnp.bfloat16

D_MODEL = 1024
HEADS = 8
HEAD_DIM = 128
HG_CHUNK = 32
QK_NOPE = 128
QK_ROPE = 64
QK_DIM = QK_NOPE + QK_ROPE
QK_PAD = 256
Q_LORA = 384
KV_LORA = 256
MS_COLS = 768
ROPE_THETA = 10000.0
EPS = 1e-6
ATT_SCALE = QK_DIM ** -0.5
LOG2E = 1.4426950408889634
LN2 = 0.6931471805599453
Q_PRESCALE = ATT_SCALE * LOG2E

ADAM_LR = 0.001
ADAM_B1 = 0.9
ADAM_B2 = 0.999
ADAM_EPS = 1e-08
ADAM_WD = 0.01
ADAM_STEP = 10

N_CHIPS = 4
N_DEV = 8
W_IN_COLS = 7872
W_IN_BLK = W_IN_COLS // N_CHIPS
PACK_ROWS = 1968 + 144 + 128 + 3 * 256
HALF_ROWS = PACK_ROWS // 2
SMALL_COLS = 7168

TM_MM = 512
TM_FUSED = 256
HG_ROWS = 256
TQ = 512
VMEM_LIMIT = 56 * 1024 * 1024


def _dot(a, b):
    return lax.dot_general(a, b, (((1,), (0,)), ((), ())), preferred_element_type=F32)


def _dot_nt(a, b):
    return lax.dot_general(a, b, (((1,), (1,)), ((), ())), preferred_element_type=F32)


def _dot_tn(a, b):
    return lax.dot_general(a, b, (((0,), (0,)), ((), ())), preferred_element_type=F32)


def _params(n_axes):
    return pltpu.CompilerParams(dimension_semantics=("arbitrary",) * n_axes, vmem_limit_bytes=VMEM_LIMIT)


def _rms(x, g):
    r = lax.rsqrt(jnp.mean(x * x, axis=-1, keepdims=True) + EPS)
    return x * r * g


def _rms_bwd(x, g, dy):
    r = lax.rsqrt(jnp.mean(x * x, axis=-1, keepdims=True) + EPS)
    xh = x * r
    dyg = dy * g
    dx = r * (dyg - xh * jnp.mean(dyg * xh, axis=-1, keepdims=True))
    return dx, dy * xh


def _silu_parts(z):
    s = jax.nn.sigmoid(z)
    return z * s, s * (1.0 + z * (1.0 - s))


def _rope(x, c, sa, sb):
    return x * c + pltpu.roll(x, 32, 1) * sa + pltpu.roll(x, 96, 1) * sb


def _rope_bwd(dy, c, sa, sb):
    return dy * c + pltpu.roll(dy * sa, 96, 1) + pltpu.roll(dy * sb, 32, 1)


def _rope_tables(seq):
    inv = ROPE_THETA ** (-jnp.arange(0, QK_ROPE, 2, dtype=F32) / QK_ROPE)
    ang = jnp.arange(seq, dtype=F32)[:, None] * inv[None, :]
    cos, sin = jnp.cos(ang), jnp.sin(ang)
    z32 = jnp.zeros_like(cos)
    z64 = jnp.zeros((seq, 64), F32)
    c = jnp.concatenate([cos, cos, z64], axis=1)
    sa = jnp.concatenate([z32, sin, z64], axis=1)
    sb = jnp.concatenate([-sin, z32, z64], axis=1)
    return c, sa, sb


def _mm(a, b, *, name, trans_b=False, add=None, out_dtype=F32, tm=TM_MM, tn=1024, tk=1024):
    m, k = a.shape
    n = b.shape[0] if trans_b else b.shape[1]
    tm, tn, tk = min(tm, m), min(tn, n), min(tk, k)
    assert m % tm == 0 and n % tn == 0 and k % tk == 0
    nk = k // tk
    has_add = add is not None

    def body(*refs):
        if has_add:
            a_ref, b_ref, add_ref, o_ref, acc_ref = refs
        else:
            a_ref, b_ref, o_ref, acc_ref = refs
        kk = pl.program_id(2)

        @pl.when(kk == 0)
        def _():
            acc_ref[...] = add_ref[...] if has_add else jnp.zeros_like(acc_ref)

        if trans_b:
            acc_ref[...] += _dot_nt(a_ref[...], b_ref[...])
        else:
            acc_ref[...] += _dot(a_ref[...], b_ref[...])

        @pl.when(kk == nk - 1)
        def _():
            o_ref[...] = acc_ref[...].astype(out_dtype)

    in_specs = [pl.BlockSpec((tm, tk), lambda i, j, kk: (i, kk))]
    if trans_b:
        in_specs.append(pl.BlockSpec((tn, tk), lambda i, j, kk: (j, kk)))
    else:
        in_specs.append(pl.BlockSpec((tk, tn), lambda i, j, kk: (kk, j)))
    args = [a, b]
    if has_add:
        in_specs.append(pl.BlockSpec((tm, tn), lambda i, j, kk: (i, j)))
        args.append(add)
    return pl.pallas_call(
        body, name=name, grid=(m // tm, n // tn, nk),
        in_specs=in_specs, out_specs=pl.BlockSpec((tm, tn), lambda i, j, kk: (i, j)),
        out_shape=jax.ShapeDtypeStruct((m, n), out_dtype),
        scratch_shapes=[pltpu.VMEM((tm, tn), F32)], compiler_params=_params(3),
    )(*args)


def _mm_tn(a, b, *, name, tm=TM_MM, tn=1024):
    m, k = a.shape
    n = b.shape[1]
    tm, tn = min(tm, m), min(tn, n)
    assert m % tm == 0 and n % tn == 0

    def body(a_ref, b_ref, o_ref):
        @pl.when(pl.program_id(1) == 0)
        def _():
            o_ref[...] = jnp.zeros_like(o_ref)

        o_ref[...] += _dot_tn(a_ref[...], b_ref[...])

    return pl.pallas_call(
        body, name=name, grid=(n // tn, m // tm),
        in_specs=[pl.BlockSpec((tm, k), lambda j, i: (i, 0)), pl.BlockSpec((tm, tn), lambda j, i: (i, j))],
        out_specs=pl.BlockSpec((k, tn), lambda j, i: (0, j)),
        out_shape=jax.ShapeDtypeStruct((k, n), F32), compiler_params=_params(2),
    )(a, b)


def _norm_in(x, g):
    s = x.shape[0]
    tm = min(TM_MM, s)

    def body(x_ref, g_ref, h_ref):
        h_ref[...] = _rms(x_ref[...], g_ref[...]).astype(BF16)

    return pl.pallas_call(
        body, name="norm_in", grid=(s // tm,),
        in_specs=[pl.BlockSpec((tm, D_MODEL), lambda i: (i, 0)), pl.BlockSpec((1, D_MODEL), lambda i: (0, 0))],
        out_specs=pl.BlockSpec((tm, D_MODEL), lambda i: (i, 0)),
        out_shape=jax.ShapeDtypeStruct((s, D_MODEL), BF16), compiler_params=_params(1),
    )(x, g)


def _norm_in_bwd(x, g, dh, dx2):
    s = x.shape[0]
    tm = min(TM_MM, s)

    def body(x_ref, g_ref, dh_ref, dx2_ref, dx_ref, dg_ref):
        @pl.when(pl.program_id(0) == 0)
        def _():
            dg_ref[...] = jnp.zeros_like(dg_ref)

        dx, dg_rows = _rms_bwd(x_ref[...], g_ref[...], dh_ref[...])
        dx_ref[...] = dx + dx2_ref[...]
        dg_ref[...] += jnp.sum(dg_rows, axis=0, keepdims=True)

    row = pl.BlockSpec((tm, D_MODEL), lambda i: (i, 0))
    vec = pl.BlockSpec((1, D_MODEL), lambda i: (0, 0))
    return pl.pallas_call(
        body, name="norm_in_bwd", grid=(s // tm,),
        in_specs=[row, vec, row, row], out_specs=[row, vec],
        out_shape=[jax.ShapeDtypeStruct((s, D_MODEL), F32), jax.ShapeDtypeStruct((1, D_MODEL), F32)],
        compiler_params=_params(1),
    )(x, g, dh, dx2)


def _chunk_rows(rows):
    return lax.broadcasted_iota(jnp.int32, (rows, HEAD_DIM), 0) & (HG_CHUNK - 1)


def _chunk_cumsum(x, rows):
    pos = _chunk_rows(rows)
    shift = 1
    while shift < HG_CHUNK:
        x = x + jnp.where(pos >= shift, pltpu.roll(x, shift, 0), 0.0)
        shift *= 2
    return x


def _chunk_revcumsum(x, rows):
    pos = _chunk_rows(rows)
    shift = 1
    while shift < HG_CHUNK:
        x = x + jnp.where(pos + shift < HG_CHUNK, pltpu.roll(x, rows - shift, 0), 0.0)
        shift *= 2
    return x


def _lower_bound(lbl):
    mx = jnp.maximum(lbl[0:1, :], lbl[1:2, :])
    e0 = jnp.exp(lbl[0:1, :] - mx)
    e1 = jnp.exp(lbl[1:2, :] - mx)
    p0 = e0 / (e0 + e1)
    return p0, p0 * (e1 / (e0 + e1))


def _tril():
    r = lax.broadcasted_iota(jnp.int32, (HG_CHUNK, HG_CHUNK), 0)
    c = lax.broadcasted_iota(jnp.int32, (HG_CHUNK, HG_CHUNK), 1)
    return r >= c


def _hgrn_fwd(hg, lb_logits, norm_g):
    s = hg.shape[0]
    rows = min(HG_ROWS, s)
    nblk = s // rows
    nch = rows // HG_CHUNK

    def body(hg_ref, lbl_ref, g_ref, o_ref, ya_ref, st0_ref, st_s):
        @pl.when(pl.program_id(1) == 0)
        def _():
            st_s[...] = jnp.zeros_like(st_s)

        hq = hg_ref[:, 0:128]
        hf = hg_ref[:, 128:256]
        hi = hg_ref[:, 256:384]
        hz = hg_ref[:, 384:512]
        lb, _ = _lower_bound(lbl_ref[...])
        f = lb + (1.0 - lb) * jax.nn.sigmoid(hf)
        q = hq * jax.nn.sigmoid(hq)
        k = 1.0 - f
        b = _chunk_cumsum(jnp.log(f), rows)
        q_in = (q * jnp.exp(b)).astype(BF16)
        k_in = (k * jnp.exp(-b)).astype(BF16)
        vb = hi.astype(BF16)
        tril = _tril()
        st = st_s[...]
        st0_ref[...] = st
        for c in range(nch):
            sl = slice(c * HG_CHUNK, (c + 1) * HG_CHUNK)
            b_c = b[sl]
            bl = b_c[HG_CHUNK - 1:HG_CHUNK, :]
            k_out = (k[sl] * jnp.exp(bl - b_c)).astype(BF16)
            sc = jnp.where(tril, _dot_nt(q_in[sl], k_in[sl]), 0.0)
            o_ref[sl, :] = _dot(sc.astype(BF16), vb[sl]) + _dot_nt(q_in[sl], st.astype(BF16))
            st = st * jnp.exp(bl) + _dot_tn(vb[sl], k_out)
        st_s[...] = st
        o = o_ref[...]
        silu_z, _ = _silu_parts(hz)
        ya_ref[...] = (_rms(o, g_ref[...]) * silu_z).astype(BF16)

    return pl.pallas_call(
        body, name="hgrn_fwd", grid=(HEADS, nblk),
        in_specs=[pl.BlockSpec((rows, 512), lambda h, i: (i, h)),
                  pl.BlockSpec((2, HEAD_DIM), lambda h, i: (0, h)),
                  pl.BlockSpec((1, HEAD_DIM), lambda h, i: (0, 0))],
        out_specs=[pl.BlockSpec((rows, HEAD_DIM), lambda h, i: (i, h)),
                   pl.BlockSpec((rows, HEAD_DIM), lambda h, i: (i, h)),
                   pl.BlockSpec((None, None, HEAD_DIM, HEAD_DIM), lambda h, i: (h, i, 0, 0))],
        out_shape=[jax.ShapeDtypeStruct((s, D_MODEL), F32), jax.ShapeDtypeStruct((s, D_MODEL), BF16),
                   jax.ShapeDtypeStruct((HEADS, nblk, HEAD_DIM, HEAD_DIM), F32)],
        scratch_shapes=[pltpu.VMEM((HEAD_DIM, HEAD_DIM), F32)], compiler_params=_params(2),
    )(hg, lb_logits, norm_g)


def _hgrn_bwd(hg, o_pre, dya, st0, lb_logits, norm_g):
    s = hg.shape[0]
    rows = min(HG_ROWS, s)
    nblk = s // rows
    nch = rows // HG_CHUNK

    def body(hg_ref, o_ref, dya_ref, st0_ref, lbl_ref, g_ref, dhg_ref, dlb_ref, dg_ref,
             dst_s, stp_s, ebl_s, dqin_s, dkin_s, dkout_s, dv_s, dbl_s):
        @pl.when(pl.program_id(1) == 0)
        def _():
            dst_s[...] = jnp.zeros_like(dst_s)
            dlb_ref[...] = jnp.zeros_like(dlb_ref)
            dg_ref[...] = jnp.zeros_like(dg_ref)

        hq = hg_ref[:, 0:128]
        hf = hg_ref[:, 128:256]
        hi = hg_ref[:, 256:384]
        hz = hg_ref[:, 384:512]
        lb, _ = _lower_bound(lbl_ref[...])
        sg = jax.nn.sigmoid(hf)
        f = lb + (1.0 - lb) * sg
        q, dsilu_q = _silu_parts(hq)
        k = 1.0 - f
        b = _chunk_cumsum(jnp.log(f), rows)
        eb = jnp.exp(b)
        enb = jnp.exp(-b)
        q_in32 = q * eb
        k_in32 = k * enb
        q_in = q_in32.astype(BF16)
        k_in = k_in32.astype(BF16)
        vb = hi.astype(BF16)
        tril = _tril()

        st = st0_ref[...]
        decs = []
        for c in range(nch):
            sl = slice(c * HG_CHUNK, (c + 1) * HG_CHUNK)
            b_c = b[sl]
            bl = b_c[HG_CHUNK - 1:HG_CHUNK, :]
            ebl = jnp.exp(bl - b_c)
            ebl_s[sl, :] = ebl
            decs.append(jnp.exp(bl))
            stp_s[c] = st
            st = st * decs[c] + _dot_tn(vb[sl], (k[sl] * ebl).astype(BF16))

        g = g_ref[...]
        o = o_ref[...]
        rstd = lax.rsqrt(jnp.mean(o * o, axis=-1, keepdims=True) + EPS)
        oh = o * rstd
        silu_z, dsilu_z = _silu_parts(hz)
        dya_v = dya_ref[...]
        dn = dya_v * silu_z
        dhz = dya_v * (oh * g) * dsilu_z
        dg_ref[...] += jnp.sum(dn * oh, axis=0, keepdims=True)
        doh = dn * g
        do = (rstd * (doh - oh * jnp.mean(doh * oh, axis=-1, keepdims=True))).astype(BF16)

        k_out32 = k * ebl_s[...]
        k_out = k_out32.astype(BF16)
        dst = dst_s[...]
        for c in reversed(range(nch)):
            sl = slice(c * HG_CHUNK, (c + 1) * HG_CHUNK)
            stp = stp_s[c]
            dstb = dst.astype(BF16)
            sc = jnp.where(tril, _dot_nt(q_in[sl], k_in[sl]), 0.0)
            dkout = _dot(vb[sl], dstb)
            dv_s[sl, :] = _dot_nt(k_out[sl], dstb) + _dot_tn(sc.astype(BF16), do[sl])
            dsc = jnp.where(tril, _dot_nt(do[sl], vb[sl]), 0.0).astype(BF16)
            dqin_s[sl, :] = _dot(dsc, k_in[sl]) + _dot(do[sl], stp.astype(BF16))
            dkin_s[sl, :] = _dot_tn(dsc, q_in[sl])
            dkout_s[sl, :] = dkout
            ddec = jnp.sum(dst * stp, axis=0, keepdims=True)
            dbl = jnp.sum(dkout * k_out32[sl], axis=0, keepdims=True) + ddec * decs[c]
            dbl_s[sl, :] = jnp.broadcast_to(dbl, (HG_CHUNK, HEAD_DIM))
            dst = dst * decs[c] + _dot_tn(do[sl], q_in[sl])
        dst_s[...] = dst

        dqin = dqin_s[...]
        dkin = dkin_s[...]
        dkout = dkout_s[...]
        dq = dqin * eb
        dk = dkin * enb + dkout * ebl_s[...]
        db = dqin * q_in32 - dkin * k_in32 - dkout * k_out32
        dlogf = _chunk_revcumsum(db, rows) + dbl_s[...]
        df = dlogf / f - dk
        dlb_ref[...] += jnp.sum(df * (1.0 - sg), axis=0, keepdims=True)
        dhg_ref[:, 0:128] = (dq * dsilu_q).astype(BF16)
        dhg_ref[:, 128:256] = (df * (1.0 - lb) * sg * (1.0 - sg)).astype(BF16)
        dhg_ref[:, 256:384] = dv_s[...].astype(BF16)
        dhg_ref[:, 384:512] = dhz.astype(BF16)

    last = nblk - 1
    tile = pltpu.VMEM((rows, HEAD_DIM), F32)
    return pl.pallas_call(
        body, name="hgrn_bwd", grid=(HEADS, nblk),
        in_specs=[pl.BlockSpec((rows, 512), lambda h, i: (last - i, h)),
                  pl.BlockSpec((rows, HEAD_DIM), lambda h, i: (last - i, h)),
                  pl.BlockSpec((rows, HEAD_DIM), lambda h, i: (last - i, h)),
                  pl.BlockSpec((None, None, HEAD_DIM, HEAD_DIM), lambda h, i: (h, last - i, 0, 0)),
                  pl.BlockSpec((2, HEAD_DIM), lambda h, i: (0, h)),
                  pl.BlockSpec((1, HEAD_DIM), lambda h, i: (0, 0))],
        out_specs=[pl.BlockSpec((rows, 512), lambda h, i: (last - i, h)),
                   pl.BlockSpec((1, HEAD_DIM), lambda h, i: (0, h)),
                   pl.BlockSpec((None, 1, HEAD_DIM), lambda h, i: (h, 0, 0))],
        out_shape=[jax.ShapeDtypeStruct((s, 4 * D_MODEL), BF16), jax.ShapeDtypeStruct((1, D_MODEL), F32),
                   jax.ShapeDtypeStruct((HEADS, 1, HEAD_DIM), F32)],
        scratch_shapes=[pltpu.VMEM((HEAD_DIM, HEAD_DIM), F32), pltpu.VMEM((nch, HEAD_DIM, HEAD_DIM), F32),
                        tile, tile, tile, tile, tile, tile],
        compiler_params=_params(2),
    )(hg, o_pre, dya, st0, lb_logits, norm_g)


def _mla_pre(ms, q_a_g, kv_a_g, wuq3, wukv3, tabs):
    s = ms.shape[0]
    tm = min(TM_FUSED, s)

    def body(ms_ref, qg_ref, kvg_ref, wuq_ref, wukv_ref, c_ref, sa_ref, sb_ref,
             q_ref, k_ref, v_ref, cqn_ref, ckvn_ref):
        c, sa, sb = c_ref[...], sa_ref[...], sb_ref[...]
        cqn = _rms(ms_ref[:, 0:Q_LORA], qg_ref[...]).astype(BF16)
        ckvn = _rms(ms_ref[:, Q_LORA:Q_LORA + KV_LORA], kvg_ref[...]).astype(BF16)
        cqn_ref[...] = cqn
        ckvn_ref[...] = ckvn
        k_pe = _rope(ms_ref[:, Q_LORA + KV_LORA:MS_COLS], c, sa, sb).astype(BF16)
        for h in range(HEADS):
            qh = _dot(cqn, wuq_ref[h])
            q_ref[h, :, 0:128] = (qh[:, 0:128] * Q_PRESCALE).astype(BF16)
            q_ref[h, :, 128:256] = (_rope(qh[:, 128:256], c, sa, sb) * Q_PRESCALE).astype(BF16)
            kvh = _dot(ckvn, wukv_ref[h])
            k_ref[h, :, 0:128] = kvh[:, 0:128].astype(BF16)
            k_ref[h, :, 128:256] = k_pe
            v_ref[h] = kvh[:, 128:256].astype(BF16)

    tab = pl.BlockSpec((tm, 128), lambda i: (i, 0))
    return pl.pallas_call(
        body, name="mla_pre", grid=(s // tm,),
        in_specs=[pl.BlockSpec((tm, MS_COLS), lambda i: (i, 0)),
                  pl.BlockSpec((1, Q_LORA), lambda i: (0, 0)), pl.BlockSpec((1, KV_LORA), lambda i: (0, 0)),
                  pl.BlockSpec((HEADS, Q_LORA, QK_PAD), lambda i: (0, 0, 0)),
                  pl.BlockSpec((HEADS, KV_LORA, 256), lambda i: (0, 0, 0)), tab, tab, tab],
        out_specs=[pl.BlockSpec((HEADS, tm, QK_PAD), lambda i: (0, i, 0)),
                   pl.BlockSpec((HEADS, tm, QK_PAD), lambda i: (0, i, 0)),
                   pl.BlockSpec((HEADS, tm, HEAD_DIM), lambda i: (0, i, 0)),
                   pl.BlockSpec((tm, Q_LORA), lambda i: (i, 0)), pl.BlockSpec((tm, KV_LORA), lambda i: (i, 0))],
        out_shape=[jax.ShapeDtypeStruct((HEADS, s, QK_PAD), BF16), jax.ShapeDtypeStruct((HEADS, s, QK_PAD), BF16),
                   jax.ShapeDtypeStruct((HEADS, s, HEAD_DIM), BF16),
                   jax.ShapeDtypeStruct((s, Q_LORA), BF16), jax.ShapeDtypeStruct((s, KV_LORA), BF16)],
        compiler_params=_params(1),
    )(ms, q_a_g, kv_a_g, wuq3, wukv3, *tabs)


def _causal_mask(t):
    r = lax.broadcasted_iota(jnp.int32, (t, t), 0)
    c = lax.broadcasted_iota(jnp.int32, (t, t), 1)
    return r >= c


def _flash_fwd(q, k, v, mz):
    s = q.shape[1]
    t = min(TQ, s)

    def body(q_ref, k_ref, v_ref, mz_ref, o_ref, yb_ref, lse_ref, m_s, l_s, acc_s):
        i = pl.program_id(1)
        qv = q_ref[...]
        m_s[...] = jnp.full_like(m_s, -jnp.inf)
        l_s[...] = jnp.zeros_like(l_s)
        acc_s[...] = jnp.zeros_like(acc_s)

        def step(j, masked):
            rows = pl.ds(pl.multiple_of(j * t, t), t)
            sc = _dot_nt(qv, k_ref[rows, :])
            if masked:
                sc = jnp.where(_causal_mask(t), sc, -jnp.inf)
            m_prev = m_s[...]
            m_new = jnp.maximum(m_prev, jnp.max(sc, axis=-1, keepdims=True))
            p = jnp.exp2(sc - jnp.tile(m_new, (1, t // 128)))
            alpha = jnp.exp2(m_prev - m_new)
            l_s[...] = alpha * l_s[...] + jnp.sum(p, axis=-1, keepdims=True)
            acc_s[...] = alpha * acc_s[...] + _dot(p.astype(BF16), v_ref[rows, :])
            m_s[...] = m_new

        def loop_body(j, carry):
            step(j, False)
            return carry

        lax.fori_loop(0, i, loop_body, 0)
        step(i, True)
        out = acc_s[...] / l_s[...]
        o_ref[...] = out
        silu_z, _ = _silu_parts(mz_ref[...])
        yb_ref[...] = (out * silu_z).astype(BF16)
        lse_ref[...] = m_s[...] + jnp.log2(l_s[...])

    col = pl.BlockSpec((t, HEAD_DIM), lambda h, i: (i, h))
    return pl.pallas_call(
        body, name="flash_fwd", grid=(HEADS, s // t),
        in_specs=[pl.BlockSpec((None, t, QK_PAD), lambda h, i: (h, i, 0)),
                  pl.BlockSpec((None, s, QK_PAD), lambda h, i: (h, 0, 0)),
                  pl.BlockSpec((None, s, HEAD_DIM), lambda h, i: (h, 0, 0)), col],
        out_specs=[col, col, pl.BlockSpec((None, t, 128), lambda h, i: (h, i, 0))],
        out_shape=[jax.ShapeDtypeStruct((s, D_MODEL), F32), jax.ShapeDtypeStruct((s, D_MODEL), BF16),
                   jax.ShapeDtypeStruct((HEADS, s, 128), F32)],
        scratch_shapes=[pltpu.VMEM((t, 128), F32), pltpu.VMEM((t, 128), F32), pltpu.VMEM((t, HEAD_DIM), F32)],
        compiler_params=_params(2),
    )(q, k, v, mz)


def _flash_bwd(q, k, v, dyb, mz, o_att, lse, tabs):
    s = q.shape[1]
    t = min(TQ, s)

    def body(q_ref, k_ref, v_ref, dyb_ref, mz_ref, o_ref, lse_ref, c_ref, sa_ref, sb_ref,
             dq_ref, dk_ref, dv_ref, dmz_ref, dq_s, delta_s):
        i = pl.program_id(1)

        @pl.when(i == 0)
        def _():
            dk_ref[...] = jnp.zeros_like(dk_ref)
            dv_ref[...] = jnp.zeros_like(dv_ref)

        silu_z, dsilu_z = _silu_parts(mz_ref[...])
        dyb_v = dyb_ref[...]
        out = o_ref[...]
        do32 = dyb_v * silu_z
        dmz_ref[...] = (dyb_v * out * dsilu_z).astype(BF16)
        delta_s[...] = jnp.broadcast_to(jnp.sum(do32 * out, axis=-1, keepdims=True), (t, 128))
        do = do32.astype(BF16)
        qv = q_ref[...]
        dq_s[...] = jnp.zeros_like(dq_s)

        def step(j, masked):
            rows = pl.ds(pl.multiple_of(j * t, t), t)
            kj = k_ref[rows, :]
            vj = v_ref[rows, :]
            sc = _dot_nt(qv, kj)
            if masked:
                sc = jnp.where(_causal_mask(t), sc, -jnp.inf)
            p = jnp.exp2(sc - jnp.tile(lse_ref[...], (1, t // 128)))
            dp = _dot_nt(do, vj)
            ds = (p * (dp - jnp.tile(delta_s[...], (1, t // 128)))).astype(BF16)
            dv_ref[rows, :] += _dot_tn(p.astype(BF16), do)
            dk_ref[rows, :] += _dot_tn(ds, qv)
            dq_s[...] += _dot(ds, kj)

        def loop_body(j, carry):
            step(j, False)
            return carry

        lax.fori_loop(0, i, loop_body, 0)
        step(i, True)
        dq = dq_s[...] * ATT_SCALE
        dq_ref[:, 0:128] = dq[:, 0:128].astype(BF16)
        dq_ref[:, 128:256] = _rope_bwd(dq[:, 128:256], c_ref[...], sa_ref[...], sb_ref[...]).astype(BF16)

    col = pl.BlockSpec((t, HEAD_DIM), lambda h, i: (i, h))
    tab = pl.BlockSpec((t, 128), lambda h, i: (i, 0))
    return pl.pallas_call(
        body, name="flash_bwd", grid=(HEADS, s // t),
        in_specs=[pl.BlockSpec((None, t, QK_PAD), lambda h, i: (h, i, 0)),
                  pl.BlockSpec((None, s, QK_PAD), lambda h, i: (h, 0, 0)),
                  pl.BlockSpec((None, s, HEAD_DIM), lambda h, i: (h, 0, 0)),
                  col, col, col, pl.BlockSpec((None, t, 128), lambda h, i: (h, i, 0)), tab, tab, tab],
        out_specs=[pl.BlockSpec((None, t, QK_PAD), lambda h, i: (h, i, 0)),
                   pl.BlockSpec((None, s, QK_PAD), lambda h, i: (h, 0, 0)),
                   pl.BlockSpec((None, s, HEAD_DIM), lambda h, i: (h, 0, 0)), col],
        out_shape=[jax.ShapeDtypeStruct((HEADS, s, QK_PAD), BF16), jax.ShapeDtypeStruct((HEADS, s, QK_PAD), F32),
                   jax.ShapeDtypeStruct((HEADS, s, HEAD_DIM), F32), jax.ShapeDtypeStruct((s, D_MODEL), BF16)],
        scratch_shapes=[pltpu.VMEM((t, QK_PAD), F32), pltpu.VMEM((t, 128), F32)], compiler_params=_params(2),
    )(q, k, v, dyb, mz, o_att, lse, *tabs)


def _mla_bwd_proj(dq, dk, dv, cqn, ckvn, ms, q_a_g, kv_a_g, wuq3, wukv3, tabs):
    s = ms.shape[0]
    tm = min(TM_FUSED, s)

    def body(dq_ref, dk_ref, dv_ref, cqn_ref, ckvn_ref, ms_ref, qg_ref, kvg_ref, wuq_ref, wukv_ref,
             c_ref, sa_ref, sb_ref, dms_ref, dwuq_ref, dwukv_ref, dqg_ref, dkvg_ref):
        @pl.when(pl.program_id(0) == 0)
        def _():
            dwuq_ref[...] = jnp.zeros_like(dwuq_ref)
            dwukv_ref[...] = jnp.zeros_like(dwukv_ref)
            dqg_ref[...] = jnp.zeros_like(dqg_ref)
            dkvg_ref[...] = jnp.zeros_like(dkvg_ref)

        cqn = cqn_ref[...]
        ckvn = ckvn_ref[...]
        dcqn = jnp.zeros((tm, Q_LORA), F32)
        dckvn = jnp.zeros((tm, KV_LORA), F32)
        dkpe = jnp.zeros((tm, 128), F32)
        for h in range(HEADS):
            dqh = dq_ref[h]
            dcqn += _dot_nt(dqh, wuq_ref[h])
            dwuq_ref[h] += _dot_tn(cqn, dqh)
            dkh = dk_ref[h] * LN2
            dkvh = jnp.concatenate([dkh[:, 0:128], dv_ref[h]], axis=1).astype(BF16)
            dckvn += _dot_nt(dkvh, wukv_ref[h])
            dwukv_ref[h] += _dot_tn(ckvn, dkvh)
            dkpe += dkh[:, 128:256]
        dcq, dqg_rows = _rms_bwd(ms_ref[:, 0:Q_LORA], qg_ref[...], dcqn)
        dckv, dkvg_rows = _rms_bwd(ms_ref[:, Q_LORA:Q_LORA + KV_LORA], kvg_ref[...], dckvn)
        dqg_ref[...] += jnp.sum(dqg_rows, axis=0, keepdims=True)
        dkvg_ref[...] += jnp.sum(dkvg_rows, axis=0, keepdims=True)
        dms_ref[:, 0:Q_LORA] = dcq.astype(BF16)
        dms_ref[:, Q_LORA:Q_LORA + KV_LORA] = dckv.astype(BF16)
        dms_ref[:, Q_LORA + KV_LORA:MS_COLS] = _rope_bwd(dkpe, c_ref[...], sa_ref[...], sb_ref[...]).astype(BF16)

    tab = pl.BlockSpec((tm, 128), lambda i: (i, 0))
    wq = pl.BlockSpec((HEADS, Q_LORA, QK_PAD), lambda i: (0, 0, 0))
    wkv = pl.BlockSpec((HEADS, KV_LORA, 256), lambda i: (0, 0, 0))
    qg = pl.BlockSpec((1, Q_LORA), lambda i: (0, 0))
    kvg = pl.BlockSpec((1, KV_LORA), lambda i: (0, 0))
    return pl.pallas_call(
        body, name="mla_bwd_proj", grid=(s // tm,),
        in_specs=[pl.BlockSpec((HEADS, tm, QK_PAD), lambda i: (0, i, 0)),
                  pl.BlockSpec((HEADS, tm, QK_PAD), lambda i: (0, i, 0)),
                  pl.BlockSpec((HEADS, tm, HEAD_DIM), lambda i: (0, i, 0)),
                  pl.BlockSpec((tm, Q_LORA), lambda i: (i, 0)), pl.BlockSpec((tm, KV_LORA), lambda i: (i, 0)),
                  pl.BlockSpec((tm, MS_COLS), lambda i: (i, 0)), qg, kvg, wq, wkv, tab, tab, tab],
        out_specs=[pl.BlockSpec((tm, MS_COLS), lambda i: (i, 0)), wq, wkv, qg, kvg],
        out_shape=[jax.ShapeDtypeStruct((s, MS_COLS), BF16), jax.ShapeDtypeStruct((HEADS, Q_LORA, QK_PAD), F32),
                   jax.ShapeDtypeStruct((HEADS, KV_LORA, 256), F32),
                   jax.ShapeDtypeStruct((1, Q_LORA), F32), jax.ShapeDtypeStruct((1, KV_LORA), F32)],
        compiler_params=_params(1),
    )(dq, dk, dv, cqn, ckvn, ms, q_a_g, kv_a_g, wuq3, wukv3, *tabs)


def _merge_loss(ya, yb, glog, b_gate, x, tgt, fg, wpa, wpb, wout):
    s = x.shape[0]
    tm = min(TM_FUSED, s)

    def body(ya_ref, yb_ref, g0_ref, g1_ref, b0_ref, b1_ref, x_ref, t_ref, fg_ref, wpa_ref, wpb_ref, wout_ref,
             mg_ref, pa_ref, pb_ref, dx2_ref, loss_ref, dfg_ref):
        @pl.when(pl.program_id(0) == 0)
        def _():
            loss_ref[...] = jnp.zeros_like(loss_ref)
            dfg_ref[...] = jnp.zeros_like(dfg_ref)

        pa = _dot(ya_ref[...], wpa_ref[...])
        pb = _dot(yb_ref[...], wpb_ref[...])
        pa_ref[...] = pa
        pb_ref[...] = pb
        merged = (jax.nn.sigmoid(g0_ref[...] + b0_ref[...]) * pa
                  + jax.nn.sigmoid(g1_ref[...] + b1_ref[...]) * pb).astype(BF16)
        mg_ref[...] = merged
        x2 = x_ref[...] + _dot(merged, wout_ref[...])
        fg_v = fg_ref[...]
        err = _rms(x2, fg_v) - t_ref[...]
        loss_ref[...] += 0.5 * jnp.sum(jnp.mean(err * err, axis=-1, keepdims=True), axis=0, keepdims=True)
        dx2, dfg_rows = _rms_bwd(x2, fg_v, err * (1.0 / D_MODEL))
        dx2_ref[...] = dx2
        dfg_ref[...] += jnp.sum(dfg_rows, axis=0, keepdims=True)

    row = pl.BlockSpec((tm, D_MODEL), lambda i: (i, 0))
    row1 = pl.BlockSpec((tm, D_MODEL), lambda i: (i, 1))
    vec = pl.BlockSpec((1, D_MODEL), lambda i: (0, 0))
    vec1 = pl.BlockSpec((1, D_MODEL), lambda i: (0, 1))
    wsp = pl.BlockSpec((D_MODEL, D_MODEL), lambda i: (0, 0))
    return pl.pallas_call(
        body, name="merge_loss", grid=(s // tm,),
        in_specs=[row, row, row, row1, vec, vec1, row, row, vec, wsp, wsp, wsp],
        out_specs=[row, row, row, row, pl.BlockSpec((1, 128), lambda i: (0, 0)), vec],
        out_shape=[jax.ShapeDtypeStruct((s, D_MODEL), BF16), jax.ShapeDtypeStruct((s, D_MODEL), F32),
                   jax.ShapeDtypeStruct((s, D_MODEL), F32), jax.ShapeDtypeStruct((s, D_MODEL), F32),
                   jax.ShapeDtypeStruct((1, 128), F32), jax.ShapeDtypeStruct((1, D_MODEL), F32)],
        compiler_params=_params(1),
    )(ya, yb, glog, glog, b_gate, b_gate, x, tgt, fg, wpa, wpb, wout)


def _merge_bwd(dx2, pa, pb, glog, b_gate, wpa, wpb, wout):
    s = dx2.shape[0]
    tm = min(TM_FUSED, s)

    def body(dx2_ref, pa_ref, pb_ref, g0_ref, g1_ref, b0_ref, b1_ref, wpa_ref, wpb_ref, wout_ref,
             dya_ref, dyb_ref, dgl_ref, dpa_ref, dpb_ref, dx2b_ref, dbg_ref):
        @pl.when(pl.program_id(0) == 0)
        def _():
            dbg_ref[...] = jnp.zeros_like(dbg_ref)

        dx2b = dx2_ref[...].astype(BF16)
        dx2b_ref[...] = dx2b
        dmg = _dot_nt(dx2b, wout_ref[...])
        g0 = jax.nn.sigmoid(g0_ref[...] + b0_ref[...])
        g1 = jax.nn.sigmoid(g1_ref[...] + b1_ref[...])
        dpa = (dmg * g0).astype(BF16)
        dpb = (dmg * g1).astype(BF16)
        dpa_ref[...] = dpa
        dpb_ref[...] = dpb
        dgl0 = dmg * pa_ref[...] * g0 * (1.0 - g0)
        dgl1 = dmg * pb_ref[...] * g1 * (1.0 - g1)
        dgl_ref[:, 0:D_MODEL] = dgl0.astype(BF16)
        dgl_ref[:, D_MODEL:2 * D_MODEL] = dgl1.astype(BF16)
        dbg_ref[:, 0:D_MODEL] += jnp.sum(dgl0, axis=0, keepdims=True)
        dbg_ref[:, D_MODEL:2 * D_MODEL] += jnp.sum(dgl1, axis=0, keepdims=True)
        dya_ref[...] = _dot_nt(dpa, wpa_ref[...])
        dyb_ref[...] = _dot_nt(dpb, wpb_ref[...])

    row = pl.BlockSpec((tm, D_MODEL), lambda i: (i, 0))
    row1 = pl.BlockSpec((tm, D_MODEL), lambda i: (i, 1))
    row2 = pl.BlockSpec((tm, 2 * D_MODEL), lambda i: (i, 0))
    vec = pl.BlockSpec((1, D_MODEL), lambda i: (0, 0))
    vec1 = pl.BlockSpec((1, D_MODEL), lambda i: (0, 1))
    vec2 = pl.BlockSpec((1, 2 * D_MODEL), lambda i: (0, 0))
    wsp = pl.BlockSpec((D_MODEL, D_MODEL), lambda i: (0, 0))
    return pl.pallas_call(
        body, name="merge_bwd", grid=(s // tm,),
        in_specs=[row, row, row, row, row1, vec, vec1, wsp, wsp, wsp],
        out_specs=[row, row, row2, row, row, row, vec2],
        out_shape=[jax.ShapeDtypeStruct((s, D_MODEL), F32), jax.ShapeDtypeStruct((s, D_MODEL), F32),
                   jax.ShapeDtypeStruct((s, 2 * D_MODEL), BF16), jax.ShapeDtypeStruct((s, D_MODEL), BF16),
                   jax.ShapeDtypeStruct((s, D_MODEL), BF16), jax.ShapeDtypeStruct((s, D_MODEL), BF16),
                   jax.ShapeDtypeStruct((1, 2 * D_MODEL), F32)],
        compiler_params=_params(1),
    )(dx2, pa, pb, glog, glog, b_gate, b_gate, wpa, wpb, wout)


def _hg_perm(w):
    r = w.shape[0]
    return w.reshape(r, 4, HEADS, HEAD_DIM).transpose(0, 2, 1, 3).reshape(r, 4 * D_MODEL)


def _hg_unperm(w):
    r = w.shape[0]
    return w.reshape(r, HEADS, 4, HEAD_DIM).transpose(0, 2, 1, 3).reshape(r, 4 * D_MODEL)


def _local_step(x, tgt, w_in, w_uq, w_ukv, wpa, wpb, wout, norm_g, b_gate, lb_logits, hg_norm_g, q_a_g, kv_a_g, fg):
    s = x.shape[0]
    zpad = jnp.zeros((D_MODEL, 64), BF16)
    w_hg = _hg_perm(w_in[:, 0:4096])
    w_ms = jnp.concatenate([w_in[:, 4096:4800], zpad], axis=1)
    w_mz = w_in[:, 4800:5824]
    w_gl = w_in[:, 5824:7872]
    wuq3 = jnp.pad(w_uq.reshape(Q_LORA, HEADS, QK_DIM).transpose(1, 0, 2), ((0, 0), (0, 0), (0, QK_PAD - QK_DIM)))
    wukv3 = w_ukv.reshape(KV_LORA, HEADS, 256).transpose(1, 0, 2)
    tabs = _rope_tables(s)

    h = _norm_in(x, norm_g)
    hg = _mm(h, w_hg, name="proj_hg")
    ms = _mm(h, w_ms, name="proj_ms")
    mz = _mm(h, w_mz, name="proj_mz")
    glog = _mm(h, w_gl, name="proj_gate")
    o_pre, ya, st0 = _hgrn_fwd(hg, lb_logits, hg_norm_g)
    q, k, v, cqn, ckvn = _mla_pre(ms, q_a_g, kv_a_g, wuq3, wukv3, tabs)
    o_att, yb, lse = _flash_fwd(q, k, v, mz)
    merged, pa, pb, dx2, loss, dfg = _merge_loss(ya, yb, glog, b_gate, x, tgt, fg, wpa, wpb, wout)

    dya, dyb, dglog, dpa, dpb, dx2b, dbg = _merge_bwd(dx2, pa, pb, glog, b_gate, wpa, wpb, wout)
    d_wout = _mm_tn(merged, dx2b, name="dw_out")
    d_wpa = _mm_tn(ya, dpa, name="dw_proj_a")
    d_wpb = _mm_tn(yb, dpb, name="dw_proj_b")
    dhg, dlb, dhgg = _hgrn_bwd(hg, o_pre, dya, st0, lb_logits, hg_norm_g)
    dq, dk, dv, dmz = _flash_bwd(q, k, v, dyb, mz, o_att, lse, tabs)
    dms, d_wuq3, d_wukv3, dqg, dkvg = _mla_bwd_proj(dq, dk, dv, cqn, ckvn, ms, q_a_g, kv_a_g, wuq3, wukv3, tabs)
    d_w_hg = _mm_tn(h, dhg, name="dw_in_hg")
    d_w_ms = _mm_tn(h, dms, name="dw_in_ms")
    d_w_mz = _mm_tn(h, dmz, name="dw_in_mz")
    d_w_gl = _mm_tn(h, dglog, name="dw_in_gate")
    dh = _mm(dhg, w_hg, trans_b=True, name="dh_hg")
    dh = _mm(dms, w_ms, trans_b=True, add=dh, name="dh_ms")
    dh = _mm(dmz, w_mz, trans_b=True, add=dh, name="dh_mz")
    dh = _mm(dglog, w_gl, trans_b=True, add=dh, name="dh_gate")
    grad_x, dng = _norm_in_bwd(x, norm_g, dh, dx2)

    big = {
        "w_in": jnp.concatenate([_hg_unperm(d_w_hg), d_w_ms[:, 0:704], d_w_mz, d_w_gl], axis=1),
        "w_uq": d_wuq3.transpose(1, 0, 2)[:, :, 0:QK_DIM].reshape(Q_LORA, HEADS * QK_DIM),
        "w_ukv": d_wukv3.transpose(1, 0, 2).reshape(KV_LORA, HEADS * 256),
        "w_proj_a": d_wpa, "w_proj_b": d_wpb, "w_out": d_wout,
    }
    small = {"norm_g": dng, "b_gate": dbg, "lb": dlb, "hg_norm_g": dhgg, "q_a_g": dqg, "kv_a_g": dkvg,
             "final_norm_g": dfg}
    return loss, grad_x, big, small


def _pack_block(w_in_b, w_uq_b, w_ukv_b, wpa_b, wpb_b, wout_b):
    return jnp.concatenate([w_in_b.reshape(1968, D_MODEL), w_uq_b.reshape(144, D_MODEL),
                            w_ukv_b.reshape(128, D_MODEL), wpa_b, wpb_b, wout_b], axis=0)


def _unpack_block(p):
    return (p[0:1968].reshape(D_MODEL, W_IN_BLK), p[1968:2112].reshape(Q_LORA, 384),
            p[2112:2240].reshape(KV_LORA, 512), p[2240:2496], p[2496:2752], p[2752:3008])


def _pack_full(big):
    blocks = []
    for b in range(N_CHIPS):
        blocks.append(_pack_block(
            big["w_in"][:, b * W_IN_BLK:(b + 1) * W_IN_BLK], big["w_uq"][:, b * 384:(b + 1) * 384],
            big["w_ukv"][:, b * 512:(b + 1) * 512], big["w_proj_a"][b * 256:(b + 1) * 256],
            big["w_proj_b"][b * 256:(b + 1) * 256], big["w_out"][b * 256:(b + 1) * 256]))
    return jnp.stack(blocks, axis=0)


def _unpack_full(g):
    parts = [_unpack_block(g[b]) for b in range(N_CHIPS)]
    w_in, w_uq, w_ukv = (jnp.concatenate([p[n] for p in parts], axis=1) for n in range(3))
    wpa, wpb, wout = (jnp.concatenate([p[n] for p in parts], axis=0) for n in range(3, 6))
    return w_in, w_uq, w_ukv, wpa, wpb, wout


MESH_ID = pl.DeviceIdType.MESH
ANY = pl.BlockSpec(memory_space=pl.ANY)


def _me():
    return lax.axis_index("x"), lax.axis_index("y"), lax.axis_index("c")


def _other_chips(x, y):
    return [(1 - x, y), (x, 1 - y), (1 - x, 1 - y)]


def _gather_weights(block):
    def body(in_ref, out_ref, send_sems, recv_sems, local_sem):
        x, y, c = _me()
        chips = _other_chips(x, y)
        half = pl.ds(c * HALF_ROWS, HALF_ROWS)
        other_half = pl.ds((1 - c) * HALF_ROWS, HALF_ROWS)

        def copy(k, src, dst, to):
            return pltpu.make_async_remote_copy(src_ref=src, dst_ref=dst, send_sem=send_sems.at[k],
                                                recv_sem=recv_sems.at[k], device_id=to, device_id_type=MESH_ID)

        mine = pltpu.make_async_copy(in_ref, out_ref.at[2 * x + y], local_sem)
        mine.start()
        first = [copy(j, in_ref.at[half, :], out_ref.at[2 * x + y, half, :], (cx, cy, c))
                 for j, (cx, cy) in enumerate(chips)]
        for cp in first:
            cp.start()
        passed = []
        for j, (cx, cy) in enumerate(chips):
            landed = out_ref.at[2 * cx + cy, half, :]
            copy(j, landed, landed, (cx, cy, c)).wait_recv()
            fwd = copy(3 + j, landed, landed, (x, y, 1 - c))
            fwd.start()
            passed.append(fwd)
        for j, (cx, cy) in enumerate(chips):
            theirs = out_ref.at[2 * cx + cy, other_half, :]
            copy(3 + j, theirs, theirs, (x, y, 1 - c)).wait_recv()
        for cp in first + passed:
            cp.wait_send()
        mine.wait()

    return pl.pallas_call(
        body, name="gather_weights", in_specs=[ANY], out_specs=ANY,
        out_shape=jax.ShapeDtypeStruct((N_CHIPS, PACK_ROWS, D_MODEL), block.dtype),
        scratch_shapes=[pltpu.SemaphoreType.DMA((6,)), pltpu.SemaphoreType.DMA((6,)), pltpu.SemaphoreType.DMA],
    )(block)


def _swap_halves(g):
    def body(g_ref, out_ref, send_sem, recv_sem):
        x, y, c = _me()
        cp = pltpu.make_async_remote_copy(
            src_ref=g_ref.at[:, pl.ds((1 - c) * HALF_ROWS, HALF_ROWS), :], dst_ref=out_ref,
            send_sem=send_sem, recv_sem=recv_sem, device_id=(x, y, 1 - c), device_id_type=MESH_ID)
        cp.start()
        cp.wait()

    return pl.pallas_call(
        body, name="grad_swap_halves", in_specs=[ANY], out_specs=ANY,
        out_shape=jax.ShapeDtypeStruct((N_CHIPS, HALF_ROWS, D_MODEL), g.dtype),
        scratch_shapes=[pltpu.SemaphoreType.DMA, pltpu.SemaphoreType.DMA],
    )(g)


def _scatter_blocks(hsum):
    def body(h_ref, out_ref, send_sems, recv_sems):
        x, y, c = _me()
        chips = _other_chips(x, y)
        cps = []
        for j, (cx, cy) in enumerate(chips):
            cps.append(pltpu.make_async_remote_copy(
                src_ref=h_ref.at[2 * cx + cy], dst_ref=out_ref.at[j], send_sem=send_sems.at[j],
                recv_sem=recv_sems.at[j], device_id=(cx, cy, c), device_id_type=MESH_ID))
        for cp in cps:
            cp.start()
        for cp in cps:
            cp.wait()

    return pl.pallas_call(
        body, name="grad_scatter_blocks", in_specs=[ANY], out_specs=ANY,
        out_shape=jax.ShapeDtypeStruct((3, HALF_ROWS, D_MODEL), hsum.dtype),
        scratch_shapes=[pltpu.SemaphoreType.DMA((3,)), pltpu.SemaphoreType.DMA((3,))],
    )(hsum)


def _join_halves(rhalf):
    def body(r_ref, out_ref, send_sem, recv_sem, local_sem):
        x, y, c = _me()
        mine = pltpu.make_async_copy(r_ref, out_ref.at[c], local_sem)
        mine.start()
        cp = pltpu.make_async_remote_copy(src_ref=r_ref, dst_ref=out_ref.at[c], send_sem=send_sem,
                                          recv_sem=recv_sem, device_id=(x, y, 1 - c), device_id_type=MESH_ID)
        cp.start()
        cp.wait()
        mine.wait()

    return pl.pallas_call(
        body, name="grad_join_halves", in_specs=[ANY], out_specs=ANY,
        out_shape=jax.ShapeDtypeStruct((2, HALF_ROWS, D_MODEL), rhalf.dtype),
        scratch_shapes=[pltpu.SemaphoreType.DMA, pltpu.SemaphoreType.DMA, pltpu.SemaphoreType.DMA],
    )(rhalf)


def _gather_small(vec):
    def body(v_ref, out_ref, send_sems, recv_sems, local_sem):
        x, y, c = _me()
        my_id = 4 * x + 2 * y + c
        mine = pltpu.make_async_copy(v_ref, out_ref.at[my_id], local_sem)
        mine.start()
        cps = []
        for r in range(1, N_DEV):
            peer = (x ^ (r >> 2), y ^ ((r >> 1) & 1), c ^ (r & 1))
            cps.append(pltpu.make_async_remote_copy(
                src_ref=v_ref, dst_ref=out_ref.at[my_id], send_sem=send_sems.at[r - 1],
                recv_sem=recv_sems.at[r - 1], device_id=peer, device_id_type=MESH_ID))
        for cp in cps:
            cp.start()
        for cp in cps:
            cp.wait()
        mine.wait()

    return pl.pallas_call(
        body, name="gather_small", in_specs=[ANY], out_specs=ANY,
        out_shape=jax.ShapeDtypeStruct((N_DEV, 1, SMALL_COLS), vec.dtype),
        scratch_shapes=[pltpu.SemaphoreType.DMA((N_DEV - 1,)), pltpu.SemaphoreType.DMA((N_DEV - 1,)),
                        pltpu.SemaphoreType.DMA],
    )(vec)


def _add_rows(terms, *, name):
    r = terms[0].shape[0]
    tm = 376 if r % 376 == 0 else r
    n = len(terms)

    def body(*refs):
        acc = refs[0][...]
        for t in refs[1:n]:
            acc = acc + t[...]
        refs[n][...] = acc

    row = pl.BlockSpec((tm, D_MODEL), lambda i: (i, 0))
    return pl.pallas_call(
        body, name=name, grid=(r // tm,), in_specs=[row] * n, out_specs=row,
        out_shape=jax.ShapeDtypeStruct((r, D_MODEL), F32), compiler_params=_params(1),
    )(*terms)


def _pack_small(small, lb_logits, loss):
    def body(ng_ref, bg_ref, dlb_ref, lbl_ref, hgg_ref, qg_ref, kvg_ref, fg_ref, loss_ref, out_ref):
        out_ref[...] = jnp.zeros_like(out_ref)
        out_ref[:, 0:1024] = ng_ref[...]
        out_ref[:, 1024:3072] = bg_ref[...]
        _, p0p1 = _lower_bound(lbl_ref[...])
        dl0 = dlb_ref[...] * p0p1
        out_ref[:, 3072:4096] = dl0
        out_ref[:, 4096:5120] = -dl0
        hgg = hgg_ref[0]
        for h in range(1, HEADS):
            hgg = hgg + hgg_ref[h]
        out_ref[:, 5120:5248] = hgg
        out_ref[:, 5248:5632] = qg_ref[...]
        out_ref[:, 5632:5888] = kvg_ref[...]
        out_ref[:, 5888:6912] = fg_ref[...]
        out_ref[:, 6912:7040] = loss_ref[...]

    return pl.pallas_call(
        body, name="pack_small", out_shape=jax.ShapeDtypeStruct((1, SMALL_COLS), F32),
    )(small["norm_g"], small["b_gate"], small["lb"], lb_logits, small["hg_norm_g"], small["q_a_g"],
      small["kv_a_g"], small["final_norm_g"], loss)


def _sum_small(gathered):
    def body(g_ref, out_ref):
        acc = g_ref[0]
        for d in range(1, N_DEV):
            acc = acc + g_ref[d]
        out_ref[...] = acc

    return pl.pallas_call(
        body, name="sum_small", out_shape=jax.ShapeDtypeStruct((1, SMALL_COLS), F32),
    )(gathered)


def _adamw(w, g, m, v, *, name):
    r, cols = w.shape
    tm = 376 if r % 376 == 0 else r
    c1 = 1.0 - ADAM_B1 ** ADAM_STEP
    c2 = 1.0 - ADAM_B2 ** ADAM_STEP

    def body(w_ref, g_ref, m_ref, v_ref, d_ref, nm_ref, nv_ref):
        gv = g_ref[...]
        nm = ADAM_B1 * m_ref[...] + (1.0 - ADAM_B1) * gv
        nv = ADAM_B2 * v_ref[...] + (1.0 - ADAM_B2) * (gv * gv)
        nm_ref[...] = nm
        nv_ref[...] = nv
        d_ref[...] = -ADAM_LR * ((nm / c1) / (jnp.sqrt(nv / c2) + ADAM_EPS) + ADAM_WD * w_ref[...])

    row = pl.BlockSpec((tm, cols), lambda i: (i, 0))
    shp = jax.ShapeDtypeStruct((r, cols), F32)
    return pl.pallas_call(
        body, name=name, grid=(r // tm,), in_specs=[row] * 4, out_specs=[row] * 3, out_shape=[shp] * 3,
        compiler_params=_params(1),
    )(w, g, m, v)


def _small_vec(norm_g, b_gate, lb_logits, hg_norm_g, q_a_g, kv_a_g, fg):
    parts = [norm_g.reshape(1, -1), b_gate.reshape(1, -1), lb_logits.reshape(1, -1), hg_norm_g.reshape(1, -1),
             q_a_g.reshape(1, -1), kv_a_g.reshape(1, -1), fg.reshape(1, -1), jnp.zeros((1, SMALL_COLS - 6912), F32)]
    return jnp.concatenate(parts, axis=1)


def _split_small(vec):
    v = vec.reshape(-1)
    return (v[0:1024].reshape(1, 1024), v[1024:3072].reshape(1, 2048), v[3072:5120].reshape(2, 1024),
            v[5120:5248].reshape(1, 128), v[5248:5632].reshape(1, 384), v[5632:5888].reshape(1, 256), v[5888:6912])


def kernel(x, norm_g, w_in, b_gate, lb_logits, hg_norm_g, q_a_g, w_uq, kv_a_g, w_ukv, w_proj_a, w_proj_b, w_out, final_norm_g, loss_target, m_norm_g, m_w_in, m_b_gate, m_lb_logits, m_hg_norm_g, m_q_a_g, m_w_uq, m_kv_a_g, m_w_ukv, m_w_proj_a, m_w_proj_b, m_w_out, m_final_norm_g, v_norm_g, v_w_in, v_b_gate, v_lb_logits, v_hg_norm_g, v_q_a_g, v_w_uq, v_kv_a_g, v_w_ukv, v_w_proj_a, v_w_proj_b, v_w_out, v_final_norm_g):
    c = lax.axis_index("c")
    chip = 2 * lax.axis_index("x") + lax.axis_index("y")

    w_pack = _pack_block(w_in[0], w_uq[0], w_ukv[0], w_proj_a[0], w_proj_b[0], w_out[0])
    gathered = _gather_weights(w_pack.astype(BF16))
    fw_in, fw_uq, fw_ukv, fwpa, fwpb, fwout = _unpack_full(gathered)

    loss, grad_x, big, small = _local_step(
        x[0], loss_target[0], fw_in, fw_uq, fw_ukv, fwpa, fwpb, fwout,
        norm_g, b_gate, lb_logits, hg_norm_g, q_a_g, kv_a_g, final_norm_g.reshape(1, D_MODEL))

    g_full = _pack_full(big)
    from_sibling = _swap_halves(g_full)
    my_half = lax.dynamic_slice_in_dim(g_full, c * HALF_ROWS, HALF_ROWS, axis=1)
    hsum = _add_rows([my_half.reshape(N_CHIPS * HALF_ROWS, D_MODEL), from_sibling.reshape(N_CHIPS * HALF_ROWS, D_MODEL)],
                     name="grad_add_cores").reshape(N_CHIPS, HALF_ROWS, D_MODEL)
    landed = _scatter_blocks(hsum)
    own = lax.dynamic_index_in_dim(hsum, chip, axis=0, keepdims=False)
    rhalf = _add_rows([own, landed[0], landed[1], landed[2]], name="grad_add_chips")
    g_block = _join_halves(rhalf).reshape(PACK_ROWS, D_MODEL)

    small_sum = _sum_small(_gather_small(_pack_small(small, lb_logits, loss)))

    m_pack = _pack_block(m_w_in[0], m_w_uq[0], m_w_ukv[0], m_w_proj_a[0], m_w_proj_b[0], m_w_out[0])
    v_pack = _pack_block(v_w_in[0], v_w_uq[0], v_w_ukv[0], v_w_proj_a[0], v_w_proj_b[0], v_w_out[0])
    d_pack, nm_pack, nv_pack = _adamw(w_pack, g_block, m_pack, v_pack, name="adamw_big")
    ws = _small_vec(norm_g, b_gate, lb_logits, hg_norm_g, q_a_g, kv_a_g, final_norm_g).reshape(7, 1024)
    ms_ = _small_vec(m_norm_g, m_b_gate, m_lb_logits, m_hg_norm_g, m_q_a_g, m_kv_a_g, m_final_norm_g).reshape(7, 1024)
    vs = _small_vec(v_norm_g, v_b_gate, v_lb_logits, v_hg_norm_g, v_q_a_g, v_kv_a_g, v_final_norm_g).reshape(7, 1024)
    d_small, nm_small, nv_small = _adamw(ws, small_sum.reshape(7, 1024), ms_, vs, name="adamw_small")

    def outputs(pack, vec):
        b_in, b_uq, b_ukv, b_pa, b_pb, b_out = _unpack_block(pack)
        s_ng, s_bg, s_lb, s_hg, s_qg, s_kvg, s_fg = _split_small(vec)
        return (s_ng, b_in[None], s_bg, s_lb, s_hg, s_qg, b_uq[None], s_kvg, b_ukv[None], b_pa[None], b_pb[None],
                b_out[None], s_fg)

    total_loss = small_sum[0, 6912]
    return (total_loss, grad_x[None], *outputs(g_block, small_sum), *outputs(d_pack, d_small),
            *outputs(nm_pack, nm_small), *outputs(nv_pack, nv_small))
```

```python
import functools

import jax
import jax.numpy as jnp
from jax import lax
from jax.experimental import pallas as pl
from jax.experimental.pallas import tpu as pltpu

F32 = jnp.float32
BF16 = jnp.bfloat16

D_MODEL = 1024
HEADS = 8
HEAD_DIM = 128
HG_CHUNK = 32
CHUNK_SHIFT = 5
HEAD_SHIFT = 7
QK_NOPE = 128
QK_ROPE = 64
QK_DIM = QK_NOPE + QK_ROPE
QK_PAD = 256
Q_LORA = 384
KV_LORA = 256
MS_COLS = 768
ROPE_THETA = 10000.0
EPS = 1e-6
ATT_SCALE = QK_DIM ** -0.5
LOG2E = 1.4426950408889634
LN2 = 0.6931471805599453
Q_PRESCALE = ATT_SCALE * LOG2E

ADAM_LR = 0.001
ADAM_B1 = 0.9
ADAM_B2 = 0.999
ADAM_EPS = 1e-08
ADAM_WD = 0.01
ADAM_STEP = 10

N_CHIPS = 4
N_DEV = 8
W_IN_COLS = 7872
W_IN_BLK = W_IN_COLS // N_CHIPS
PACK_ROWS = 1968 + 144 + 128 + 3 * 256
HALF_ROWS = PACK_ROWS // 2
SMALL_COLS = 7168

TM_MM = 512
TM_FUSED = 256
HG_ROWS = 256
TQ = 512
VMEM_LIMIT = 56 * 1024 * 1024


def _dot(a, b):
    return lax.dot_general(a, b, (((1,), (0,)), ((), ())), preferred_element_type=F32)


def _dot_nt(a, b):
    return lax.dot_general(a, b, (((1,), (1,)), ((), ())), preferred_element_type=F32)


def _dot_tn(a, b):
    return lax.dot_general(a, b, (((0,), (0,)), ((), ())), preferred_element_type=F32)


def _params(n_axes):
    return pltpu.CompilerParams(dimension_semantics=("arbitrary",) * n_axes, vmem_limit_bytes=VMEM_LIMIT)


def _rms(x, g):
    r = lax.rsqrt(jnp.mean(x * x, axis=-1, keepdims=True) + EPS)
    return x * r * g


def _rms_bwd(x, g, dy):
    r = lax.rsqrt(jnp.mean(x * x, axis=-1, keepdims=True) + EPS)
    xh = x * r
    dyg = dy * g
    dx = r * (dyg - xh * jnp.mean(dyg * xh, axis=-1, keepdims=True))
    return dx, dy * xh


def _silu_parts(z):
    s = jax.nn.sigmoid(z)
    return z * s, s * (1.0 + z * (1.0 - s))


def _rope(x, c, sa, sb):
    return x * c + pltpu.roll(x, 32, 1) * sa + pltpu.roll(x, 96, 1) * sb


def _rope_bwd(dy, c, sa, sb):
    return dy * c + pltpu.roll(dy * sa, 96, 1) + pltpu.roll(dy * sb, 32, 1)


def _rope_tables(seq):
    inv = ROPE_THETA ** (-jnp.arange(0, QK_ROPE, 2, dtype=F32) / QK_ROPE)
    ang = jnp.arange(seq, dtype=F32)[:, None] * inv[None, :]
    cos, sin = jnp.cos(ang), jnp.sin(ang)
    z32 = jnp.zeros_like(cos)
    z64 = jnp.zeros((seq, 64), F32)
    c = jnp.concatenate([cos, cos, z64], axis=1)
    sa = jnp.concatenate([z32, sin, z64], axis=1)
    sb = jnp.concatenate([-sin, z32, z64], axis=1)
    return c, sa, sb


def _mm(a, b, *, name, trans_b=False, add=None, out_dtype=F32, tm=TM_MM, tn=1024, tk=1024):
    m, k = a.shape
    n = b.shape[0] if trans_b else b.shape[1]
    tm, tn, tk = min(tm, m), min(tn, n), min(tk, k)
    assert m % tm == 0 and n % tn == 0 and k % tk == 0
    nk = k // tk
    has_add = add is not None

    def body(*refs):
        if has_add:
            a_ref, b_ref, add_ref, o_ref, acc_ref = refs
        else:
            a_ref, b_ref, o_ref, acc_ref = refs
        kk = pl.program_id(2)

        @pl.when(kk == 0)
        def _():
            acc_ref[...] = add_ref[...] if has_add else jnp.zeros_like(acc_ref)

        if trans_b:
            acc_ref[...] += _dot_nt(a_ref[...], b_ref[...])
        else:
            acc_ref[...] += _dot(a_ref[...], b_ref[...])

        @pl.when(kk == nk - 1)
        def _():
            o_ref[...] = acc_ref[...].astype(out_dtype)

    in_specs = [pl.BlockSpec((tm, tk), lambda i, j, kk: (i, kk))]
    if trans_b:
        in_specs.append(pl.BlockSpec((tn, tk), lambda i, j, kk: (j, kk)))
    else:
        in_specs.append(pl.BlockSpec((tk, tn), lambda i, j, kk: (kk, j)))
    args = [a, b]
    if has_add:
        in_specs.append(pl.BlockSpec((tm, tn), lambda i, j, kk: (i, j)))
        args.append(add)
    return pl.pallas_call(
        body, name=name, grid=(m // tm, n // tn, nk),
        in_specs=in_specs, out_specs=pl.BlockSpec((tm, tn), lambda i, j, kk: (i, j)),
        out_shape=jax.ShapeDtypeStruct((m, n), out_dtype),
        scratch_shapes=[pltpu.VMEM((tm, tn), F32)], compiler_params=_params(3),
    )(*args)


def _mm_tn(a, b, *, name, tm=TM_MM, tn=1024):
    m, k = a.shape
    n = b.shape[1]
    tm, tn = min(tm, m), min(tn, n)
    assert m % tm == 0 and n % tn == 0

    def body(a_ref, b_ref, o_ref):
        @pl.when(pl.program_id(1) == 0)
        def _():
            o_ref[...] = jnp.zeros_like(o_ref)

        o_ref[...] += _dot_tn(a_ref[...], b_ref[...])

    return pl.pallas_call(
        body, name=name, grid=(n // tn, m // tm),
        in_specs=[pl.BlockSpec((tm, k), lambda j, i: (i, 0)), pl.BlockSpec((tm, tn), lambda j, i: (i, j))],
        out_specs=pl.BlockSpec((k, tn), lambda j, i: (0, j)),
        out_shape=jax.ShapeDtypeStruct((k, n), F32), compiler_params=_params(2),
    )(a, b)


def _norm_in(x, g):
    s = x.shape[0]
    tm = min(TM_MM, s)

    def body(x_ref, g_ref, h_ref):
        h_ref[...] = _rms(x_ref[...], g_ref[...]).astype(BF16)

    return pl.pallas_call(
        body, name="norm_in", grid=(s // tm,),
        in_specs=[pl.BlockSpec((tm, D_MODEL), lambda i: (i, 0)), pl.BlockSpec((1, D_MODEL), lambda i: (0, 0))],
        out_specs=pl.BlockSpec((tm, D_MODEL), lambda i: (i, 0)),
        out_shape=jax.ShapeDtypeStruct((s, D_MODEL), BF16), compiler_params=_params(1),
    )(x, g)


def _norm_in_bwd(x, g, dh, dx2):
    s = x.shape[0]
    tm = min(TM_MM, s)

    def body(x_ref, g_ref, dh_ref, dx2_ref, dx_ref, dg_ref):
        @pl.when(pl.program_id(0) == 0)
        def _():
            dg_ref[...] = jnp.zeros_like(dg_ref)

        dx, dg_rows = _rms_bwd(x_ref[...], g_ref[...], dh_ref[...])
        dx_ref[...] = dx + dx2_ref[...]
        dg_ref[...] += jnp.sum(dg_rows, axis=0, keepdims=True)

    row = pl.BlockSpec((tm, D_MODEL), lambda i: (i, 0))
    vec = pl.BlockSpec((1, D_MODEL), lambda i: (0, 0))
    return pl.pallas_call(
        body, name="norm_in_bwd", grid=(s // tm,),
        in_specs=[row, vec, row, row], out_specs=[row, vec],
        out_shape=[jax.ShapeDtypeStruct((s, D_MODEL), F32), jax.ShapeDtypeStruct((1, D_MODEL), F32)],
        compiler_params=_params(1),
    )(x, g, dh, dx2)


def _chunk_rows(rows):
    return lax.broadcasted_iota(jnp.int32, (rows, HEAD_DIM), 0) & (HG_CHUNK - 1)


def _chunk_cumsum(x, rows):
    pos = _chunk_rows(rows)
    shift = 1
    while shift < HG_CHUNK:
        x = x + jnp.where(pos >= shift, pltpu.roll(x, shift, 0), 0.0)
        shift *= 2
    return x


def _chunk_revcumsum(x, rows):
    pos = _chunk_rows(rows)
    shift = 1
    while shift < HG_CHUNK:
        x = x + jnp.where(pos + shift < HG_CHUNK, pltpu.roll(x, rows - shift, 0), 0.0)
        shift *= 2
    return x


def _lower_bound(lbl):
    mx = jnp.maximum(lbl[0:1, :], lbl[1:2, :])
    e0 = jnp.exp(lbl[0:1, :] - mx)
    e1 = jnp.exp(lbl[1:2, :] - mx)
    p0 = e0 / (e0 + e1)
    return p0, p0 * (e1 / (e0 + e1))


def _hg_masks(rows, nch, tmask_s, bdmask_s):
    r = lax.broadcasted_iota(jnp.int32, (rows, rows), 0)
    c = lax.broadcasted_iota(jnp.int32, (rows, rows), 1)
    tmask_s[...] = jnp.where(((r >> CHUNK_SHIFT) == (c >> CHUNK_SHIFT)) & (r >= c), 1.0, 0.0)
    r = lax.broadcasted_iota(jnp.int32, (rows, nch * HEAD_DIM), 0)
    c = lax.broadcasted_iota(jnp.int32, (rows, nch * HEAD_DIM), 1)
    bdmask_s[...] = jnp.where((r >> CHUNK_SHIFT) == (c >> HEAD_SHIFT), 1.0, 0.0).astype(BF16)


def _block_diag(x, nch, bdmask):
    return jnp.tile(x, (1, nch)) * bdmask


def _hgrn_fwd(hg, lb_logits, norm_g):
    s = hg.shape[0]
    rows = min(HG_ROWS, s)
    nblk = s // rows
    nch = rows // HG_CHUNK

    def body(hg_ref, lbl_ref, g_ref, o_ref, ya_ref, st0_ref, st_s, stall_s, tmask_s, bdmask_s):
        @pl.when(pl.program_id(1) == 0)
        def _():
            st_s[...] = jnp.zeros_like(st_s)
            _hg_masks(rows, nch, tmask_s, bdmask_s)

        hq = hg_ref[:, 0:128]
        hf = hg_ref[:, 128:256]
        hi = hg_ref[:, 256:384]
        hz = hg_ref[:, 384:512]
        lb, _ = _lower_bound(lbl_ref[...])
        f = lb + (1.0 - lb) * jax.nn.sigmoid(hf)
        q = hq * jax.nn.sigmoid(hq)
        k = 1.0 - f
        logf = jnp.log(f)
        b = _chunk_cumsum(logf, rows)
        q_in = (q * jnp.exp(b)).astype(BF16)
        k_in = (k * jnp.exp(-b)).astype(BF16)
        k_out = (k * jnp.exp(_chunk_revcumsum(logf, rows) - logf)).astype(BF16)
        vb = hi.astype(BF16)
        bdmask = bdmask_s[...]

        sc = jnp.where(tmask_s[...] > 0.5, _dot_nt(q_in, k_in), 0.0)
        o_intra = _dot(sc.astype(BF16), vb)
        kvt = _dot_tn(vb, _block_diag(k_out, nch, bdmask))
        st = st_s[...]
        st0_ref[...] = st
        for c in range(nch):
            cols = slice(c * HEAD_DIM, (c + 1) * HEAD_DIM)
            last = (c + 1) * HG_CHUNK - 1
            stall_s[:, cols] = st.astype(BF16)
            st = st * jnp.exp(b[last:last + 1, :]) + kvt[:, cols]
        st_s[...] = st
        o = o_intra + _dot_nt(_block_diag(q_in, nch, bdmask), stall_s[...])
        o_ref[...] = o
        silu_z, _ = _silu_parts(hz)
        ya_ref[...] = (_rms(o, g_ref[...]) * silu_z).astype(BF16)

    return pl.pallas_call(
        body, name="hgrn_fwd", grid=(HEADS, nblk),
        in_specs=[pl.BlockSpec((rows, 512), lambda h, i: (i, h)),
                  pl.BlockSpec((2, HEAD_DIM), lambda h, i: (0, h)),
                  pl.BlockSpec((1, HEAD_DIM), lambda h, i: (0, 0))],
        out_specs=[pl.BlockSpec((rows, HEAD_DIM), lambda h, i: (i, h)),
                   pl.BlockSpec((rows, HEAD_DIM), lambda h, i: (i, h)),
                   pl.BlockSpec((None, None, HEAD_DIM, HEAD_DIM), lambda h, i: (h, i, 0, 0))],
        out_shape=[jax.ShapeDtypeStruct((s, D_MODEL), F32), jax.ShapeDtypeStruct((s, D_MODEL), BF16),
                   jax.ShapeDtypeStruct((HEADS, nblk, HEAD_DIM, HEAD_DIM), F32)],
        scratch_shapes=[pltpu.VMEM((HEAD_DIM, HEAD_DIM), F32), pltpu.VMEM((HEAD_DIM, nch * HEAD_DIM), BF16),
                        pltpu.VMEM((rows, rows), F32), pltpu.VMEM((rows, nch * HEAD_DIM), BF16)],
        compiler_params=_params(2),
    )(hg, lb_logits, norm_g)


def _hgrn_bwd(hg, o_pre, dya, st0, lb_logits, norm_g):
    s = hg.shape[0]
    rows = min(HG_ROWS, s)
    nblk = s // rows
    nch = rows // HG_CHUNK

    def body(hg_ref, o_ref, dya_ref, st0_ref, lbl_ref, g_ref, dhg_ref, dlb_ref, dg_ref,
             dst_s, stp_s, stp_rows_s, dst_rows_s, dst_lane_s, dbl_s, tmask_s, bdmask_s):
        @pl.when(pl.program_id(1) == 0)
        def _():
            dst_s[...] = jnp.zeros_like(dst_s)
            dlb_ref[...] = jnp.zeros_like(dlb_ref)
            dg_ref[...] = jnp.zeros_like(dg_ref)
            _hg_masks(rows, nch, tmask_s, bdmask_s)

        hq = hg_ref[:, 0:128]
        hf = hg_ref[:, 128:256]
        hi = hg_ref[:, 256:384]
        hz = hg_ref[:, 384:512]
        lb, _ = _lower_bound(lbl_ref[...])
        sg = jax.nn.sigmoid(hf)
        f = lb + (1.0 - lb) * sg
        q, dsilu_q = _silu_parts(hq)
        k = 1.0 - f
        logf = jnp.log(f)
        b = _chunk_cumsum(logf, rows)
        eb = jnp.exp(b)
        enb = jnp.exp(-b)
        ebl = jnp.exp(_chunk_revcumsum(logf, rows) - logf)
        q_in32 = q * eb
        k_in32 = k * enb
        k_out32 = k * ebl
        q_in = q_in32.astype(BF16)
        k_in = k_in32.astype(BF16)
        k_out = k_out32.astype(BF16)
        vb = hi.astype(BF16)
        bdmask = bdmask_s[...]
        tmask = tmask_s[...] > 0.5
        kbd = _block_diag(k_out, nch, bdmask)
        qbd = _block_diag(q_in, nch, bdmask)
        decs = [jnp.exp(b[(c + 1) * HG_CHUNK - 1:(c + 1) * HG_CHUNK, :]) for c in range(nch)]

        kvt = _dot_tn(vb, kbd)
        st = st0_ref[...]
        for c in range(nch):
            stp_s[c] = st
            stp_rows_s[c * HEAD_DIM:(c + 1) * HEAD_DIM, :] = st.astype(BF16)
            st = st * decs[c] + kvt[:, c * HEAD_DIM:(c + 1) * HEAD_DIM]

        g = g_ref[...]
        o = o_ref[...]
        rstd = lax.rsqrt(jnp.mean(o * o, axis=-1, keepdims=True) + EPS)
        oh = o * rstd
        silu_z, dsilu_z = _silu_parts(hz)
        dya_v = dya_ref[...]
        dn = dya_v * silu_z
        dhz = dya_v * (oh * g) * dsilu_z
        dg_ref[...] += jnp.sum(dn * oh, axis=0, keepdims=True)
        doh = dn * g
        do = (rstd * (doh - oh * jnp.mean(doh * oh, axis=-1, keepdims=True))).astype(BF16)

        dq_all = _dot_tn(do, qbd)
        dst = dst_s[...]
        ddecs = [None] * nch
        for c in reversed(range(nch)):
            dstb = dst.astype(BF16)
            dst_lane_s[:, c * HEAD_DIM:(c + 1) * HEAD_DIM] = dstb
            dst_rows_s[c * HEAD_DIM:(c + 1) * HEAD_DIM, :] = dstb
            ddecs[c] = jnp.sum(dst * stp_s[c], axis=0, keepdims=True) * decs[c]
            dst = dst * decs[c] + dq_all[:, c * HEAD_DIM:(c + 1) * HEAD_DIM]
        dst_s[...] = dst

        sc = jnp.where(tmask, _dot_nt(q_in, k_in), 0.0).astype(BF16)
        dkout = _dot(_block_diag(vb, nch, bdmask), dst_rows_s[...])
        dv = _dot_nt(kbd, dst_lane_s[...]) + _dot_tn(sc, do)
        dsc = jnp.where(tmask, _dot_nt(do, vb), 0.0).astype(BF16)
        dqin = _dot(dsc, k_in) + _dot(_block_diag(do, nch, bdmask), stp_rows_s[...])
        dkin = _dot_tn(dsc, q_in)

        dko = dkout * k_out32
        for c in range(nch):
            sl = slice(c * HG_CHUNK, (c + 1) * HG_CHUNK)
            dbl = jnp.sum(dko[sl], axis=0, keepdims=True) + ddecs[c]
            dbl_s[sl, :] = jnp.broadcast_to(dbl, (HG_CHUNK, HEAD_DIM))
        dq = dqin * eb
        dk = dkin * enb + dkout * ebl
        db = dqin * q_in32 - dkin * k_in32 - dko
        dlogf = _chunk_revcumsum(db, rows) + dbl_s[...]
        df = dlogf / f - dk
        dlb_ref[...] += jnp.sum(df * (1.0 - sg), axis=0, keepdims=True)
        dhg_ref[:, 0:128] = (dq * dsilu_q).astype(BF16)
        dhg_ref[:, 128:256] = (df * (1.0 - lb) * sg * (1.0 - sg)).astype(BF16)
        dhg_ref[:, 256:384] = dv.astype(BF16)
        dhg_ref[:, 384:512] = dhz.astype(BF16)

    last = nblk - 1
    return pl.pallas_call(
        body, name="hgrn_bwd", grid=(HEADS, nblk),
        in_specs=[pl.BlockSpec((rows, 512), lambda h, i: (last - i, h)),
                  pl.BlockSpec((rows, HEAD_DIM), lambda h, i: (last - i, h)),
                  pl.BlockSpec((rows, HEAD_DIM), lambda h, i: (last - i, h)),
                  pl.BlockSpec((None, None, HEAD_DIM, HEAD_DIM), lambda h, i: (h, last - i, 0, 0)),
                  pl.BlockSpec((2, HEAD_DIM), lambda h, i: (0, h)),
                  pl.BlockSpec((1, HEAD_DIM), lambda h, i: (0, 0))],
        out_specs=[pl.BlockSpec((rows, 512), lambda h, i: (last - i, h)),
                   pl.BlockSpec((1, HEAD_DIM), lambda h, i: (0, h)),
                   pl.BlockSpec((None, 1, HEAD_DIM), lambda h, i: (h, 0, 0))],
        out_shape=[jax.ShapeDtypeStruct((s, 4 * D_MODEL), BF16), jax.ShapeDtypeStruct((1, D_MODEL), F32),
                   jax.ShapeDtypeStruct((HEADS, 1, HEAD_DIM), F32)],
        scratch_shapes=[pltpu.VMEM((HEAD_DIM, HEAD_DIM), F32), pltpu.VMEM((nch, HEAD_DIM, HEAD_DIM), F32),
                        pltpu.VMEM((nch * HEAD_DIM, HEAD_DIM), BF16), pltpu.VMEM((nch * HEAD_DIM, HEAD_DIM), BF16),
                        pltpu.VMEM((HEAD_DIM, nch * HEAD_DIM), BF16), pltpu.VMEM((rows, HEAD_DIM), F32),
                        pltpu.VMEM((rows, rows), F32), pltpu.VMEM((rows, nch * HEAD_DIM), BF16)],
        compiler_params=_params(2),
    )(hg, o_pre, dya, st0, lb_logits, norm_g)


def _mla_pre(ms, q_a_g, kv_a_g, wuq3, wukv3, tabs):
    s = ms.shape[0]
    tm = min(TM_FUSED, s)

    def body(ms_ref, qg_ref, kvg_ref, wuq_ref, wukv_ref, c_ref, sa_ref, sb_ref,
             q_ref, k_ref, v_ref, cqn_ref, ckvn_ref):
        c, sa, sb = c_ref[...], sa_ref[...], sb_ref[...]
        cqn = _rms(ms_ref[:, 0:Q_LORA], qg_ref[...]).astype(BF16)
        ckvn = _rms(ms_ref[:, Q_LORA:Q_LORA + KV_LORA], kvg_ref[...]).astype(BF16)
        cqn_ref[...] = cqn
        ckvn_ref[...] = ckvn
        k_pe = _rope(ms_ref[:, Q_LORA + KV_LORA:MS_COLS], c, sa, sb).astype(BF16)
        for h in range(HEADS):
            qh = _dot(cqn, wuq_ref[h])
            q_ref[h, :, 0:128] = (qh[:, 0:128] * Q_PRESCALE).astype(BF16)
            q_ref[h, :, 128:256] = (_rope(qh[:, 128:256], c, sa, sb) * Q_PRESCALE).astype(BF16)
            kvh = _dot(ckvn, wukv_ref[h])
            k_ref[h, :, 0:128] = kvh[:, 0:128].astype(BF16)
            k_ref[h, :, 128:256] = k_pe
            v_ref[h] = kvh[:, 128:256].astype(BF16)

    tab = pl.BlockSpec((tm, 128), lambda i: (i, 0))
    return pl.pallas_call(
        body, name="mla_pre", grid=(s // tm,),
        in_specs=[pl.BlockSpec((tm, MS_COLS), lambda i: (i, 0)),
                  pl.BlockSpec((1, Q_LORA), lambda i: (0, 0)), pl.BlockSpec((1, KV_LORA), lambda i: (0, 0)),
                  pl.BlockSpec((HEADS, Q_LORA, QK_PAD), lambda i: (0, 0, 0)),
                  pl.BlockSpec((HEADS, KV_LORA, 256), lambda i: (0, 0, 0)), tab, tab, tab],
        out_specs=[pl.BlockSpec((HEADS, tm, QK_PAD), lambda i: (0, i, 0)),
                   pl.BlockSpec((HEADS, tm, QK_PAD), lambda i: (0, i, 0)),
                   pl.BlockSpec((HEADS, tm, HEAD_DIM), lambda i: (0, i, 0)),
                   pl.BlockSpec((tm, Q_LORA), lambda i: (i, 0)), pl.BlockSpec((tm, KV_LORA), lambda i: (i, 0))],
        out_shape=[jax.ShapeDtypeStruct((HEADS, s, QK_PAD), BF16), jax.ShapeDtypeStruct((HEADS, s, QK_PAD), BF16),
                   jax.ShapeDtypeStruct((HEADS, s, HEAD_DIM), BF16),
                   jax.ShapeDtypeStruct((s, Q_LORA), BF16), jax.ShapeDtypeStruct((s, KV_LORA), BF16)],
        compiler_params=_params(1),
    )(ms, q_a_g, kv_a_g, wuq3, wukv3, *tabs)


def _causal_mask(t):
    r = lax.broadcasted_iota(jnp.int32, (t, t), 0)
    c = lax.broadcasted_iota(jnp.int32, (t, t), 1)
    return r >= c


def _flash_fwd(q, k, v, mz):
    s = q.shape[1]
    t = min(TQ, s)

    def body(q_ref, k_ref, v_ref, mz_ref, o_ref, yb_ref, lse_ref, m_s, l_s, acc_s):
        i = pl.program_id(1)
        qv = q_ref[...]
        m_s[...] = jnp.full_like(m_s, -jnp.inf)
        l_s[...] = jnp.zeros_like(l_s)
        acc_s[...] = jnp.zeros_like(acc_s)

        def step(j, masked):
            rows = pl.ds(pl.multiple_of(j * t, t), t)
            sc = _dot_nt(qv, k_ref[rows, :])
            if masked:
                sc = jnp.where(_causal_mask(t), sc, -jnp.inf)
            m_prev = m_s[...]
            m_new = jnp.maximum(m_prev, jnp.max(sc, axis=-1, keepdims=True))
            p = jnp.exp2(sc - jnp.tile(m_new, (1, t // 128)))
            alpha = jnp.exp2(m_prev - m_new)
            l_s[...] = alpha * l_s[...] + jnp.sum(p, axis=-1, keepdims=True)
            acc_s[...] = alpha * acc_s[...] + _dot(p.astype(BF16), v_ref[rows, :])
            m_s[...] = m_new

        def loop_body(j, carry):
            step(j, False)
            return carry

        lax.fori_loop(0, i, loop_body, 0)
        step(i, True)
        out = acc_s[...] / l_s[...]
        o_ref[...] = out
        silu_z, _ = _silu_parts(mz_ref[...])
        yb_ref[...] = (out * silu_z).astype(BF16)
        lse_ref[...] = m_s[...] + jnp.log2(l_s[...])

    col = pl.BlockSpec((t, HEAD_DIM), lambda h, i: (i, h))
    return pl.pallas_call(
        body, name="flash_fwd", grid=(HEADS, s // t),
        in_specs=[pl.BlockSpec((None, t, QK_PAD), lambda h, i: (h, i, 0)),
                  pl.BlockSpec((None, s, QK_PAD), lambda h, i: (h, 0, 0)),
                  pl.BlockSpec((None, s, HEAD_DIM), lambda h, i: (h, 0, 0)), col],
        out_specs=[col, col, pl.BlockSpec((None, t, 128), lambda h, i: (h, i, 0))],
        out_shape=[jax.ShapeDtypeStruct((s, D_MODEL), F32), jax.ShapeDtypeStruct((s, D_MODEL), BF16),
                   jax.ShapeDtypeStruct((HEADS, s, 128), F32)],
        scratch_shapes=[pltpu.VMEM((t, 128), F32), pltpu.VMEM((t, 128), F32), pltpu.VMEM((t, HEAD_DIM), F32)],
        compiler_params=_params(2),
    )(q, k, v, mz)


def _flash_bwd(q, k, v, dyb, mz, o_att, lse, tabs):
    s = q.shape[1]
    t = min(TQ, s)

    def body(q_ref, k_ref, v_ref, dyb_ref, mz_ref, o_ref, lse_ref, c_ref, sa_ref, sb_ref,
             dq_ref, dk_ref, dv_ref, dmz_ref, dq_s, delta_s):
        i = pl.program_id(1)

        @pl.when(i == 0)
        def _():
            dk_ref[...] = jnp.zeros_like(dk_ref)
            dv_ref[...] = jnp.zeros_like(dv_ref)

        silu_z, dsilu_z = _silu_parts(mz_ref[...])
        dyb_v = dyb_ref[...]
        out = o_ref[...]
        do32 = dyb_v * silu_z
        dmz_ref[...] = (dyb_v * out * dsilu_z).astype(BF16)
        delta_s[...] = jnp.broadcast_to(jnp.sum(do32 * out, axis=-1, keepdims=True), (t, 128))
        do = do32.astype(BF16)
        qv = q_ref[...]
        dq_s[...] = jnp.zeros_like(dq_s)

        def step(j, masked):
            rows = pl.ds(pl.multiple_of(j * t, t), t)
            kj = k_ref[rows, :]
            vj = v_ref[rows, :]
            sc = _dot_nt(qv, kj)
            if masked:
                sc = jnp.where(_causal_mask(t), sc, -jnp.inf)
            p = jnp.exp2(sc - jnp.tile(lse_ref[...], (1, t // 128)))
            dp = _dot_nt(do, vj)
            ds = (p * (dp - jnp.tile(delta_s[...], (1, t // 128)))).astype(BF16)
            dv_ref[rows, :] += _dot_tn(p.astype(BF16), do)
            dk_ref[rows, :] += _dot_tn(ds, qv)
            dq_s[...] += _dot(ds, kj)

        def loop_body(j, carry):
            step(j, False)
            return carry

        lax.fori_loop(0, i, loop_body, 0)
        step(i, True)
        dq = dq_s[...] * ATT_SCALE
        dq_ref[:, 0:128] = dq[:, 0:128].astype(BF16)
        dq_ref[:, 128:256] = _rope_bwd(dq[:, 128:256], c_ref[...], sa_ref[...], sb_ref[...]).astype(BF16)

    col = pl.BlockSpec((t, HEAD_DIM), lambda h, i: (i, h))
    tab = pl.BlockSpec((t, 128), lambda h, i: (i, 0))
    return pl.pallas_call(
        body, name="flash_bwd", grid=(HEADS, s // t),
        in_specs=[pl.BlockSpec((None, t, QK_PAD), lambda h, i: (h, i, 0)),
                  pl.BlockSpec((None, s, QK_PAD), lambda h, i: (h, 0, 0)),
                  pl.BlockSpec((None, s, HEAD_DIM), lambda h, i: (h, 0, 0)),
                  col, col, col, pl.BlockSpec((None, t, 128), lambda h, i: (h, i, 0)), tab, tab, tab],
        out_specs=[pl.BlockSpec((None, t, QK_PAD), lambda h, i: (h, i, 0)),
                   pl.BlockSpec((None, s, QK_PAD), lambda h, i: (h, 0, 0)),
                   pl.BlockSpec((None, s, HEAD_DIM), lambda h, i: (h, 0, 0)), col],
        out_shape=[jax.ShapeDtypeStruct((HEADS, s, QK_PAD), BF16), jax.ShapeDtypeStruct((HEADS, s, QK_PAD), F32),
                   jax.ShapeDtypeStruct((HEADS, s, HEAD_DIM), F32), jax.ShapeDtypeStruct((s, D_MODEL), BF16)],
        scratch_shapes=[pltpu.VMEM((t, QK_PAD), F32), pltpu.VMEM((t, 128), F32)], compiler_params=_params(2),
    )(q, k, v, dyb, mz, o_att, lse, *tabs)


def _mla_bwd_proj(dq, dk, dv, cqn, ckvn, ms, q_a_g, kv_a_g, wuq3, wukv3, tabs):
    s = ms.shape[0]
    tm = min(TM_FUSED, s)

    def body(dq_ref, dk_ref, dv_ref, cqn_ref, ckvn_ref, ms_ref, qg_ref, kvg_ref, wuq_ref, wukv_ref,
             c_ref, sa_ref, sb_ref, dms_ref, dwuq_ref, dwukv_ref, dqg_ref, dkvg_ref):
        @pl.when(pl.program_id(0) == 0)
        def _():
            dwuq_ref[...] = jnp.zeros_like(dwuq_ref)
            dwukv_ref[...] = jnp.zeros_like(dwukv_ref)
            dqg_ref[...] = jnp.zeros_like(dqg_ref)
            dkvg_ref[...] = jnp.zeros_like(dkvg_ref)

        cqn = cqn_ref[...]
        ckvn = ckvn_ref[...]
        dcqn = jnp.zeros((tm, Q_LORA), F32)
        dckvn = jnp.zeros((tm, KV_LORA), F32)
        dkpe = jnp.zeros((tm, 128), F32)
        for h in range(HEADS):
            dqh = dq_ref[h]
            dcqn += _dot_nt(dqh, wuq_ref[h])
            dwuq_ref[h] += _dot_tn(cqn, dqh)
            dkh = dk_ref[h] * LN2
            dkvh = jnp.concatenate([dkh[:, 0:128], dv_ref[h]], axis=1).astype(BF16)
            dckvn += _dot_nt(dkvh, wukv_ref[h])
            dwukv_ref[h] += _dot_tn(ckvn, dkvh)
            dkpe += dkh[:, 128:256]
        dcq, dqg_rows = _rms_bwd(ms_ref[:, 0:Q_LORA], qg_ref[...], dcqn)
        dckv, dkvg_rows = _rms_bwd(ms_ref[:, Q_LORA:Q_LORA + KV_LORA], kvg_ref[...], dckvn)
        dqg_ref[...] += jnp.sum(dqg_rows, axis=0, keepdims=True)
        dkvg_ref[...] += jnp.sum(dkvg_rows, axis=0, keepdims=True)
        dms_ref[:, 0:Q_LORA] = dcq.astype(BF16)
        dms_ref[:, Q_LORA:Q_LORA + KV_LORA] = dckv.astype(BF16)
        dms_ref[:, Q_LORA + KV_LORA:MS_COLS] = _rope_bwd(dkpe, c_ref[...], sa_ref[...], sb_ref[...]).astype(BF16)

    tab = pl.BlockSpec((tm, 128), lambda i: (i, 0))
    wq = pl.BlockSpec((HEADS, Q_LORA, QK_PAD), lambda i: (0, 0, 0))
    wkv = pl.BlockSpec((HEADS, KV_LORA, 256), lambda i: (0, 0, 0))
    qg = pl.BlockSpec((1, Q_LORA), lambda i: (0, 0))
    kvg = pl.BlockSpec((1, KV_LORA), lambda i: (0, 0))
    return pl.pallas_call(
        body, name="mla_bwd_proj", grid=(s // tm,),
        in_specs=[pl.BlockSpec((HEADS, tm, QK_PAD), lambda i: (0, i, 0)),
                  pl.BlockSpec((HEADS, tm, QK_PAD), lambda i: (0, i, 0)),
                  pl.BlockSpec((HEADS, tm, HEAD_DIM), lambda i: (0, i, 0)),
                  pl.BlockSpec((tm, Q_LORA), lambda i: (i, 0)), pl.BlockSpec((tm, KV_LORA), lambda i: (i, 0)),
                  pl.BlockSpec((tm, MS_COLS), lambda i: (i, 0)), qg, kvg, wq, wkv, tab, tab, tab],
        out_specs=[pl.BlockSpec((tm, MS_COLS), lambda i: (i, 0)), wq, wkv, qg, kvg],
        out_shape=[jax.ShapeDtypeStruct((s, MS_COLS), BF16), jax.ShapeDtypeStruct((HEADS, Q_LORA, QK_PAD), F32),
                   jax.ShapeDtypeStruct((HEADS, KV_LORA, 256), F32),
                   jax.ShapeDtypeStruct((1, Q_LORA), F32), jax.ShapeDtypeStruct((1, KV_LORA), F32)],
        compiler_params=_params(1),
    )(dq, dk, dv, cqn, ckvn, ms, q_a_g, kv_a_g, wuq3, wukv3, *tabs)


def _merge_loss(ya, yb, glog, b_gate, x, tgt, fg, wpa, wpb, wout):
    s = x.shape[0]
    tm = min(TM_FUSED, s)

    def body(ya_ref, yb_ref, g0_ref, g1_ref, b0_ref, b1_ref, x_ref, t_ref, fg_ref, wpa_ref, wpb_ref, wout_ref,
             mg_ref, pa_ref, pb_ref, dx2_ref, loss_ref, dfg_ref):
        @pl.when(pl.program_id(0) == 0)
        def _():
            loss_ref[...] = jnp.zeros_like(loss_ref)
            dfg_ref[...] = jnp.zeros_like(dfg_ref)

        pa = _dot(ya_ref[...], wpa_ref[...])
        pb = _dot(yb_ref[...], wpb_ref[...])
        pa_ref[...] = pa
        pb_ref[...] = pb
        merged = (jax.nn.sigmoid(g0_ref[...] + b0_ref[...]) * pa
                  + jax.nn.sigmoid(g1_ref[...] + b1_ref[...]) * pb).astype(BF16)
        mg_ref[...] = merged
        x2 = x_ref[...] + _dot(merged, wout_ref[...])
        fg_v = fg_ref[...]
        err = _rms(x2, fg_v) - t_ref[...]
        loss_ref[...] += 0.5 * jnp.sum(jnp.mean(err * err, axis=-1, keepdims=True), axis=0, keepdims=True)
        dx2, dfg_rows = _rms_bwd(x2, fg_v, err * (1.0 / D_MODEL))
        dx2_ref[...] = dx2
        dfg_ref[...] += jnp.sum(dfg_rows, axis=0, keepdims=True)

    row = pl.BlockSpec((tm, D_MODEL), lambda i: (i, 0))
    row1 = pl.BlockSpec((tm, D_MODEL), lambda i: (i, 1))
    vec = pl.BlockSpec((1, D_MODEL), lambda i: (0, 0))
    vec1 = pl.BlockSpec((1, D_MODEL), lambda i: (0, 1))
    wsp = pl.BlockSpec((D_MODEL, D_MODEL), lambda i: (0, 0))
    return pl.pallas_call(
        body, name="merge_loss", grid=(s // tm,),
        in_specs=[row, row, row, row1, vec, vec1, row, row, vec, wsp, wsp, wsp],
        out_specs=[row, row, row, row, pl.BlockSpec((1, 128), lambda i: (0, 0)), vec],
        out_shape=[jax.ShapeDtypeStruct((s, D_MODEL), BF16), jax.ShapeDtypeStruct((s, D_MODEL), F32),
                   jax.ShapeDtypeStruct((s, D_MODEL), F32), jax.ShapeDtypeStruct((s, D_MODEL), F32),
                   jax.ShapeDtypeStruct((1, 128), F32), jax.ShapeDtypeStruct((1, D_MODEL), F32)],
        compiler_params=_params(1),
    )(ya, yb, glog, glog, b_gate, b_gate, x, tgt, fg, wpa, wpb, wout)


def _merge_bwd(dx2, pa, pb, glog, b_gate, wpa, wpb, wout):
    s = dx2.shape[0]
    tm = min(TM_FUSED, s)

    def body(dx2_ref, pa_ref, pb_ref, g0_ref, g1_ref, b0_ref, b1_ref, wpa_ref, wpb_ref, wout_ref,
             dya_ref, dyb_ref, dgl_ref, dpa_ref, dpb_ref, dx2b_ref, dbg_ref):
        @pl.when(pl.program_id(0) == 0)
        def _():
            dbg_ref[...] = jnp.zeros_like(dbg_ref)

        dx2b = dx2_ref[...].astype(BF16)
        dx2b_ref[...] = dx2b
        dmg = _dot_nt(dx2b, wout_ref[...])
        g0 = jax.nn.sigmoid(g0_ref[...] + b0_ref[...])
        g1 = jax.nn.sigmoid(g1_ref[...] + b1_ref[...])
        dpa = (dmg * g0).astype(BF16)
        dpb = (dmg * g1).astype(BF16)
        dpa_ref[...] = dpa
        dpb_ref[...] = dpb
        dgl0 = dmg * pa_ref[...] * g0 * (1.0 - g0)
        dgl1 = dmg * pb_ref[...] * g1 * (1.0 - g1)
        dgl_ref[:, 0:D_MODEL] = dgl0.astype(BF16)
        dgl_ref[:, D_MODEL:2 * D_MODEL] = dgl1.astype(BF16)
        dbg_ref[:, 0:D_MODEL] += jnp.sum(dgl0, axis=0, keepdims=True)
        dbg_ref[:, D_MODEL:2 * D_MODEL] += jnp.sum(dgl1, axis=0, keepdims=True)
        dya_ref[...] = _dot_nt(dpa, wpa_ref[...])
        dyb_ref[...] = _dot_nt(dpb, wpb_ref[...])

    row = pl.BlockSpec((tm, D_MODEL), lambda i: (i, 0))
    row1 = pl.BlockSpec((tm, D_MODEL), lambda i: (i, 1))
    row2 = pl.BlockSpec((tm, 2 * D_MODEL), lambda i: (i, 0))
    vec = pl.BlockSpec((1, D_MODEL), lambda i: (0, 0))
    vec1 = pl.BlockSpec((1, D_MODEL), lambda i: (0, 1))
    vec2 = pl.BlockSpec((1, 2 * D_MODEL), lambda i: (0, 0))
    wsp = pl.BlockSpec((D_MODEL, D_MODEL), lambda i: (0, 0))
    return pl.pallas_call(
        body, name="merge_bwd", grid=(s // tm,),
        in_specs=[row, row, row, row, row1, vec, vec1, wsp, wsp, wsp],
        out_specs=[row, row, row2, row, row, row, vec2],
        out_shape=[jax.ShapeDtypeStruct((s, D_MODEL), F32), jax.ShapeDtypeStruct((s, D_MODEL), F32),
                   jax.ShapeDtypeStruct((s, 2 * D_MODEL), BF16), jax.ShapeDtypeStruct((s, D_MODEL), BF16),
                   jax.ShapeDtypeStruct((s, D_MODEL), BF16), jax.ShapeDtypeStruct((s, D_MODEL), BF16),
                   jax.ShapeDtypeStruct((1, 2 * D_MODEL), F32)],
        compiler_params=_params(1),
    )(dx2, pa, pb, glog, glog, b_gate, b_gate, wpa, wpb, wout)


def _hg_perm(w):
    r = w.shape[0]
    return w.reshape(r, 4, HEADS, HEAD_DIM).transpose(0, 2, 1, 3).reshape(r, 4 * D_MODEL)


def _hg_unperm(w):
    r = w.shape[0]
    return w.reshape(r, HEADS, 4, HEAD_DIM).transpose(0, 2, 1, 3).reshape(r, 4 * D_MODEL)


def _local_step(x, tgt, w_in, w_uq, w_ukv, wpa, wpb, wout, norm_g, b_gate, lb_logits, hg_norm_g, q_a_g, kv_a_g, fg):
    s = x.shape[0]
    zpad = jnp.zeros((D_MODEL, 64), BF16)
    w_hg = _hg_perm(w_in[:, 0:4096])
    w_ms = jnp.concatenate([w_in[:, 4096:4800], zpad], axis=1)
    w_mz = w_in[:, 4800:5824]
    w_gl = w_in[:, 5824:7872]
    wuq3 = jnp.pad(w_uq.reshape(Q_LORA, HEADS, QK_DIM).transpose(1, 0, 2), ((0, 0), (0, 0), (0, QK_PAD - QK_DIM)))
    wukv3 = w_ukv.reshape(KV_LORA, HEADS, 256).transpose(1, 0, 2)
    tabs = _rope_tables(s)

    h = _norm_in(x, norm_g)
    hg = _mm(h, w_hg, name="proj_hg")
    ms = _mm(h, w_ms, name="proj_ms")
    mz = _mm(h, w_mz, name="proj_mz")
    glog = _mm(h, w_gl, name="proj_gate")
    o_pre, ya, st0 = _hgrn_fwd(hg, lb_logits, hg_norm_g)
    q, k, v, cqn, ckvn = _mla_pre(ms, q_a_g, kv_a_g, wuq3, wukv3, tabs)
    o_att, yb, lse = _flash_fwd(q, k, v, mz)
    merged, pa, pb, dx2, loss, dfg = _merge_loss(ya, yb, glog, b_gate, x, tgt, fg, wpa, wpb, wout)

    dya, dyb, dglog, dpa, dpb, dx2b, dbg = _merge_bwd(dx2, pa, pb, glog, b_gate, wpa, wpb, wout)
    d_wout = _mm_tn(merged, dx2b, name="dw_out")
    d_wpa = _mm_tn(ya, dpa, name="dw_proj_a")
    d_wpb = _mm_tn(yb, dpb, name="dw_proj_b")
    dhg, dlb, dhgg = _hgrn_bwd(hg, o_pre, dya, st0, lb_logits, hg_norm_g)
    dq, dk, dv, dmz = _flash_bwd(q, k, v, dyb, mz, o_att, lse, tabs)
    dms, d_wuq3, d_wukv3, dqg, dkvg = _mla_bwd_proj(dq, dk, dv, cqn, ckvn, ms, q_a_g, kv_a_g, wuq3, wukv3, tabs)
    d_w_hg = _mm_tn(h, dhg, name="dw_in_hg")
    d_w_ms = _mm_tn(h, dms, name="dw_in_ms")
    d_w_mz = _mm_tn(h, dmz, name="dw_in_mz")
    d_w_gl = _mm_tn(h, dglog, name="dw_in_gate")
    dh = _mm(dhg, w_hg, trans_b=True, name="dh_hg")
    dh = _mm(dms, w_ms, trans_b=True, add=dh, name="dh_ms")
    dh = _mm(dmz, w_mz, trans_b=True, add=dh, name="dh_mz")
    dh = _mm(dglog, w_gl, trans_b=True, add=dh, name="dh_gate")
    grad_x, dng = _norm_in_bwd(x, norm_g, dh, dx2)

    big = {
        "w_in": jnp.concatenate([_hg_unperm(d_w_hg), d_w_ms[:, 0:704], d_w_mz, d_w_gl], axis=1),
        "w_uq": d_wuq3.transpose(1, 0, 2)[:, :, 0:QK_DIM].reshape(Q_LORA, HEADS * QK_DIM),
        "w_ukv": d_wukv3.transpose(1, 0, 2).reshape(KV_LORA, HEADS * 256),
        "w_proj_a": d_wpa, "w_proj_b": d_wpb, "w_out": d_wout,
    }
    small = {"norm_g": dng, "b_gate": dbg, "lb": dlb, "hg_norm_g": dhgg, "q_a_g": dqg, "kv_a_g": dkvg,
             "final_norm_g": dfg}
    return loss, grad_x, big, small


def _pack_block(w_in_b, w_uq_b, w_ukv_b, wpa_b, wpb_b, wout_b):
    return jnp.concatenate([w_in_b.reshape(1968, D_MODEL), w_uq_b.reshape(144, D_MODEL),
                            w_ukv_b.reshape(128, D_MODEL), wpa_b, wpb_b, wout_b], axis=0)


def _unpack_block(p):
    return (p[0:1968].reshape(D_MODEL, W_IN_BLK), p[1968:2112].reshape(Q_LORA, 384),
            p[2112:2240].reshape(KV_LORA, 512), p[2240:2496], p[2496:2752], p[2752:3008])


def _pack_full(big):
    blocks = []
    for b in range(N_CHIPS):
        blocks.append(_pack_block(
            big["w_in"][:, b * W_IN_BLK:(b + 1) * W_IN_BLK], big["w_uq"][:, b * 384:(b + 1) * 384],
            big["w_ukv"][:, b * 512:(b + 1) * 512], big["w_proj_a"][b * 256:(b + 1) * 256],
            big["w_proj_b"][b * 256:(b + 1) * 256], big["w_out"][b * 256:(b + 1) * 256]))
    return jnp.stack(blocks, axis=0)


def _unpack_full(g):
    parts = [_unpack_block(g[b]) for b in range(N_CHIPS)]
    w_in, w_uq, w_ukv = (jnp.concatenate([p[n] for p in parts], axis=1) for n in range(3))
    wpa, wpb, wout = (jnp.concatenate([p[n] for p in parts], axis=0) for n in range(3, 6))
    return w_in, w_uq, w_ukv, wpa, wpb, wout


MESH_ID = pl.DeviceIdType.MESH
ANY = pl.BlockSpec(memory_space=pl.ANY)


def _me():
    return lax.axis_index("x"), lax.axis_index("y"), lax.axis_index("c")


def _other_chips(x, y):
    return [(1 - x, y), (x, 1 - y), (1 - x, 1 - y)]


def _gather_weights(block):
    def body(in_ref, out_ref, send_sems, recv_sems, local_sem):
        x, y, c = _me()
        chips = _other_chips(x, y)
        half = pl.ds(c * HALF_ROWS, HALF_ROWS)
        other_half = pl.ds((1 - c) * HALF_ROWS, HALF_ROWS)

        def copy(k, src, dst, to):
            return pltpu.make_async_remote_copy(src_ref=src, dst_ref=dst, send_sem=send_sems.at[k],
                                                recv_sem=recv_sems.at[k], device_id=to, device_id_type=MESH_ID)

        mine = pltpu.make_async_copy(in_ref, out_ref.at[2 * x + y], local_sem)
        mine.start()
        first = [copy(j, in_ref.at[half, :], out_ref.at[2 * x + y, half, :], (cx, cy, c))
                 for j, (cx, cy) in enumerate(chips)]
        for cp in first:
            cp.start()
        passed = []
        for j, (cx, cy) in enumerate(chips):
            landed = out_ref.at[2 * cx + cy, half, :]
            copy(j, landed, landed, (cx, cy, c)).wait_recv()
            fwd = copy(3 + j, landed, landed, (x, y, 1 - c))
            fwd.start()
            passed.append(fwd)
        for j, (cx, cy) in enumerate(chips):
            theirs = out_ref.at[2 * cx + cy, other_half, :]
            copy(3 + j, theirs, theirs, (x, y, 1 - c)).wait_recv()
        for cp in first + passed:
            cp.wait_send()
        mine.wait()

    return pl.pallas_call(
        body, name="gather_weights", in_specs=[ANY], out_specs=ANY,
        out_shape=jax.ShapeDtypeStruct((N_CHIPS, PACK_ROWS, D_MODEL), block.dtype),
        scratch_shapes=[pltpu.SemaphoreType.DMA((6,)), pltpu.SemaphoreType.DMA((6,)), pltpu.SemaphoreType.DMA],
    )(block)


def _swap_halves(g):
    def body(g_ref, out_ref, send_sem, recv_sem):
        x, y, c = _me()
        cp = pltpu.make_async_remote_copy(
            src_ref=g_ref.at[:, pl.ds((1 - c) * HALF_ROWS, HALF_ROWS), :], dst_ref=out_ref,
            send_sem=send_sem, recv_sem=recv_sem, device_id=(x, y, 1 - c), device_id_type=MESH_ID)
        cp.start()
        cp.wait()

    return pl.pallas_call(
        body, name="grad_swap_halves", in_specs=[ANY], out_specs=ANY,
        out_shape=jax.ShapeDtypeStruct((N_CHIPS, HALF_ROWS, D_MODEL), g.dtype),
        scratch_shapes=[pltpu.SemaphoreType.DMA, pltpu.SemaphoreType.DMA],
    )(g)


def _scatter_blocks(hsum):
    def body(h_ref, out_ref, send_sems, recv_sems):
        x, y, c = _me()
        chips = _other_chips(x, y)
        cps = []
        for j, (cx, cy) in enumerate(chips):
            cps.append(pltpu.make_async_remote_copy(
                src_ref=h_ref.at[2 * cx + cy], dst_ref=out_ref.at[j], send_sem=send_sems.at[j],
                recv_sem=recv_sems.at[j], device_id=(cx, cy, c), device_id_type=MESH_ID))
        for cp in cps:
            cp.start()
        for cp in cps:
            cp.wait()

    return pl.pallas_call(
        body, name="grad_scatter_blocks", in_specs=[ANY], out_specs=ANY,
        out_shape=jax.ShapeDtypeStruct((3, HALF_ROWS, D_MODEL), hsum.dtype),
        scratch_shapes=[pltpu.SemaphoreType.DMA((3,)), pltpu.SemaphoreType.DMA((3,))],
    )(hsum)


def _join_halves(rhalf):
    def body(r_ref, out_ref, send_sem, recv_sem, local_sem):
        x, y, c = _me()
        mine = pltpu.make_async_copy(r_ref, out_ref.at[c], local_sem)
        mine.start()
        cp = pltpu.make_async_remote_copy(src_ref=r_ref, dst_ref=out_ref.at[c], send_sem=send_sem,
                                          recv_sem=recv_sem, device_id=(x, y, 1 - c), device_id_type=MESH_ID)
        cp.start()
        cp.wait()
        mine.wait()

    return pl.pallas_call(
        body, name="grad_join_halves", in_specs=[ANY], out_specs=ANY,
        out_shape=jax.ShapeDtypeStruct((2, HALF_ROWS, D_MODEL), rhalf.dtype),
        scratch_shapes=[pltpu.SemaphoreType.DMA, pltpu.SemaphoreType.DMA, pltpu.SemaphoreType.DMA],
    )(rhalf)


def _gather_small(vec):
    def body(v_ref, out_ref, send_sems, recv_sems, local_sem):
        x, y, c = _me()
        my_id = 4 * x + 2 * y + c
        mine = pltpu.make_async_copy(v_ref, out_ref.at[my_id], local_sem)
        mine.start()
        cps = []
        for r in range(1, N_DEV):
            peer = (x ^ (r >> 2), y ^ ((r >> 1) & 1), c ^ (r & 1))
            cps.append(pltpu.make_async_remote_copy(
                src_ref=v_ref, dst_ref=out_ref.at[my_id], send_sem=send_sems.at[r - 1],
                recv_sem=recv_sems.at[r - 1], device_id=peer, device_id_type=MESH_ID))
        for cp in cps:
            cp.start()
        for cp in cps:
            cp.wait()
        mine.wait()

    return pl.pallas_call(
        body, name="gather_small", in_specs=[ANY], out_specs=ANY,
        out_shape=jax.ShapeDtypeStruct((N_DEV, 1, SMALL_COLS), vec.dtype),
        scratch_shapes=[pltpu.SemaphoreType.DMA((N_DEV - 1,)), pltpu.SemaphoreType.DMA((N_DEV - 1,)),
                        pltpu.SemaphoreType.DMA],
    )(vec)


def _add_rows(terms, *, name):
    r = terms[0].shape[0]
    tm = 376 if r % 376 == 0 else r
    n = len(terms)

    def body(*refs):
        acc = refs[0][...]
        for t in refs[1:n]:
            acc = acc + t[...]
        refs[n][...] = acc

    row = pl.BlockSpec((tm, D_MODEL), lambda i: (i, 0))
    return pl.pallas_call(
        body, name=name, grid=(r // tm,), in_specs=[row] * n, out_specs=row,
        out_shape=jax.ShapeDtypeStruct((r, D_MODEL), F32), compiler_params=_params(1),
    )(*terms)


def _pack_small(small, lb_logits, loss):
    def body(ng_ref, bg_ref, dlb_ref, lbl_ref, hgg_ref, qg_ref, kvg_ref, fg_ref, loss_ref, out_ref):
        out_ref[...] = jnp.zeros_like(out_ref)
        out_ref[:, 0:1024] = ng_ref[...]
        out_ref[:, 1024:3072] = bg_ref[...]
        _, p0p1 = _lower_bound(lbl_ref[...])
        dl0 = dlb_ref[...] * p0p1
        out_ref[:, 3072:4096] = dl0
        out_ref[:, 4096:5120] = -dl0
        hgg = hgg_ref[0]
        for h in range(1, HEADS):
            hgg = hgg + hgg_ref[h]
        out_ref[:, 5120:5248] = hgg
        out_ref[:, 5248:5632] = qg_ref[...]
        out_ref[:, 5632:5888] = kvg_ref[...]
        out_ref[:, 5888:6912] = fg_ref[...]
        out_ref[:, 6912:7040] = loss_ref[...]

    return pl.pallas_call(
        body, name="pack_small", out_shape=jax.ShapeDtypeStruct((1, SMALL_COLS), F32),
    )(small["norm_g"], small["b_gate"], small["lb"], lb_logits, small["hg_norm_g"], small["q_a_g"],
      small["kv_a_g"], small["final_norm_g"], loss)


def _sum_small(gathered):
    def body(g_ref, out_ref):
        acc = g_ref[0]
        for d in range(1, N_DEV):
            acc = acc + g_ref[d]
        out_ref[...] = acc

    return pl.pallas_call(
        body, name="sum_small", out_shape=jax.ShapeDtypeStruct((1, SMALL_COLS), F32),
    )(gathered)


def _adamw(w, g, m, v, *, name):
    r, cols = w.shape
    tm = 376 if r % 376 == 0 else r
    c1 = 1.0 - ADAM_B1 ** ADAM_STEP
    c2 = 1.0 - ADAM_B2 ** ADAM_STEP

    def body(w_ref, g_ref, m_ref, v_ref, d_ref, nm_ref, nv_ref):
        gv = g_ref[...]
        nm = ADAM_B1 * m_ref[...] + (1.0 - ADAM_B1) * gv
        nv = ADAM_B2 * v_ref[...] + (1.0 - ADAM_B2) * (gv * gv)
        nm_ref[...] = nm
        nv_ref[...] = nv
        d_ref[...] = -ADAM_LR * ((nm / c1) / (jnp.sqrt(nv / c2) + ADAM_EPS) + ADAM_WD * w_ref[...])

    row = pl.BlockSpec((tm, cols), lambda i: (i, 0))
    shp = jax.ShapeDtypeStruct((r, cols), F32)
    return pl.pallas_call(
        body, name=name, grid=(r // tm,), in_specs=[row] * 4, out_specs=[row] * 3, out_shape=[shp] * 3,
        compiler_params=_params(1),
    )(w, g, m, v)


def _small_vec(norm_g, b_gate, lb_logits, hg_norm_g, q_a_g, kv_a_g, fg):
    parts = [norm_g.reshape(1, -1), b_gate.reshape(1, -1), lb_logits.reshape(1, -1), hg_norm_g.reshape(1, -1),
             q_a_g.reshape(1, -1), kv_a_g.reshape(1, -1), fg.reshape(1, -1), jnp.zeros((1, SMALL_COLS - 6912), F32)]
    return jnp.concatenate(parts, axis=1)


def _split_small(vec):
    v = vec.reshape(-1)
    return (v[0:1024].reshape(1, 1024), v[1024:3072].reshape(1, 2048), v[3072:5120].reshape(2, 1024),
            v[5120:5248].reshape(1, 128), v[5248:5632].reshape(1, 384), v[5632:5888].reshape(1, 256), v[5888:6912])


def kernel(x, norm_g, w_in, b_gate, lb_logits, hg_norm_g, q_a_g, w_uq, kv_a_g, w_ukv, w_proj_a, w_proj_b, w_out, final_norm_g, loss_target, m_norm_g, m_w_in, m_b_gate, m_lb_logits, m_hg_norm_g, m_q_a_g, m_w_uq, m_kv_a_g, m_w_ukv, m_w_proj_a, m_w_proj_b, m_w_out, m_final_norm_g, v_norm_g, v_w_in, v_b_gate, v_lb_logits, v_hg_norm_g, v_q_a_g, v_w_uq, v_kv_a_g, v_w_ukv, v_w_proj_a, v_w_proj_b, v_w_out, v_final_norm_g):
    c = lax.axis_index("c")
    chip = 2 * lax.axis_index("x") + lax.axis_index("y")

    w_pack = _pack_block(w_in[0], w_uq[0], w_ukv[0], w_proj_a[0], w_proj_b[0], w_out[0])
    gathered = _gather_weights(w_pack.astype(BF16))
    fw_in, fw_uq, fw_ukv, fwpa, fwpb, fwout = _unpack_full(gathered)

    loss, grad_x, big, small = _local_step(
        x[0], loss_target[0], fw_in, fw_uq, fw_ukv, fwpa, fwpb, fwout,
        norm_g, b_gate, lb_logits, hg_norm_g, q_a_g, kv_a_g, final_norm_g.reshape(1, D_MODEL))

    g_full = _pack_full(big)
    from_sibling = _swap_halves(g_full)
    my_half = lax.dynamic_slice_in_dim(g_full, c * HALF_ROWS, HALF_ROWS, axis=1)
    hsum = _add_rows([my_half.reshape(N_CHIPS * HALF_ROWS, D_MODEL), from_sibling.reshape(N_CHIPS * HALF_ROWS, D_MODEL)],
                     name="grad_add_cores").reshape(N_CHIPS, HALF_ROWS, D_MODEL)
    landed = _scatter_blocks(hsum)
    own = lax.dynamic_index_in_dim(hsum, chip, axis=0, keepdims=False)
    rhalf = _add_rows([own, landed[0], landed[1], landed[2]], name="grad_add_chips")
    g_block = _join_halves(rhalf).reshape(PACK_ROWS, D_MODEL)

    small_sum = _sum_small(_gather_small(_pack_small(small, lb_logits, loss)))

    m_pack = _pack_block(m_w_in[0], m_w_uq[0], m_w_ukv[0], m_w_proj_a[0], m_w_proj_b[0], m_w_out[0])
    v_pack = _pack_block(v_w_in[0], v_w_uq[0], v_w_ukv[0], v_w_proj_a[0], v_w_proj_b[0], v_w_out[0])
    d_pack, nm_pack, nv_pack = _adamw(w_pack, g_block, m_pack, v_pack, name="adamw_big")
    ws = _small_vec(norm_g, b_gate, lb_logits, hg_norm_g, q_a_g, kv_a_g, final_norm_g).reshape(7, 1024)
    ms_ = _small_vec(m_norm_g, m_b_gate, m_lb_logits, m_hg_norm_g, m_q_a_g, m_kv_a_g, m_final_norm_g).reshape(7, 1024)
    vs = _small_vec(v_norm_g, v_b_gate, v_lb_logits, v_hg_norm_g, v_q_a_g, v_kv_a_g, v_final_norm_g).reshape(7, 1024)
    d_small, nm_small, nv_small = _adamw(ws, small_sum.reshape(7, 1024), ms_, vs, name="adamw_small")

    def outputs(pack, vec):
        b_in, b_uq, b_ukv, b_pa, b_pb, b_out = _unpack_block(pack)
        s_ng, s_bg, s_lb, s_hg, s_qg, s_kvg, s_fg = _split_small(vec)
        return (s_ng, b_in[None], s_bg, s_lb, s_hg, s_qg, b_uq[None], s_kvg, b_ukv[None], b_pa[None], b_pb[None],
                b_out[None], s_fg)

    total_loss = small_sum[0, 6912]
    return (total_loss, grad_x[None], *outputs(g_block, small_sum), *outputs(d_pack, d_small),
            *outputs(nm_pack, nm_small), *outputs(nv_pack, nv_small))
```

```python
import functools

import jax
import jax.numpy as jnp
from jax import lax
from jax.experimental import pallas as pl
from jax.experimental.pallas import tpu as pltpu

F32 = jnp.float32
BF16 = jnp.bfloat16

D_MODEL = 1024
HEADS = 8
HEAD_DIM = 128
HG_CHUNK = 32
CHUNK_SHIFT = 5
HEAD_SHIFT = 7
QK_NOPE = 128
QK_ROPE = 64
QK_DIM = QK_NOPE + QK_ROPE
QK_PAD = 256
Q_LORA = 384
KV_LORA = 256
MS_COLS = 768
ROPE_THETA = 10000.0
EPS = 1e-6
ATT_SCALE = QK_DIM ** -0.5
LOG2E = 1.4426950408889634
LN2 = 0.6931471805599453
Q_PRESCALE = ATT_SCALE * LOG2E

ADAM_LR = 0.001
ADAM_B1 = 0.9
ADAM_B2 = 0.999
ADAM_EPS = 1e-08
ADAM_WD = 0.01
ADAM_STEP = 10

N_CHIPS = 4
N_DEV = 8
W_IN_COLS = 7872
W_IN_BLK = W_IN_COLS // N_CHIPS
REST_ROWS = 144 + 128 + 3 * 256
SMALL_COLS = 7168

TM_MM = 512
TM_FUSED = 256
HG_ROWS = 256
TQ = 512
FLASH_HEADS = 2
VMEM_LIMIT = 56 * 1024 * 1024


def _dot(a, b):
    return lax.dot_general(a, b, (((1,), (0,)), ((), ())), preferred_element_type=F32)


def _dot_nt(a, b):
    return lax.dot_general(a, b, (((1,), (1,)), ((), ())), preferred_element_type=F32)


def _dot_tn(a, b):
    return lax.dot_general(a, b, (((0,), (0,)), ((), ())), preferred_element_type=F32)


def _params(n_axes):
    return pltpu.CompilerParams(dimension_semantics=("arbitrary",) * n_axes, vmem_limit_bytes=VMEM_LIMIT)


def _rms(x, g):
    r = lax.rsqrt(jnp.mean(x * x, axis=-1, keepdims=True) + EPS)
    return x * r * g


def _rms_bwd(x, g, dy):
    r = lax.rsqrt(jnp.mean(x * x, axis=-1, keepdims=True) + EPS)
    xh = x * r
    dyg = dy * g
    dx = r * (dyg - xh * jnp.mean(dyg * xh, axis=-1, keepdims=True))
    return dx, dy * xh


def _silu_parts(z):
    s = jax.nn.sigmoid(z)
    return z * s, s * (1.0 + z * (1.0 - s))


def _rope(x, c, sa, sb):
    return x * c + pltpu.roll(x, 32, 1) * sa + pltpu.roll(x, 96, 1) * sb


def _rope_bwd(dy, c, sa, sb):
    return dy * c + pltpu.roll(dy * sa, 96, 1) + pltpu.roll(dy * sb, 32, 1)


def _rope_tables(seq):
    inv = ROPE_THETA ** (-jnp.arange(0, QK_ROPE, 2, dtype=F32) / QK_ROPE)
    ang = jnp.arange(seq, dtype=F32)[:, None] * inv[None, :]
    cos, sin = jnp.cos(ang), jnp.sin(ang)
    z32 = jnp.zeros_like(cos)
    z64 = jnp.zeros((seq, 64), F32)
    c = jnp.concatenate([cos, cos, z64], axis=1)
    sa = jnp.concatenate([z32, sin, z64], axis=1)
    sb = jnp.concatenate([-sin, z32, z64], axis=1)
    return c, sa, sb


def _mm(a, b, *, name, trans_b=False, add=None, out_dtype=F32, tm=TM_MM, tn=1024, tk=1024):
    m, k = a.shape
    n = b.shape[0] if trans_b else b.shape[1]
    tm, tn, tk = min(tm, m), min(tn, n), min(tk, k)
    assert m % tm == 0 and n % tn == 0 and k % tk == 0
    nk = k // tk
    has_add = add is not None

    def body(*refs):
        if has_add:
            a_ref, b_ref, add_ref, o_ref, acc_ref = refs
        else:
            a_ref, b_ref, o_ref, acc_ref = refs
        kk = pl.program_id(2)

        @pl.when(kk == 0)
        def _():
            acc_ref[...] = add_ref[...] if has_add else jnp.zeros_like(acc_ref)

        if trans_b:
            acc_ref[...] += _dot_nt(a_ref[...], b_ref[...])
        else:
            acc_ref[...] += _dot(a_ref[...], b_ref[...])

        @pl.when(kk == nk - 1)
        def _():
            o_ref[...] = acc_ref[...].astype(out_dtype)

    in_specs = [pl.BlockSpec((tm, tk), lambda i, j, kk: (i, kk))]
    if trans_b:
        in_specs.append(pl.BlockSpec((tn, tk), lambda i, j, kk: (j, kk)))
    else:
        in_specs.append(pl.BlockSpec((tk, tn), lambda i, j, kk: (kk, j)))
    args = [a, b]
    if has_add:
        in_specs.append(pl.BlockSpec((tm, tn), lambda i, j, kk: (i, j)))
        args.append(add)
    return pl.pallas_call(
        body, name=name, grid=(m // tm, n // tn, nk),
        in_specs=in_specs, out_specs=pl.BlockSpec((tm, tn), lambda i, j, kk: (i, j)),
        out_shape=jax.ShapeDtypeStruct((m, n), out_dtype),
        scratch_shapes=[pltpu.VMEM((tm, tn), F32)], compiler_params=_params(3),
    )(*args)


def _mm_tn(a, b, *, name, tm=TM_MM, tn=1024):
    flat = a.ndim == 2
    if flat:
        a = a[None]
    g, m, k = a.shape
    n = b.shape[1]
    tm, tn = min(tm, m), min(tn, n)
    assert m % tm == 0 and n % tn == 0

    def body(a_ref, b_ref, o_ref):
        @pl.when(pl.program_id(2) == 0)
        def _():
            o_ref[...] = jnp.zeros_like(o_ref)

        o_ref[...] += _dot_tn(a_ref[...], b_ref[...])

    out = pl.pallas_call(
        body, name=name, grid=(g, n // tn, m // tm),
        in_specs=[pl.BlockSpec((None, tm, k), lambda s, j, i: (s, i, 0)),
                  pl.BlockSpec((tm, tn), lambda s, j, i: (i, j))],
        out_specs=pl.BlockSpec((None, k, tn), lambda s, j, i: (s, 0, j)),
        out_shape=jax.ShapeDtypeStruct((g, k, n), F32), compiler_params=_params(3),
    )(a, b)
    return out[0] if flat else out


def _mm_slabs_out(a, wt, slabs, *, name, tm=TM_MM):
    m, k = a.shape
    tm = min(tm, m)
    n = D_MODEL

    def body(a_ref, w_ref, o_ref):
        o_ref[...] = _dot_nt(a_ref[...], w_ref[...])

    return pl.pallas_call(
        body, name=name, grid=(m // tm, slabs),
        in_specs=[pl.BlockSpec((tm, k), lambda i, j: (i, 0)), pl.BlockSpec((n, k), lambda i, j: (j, 0))],
        out_specs=pl.BlockSpec((None, tm, n), lambda i, j: (j, i, 0)),
        out_shape=jax.ShapeDtypeStruct((slabs, m, n), F32), compiler_params=_params(2),
    )(a, wt)


def _mm_slabs_in(a3, w, *, name, tm=TM_MM):
    slabs, m, k = a3.shape
    n = w.shape[1]
    tm = min(tm, m)

    def body(a_ref, w_ref, o_ref, acc_ref):
        j = pl.program_id(1)

        @pl.when(j == 0)
        def _():
            acc_ref[...] = jnp.zeros_like(acc_ref)

        acc_ref[...] += _dot(a_ref[...], w_ref[...])

        @pl.when(j == slabs - 1)
        def _():
            o_ref[...] = acc_ref[...]

    return pl.pallas_call(
        body, name=name, grid=(m // tm, slabs),
        in_specs=[pl.BlockSpec((None, tm, k), lambda i, j: (j, i, 0)), pl.BlockSpec((k, n), lambda i, j: (j, 0))],
        out_specs=pl.BlockSpec((tm, n), lambda i, j: (i, 0)),
        out_shape=jax.ShapeDtypeStruct((m, n), F32),
        scratch_shapes=[pltpu.VMEM((tm, n), F32)], compiler_params=_params(2),
    )(a3, w)


def _norm_in(x, g):
    s = x.shape[0]
    tm = min(TM_MM, s)

    def body(x_ref, g_ref, h_ref):
        h_ref[...] = _rms(x_ref[...], g_ref[...]).astype(BF16)

    return pl.pallas_call(
        body, name="norm_in", grid=(s // tm,),
        in_specs=[pl.BlockSpec((tm, D_MODEL), lambda i: (i, 0)), pl.BlockSpec((1, D_MODEL), lambda i: (0, 0))],
        out_specs=pl.BlockSpec((tm, D_MODEL), lambda i: (i, 0)),
        out_shape=jax.ShapeDtypeStruct((s, D_MODEL), BF16), compiler_params=_params(1),
    )(x, g)


def _norm_in_bwd(x, g, dh, dx2):
    s = x.shape[0]
    tm = min(TM_MM, s)

    def body(x_ref, g_ref, dh_ref, dx2_ref, dx_ref, dg_ref):
        @pl.when(pl.program_id(0) == 0)
        def _():
            dg_ref[...] = jnp.zeros_like(dg_ref)

        dx, dg_rows = _rms_bwd(x_ref[...], g_ref[...], dh_ref[...])
        dx_ref[...] = dx + dx2_ref[...]
        dg_ref[...] += jnp.sum(dg_rows, axis=0, keepdims=True)

    row = pl.BlockSpec((tm, D_MODEL), lambda i: (i, 0))
    vec = pl.BlockSpec((1, D_MODEL), lambda i: (0, 0))
    return pl.pallas_call(
        body, name="norm_in_bwd", grid=(s // tm,),
        in_specs=[row, vec, row, row], out_specs=[row, vec],
        out_shape=[jax.ShapeDtypeStruct((s, D_MODEL), F32), jax.ShapeDtypeStruct((1, D_MODEL), F32)],
        compiler_params=_params(1),
    )(x, g, dh, dx2)


def _chunk_rows(rows):
    return lax.broadcasted_iota(jnp.int32, (rows, HEAD_DIM), 0) & (HG_CHUNK - 1)


def _chunk_cumsum(x, rows):
    pos = _chunk_rows(rows)
    shift = 1
    while shift < HG_CHUNK:
        x = x + jnp.where(pos >= shift, pltpu.roll(x, shift, 0), 0.0)
        shift *= 2
    return x


def _chunk_revcumsum(x, rows):
    pos = _chunk_rows(rows)
    shift = 1
    while shift < HG_CHUNK:
        x = x + jnp.where(pos + shift < HG_CHUNK, pltpu.roll(x, rows - shift, 0), 0.0)
        shift *= 2
    return x


def _lower_bound(lbl):
    mx = jnp.maximum(lbl[0:1, :], lbl[1:2, :])
    e0 = jnp.exp(lbl[0:1, :] - mx)
    e1 = jnp.exp(lbl[1:2, :] - mx)
    p0 = e0 / (e0 + e1)
    return p0, p0 * (e1 / (e0 + e1))


def _hg_masks(rows, nch, tmask_s, bdmask_s):
    r = lax.broadcasted_iota(jnp.int32, (rows, rows), 0)
    c = lax.broadcasted_iota(jnp.int32, (rows, rows), 1)
    tmask_s[...] = jnp.where(((r >> CHUNK_SHIFT) == (c >> CHUNK_SHIFT)) & (r >= c), 1.0, 0.0)
    r = lax.broadcasted_iota(jnp.int32, (rows, nch * HEAD_DIM), 0)
    c = lax.broadcasted_iota(jnp.int32, (rows, nch * HEAD_DIM), 1)
    bdmask_s[...] = jnp.where((r >> CHUNK_SHIFT) == (c >> HEAD_SHIFT), 1.0, 0.0).astype(BF16)


def _block_diag(x, nch, bdmask):
    return jnp.tile(x, (1, nch)) * bdmask


def _hgrn_fwd(hg, lb_logits, norm_g):
    s = hg.shape[1]
    rows = min(HG_ROWS, s)
    nblk = s // rows
    nch = rows // HG_CHUNK

    def body(hg_ref, lbl_ref, g_ref, o_ref, ya_ref, st0_ref, st_s, stall_s, tmask_s, bdmask_s):
        @pl.when(pl.program_id(1) == 0)
        def _():
            st_s[...] = jnp.zeros_like(st_s)
            _hg_masks(rows, nch, tmask_s, bdmask_s)

        hq = hg_ref[0]
        hf = hg_ref[1]
        hi = hg_ref[2]
        hz = hg_ref[3]
        lb, _ = _lower_bound(lbl_ref[...])
        f = lb + (1.0 - lb) * jax.nn.sigmoid(hf)
        q = hq * jax.nn.sigmoid(hq)
        k = 1.0 - f
        logf = jnp.log(f)
        b = _chunk_cumsum(logf, rows)
        q_in = (q * jnp.exp(b)).astype(BF16)
        k_in = (k * jnp.exp(-b)).astype(BF16)
        k_out = (k * jnp.exp(_chunk_revcumsum(logf, rows) - logf)).astype(BF16)
        vb = hi.astype(BF16)
        bdmask = bdmask_s[...]

        sc = jnp.where(tmask_s[...] > 0.5, _dot_nt(q_in, k_in), 0.0)
        o_intra = _dot(sc.astype(BF16), vb)
        kvt = _dot_tn(vb, _block_diag(k_out, nch, bdmask))
        st = st_s[...]
        st0_ref[...] = st
        for c in range(nch):
            cols = slice(c * HEAD_DIM, (c + 1) * HEAD_DIM)
            last = (c + 1) * HG_CHUNK - 1
            stall_s[:, cols] = st.astype(BF16)
            st = st * jnp.exp(b[last:last + 1, :]) + kvt[:, cols]
        st_s[...] = st
        o = o_intra + _dot_nt(_block_diag(q_in, nch, bdmask), stall_s[...])
        o_ref[...] = o
        silu_z, _ = _silu_parts(hz)
        ya_ref[...] = (_rms(o, g_ref[...]) * silu_z).astype(BF16)

    return pl.pallas_call(
        body, name="hgrn_fwd", grid=(HEADS, nblk),
        in_specs=[pl.BlockSpec((4, rows, HEAD_DIM), lambda h, i: (0, i, h)),
                  pl.BlockSpec((2, HEAD_DIM), lambda h, i: (0, h)),
                  pl.BlockSpec((1, HEAD_DIM), lambda h, i: (0, 0))],
        out_specs=[pl.BlockSpec((rows, HEAD_DIM), lambda h, i: (i, h)),
                   pl.BlockSpec((rows, HEAD_DIM), lambda h, i: (i, h)),
                   pl.BlockSpec((None, None, HEAD_DIM, HEAD_DIM), lambda h, i: (h, i, 0, 0))],
        out_shape=[jax.ShapeDtypeStruct((s, D_MODEL), F32), jax.ShapeDtypeStruct((s, D_MODEL), BF16),
                   jax.ShapeDtypeStruct((HEADS, nblk, HEAD_DIM, HEAD_DIM), F32)],
        scratch_shapes=[pltpu.VMEM((HEAD_DIM, HEAD_DIM), F32), pltpu.VMEM((HEAD_DIM, nch * HEAD_DIM), BF16),
                        pltpu.VMEM((rows, rows), F32), pltpu.VMEM((rows, nch * HEAD_DIM), BF16)],
        compiler_params=_params(2),
    )(hg, lb_logits, norm_g)


def _hgrn_bwd(hg, o_pre, dya, st0, lb_logits, norm_g):
    s = hg.shape[1]
    rows = min(HG_ROWS, s)
    nblk = s // rows
    nch = rows // HG_CHUNK

    def body(hg_ref, o_ref, dya_ref, st0_ref, lbl_ref, g_ref, dhg_ref, dlb_ref, dg_ref,
             dst_s, stp_s, stp_rows_s, dst_rows_s, dst_lane_s, dbl_s, tmask_s, bdmask_s):
        @pl.when(pl.program_id(1) == 0)
        def _():
            dst_s[...] = jnp.zeros_like(dst_s)
            dlb_ref[...] = jnp.zeros_like(dlb_ref)
            dg_ref[...] = jnp.zeros_like(dg_ref)
            _hg_masks(rows, nch, tmask_s, bdmask_s)

        hq = hg_ref[0]
        hf = hg_ref[1]
        hi = hg_ref[2]
        hz = hg_ref[3]
        lb, _ = _lower_bound(lbl_ref[...])
        sg = jax.nn.sigmoid(hf)
        f = lb + (1.0 - lb) * sg
        q, dsilu_q = _silu_parts(hq)
        k = 1.0 - f
        logf = jnp.log(f)
        b = _chunk_cumsum(logf, rows)
        eb = jnp.exp(b)
        enb = jnp.exp(-b)
        ebl = jnp.exp(_chunk_revcumsum(logf, rows) - logf)
        q_in32 = q * eb
        k_in32 = k * enb
        k_out32 = k * ebl
        q_in = q_in32.astype(BF16)
        k_in = k_in32.astype(BF16)
        k_out = k_out32.astype(BF16)
        vb = hi.astype(BF16)
        bdmask = bdmask_s[...]
        tmask = tmask_s[...] > 0.5
        kbd = _block_diag(k_out, nch, bdmask)
        qbd = _block_diag(q_in, nch, bdmask)
        decs = [jnp.exp(b[(c + 1) * HG_CHUNK - 1:(c + 1) * HG_CHUNK, :]) for c in range(nch)]

        kvt = _dot_tn(vb, kbd)
        st = st0_ref[...]
        for c in range(nch):
            stp_s[c] = st
            stp_rows_s[c * HEAD_DIM:(c + 1) * HEAD_DIM, :] = st.astype(BF16)
            st = st * decs[c] + kvt[:, c * HEAD_DIM:(c + 1) * HEAD_DIM]

        g = g_ref[...]
        o = o_ref[...]
        rstd = lax.rsqrt(jnp.mean(o * o, axis=-1, keepdims=True) + EPS)
        oh = o * rstd
        silu_z, dsilu_z = _silu_parts(hz)
        dya_v = dya_ref[...]
        dn = dya_v * silu_z
        dhz = dya_v * (oh * g) * dsilu_z
        dg_ref[...] += jnp.sum(dn * oh, axis=0, keepdims=True)
        doh = dn * g
        do = (rstd * (doh - oh * jnp.mean(doh * oh, axis=-1, keepdims=True))).astype(BF16)

        dq_all = _dot_tn(do, qbd)
        dst = dst_s[...]
        ddecs = [None] * nch
        for c in reversed(range(nch)):
            dstb = dst.astype(BF16)
            dst_lane_s[:, c * HEAD_DIM:(c + 1) * HEAD_DIM] = dstb
            dst_rows_s[c * HEAD_DIM:(c + 1) * HEAD_DIM, :] = dstb
            ddecs[c] = jnp.sum(dst * stp_s[c], axis=0, keepdims=True) * decs[c]
            dst = dst * decs[c] + dq_all[:, c * HEAD_DIM:(c + 1) * HEAD_DIM]
        dst_s[...] = dst

        sc = jnp.where(tmask, _dot_nt(q_in, k_in), 0.0).astype(BF16)
        dkout = _dot(_block_diag(vb, nch, bdmask), dst_rows_s[...])
        dv = _dot_nt(kbd, dst_lane_s[...]) + _dot_tn(sc, do)
        dsc = jnp.where(tmask, _dot_nt(do, vb), 0.0).astype(BF16)
        dqin = _dot(dsc, k_in) + _dot(_block_diag(do, nch, bdmask), stp_rows_s[...])
        dkin = _dot_tn(dsc, q_in)

        dko = dkout * k_out32
        for c in range(nch):
            sl = slice(c * HG_CHUNK, (c + 1) * HG_CHUNK)
            dbl = jnp.sum(dko[sl], axis=0, keepdims=True) + ddecs[c]
            dbl_s[sl, :] = jnp.broadcast_to(dbl, (HG_CHUNK, HEAD_DIM))
        dq = dqin * eb
        dk = dkin * enb + dkout * ebl
        db = dqin * q_in32 - dkin * k_in32 - dko
        dlogf = _chunk_revcumsum(db, rows) + dbl_s[...]
        df = dlogf / f - dk
        dlb_ref[...] += jnp.sum(df * (1.0 - sg), axis=0, keepdims=True)
        dhg_ref[0] = (dq * dsilu_q).astype(BF16)
        dhg_ref[1] = (df * (1.0 - lb) * sg * (1.0 - sg)).astype(BF16)
        dhg_ref[2] = dv.astype(BF16)
        dhg_ref[3] = dhz.astype(BF16)

    last = nblk - 1
    return pl.pallas_call(
        body, name="hgrn_bwd", grid=(HEADS, nblk),
        in_specs=[pl.BlockSpec((4, rows, HEAD_DIM), lambda h, i: (0, last - i, h)),
                  pl.BlockSpec((rows, HEAD_DIM), lambda h, i: (last - i, h)),
                  pl.BlockSpec((rows, HEAD_DIM), lambda h, i: (last - i, h)),
                  pl.BlockSpec((None, None, HEAD_DIM, HEAD_DIM), lambda h, i: (h, last - i, 0, 0)),
                  pl.BlockSpec((2, HEAD_DIM), lambda h, i: (0, h)),
                  pl.BlockSpec((1, HEAD_DIM), lambda h, i: (0, 0))],
        out_specs=[pl.BlockSpec((4, rows, HEAD_DIM), lambda h, i: (0, last - i, h)),
                   pl.BlockSpec((1, HEAD_DIM), lambda h, i: (0, h)),
                   pl.BlockSpec((None, 1, HEAD_DIM), lambda h, i: (h, 0, 0))],
        out_shape=[jax.ShapeDtypeStruct((4, s, D_MODEL), BF16), jax.ShapeDtypeStruct((1, D_MODEL), F32),
                   jax.ShapeDtypeStruct((HEADS, 1, HEAD_DIM), F32)],
        scratch_shapes=[pltpu.VMEM((HEAD_DIM, HEAD_DIM), F32), pltpu.VMEM((nch, HEAD_DIM, HEAD_DIM), F32),
                        pltpu.VMEM((nch * HEAD_DIM, HEAD_DIM), BF16), pltpu.VMEM((nch * HEAD_DIM, HEAD_DIM), BF16),
                        pltpu.VMEM((HEAD_DIM, nch * HEAD_DIM), BF16), pltpu.VMEM((rows, HEAD_DIM), F32),
                        pltpu.VMEM((rows, rows), F32), pltpu.VMEM((rows, nch * HEAD_DIM), BF16)],
        compiler_params=_params(2),
    )(hg, o_pre, dya, st0, lb_logits, norm_g)


def _mla_pre(ms, q_a_g, kv_a_g, wuq3, wukv3, tabs):
    s = ms.shape[0]
    tm = min(TM_FUSED, s)

    def body(ms_ref, qg_ref, kvg_ref, wuq_ref, wukv_ref, c_ref, sa_ref, sb_ref,
             q_ref, k_ref, v_ref, cqn_ref, ckvn_ref):
        c, sa, sb = c_ref[...], sa_ref[...], sb_ref[...]
        cqn = _rms(ms_ref[:, 0:Q_LORA], qg_ref[...]).astype(BF16)
        ckvn = _rms(ms_ref[:, Q_LORA:Q_LORA + KV_LORA], kvg_ref[...]).astype(BF16)
        cqn_ref[...] = cqn
        ckvn_ref[...] = ckvn
        k_pe = _rope(ms_ref[:, Q_LORA + KV_LORA:MS_COLS], c, sa, sb).astype(BF16)
        for h in range(HEADS):
            qh = _dot(cqn, wuq_ref[h])
            q_ref[h, :, 0:128] = (qh[:, 0:128] * Q_PRESCALE).astype(BF16)
            q_ref[h, :, 128:256] = (_rope(qh[:, 128:256], c, sa, sb) * Q_PRESCALE).astype(BF16)
            kvh = _dot(ckvn, wukv_ref[h])
            k_ref[h, :, 0:128] = kvh[:, 0:128].astype(BF16)
            k_ref[h, :, 128:256] = k_pe
            v_ref[h] = kvh[:, 128:256].astype(BF16)

    tab = pl.BlockSpec((tm, 128), lambda i: (i, 0))
    return pl.pallas_call(
        body, name="mla_pre", grid=(s // tm,),
        in_specs=[pl.BlockSpec((tm, MS_COLS), lambda i: (i, 0)),
                  pl.BlockSpec((1, Q_LORA), lambda i: (0, 0)), pl.BlockSpec((1, KV_LORA), lambda i: (0, 0)),
                  pl.BlockSpec((HEADS, Q_LORA, QK_PAD), lambda i: (0, 0, 0)),
                  pl.BlockSpec((HEADS, KV_LORA, 256), lambda i: (0, 0, 0)), tab, tab, tab],
        out_specs=[pl.BlockSpec((HEADS, tm, QK_PAD), lambda i: (0, i, 0)),
                   pl.BlockSpec((HEADS, tm, QK_PAD), lambda i: (0, i, 0)),
                   pl.BlockSpec((HEADS, tm, HEAD_DIM), lambda i: (0, i, 0)),
                   pl.BlockSpec((tm, Q_LORA), lambda i: (i, 0)), pl.BlockSpec((tm, KV_LORA), lambda i: (i, 0))],
        out_shape=[jax.ShapeDtypeStruct((HEADS, s, QK_PAD), BF16), jax.ShapeDtypeStruct((HEADS, s, QK_PAD), BF16),
                   jax.ShapeDtypeStruct((HEADS, s, HEAD_DIM), BF16),
                   jax.ShapeDtypeStruct((s, Q_LORA), BF16), jax.ShapeDtypeStruct((s, KV_LORA), BF16)],
        compiler_params=_params(1),
    )(ms, q_a_g, kv_a_g, wuq3, wukv3, *tabs)


def _causal_mask(t):
    r = lax.broadcasted_iota(jnp.int32, (t, t), 0)
    c = lax.broadcasted_iota(jnp.int32, (t, t), 1)
    return r >= c


def _flash_fwd(q, k, v, mz):
    s = q.shape[1]
    t = min(TQ, s)

    def body(q_ref, k_ref, v_ref, mz_ref, o_ref, yb_ref, lse_ref, m_s, l_s, acc_s):
        i = pl.program_id(1)
        m_s[...] = jnp.full_like(m_s, -jnp.inf)
        l_s[...] = jnp.zeros_like(l_s)
        acc_s[...] = jnp.zeros_like(acc_s)

        def step(j, masked):
            rows = pl.ds(pl.multiple_of(j * t, t), t)
            for hh in range(FLASH_HEADS):
                sc = _dot_nt(q_ref[hh], k_ref[hh, rows, :])
                if masked:
                    sc = jnp.where(_causal_mask(t), sc, -jnp.inf)
                m_prev = m_s[hh]
                m_new = jnp.maximum(m_prev, jnp.max(sc, axis=-1, keepdims=True))
                p = jnp.exp2(sc - jnp.tile(m_new, (1, t // 128)))
                alpha = jnp.exp2(m_prev - m_new)
                l_s[hh] = alpha * l_s[hh] + jnp.sum(p, axis=-1, keepdims=True)
                acc_s[hh] = alpha * acc_s[hh] + _dot(p.astype(BF16), v_ref[hh, rows, :])
                m_s[hh] = m_new

        def loop_body(j, carry):
            step(j, False)
            return carry

        lax.fori_loop(0, i, loop_body, 0)
        step(i, True)
        for hh in range(FLASH_HEADS):
            cols = slice(hh * HEAD_DIM, (hh + 1) * HEAD_DIM)
            out = acc_s[hh] / l_s[hh]
            o_ref[:, cols] = out
            silu_z, _ = _silu_parts(mz_ref[:, cols])
            yb_ref[:, cols] = (out * silu_z).astype(BF16)
            lse_ref[hh] = m_s[hh] + jnp.log2(l_s[hh])

    nh = FLASH_HEADS
    col = pl.BlockSpec((t, nh * HEAD_DIM), lambda h, i: (i, h))
    return pl.pallas_call(
        body, name="flash_fwd", grid=(HEADS // nh, s // t),
        in_specs=[pl.BlockSpec((nh, t, QK_PAD), lambda h, i: (h, i, 0)),
                  pl.BlockSpec((nh, s, QK_PAD), lambda h, i: (h, 0, 0)),
                  pl.BlockSpec((nh, s, HEAD_DIM), lambda h, i: (h, 0, 0)), col],
        out_specs=[col, col, pl.BlockSpec((nh, t, 128), lambda h, i: (h, i, 0))],
        out_shape=[jax.ShapeDtypeStruct((s, D_MODEL), F32), jax.ShapeDtypeStruct((s, D_MODEL), BF16),
                   jax.ShapeDtypeStruct((HEADS, s, 128), F32)],
        scratch_shapes=[pltpu.VMEM((nh, t, 128), F32), pltpu.VMEM((nh, t, 128), F32),
                        pltpu.VMEM((nh, t, HEAD_DIM), F32)],
        compiler_params=_params(2),
    )(q, k, v, mz)


def _flash_bwd(q, k, v, dyb, mz, o_att, lse, tabs):
    s = q.shape[1]
    t = min(TQ, s)

    def body(q_ref, k_ref, v_ref, dyb_ref, mz_ref, o_ref, lse_ref, c_ref, sa_ref, sb_ref,
             dq_ref, dk_ref, dv_ref, dmz_ref, dq_s, delta_s):
        i = pl.program_id(1)

        @pl.when(i == 0)
        def _():
            dk_ref[...] = jnp.zeros_like(dk_ref)
            dv_ref[...] = jnp.zeros_like(dv_ref)

        silu_z, dsilu_z = _silu_parts(mz_ref[...])
        dyb_v = dyb_ref[...]
        out = o_ref[...]
        do32 = dyb_v * silu_z
        dmz_ref[...] = (dyb_v * out * dsilu_z).astype(BF16)
        delta_s[...] = jnp.broadcast_to(jnp.sum(do32 * out, axis=-1, keepdims=True), (t, 128))
        do = do32.astype(BF16)
        qv = q_ref[...]
        dq_s[...] = jnp.zeros_like(dq_s)

        def step(j, masked):
            rows = pl.ds(pl.multiple_of(j * t, t), t)
            kj = k_ref[rows, :]
            vj = v_ref[rows, :]
            sc = _dot_nt(qv, kj)
            if masked:
                sc = jnp.where(_causal_mask(t), sc, -jnp.inf)
            p = jnp.exp2(sc - jnp.tile(lse_ref[...], (1, t // 128)))
            dp = _dot_nt(do, vj)
            ds = (p * (dp - jnp.tile(delta_s[...], (1, t // 128)))).astype(BF16)
            dv_ref[rows, :] += _dot_tn(p.astype(BF16), do)
            dk_ref[rows, :] += _dot_tn(ds, qv)
            dq_s[...] += _dot(ds, kj)

        def loop_body(j, carry):
            step(j, False)
            return carry

        lax.fori_loop(0, i, loop_body, 0)
        step(i, True)
        dq = dq_s[...] * ATT_SCALE
        dq_ref[:, 0:128] = dq[:, 0:128].astype(BF16)
        dq_ref[:, 128:256] = _rope_bwd(dq[:, 128:256], c_ref[...], sa_ref[...], sb_ref[...]).astype(BF16)

    col = pl.BlockSpec((t, HEAD_DIM), lambda h, i: (i, h))
    tab = pl.BlockSpec((t, 128), lambda h, i: (i, 0))
    return pl.pallas_call(
        body, name="flash_bwd", grid=(HEADS, s // t),
        in_specs=[pl.BlockSpec((None, t, QK_PAD), lambda h, i: (h, i, 0)),
                  pl.BlockSpec((None, s, QK_PAD), lambda h, i: (h, 0, 0)),
                  pl.BlockSpec((None, s, HEAD_DIM), lambda h, i: (h, 0, 0)),
                  col, col, col, pl.BlockSpec((None, t, 128), lambda h, i: (h, i, 0)), tab, tab, tab],
        out_specs=[pl.BlockSpec((None, t, QK_PAD), lambda h, i: (h, i, 0)),
                   pl.BlockSpec((None, s, QK_PAD), lambda h, i: (h, 0, 0)),
                   pl.BlockSpec((None, s, HEAD_DIM), lambda h, i: (h, 0, 0)), col],
        out_shape=[jax.ShapeDtypeStruct((HEADS, s, QK_PAD), BF16), jax.ShapeDtypeStruct((HEADS, s, QK_PAD), F32),
                   jax.ShapeDtypeStruct((HEADS, s, HEAD_DIM), F32), jax.ShapeDtypeStruct((s, D_MODEL), BF16)],
        scratch_shapes=[pltpu.VMEM((t, QK_PAD), F32), pltpu.VMEM((t, 128), F32)], compiler_params=_params(2),
    )(q, k, v, dyb, mz, o_att, lse, *tabs)


def _mla_bwd_proj(dq, dk, dv, cqn, ckvn, ms, q_a_g, kv_a_g, wuq3, wukv3, tabs):
    s = ms.shape[0]
    tm = min(TM_FUSED, s)

    def body(dq_ref, dk_ref, dv_ref, cqn_ref, ckvn_ref, ms_ref, qg_ref, kvg_ref, wuq_ref, wukv_ref,
             c_ref, sa_ref, sb_ref, dms_ref, dwuq_ref, dwukv_ref, dqg_ref, dkvg_ref):
        @pl.when(pl.program_id(0) == 0)
        def _():
            dwuq_ref[...] = jnp.zeros_like(dwuq_ref)
            dwukv_ref[...] = jnp.zeros_like(dwukv_ref)
            dqg_ref[...] = jnp.zeros_like(dqg_ref)
            dkvg_ref[...] = jnp.zeros_like(dkvg_ref)

        cqn = cqn_ref[...]
        ckvn = ckvn_ref[...]
        dcqn = jnp.zeros((tm, Q_LORA), F32)
        dckvn = jnp.zeros((tm, KV_LORA), F32)
        dkpe = jnp.zeros((tm, 128), F32)
        for h in range(HEADS):
            dqh = dq_ref[h]
            dcqn += _dot_nt(dqh, wuq_ref[h])
            dwuq_ref[h] += _dot_tn(cqn, dqh)
            dkh = dk_ref[h] * LN2
            dkvh = jnp.concatenate([dkh[:, 0:128], dv_ref[h]], axis=1).astype(BF16)
            dckvn += _dot_nt(dkvh, wukv_ref[h])
            dwukv_ref[h] += _dot_tn(ckvn, dkvh)
            dkpe += dkh[:, 128:256]
        dcq, dqg_rows = _rms_bwd(ms_ref[:, 0:Q_LORA], qg_ref[...], dcqn)
        dckv, dkvg_rows = _rms_bwd(ms_ref[:, Q_LORA:Q_LORA + KV_LORA], kvg_ref[...], dckvn)
        dqg_ref[...] += jnp.sum(dqg_rows, axis=0, keepdims=True)
        dkvg_ref[...] += jnp.sum(dkvg_rows, axis=0, keepdims=True)
        dms_ref[:, 0:Q_LORA] = dcq.astype(BF16)
        dms_ref[:, Q_LORA:Q_LORA + KV_LORA] = dckv.astype(BF16)
        dms_ref[:, Q_LORA + KV_LORA:MS_COLS] = _rope_bwd(dkpe, c_ref[...], sa_ref[...], sb_ref[...]).astype(BF16)

    tab = pl.BlockSpec((tm, 128), lambda i: (i, 0))
    wq = pl.BlockSpec((HEADS, Q_LORA, QK_PAD), lambda i: (0, 0, 0))
    wkv = pl.BlockSpec((HEADS, KV_LORA, 256), lambda i: (0, 0, 0))
    qg = pl.BlockSpec((1, Q_LORA), lambda i: (0, 0))
    kvg = pl.BlockSpec((1, KV_LORA), lambda i: (0, 0))
    return pl.pallas_call(
        body, name="mla_bwd_proj", grid=(s // tm,),
        in_specs=[pl.BlockSpec((HEADS, tm, QK_PAD), lambda i: (0, i, 0)),
                  pl.BlockSpec((HEADS, tm, QK_PAD), lambda i: (0, i, 0)),
                  pl.BlockSpec((HEADS, tm, HEAD_DIM), lambda i: (0, i, 0)),
                  pl.BlockSpec((tm, Q_LORA), lambda i: (i, 0)), pl.BlockSpec((tm, KV_LORA), lambda i: (i, 0)),
                  pl.BlockSpec((tm, MS_COLS), lambda i: (i, 0)), qg, kvg, wq, wkv, tab, tab, tab],
        out_specs=[pl.BlockSpec((tm, MS_COLS), lambda i: (i, 0)), wq, wkv, qg, kvg],
        out_shape=[jax.ShapeDtypeStruct((s, MS_COLS), BF16), jax.ShapeDtypeStruct((HEADS, Q_LORA, QK_PAD), F32),
                   jax.ShapeDtypeStruct((HEADS, KV_LORA, 256), F32),
                   jax.ShapeDtypeStruct((1, Q_LORA), F32), jax.ShapeDtypeStruct((1, KV_LORA), F32)],
        compiler_params=_params(1),
    )(dq, dk, dv, cqn, ckvn, ms, q_a_g, kv_a_g, wuq3, wukv3, *tabs)


def _merge_loss(ya, yb, glog, b_gate, x, tgt, fg, wpa, wpb, wout):
    s = x.shape[0]
    tm = min(TM_FUSED, s)

    def body(ya_ref, yb_ref, g0_ref, g1_ref, b0_ref, b1_ref, x_ref, t_ref, fg_ref, wpa_ref, wpb_ref, wout_ref,
             mg_ref, pa_ref, pb_ref, dx2_ref, loss_ref, dfg_ref):
        @pl.when(pl.program_id(0) == 0)
        def _():
            loss_ref[...] = jnp.zeros_like(loss_ref)
            dfg_ref[...] = jnp.zeros_like(dfg_ref)

        pa = _dot(ya_ref[...], wpa_ref[...])
        pb = _dot(yb_ref[...], wpb_ref[...])
        pa_ref[...] = pa
        pb_ref[...] = pb
        merged = (jax.nn.sigmoid(g0_ref[...] + b0_ref[...]) * pa
                  + jax.nn.sigmoid(g1_ref[...] + b1_ref[...]) * pb).astype(BF16)
        mg_ref[...] = merged
        x2 = x_ref[...] + _dot(merged, wout_ref[...])
        fg_v = fg_ref[...]
        err = _rms(x2, fg_v) - t_ref[...]
        loss_ref[...] += 0.5 * jnp.sum(jnp.mean(err * err, axis=-1, keepdims=True), axis=0, keepdims=True)
        dx2, dfg_rows = _rms_bwd(x2, fg_v, err * (1.0 / D_MODEL))
        dx2_ref[...] = dx2
        dfg_ref[...] += jnp.sum(dfg_rows, axis=0, keepdims=True)

    row = pl.BlockSpec((tm, D_MODEL), lambda i: (i, 0))
    row1 = pl.BlockSpec((tm, D_MODEL), lambda i: (i, 1))
    vec = pl.BlockSpec((1, D_MODEL), lambda i: (0, 0))
    vec1 = pl.BlockSpec((1, D_MODEL), lambda i: (0, 1))
    wsp = pl.BlockSpec((D_MODEL, D_MODEL), lambda i: (0, 0))
    return pl.pallas_call(
        body, name="merge_loss", grid=(s // tm,),
        in_specs=[row, row, row, row1, vec, vec1, row, row, vec, wsp, wsp, wsp],
        out_specs=[row, row, row, row, pl.BlockSpec((1, 128), lambda i: (0, 0)), vec],
        out_shape=[jax.ShapeDtypeStruct((s, D_MODEL), BF16), jax.ShapeDtypeStruct((s, D_MODEL), F32),
                   jax.ShapeDtypeStruct((s, D_MODEL), F32), jax.ShapeDtypeStruct((s, D_MODEL), F32),
                   jax.ShapeDtypeStruct((1, 128), F32), jax.ShapeDtypeStruct((1, D_MODEL), F32)],
        compiler_params=_params(1),
    )(ya, yb, glog, glog, b_gate, b_gate, x, tgt, fg, wpa, wpb, wout)


def _merge_bwd(dx2, pa, pb, glog, b_gate, wpa, wpb, wout):
    s = dx2.shape[0]
    tm = min(TM_FUSED, s)

    def body(dx2_ref, pa_ref, pb_ref, g0_ref, g1_ref, b0_ref, b1_ref, wpa_ref, wpb_ref, wout_ref,
             dya_ref, dyb_ref, dgl_ref, dpa_ref, dpb_ref, dx2b_ref, dbg_ref):
        @pl.when(pl.program_id(0) == 0)
        def _():
            dbg_ref[...] = jnp.zeros_like(dbg_ref)

        dx2b = dx2_ref[...].astype(BF16)
        dx2b_ref[...] = dx2b
        dmg = _dot_nt(dx2b, wout_ref[...])
        g0 = jax.nn.sigmoid(g0_ref[...] + b0_ref[...])
        g1 = jax.nn.sigmoid(g1_ref[...] + b1_ref[...])
        dpa = (dmg * g0).astype(BF16)
        dpb = (dmg * g1).astype(BF16)
        dpa_ref[...] = dpa
        dpb_ref[...] = dpb
        dgl0 = dmg * pa_ref[...] * g0 * (1.0 - g0)
        dgl1 = dmg * pb_ref[...] * g1 * (1.0 - g1)
        dgl_ref[:, 0:D_MODEL] = dgl0.astype(BF16)
        dgl_ref[:, D_MODEL:2 * D_MODEL] = dgl1.astype(BF16)
        dbg_ref[:, 0:D_MODEL] += jnp.sum(dgl0, axis=0, keepdims=True)
        dbg_ref[:, D_MODEL:2 * D_MODEL] += jnp.sum(dgl1, axis=0, keepdims=True)
        dya_ref[...] = _dot_nt(dpa, wpa_ref[...])
        dyb_ref[...] = _dot_nt(dpb, wpb_ref[...])

    row = pl.BlockSpec((tm, D_MODEL), lambda i: (i, 0))
    row1 = pl.BlockSpec((tm, D_MODEL), lambda i: (i, 1))
    row2 = pl.BlockSpec((tm, 2 * D_MODEL), lambda i: (i, 0))
    vec = pl.BlockSpec((1, D_MODEL), lambda i: (0, 0))
    vec1 = pl.BlockSpec((1, D_MODEL), lambda i: (0, 1))
    vec2 = pl.BlockSpec((1, 2 * D_MODEL), lambda i: (0, 0))
    wsp = pl.BlockSpec((D_MODEL, D_MODEL), lambda i: (0, 0))
    return pl.pallas_call(
        body, name="merge_bwd", grid=(s // tm,),
        in_specs=[row, row, row, row, row1, vec, vec1, wsp, wsp, wsp],
        out_specs=[row, row, row2, row, row, row, vec2],
        out_shape=[jax.ShapeDtypeStruct((s, D_MODEL), F32), jax.ShapeDtypeStruct((s, D_MODEL), F32),
                   jax.ShapeDtypeStruct((s, 2 * D_MODEL), BF16), jax.ShapeDtypeStruct((s, D_MODEL), BF16),
                   jax.ShapeDtypeStruct((s, D_MODEL), BF16), jax.ShapeDtypeStruct((s, D_MODEL), BF16),
                   jax.ShapeDtypeStruct((1, 2 * D_MODEL), F32)],
        compiler_params=_params(1),
    )(dx2, pa, pb, glog, glog, b_gate, b_gate, wpa, wpb, wout)


def _local_step(x, tgt, w_int, w_uq, w_ukv, wproj, norm_g, b_gate, lb_logits, hg_norm_g, q_a_g, kv_a_g, fg):
    s = x.shape[0]
    w_ms = jnp.concatenate([w_int[4096:4800], jnp.zeros((64, D_MODEL), BF16)], axis=0)
    w_mz = w_int[4800:5824]
    w_gl = w_int[5824:7872]
    wuq3 = jnp.pad(w_uq.reshape(Q_LORA, HEADS, QK_DIM).transpose(1, 0, 2), ((0, 0), (0, 0), (0, QK_PAD - QK_DIM)))
    wukv3 = w_ukv.reshape(KV_LORA, HEADS, 256).transpose(1, 0, 2)
    wpa, wpb, wout = wproj[0], wproj[1], wproj[2]
    tabs = _rope_tables(s)

    h = _norm_in(x, norm_g)
    hg = _mm_slabs_out(h, w_int, 4, name="proj_hg")
    ms = _mm(h, w_ms, trans_b=True, name="proj_ms")
    mz = _mm(h, w_mz, trans_b=True, name="proj_mz")
    glog = _mm(h, w_gl, trans_b=True, name="proj_gate")
    o_pre, ya, st0 = _hgrn_fwd(hg, lb_logits, hg_norm_g)
    q, k, v, cqn, ckvn = _mla_pre(ms, q_a_g, kv_a_g, wuq3, wukv3, tabs)
    o_att, yb, lse = _flash_fwd(q, k, v, mz)
    merged, pa, pb, dx2, loss, dfg = _merge_loss(ya, yb, glog, b_gate, x, tgt, fg, wpa, wpb, wout)

    dya, dyb, dglog, dpa, dpb, dx2b, dbg = _merge_bwd(dx2, pa, pb, glog, b_gate, wpa, wpb, wout)
    d_wout = _mm_tn(merged, dx2b, name="dw_out")
    d_wpa = _mm_tn(ya, dpa, name="dw_proj_a")
    d_wpb = _mm_tn(yb, dpb, name="dw_proj_b")
    dhg, dlb, dhgg = _hgrn_bwd(hg, o_pre, dya, st0, lb_logits, hg_norm_g)
    dq, dk, dv, dmz = _flash_bwd(q, k, v, dyb, mz, o_att, lse, tabs)
    dms, d_wuq3, d_wukv3, dqg, dkvg = _mla_bwd_proj(dq, dk, dv, cqn, ckvn, ms, q_a_g, kv_a_g, wuq3, wukv3, tabs)
    d_hg = _mm_tn(dhg, h, name="dw_in_hg")
    d_ms = _mm_tn(dms, h, name="dw_in_ms")
    d_mz = _mm_tn(dmz, h, name="dw_in_mz")
    d_gl = _mm_tn(dglog, h, name="dw_in_gate")
    dh = _mm_slabs_in(dhg, w_int, name="dh_hg")
    dh = _mm(dms, w_ms, add=dh, name="dh_ms")
    dh = _mm(dmz, w_mz, add=dh, name="dh_mz")
    dh = _mm(dglog, w_gl, add=dh, name="dh_gate")
    grad_x, dng = _norm_in_bwd(x, norm_g, dh, dx2)

    d_w_int = jnp.concatenate([d_hg.reshape(4 * D_MODEL, D_MODEL), d_ms[0:704], d_mz, d_gl], axis=0)
    small = {"norm_g": dng, "b_gate": dbg, "lb": dlb, "hg_norm_g": dhgg, "q_a_g": dqg, "kv_a_g": dkvg,
             "final_norm_g": dfg}
    return loss, grad_x, d_w_int, d_wuq3, d_wukv3, (d_wpa, d_wpb, d_wout), small


def _pack_rest(w_uq_b, w_ukv_b, wpa_b, wpb_b, wout_b):
    return jnp.concatenate([w_uq_b.reshape(144, D_MODEL), w_ukv_b.reshape(128, D_MODEL), wpa_b, wpb_b, wout_b], axis=0)


def _unpack_rest(p):
    return (p[0:144].reshape(Q_LORA, 384), p[144:272].reshape(KV_LORA, 512), p[272:528], p[528:784], p[784:1040])


def _pack_rest_grads(d_wuq3, d_wukv3, d_proj):
    d_wuq = d_wuq3.transpose(1, 0, 2)[:, :, 0:QK_DIM].reshape(Q_LORA, HEADS * QK_DIM)
    d_wukv = d_wukv3.transpose(1, 0, 2).reshape(KV_LORA, HEADS * 256)
    blocks = []
    for b in range(N_CHIPS):
        rows = slice(b * 256, (b + 1) * 256)
        blocks.append(_pack_rest(d_wuq[:, b * 384:(b + 1) * 384], d_wukv[:, b * 512:(b + 1) * 512],
                                 d_proj[0][rows], d_proj[1][rows], d_proj[2][rows]))
    return jnp.stack(blocks, axis=0)


def _unpack_rest_weights(g):
    parts = [_unpack_rest(g[b]) for b in range(N_CHIPS)]
    w_uq, w_ukv = (jnp.concatenate([p[n] for p in parts], axis=1) for n in range(2))
    wproj = jnp.stack([jnp.concatenate([p[n] for p in parts], axis=0) for n in range(2, 5)], axis=0)
    return w_uq, w_ukv, wproj


MESH_ID = pl.DeviceIdType.MESH
ANY = pl.BlockSpec(memory_space=pl.ANY)
HALF_COLS = D_MODEL // 2


def _me():
    return lax.axis_index("x"), lax.axis_index("y"), lax.axis_index("c")


def _other_chips(x, y):
    return [(1 - x, y), (x, 1 - y), (1 - x, 1 - y)]


def _cols(c):
    return pl.ds(c * HALF_COLS, HALF_COLS)


def _gather_weights(w_blk, r_blk):
    def body(w_ref, r_ref, ow_ref, or_ref, send_sems, recv_sems, local_sems):
        x, y, c = _me()
        chips = _other_chips(x, y)
        me = 2 * x + y
        pairs = [(w_ref, ow_ref), (r_ref, or_ref)]

        def copy(k, src, dst, to):
            return pltpu.make_async_remote_copy(src_ref=src, dst_ref=dst, send_sem=send_sems.at[k],
                                                recv_sem=recv_sems.at[k], device_id=to, device_id_type=MESH_ID)

        local = [pltpu.make_async_copy(src, dst.at[me], local_sems.at[a]) for a, (src, dst) in enumerate(pairs)]
        for cp in local:
            cp.start()
        first = [copy(6 * a + j, src.at[:, _cols(c)], dst.at[me, :, _cols(c)], (cx, cy, c))
                 for a, (src, dst) in enumerate(pairs) for j, (cx, cy) in enumerate(chips)]
        for cp in first:
            cp.start()
        passed = []
        for a, (src, dst) in enumerate(pairs):
            for j, (cx, cy) in enumerate(chips):
                landed = dst.at[2 * cx + cy, :, _cols(c)]
                copy(6 * a + j, landed, landed, (cx, cy, c)).wait_recv()
                fwd = copy(6 * a + 3 + j, landed, landed, (x, y, 1 - c))
                fwd.start()
                passed.append(fwd)
        for a, (src, dst) in enumerate(pairs):
            for j, (cx, cy) in enumerate(chips):
                theirs = dst.at[2 * cx + cy, :, _cols(1 - c)]
                copy(6 * a + 3 + j, theirs, theirs, (x, y, 1 - c)).wait_recv()
        for cp in first + passed:
            cp.wait_send()
        for cp in local:
            cp.wait()

    return pl.pallas_call(
        body, name="gather_weights", in_specs=[ANY, ANY], out_specs=[ANY, ANY],
        out_shape=[jax.ShapeDtypeStruct((N_CHIPS,) + w_blk.shape, w_blk.dtype),
                   jax.ShapeDtypeStruct((N_CHIPS,) + r_blk.shape, r_blk.dtype)],
        scratch_shapes=[pltpu.SemaphoreType.DMA((12,)), pltpu.SemaphoreType.DMA((12,)), pltpu.SemaphoreType.DMA((2,))],
    )(w_blk, r_blk)


def _swap_halves(gw, gr):
    def body(gw_ref, gr_ref, lw_ref, lr_ref, send_sems, recv_sems):
        x, y, c = _me()
        cps = [pltpu.make_async_remote_copy(
            src_ref=src, dst_ref=dst, send_sem=send_sems.at[a], recv_sem=recv_sems.at[a],
            device_id=(x, y, 1 - c), device_id_type=MESH_ID)
            for a, (src, dst) in enumerate([(gw_ref.at[:, _cols(1 - c)], lw_ref),
                                            (gr_ref.at[:, :, _cols(1 - c)], lr_ref)])]
        for cp in cps:
            cp.start()
        for cp in cps:
            cp.wait()

    return pl.pallas_call(
        body, name="grad_swap_halves", in_specs=[ANY, ANY], out_specs=[ANY, ANY],
        out_shape=[jax.ShapeDtypeStruct((gw.shape[0], HALF_COLS), gw.dtype),
                   jax.ShapeDtypeStruct(gr.shape[:2] + (HALF_COLS,), gr.dtype)],
        scratch_shapes=[pltpu.SemaphoreType.DMA((2,)), pltpu.SemaphoreType.DMA((2,))],
    )(gw, gr)


def _scatter_blocks(hw, hr):
    nw, nr = hw.shape[0] // N_CHIPS, hr.shape[0] // N_CHIPS

    def body(hw_ref, hr_ref, lw_ref, lr_ref, send_sems, recv_sems):
        x, y, c = _me()
        cps = []
        for a, (src, dst, n) in enumerate([(hw_ref, lw_ref, nw), (hr_ref, lr_ref, nr)]):
            for j, (cx, cy) in enumerate(_other_chips(x, y)):
                cps.append(pltpu.make_async_remote_copy(
                    src_ref=src.at[pl.ds((2 * cx + cy) * n, n), :], dst_ref=dst.at[j], send_sem=send_sems.at[3 * a + j],
                    recv_sem=recv_sems.at[3 * a + j], device_id=(cx, cy, c), device_id_type=MESH_ID))
        for cp in cps:
            cp.start()
        for cp in cps:
            cp.wait()

    return pl.pallas_call(
        body, name="grad_scatter_blocks", in_specs=[ANY, ANY], out_specs=[ANY, ANY],
        out_shape=[jax.ShapeDtypeStruct((3, nw, HALF_COLS), hw.dtype), jax.ShapeDtypeStruct((3, nr, HALF_COLS), hr.dtype)],
        scratch_shapes=[pltpu.SemaphoreType.DMA((6,)), pltpu.SemaphoreType.DMA((6,))],
    )(hw, hr)


def _join_halves(rw, rr):
    def body(rw_ref, rr_ref, ow_ref, or_ref, send_sems, recv_sems, local_sems):
        x, y, c = _me()
        cps, local = [], []
        for a, (src, dst) in enumerate([(rw_ref, ow_ref), (rr_ref, or_ref)]):
            local.append(pltpu.make_async_copy(src, dst.at[:, _cols(c)], local_sems.at[a]))
            cps.append(pltpu.make_async_remote_copy(
                src_ref=src, dst_ref=dst.at[:, _cols(c)], send_sem=send_sems.at[a], recv_sem=recv_sems.at[a],
                device_id=(x, y, 1 - c), device_id_type=MESH_ID))
        for cp in local + cps:
            cp.start()
        for cp in cps + local:
            cp.wait()

    return pl.pallas_call(
        body, name="grad_join_halves", in_specs=[ANY, ANY], out_specs=[ANY, ANY],
        out_shape=[jax.ShapeDtypeStruct((rw.shape[0], D_MODEL), rw.dtype), jax.ShapeDtypeStruct((rr.shape[0], D_MODEL), rr.dtype)],
        scratch_shapes=[pltpu.SemaphoreType.DMA((2,)), pltpu.SemaphoreType.DMA((2,)), pltpu.SemaphoreType.DMA((2,))],
    )(rw, rr)


def _gather_small(vec):
    def body(v_ref, out_ref, send_sems, recv_sems, local_sem):
        x, y, c = _me()
        my_id = 4 * x + 2 * y + c
        mine = pltpu.make_async_copy(v_ref, out_ref.at[my_id], local_sem)
        mine.start()
        cps = []
        for r in range(1, N_DEV):
            peer = (x ^ (r >> 2), y ^ ((r >> 1) & 1), c ^ (r & 1))
            cps.append(pltpu.make_async_remote_copy(
                src_ref=v_ref, dst_ref=out_ref.at[my_id], send_sem=send_sems.at[r - 1],
                recv_sem=recv_sems.at[r - 1], device_id=peer, device_id_type=MESH_ID))
        for cp in cps:
            cp.start()
        for cp in cps:
            cp.wait()
        mine.wait()

    return pl.pallas_call(
        body, name="gather_small", in_specs=[ANY], out_specs=ANY,
        out_shape=jax.ShapeDtypeStruct((N_DEV, 1, SMALL_COLS), vec.dtype),
        scratch_shapes=[pltpu.SemaphoreType.DMA((N_DEV - 1,)), pltpu.SemaphoreType.DMA((N_DEV - 1,)),
                        pltpu.SemaphoreType.DMA],
    )(vec)


def _add_cores(c_idx, g, landed, *, tm, name):
    r = g.shape[0]

    def body(c_ref, g_ref, l_ref, o32_ref, o16_ref):
        acc = g_ref[...] + l_ref[...]
        o32_ref[...] = acc
        o16_ref[...] = acc.astype(BF16)

    half = pl.BlockSpec((tm, HALF_COLS), lambda i, c_ref: (i, 0))
    grid_spec = pltpu.PrefetchScalarGridSpec(
        num_scalar_prefetch=1, grid=(r // tm,),
        in_specs=[pl.BlockSpec((tm, HALF_COLS), lambda i, c_ref: (i, c_ref[0])), half], out_specs=[half, half])
    return pl.pallas_call(
        body, name=name, grid_spec=grid_spec,
        out_shape=[jax.ShapeDtypeStruct((r, HALF_COLS), F32), jax.ShapeDtypeStruct((r, HALF_COLS), BF16)],
        compiler_params=_params(1),
    )(c_idx, g, landed)


def _add_chips(chip_idx, h32, landed, *, tm, name):
    n = landed.shape[1]
    per = n // tm

    def body(chip_ref, h_ref, l_ref, o_ref):
        acc = h_ref[...]
        for j in range(3):
            acc = acc + l_ref[j].astype(F32)
        o_ref[...] = acc

    grid_spec = pltpu.PrefetchScalarGridSpec(
        num_scalar_prefetch=1, grid=(per,),
        in_specs=[pl.BlockSpec((tm, HALF_COLS), lambda i, chip_ref: (chip_ref[0] * per + i, 0)),
                  pl.BlockSpec((3, tm, HALF_COLS), lambda i, chip_ref: (0, i, 0))],
        out_specs=pl.BlockSpec((tm, HALF_COLS), lambda i, chip_ref: (i, 0)))
    return pl.pallas_call(
        body, name=name, grid_spec=grid_spec, out_shape=jax.ShapeDtypeStruct((n, HALF_COLS), F32),
        compiler_params=_params(1),
    )(chip_idx, h32, landed)


def _pack_small(small, lb_logits, loss):
    def body(ng_ref, bg_ref, dlb_ref, lbl_ref, hgg_ref, qg_ref, kvg_ref, fg_ref, loss_ref, out_ref):
        out_ref[...] = jnp.zeros_like(out_ref)
        out_ref[:, 0:1024] = ng_ref[...]
        out_ref[:, 1024:3072] = bg_ref[...]
        _, p0p1 = _lower_bound(lbl_ref[...])
        dl0 = dlb_ref[...] * p0p1
        out_ref[:, 3072:4096] = dl0
        out_ref[:, 4096:5120] = -dl0
        hgg = hgg_ref[0]
        for h in range(1, HEADS):
            hgg = hgg + hgg_ref[h]
        out_ref[:, 5120:5248] = hgg
        out_ref[:, 5248:5632] = qg_ref[...]
        out_ref[:, 5632:5888] = kvg_ref[...]
        out_ref[:, 5888:6912] = fg_ref[...]
        out_ref[:, 6912:7040] = loss_ref[...]

    return pl.pallas_call(
        body, name="pack_small", out_shape=jax.ShapeDtypeStruct((1, SMALL_COLS), F32),
    )(small["norm_g"], small["b_gate"], small["lb"], lb_logits, small["hg_norm_g"], small["q_a_g"],
      small["kv_a_g"], small["final_norm_g"], loss)


def _sum_small(gathered):
    def body(g_ref, out_ref):
        acc = g_ref[0]
        for d in range(1, N_DEV):
            acc = acc + g_ref[d]
        out_ref[...] = acc

    return pl.pallas_call(
        body, name="sum_small", out_shape=jax.ShapeDtypeStruct((1, SMALL_COLS), F32),
    )(gathered)


def _adamw(w, g, m, v, *, name, tm):
    r, cols = w.shape
    c1 = 1.0 - ADAM_B1 ** ADAM_STEP
    c2 = 1.0 - ADAM_B2 ** ADAM_STEP

    def body(w_ref, g_ref, m_ref, v_ref, d_ref, nm_ref, nv_ref):
        gv = g_ref[...]
        nm = ADAM_B1 * m_ref[...] + (1.0 - ADAM_B1) * gv
        nv = ADAM_B2 * v_ref[...] + (1.0 - ADAM_B2) * (gv * gv)
        nm_ref[...] = nm
        nv_ref[...] = nv
        d_ref[...] = -ADAM_LR * ((nm / c1) / (jnp.sqrt(nv / c2) + ADAM_EPS) + ADAM_WD * w_ref[...])

    row = pl.BlockSpec((tm, cols), lambda i: (i, 0))
    shp = jax.ShapeDtypeStruct((r, cols), F32)
    return pl.pallas_call(
        body, name=name, grid=(r // tm,), in_specs=[row] * 4, out_specs=[row] * 3, out_shape=[shp] * 3,
        compiler_params=_params(1),
    )(w, g, m, v)


def _small_vec(norm_g, b_gate, lb_logits, hg_norm_g, q_a_g, kv_a_g, fg):
    parts = [norm_g.reshape(1, -1), b_gate.reshape(1, -1), lb_logits.reshape(1, -1), hg_norm_g.reshape(1, -1),
             q_a_g.reshape(1, -1), kv_a_g.reshape(1, -1), fg.reshape(1, -1), jnp.zeros((1, SMALL_COLS - 6912), F32)]
    return jnp.concatenate(parts, axis=1)


def _split_small(vec):
    v = vec.reshape(-1)
    return (v[0:1024].reshape(1, 1024), v[1024:3072].reshape(1, 2048), v[3072:5120].reshape(2, 1024),
            v[5120:5248].reshape(1, 128), v[5248:5632].reshape(1, 384), v[5632:5888].reshape(1, 256), v[5888:6912])


def kernel(x, norm_g, w_in, b_gate, lb_logits, hg_norm_g, q_a_g, w_uq, kv_a_g, w_ukv, w_proj_a, w_proj_b, w_out, final_norm_g, loss_target, m_norm_g, m_w_in, m_b_gate, m_lb_logits, m_hg_norm_g, m_q_a_g, m_w_uq, m_kv_a_g, m_w_ukv, m_w_proj_a, m_w_proj_b, m_w_out, m_final_norm_g, v_norm_g, v_w_in, v_b_gate, v_lb_logits, v_hg_norm_g, v_q_a_g, v_w_uq, v_kv_a_g, v_w_ukv, v_w_proj_a, v_w_proj_b, v_w_out, v_final_norm_g):
    c_idx = lax.axis_index("c").astype(jnp.int32).reshape(1)
    chip_idx = (2 * lax.axis_index("x") + lax.axis_index("y")).astype(jnp.int32).reshape(1)

    w_blk = w_in[0].T.astype(BF16)
    r_blk = _pack_rest(w_uq[0], w_ukv[0], w_proj_a[0], w_proj_b[0], w_out[0]).astype(BF16)
    gw, gr = _gather_weights(w_blk, r_blk)
    fw_uq, fw_ukv, fwproj = _unpack_rest_weights(gr)

    loss, grad_x, d_w_int, d_wuq3, d_wukv3, d_proj, small = _local_step(
        x[0], loss_target[0], gw.reshape(W_IN_COLS, D_MODEL), fw_uq, fw_ukv, fwproj,
        norm_g, b_gate, lb_logits, hg_norm_g, q_a_g, kv_a_g, final_norm_g.reshape(1, D_MODEL))

    d_rest = _pack_rest_grads(d_wuq3, d_wukv3, d_proj)
    lw, lr = _swap_halves(d_w_int, d_rest)
    hw32, hw16 = _add_cores(c_idx, d_w_int, lw, tm=656, name="grad_add_cores_w")
    hr32, hr16 = _add_cores(c_idx, d_rest.reshape(N_CHIPS * REST_ROWS, D_MODEL), lr.reshape(N_CHIPS * REST_ROWS, HALF_COLS),
                            tm=REST_ROWS, name="grad_add_cores_r")
    landed_w, landed_r = _scatter_blocks(hw16, hr16)
    rw = _add_chips(chip_idx, hw32, landed_w, tm=656, name="grad_add_chips_w")
    rr = _add_chips(chip_idx, hr32, landed_r, tm=208, name="grad_add_chips_r")
    g_wt, g_rest = _join_halves(rw, rr)
    g_w_in = g_wt.T
    g_uq, g_ukv, g_pa, g_pb, g_out = _unpack_rest(g_rest)

    small_sum = _sum_small(_gather_small(_pack_small(small, lb_logits, loss)))

    upd = {
        "w_in": _adamw(w_in[0], g_w_in, m_w_in[0], v_w_in[0], name="adamw_w_in", tm=128),
        "w_uq": _adamw(w_uq[0], g_uq, m_w_uq[0], v_w_uq[0], name="adamw_w_uq", tm=Q_LORA),
        "w_ukv": _adamw(w_ukv[0], g_ukv, m_w_ukv[0], v_w_ukv[0], name="adamw_w_ukv", tm=KV_LORA),
        "w_proj_a": _adamw(w_proj_a[0], g_pa, m_w_proj_a[0], v_w_proj_a[0], name="adamw_w_proj_a", tm=256),
        "w_proj_b": _adamw(w_proj_b[0], g_pb, m_w_proj_b[0], v_w_proj_b[0], name="adamw_w_proj_b", tm=256),
        "w_out": _adamw(w_out[0], g_out, m_w_out[0], v_w_out[0], name="adamw_w_out", tm=256),
    }
    ws = _small_vec(norm_g, b_gate, lb_logits, hg_norm_g, q_a_g, kv_a_g, final_norm_g).reshape(7, 1024)
    ms_ = _small_vec(m_norm_g, m_b_gate, m_lb_logits, m_hg_norm_g, m_q_a_g, m_kv_a_g, m_final_norm_g).reshape(7, 1024)
    vs = _small_vec(v_norm_g, v_b_gate, v_lb_logits, v_hg_norm_g, v_q_a_g, v_kv_a_g, v_final_norm_g).reshape(7, 1024)
    upd_small = _adamw(ws, small_sum.reshape(7, 1024), ms_, vs, name="adamw_small", tm=7)

    def outputs(big, vec):
        s_ng, s_bg, s_lb, s_hg, s_qg, s_kvg, s_fg = _split_small(vec)
        return (s_ng, big["w_in"][None], s_bg, s_lb, s_hg, s_qg, big["w_uq"][None], s_kvg, big["w_ukv"][None],
                big["w_proj_a"][None], big["w_proj_b"][None], big["w_out"][None], s_fg)

    grads = {"w_in": g_w_in, "w_uq": g_uq, "w_ukv": g_ukv, "w_proj_a": g_pa, "w_proj_b": g_pb, "w_out": g_out}
    total_loss = small_sum[0, 6912]
    return (total_loss, grad_x[None], *outputs(grads, small_sum),
            *(o for k in range(3) for o in outputs({n: u[k] for n, u in upd.items()}, upd_small[k])))
```

```python
import functools

import jax
import jax.numpy as jnp
from jax import lax
from jax.experimental import pallas as pl
from jax.experimental.pallas import tpu as pltpu

F32 = jnp.float32
BF16 = jnp.bfloat16

D_MODEL = 1024
HEADS = 8
HEAD_DIM = 128
HG_CHUNK = 32
CHUNK_SHIFT = 5
HEAD_SHIFT = 7
QK_NOPE = 128
QK_ROPE = 64
QK_DIM = QK_NOPE + QK_ROPE
QK_PAD = 256
Q_LORA = 384
KV_LORA = 256
MS_COLS = 768
ROPE_THETA = 10000.0
EPS = 1e-6
ATT_SCALE = QK_DIM ** -0.5
LOG2E = 1.4426950408889634
LN2 = 0.6931471805599453
Q_PRESCALE = ATT_SCALE * LOG2E

ADAM_LR = 0.001
ADAM_B1 = 0.9
ADAM_B2 = 0.999
ADAM_EPS = 1e-08
ADAM_WD = 0.01
ADAM_STEP = 10

N_CHIPS = 4
N_DEV = 8
W_IN_COLS = 7872
W_IN_BLK = W_IN_COLS // N_CHIPS
REST_ROWS = 144 + 128 + 3 * 256
SMALL_COLS = 7168

TM_MM = 1024
TM_FUSED = 256
HG_ROWS = 256
TQ = 512
FLASH_HEADS = 2
HG_HEADS = 2
VMEM_LIMIT = 56 * 1024 * 1024


def _dot(a, b):
    return lax.dot_general(a, b, (((1,), (0,)), ((), ())), preferred_element_type=F32)


def _dot_nt(a, b):
    return lax.dot_general(a, b, (((1,), (1,)), ((), ())), preferred_element_type=F32)


def _dot_tn(a, b):
    return lax.dot_general(a, b, (((0,), (0,)), ((), ())), preferred_element_type=F32)


def _params(n_axes):
    return pltpu.CompilerParams(dimension_semantics=("arbitrary",) * n_axes, vmem_limit_bytes=VMEM_LIMIT)


def _rms(x, g):
    r = lax.rsqrt(jnp.mean(x * x, axis=-1, keepdims=True) + EPS)
    return x * r * g


def _rms_bwd(x, g, dy):
    r = lax.rsqrt(jnp.mean(x * x, axis=-1, keepdims=True) + EPS)
    xh = x * r
    dyg = dy * g
    dx = r * (dyg - xh * jnp.mean(dyg * xh, axis=-1, keepdims=True))
    return dx, dy * xh


def _silu_parts(z):
    s = jax.nn.sigmoid(z)
    return z * s, s * (1.0 + z * (1.0 - s))


def _rope(x, c, sa, sb):
    return x * c + pltpu.roll(x, 32, 1) * sa + pltpu.roll(x, 96, 1) * sb


def _rope_bwd(dy, c, sa, sb):
    return dy * c + pltpu.roll(dy * sa, 96, 1) + pltpu.roll(dy * sb, 32, 1)


def _rope_tables(seq):
    inv = ROPE_THETA ** (-jnp.arange(0, QK_ROPE, 2, dtype=F32) / QK_ROPE)
    ang = jnp.arange(seq, dtype=F32)[:, None] * inv[None, :]
    cos, sin = jnp.cos(ang), jnp.sin(ang)
    z32 = jnp.zeros_like(cos)
    z64 = jnp.zeros((seq, 64), F32)
    c = jnp.concatenate([cos, cos, z64], axis=1)
    sa = jnp.concatenate([z32, sin, z64], axis=1)
    sb = jnp.concatenate([-sin, z32, z64], axis=1)
    return c, sa, sb


def _mm(a, b, *, name, trans_b=False, add=None, out_dtype=F32, tm=TM_MM, tn=1024, tk=1024):
    m, k = a.shape
    n = b.shape[0] if trans_b else b.shape[1]
    tm, tn, tk = min(tm, m), min(tn, n), min(tk, k)
    assert m % tm == 0 and n % tn == 0 and k % tk == 0
    nk = k // tk
    has_add = add is not None

    def body(*refs):
        if has_add:
            a_ref, b_ref, add_ref, o_ref, acc_ref = refs
        else:
            a_ref, b_ref, o_ref, acc_ref = refs
        kk = pl.program_id(2)

        @pl.when(kk == 0)
        def _():
            acc_ref[...] = add_ref[...] if has_add else jnp.zeros_like(acc_ref)

        if trans_b:
            acc_ref[...] += _dot_nt(a_ref[...], b_ref[...])
        else:
            acc_ref[...] += _dot(a_ref[...], b_ref[...])

        @pl.when(kk == nk - 1)
        def _():
            o_ref[...] = acc_ref[...].astype(out_dtype)

    in_specs = [pl.BlockSpec((tm, tk), lambda i, j, kk: (i, kk))]
    if trans_b:
        in_specs.append(pl.BlockSpec((tn, tk), lambda i, j, kk: (j, kk)))
    else:
        in_specs.append(pl.BlockSpec((tk, tn), lambda i, j, kk: (kk, j)))
    args = [a, b]
    if has_add:
        in_specs.append(pl.BlockSpec((tm, tn), lambda i, j, kk: (i, j)))
        args.append(add)
    return pl.pallas_call(
        body, name=name, grid=(m // tm, n // tn, nk),
        in_specs=in_specs, out_specs=pl.BlockSpec((tm, tn), lambda i, j, kk: (i, j)),
        out_shape=jax.ShapeDtypeStruct((m, n), out_dtype),
        scratch_shapes=[pltpu.VMEM((tm, tn), F32)], compiler_params=_params(3),
    )(*args)


def _mm_tn(a, b, *, name, tm=TM_MM, tn=1024):
    flat = a.ndim == 2
    if flat:
        a = a[None]
    g, m, k = a.shape
    n = b.shape[1]
    tm, tn = min(tm, m), min(tn, n)
    assert m % tm == 0 and n % tn == 0

    def body(a_ref, b_ref, o_ref):
        @pl.when(pl.program_id(2) == 0)
        def _():
            o_ref[...] = jnp.zeros_like(o_ref)

        o_ref[...] += _dot_tn(a_ref[...], b_ref[...])

    out = pl.pallas_call(
        body, name=name, grid=(g, n // tn, m // tm),
        in_specs=[pl.BlockSpec((None, tm, k), lambda s, j, i: (s, i, 0)),
                  pl.BlockSpec((tm, tn), lambda s, j, i: (i, j))],
        out_specs=pl.BlockSpec((None, k, tn), lambda s, j, i: (s, 0, j)),
        out_shape=jax.ShapeDtypeStruct((g, k, n), F32), compiler_params=_params(3),
    )(a, b)
    return out[0] if flat else out


def _mm_slabs_out(a, wt, slabs, *, name, tm=TM_MM):
    m, k = a.shape
    tm = min(tm, m)
    n = D_MODEL

    def body(a_ref, w_ref, o_ref):
        o_ref[...] = _dot_nt(a_ref[...], w_ref[...])

    return pl.pallas_call(
        body, name=name, grid=(m // tm, slabs),
        in_specs=[pl.BlockSpec((tm, k), lambda i, j: (i, 0)), pl.BlockSpec((n, k), lambda i, j: (j, 0))],
        out_specs=pl.BlockSpec((None, tm, n), lambda i, j: (j, i, 0)),
        out_shape=jax.ShapeDtypeStruct((slabs, m, n), F32), compiler_params=_params(2),
    )(a, wt)


def _mm_slabs_in(a3, w, *, name, tm=TM_MM):
    slabs, m, k = a3.shape
    n = w.shape[1]
    tm = min(tm, m)

    def body(a_ref, w_ref, o_ref, acc_ref):
        j = pl.program_id(1)

        @pl.when(j == 0)
        def _():
            acc_ref[...] = jnp.zeros_like(acc_ref)

        acc_ref[...] += _dot(a_ref[...], w_ref[...])

        @pl.when(j == slabs - 1)
        def _():
            o_ref[...] = acc_ref[...]

    return pl.pallas_call(
        body, name=name, grid=(m // tm, slabs),
        in_specs=[pl.BlockSpec((None, tm, k), lambda i, j: (j, i, 0)), pl.BlockSpec((k, n), lambda i, j: (j, 0))],
        out_specs=pl.BlockSpec((tm, n), lambda i, j: (i, 0)),
        out_shape=jax.ShapeDtypeStruct((m, n), F32),
        scratch_shapes=[pltpu.VMEM((tm, n), F32)], compiler_params=_params(2),
    )(a3, w)


def _norm_in(x, g):
    s = x.shape[0]
    tm = min(TM_MM, s)

    def body(x_ref, g_ref, h_ref):
        h_ref[...] = _rms(x_ref[...], g_ref[...]).astype(BF16)

    return pl.pallas_call(
        body, name="norm_in", grid=(s // tm,),
        in_specs=[pl.BlockSpec((tm, D_MODEL), lambda i: (i, 0)), pl.BlockSpec((1, D_MODEL), lambda i: (0, 0))],
        out_specs=pl.BlockSpec((tm, D_MODEL), lambda i: (i, 0)),
        out_shape=jax.ShapeDtypeStruct((s, D_MODEL), BF16), compiler_params=_params(1),
    )(x, g)


def _norm_in_bwd(x, g, dh, dx2):
    s = x.shape[0]
    tm = min(TM_MM, s)

    def body(x_ref, g_ref, dh_ref, dx2_ref, dx_ref, dg_ref):
        @pl.when(pl.program_id(0) == 0)
        def _():
            dg_ref[...] = jnp.zeros_like(dg_ref)

        dx, dg_rows = _rms_bwd(x_ref[...], g_ref[...], dh_ref[...])
        dx_ref[...] = dx + dx2_ref[...]
        dg_ref[...] += jnp.sum(dg_rows, axis=0, keepdims=True)

    row = pl.BlockSpec((tm, D_MODEL), lambda i: (i, 0))
    vec = pl.BlockSpec((1, D_MODEL), lambda i: (0, 0))
    return pl.pallas_call(
        body, name="norm_in_bwd", grid=(s // tm,),
        in_specs=[row, vec, row, row], out_specs=[row, vec],
        out_shape=[jax.ShapeDtypeStruct((s, D_MODEL), F32), jax.ShapeDtypeStruct((1, D_MODEL), F32)],
        compiler_params=_params(1),
    )(x, g, dh, dx2)


def _chunk_rows(rows):
    return lax.broadcasted_iota(jnp.int32, (rows, HEAD_DIM), 0) & (HG_CHUNK - 1)


def _chunk_cumsum(x, rows):
    pos = _chunk_rows(rows)
    shift = 1
    while shift < HG_CHUNK:
        x = x + jnp.where(pos >= shift, pltpu.roll(x, shift, 0), 0.0)
        shift *= 2
    return x


def _chunk_revcumsum(x, rows):
    pos = _chunk_rows(rows)
    shift = 1
    while shift < HG_CHUNK:
        x = x + jnp.where(pos + shift < HG_CHUNK, pltpu.roll(x, rows - shift, 0), 0.0)
        shift *= 2
    return x


def _lower_bound(lbl):
    mx = jnp.maximum(lbl[0:1, :], lbl[1:2, :])
    e0 = jnp.exp(lbl[0:1, :] - mx)
    e1 = jnp.exp(lbl[1:2, :] - mx)
    p0 = e0 / (e0 + e1)
    return p0, p0 * (e1 / (e0 + e1))


def _hg_masks(rows, nch, tmask_s, bdmask_s):
    r = lax.broadcasted_iota(jnp.int32, (rows, rows), 0)
    c = lax.broadcasted_iota(jnp.int32, (rows, rows), 1)
    tmask_s[...] = jnp.where(((r >> CHUNK_SHIFT) == (c >> CHUNK_SHIFT)) & (r >= c), 1.0, 0.0)
    r = lax.broadcasted_iota(jnp.int32, (rows, nch * HEAD_DIM), 0)
    c = lax.broadcasted_iota(jnp.int32, (rows, nch * HEAD_DIM), 1)
    bdmask_s[...] = jnp.where((r >> CHUNK_SHIFT) == (c >> HEAD_SHIFT), 1.0, 0.0).astype(BF16)


def _block_diag(x, nch, bdmask):
    return jnp.tile(x, (1, nch)) * bdmask


def _hgrn_fwd(hg, lb_logits, norm_g):
    s = hg.shape[1]
    rows = min(HG_ROWS, s)
    nblk = s // rows
    nch = rows // HG_CHUNK

    def body(hg_ref, lbl_ref, g_ref, o_ref, ya_ref, st0_ref, st_s, stall_s, tmask_s, bdmask_s):
        @pl.when(pl.program_id(1) == 0)
        def _():
            st_s[...] = jnp.zeros_like(st_s)
            _hg_masks(rows, nch, tmask_s, bdmask_s)

        bdmask = bdmask_s[...]
        tmask = tmask_s[...] > 0.5
        for hh in range(HG_HEADS):
            hc = slice(hh * HEAD_DIM, (hh + 1) * HEAD_DIM)
            hq = hg_ref[0, :, hc]
            hf = hg_ref[1, :, hc]
            hi = hg_ref[2, :, hc]
            hz = hg_ref[3, :, hc]
            lb, _ = _lower_bound(lbl_ref[:, hc])
            f = lb + (1.0 - lb) * jax.nn.sigmoid(hf)
            q = hq * jax.nn.sigmoid(hq)
            k = 1.0 - f
            logf = jnp.log(f)
            b = _chunk_cumsum(logf, rows)
            q_in = (q * jnp.exp(b)).astype(BF16)
            k_in = (k * jnp.exp(-b)).astype(BF16)
            k_out = (k * jnp.exp(_chunk_revcumsum(logf, rows) - logf)).astype(BF16)
            vb = hi.astype(BF16)

            sc = jnp.where(tmask, _dot_nt(q_in, k_in), 0.0)
            o_intra = _dot(sc.astype(BF16), vb)
            kvt = _dot_tn(vb, _block_diag(k_out, nch, bdmask))
            st = st_s[hh]
            st0_ref[hh] = st
            for c in range(nch):
                cols = slice(c * HEAD_DIM, (c + 1) * HEAD_DIM)
                last = (c + 1) * HG_CHUNK - 1
                stall_s[hh, :, cols] = st.astype(BF16)
                st = st * jnp.exp(b[last:last + 1, :]) + kvt[:, cols]
            st_s[hh] = st
            o = o_intra + _dot_nt(_block_diag(q_in, nch, bdmask), stall_s[hh])
            o_ref[:, hc] = o
            silu_z, _ = _silu_parts(hz)
            ya_ref[:, hc] = (_rms(o, g_ref[...]) * silu_z).astype(BF16)

    nh = HG_HEADS
    return pl.pallas_call(
        body, name="hgrn_fwd", grid=(HEADS // nh, nblk),
        in_specs=[pl.BlockSpec((4, rows, nh * HEAD_DIM), lambda h, i: (0, i, h)),
                  pl.BlockSpec((2, nh * HEAD_DIM), lambda h, i: (0, h)),
                  pl.BlockSpec((1, HEAD_DIM), lambda h, i: (0, 0))],
        out_specs=[pl.BlockSpec((rows, nh * HEAD_DIM), lambda h, i: (i, h)),
                   pl.BlockSpec((rows, nh * HEAD_DIM), lambda h, i: (i, h)),
                   pl.BlockSpec((nh, None, HEAD_DIM, HEAD_DIM), lambda h, i: (h, i, 0, 0))],
        out_shape=[jax.ShapeDtypeStruct((s, D_MODEL), F32), jax.ShapeDtypeStruct((s, D_MODEL), BF16),
                   jax.ShapeDtypeStruct((HEADS, nblk, HEAD_DIM, HEAD_DIM), F32)],
        scratch_shapes=[pltpu.VMEM((nh, HEAD_DIM, HEAD_DIM), F32), pltpu.VMEM((nh, HEAD_DIM, nch * HEAD_DIM), BF16),
                        pltpu.VMEM((rows, rows), F32), pltpu.VMEM((rows, nch * HEAD_DIM), BF16)],
        compiler_params=_params(2),
    )(hg, lb_logits, norm_g)


def _hgrn_bwd(hg, o_pre, dya, st0, lb_logits, norm_g):
    s = hg.shape[1]
    rows = min(HG_ROWS, s)
    nblk = s // rows
    nch = rows // HG_CHUNK

    def body(hg_ref, o_ref, dya_ref, st0_ref, lbl_ref, g_ref, dhg_ref, dlb_ref, dg_ref,
             dst_s, stp_s, stp_rows_s, dst_rows_s, dst_lane_s, dbl_s, tmask_s, bdmask_s):
        @pl.when(pl.program_id(1) == 0)
        def _():
            dst_s[...] = jnp.zeros_like(dst_s)
            dlb_ref[...] = jnp.zeros_like(dlb_ref)
            dg_ref[...] = jnp.zeros_like(dg_ref)
            _hg_masks(rows, nch, tmask_s, bdmask_s)

        bdmask = bdmask_s[...]
        tmask = tmask_s[...] > 0.5
        g = g_ref[...]
        for hh in range(HG_HEADS):
            hc = slice(hh * HEAD_DIM, (hh + 1) * HEAD_DIM)
            hq = hg_ref[0, :, hc]
            hf = hg_ref[1, :, hc]
            hi = hg_ref[2, :, hc]
            hz = hg_ref[3, :, hc]
            lb, _ = _lower_bound(lbl_ref[:, hc])
            sg = jax.nn.sigmoid(hf)
            f = lb + (1.0 - lb) * sg
            q, dsilu_q = _silu_parts(hq)
            k = 1.0 - f
            logf = jnp.log(f)
            b = _chunk_cumsum(logf, rows)
            eb = jnp.exp(b)
            enb = jnp.exp(-b)
            ebl = jnp.exp(_chunk_revcumsum(logf, rows) - logf)
            q_in32 = q * eb
            k_in32 = k * enb
            k_out32 = k * ebl
            q_in = q_in32.astype(BF16)
            k_in = k_in32.astype(BF16)
            k_out = k_out32.astype(BF16)
            vb = hi.astype(BF16)
            kbd = _block_diag(k_out, nch, bdmask)
            qbd = _block_diag(q_in, nch, bdmask)
            decs = [jnp.exp(b[(c + 1) * HG_CHUNK - 1:(c + 1) * HG_CHUNK, :]) for c in range(nch)]

            kvt = _dot_tn(vb, kbd)
            st = st0_ref[hh]
            for c in range(nch):
                stp_s[hh, c] = st
                stp_rows_s[hh, c * HEAD_DIM:(c + 1) * HEAD_DIM, :] = st.astype(BF16)
                st = st * decs[c] + kvt[:, c * HEAD_DIM:(c + 1) * HEAD_DIM]

            o = o_ref[:, hc]
            rstd = lax.rsqrt(jnp.mean(o * o, axis=-1, keepdims=True) + EPS)
            oh = o * rstd
            silu_z, dsilu_z = _silu_parts(hz)
            dya_v = dya_ref[:, hc]
            dn = dya_v * silu_z
            dhz = dya_v * (oh * g) * dsilu_z
            dg_ref[hh] += jnp.sum(dn * oh, axis=0, keepdims=True)
            doh = dn * g
            do = (rstd * (doh - oh * jnp.mean(doh * oh, axis=-1, keepdims=True))).astype(BF16)

            dq_all = _dot_tn(do, qbd)
            dst = dst_s[hh]
            ddecs = [None] * nch
            for c in reversed(range(nch)):
                dstb = dst.astype(BF16)
                dst_lane_s[hh, :, c * HEAD_DIM:(c + 1) * HEAD_DIM] = dstb
                dst_rows_s[hh, c * HEAD_DIM:(c + 1) * HEAD_DIM, :] = dstb
                ddecs[c] = jnp.sum(dst * stp_s[hh, c], axis=0, keepdims=True) * decs[c]
                dst = dst * decs[c] + dq_all[:, c * HEAD_DIM:(c + 1) * HEAD_DIM]
            dst_s[hh] = dst

            sc = jnp.where(tmask, _dot_nt(q_in, k_in), 0.0).astype(BF16)
            dkout = _dot(_block_diag(vb, nch, bdmask), dst_rows_s[hh])
            dv = _dot_nt(kbd, dst_lane_s[hh]) + _dot_tn(sc, do)
            dsc = jnp.where(tmask, _dot_nt(do, vb), 0.0).astype(BF16)
            dqin = _dot(dsc, k_in) + _dot(_block_diag(do, nch, bdmask), stp_rows_s[hh])
            dkin = _dot_tn(dsc, q_in)

            dko = dkout * k_out32
            for c in range(nch):
                sl = slice(c * HG_CHUNK, (c + 1) * HG_CHUNK)
                dbl = jnp.sum(dko[sl], axis=0, keepdims=True) + ddecs[c]
                dbl_s[hh, sl, :] = jnp.broadcast_to(dbl, (HG_CHUNK, HEAD_DIM))
            dq = dqin * eb
            dk = dkin * enb + dkout * ebl
            db = dqin * q_in32 - dkin * k_in32 - dko
            dlogf = _chunk_revcumsum(db, rows) + dbl_s[hh]
            df = dlogf / f - dk
            dlb_ref[:, hc] += jnp.sum(df * (1.0 - sg), axis=0, keepdims=True)
            dhg_ref[0, :, hc] = (dq * dsilu_q).astype(BF16)
            dhg_ref[1, :, hc] = (df * (1.0 - lb) * sg * (1.0 - sg)).astype(BF16)
            dhg_ref[2, :, hc] = dv.astype(BF16)
            dhg_ref[3, :, hc] = dhz.astype(BF16)

    last = nblk - 1
    nh = HG_HEADS
    wide = nh * HEAD_DIM
    return pl.pallas_call(
        body, name="hgrn_bwd", grid=(HEADS // nh, nblk),
        in_specs=[pl.BlockSpec((4, rows, wide), lambda h, i: (0, last - i, h)),
                  pl.BlockSpec((rows, wide), lambda h, i: (last - i, h)),
                  pl.BlockSpec((rows, wide), lambda h, i: (last - i, h)),
                  pl.BlockSpec((nh, None, HEAD_DIM, HEAD_DIM), lambda h, i: (h, last - i, 0, 0)),
                  pl.BlockSpec((2, wide), lambda h, i: (0, h)),
                  pl.BlockSpec((1, HEAD_DIM), lambda h, i: (0, 0))],
        out_specs=[pl.BlockSpec((4, rows, wide), lambda h, i: (0, last - i, h)),
                   pl.BlockSpec((1, wide), lambda h, i: (0, h)),
                   pl.BlockSpec((nh, 1, HEAD_DIM), lambda h, i: (h, 0, 0))],
        out_shape=[jax.ShapeDtypeStruct((4, s, D_MODEL), BF16), jax.ShapeDtypeStruct((1, D_MODEL), F32),
                   jax.ShapeDtypeStruct((HEADS, 1, HEAD_DIM), F32)],
        scratch_shapes=[pltpu.VMEM((nh, HEAD_DIM, HEAD_DIM), F32), pltpu.VMEM((nh, nch, HEAD_DIM, HEAD_DIM), F32),
                        pltpu.VMEM((nh, nch * HEAD_DIM, HEAD_DIM), BF16), pltpu.VMEM((nh, nch * HEAD_DIM, HEAD_DIM), BF16),
                        pltpu.VMEM((nh, HEAD_DIM, nch * HEAD_DIM), BF16), pltpu.VMEM((nh, rows, HEAD_DIM), F32),
                        pltpu.VMEM((rows, rows), F32), pltpu.VMEM((rows, nch * HEAD_DIM), BF16)],
        compiler_params=_params(2),
    )(hg, o_pre, dya, st0, lb_logits, norm_g)


def _mla_pre(ms, q_a_g, kv_a_g, wuq3, wukv3, tabs):
    s = ms.shape[0]
    tm = min(TM_FUSED, s)

    def body(ms_ref, qg_ref, kvg_ref, wuq_ref, wukv_ref, c_ref, sa_ref, sb_ref,
             q_ref, k_ref, v_ref, cqn_ref, ckvn_ref):
        c, sa, sb = c_ref[...], sa_ref[...], sb_ref[...]
        cqn = _rms(ms_ref[:, 0:Q_LORA], qg_ref[...]).astype(BF16)
        ckvn = _rms(ms_ref[:, Q_LORA:Q_LORA + KV_LORA], kvg_ref[...]).astype(BF16)
        cqn_ref[...] = cqn
        ckvn_ref[...] = ckvn
        k_pe = _rope(ms_ref[:, Q_LORA + KV_LORA:MS_COLS], c, sa, sb).astype(BF16)
        for h in range(HEADS):
            qh = _dot(cqn, wuq_ref[h])
            q_ref[h, :, 0:128] = (qh[:, 0:128] * Q_PRESCALE).astype(BF16)
            q_ref[h, :, 128:256] = (_rope(qh[:, 128:256], c, sa, sb) * Q_PRESCALE).astype(BF16)
            kvh = _dot(ckvn, wukv_ref[h])
            k_ref[h, :, 0:128] = kvh[:, 0:128].astype(BF16)
            k_ref[h, :, 128:256] = k_pe
            v_ref[h] = kvh[:, 128:256].astype(BF16)

    tab = pl.BlockSpec((tm, 128), lambda i: (i, 0))
    return pl.pallas_call(
        body, name="mla_pre", grid=(s // tm,),
        in_specs=[pl.BlockSpec((tm, MS_COLS), lambda i: (i, 0)),
                  pl.BlockSpec((1, Q_LORA), lambda i: (0, 0)), pl.BlockSpec((1, KV_LORA), lambda i: (0, 0)),
                  pl.BlockSpec((HEADS, Q_LORA, QK_PAD), lambda i: (0, 0, 0)),
                  pl.BlockSpec((HEADS, KV_LORA, 256), lambda i: (0, 0, 0)), tab, tab, tab],
        out_specs=[pl.BlockSpec((HEADS, tm, QK_PAD), lambda i: (0, i, 0)),
                   pl.BlockSpec((HEADS, tm, QK_PAD), lambda i: (0, i, 0)),
                   pl.BlockSpec((HEADS, tm, HEAD_DIM), lambda i: (0, i, 0)),
                   pl.BlockSpec((tm, Q_LORA), lambda i: (i, 0)), pl.BlockSpec((tm, KV_LORA), lambda i: (i, 0))],
        out_shape=[jax.ShapeDtypeStruct((HEADS, s, QK_PAD), BF16), jax.ShapeDtypeStruct((HEADS, s, QK_PAD), BF16),
                   jax.ShapeDtypeStruct((HEADS, s, HEAD_DIM), BF16),
                   jax.ShapeDtypeStruct((s, Q_LORA), BF16), jax.ShapeDtypeStruct((s, KV_LORA), BF16)],
        compiler_params=_params(1),
    )(ms, q_a_g, kv_a_g, wuq3, wukv3, *tabs)


def _causal_mask(t):
    r = lax.broadcasted_iota(jnp.int32, (t, t), 0)
    c = lax.broadcasted_iota(jnp.int32, (t, t), 1)
    return r >= c


def _flash_fwd(q, k, v, mz):
    s = q.shape[1]
    t = min(TQ, s)

    def body(q_ref, k_ref, v_ref, mz_ref, o_ref, yb_ref, lse_ref, m_s, l_s, acc_s):
        i = pl.program_id(1)
        m_s[...] = jnp.full_like(m_s, -jnp.inf)
        l_s[...] = jnp.zeros_like(l_s)
        acc_s[...] = jnp.zeros_like(acc_s)

        def step(j, masked):
            rows = pl.ds(pl.multiple_of(j * t, t), t)
            for hh in range(FLASH_HEADS):
                sc = _dot_nt(q_ref[hh], k_ref[hh, rows, :])
                if masked:
                    sc = jnp.where(_causal_mask(t), sc, -jnp.inf)
                m_prev = m_s[hh]
                m_new = jnp.maximum(m_prev, jnp.max(sc, axis=-1, keepdims=True))
                p = jnp.exp2(sc - jnp.tile(m_new, (1, t // 128)))
                alpha = jnp.exp2(m_prev - m_new)
                l_s[hh] = alpha * l_s[hh] + jnp.sum(p, axis=-1, keepdims=True)
                acc_s[hh] = alpha * acc_s[hh] + _dot(p.astype(BF16), v_ref[hh, rows, :])
                m_s[hh] = m_new

        def loop_body(j, carry):
            step(j, False)
            return carry

        lax.fori_loop(0, i, loop_body, 0)
        step(i, True)
        for hh in range(FLASH_HEADS):
            cols = slice(hh * HEAD_DIM, (hh + 1) * HEAD_DIM)
            out = acc_s[hh] / l_s[hh]
            o_ref[:, cols] = out
            silu_z, _ = _silu_parts(mz_ref[:, cols])
            yb_ref[:, cols] = (out * silu_z).astype(BF16)
            lse_ref[hh] = m_s[hh] + jnp.log2(l_s[hh])

    nh = FLASH_HEADS
    col = pl.BlockSpec((t, nh * HEAD_DIM), lambda h, i: (i, h))
    return pl.pallas_call(
        body, name="flash_fwd", grid=(HEADS // nh, s // t),
        in_specs=[pl.BlockSpec((nh, t, QK_PAD), lambda h, i: (h, i, 0)),
                  pl.BlockSpec((nh, s, QK_PAD), lambda h, i: (h, 0, 0)),
                  pl.BlockSpec((nh, s, HEAD_DIM), lambda h, i: (h, 0, 0)), col],
        out_specs=[col, col, pl.BlockSpec((nh, t, 128), lambda h, i: (h, i, 0))],
        out_shape=[jax.ShapeDtypeStruct((s, D_MODEL), F32), jax.ShapeDtypeStruct((s, D_MODEL), BF16),
                   jax.ShapeDtypeStruct((HEADS, s, 128), F32)],
        scratch_shapes=[pltpu.VMEM((nh, t, 128), F32), pltpu.VMEM((nh, t, 128), F32),
                        pltpu.VMEM((nh, t, HEAD_DIM), F32)],
        compiler_params=_params(2),
    )(q, k, v, mz)


def _flash_bwd(q, k, v, dyb, mz, o_att, lse, tabs):
    s = q.shape[1]
    t = min(TQ, s)

    def body(q_ref, k_ref, v_ref, dyb_ref, mz_ref, o_ref, lse_ref, c_ref, sa_ref, sb_ref,
             dq_ref, dk_ref, dv_ref, dmz_ref, dq_s, delta_s):
        i = pl.program_id(1)

        @pl.when(i == 0)
        def _():
            dk_ref[...] = jnp.zeros_like(dk_ref)
            dv_ref[...] = jnp.zeros_like(dv_ref)

        silu_z, dsilu_z = _silu_parts(mz_ref[...])
        dyb_v = dyb_ref[...]
        out = o_ref[...]
        do32 = dyb_v * silu_z
        dmz_ref[...] = (dyb_v * out * dsilu_z).astype(BF16)
        delta_s[...] = jnp.broadcast_to(jnp.sum(do32 * out, axis=-1, keepdims=True), (t, 128))
        do = do32.astype(BF16)
        qv = q_ref[...]
        dq_s[...] = jnp.zeros_like(dq_s)

        def step(j, masked):
            rows = pl.ds(pl.multiple_of(j * t, t), t)
            kj = k_ref[rows, :]
            vj = v_ref[rows, :]
            sc = _dot_nt(qv, kj)
            if masked:
                sc = jnp.where(_causal_mask(t), sc, -jnp.inf)
            p = jnp.exp2(sc - jnp.tile(lse_ref[...], (1, t // 128)))
            dp = _dot_nt(do, vj)
            ds = (p * (dp - jnp.tile(delta_s[...], (1, t // 128)))).astype(BF16)
            dv_ref[rows, :] += _dot_tn(p.astype(BF16), do)
            dk_ref[rows, :] += _dot_tn(ds, qv)
            dq_s[...] += _dot(ds, kj)

        def loop_body(j, carry):
            step(j, False)
            return carry

        lax.fori_loop(0, i, loop_body, 0)
        step(i, True)
        dq = dq_s[...] * ATT_SCALE
        dq_ref[:, 0:128] = dq[:, 0:128].astype(BF16)
        dq_ref[:, 128:256] = _rope_bwd(dq[:, 128:256], c_ref[...], sa_ref[...], sb_ref[...]).astype(BF16)

    col = pl.BlockSpec((t, HEAD_DIM), lambda h, i: (i, h))
    tab = pl.BlockSpec((t, 128), lambda h, i: (i, 0))
    return pl.pallas_call(
        body, name="flash_bwd", grid=(HEADS, s // t),
        in_specs=[pl.BlockSpec((None, t, QK_PAD), lambda h, i: (h, i, 0)),
                  pl.BlockSpec((None, s, QK_PAD), lambda h, i: (h, 0, 0)),
                  pl.BlockSpec((None, s, HEAD_DIM), lambda h, i: (h, 0, 0)),
                  col, col, col, pl.BlockSpec((None, t, 128), lambda h, i: (h, i, 0)), tab, tab, tab],
        out_specs=[pl.BlockSpec((None, t, QK_PAD), lambda h, i: (h, i, 0)),
                   pl.BlockSpec((None, s, QK_PAD), lambda h, i: (h, 0, 0)),
                   pl.BlockSpec((None, s, HEAD_DIM), lambda h, i: (h, 0, 0)), col],
        out_shape=[jax.ShapeDtypeStruct((HEADS, s, QK_PAD), BF16), jax.ShapeDtypeStruct((HEADS, s, QK_PAD), F32),
                   jax.ShapeDtypeStruct((HEADS, s, HEAD_DIM), F32), jax.ShapeDtypeStruct((s, D_MODEL), BF16)],
        scratch_shapes=[pltpu.VMEM((t, QK_PAD), F32), pltpu.VMEM((t, 128), F32)], compiler_params=_params(2),
    )(q, k, v, dyb, mz, o_att, lse, *tabs)


def _mla_bwd_proj(dq, dk, dv, cqn, ckvn, ms, q_a_g, kv_a_g, wuq3, wukv3, tabs):
    s = ms.shape[0]
    tm = min(TM_FUSED, s)

    def body(dq_ref, dk_ref, dv_ref, cqn_ref, ckvn_ref, ms_ref, qg_ref, kvg_ref, wuq_ref, wukv_ref,
             c_ref, sa_ref, sb_ref, dms_ref, dwuq_ref, dwukv_ref, dqg_ref, dkvg_ref):
        @pl.when(pl.program_id(0) == 0)
        def _():
            dwuq_ref[...] = jnp.zeros_like(dwuq_ref)
            dwukv_ref[...] = jnp.zeros_like(dwukv_ref)
            dqg_ref[...] = jnp.zeros_like(dqg_ref)
            dkvg_ref[...] = jnp.zeros_like(dkvg_ref)

        cqn = cqn_ref[...]
        ckvn = ckvn_ref[...]
        dcqn = jnp.zeros((tm, Q_LORA), F32)
        dckvn = jnp.zeros((tm, KV_LORA), F32)
        dkpe = jnp.zeros((tm, 128), F32)
        for h in range(HEADS):
            dqh = dq_ref[h]
            dcqn += _dot_nt(dqh, wuq_ref[h])
            dwuq_ref[h] += _dot_tn(cqn, dqh)
            dkh = dk_ref[h] * LN2
            dkvh = jnp.concatenate([dkh[:, 0:128], dv_ref[h]], axis=1).astype(BF16)
            dckvn += _dot_nt(dkvh, wukv_ref[h])
            dwukv_ref[h] += _dot_tn(ckvn, dkvh)
            dkpe += dkh[:, 128:256]
        dcq, dqg_rows = _rms_bwd(ms_ref[:, 0:Q_LORA], qg_ref[...], dcqn)
        dckv, dkvg_rows = _rms_bwd(ms_ref[:, Q_LORA:Q_LORA + KV_LORA], kvg_ref[...], dckvn)
        dqg_ref[...] += jnp.sum(dqg_rows, axis=0, keepdims=True)
        dkvg_ref[...] += jnp.sum(dkvg_rows, axis=0, keepdims=True)
        dms_ref[:, 0:Q_LORA] = dcq.astype(BF16)
        dms_ref[:, Q_LORA:Q_LORA + KV_LORA] = dckv.astype(BF16)
        dms_ref[:, Q_LORA + KV_LORA:MS_COLS] = _rope_bwd(dkpe, c_ref[...], sa_ref[...], sb_ref[...]).astype(BF16)

    tab = pl.BlockSpec((tm, 128), lambda i: (i, 0))
    wq = pl.BlockSpec((HEADS, Q_LORA, QK_PAD), lambda i: (0, 0, 0))
    wkv = pl.BlockSpec((HEADS, KV_LORA, 256), lambda i: (0, 0, 0))
    qg = pl.BlockSpec((1, Q_LORA), lambda i: (0, 0))
    kvg = pl.BlockSpec((1, KV_LORA), lambda i: (0, 0))
    return pl.pallas_call(
        body, name="mla_bwd_proj", grid=(s // tm,),
        in_specs=[pl.BlockSpec((HEADS, tm, QK_PAD), lambda i: (0, i, 0)),
                  pl.BlockSpec((HEADS, tm, QK_PAD), lambda i: (0, i, 0)),
                  pl.BlockSpec((HEADS, tm, HEAD_DIM), lambda i: (0, i, 0)),
                  pl.BlockSpec((tm, Q_LORA), lambda i: (i, 0)), pl.BlockSpec((tm, KV_LORA), lambda i: (i, 0)),
                  pl.BlockSpec((tm, MS_COLS), lambda i: (i, 0)), qg, kvg, wq, wkv, tab, tab, tab],
        out_specs=[pl.BlockSpec((tm, MS_COLS), lambda i: (i, 0)), wq, wkv, qg, kvg],
        out_shape=[jax.ShapeDtypeStruct((s, MS_COLS), BF16), jax.ShapeDtypeStruct((HEADS, Q_LORA, QK_PAD), F32),
                   jax.ShapeDtypeStruct((HEADS, KV_LORA, 256), F32),
                   jax.ShapeDtypeStruct((1, Q_LORA), F32), jax.ShapeDtypeStruct((1, KV_LORA), F32)],
        compiler_params=_params(1),
    )(dq, dk, dv, cqn, ckvn, ms, q_a_g, kv_a_g, wuq3, wukv3, *tabs)


def _merge_loss(ya, yb, glog, b_gate, x, tgt, fg, wpa, wpb, wout):
    s = x.shape[0]
    tm = min(TM_FUSED, s)

    def body(ya_ref, yb_ref, g0_ref, g1_ref, b0_ref, b1_ref, x_ref, t_ref, fg_ref, wpa_ref, wpb_ref, wout_ref,
             mg_ref, pa_ref, pb_ref, dx2_ref, loss_ref, dfg_ref):
        @pl.when(pl.program_id(0) == 0)
        def _():
            loss_ref[...] = jnp.zeros_like(loss_ref)
            dfg_ref[...] = jnp.zeros_like(dfg_ref)

        pa = _dot(ya_ref[...], wpa_ref[...])
        pb = _dot(yb_ref[...], wpb_ref[...])
        pa_ref[...] = pa
        pb_ref[...] = pb
        merged = (jax.nn.sigmoid(g0_ref[...] + b0_ref[...]) * pa
                  + jax.nn.sigmoid(g1_ref[...] + b1_ref[...]) * pb).astype(BF16)
        mg_ref[...] = merged
        x2 = x_ref[...] + _dot(merged, wout_ref[...])
        fg_v = fg_ref[...]
        err = _rms(x2, fg_v) - t_ref[...]
        loss_ref[...] += 0.5 * jnp.sum(jnp.mean(err * err, axis=-1, keepdims=True), axis=0, keepdims=True)
        dx2, dfg_rows = _rms_bwd(x2, fg_v, err * (1.0 / D_MODEL))
        dx2_ref[...] = dx2
        dfg_ref[...] += jnp.sum(dfg_rows, axis=0, keepdims=True)

    row = pl.BlockSpec((tm, D_MODEL), lambda i: (i, 0))
    row1 = pl.BlockSpec((tm, D_MODEL), lambda i: (i, 1))
    vec = pl.BlockSpec((1, D_MODEL), lambda i: (0, 0))
    vec1 = pl.BlockSpec((1, D_MODEL), lambda i: (0, 1))
    wsp = pl.BlockSpec((D_MODEL, D_MODEL), lambda i: (0, 0))
    return pl.pallas_call(
        body, name="merge_loss", grid=(s // tm,),
        in_specs=[row, row, row, row1, vec, vec1, row, row, vec, wsp, wsp, wsp],
        out_specs=[row, row, row, row, pl.BlockSpec((1, 128), lambda i: (0, 0)), vec],
        out_shape=[jax.ShapeDtypeStruct((s, D_MODEL), BF16), jax.ShapeDtypeStruct((s, D_MODEL), F32),
                   jax.ShapeDtypeStruct((s, D_MODEL), F32), jax.ShapeDtypeStruct((s, D_MODEL), F32),
                   jax.ShapeDtypeStruct((1, 128), F32), jax.ShapeDtypeStruct((1, D_MODEL), F32)],
        compiler_params=_params(1),
    )(ya, yb, glog, glog, b_gate, b_gate, x, tgt, fg, wpa, wpb, wout)


def _merge_bwd(dx2, pa, pb, glog, b_gate, wpa, wpb, wout):
    s = dx2.shape[0]
    tm = min(TM_FUSED, s)

    def body(dx2_ref, pa_ref, pb_ref, g0_ref, g1_ref, b0_ref, b1_ref, wpa_ref, wpb_ref, wout_ref,
             dya_ref, dyb_ref, dgl_ref, dpa_ref, dpb_ref, dx2b_ref, dbg_ref):
        @pl.when(pl.program_id(0) == 0)
        def _():
            dbg_ref[...] = jnp.zeros_like(dbg_ref)

        dx2b = dx2_ref[...].astype(BF16)
        dx2b_ref[...] = dx2b
        dmg = _dot_nt(dx2b, wout_ref[...])
        g0 = jax.nn.sigmoid(g0_ref[...] + b0_ref[...])
        g1 = jax.nn.sigmoid(g1_ref[...] + b1_ref[...])
        dpa = (dmg * g0).astype(BF16)
        dpb = (dmg * g1).astype(BF16)
        dpa_ref[...] = dpa
        dpb_ref[...] = dpb
        dgl0 = dmg * pa_ref[...] * g0 * (1.0 - g0)
        dgl1 = dmg * pb_ref[...] * g1 * (1.0 - g1)
        dgl_ref[:, 0:D_MODEL] = dgl0.astype(BF16)
        dgl_ref[:, D_MODEL:2 * D_MODEL] = dgl1.astype(BF16)
        dbg_ref[:, 0:D_MODEL] += jnp.sum(dgl0, axis=0, keepdims=True)
        dbg_ref[:, D_MODEL:2 * D_MODEL] += jnp.sum(dgl1, axis=0, keepdims=True)
        dya_ref[...] = _dot_nt(dpa, wpa_ref[...])
        dyb_ref[...] = _dot_nt(dpb, wpb_ref[...])

    row = pl.BlockSpec((tm, D_MODEL), lambda i: (i, 0))
    row1 = pl.BlockSpec((tm, D_MODEL), lambda i: (i, 1))
    row2 = pl.BlockSpec((tm, 2 * D_MODEL), lambda i: (i, 0))
    vec = pl.BlockSpec((1, D_MODEL), lambda i: (0, 0))
    vec1 = pl.BlockSpec((1, D_MODEL), lambda i: (0, 1))
    vec2 = pl.BlockSpec((1, 2 * D_MODEL), lambda i: (0, 0))
    wsp = pl.BlockSpec((D_MODEL, D_MODEL), lambda i: (0, 0))
    return pl.pallas_call(
        body, name="merge_bwd", grid=(s // tm,),
        in_specs=[row, row, row, row, row1, vec, vec1, wsp, wsp, wsp],
        out_specs=[row, row, row2, row, row, row, vec2],
        out_shape=[jax.ShapeDtypeStruct((s, D_MODEL), F32), jax.ShapeDtypeStruct((s, D_MODEL), F32),
                   jax.ShapeDtypeStruct((s, 2 * D_MODEL), BF16), jax.ShapeDtypeStruct((s, D_MODEL), BF16),
                   jax.ShapeDtypeStruct((s, D_MODEL), BF16), jax.ShapeDtypeStruct((s, D_MODEL), BF16),
                   jax.ShapeDtypeStruct((1, 2 * D_MODEL), F32)],
        compiler_params=_params(1),
    )(dx2, pa, pb, glog, glog, b_gate, b_gate, wpa, wpb, wout)


def _local_step(x, tgt, w_int, w_uq, w_ukv, wproj, norm_g, b_gate, lb_logits, hg_norm_g, q_a_g, kv_a_g, fg):
    s = x.shape[0]
    w_ms = jnp.concatenate([w_int[4096:4800], jnp.zeros((64, D_MODEL), BF16)], axis=0)
    w_mz = w_int[4800:5824]
    w_gl = w_int[5824:7872]
    wuq3 = jnp.pad(w_uq.reshape(Q_LORA, HEADS, QK_DIM).transpose(1, 0, 2), ((0, 0), (0, 0), (0, QK_PAD - QK_DIM)))
    wukv3 = w_ukv.reshape(KV_LORA, HEADS, 256).transpose(1, 0, 2)
    wpa, wpb, wout = wproj[0], wproj[1], wproj[2]
    tabs = _rope_tables(s)

    h = _norm_in(x, norm_g)
    hg = _mm_slabs_out(h, w_int, 4, name="proj_hg")
    ms = _mm(h, w_ms, trans_b=True, name="proj_ms")
    mz = _mm(h, w_mz, trans_b=True, name="proj_mz")
    glog = _mm(h, w_gl, trans_b=True, name="proj_gate")
    o_pre, ya, st0 = _hgrn_fwd(hg, lb_logits, hg_norm_g)
    q, k, v, cqn, ckvn = _mla_pre(ms, q_a_g, kv_a_g, wuq3, wukv3, tabs)
    o_att, yb, lse = _flash_fwd(q, k, v, mz)
    merged, pa, pb, dx2, loss, dfg = _merge_loss(ya, yb, glog, b_gate, x, tgt, fg, wpa, wpb, wout)

    dya, dyb, dglog, dpa, dpb, dx2b, dbg = _merge_bwd(dx2, pa, pb, glog, b_gate, wpa, wpb, wout)
    d_wout = _mm_tn(merged, dx2b, name="dw_out")
    d_wpa = _mm_tn(ya, dpa, name="dw_proj_a")
    d_wpb = _mm_tn(yb, dpb, name="dw_proj_b")
    dhg, dlb, dhgg = _hgrn_bwd(hg, o_pre, dya, st0, lb_logits, hg_norm_g)
    dq, dk, dv, dmz = _flash_bwd(q, k, v, dyb, mz, o_att, lse, tabs)
    dms, d_wuq3, d_wukv3, dqg, dkvg = _mla_bwd_proj(dq, dk, dv, cqn, ckvn, ms, q_a_g, kv_a_g, wuq3, wukv3, tabs)
    d_hg = _mm_tn(dhg, h, name="dw_in_hg")
    d_ms = _mm_tn(dms, h, name="dw_in_ms")
    d_mz = _mm_tn(dmz, h, name="dw_in_mz")
    d_gl = _mm_tn(dglog, h, name="dw_in_gate")
    dh = _mm_slabs_in(dhg, w_int, name="dh_hg")
    dh = _mm(dms, w_ms, add=dh, name="dh_ms")
    dh = _mm(dmz, w_mz, add=dh, name="dh_mz")
    dh = _mm(dglog, w_gl, add=dh, name="dh_gate")
    grad_x, dng = _norm_in_bwd(x, norm_g, dh, dx2)

    d_w_int = jnp.concatenate([d_hg.reshape(4 * D_MODEL, D_MODEL), d_ms[0:704], d_mz, d_gl], axis=0)
    small = {"norm_g": dng, "b_gate": dbg, "lb": dlb, "hg_norm_g": dhgg, "q_a_g": dqg, "kv_a_g": dkvg,
             "final_norm_g": dfg}
    return loss, grad_x, d_w_int, d_wuq3, d_wukv3, (d_wpa, d_wpb, d_wout), small


def _pack_rest(w_uq_b, w_ukv_b, wpa_b, wpb_b, wout_b):
    return jnp.concatenate([w_uq_b.reshape(144, D_MODEL), w_ukv_b.reshape(128, D_MODEL), wpa_b, wpb_b, wout_b], axis=0)


def _unpack_rest(p):
    return (p[0:144].reshape(Q_LORA, 384), p[144:272].reshape(KV_LORA, 512), p[272:528], p[528:784], p[784:1040])


def _pack_rest_grads(d_wuq3, d_wukv3, d_proj):
    d_wuq = d_wuq3.transpose(1, 0, 2)[:, :, 0:QK_DIM].reshape(Q_LORA, HEADS * QK_DIM)
    d_wukv = d_wukv3.transpose(1, 0, 2).reshape(KV_LORA, HEADS * 256)
    blocks = []
    for b in range(N_CHIPS):
        rows = slice(b * 256, (b + 1) * 256)
        blocks.append(_pack_rest(d_wuq[:, b * 384:(b + 1) * 384], d_wukv[:, b * 512:(b + 1) * 512],
                                 d_proj[0][rows], d_proj[1][rows], d_proj[2][rows]))
    return jnp.stack(blocks, axis=0)


def _unpack_rest_weights(g):
    parts = [_unpack_rest(g[b]) for b in range(N_CHIPS)]
    w_uq, w_ukv = (jnp.concatenate([p[n] for p in parts], axis=1) for n in range(2))
    wproj = jnp.stack([jnp.concatenate([p[n] for p in parts], axis=0) for n in range(2, 5)], axis=0)
    return w_uq, w_ukv, wproj


MESH_ID = pl.DeviceIdType.MESH
ANY = pl.BlockSpec(memory_space=pl.ANY)
HALF_COLS = D_MODEL // 2


def _me():
    return lax.axis_index("x"), lax.axis_index("y"), lax.axis_index("c")


def _other_chips(x, y):
    return [(1 - x, y), (x, 1 - y), (1 - x, 1 - y)]


def _cols(c):
    return pl.ds(c * HALF_COLS, HALF_COLS)


def _gather_weights(w_blk, r_blk):
    def body(w_ref, r_ref, ow_ref, or_ref, send_sems, recv_sems):
        x, y, c = _me()
        chips = _other_chips(x, y)
        me = 2 * x + y
        pairs = [(w_ref, ow_ref), (r_ref, or_ref)]

        def copy(k, src, dst, to):
            return pltpu.make_async_remote_copy(src_ref=src, dst_ref=dst, send_sem=send_sems.at[k],
                                                recv_sem=recv_sems.at[k], device_id=to, device_id_type=MESH_ID)

        first =[copy(6 * a + j, src.at[:, _cols(c)], dst.at[me, :, _cols(c)], (cx, cy, c))
                 for a, (src, dst) in enumerate(pairs) for j, (cx, cy) in enumerate(chips)]
        for cp in first:
            cp.start()
        passed = []
        for a, (src, dst) in enumerate(pairs):
            for j, (cx, cy) in enumerate(chips):
                landed = dst.at[2 * cx + cy, :, _cols(c)]
                copy(6 * a + j, landed, landed, (cx, cy, c)).wait_recv()
                fwd = copy(6 * a + 3 + j, landed, landed, (x, y, 1 - c))
                fwd.start()
                passed.append(fwd)
        for a, (src, dst) in enumerate(pairs):
            for j, (cx, cy) in enumerate(chips):
                theirs = dst.at[2 * cx + cy, :, _cols(1 - c)]
                copy(6 * a + 3 + j, theirs, theirs, (x, y, 1 - c)).wait_recv()
        for cp in first + passed:
            cp.wait_send()

    gw, gr = pl.pallas_call(
        body, name="gather_weights", in_specs=[ANY, ANY], out_specs=[ANY, ANY],
        out_shape=[jax.ShapeDtypeStruct((N_CHIPS,) + w_blk.shape, w_blk.dtype),
                   jax.ShapeDtypeStruct((N_CHIPS,) + r_blk.shape, r_blk.dtype)],
        scratch_shapes=[pltpu.SemaphoreType.DMA((12,)), pltpu.SemaphoreType.DMA((12,))],
    )(w_blk, r_blk)
    chip = 2 * lax.axis_index("x") + lax.axis_index("y")
    return (lax.dynamic_update_slice(gw, w_blk[None], (chip, 0, 0)),
            lax.dynamic_update_slice(gr, r_blk[None], (chip, 0, 0)))


def _swap_halves(gw, gr):
    def body(gw_ref, gr_ref, lw_ref, lr_ref, send_sems, recv_sems):
        x, y, c = _me()
        cps = [pltpu.make_async_remote_copy(
            src_ref=src, dst_ref=dst, send_sem=send_sems.at[a], recv_sem=recv_sems.at[a],
            device_id=(x, y, 1 - c), device_id_type=MESH_ID)
            for a, (src, dst) in enumerate([(gw_ref.at[:, _cols(1 - c)], lw_ref),
                                            (gr_ref.at[:, :, _cols(1 - c)], lr_ref)])]
        for cp in cps:
            cp.start()
        for cp in cps:
            cp.wait()

    return pl.pallas_call(
        body, name="grad_swap_halves", in_specs=[ANY, ANY], out_specs=[ANY, ANY],
        out_shape=[jax.ShapeDtypeStruct((gw.shape[0], HALF_COLS), gw.dtype),
                   jax.ShapeDtypeStruct(gr.shape[:2] + (HALF_COLS,), gr.dtype)],
        scratch_shapes=[pltpu.SemaphoreType.DMA((2,)), pltpu.SemaphoreType.DMA((2,))],
    )(gw, gr)


def _scatter_blocks(hw, hr):
    nw, nr = hw.shape[0] // N_CHIPS, hr.shape[0] // N_CHIPS

    def body(hw_ref, hr_ref, lw_ref, lr_ref, send_sems, recv_sems):
        x, y, c = _me()
        cps = []
        for a, (src, dst, n) in enumerate([(hw_ref, lw_ref, nw), (hr_ref, lr_ref, nr)]):
            for j, (cx, cy) in enumerate(_other_chips(x, y)):
                cps.append(pltpu.make_async_remote_copy(
                    src_ref=src.at[pl.ds((2 * cx + cy) * n, n), :], dst_ref=dst.at[j], send_sem=send_sems.at[3 * a + j],
                    recv_sem=recv_sems.at[3 * a + j], device_id=(cx, cy, c), device_id_type=MESH_ID))
        for cp in cps:
            cp.start()
        for cp in cps:
            cp.wait()

    return pl.pallas_call(
        body, name="grad_scatter_blocks", in_specs=[ANY, ANY], out_specs=[ANY, ANY],
        out_shape=[jax.ShapeDtypeStruct((3, nw, HALF_COLS), hw.dtype), jax.ShapeDtypeStruct((3, nr, HALF_COLS), hr.dtype)],
        scratch_shapes=[pltpu.SemaphoreType.DMA((6,)), pltpu.SemaphoreType.DMA((6,))],
    )(hw, hr)


def _swap_reduced(rw, rr):
    def body(rw_ref, rr_ref, ow_ref, or_ref, send_sems, recv_sems):
        x, y, c = _me()
        cps = [pltpu.make_async_remote_copy(
            src_ref=src, dst_ref=dst, send_sem=send_sems.at[a], recv_sem=recv_sems.at[a],
            device_id=(x, y, 1 - c), device_id_type=MESH_ID)
            for a, (src, dst) in enumerate([(rw_ref, ow_ref), (rr_ref, or_ref)])]
        for cp in cps:
            cp.start()
        for cp in cps:
            cp.wait()

    return pl.pallas_call(
        body, name="grad_swap_reduced", in_specs=[ANY, ANY], out_specs=[ANY, ANY],
        out_shape=[jax.ShapeDtypeStruct(rw.shape, rw.dtype), jax.ShapeDtypeStruct(rr.shape, rr.dtype)],
        scratch_shapes=[pltpu.SemaphoreType.DMA((2,)), pltpu.SemaphoreType.DMA((2,))],
    )(rw, rr)


def _join_cols(mine, theirs):
    first = lax.axis_index("c") == 0
    return jnp.concatenate([jnp.where(first, mine, theirs), jnp.where(first, theirs, mine)], axis=1)


def _gather_small(vec):
    def body(v_ref, out_ref, send_sems, recv_sems, local_sem):
        x, y, c = _me()
        my_id = 4 * x + 2 * y + c
        mine = pltpu.make_async_copy(v_ref, out_ref.at[my_id], local_sem)
        mine.start()
        cps = []
        for r in range(1, N_DEV):
            peer = (x ^ (r >> 2), y ^ ((r >> 1) & 1), c ^ (r & 1))
            cps.append(pltpu.make_async_remote_copy(
                src_ref=v_ref, dst_ref=out_ref.at[my_id], send_sem=send_sems.at[r - 1],
                recv_sem=recv_sems.at[r - 1], device_id=peer, device_id_type=MESH_ID))
        for cp in cps:
            cp.start()
        for cp in cps:
            cp.wait()
        mine.wait()

    return pl.pallas_call(
        body, name="gather_small", in_specs=[ANY], out_specs=ANY,
        out_shape=jax.ShapeDtypeStruct((N_DEV, 1, SMALL_COLS), vec.dtype),
        scratch_shapes=[pltpu.SemaphoreType.DMA((N_DEV - 1,)), pltpu.SemaphoreType.DMA((N_DEV - 1,)),
                        pltpu.SemaphoreType.DMA],
    )(vec)


def _add_cores(c_idx, g, landed, *, tm, name):
    r = g.shape[0]

    def body(c_ref, g_ref, l_ref, o32_ref, o16_ref):
        acc = g_ref[...] + l_ref[...]
        o32_ref[...] = acc
        o16_ref[...] = acc.astype(BF16)

    half = pl.BlockSpec((tm, HALF_COLS), lambda i, c_ref: (i, 0))
    grid_spec = pltpu.PrefetchScalarGridSpec(
        num_scalar_prefetch=1, grid=(r // tm,),
        in_specs=[pl.BlockSpec((tm, HALF_COLS), lambda i, c_ref: (i, c_ref[0])), half], out_specs=[half, half])
    return pl.pallas_call(
        body, name=name, grid_spec=grid_spec,
        out_shape=[jax.ShapeDtypeStruct((r, HALF_COLS), F32), jax.ShapeDtypeStruct((r, HALF_COLS), BF16)],
        compiler_params=_params(1),
    )(c_idx, g, landed)


def _add_chips(chip_idx, h32, landed, *, tm, name):
    n = landed.shape[1]
    per = n // tm

    def body(chip_ref, h_ref, l_ref, o_ref):
        acc = h_ref[...]
        for j in range(3):
            acc = acc + l_ref[j].astype(F32)
        o_ref[...] = acc

    grid_spec = pltpu.PrefetchScalarGridSpec(
        num_scalar_prefetch=1, grid=(per,),
        in_specs=[pl.BlockSpec((tm, HALF_COLS), lambda i, chip_ref: (chip_ref[0] * per + i, 0)),
                  pl.BlockSpec((3, tm, HALF_COLS), lambda i, chip_ref: (0, i, 0))],
        out_specs=pl.BlockSpec((tm, HALF_COLS), lambda i, chip_ref: (i, 0)))
    return pl.pallas_call(
        body, name=name, grid_spec=grid_spec, out_shape=jax.ShapeDtypeStruct((n, HALF_COLS), F32),
        compiler_params=_params(1),
    )(chip_idx, h32, landed)


def _pack_small(small, lb_logits, loss):
    def body(ng_ref, bg_ref, dlb_ref, lbl_ref, hgg_ref, qg_ref, kvg_ref, fg_ref, loss_ref, out_ref):
        out_ref[...] = jnp.zeros_like(out_ref)
        out_ref[:, 0:1024] = ng_ref[...]
        out_ref[:, 1024:3072] = bg_ref[...]
        _, p0p1 = _lower_bound(lbl_ref[...])
        dl0 = dlb_ref[...] * p0p1
        out_ref[:, 3072:4096] = dl0
        out_ref[:, 4096:5120] = -dl0
        hgg = hgg_ref[0]
        for h in range(1, HEADS):
            hgg = hgg + hgg_ref[h]
        out_ref[:, 5120:5248] = hgg
        out_ref[:, 5248:5632] = qg_ref[...]
        out_ref[:, 5632:5888] = kvg_ref[...]
        out_ref[:, 5888:6912] = fg_ref[...]
        out_ref[:, 6912:7040] = loss_ref[...]

    return pl.pallas_call(
        body, name="pack_small", out_shape=jax.ShapeDtypeStruct((1, SMALL_COLS), F32),
    )(small["norm_g"], small["b_gate"], small["lb"], lb_logits, small["hg_norm_g"], small["q_a_g"],
      small["kv_a_g"], small["final_norm_g"], loss)


def _sum_small(gathered):
    def body(g_ref, out_ref):
        acc = g_ref[0]
        for d in range(1, N_DEV):
            acc = acc + g_ref[d]
        out_ref[...] = acc

    return pl.pallas_call(
        body, name="sum_small", out_shape=jax.ShapeDtypeStruct((1, SMALL_COLS), F32),
    )(gathered)


def _adamw(w, g, m, v, *, name, tm):
    r, cols = w.shape
    c1 = 1.0 - ADAM_B1 ** ADAM_STEP
    c2 = 1.0 - ADAM_B2 ** ADAM_STEP

    def body(w_ref, g_ref, m_ref, v_ref, d_ref, nm_ref, nv_ref):
        gv = g_ref[...]
        nm = ADAM_B1 * m_ref[...] + (1.0 - ADAM_B1) * gv
        nv = ADAM_B2 * v_ref[...] + (1.0 - ADAM_B2) * (gv * gv)
        nm_ref[...] = nm
        nv_ref[...] = nv
        d_ref[...] = -ADAM_LR * ((nm / c1) / (jnp.sqrt(nv / c2) + ADAM_EPS) + ADAM_WD * w_ref[...])

    row = pl.BlockSpec((tm, cols), lambda i: (i, 0))
    shp = jax.ShapeDtypeStruct((r, cols), F32)
    return pl.pallas_call(
        body, name=name, grid=(r // tm,), in_specs=[row] * 4, out_specs=[row] * 3, out_shape=[shp] * 3,
        compiler_params=_params(1),
    )(w, g, m, v)


def _small_vec(norm_g, b_gate, lb_logits, hg_norm_g, q_a_g, kv_a_g, fg):
    parts = [norm_g.reshape(1, -1), b_gate.reshape(1, -1), lb_logits.reshape(1, -1), hg_norm_g.reshape(1, -1),
             q_a_g.reshape(1, -1), kv_a_g.reshape(1, -1), fg.reshape(1, -1), jnp.zeros((1, SMALL_COLS - 6912), F32)]
    return jnp.concatenate(parts, axis=1)


def _split_small(vec):
    v = vec.reshape(-1)
    return (v[0:1024].reshape(1, 1024), v[1024:3072].reshape(1, 2048), v[3072:5120].reshape(2, 1024),
            v[5120:5248].reshape(1, 128), v[5248:5632].reshape(1, 384), v[5632:5888].reshape(1, 256), v[5888:6912])


def kernel(x, norm_g, w_in, b_gate, lb_logits, hg_norm_g, q_a_g, w_uq, kv_a_g, w_ukv, w_proj_a, w_proj_b, w_out, final_norm_g, loss_target, m_norm_g, m_w_in, m_b_gate, m_lb_logits, m_hg_norm_g, m_q_a_g, m_w_uq, m_kv_a_g, m_w_ukv, m_w_proj_a, m_w_proj_b, m_w_out, m_final_norm_g, v_norm_g, v_w_in, v_b_gate, v_lb_logits, v_hg_norm_g, v_q_a_g, v_w_uq, v_kv_a_g, v_w_ukv, v_w_proj_a, v_w_proj_b, v_w_out, v_final_norm_g):
    c_idx = lax.axis_index("c").astype(jnp.int32).reshape(1)
    chip_idx = (2 * lax.axis_index("x") + lax.axis_index("y")).astype(jnp.int32).reshape(1)

    w_blk = w_in[0].T.astype(BF16)
    r_blk = _pack_rest(w_uq[0], w_ukv[0], w_proj_a[0], w_proj_b[0], w_out[0]).astype(BF16)
    gw, gr = _gather_weights(w_blk, r_blk)
    fw_uq, fw_ukv, fwproj = _unpack_rest_weights(gr)

    loss, grad_x, d_w_int, d_wuq3, d_wukv3, d_proj, small = _local_step(
        x[0], loss_target[0], gw.reshape(W_IN_COLS, D_MODEL), fw_uq, fw_ukv, fwproj,
        norm_g, b_gate, lb_logits, hg_norm_g, q_a_g, kv_a_g, final_norm_g.reshape(1, D_MODEL))

    d_rest = _pack_rest_grads(d_wuq3, d_wukv3, d_proj)
    lw, lr = _swap_halves(d_w_int, d_rest)
    hw32, hw16 = _add_cores(c_idx, d_w_int, lw, tm=656, name="grad_add_cores_w")
    hr32, hr16 = _add_cores(c_idx, d_rest.reshape(N_CHIPS * REST_ROWS, D_MODEL), lr.reshape(N_CHIPS * REST_ROWS, HALF_COLS),
                            tm=REST_ROWS, name="grad_add_cores_r")
    landed_w, landed_r = _scatter_blocks(hw16, hr16)
    rw = _add_chips(chip_idx, hw32, landed_w, tm=656, name="grad_add_chips_w")
    rr = _add_chips(chip_idx, hr32, landed_r, tm=208, name="grad_add_chips_r")
    tw, tr = _swap_reduced(rw, rr)
    g_w_in = _join_cols(rw, tw).T
    g_rest = _join_cols(rr, tr)
    g_uq, g_ukv, g_pa, g_pb, g_out = _unpack_rest(g_rest)

    small_sum = _sum_small(_gather_small(_pack_small(small, lb_logits, loss)))

    upd = {
        "w_in": _adamw(w_in[0], g_w_in, m_w_in[0], v_w_in[0], name="adamw_w_in", tm=128),
        "w_uq": _adamw(w_uq[0], g_uq, m_w_uq[0], v_w_uq[0], name="adamw_w_uq", tm=Q_LORA),
        "w_ukv": _adamw(w_ukv[0], g_ukv, m_w_ukv[0], v_w_ukv[0], name="adamw_w_ukv", tm=KV_LORA),
        "w_proj_a": _adamw(w_proj_a[0], g_pa, m_w_proj_a[0], v_w_proj_a[0], name="adamw_w_proj_a", tm=256),
        "w_proj_b": _adamw(w_proj_b[0], g_pb, m_w_proj_b[0], v_w_proj_b[0], name="adamw_w_proj_b", tm=256),
        "w_out": _adamw(w_out[0], g_out, m_w_out[0], v_w_out[0], name="adamw_w_out", tm=256),
    }
    ws = _small_vec(norm_g, b_gate, lb_logits, hg_norm_g, q_a_g, kv_a_g, final_norm_g).reshape(7, 1024)
    ms_ = _small_vec(m_norm_g, m_b_gate, m_lb_logits, m_hg_norm_g, m_q_a_g, m_kv_a_g, m_final_norm_g).reshape(7, 1024)
    vs = _small_vec(v_norm_g, v_b_gate, v_lb_logits, v_hg_norm_g, v_q_a_g, v_kv_a_g, v_final_norm_g).reshape(7, 1024)
    upd_small = _adamw(ws, small_sum.reshape(7, 1024), ms_, vs, name="adamw_small", tm=7)

    def outputs(big, vec):
        s_ng, s_bg, s_lb, s_hg, s_qg, s_kvg, s_fg = _split_small(vec)
        return (s_ng, big["w_in"][None], s_bg, s_lb, s_hg, s_qg, big["w_uq"][None], s_kvg, big["w_ukv"][None],
                big["w_proj_a"][None], big["w_proj_b"][None], big["w_out"][None], s_fg)

    grads = {"w_in": g_w_in, "w_uq": g_uq, "w_ukv": g_ukv, "w_proj_a": g_pa, "w_proj_b": g_pb, "w_out": g_out}
    total_loss = small_sum[0, 6912]
    return (total_loss, grad_x[None], *outputs(grads, small_sum),
            *(o for k in range(3) for o in outputs({n: u[k] for n, u in upd.items()}, upd_small[k])))
```

```python
import functools

import jax
import jax.numpy as jnp
from jax import lax
from jax.experimental import pallas as pl
from jax.experimental.pallas import tpu as pltpu

F32 = jnp.float32
BF16 = jnp.bfloat16

D_MODEL = 1024
HEADS = 8
HEAD_DIM = 128
HG_CHUNK = 32
CHUNK_SHIFT = 5
HEAD_SHIFT = 7
QK_NOPE = 128
QK_ROPE = 64
QK_DIM = QK_NOPE + QK_ROPE
QK_PAD = 256
Q_LORA = 384
KV_LORA = 256
MS_COLS = 768
ROPE_THETA = 10000.0
EPS = 1e-6
ATT_SCALE = QK_DIM ** -0.5
LOG2E = 1.4426950408889634
LN2 = 0.6931471805599453
Q_PRESCALE = ATT_SCALE * LOG2E

ADAM_LR = 0.001
ADAM_B1 = 0.9
ADAM_B2 = 0.999
ADAM_EPS = 1e-08
ADAM_WD = 0.01
ADAM_STEP = 10

N_CHIPS = 4
N_DEV = 8
W_IN_COLS = 7872
W_IN_BLK = W_IN_COLS // N_CHIPS
REST_ROWS = 144 + 128 + 3 * 256
SMALL_COLS = 7168

TM_MM = 1024
TM_FUSED = 256
HG_ROWS = 256
TQ = 512
FLASH_HEADS = 2
HG_HEADS = 2
VMEM_LIMIT = 56 * 1024 * 1024


def _dot(a, b):
    return lax.dot_general(a, b, (((1,), (0,)), ((), ())), preferred_element_type=F32)


def _dot_nt(a, b):
    return lax.dot_general(a, b, (((1,), (1,)), ((), ())), preferred_element_type=F32)


def _dot_tn(a, b):
    return lax.dot_general(a, b, (((0,), (0,)), ((), ())), preferred_element_type=F32)


def _params(n_axes):
    return pltpu.CompilerParams(dimension_semantics=("arbitrary",) * n_axes, vmem_limit_bytes=VMEM_LIMIT)


def _rms(x, g):
    r = lax.rsqrt(jnp.mean(x * x, axis=-1, keepdims=True) + EPS)
    return x * r * g


def _rms_bwd(x, g, dy):
    r = lax.rsqrt(jnp.mean(x * x, axis=-1, keepdims=True) + EPS)
    xh = x * r
    dyg = dy * g
    dx = r * (dyg - xh * jnp.mean(dyg * xh, axis=-1, keepdims=True))
    return dx, dy * xh


def _silu_parts(z):
    s = jax.nn.sigmoid(z)
    return z * s, s * (1.0 + z * (1.0 - s))


def _rope(x, c, sa, sb):
    return x * c + pltpu.roll(x, 32, 1) * sa + pltpu.roll(x, 96, 1) * sb


def _rope_bwd(dy, c, sa, sb):
    return dy * c + pltpu.roll(dy * sa, 96, 1) + pltpu.roll(dy * sb, 32, 1)


def _rope_tables(seq):
    inv = ROPE_THETA ** (-jnp.arange(0, QK_ROPE, 2, dtype=F32) / QK_ROPE)
    ang = jnp.arange(seq, dtype=F32)[:, None] * inv[None, :]
    cos, sin = jnp.cos(ang), jnp.sin(ang)
    z32 = jnp.zeros_like(cos)
    z64 = jnp.zeros((seq, 64), F32)
    c = jnp.concatenate([cos, cos, z64], axis=1)
    sa = jnp.concatenate([z32, sin, z64], axis=1)
    sb = jnp.concatenate([-sin, z32, z64], axis=1)
    return c, sa, sb


def _mm(a, b, *, name, trans_b=False, add=None, out_dtype=F32, tm=TM_MM, tn=1024, tk=1024):
    m, k = a.shape
    n = b.shape[0] if trans_b else b.shape[1]
    tm, tn, tk = min(tm, m), min(tn, n), min(tk, k)
    assert m % tm == 0 and n % tn == 0 and k % tk == 0
    nk = k // tk
    has_add = add is not None

    def body(*refs):
        if has_add:
            a_ref, b_ref, add_ref, o_ref, acc_ref = refs
        else:
            a_ref, b_ref, o_ref, acc_ref = refs
        kk = pl.program_id(2)

        @pl.when(kk == 0)
        def _():
            acc_ref[...] = add_ref[...] if has_add else jnp.zeros_like(acc_ref)

        if trans_b:
            acc_ref[...] += _dot_nt(a_ref[...], b_ref[...])
        else:
            acc_ref[...] += _dot(a_ref[...], b_ref[...])

        @pl.when(kk == nk - 1)
        def _():
            o_ref[...] = acc_ref[...].astype(out_dtype)

    in_specs = [pl.BlockSpec((tm, tk), lambda i, j, kk: (i, kk))]
    if trans_b:
        in_specs.append(pl.BlockSpec((tn, tk), lambda i, j, kk: (j, kk)))
    else:
        in_specs.append(pl.BlockSpec((tk, tn), lambda i, j, kk: (kk, j)))
    args = [a, b]
    if has_add:
        in_specs.append(pl.BlockSpec((tm, tn), lambda i, j, kk: (i, j)))
        args.append(add)
    return pl.pallas_call(
        body, name=name, grid=(m // tm, n // tn, nk),
        in_specs=in_specs, out_specs=pl.BlockSpec((tm, tn), lambda i, j, kk: (i, j)),
        out_shape=jax.ShapeDtypeStruct((m, n), out_dtype),
        scratch_shapes=[pltpu.VMEM((tm, tn), F32)], compiler_params=_params(3),
    )(*args)


def _mm_tn(a, b, *, name, tm=TM_MM, tn=1024):
    flat = a.ndim == 2
    if flat:
        a = a[None]
    g, m, k = a.shape
    n = b.shape[1]
    tm, tn = min(tm, m), min(tn, n)
    assert m % tm == 0 and n % tn == 0

    def body(a_ref, b_ref, o_ref):
        @pl.when(pl.program_id(2) == 0)
        def _():
            o_ref[...] = jnp.zeros_like(o_ref)

        o_ref[...] += _dot_tn(a_ref[...], b_ref[...])

    out = pl.pallas_call(
        body, name=name, grid=(g, n // tn, m // tm),
        in_specs=[pl.BlockSpec((None, tm, k), lambda s, j, i: (s, i, 0)),
                  pl.BlockSpec((tm, tn), lambda s, j, i: (i, j))],
        out_specs=pl.BlockSpec((None, k, tn), lambda s, j, i: (s, 0, j)),
        out_shape=jax.ShapeDtypeStruct((g, k, n), F32), compiler_params=_params(3),
    )(a, b)
    return out[0] if flat else out


def _mm_slabs_out(a, wt, slabs, *, name, tm=TM_MM):
    m, k = a.shape
    tm = min(tm, m)
    n = D_MODEL

    def body(a_ref, w_ref, o_ref):
        o_ref[...] = _dot_nt(a_ref[...], w_ref[...])

    return pl.pallas_call(
        body, name=name, grid=(m // tm, slabs),
        in_specs=[pl.BlockSpec((tm, k), lambda i, j: (i, 0)), pl.BlockSpec((n, k), lambda i, j: (j, 0))],
        out_specs=pl.BlockSpec((None, tm, n), lambda i, j: (j, i, 0)),
        out_shape=jax.ShapeDtypeStruct((slabs, m, n), F32), compiler_params=_params(2),
    )(a, wt)


def _mm_slabs_in(a3, w, *, name, tm=TM_MM):
    slabs, m, k = a3.shape
    n = w.shape[1]
    tm = min(tm, m)

    def body(a_ref, w_ref, o_ref, acc_ref):
        j = pl.program_id(1)

        @pl.when(j == 0)
        def _():
            acc_ref[...] = jnp.zeros_like(acc_ref)

        acc_ref[...] += _dot(a_ref[...], w_ref[...])

        @pl.when(j == slabs - 1)
        def _():
            o_ref[...] = acc_ref[...]

    return pl.pallas_call(
        body, name=name, grid=(m // tm, slabs),
        in_specs=[pl.BlockSpec((None, tm, k), lambda i, j: (j, i, 0)), pl.BlockSpec((k, n), lambda i, j: (j, 0))],
        out_specs=pl.BlockSpec((tm, n), lambda i, j: (i, 0)),
        out_shape=jax.ShapeDtypeStruct((m, n), F32),
        scratch_shapes=[pltpu.VMEM((tm, n), F32)], compiler_params=_params(2),
    )(a3, w)


def _norm_in(x, g):
    s = x.shape[0]
    tm = min(TM_MM, s)

    def body(x_ref, g_ref, h_ref):
        h_ref[...] = _rms(x_ref[...], g_ref[...]).astype(BF16)

    return pl.pallas_call(
        body, name="norm_in", grid=(s // tm,),
        in_specs=[pl.BlockSpec((tm, D_MODEL), lambda i: (i, 0)), pl.BlockSpec((1, D_MODEL), lambda i: (0, 0))],
        out_specs=pl.BlockSpec((tm, D_MODEL), lambda i: (i, 0)),
        out_shape=jax.ShapeDtypeStruct((s, D_MODEL), BF16), compiler_params=_params(1),
    )(x, g)


def _norm_in_bwd(x, g, dh, dx2):
    s = x.shape[0]
    tm = min(TM_MM, s)

    def body(x_ref, g_ref, dh_ref, dx2_ref, dx_ref, dg_ref):
        @pl.when(pl.program_id(0) == 0)
        def _():
            dg_ref[...] = jnp.zeros_like(dg_ref)

        dx, dg_rows = _rms_bwd(x_ref[...], g_ref[...], dh_ref[...])
        dx_ref[...] = dx + dx2_ref[...]
        dg_ref[...] += jnp.sum(dg_rows, axis=0, keepdims=True)

    row = pl.BlockSpec((tm, D_MODEL), lambda i: (i, 0))
    vec = pl.BlockSpec((1, D_MODEL), lambda i: (0, 0))
    return pl.pallas_call(
        body, name="norm_in_bwd", grid=(s // tm,),
        in_specs=[row, vec, row, row], out_specs=[row, vec],
        out_shape=[jax.ShapeDtypeStruct((s, D_MODEL), F32), jax.ShapeDtypeStruct((1, D_MODEL), F32)],
        compiler_params=_params(1),
    )(x, g, dh, dx2)


def _chunk_rows(rows):
    return lax.broadcasted_iota(jnp.int32, (rows, HEAD_DIM), 0) & (HG_CHUNK - 1)


def _chunk_cumsum(x, rows):
    pos = _chunk_rows(rows)
    shift = 1
    while shift < HG_CHUNK:
        x = x + jnp.where(pos >= shift, pltpu.roll(x, shift, 0), 0.0)
        shift *= 2
    return x


def _chunk_revcumsum(x, rows):
    pos = _chunk_rows(rows)
    shift = 1
    while shift < HG_CHUNK:
        x = x + jnp.where(pos + shift < HG_CHUNK, pltpu.roll(x, rows - shift, 0), 0.0)
        shift *= 2
    return x


def _lower_bound(lbl):
    mx = jnp.maximum(lbl[0:1, :], lbl[1:2, :])
    e0 = jnp.exp(lbl[0:1, :] - mx)
    e1 = jnp.exp(lbl[1:2, :] - mx)
    p0 = e0 / (e0 + e1)
    return p0, p0 * (e1 / (e0 + e1))


def _hg_masks(rows, nch, tmask_s, bdmask_s):
    r = lax.broadcasted_iota(jnp.int32, (rows, rows), 0)
    c = lax.broadcasted_iota(jnp.int32, (rows, rows), 1)
    tmask_s[...] = jnp.where(((r >> CHUNK_SHIFT) == (c >> CHUNK_SHIFT)) & (r >= c), 1.0, 0.0)
    r = lax.broadcasted_iota(jnp.int32, (rows, nch * HEAD_DIM), 0)
    c = lax.broadcasted_iota(jnp.int32, (rows, nch * HEAD_DIM), 1)
    bdmask_s[...] = jnp.where((r >> CHUNK_SHIFT) == (c >> HEAD_SHIFT), 1.0, 0.0).astype(BF16)


def _block_diag(x, nch, bdmask):
    return jnp.tile(x, (1, nch)) * bdmask


def _hgrn_fwd(hg, lb_logits, norm_g):
    s = hg.shape[1]
    rows = min(HG_ROWS, s)
    nblk = s // rows
    nch = rows // HG_CHUNK

    def body(hg_ref, lbl_ref, g_ref, o_ref, ya_ref, st0_ref, st_s, stall_s, tmask_s, bdmask_s):
        @pl.when(pl.program_id(1) == 0)
        def _():
            st_s[...] = jnp.zeros_like(st_s)
            _hg_masks(rows, nch, tmask_s, bdmask_s)

        bdmask = bdmask_s[...]
        tmask = tmask_s[...] > 0.5
        for hh in range(HG_HEADS):
            hc = slice(hh * HEAD_DIM, (hh + 1) * HEAD_DIM)
            hq = hg_ref[0, :, hc]
            hf = hg_ref[1, :, hc]
            hi = hg_ref[2, :, hc]
            hz = hg_ref[3, :, hc]
            lb, _ = _lower_bound(lbl_ref[:, hc])
            f = lb + (1.0 - lb) * jax.nn.sigmoid(hf)
            q = hq * jax.nn.sigmoid(hq)
            k = 1.0 - f
            logf = jnp.log(f)
            b = _chunk_cumsum(logf, rows)
            q_in = (q * jnp.exp(b)).astype(BF16)
            k_in = (k * jnp.exp(-b)).astype(BF16)
            k_out = (k * jnp.exp(_chunk_revcumsum(logf, rows) - logf)).astype(BF16)
            vb = hi.astype(BF16)

            sc = jnp.where(tmask, _dot_nt(q_in, k_in), 0.0)
            o_intra = _dot(sc.astype(BF16), vb)
            kvt = _dot_tn(vb, _block_diag(k_out, nch, bdmask))
            st = st_s[hh]
            st0_ref[hh] = st
            for c in range(nch):
                cols = slice(c * HEAD_DIM, (c + 1) * HEAD_DIM)
                last = (c + 1) * HG_CHUNK - 1
                stall_s[hh, :, cols] = st.astype(BF16)
                st = st * jnp.exp(b[last:last + 1, :]) + kvt[:, cols]
            st_s[hh] = st
            o = o_intra + _dot_nt(_block_diag(q_in, nch, bdmask), stall_s[hh])
            o_ref[:, hc] = o
            silu_z, _ = _silu_parts(hz)
            ya_ref[:, hc] = (_rms(o, g_ref[...]) * silu_z).astype(BF16)

    nh = HG_HEADS
    return pl.pallas_call(
        body, name="hgrn_fwd", grid=(HEADS // nh, nblk),
        in_specs=[pl.BlockSpec((4, rows, nh * HEAD_DIM), lambda h, i: (0, i, h)),
                  pl.BlockSpec((2, nh * HEAD_DIM), lambda h, i: (0, h)),
                  pl.BlockSpec((1, HEAD_DIM), lambda h, i: (0, 0))],
        out_specs=[pl.BlockSpec((rows, nh * HEAD_DIM), lambda h, i: (i, h)),
                   pl.BlockSpec((rows, nh * HEAD_DIM), lambda h, i: (i, h)),
                   pl.BlockSpec((nh, None, HEAD_DIM, HEAD_DIM), lambda h, i: (h, i, 0, 0))],
        out_shape=[jax.ShapeDtypeStruct((s, D_MODEL), F32), jax.ShapeDtypeStruct((s, D_MODEL), BF16),
                   jax.ShapeDtypeStruct((HEADS, nblk, HEAD_DIM, HEAD_DIM), F32)],
        scratch_shapes=[pltpu.VMEM((nh, HEAD_DIM, HEAD_DIM), F32), pltpu.VMEM((nh, HEAD_DIM, nch * HEAD_DIM), BF16),
                        pltpu.VMEM((rows, rows), F32), pltpu.VMEM((rows, nch * HEAD_DIM), BF16)],
        compiler_params=_params(2),
    )(hg, lb_logits, norm_g)


def _hgrn_bwd(hg, o_pre, dya, st0, lb_logits, norm_g):
    s = hg.shape[1]
    rows = min(HG_ROWS, s)
    nblk = s // rows
    nch = rows // HG_CHUNK

    def body(hg_ref, o_ref, dya_ref, st0_ref, lbl_ref, g_ref, dhg_ref, dlb_ref, dg_ref,
             dst_s, stp_s, stp_rows_s, dst_rows_s, dst_lane_s, dbl_s, tmask_s, bdmask_s):
        @pl.when(pl.program_id(1) == 0)
        def _():
            dst_s[...] = jnp.zeros_like(dst_s)
            dlb_ref[...] = jnp.zeros_like(dlb_ref)
            dg_ref[...] = jnp.zeros_like(dg_ref)
            _hg_masks(rows, nch, tmask_s, bdmask_s)

        bdmask = bdmask_s[...]
        tmask = tmask_s[...] > 0.5
        g = g_ref[...]
        for hh in range(HG_HEADS):
            hc = slice(hh * HEAD_DIM, (hh + 1) * HEAD_DIM)
            hq = hg_ref[0, :, hc]
            hf = hg_ref[1, :, hc]
            hi = hg_ref[2, :, hc]
            hz = hg_ref[3, :, hc]
            lb, _ = _lower_bound(lbl_ref[:, hc])
            sg = jax.nn.sigmoid(hf)
            f = lb + (1.0 - lb) * sg
            q, dsilu_q = _silu_parts(hq)
            k = 1.0 - f
            logf = jnp.log(f)
            b = _chunk_cumsum(logf, rows)
            eb = jnp.exp(b)
            enb = jnp.exp(-b)
            ebl = jnp.exp(_chunk_revcumsum(logf, rows) - logf)
            q_in32 = q * eb
            k_in32 = k * enb
            k_out32 = k * ebl
            q_in = q_in32.astype(BF16)
            k_in = k_in32.astype(BF16)
            k_out = k_out32.astype(BF16)
            vb = hi.astype(BF16)
            kbd = _block_diag(k_out, nch, bdmask)
            qbd = _block_diag(q_in, nch, bdmask)
            decs = [jnp.exp(b[(c + 1) * HG_CHUNK - 1:(c + 1) * HG_CHUNK, :]) for c in range(nch)]

            kvt = _dot_tn(vb, kbd)
            st = st0_ref[hh]
            for c in range(nch):
                stp_s[hh, c] = st
                stp_rows_s[hh, c * HEAD_DIM:(c + 1) * HEAD_DIM, :] = st.astype(BF16)
                st = st * decs[c] + kvt[:, c * HEAD_DIM:(c + 1) * HEAD_DIM]

            o = o_ref[:, hc]
            rstd = lax.rsqrt(jnp.mean(o * o, axis=-1, keepdims=True) + EPS)
            oh = o * rstd
            silu_z, dsilu_z = _silu_parts(hz)
            dya_v = dya_ref[:, hc]
            dn = dya_v * silu_z
            dhz = dya_v * (oh * g) * dsilu_z
            dg_ref[hh] += jnp.sum(dn * oh, axis=0, keepdims=True)
            doh = dn * g
            do = (rstd * (doh - oh * jnp.mean(doh * oh, axis=-1, keepdims=True))).astype(BF16)

            dq_all = _dot_tn(do, qbd)
            dst = dst_s[hh]
            ddecs = [None] * nch
            for c in reversed(range(nch)):
                dstb = dst.astype(BF16)
                dst_lane_s[hh, :, c * HEAD_DIM:(c + 1) * HEAD_DIM] = dstb
                dst_rows_s[hh, c * HEAD_DIM:(c + 1) * HEAD_DIM, :] = dstb
                ddecs[c] = jnp.sum(dst * stp_s[hh, c], axis=0, keepdims=True) * decs[c]
                dst = dst * decs[c] + dq_all[:, c * HEAD_DIM:(c + 1) * HEAD_DIM]
            dst_s[hh] = dst

            sc = jnp.where(tmask, _dot_nt(q_in, k_in), 0.0).astype(BF16)
            dkout = _dot(_block_diag(vb, nch, bdmask), dst_rows_s[hh])
            dv = _dot_nt(kbd, dst_lane_s[hh]) + _dot_tn(sc, do)
            dsc = jnp.where(tmask, _dot_nt(do, vb), 0.0).astype(BF16)
            dqin = _dot(dsc, k_in) + _dot(_block_diag(do, nch, bdmask), stp_rows_s[hh])
            dkin = _dot_tn(dsc, q_in)

            dko = dkout * k_out32
            for c in range(nch):
                sl = slice(c * HG_CHUNK, (c + 1) * HG_CHUNK)
                dbl = jnp.sum(dko[sl], axis=0, keepdims=True) + ddecs[c]
                dbl_s[hh, sl, :] = jnp.broadcast_to(dbl, (HG_CHUNK, HEAD_DIM))
            dq = dqin * eb
            dk = dkin * enb + dkout * ebl
            db = dqin * q_in32 - dkin * k_in32 - dko
            dlogf = _chunk_revcumsum(db, rows) + dbl_s[hh]
            df = dlogf / f - dk
            dlb_ref[:, hc] += jnp.sum(df * (1.0 - sg), axis=0, keepdims=True)
            dhg_ref[0, :, hc] = (dq * dsilu_q).astype(BF16)
            dhg_ref[1, :, hc] = (df * (1.0 - lb) * sg * (1.0 - sg)).astype(BF16)
            dhg_ref[2, :, hc] = dv.astype(BF16)
            dhg_ref[3, :, hc] = dhz.astype(BF16)

    last = nblk - 1
    nh = HG_HEADS
    wide = nh * HEAD_DIM
    return pl.pallas_call(
        body, name="hgrn_bwd", grid=(HEADS // nh, nblk),
        in_specs=[pl.BlockSpec((4, rows, wide), lambda h, i: (0, last - i, h)),
                  pl.BlockSpec((rows, wide), lambda h, i: (last - i, h)),
                  pl.BlockSpec((rows, wide), lambda h, i: (last - i, h)),
                  pl.BlockSpec((nh, None, HEAD_DIM, HEAD_DIM), lambda h, i: (h, last - i, 0, 0)),
                  pl.BlockSpec((2, wide), lambda h, i: (0, h)),
                  pl.BlockSpec((1, HEAD_DIM), lambda h, i: (0, 0))],
        out_specs=[pl.BlockSpec((4, rows, wide), lambda h, i: (0, last - i, h)),
                   pl.BlockSpec((1, wide), lambda h, i: (0, h)),
                   pl.BlockSpec((nh, 1, HEAD_DIM), lambda h, i: (h, 0, 0))],
        out_shape=[jax.ShapeDtypeStruct((4, s, D_MODEL), BF16), jax.ShapeDtypeStruct((1, D_MODEL), F32),
                   jax.ShapeDtypeStruct((HEADS, 1, HEAD_DIM), F32)],
        scratch_shapes=[pltpu.VMEM((nh, HEAD_DIM, HEAD_DIM), F32), pltpu.VMEM((nh, nch, HEAD_DIM, HEAD_DIM), F32),
                        pltpu.VMEM((nh, nch * HEAD_DIM, HEAD_DIM), BF16), pltpu.VMEM((nh, nch * HEAD_DIM, HEAD_DIM), BF16),
                        pltpu.VMEM((nh, HEAD_DIM, nch * HEAD_DIM), BF16), pltpu.VMEM((nh, rows, HEAD_DIM), F32),
                        pltpu.VMEM((rows, rows), F32), pltpu.VMEM((rows, nch * HEAD_DIM), BF16)],
        compiler_params=_params(2),
    )(hg, o_pre, dya, st0, lb_logits, norm_g)


def _mla_pre(ms, q_a_g, kv_a_g, wuq3, wukv3, tabs):
    s = ms.shape[0]
    tm = min(TM_FUSED, s)

    def body(ms_ref, qg_ref, kvg_ref, wuq_ref, wukv_ref, c_ref, sa_ref, sb_ref,
             q_ref, k_ref, v_ref, cqn_ref, ckvn_ref):
        c, sa, sb = c_ref[...], sa_ref[...], sb_ref[...]
        cqn = _rms(ms_ref[:, 0:Q_LORA], qg_ref[...]).astype(BF16)
        ckvn = _rms(ms_ref[:, Q_LORA:Q_LORA + KV_LORA], kvg_ref[...]).astype(BF16)
        cqn_ref[...] = cqn
        ckvn_ref[...] = ckvn
        k_pe = _rope(ms_ref[:, Q_LORA + KV_LORA:MS_COLS], c, sa, sb).astype(BF16)
        for h in range(HEADS):
            qh = _dot(cqn, wuq_ref[h])
            q_ref[h, :, 0:128] = (qh[:, 0:128] * Q_PRESCALE).astype(BF16)
            q_ref[h, :, 128:256] = (_rope(qh[:, 128:256], c, sa, sb) * Q_PRESCALE).astype(BF16)
            kvh = _dot(ckvn, wukv_ref[h])
            k_ref[h, :, 0:128] = kvh[:, 0:128].astype(BF16)
            k_ref[h, :, 128:256] = k_pe
            v_ref[h] = kvh[:, 128:256].astype(BF16)

    tab = pl.BlockSpec((tm, 128), lambda i: (i, 0))
    return pl.pallas_call(
        body, name="mla_pre", grid=(s // tm,),
        in_specs=[pl.BlockSpec((tm, MS_COLS), lambda i: (i, 0)),
                  pl.BlockSpec((1, Q_LORA), lambda i: (0, 0)), pl.BlockSpec((1, KV_LORA), lambda i: (0, 0)),
                  pl.BlockSpec((HEADS, Q_LORA, QK_PAD), lambda i: (0, 0, 0)),
                  pl.BlockSpec((HEADS, KV_LORA, 256), lambda i: (0, 0, 0)), tab, tab, tab],
        out_specs=[pl.BlockSpec((HEADS, tm, QK_PAD), lambda i: (0, i, 0)),
                   pl.BlockSpec((HEADS, tm, QK_PAD), lambda i: (0, i, 0)),
                   pl.BlockSpec((HEADS, tm, HEAD_DIM), lambda i: (0, i, 0)),
                   pl.BlockSpec((tm, Q_LORA), lambda i: (i, 0)), pl.BlockSpec((tm, KV_LORA), lambda i: (i, 0))],
        out_shape=[jax.ShapeDtypeStruct((HEADS, s, QK_PAD), BF16), jax.ShapeDtypeStruct((HEADS, s, QK_PAD), BF16),
                   jax.ShapeDtypeStruct((HEADS, s, HEAD_DIM), BF16),
                   jax.ShapeDtypeStruct((s, Q_LORA), BF16), jax.ShapeDtypeStruct((s, KV_LORA), BF16)],
        compiler_params=_params(1),
    )(ms, q_a_g, kv_a_g, wuq3, wukv3, *tabs)


def _causal_mask(t):
    r = lax.broadcasted_iota(jnp.int32, (t, t), 0)
    c = lax.broadcasted_iota(jnp.int32, (t, t), 1)
    return r >= c


def _flash_fwd(q, k, v, mz):
    s = q.shape[1]
    t = min(TQ, s)

    def body(q_ref, k_ref, v_ref, mz_ref, o_ref, yb_ref, lse_ref, m_s, l_s, acc_s):
        i = pl.program_id(1)
        m_s[...] = jnp.full_like(m_s, -jnp.inf)
        l_s[...] = jnp.zeros_like(l_s)
        acc_s[...] = jnp.zeros_like(acc_s)

        def step(j, modes):
            rows = pl.ds(pl.multiple_of(j * t, t), t)
            for hh in range(FLASH_HEADS):
                for ch, masked in enumerate(modes):
                    if masked is None:
                        continue
                    r = slice(ch * t, (ch + 1) * t)
                    sc = _dot_nt(q_ref[hh, r, :], k_ref[hh, rows, :])
                    if masked:
                        sc = jnp.where(_causal_mask(t), sc, -jnp.inf)
                    m_prev = m_s[hh, r, :]
                    m_new = jnp.maximum(m_prev, jnp.max(sc, axis=-1, keepdims=True))
                    p = jnp.exp2(sc - jnp.tile(m_new, (1, t // 128)))
                    alpha = jnp.exp2(m_prev - m_new)
                    l_s[hh, r, :] = alpha * l_s[hh, r, :] + jnp.sum(p, axis=-1, keepdims=True)
                    acc_s[hh, r, :] = alpha * acc_s[hh, r, :] + _dot(p.astype(BF16), v_ref[hh, rows, :])
                    m_s[hh, r, :] = m_new

        def loop_body(j, carry):
            step(j, (False, False))
            return carry

        lax.fori_loop(0, 2 * i, loop_body, 0)
        step(2 * i, (True, False))
        step(2 * i + 1, (None, True))
        for hh in range(FLASH_HEADS):
            cols = slice(hh * HEAD_DIM, (hh + 1) * HEAD_DIM)
            out = acc_s[hh] / l_s[hh]
            o_ref[:, cols] = out
            silu_z, _ = _silu_parts(mz_ref[:, cols])
            yb_ref[:, cols] = (out * silu_z).astype(BF16)
            lse_ref[hh] = m_s[hh] + jnp.log2(l_s[hh])

    nh = FLASH_HEADS
    t2 = 2 * t
    col = pl.BlockSpec((t2, nh * HEAD_DIM), lambda h, i: (i, h))
    return pl.pallas_call(
        body, name="flash_fwd", grid=(HEADS // nh, s // t2),
        in_specs=[pl.BlockSpec((nh, t2, QK_PAD), lambda h, i: (h, i, 0)),
                  pl.BlockSpec((nh, s, QK_PAD), lambda h, i: (h, 0, 0)),
                  pl.BlockSpec((nh, s, HEAD_DIM), lambda h, i: (h, 0, 0)), col],
        out_specs=[col, col, pl.BlockSpec((nh, t2, 128), lambda h, i: (h, i, 0))],
        out_shape=[jax.ShapeDtypeStruct((s, D_MODEL), F32), jax.ShapeDtypeStruct((s, D_MODEL), BF16),
                   jax.ShapeDtypeStruct((HEADS, s, 128), F32)],
        scratch_shapes=[pltpu.VMEM((nh, t2, 128), F32), pltpu.VMEM((nh, t2, 128), F32),
                        pltpu.VMEM((nh, t2, HEAD_DIM), F32)],
        compiler_params=_params(2),
    )(q, k, v, mz)


def _flash_bwd(q, k, v, dyb, mz, o_att, lse, tabs):
    s = q.shape[1]
    t = min(TQ, s)

    def body(q_ref, k_ref, v_ref, dyb_ref, mz_ref, o_ref, lse_ref, c_ref, sa_ref, sb_ref,
             dq_ref, dk_ref, dv_ref, dmz_ref, dq_s, delta_s, do_s):
        i = pl.program_id(1)

        @pl.when(i == 0)
        def _():
            dk_ref[...] = jnp.zeros_like(dk_ref)
            dv_ref[...] = jnp.zeros_like(dv_ref)

        silu_z, dsilu_z = _silu_parts(mz_ref[...])
        dyb_v = dyb_ref[...]
        out = o_ref[...]
        do32 = dyb_v * silu_z
        dmz_ref[...] = (dyb_v * out * dsilu_z).astype(BF16)
        delta_s[...] = jnp.broadcast_to(jnp.sum(do32 * out, axis=-1, keepdims=True), (2 * t, 128))
        do_s[...] = do32.astype(BF16)
        dq_s[...] = jnp.zeros_like(dq_s)

        def step(j, modes):
            rows = pl.ds(pl.multiple_of(j * t, t), t)
            kj = k_ref[rows, :]
            vj = v_ref[rows, :]
            dv_acc = None
            dk_acc = None
            for ch, masked in enumerate(modes):
                if masked is None:
                    continue
                r = slice(ch * t, (ch + 1) * t)
                qv = q_ref[r, :]
                do = do_s[r, :]
                sc = _dot_nt(qv, kj)
                if masked:
                    sc = jnp.where(_causal_mask(t), sc, -jnp.inf)
                p = jnp.exp2(sc - jnp.tile(lse_ref[r, :], (1, t // 128)))
                dp = _dot_nt(do, vj)
                ds = (p * (dp - jnp.tile(delta_s[r, :], (1, t // 128)))).astype(BF16)
                dv_c = _dot_tn(p.astype(BF16), do)
                dk_c = _dot_tn(ds, qv)
                dv_acc = dv_c if dv_acc is None else dv_acc + dv_c
                dk_acc = dk_c if dk_acc is None else dk_acc + dk_c
                dq_s[r, :] += _dot(ds, kj)
            dv_ref[rows, :] += dv_acc
            dk_ref[rows, :] += dk_acc

        def loop_body(j, carry):
            step(j, (False, False))
            return carry

        lax.fori_loop(0, 2 * i, loop_body, 0)
        step(2 * i, (True, False))
        step(2 * i + 1, (None, True))
        dq = dq_s[...] * ATT_SCALE
        dq_ref[:, 0:128] = dq[:, 0:128].astype(BF16)
        dq_ref[:, 128:256] = _rope_bwd(dq[:, 128:256], c_ref[...], sa_ref[...], sb_ref[...]).astype(BF16)

    t2 = 2 * t
    col = pl.BlockSpec((t2, HEAD_DIM), lambda h, i: (i, h))
    tab = pl.BlockSpec((t2, 128), lambda h, i: (i, 0))
    return pl.pallas_call(
        body, name="flash_bwd", grid=(HEADS, s // t2),
        in_specs=[pl.BlockSpec((None, t2, QK_PAD), lambda h, i: (h, i, 0)),
                  pl.BlockSpec((None, s, QK_PAD), lambda h, i: (h, 0, 0)),
                  pl.BlockSpec((None, s, HEAD_DIM), lambda h, i: (h, 0, 0)),
                  col, col, col, pl.BlockSpec((None, t2, 128), lambda h, i: (h, i, 0)), tab, tab, tab],
        out_specs=[pl.BlockSpec((None, t2, QK_PAD), lambda h, i: (h, i, 0)),
                   pl.BlockSpec((None, s, QK_PAD), lambda h, i: (h, 0, 0)),
                   pl.BlockSpec((None, s, HEAD_DIM), lambda h, i: (h, 0, 0)), col],
        out_shape=[jax.ShapeDtypeStruct((HEADS, s, QK_PAD), BF16), jax.ShapeDtypeStruct((HEADS, s, QK_PAD), F32),
                   jax.ShapeDtypeStruct((HEADS, s, HEAD_DIM), F32), jax.ShapeDtypeStruct((s, D_MODEL), BF16)],
        scratch_shapes=[pltpu.VMEM((t2, QK_PAD), F32), pltpu.VMEM((t2, 128), F32), pltpu.VMEM((t2, HEAD_DIM), BF16)],
        compiler_params=_params(2),
    )(q, k, v, dyb, mz, o_att, lse, *tabs)


def _mla_bwd_proj(dq, dk, dv, cqn, ckvn, ms, q_a_g, kv_a_g, wuq3, wukv3, tabs):
    s = ms.shape[0]
    tm = min(TM_FUSED, s)

    def body(dq_ref, dk_ref, dv_ref, cqn_ref, ckvn_ref, ms_ref, qg_ref, kvg_ref, wuq_ref, wukv_ref,
             c_ref, sa_ref, sb_ref, dms_ref, dwuq_ref, dwukv_ref, dqg_ref, dkvg_ref):
        @pl.when(pl.program_id(0) == 0)
        def _():
            dwuq_ref[...] = jnp.zeros_like(dwuq_ref)
            dwukv_ref[...] = jnp.zeros_like(dwukv_ref)
            dqg_ref[...] = jnp.zeros_like(dqg_ref)
            dkvg_ref[...] = jnp.zeros_like(dkvg_ref)

        cqn = cqn_ref[...]
        ckvn = ckvn_ref[...]
        dcqn = jnp.zeros((tm, Q_LORA), F32)
        dckvn = jnp.zeros((tm, KV_LORA), F32)
        dkpe = jnp.zeros((tm, 128), F32)
        for h in range(HEADS):
            dqh = dq_ref[h]
            dcqn += _dot_nt(dqh, wuq_ref[h])
            dwuq_ref[h] += _dot_tn(cqn, dqh)
            dkh = dk_ref[h] * LN2
            dkvh = jnp.concatenate([dkh[:, 0:128], dv_ref[h]], axis=1).astype(BF16)
            dckvn += _dot_nt(dkvh, wukv_ref[h])
            dwukv_ref[h] += _dot_tn(ckvn, dkvh)
            dkpe += dkh[:, 128:256]
        dcq, dqg_rows = _rms_bwd(ms_ref[:, 0:Q_LORA], qg_ref[...], dcqn)
        dckv, dkvg_rows = _rms_bwd(ms_ref[:, Q_LORA:Q_LORA + KV_LORA], kvg_ref[...], dckvn)
        dqg_ref[...] += jnp.sum(dqg_rows, axis=0, keepdims=True)
        dkvg_ref[...] += jnp.sum(dkvg_rows, axis=0, keepdims=True)
        dms_ref[:, 0:Q_LORA] = dcq.astype(BF16)
        dms_ref[:, Q_LORA:Q_LORA + KV_LORA] = dckv.astype(BF16)
        dms_ref[:, Q_LORA + KV_LORA:MS_COLS] = _rope_bwd(dkpe, c_ref[...], sa_ref[...], sb_ref[...]).astype(BF16)

    tab = pl.BlockSpec((tm, 128), lambda i: (i, 0))
    wq = pl.BlockSpec((HEADS, Q_LORA, QK_PAD), lambda i: (0, 0, 0))
    wkv = pl.BlockSpec((HEADS, KV_LORA, 256), lambda i: (0, 0, 0))
    qg = pl.BlockSpec((1, Q_LORA), lambda i: (0, 0))
    kvg = pl.BlockSpec((1, KV_LORA), lambda i: (0, 0))
    return pl.pallas_call(
        body, name="mla_bwd_proj", grid=(s // tm,),
        in_specs=[pl.BlockSpec((HEADS, tm, QK_PAD), lambda i: (0, i, 0)),
                  pl.BlockSpec((HEADS, tm, QK_PAD), lambda i: (0, i, 0)),
                  pl.BlockSpec((HEADS, tm, HEAD_DIM), lambda i: (0, i, 0)),
                  pl.BlockSpec((tm, Q_LORA), lambda i: (i, 0)), pl.BlockSpec((tm, KV_LORA), lambda i: (i, 0)),
                  pl.BlockSpec((tm, MS_COLS), lambda i: (i, 0)), qg, kvg, wq, wkv, tab, tab, tab],
        out_specs=[pl.BlockSpec((tm, MS_COLS), lambda i: (i, 0)), wq, wkv, qg, kvg],
        out_shape=[jax.ShapeDtypeStruct((s, MS_COLS), BF16), jax.ShapeDtypeStruct((HEADS, Q_LORA, QK_PAD), F32),
                   jax.ShapeDtypeStruct((HEADS, KV_LORA, 256), F32),
                   jax.ShapeDtypeStruct((1, Q_LORA), F32), jax.ShapeDtypeStruct((1, KV_LORA), F32)],
        compiler_params=_params(1),
    )(dq, dk, dv, cqn, ckvn, ms, q_a_g, kv_a_g, wuq3, wukv3, *tabs)


def _merge_loss(ya, yb, glog, b_gate, x, tgt, fg, wpa, wpb, wout):
    s = x.shape[0]
    tm = min(TM_FUSED, s)

    def body(ya_ref, yb_ref, g0_ref, g1_ref, b0_ref, b1_ref, x_ref, t_ref, fg_ref, wpa_ref, wpb_ref, wout_ref,
             mg_ref, pa_ref, pb_ref, dx2_ref, loss_ref, dfg_ref):
        @pl.when(pl.program_id(0) == 0)
        def _():
            loss_ref[...] = jnp.zeros_like(loss_ref)
            dfg_ref[...] = jnp.zeros_like(dfg_ref)

        pa = _dot(ya_ref[...], wpa_ref[...])
        pb = _dot(yb_ref[...], wpb_ref[...])
        pa_ref[...] = pa
        pb_ref[...] = pb
        merged = (jax.nn.sigmoid(g0_ref[...] + b0_ref[...]) * pa
                  + jax.nn.sigmoid(g1_ref[...] + b1_ref[...]) * pb).astype(BF16)
        mg_ref[...] = merged
        x2 = x_ref[...] + _dot(merged, wout_ref[...])
        fg_v = fg_ref[...]
        err = _rms(x2, fg_v) - t_ref[...]
        loss_ref[...] += 0.5 * jnp.sum(jnp.mean(err * err, axis=-1, keepdims=True), axis=0, keepdims=True)
        dx2, dfg_rows = _rms_bwd(x2, fg_v, err * (1.0 / D_MODEL))
        dx2_ref[...] = dx2
        dfg_ref[...] += jnp.sum(dfg_rows, axis=0, keepdims=True)

    row = pl.BlockSpec((tm, D_MODEL), lambda i: (i, 0))
    row1 = pl.BlockSpec((tm, D_MODEL), lambda i: (i, 1))
    vec = pl.BlockSpec((1, D_MODEL), lambda i: (0, 0))
    vec1 = pl.BlockSpec((1, D_MODEL), lambda i: (0, 1))
    wsp = pl.BlockSpec((D_MODEL, D_MODEL), lambda i: (0, 0))
    return pl.pallas_call(
        body, name="merge_loss", grid=(s // tm,),
        in_specs=[row, row, row, row1, vec, vec1, row, row, vec, wsp, wsp, wsp],
        out_specs=[row, row, row, row, pl.BlockSpec((1, 128), lambda i: (0, 0)), vec],
        out_shape=[jax.ShapeDtypeStruct((s, D_MODEL), BF16), jax.ShapeDtypeStruct((s, D_MODEL), F32),
                   jax.ShapeDtypeStruct((s, D_MODEL), F32), jax.ShapeDtypeStruct((s, D_MODEL), F32),
                   jax.ShapeDtypeStruct((1, 128), F32), jax.ShapeDtypeStruct((1, D_MODEL), F32)],
        compiler_params=_params(1),
    )(ya, yb, glog, glog, b_gate, b_gate, x, tgt, fg, wpa, wpb, wout)


def _merge_bwd(dx2, pa, pb, glog, b_gate, wpa, wpb, wout):
    s = dx2.shape[0]
    tm = min(TM_FUSED, s)

    def body(dx2_ref, pa_ref, pb_ref, g0_ref, g1_ref, b0_ref, b1_ref, wpa_ref, wpb_ref, wout_ref,
             dya_ref, dyb_ref, dgl_ref, dpa_ref, dpb_ref, dx2b_ref, dbg_ref):
        @pl.when(pl.program_id(0) == 0)
        def _():
            dbg_ref[...] = jnp.zeros_like(dbg_ref)

        dx2b = dx2_ref[...].astype(BF16)
        dx2b_ref[...] = dx2b
        dmg = _dot_nt(dx2b, wout_ref[...])
        g0 = jax.nn.sigmoid(g0_ref[...] + b0_ref[...])
        g1 = jax.nn.sigmoid(g1_ref[...] + b1_ref[...])
        dpa = (dmg * g0).astype(BF16)
        dpb = (dmg * g1).astype(BF16)
        dpa_ref[...] = dpa
        dpb_ref[...] = dpb
        dgl0 = dmg * pa_ref[...] * g0 * (1.0 - g0)
        dgl1 = dmg * pb_ref[...] * g1 * (1.0 - g1)
        dgl_ref[:, 0:D_MODEL] = dgl0.astype(BF16)
        dgl_ref[:, D_MODEL:2 * D_MODEL] = dgl1.astype(BF16)
        dbg_ref[:, 0:D_MODEL] += jnp.sum(dgl0, axis=0, keepdims=True)
        dbg_ref[:, D_MODEL:2 * D_MODEL] += jnp.sum(dgl1, axis=0, keepdims=True)
        dya_ref[...] = _dot_nt(dpa, wpa_ref[...])
        dyb_ref[...] = _dot_nt(dpb, wpb_ref[...])

    row = pl.BlockSpec((tm, D_MODEL), lambda i: (i, 0))
    row1 = pl.BlockSpec((tm, D_MODEL), lambda i: (i, 1))
    row2 = pl.BlockSpec((tm, 2 * D_MODEL), lambda i: (i, 0))
    vec = pl.BlockSpec((1, D_MODEL), lambda i: (0, 0))
    vec1 = pl.BlockSpec((1, D_MODEL), lambda i: (0, 1))
    vec2 = pl.BlockSpec((1, 2 * D_MODEL), lambda i: (0, 0))
    wsp = pl.BlockSpec((D_MODEL, D_MODEL), lambda i: (0, 0))
    return pl.pallas_call(
        body, name="merge_bwd", grid=(s // tm,),
        in_specs=[row, row, row, row, row1, vec, vec1, wsp, wsp, wsp],
        out_specs=[row, row, row2, row, row, row, vec2],
        out_shape=[jax.ShapeDtypeStruct((s, D_MODEL), F32), jax.ShapeDtypeStruct((s, D_MODEL), F32),
                   jax.ShapeDtypeStruct((s, 2 * D_MODEL), BF16), jax.ShapeDtypeStruct((s, D_MODEL), BF16),
                   jax.ShapeDtypeStruct((s, D_MODEL), BF16), jax.ShapeDtypeStruct((s, D_MODEL), BF16),
                   jax.ShapeDtypeStruct((1, 2 * D_MODEL), F32)],
        compiler_params=_params(1),
    )(dx2, pa, pb, glog, glog, b_gate, b_gate, wpa, wpb, wout)


def _local_step(x, tgt, w_int, w_uq, w_ukv, wproj, norm_g, b_gate, lb_logits, hg_norm_g, q_a_g, kv_a_g, fg):
    s = x.shape[0]
    w_ms = jnp.concatenate([w_int[4096:4800], jnp.zeros((64, D_MODEL), BF16)], axis=0)
    w_mz = w_int[4800:5824]
    w_gl = w_int[5824:7872]
    wuq3 = jnp.pad(w_uq.reshape(Q_LORA, HEADS, QK_DIM).transpose(1, 0, 2), ((0, 0), (0, 0), (0, QK_PAD - QK_DIM)))
    wukv3 = w_ukv.reshape(KV_LORA, HEADS, 256).transpose(1, 0, 2)
    wpa, wpb, wout = wproj[0], wproj[1], wproj[2]
    tabs = _rope_tables(s)

    h = _norm_in(x, norm_g)
    hg = _mm_slabs_out(h, w_int, 4, name="proj_hg")
    ms = _mm(h, w_ms, trans_b=True, name="proj_ms")
    mz = _mm(h, w_mz, trans_b=True, name="proj_mz")
    glog = _mm(h, w_gl, trans_b=True, name="proj_gate")
    o_pre, ya, st0 = _hgrn_fwd(hg, lb_logits, hg_norm_g)
    q, k, v, cqn, ckvn = _mla_pre(ms, q_a_g, kv_a_g, wuq3, wukv3, tabs)
    o_att, yb, lse = _flash_fwd(q, k, v, mz)
    merged, pa, pb, dx2, loss, dfg = _merge_loss(ya, yb, glog, b_gate, x, tgt, fg, wpa, wpb, wout)

    dya, dyb, dglog, dpa, dpb, dx2b, dbg = _merge_bwd(dx2, pa, pb, glog, b_gate, wpa, wpb, wout)
    d_wout = _mm_tn(merged, dx2b, name="dw_out")
    d_wpa = _mm_tn(ya, dpa, name="dw_proj_a")
    d_wpb = _mm_tn(yb, dpb, name="dw_proj_b")
    dhg, dlb, dhgg = _hgrn_bwd(hg, o_pre, dya, st0, lb_logits, hg_norm_g)
    dq, dk, dv, dmz = _flash_bwd(q, k, v, dyb, mz, o_att, lse, tabs)
    dms, d_wuq3, d_wukv3, dqg, dkvg = _mla_bwd_proj(dq, dk, dv, cqn, ckvn, ms, q_a_g, kv_a_g, wuq3, wukv3, tabs)
    d_hg = _mm_tn(dhg, h, name="dw_in_hg")
    d_ms = _mm_tn(dms, h, name="dw_in_ms")
    d_mz = _mm_tn(dmz, h, name="dw_in_mz")
    d_gl = _mm_tn(dglog, h, name="dw_in_gate")
    dh = _mm_slabs_in(dhg, w_int, name="dh_hg")
    dh = _mm(dms, w_ms, add=dh, name="dh_ms")
    dh = _mm(dmz, w_mz, add=dh, name="dh_mz")
    dh = _mm(dglog, w_gl, add=dh, name="dh_gate")
    grad_x, dng = _norm_in_bwd(x, norm_g, dh, dx2)

    d_w_int = jnp.concatenate([d_hg.reshape(4 * D_MODEL, D_MODEL), d_ms[0:704], d_mz, d_gl], axis=0)
    small = {"norm_g": dng, "b_gate": dbg, "lb": dlb, "hg_norm_g": dhgg, "q_a_g": dqg, "kv_a_g": dkvg,
             "final_norm_g": dfg}
    return loss, grad_x, d_w_int, d_wuq3, d_wukv3, (d_wpa, d_wpb, d_wout), small


def _pack_rest(w_uq_b, w_ukv_b, wpa_b, wpb_b, wout_b):
    return jnp.concatenate([w_uq_b.reshape(144, D_MODEL), w_ukv_b.reshape(128, D_MODEL), wpa_b, wpb_b, wout_b], axis=0)


def _unpack_rest(p):
    return (p[0:144].reshape(Q_LORA, 384), p[144:272].reshape(KV_LORA, 512), p[272:528], p[528:784], p[784:1040])


def _pack_rest_grads(d_wuq3, d_wukv3, d_proj):
    d_wuq = d_wuq3.transpose(1, 0, 2)[:, :, 0:QK_DIM].reshape(Q_LORA, HEADS * QK_DIM)
    d_wukv = d_wukv3.transpose(1, 0, 2).reshape(KV_LORA, HEADS * 256)
    blocks = []
    for b in range(N_CHIPS):
        rows = slice(b * 256, (b + 1) * 256)
        blocks.append(_pack_rest(d_wuq[:, b * 384:(b + 1) * 384], d_wukv[:, b * 512:(b + 1) * 512],
                                 d_proj[0][rows], d_proj[1][rows], d_proj[2][rows]))
    return jnp.stack(blocks, axis=0)


def _unpack_rest_weights(g):
    parts = [_unpack_rest(g[b]) for b in range(N_CHIPS)]
    w_uq, w_ukv = (jnp.concatenate([p[n] for p in parts], axis=1) for n in range(2))
    wproj = jnp.stack([jnp.concatenate([p[n] for p in parts], axis=0) for n in range(2, 5)], axis=0)
    return w_uq, w_ukv, wproj


MESH_ID = pl.DeviceIdType.MESH
ANY = pl.BlockSpec(memory_space=pl.ANY)
HALF_COLS = D_MODEL // 2


def _me():
    return lax.axis_index("x"), lax.axis_index("y"), lax.axis_index("c")


def _other_chips(x, y):
    return [(1 - x, y), (x, 1 - y), (1 - x, 1 - y)]


def _cols(c):
    return pl.ds(c * HALF_COLS, HALF_COLS)


def _gather_weights(w_blk, r_blk):
    def body(w_ref, r_ref, ow_ref, or_ref, send_sems, recv_sems):
        x, y, c = _me()
        chips = _other_chips(x, y)
        me = 2 * x + y
        pairs = [(w_ref, ow_ref), (r_ref, or_ref)]

        def copy(k, src, dst, to):
            return pltpu.make_async_remote_copy(src_ref=src, dst_ref=dst, send_sem=send_sems.at[k],
                                                recv_sem=recv_sems.at[k], device_id=to, device_id_type=MESH_ID)

        first =[copy(6 * a + j, src.at[:, _cols(c)], dst.at[me, :, _cols(c)], (cx, cy, c))
                 for a, (src, dst) in enumerate(pairs) for j, (cx, cy) in enumerate(chips)]
        for cp in first:
            cp.start()
        passed = []
        for a, (src, dst) in enumerate(pairs):
            for j, (cx, cy) in enumerate(chips):
                landed = dst.at[2 * cx + cy, :, _cols(c)]
                copy(6 * a + j, landed, landed, (cx, cy, c)).wait_recv()
                fwd = copy(6 * a + 3 + j, landed, landed, (x, y, 1 - c))
                fwd.start()
                passed.append(fwd)
        for a, (src, dst) in enumerate(pairs):
            for j, (cx, cy) in enumerate(chips):
                theirs = dst.at[2 * cx + cy, :, _cols(1 - c)]
                copy(6 * a + 3 + j, theirs, theirs, (x, y, 1 - c)).wait_recv()
        for cp in first + passed:
            cp.wait_send()

    gw, gr = pl.pallas_call(
        body, name="gather_weights", in_specs=[ANY, ANY], out_specs=[ANY, ANY],
        out_shape=[jax.ShapeDtypeStruct((N_CHIPS,) + w_blk.shape, w_blk.dtype),
                   jax.ShapeDtypeStruct((N_CHIPS,) + r_blk.shape, r_blk.dtype)],
        scratch_shapes=[pltpu.SemaphoreType.DMA((12,)), pltpu.SemaphoreType.DMA((12,))],
    )(w_blk, r_blk)
    chip = 2 * lax.axis_index("x") + lax.axis_index("y")
    return (lax.dynamic_update_slice(gw, w_blk[None], (chip, 0, 0)),
            lax.dynamic_update_slice(gr, r_blk[None], (chip, 0, 0)))


def _swap_halves(gw, gr):
    def body(gw_ref, gr_ref, lw_ref, lr_ref, send_sems, recv_sems):
        x, y, c = _me()
        cps = [pltpu.make_async_remote_copy(
            src_ref=src, dst_ref=dst, send_sem=send_sems.at[a], recv_sem=recv_sems.at[a],
            device_id=(x, y, 1 - c), device_id_type=MESH_ID)
            for a, (src, dst) in enumerate([(gw_ref.at[:, _cols(1 - c)], lw_ref),
                                            (gr_ref.at[:, :, _cols(1 - c)], lr_ref)])]
        for cp in cps:
            cp.start()
        for cp in cps:
            cp.wait()

    return pl.pallas_call(
        body, name="grad_swap_halves", in_specs=[ANY, ANY], out_specs=[ANY, ANY],
        out_shape=[jax.ShapeDtypeStruct((gw.shape[0], HALF_COLS), gw.dtype),
                   jax.ShapeDtypeStruct(gr.shape[:2] + (HALF_COLS,), gr.dtype)],
        scratch_shapes=[pltpu.SemaphoreType.DMA((2,)), pltpu.SemaphoreType.DMA((2,))],
    )(gw, gr)


def _scatter_blocks(hw, hr):
    nw, nr = hw.shape[0] // N_CHIPS, hr.shape[0] // N_CHIPS

    def body(hw_ref, hr_ref, lw_ref, lr_ref, send_sems, recv_sems):
        x, y, c = _me()
        cps = []
        for a, (src, dst, n) in enumerate([(hw_ref, lw_ref, nw), (hr_ref, lr_ref, nr)]):
            for j, (cx, cy) in enumerate(_other_chips(x, y)):
                cps.append(pltpu.make_async_remote_copy(
                    src_ref=src.at[pl.ds((2 * cx + cy) * n, n), :], dst_ref=dst.at[j], send_sem=send_sems.at[3 * a + j],
                    recv_sem=recv_sems.at[3 * a + j], device_id=(cx, cy, c), device_id_type=MESH_ID))
        for cp in cps:
            cp.start()
        for cp in cps:
            cp.wait()

    return pl.pallas_call(
        body, name="grad_scatter_blocks", in_specs=[ANY, ANY], out_specs=[ANY, ANY],
        out_shape=[jax.ShapeDtypeStruct((3, nw, HALF_COLS), hw.dtype), jax.ShapeDtypeStruct((3, nr, HALF_COLS), hr.dtype)],
        scratch_shapes=[pltpu.SemaphoreType.DMA((6,)), pltpu.SemaphoreType.DMA((6,))],
    )(hw, hr)


def _swap_reduced(rw, rr):
    def body(rw_ref, rr_ref, ow_ref, or_ref, send_sems, recv_sems):
        x, y, c = _me()
        cps = [pltpu.make_async_remote_copy(
            src_ref=src, dst_ref=dst, send_sem=send_sems.at[a], recv_sem=recv_sems.at[a],
            device_id=(x, y, 1 - c), device_id_type=MESH_ID)
            for a, (src, dst) in enumerate([(rw_ref, ow_ref), (rr_ref, or_ref)])]
        for cp in cps:
            cp.start()
        for cp in cps:
            cp.wait()

    return pl.pallas_call(
        body, name="grad_swap_reduced", in_specs=[ANY, ANY], out_specs=[ANY, ANY],
        out_shape=[jax.ShapeDtypeStruct(rw.shape, rw.dtype), jax.ShapeDtypeStruct(rr.shape, rr.dtype)],
        scratch_shapes=[pltpu.SemaphoreType.DMA((2,)), pltpu.SemaphoreType.DMA((2,))],
    )(rw, rr)


def _join_cols(mine, theirs):
    first = lax.axis_index("c") == 0
    return jnp.concatenate([jnp.where(first, mine, theirs), jnp.where(first, theirs, mine)], axis=1)


def _gather_small(vec):
    def body(v_ref, out_ref, send_sems, recv_sems, local_sem):
        x, y, c = _me()
        my_id = 4 * x + 2 * y + c
        mine = pltpu.make_async_copy(v_ref, out_ref.at[my_id], local_sem)
        mine.start()
        cps = []
        for r in range(1, N_DEV):
            peer = (x ^ (r >> 2), y ^ ((r >> 1) & 1), c ^ (r & 1))
            cps.append(pltpu.make_async_remote_copy(
                src_ref=v_ref, dst_ref=out_ref.at[my_id], send_sem=send_sems.at[r - 1],
                recv_sem=recv_sems.at[r - 1], device_id=peer, device_id_type=MESH_ID))
        for cp in cps:
            cp.start()
        for cp in cps:
            cp.wait()
        mine.wait()

    return pl.pallas_call(
        body, name="gather_small", in_specs=[ANY], out_specs=ANY,
        out_shape=jax.ShapeDtypeStruct((N_DEV, 1, SMALL_COLS), vec.dtype),
        scratch_shapes=[pltpu.SemaphoreType.DMA((N_DEV - 1,)), pltpu.SemaphoreType.DMA((N_DEV - 1,)),
                        pltpu.SemaphoreType.DMA],
    )(vec)


def _add_cores(c_idx, g, landed, *, tm, name):
    r = g.shape[0]

    def body(c_ref, g_ref, l_ref, o32_ref, o16_ref):
        acc = g_ref[...] + l_ref[...]
        o32_ref[...] = acc
        o16_ref[...] = acc.astype(BF16)

    half = pl.BlockSpec((tm, HALF_COLS), lambda i, c_ref: (i, 0))
    grid_spec = pltpu.PrefetchScalarGridSpec(
        num_scalar_prefetch=1, grid=(r // tm,),
        in_specs=[pl.BlockSpec((tm, HALF_COLS), lambda i, c_ref: (i, c_ref[0])), half], out_specs=[half, half])
    return pl.pallas_call(
        body, name=name, grid_spec=grid_spec,
        out_shape=[jax.ShapeDtypeStruct((r, HALF_COLS), F32), jax.ShapeDtypeStruct((r, HALF_COLS), BF16)],
        compiler_params=_params(1),
    )(c_idx, g, landed)


def _add_chips(chip_idx, h32, landed, *, tm, name):
    n = landed.shape[1]
    per = n // tm

    def body(chip_ref, h_ref, l_ref, o_ref):
        acc = h_ref[...]
        for j in range(3):
            acc = acc + l_ref[j].astype(F32)
        o_ref[...] = acc

    grid_spec = pltpu.PrefetchScalarGridSpec(
        num_scalar_prefetch=1, grid=(per,),
        in_specs=[pl.BlockSpec((tm, HALF_COLS), lambda i, chip_ref: (chip_ref[0] * per + i, 0)),
                  pl.BlockSpec((3, tm, HALF_COLS), lambda i, chip_ref: (0, i, 0))],
        out_specs=pl.BlockSpec((tm, HALF_COLS), lambda i, chip_ref: (i, 0)))
    return pl.pallas_call(
        body, name=name, grid_spec=grid_spec, out_shape=jax.ShapeDtypeStruct((n, HALF_COLS), F32),
        compiler_params=_params(1),
    )(chip_idx, h32, landed)


def _pack_small(small, lb_logits, loss):
    def body(ng_ref, bg_ref, dlb_ref, lbl_ref, hgg_ref, qg_ref, kvg_ref, fg_ref, loss_ref, out_ref):
        out_ref[...] = jnp.zeros_like(out_ref)
        out_ref[:, 0:1024] = ng_ref[...]
        out_ref[:, 1024:3072] = bg_ref[...]
        _, p0p1 = _lower_bound(lbl_ref[...])
        dl0 = dlb_ref[...] * p0p1
        out_ref[:, 3072:4096] = dl0
        out_ref[:, 4096:5120] = -dl0
        hgg = hgg_ref[0]
        for h in range(1, HEADS):
            hgg = hgg + hgg_ref[h]
        out_ref[:, 5120:5248] = hgg
        out_ref[:, 5248:5632] = qg_ref[...]
        out_ref[:, 5632:5888] = kvg_ref[...]
        out_ref[:, 5888:6912] = fg_ref[...]
        out_ref[:, 6912:7040] = loss_ref[...]

    return pl.pallas_call(
        body, name="pack_small", out_shape=jax.ShapeDtypeStruct((1, SMALL_COLS), F32),
    )(small["norm_g"], small["b_gate"], small["lb"], lb_logits, small["hg_norm_g"], small["q_a_g"],
      small["kv_a_g"], small["final_norm_g"], loss)


def _sum_small(gathered):
    def body(g_ref, out_ref):
        acc = g_ref[0]
        for d in range(1, N_DEV):
            acc = acc + g_ref[d]
        out_ref[...] = acc

    return pl.pallas_call(
        body, name="sum_small", out_shape=jax.ShapeDtypeStruct((1, SMALL_COLS), F32),
    )(gathered)


def _adamw(w, g, m, v, *, name, tm):
    r, cols = w.shape
    c1 = 1.0 - ADAM_B1 ** ADAM_STEP
    c2 = 1.0 - ADAM_B2 ** ADAM_STEP

    def body(w_ref, g_ref, m_ref, v_ref, d_ref, nm_ref, nv_ref):
        gv = g_ref[...]
        nm = ADAM_B1 * m_ref[...] + (1.0 - ADAM_B1) * gv
        nv = ADAM_B2 * v_ref[...] + (1.0 - ADAM_B2) * (gv * gv)
        nm_ref[...] = nm
        nv_ref[...] = nv
        d_ref[...] = -ADAM_LR * ((nm / c1) / (jnp.sqrt(nv / c2) + ADAM_EPS) + ADAM_WD * w_ref[...])

    row = pl.BlockSpec((tm, cols), lambda i: (i, 0))
    shp = jax.ShapeDtypeStruct((r, cols), F32)
    return pl.pallas_call(
        body, name=name, grid=(r // tm,), in_specs=[row] * 4, out_specs=[row] * 3, out_shape=[shp] * 3,
        compiler_params=_params(1),
    )(w, g, m, v)


def _small_vec(norm_g, b_gate, lb_logits, hg_norm_g, q_a_g, kv_a_g, fg):
    parts = [norm_g.reshape(1, -1), b_gate.reshape(1, -1), lb_logits.reshape(1, -1), hg_norm_g.reshape(1, -1),
             q_a_g.reshape(1, -1), kv_a_g.reshape(1, -1), fg.reshape(1, -1), jnp.zeros((1, SMALL_COLS - 6912), F32)]
    return jnp.concatenate(parts, axis=1)


def _split_small(vec):
    v = vec.reshape(-1)
    return (v[0:1024].reshape(1, 1024), v[1024:3072].reshape(1, 2048), v[3072:5120].reshape(2, 1024),
            v[5120:5248].reshape(1, 128), v[5248:5632].reshape(1, 384), v[5632:5888].reshape(1, 256), v[5888:6912])


def kernel(x, norm_g, w_in, b_gate, lb_logits, hg_norm_g, q_a_g, w_uq, kv_a_g, w_ukv, w_proj_a, w_proj_b, w_out, final_norm_g, loss_target, m_norm_g, m_w_in, m_b_gate, m_lb_logits, m_hg_norm_g, m_q_a_g, m_w_uq, m_kv_a_g, m_w_ukv, m_w_proj_a, m_w_proj_b, m_w_out, m_final_norm_g, v_norm_g, v_w_in, v_b_gate, v_lb_logits, v_hg_norm_g, v_q_a_g, v_w_uq, v_kv_a_g, v_w_ukv, v_w_proj_a, v_w_proj_b, v_w_out, v_final_norm_g):
    c_idx = lax.axis_index("c").astype(jnp.int32).reshape(1)
    chip_idx = (2 * lax.axis_index("x") + lax.axis_index("y")).astype(jnp.int32).reshape(1)

    w_blk = w_in[0].T.astype(BF16)
    r_blk = _pack_rest(w_uq[0], w_ukv[0], w_proj_a[0], w_proj_b[0], w_out[0]).astype(BF16)
    gw, gr = _gather_weights(w_blk, r_blk)
    fw_uq, fw_ukv, fwproj = _unpack_rest_weights(gr)

    loss, grad_x, d_w_int, d_wuq3, d_wukv3, d_proj, small = _local_step(
        x[0], loss_target[0], gw.reshape(W_IN_COLS, D_MODEL), fw_uq, fw_ukv, fwproj,
        norm_g, b_gate, lb_logits, hg_norm_g, q_a_g, kv_a_g, final_norm_g.reshape(1, D_MODEL))

    d_rest = _pack_rest_grads(d_wuq3, d_wukv3, d_proj)
    lw, lr = _swap_halves(d_w_int, d_rest)
    hw32, hw16 = _add_cores(c_idx, d_w_int, lw, tm=656, name="grad_add_cores_w")
    hr32, hr16 = _add_cores(c_idx, d_rest.reshape(N_CHIPS * REST_ROWS, D_MODEL), lr.reshape(N_CHIPS * REST_ROWS, HALF_COLS),
                            tm=REST_ROWS, name="grad_add_cores_r")
    landed_w, landed_r = _scatter_blocks(hw16, hr16)
    rw = _add_chips(chip_idx, hw32, landed_w, tm=656, name="grad_add_chips_w")
    rr = _add_chips(chip_idx, hr32, landed_r, tm=208, name="grad_add_chips_r")
    tw, tr = _swap_reduced(rw, rr)
    g_w_in = _join_cols(rw, tw).T
    g_rest = _join_cols(rr, tr)
    g_uq, g_ukv, g_pa, g_pb, g_out = _unpack_rest(g_rest)

    small_sum = _sum_small(_gather_small(_pack_small(small, lb_logits, loss)))

    upd = {
        "w_in": _adamw(w_in[0], g_w_in, m_w_in[0], v_w_in[0], name="adamw_w_in", tm=128),
        "w_uq": _adamw(w_uq[0], g_uq, m_w_uq[0], v_w_uq[0], name="adamw_w_uq", tm=Q_LORA),
        "w_ukv": _adamw(w_ukv[0], g_ukv, m_w_ukv[0], v_w_ukv[0], name="adamw_w_ukv", tm=KV_LORA),
        "w_proj_a": _adamw(w_proj_a[0], g_pa, m_w_proj_a[0], v_w_proj_a[0], name="adamw_w_proj_a", tm=256),
        "w_proj_b": _adamw(w_proj_b[0], g_pb, m_w_proj_b[0], v_w_proj_b[0], name="adamw_w_proj_b", tm=256),
        "w_out": _adamw(w_out[0], g_out, m_w_out[0], v_w_out[0], name="adamw_w_out", tm=256),
    }
    ws = _small_vec(norm_g, b_gate, lb_logits, hg_norm_g, q_a_g, kv_a_g, final_norm_g).reshape(7, 1024)
    ms_ = _small_vec(m_norm_g, m_b_gate, m_lb_logits, m_hg_norm_g, m_q_a_g, m_kv_a_g, m_final_norm_g).reshape(7, 1024)
    vs = _small_vec(v_norm_g, v_b_gate, v_lb_logits, v_hg_norm_g, v_q_a_g, v_kv_a_g, v_final_norm_g).reshape(7, 1024)
    upd_small = _adamw(ws, small_sum.reshape(7, 1024), ms_, vs, name="adamw_small", tm=7)

    def outputs(big, vec):
        s_ng, s_bg, s_lb, s_hg, s_qg, s_kvg, s_fg = _split_small(vec)
        return (s_ng, big["w_in"][None], s_bg, s_lb, s_hg, s_qg, big["w_uq"][None], s_kvg, big["w_ukv"][None],
                big["w_proj_a"][None], big["w_proj_b"][None], big["w_out"][None], s_fg)

    grads = {"w_in": g_w_in, "w_uq": g_uq, "w_ukv": g_ukv, "w_proj_a": g_pa, "w_proj_b": g_pb, "w_out": g_out}
    total_loss = small_sum[0, 6912]
    return (total_loss, grad_x[None], *outputs(grads, small_sum),
            *(o for k in range(3) for o in outputs({n: u[k] for n, u in upd.items()}, upd_small[k])))
```

```python
import functools

import jax
import jax.numpy as jnp
from jax import lax
from jax.experimental import pallas as pl
from jax.experimental.pallas import tpu as pltpu

F32 = jnp.float32
BF16 = jnp.bfloat16

D_MODEL = 1024
HEADS = 8
HEAD_DIM = 128
HG_CHUNK = 32
CHUNK_SHIFT = 5
HEAD_SHIFT = 7
QK_NOPE = 128
QK_ROPE = 64
QK_DIM = QK_NOPE + QK_ROPE
QK_PAD = 256
Q_LORA = 384
KV_LORA = 256
MS_COLS = 768
ROPE_THETA = 10000.0
EPS = 1e-6
ATT_SCALE = QK_DIM ** -0.5
LOG2E = 1.4426950408889634
LN2 = 0.6931471805599453
Q_PRESCALE = ATT_SCALE * LOG2E

ADAM_LR = 0.001
ADAM_B1 = 0.9
ADAM_B2 = 0.999
ADAM_EPS = 1e-08
ADAM_WD = 0.01
ADAM_STEP = 10

N_CHIPS = 4
N_DEV = 8
W_IN_COLS = 7872
W_IN_BLK = W_IN_COLS // N_CHIPS
REST_ROWS = 144 + 128 + 3 * 256
SMALL_COLS = 7168

TM_MM = 1024
TM_FUSED = 256
HG_ROWS = 256
TQ = 512
FLASH_HEADS = 2
HG_HEADS = 2
VMEM_LIMIT = 56 * 1024 * 1024


def _dot(a, b):
    return lax.dot_general(a, b, (((1,), (0,)), ((), ())), preferred_element_type=F32)


def _dot_nt(a, b):
    return lax.dot_general(a, b, (((1,), (1,)), ((), ())), preferred_element_type=F32)


def _dot_tn(a, b):
    return lax.dot_general(a, b, (((0,), (0,)), ((), ())), preferred_element_type=F32)


def _params(n_axes):
    return pltpu.CompilerParams(dimension_semantics=("arbitrary",) * n_axes, vmem_limit_bytes=VMEM_LIMIT)


def _rms(x, g):
    r = lax.rsqrt(jnp.mean(x * x, axis=-1, keepdims=True) + EPS)
    return x * r * g


def _rms_bwd(x, g, dy):
    r = lax.rsqrt(jnp.mean(x * x, axis=-1, keepdims=True) + EPS)
    xh = x * r
    dyg = dy * g
    dx = r * (dyg - xh * jnp.mean(dyg * xh, axis=-1, keepdims=True))
    return dx, dy * xh


def _silu_parts(z):
    s = jax.nn.sigmoid(z)
    return z * s, s * (1.0 + z * (1.0 - s))


def _rope(x, c, sa, sb):
    return x * c + pltpu.roll(x, 32, 1) * sa + pltpu.roll(x, 96, 1) * sb


def _rope_bwd(dy, c, sa, sb):
    return dy * c + pltpu.roll(dy * sa, 96, 1) + pltpu.roll(dy * sb, 32, 1)


def _rope_tables(seq):
    inv = ROPE_THETA ** (-jnp.arange(0, QK_ROPE, 2, dtype=F32) / QK_ROPE)
    ang = jnp.arange(seq, dtype=F32)[:, None] * inv[None, :]
    cos, sin = jnp.cos(ang), jnp.sin(ang)
    z32 = jnp.zeros_like(cos)
    z64 = jnp.zeros((seq, 64), F32)
    c = jnp.concatenate([cos, cos, z64], axis=1)
    sa = jnp.concatenate([z32, sin, z64], axis=1)
    sb = jnp.concatenate([-sin, z32, z64], axis=1)
    return c, sa, sb


def _mm(a, b, *, name, trans_b=False, add=None, out_dtype=F32, tm=TM_MM, tn=1024, tk=1024):
    m, k = a.shape
    n = b.shape[0] if trans_b else b.shape[1]
    tm, tn, tk = min(tm, m), min(tn, n), min(tk, k)
    assert m % tm == 0 and n % tn == 0 and k % tk == 0
    nk = k // tk
    has_add = add is not None

    def body(*refs):
        if has_add:
            a_ref, b_ref, add_ref, o_ref, acc_ref = refs
        else:
            a_ref, b_ref, o_ref, acc_ref = refs
        kk = pl.program_id(2)

        @pl.when(kk == 0)
        def _():
            acc_ref[...] = add_ref[...] if has_add else jnp.zeros_like(acc_ref)

        if trans_b:
            acc_ref[...] += _dot_nt(a_ref[...], b_ref[...])
        else:
            acc_ref[...] += _dot(a_ref[...], b_ref[...])

        @pl.when(kk == nk - 1)
        def _():
            o_ref[...] = acc_ref[...].astype(out_dtype)

    in_specs = [pl.BlockSpec((tm, tk), lambda i, j, kk: (i, kk))]
    if trans_b:
        in_specs.append(pl.BlockSpec((tn, tk), lambda i, j, kk: (j, kk)))
    else:
        in_specs.append(pl.BlockSpec((tk, tn), lambda i, j, kk: (kk, j)))
    args = [a, b]
    if has_add:
        in_specs.append(pl.BlockSpec((tm, tn), lambda i, j, kk: (i, j)))
        args.append(add)
    return pl.pallas_call(
        body, name=name, grid=(m // tm, n // tn, nk),
        in_specs=in_specs, out_specs=pl.BlockSpec((tm, tn), lambda i, j, kk: (i, j)),
        out_shape=jax.ShapeDtypeStruct((m, n), out_dtype),
        scratch_shapes=[pltpu.VMEM((tm, tn), F32)], compiler_params=_params(3),
    )(*args)


def _mm_tn(a, b, *, name, tm=TM_MM, tn=1024):
    flat = a.ndim == 2
    if flat:
        a = a[None]
    g, m, k = a.shape
    n = b.shape[1]
    tm, tn = min(tm, m), min(tn, n)
    assert m % tm == 0 and n % tn == 0

    def body(a_ref, b_ref, o_ref):
        @pl.when(pl.program_id(2) == 0)
        def _():
            o_ref[...] = jnp.zeros_like(o_ref)

        o_ref[...] += _dot_tn(a_ref[...], b_ref[...])

    out = pl.pallas_call(
        body, name=name, grid=(g, n // tn, m // tm),
        in_specs=[pl.BlockSpec((None, tm, k), lambda s, j, i: (s, i, 0)),
                  pl.BlockSpec((tm, tn), lambda s, j, i: (i, j))],
        out_specs=pl.BlockSpec((None, k, tn), lambda s, j, i: (s, 0, j)),
        out_shape=jax.ShapeDtypeStruct((g, k, n), F32), compiler_params=_params(3),
    )(a, b)
    return out[0] if flat else out


def _mm_slabs_out(a, wt, slabs, *, name, tm=TM_MM):
    m, k = a.shape
    tm = min(tm, m)
    n = D_MODEL

    def body(a_ref, w_ref, o_ref):
        o_ref[...] = _dot_nt(a_ref[...], w_ref[...])

    return pl.pallas_call(
        body, name=name, grid=(m // tm, slabs),
        in_specs=[pl.BlockSpec((tm, k), lambda i, j: (i, 0)), pl.BlockSpec((n, k), lambda i, j: (j, 0))],
        out_specs=pl.BlockSpec((None, tm, n), lambda i, j: (j, i, 0)),
        out_shape=jax.ShapeDtypeStruct((slabs, m, n), F32), compiler_params=_params(2),
    )(a, wt)


def _mm_slabs_in(a3, w, *, name, tm=TM_MM):
    slabs, m, k = a3.shape
    n = w.shape[1]
    tm = min(tm, m)

    def body(a_ref, w_ref, o_ref, acc_ref):
        j = pl.program_id(1)

        @pl.when(j == 0)
        def _():
            acc_ref[...] = jnp.zeros_like(acc_ref)

        acc_ref[...] += _dot(a_ref[...], w_ref[...])

        @pl.when(j == slabs - 1)
        def _():
            o_ref[...] = acc_ref[...]

    return pl.pallas_call(
        body, name=name, grid=(m // tm, slabs),
        in_specs=[pl.BlockSpec((None, tm, k), lambda i, j: (j, i, 0)), pl.BlockSpec((k, n), lambda i, j: (j, 0))],
        out_specs=pl.BlockSpec((tm, n), lambda i, j: (i, 0)),
        out_shape=jax.ShapeDtypeStruct((m, n), F32),
        scratch_shapes=[pltpu.VMEM((tm, n), F32)], compiler_params=_params(2),
    )(a3, w)


def _norm_in(x, g):
    s = x.shape[0]
    tm = min(TM_MM, s)

    def body(x_ref, g_ref, h_ref):
        h_ref[...] = _rms(x_ref[...], g_ref[...]).astype(BF16)

    return pl.pallas_call(
        body, name="norm_in", grid=(s // tm,),
        in_specs=[pl.BlockSpec((tm, D_MODEL), lambda i: (i, 0)), pl.BlockSpec((1, D_MODEL), lambda i: (0, 0))],
        out_specs=pl.BlockSpec((tm, D_MODEL), lambda i: (i, 0)),
        out_shape=jax.ShapeDtypeStruct((s, D_MODEL), BF16), compiler_params=_params(1),
    )(x, g)


def _norm_in_bwd(x, g, dh, dx2):
    s = x.shape[0]
    tm = min(TM_MM, s)

    def body(x_ref, g_ref, dh_ref, dx2_ref, dx_ref, dg_ref):
        @pl.when(pl.program_id(0) == 0)
        def _():
            dg_ref[...] = jnp.zeros_like(dg_ref)

        dx, dg_rows = _rms_bwd(x_ref[...], g_ref[...], dh_ref[...])
        dx_ref[...] = dx + dx2_ref[...]
        dg_ref[...] += jnp.sum(dg_rows, axis=0, keepdims=True)

    row = pl.BlockSpec((tm, D_MODEL), lambda i: (i, 0))
    vec = pl.BlockSpec((1, D_MODEL), lambda i: (0, 0))
    return pl.pallas_call(
        body, name="norm_in_bwd", grid=(s // tm,),
        in_specs=[row, vec, row, row], out_specs=[row, vec],
        out_shape=[jax.ShapeDtypeStruct((s, D_MODEL), F32), jax.ShapeDtypeStruct((1, D_MODEL), F32)],
        compiler_params=_params(1),
    )(x, g, dh, dx2)


def _chunk_rows(rows):
    return lax.broadcasted_iota(jnp.int32, (rows, HEAD_DIM), 0) & (HG_CHUNK - 1)


def _chunk_cumsum(x, rows):
    pos = _chunk_rows(rows)
    shift = 1
    while shift < HG_CHUNK:
        x = x + jnp.where(pos >= shift, pltpu.roll(x, shift, 0), 0.0)
        shift *= 2
    return x


def _chunk_revcumsum(x, rows):
    pos = _chunk_rows(rows)
    shift = 1
    while shift < HG_CHUNK:
        x = x + jnp.where(pos + shift < HG_CHUNK, pltpu.roll(x, rows - shift, 0), 0.0)
        shift *= 2
    return x


def _lower_bound(lbl):
    mx = jnp.maximum(lbl[0:1, :], lbl[1:2, :])
    e0 = jnp.exp(lbl[0:1, :] - mx)
    e1 = jnp.exp(lbl[1:2, :] - mx)
    p0 = e0 / (e0 + e1)
    return p0, p0 * (e1 / (e0 + e1))


def _hg_masks(rows, nch, tmask_s, bdmask_s):
    r = lax.broadcasted_iota(jnp.int32, (rows, rows), 0)
    c = lax.broadcasted_iota(jnp.int32, (rows, rows), 1)
    tmask_s[...] = jnp.where(((r >> CHUNK_SHIFT) == (c >> CHUNK_SHIFT)) & (r >= c), 1.0, 0.0)
    r = lax.broadcasted_iota(jnp.int32, (rows, nch * HEAD_DIM), 0)
    c = lax.broadcasted_iota(jnp.int32, (rows, nch * HEAD_DIM), 1)
    bdmask_s[...] = jnp.where((r >> CHUNK_SHIFT) == (c >> HEAD_SHIFT), 1.0, 0.0).astype(BF16)


def _block_diag(x, nch, bdmask):
    return jnp.tile(x, (1, nch)) * bdmask


def _hgrn_fwd(hg, lb_logits, norm_g):
    s = hg.shape[1]
    rows = min(HG_ROWS, s)
    nblk = s // rows
    nch = rows // HG_CHUNK

    def body(hg_ref, lbl_ref, g_ref, o_ref, ya_ref, st0_ref, st_s, stall_s, tmask_s, bdmask_s):
        @pl.when(pl.program_id(1) == 0)
        def _():
            st_s[...] = jnp.zeros_like(st_s)
            _hg_masks(rows, nch, tmask_s, bdmask_s)

        bdmask = bdmask_s[...]
        tmask = tmask_s[...] > 0.5
        for hh in range(HG_HEADS):
            hc = slice(hh * HEAD_DIM, (hh + 1) * HEAD_DIM)
            hq = hg_ref[0, :, hc]
            hf = hg_ref[1, :, hc]
            hi = hg_ref[2, :, hc]
            hz = hg_ref[3, :, hc]
            lb, _ = _lower_bound(lbl_ref[:, hc])
            f = lb + (1.0 - lb) * jax.nn.sigmoid(hf)
            q = hq * jax.nn.sigmoid(hq)
            k = 1.0 - f
            logf = jnp.log(f)
            b = _chunk_cumsum(logf, rows)
            q_in = (q * jnp.exp(b)).astype(BF16)
            k_in = (k * jnp.exp(-b)).astype(BF16)
            k_out = (k * jnp.exp(_chunk_revcumsum(logf, rows) - logf)).astype(BF16)
            vb = hi.astype(BF16)

            sc = jnp.where(tmask, _dot_nt(q_in, k_in), 0.0)
            o_intra = _dot(sc.astype(BF16), vb)
            kvt = _dot_tn(vb, _block_diag(k_out, nch, bdmask))
            st = st_s[hh]
            st0_ref[hh] = st
            for c in range(nch):
                cols = slice(c * HEAD_DIM, (c + 1) * HEAD_DIM)
                last = (c + 1) * HG_CHUNK - 1
                stall_s[hh, :, cols] = st.astype(BF16)
                st = st * jnp.exp(b[last:last + 1, :]) + kvt[:, cols]
            st_s[hh] = st
            o = o_intra + _dot_nt(_block_diag(q_in, nch, bdmask), stall_s[hh])
            o_ref[:, hc] = o
            silu_z, _ = _silu_parts(hz)
            ya_ref[:, hc] = (_rms(o, g_ref[...]) * silu_z).astype(BF16)

    nh = HG_HEADS
    return pl.pallas_call(
        body, name="hgrn_fwd", grid=(HEADS // nh, nblk),
        in_specs=[pl.BlockSpec((4, rows, nh * HEAD_DIM), lambda h, i: (0, i, h)),
                  pl.BlockSpec((2, nh * HEAD_DIM), lambda h, i: (0, h)),
                  pl.BlockSpec((1, HEAD_DIM), lambda h, i: (0, 0))],
        out_specs=[pl.BlockSpec((rows, nh * HEAD_DIM), lambda h, i: (i, h)),
                   pl.BlockSpec((rows, nh * HEAD_DIM), lambda h, i: (i, h)),
                   pl.BlockSpec((nh, None, HEAD_DIM, HEAD_DIM), lambda h, i: (h, i, 0, 0))],
        out_shape=[jax.ShapeDtypeStruct((s, D_MODEL), F32), jax.ShapeDtypeStruct((s, D_MODEL), BF16),
                   jax.ShapeDtypeStruct((HEADS, nblk, HEAD_DIM, HEAD_DIM), F32)],
        scratch_shapes=[pltpu.VMEM((nh, HEAD_DIM, HEAD_DIM), F32), pltpu.VMEM((nh, HEAD_DIM, nch * HEAD_DIM), BF16),
                        pltpu.VMEM((rows, rows), F32), pltpu.VMEM((rows, nch * HEAD_DIM), BF16)],
        compiler_params=_params(2),
    )(hg, lb_logits, norm_g)


def _hgrn_bwd(hg, o_pre, dya, st0, lb_logits, norm_g):
    s = hg.shape[1]
    rows = min(HG_ROWS, s)
    nblk = s // rows
    nch = rows // HG_CHUNK

    def body(hg_ref, o_ref, dya_ref, st0_ref, lbl_ref, g_ref, dhg_ref, dlb_ref, dg_ref,
             dst_s, stp_s, stp_rows_s, dst_rows_s, dst_lane_s, dbl_s, tmask_s, bdmask_s):
        @pl.when(pl.program_id(1) == 0)
        def _():
            dst_s[...] = jnp.zeros_like(dst_s)
            dlb_ref[...] = jnp.zeros_like(dlb_ref)
            dg_ref[...] = jnp.zeros_like(dg_ref)
            _hg_masks(rows, nch, tmask_s, bdmask_s)

        bdmask = bdmask_s[...]
        tmask = tmask_s[...] > 0.5
        g = g_ref[...]
        for hh in range(HG_HEADS):
            hc = slice(hh * HEAD_DIM, (hh + 1) * HEAD_DIM)
            hq = hg_ref[0, :, hc]
            hf = hg_ref[1, :, hc]
            hi = hg_ref[2, :, hc]
            hz = hg_ref[3, :, hc]
            lb, _ = _lower_bound(lbl_ref[:, hc])
            sg = jax.nn.sigmoid(hf)
            f = lb + (1.0 - lb) * sg
            q, dsilu_q = _silu_parts(hq)
            k = 1.0 - f
            logf = jnp.log(f)
            b = _chunk_cumsum(logf, rows)
            eb = jnp.exp(b)
            enb = jnp.exp(-b)
            ebl = jnp.exp(_chunk_revcumsum(logf, rows) - logf)
            q_in32 = q * eb
            k_in32 = k * enb
            k_out32 = k * ebl
            q_in = q_in32.astype(BF16)
            k_in = k_in32.astype(BF16)
            k_out = k_out32.astype(BF16)
            vb = hi.astype(BF16)
            kbd = _block_diag(k_out, nch, bdmask)
            qbd = _block_diag(q_in, nch, bdmask)
            decs = [jnp.exp(b[(c + 1) * HG_CHUNK - 1:(c + 1) * HG_CHUNK, :]) for c in range(nch)]

            kvt = _dot_tn(vb, kbd)
            st = st0_ref[hh]
            for c in range(nch):
                stp_s[hh, c] = st
                stp_rows_s[hh, c * HEAD_DIM:(c + 1) * HEAD_DIM, :] = st.astype(BF16)
                st = st * decs[c] + kvt[:, c * HEAD_DIM:(c + 1) * HEAD_DIM]

            o = o_ref[:, hc]
            rstd = lax.rsqrt(jnp.mean(o * o, axis=-1, keepdims=True) + EPS)
            oh = o * rstd
            silu_z, dsilu_z = _silu_parts(hz)
            dya_v = dya_ref[:, hc]
            dn = dya_v * silu_z
            dhz = dya_v * (oh * g) * dsilu_z
            dg_ref[hh] += jnp.sum(dn * oh, axis=0, keepdims=True)
            doh = dn * g
            do = (rstd * (doh - oh * jnp.mean(doh * oh, axis=-1, keepdims=True))).astype(BF16)

            dq_all = _dot_tn(do, qbd)
            dst = dst_s[hh]
            ddecs = [None] * nch
            for c in reversed(range(nch)):
                dstb = dst.astype(BF16)
                dst_lane_s[hh, :, c * HEAD_DIM:(c + 1) * HEAD_DIM] = dstb
                dst_rows_s[hh, c * HEAD_DIM:(c + 1) * HEAD_DIM, :] = dstb
                ddecs[c] = jnp.sum(dst * stp_s[hh, c], axis=0, keepdims=True) * decs[c]
                dst = dst * decs[c] + dq_all[:, c * HEAD_DIM:(c + 1) * HEAD_DIM]
            dst_s[hh] = dst

            sc = jnp.where(tmask, _dot_nt(q_in, k_in), 0.0).astype(BF16)
            dkout = _dot(_block_diag(vb, nch, bdmask), dst_rows_s[hh])
            dv = _dot_nt(kbd, dst_lane_s[hh]) + _dot_tn(sc, do)
            dsc = jnp.where(tmask, _dot_nt(do, vb), 0.0).astype(BF16)
            dqin = _dot(dsc, k_in) + _dot(_block_diag(do, nch, bdmask), stp_rows_s[hh])
            dkin = _dot_tn(dsc, q_in)

            dko = dkout * k_out32
            for c in range(nch):
                sl = slice(c * HG_CHUNK, (c + 1) * HG_CHUNK)
                dbl = jnp.sum(dko[sl], axis=0, keepdims=True) + ddecs[c]
                dbl_s[hh, sl, :] = jnp.broadcast_to(dbl, (HG_CHUNK, HEAD_DIM))
            dq = dqin * eb
            dk = dkin * enb + dkout * ebl
            db = dqin * q_in32 - dkin * k_in32 - dko
            dlogf = _chunk_revcumsum(db, rows) + dbl_s[hh]
            df = dlogf / f - dk
            dlb_ref[:, hc] += jnp.sum(df * (1.0 - sg), axis=0, keepdims=True)
            dhg_ref[0, :, hc] = (dq * dsilu_q).astype(BF16)
            dhg_ref[1, :, hc] = (df * (1.0 - lb) * sg * (1.0 - sg)).astype(BF16)
            dhg_ref[2, :, hc] = dv.astype(BF16)
            dhg_ref[3, :, hc] = dhz.astype(BF16)

    last = nblk - 1
    nh = HG_HEADS
    wide = nh * HEAD_DIM
    return pl.pallas_call(
        body, name="hgrn_bwd", grid=(HEADS // nh, nblk),
        in_specs=[pl.BlockSpec((4, rows, wide), lambda h, i: (0, last - i, h)),
                  pl.BlockSpec((rows, wide), lambda h, i: (last - i, h)),
                  pl.BlockSpec((rows, wide), lambda h, i: (last - i, h)),
                  pl.BlockSpec((nh, None, HEAD_DIM, HEAD_DIM), lambda h, i: (h, last - i, 0, 0)),
                  pl.BlockSpec((2, wide), lambda h, i: (0, h)),
                  pl.BlockSpec((1, HEAD_DIM), lambda h, i: (0, 0))],
        out_specs=[pl.BlockSpec((4, rows, wide), lambda h, i: (0, last - i, h)),
                   pl.BlockSpec((1, wide), lambda h, i: (0, h)),
                   pl.BlockSpec((nh, 1, HEAD_DIM), lambda h, i: (h, 0, 0))],
        out_shape=[jax.ShapeDtypeStruct((4, s, D_MODEL), BF16), jax.ShapeDtypeStruct((1, D_MODEL), F32),
                   jax.ShapeDtypeStruct((HEADS, 1, HEAD_DIM), F32)],
        scratch_shapes=[pltpu.VMEM((nh, HEAD_DIM, HEAD_DIM), F32), pltpu.VMEM((nh, nch, HEAD_DIM, HEAD_DIM), F32),
                        pltpu.VMEM((nh, nch * HEAD_DIM, HEAD_DIM), BF16), pltpu.VMEM((nh, nch * HEAD_DIM, HEAD_DIM), BF16),
                        pltpu.VMEM((nh, HEAD_DIM, nch * HEAD_DIM), BF16), pltpu.VMEM((nh, rows, HEAD_DIM), F32),
                        pltpu.VMEM((rows, rows), F32), pltpu.VMEM((rows, nch * HEAD_DIM), BF16)],
        compiler_params=_params(2),
    )(hg, o_pre, dya, st0, lb_logits, norm_g)


def _mla_pre(ms, q_a_g, kv_a_g, wuq3, wukv3, tabs):
    s = ms.shape[0]
    tm = min(TM_FUSED, s)

    def body(ms_ref, qg_ref, kvg_ref, wuq_ref, wukv_ref, c_ref, sa_ref, sb_ref,
             q_ref, k_ref, v_ref, cqn_ref, ckvn_ref):
        c, sa, sb = c_ref[...], sa_ref[...], sb_ref[...]
        cqn = _rms(ms_ref[:, 0:Q_LORA], qg_ref[...]).astype(BF16)
        ckvn = _rms(ms_ref[:, Q_LORA:Q_LORA + KV_LORA], kvg_ref[...]).astype(BF16)
        cqn_ref[...] = cqn
        ckvn_ref[...] = ckvn
        k_pe = _rope(ms_ref[:, Q_LORA + KV_LORA:MS_COLS], c, sa, sb).astype(BF16)
        for h in range(HEADS):
            qh = _dot(cqn, wuq_ref[h])
            q_ref[h, :, 0:128] = (qh[:, 0:128] * Q_PRESCALE).astype(BF16)
            q_ref[h, :, 128:256] = (_rope(qh[:, 128:256], c, sa, sb) * Q_PRESCALE).astype(BF16)
            kvh = _dot(ckvn, wukv_ref[h])
            k_ref[h, :, 0:128] = kvh[:, 0:128].astype(BF16)
            k_ref[h, :, 128:256] = k_pe
            v_ref[h] = kvh[:, 128:256].astype(BF16)

    tab = pl.BlockSpec((tm, 128), lambda i: (i, 0))
    return pl.pallas_call(
        body, name="mla_pre", grid=(s // tm,),
        in_specs=[pl.BlockSpec((tm, MS_COLS), lambda i: (i, 0)),
                  pl.BlockSpec((1, Q_LORA), lambda i: (0, 0)), pl.BlockSpec((1, KV_LORA), lambda i: (0, 0)),
                  pl.BlockSpec((HEADS, Q_LORA, QK_PAD), lambda i: (0, 0, 0)),
                  pl.BlockSpec((HEADS, KV_LORA, 256), lambda i: (0, 0, 0)), tab, tab, tab],
        out_specs=[pl.BlockSpec((HEADS, tm, QK_PAD), lambda i: (0, i, 0)),
                   pl.BlockSpec((HEADS, tm, QK_PAD), lambda i: (0, i, 0)),
                   pl.BlockSpec((HEADS, tm, HEAD_DIM), lambda i: (0, i, 0)),
                   pl.BlockSpec((tm, Q_LORA), lambda i: (i, 0)), pl.BlockSpec((tm, KV_LORA), lambda i: (i, 0))],
        out_shape=[jax.ShapeDtypeStruct((HEADS, s, QK_PAD), BF16), jax.ShapeDtypeStruct((HEADS, s, QK_PAD), BF16),
                   jax.ShapeDtypeStruct((HEADS, s, HEAD_DIM), BF16),
                   jax.ShapeDtypeStruct((s, Q_LORA), BF16), jax.ShapeDtypeStruct((s, KV_LORA), BF16)],
        compiler_params=_params(1),
    )(ms, q_a_g, kv_a_g, wuq3, wukv3, *tabs)


def _causal_mask(t):
    r = lax.broadcasted_iota(jnp.int32, (t, t), 0)
    c = lax.broadcasted_iota(jnp.int32, (t, t), 1)
    return r >= c


def _flash_fwd(q, k, v, mz):
    s = q.shape[1]
    t = min(TQ, s)

    def body(q_ref, k_ref, v_ref, mz_ref, o_ref, yb_ref, lse_ref, m_s, l_s, acc_s):
        i = pl.program_id(1)
        m_s[...] = jnp.full_like(m_s, -jnp.inf)
        l_s[...] = jnp.zeros_like(l_s)
        acc_s[...] = jnp.zeros_like(acc_s)

        def step(j, modes):
            rows = pl.ds(pl.multiple_of(j * t, t), t)
            for hh in range(FLASH_HEADS):
                for ch, masked in enumerate(modes):
                    if masked is None:
                        continue
                    r = slice(ch * t, (ch + 1) * t)
                    sc = _dot_nt(q_ref[hh, r, :], k_ref[hh, rows, :])
                    if masked:
                        sc = jnp.where(_causal_mask(t), sc, -jnp.inf)
                    m_prev = m_s[hh, r, :]
                    m_new = jnp.maximum(m_prev, jnp.max(sc, axis=-1, keepdims=True))
                    p = jnp.exp2(sc - jnp.tile(m_new, (1, t // 128)))
                    alpha = jnp.exp2(m_prev - m_new)
                    l_s[hh, r, :] = alpha * l_s[hh, r, :] + jnp.sum(p, axis=-1, keepdims=True)
                    acc_s[hh, r, :] = alpha * acc_s[hh, r, :] + _dot(p.astype(BF16), v_ref[hh, rows, :])
                    m_s[hh, r, :] = m_new

        def loop_body(j, carry):
            step(j, (False, False))
            return carry

        lax.fori_loop(0, 2 * i, loop_body, 0)
        step(2 * i, (True, False))
        step(2 * i + 1, (None, True))
        for hh in range(FLASH_HEADS):
            cols = slice(hh * HEAD_DIM, (hh + 1) * HEAD_DIM)
            out = acc_s[hh] / l_s[hh]
            o_ref[:, cols] = out
            silu_z, _ = _silu_parts(mz_ref[:, cols])
            yb_ref[:, cols] = (out * silu_z).astype(BF16)
            lse_ref[hh] = m_s[hh] + jnp.log2(l_s[hh])

    nh = FLASH_HEADS
    t2 = 2 * t
    col = pl.BlockSpec((t2, nh * HEAD_DIM), lambda h, i: (i, h))
    return pl.pallas_call(
        body, name="flash_fwd", grid=(HEADS // nh, s // t2),
        in_specs=[pl.BlockSpec((nh, t2, QK_PAD), lambda h, i: (h, i, 0)),
                  pl.BlockSpec((nh, s, QK_PAD), lambda h, i: (h, 0, 0)),
                  pl.BlockSpec((nh, s, HEAD_DIM), lambda h, i: (h, 0, 0)), col],
        out_specs=[col, col, pl.BlockSpec((nh, t2, 128), lambda h, i: (h, i, 0))],
        out_shape=[jax.ShapeDtypeStruct((s, D_MODEL), F32), jax.ShapeDtypeStruct((s, D_MODEL), BF16),
                   jax.ShapeDtypeStruct((HEADS, s, 128), F32)],
        scratch_shapes=[pltpu.VMEM((nh, t2, 128), F32), pltpu.VMEM((nh, t2, 128), F32),
                        pltpu.VMEM((nh, t2, HEAD_DIM), F32)],
        compiler_params=_params(2),
    )(q, k, v, mz)


def _flash_bwd(q, k, v, dyb, mz, o_att, lse, tabs):
    s = q.shape[1]
    t = min(TQ, s)

    def body(q_ref, k_ref, v_ref, dyb_ref, mz_ref, o_ref, lse_ref, c_ref, sa_ref, sb_ref,
             dq_ref, dk_ref, dv_ref, dmz_ref, dq_s, delta_s, do_s):
        i = pl.program_id(1)

        @pl.when(i == 0)
        def _():
            dk_ref[...] = jnp.zeros_like(dk_ref)
            dv_ref[...] = jnp.zeros_like(dv_ref)

        silu_z, dsilu_z = _silu_parts(mz_ref[...])
        dyb_v = dyb_ref[...]
        out = o_ref[...]
        do32 = dyb_v * silu_z
        dmz_ref[...] = (dyb_v * out * dsilu_z).astype(BF16)
        delta_s[...] = jnp.broadcast_to(jnp.sum(do32 * out, axis=-1, keepdims=True), (2 * t, 128))
        do_s[...] = do32.astype(BF16)
        dq_s[...] = jnp.zeros_like(dq_s)

        def step(j, modes):
            rows = pl.ds(pl.multiple_of(j * t, t), t)
            kj = k_ref[rows, :]
            vj = v_ref[rows, :]
            dv_acc = None
            dk_acc = None
            for ch, masked in enumerate(modes):
                if masked is None:
                    continue
                r = slice(ch * t, (ch + 1) * t)
                qv = q_ref[r, :]
                do = do_s[r, :]
                sc = _dot_nt(qv, kj)
                if masked:
                    sc = jnp.where(_causal_mask(t), sc, -jnp.inf)
                p = jnp.exp2(sc - jnp.tile(lse_ref[r, :], (1, t // 128)))
                dp = _dot_nt(do, vj)
                ds = (p * (dp - jnp.tile(delta_s[r, :], (1, t // 128)))).astype(BF16)
                dv_c = _dot_tn(p.astype(BF16), do)
                dk_c = _dot_tn(ds, qv)
                dv_acc = dv_c if dv_acc is None else dv_acc + dv_c
                dk_acc = dk_c if dk_acc is None else dk_acc + dk_c
                dq_s[r, :] += _dot(ds, kj)
            dv_ref[rows, :] += dv_acc
            dk_ref[rows, :] += dk_acc

        def loop_body(j, carry):
            step(j, (False, False))
            return carry

        lax.fori_loop(0, 2 * i, loop_body, 0)
        step(2 * i, (True, False))
        step(2 * i + 1, (None, True))
        dq = dq_s[...] * ATT_SCALE
        dq_ref[:, 0:128] = dq[:, 0:128].astype(BF16)
        dq_ref[:, 128:256] = _rope_bwd(dq[:, 128:256], c_ref[...], sa_ref[...], sb_ref[...]).astype(BF16)

    t2 = 2 * t
    col = pl.BlockSpec((t2, HEAD_DIM), lambda h, i: (i, h))
    tab = pl.BlockSpec((t2, 128), lambda h, i: (i, 0))
    return pl.pallas_call(
        body, name="flash_bwd", grid=(HEADS, s // t2),
        in_specs=[pl.BlockSpec((None, t2, QK_PAD), lambda h, i: (h, i, 0)),
                  pl.BlockSpec((None, s, QK_PAD), lambda h, i: (h, 0, 0)),
                  pl.BlockSpec((None, s, HEAD_DIM), lambda h, i: (h, 0, 0)),
                  col, col, col, pl.BlockSpec((None, t2, 128), lambda h, i: (h, i, 0)), tab, tab, tab],
        out_specs=[pl.BlockSpec((None, t2, QK_PAD), lambda h, i: (h, i, 0)),
                   pl.BlockSpec((None, s, QK_PAD), lambda h, i: (h, 0, 0)),
                   pl.BlockSpec((None, s, HEAD_DIM), lambda h, i: (h, 0, 0)), col],
        out_shape=[jax.ShapeDtypeStruct((HEADS, s, QK_PAD), BF16), jax.ShapeDtypeStruct((HEADS, s, QK_PAD), F32),
                   jax.ShapeDtypeStruct((HEADS, s, HEAD_DIM), F32), jax.ShapeDtypeStruct((s, D_MODEL), BF16)],
        scratch_shapes=[pltpu.VMEM((t2, QK_PAD), F32), pltpu.VMEM((t2, 128), F32), pltpu.VMEM((t2, HEAD_DIM), BF16)],
        compiler_params=_params(2),
    )(q, k, v, dyb, mz, o_att, lse, *tabs)


def _mla_bwd_proj(dq, dk, dv, cqn, ckvn, ms, q_a_g, kv_a_g, wuq3, wukv3, tabs):
    s = ms.shape[0]
    tm = min(TM_FUSED, s)

    def body(dq_ref, dk_ref, dv_ref, cqn_ref, ckvn_ref, ms_ref, qg_ref, kvg_ref, wuq_ref, wukv_ref,
             c_ref, sa_ref, sb_ref, dms_ref, dwuq_ref, dwukv_ref, dqg_ref, dkvg_ref):
        @pl.when(pl.program_id(0) == 0)
        def _():
            dwuq_ref[...] = jnp.zeros_like(dwuq_ref)
            dwukv_ref[...] = jnp.zeros_like(dwukv_ref)
            dqg_ref[...] = jnp.zeros_like(dqg_ref)
            dkvg_ref[...] = jnp.zeros_like(dkvg_ref)

        cqn = cqn_ref[...]
        ckvn = ckvn_ref[...]
        dcqn = jnp.zeros((tm, Q_LORA), F32)
        dckvn = jnp.zeros((tm, KV_LORA), F32)
        dkpe = jnp.zeros((tm, 128), F32)
        for h in range(HEADS):
            dqh = dq_ref[h]
            dcqn += _dot_nt(dqh, wuq_ref[h])
            dwuq_ref[h] += _dot_tn(cqn, dqh)
            dkh = dk_ref[h] * LN2
            dkvh = jnp.concatenate([dkh[:, 0:128], dv_ref[h]], axis=1).astype(BF16)
            dckvn += _dot_nt(dkvh, wukv_ref[h])
            dwukv_ref[h] += _dot_tn(ckvn, dkvh)
            dkpe += dkh[:, 128:256]
        dcq, dqg_rows = _rms_bwd(ms_ref[:, 0:Q_LORA], qg_ref[...], dcqn)
        dckv, dkvg_rows = _rms_bwd(ms_ref[:, Q_LORA:Q_LORA + KV_LORA], kvg_ref[...], dckvn)
        dqg_ref[...] += jnp.sum(dqg_rows, axis=0, keepdims=True)
        dkvg_ref[...] += jnp.sum(dkvg_rows, axis=0, keepdims=True)
        dms_ref[:, 0:Q_LORA] = dcq.astype(BF16)
        dms_ref[:, Q_LORA:Q_LORA + KV_LORA] = dckv.astype(BF16)
        dms_ref[:, Q_LORA + KV_LORA:MS_COLS] = _rope_bwd(dkpe, c_ref[...], sa_ref[...], sb_ref[...]).astype(BF16)

    tab = pl.BlockSpec((tm, 128), lambda i: (i, 0))
    wq = pl.BlockSpec((HEADS, Q_LORA, QK_PAD), lambda i: (0, 0, 0))
    wkv = pl.BlockSpec((HEADS, KV_LORA, 256), lambda i: (0, 0, 0))
    qg = pl.BlockSpec((1, Q_LORA), lambda i: (0, 0))
    kvg = pl.BlockSpec((1, KV_LORA), lambda i: (0, 0))
    return pl.pallas_call(
        body, name="mla_bwd_proj", grid=(s // tm,),
        in_specs=[pl.BlockSpec((HEADS, tm, QK_PAD), lambda i: (0, i, 0)),
                  pl.BlockSpec((HEADS, tm, QK_PAD), lambda i: (0, i, 0)),
                  pl.BlockSpec((HEADS, tm, HEAD_DIM), lambda i: (0, i, 0)),
                  pl.BlockSpec((tm, Q_LORA), lambda i: (i, 0)), pl.BlockSpec((tm, KV_LORA), lambda i: (i, 0)),
                  pl.BlockSpec((tm, MS_COLS), lambda i: (i, 0)), qg, kvg, wq, wkv, tab, tab, tab],
        out_specs=[pl.BlockSpec((tm, MS_COLS), lambda i: (i, 0)), wq, wkv, qg, kvg],
        out_shape=[jax.ShapeDtypeStruct((s, MS_COLS), BF16), jax.ShapeDtypeStruct((HEADS, Q_LORA, QK_PAD), F32),
                   jax.ShapeDtypeStruct((HEADS, KV_LORA, 256), F32),
                   jax.ShapeDtypeStruct((1, Q_LORA), F32), jax.ShapeDtypeStruct((1, KV_LORA), F32)],
        compiler_params=_params(1),
    )(dq, dk, dv, cqn, ckvn, ms, q_a_g, kv_a_g, wuq3, wukv3, *tabs)


def _merge_loss(ya, yb, glog, b_gate, x, tgt, fg, wpa, wpb, wout):
    s = x.shape[0]
    tm = min(TM_FUSED, s)

    def body(ya_ref, yb_ref, g0_ref, g1_ref, b0_ref, b1_ref, x_ref, t_ref, fg_ref, wpa_ref, wpb_ref, wout_ref,
             mg_ref, pa_ref, pb_ref, dx2_ref, loss_ref, dfg_ref):
        @pl.when(pl.program_id(0) == 0)
        def _():
            loss_ref[...] = jnp.zeros_like(loss_ref)
            dfg_ref[...] = jnp.zeros_like(dfg_ref)

        pa = _dot(ya_ref[...], wpa_ref[...])
        pb = _dot(yb_ref[...], wpb_ref[...])
        pa_ref[...] = pa
        pb_ref[...] = pb
        merged = (jax.nn.sigmoid(g0_ref[...] + b0_ref[...]) * pa
                  + jax.nn.sigmoid(g1_ref[...] + b1_ref[...]) * pb).astype(BF16)
        mg_ref[...] = merged
        x2 = x_ref[...] + _dot(merged, wout_ref[...])
        fg_v = fg_ref[...]
        err = _rms(x2, fg_v) - t_ref[...]
        loss_ref[...] += 0.5 * jnp.sum(jnp.mean(err * err, axis=-1, keepdims=True), axis=0, keepdims=True)
        dx2, dfg_rows = _rms_bwd(x2, fg_v, err * (1.0 / D_MODEL))
        dx2_ref[...] = dx2
        dfg_ref[...] += jnp.sum(dfg_rows, axis=0, keepdims=True)

    row = pl.BlockSpec((tm, D_MODEL), lambda i: (i, 0))
    row1 = pl.BlockSpec((tm, D_MODEL), lambda i: (i, 1))
    vec = pl.BlockSpec((1, D_MODEL), lambda i: (0, 0))
    vec1 = pl.BlockSpec((1, D_MODEL), lambda i: (0, 1))
    wsp = pl.BlockSpec((D_MODEL, D_MODEL), lambda i: (0, 0))
    return pl.pallas_call(
        body, name="merge_loss", grid=(s // tm,),
        in_specs=[row, row, row, row1, vec, vec1, row, row, vec, wsp, wsp, wsp],
        out_specs=[row, row, row, row, pl.BlockSpec((1, 128), lambda i: (0, 0)), vec],
        out_shape=[jax.ShapeDtypeStruct((s, D_MODEL), BF16), jax.ShapeDtypeStruct((s, D_MODEL), F32),
                   jax.ShapeDtypeStruct((s, D_MODEL), F32), jax.ShapeDtypeStruct((s, D_MODEL), F32),
                   jax.ShapeDtypeStruct((1, 128), F32), jax.ShapeDtypeStruct((1, D_MODEL), F32)],
        compiler_params=_params(1),
    )(ya, yb, glog, glog, b_gate, b_gate, x, tgt, fg, wpa, wpb, wout)


def _merge_bwd(dx2, pa, pb, glog, b_gate, wpa, wpb, wout):
    s = dx2.shape[0]
    tm = min(TM_FUSED, s)

    def body(dx2_ref, pa_ref, pb_ref, g0_ref, g1_ref, b0_ref, b1_ref, wpa_ref, wpb_ref, wout_ref,
             dya_ref, dyb_ref, dgl_ref, dpa_ref, dpb_ref, dx2b_ref, dbg_ref):
        @pl.when(pl.program_id(0) == 0)
        def _():
            dbg_ref[...] = jnp.zeros_like(dbg_ref)

        dx2b = dx2_ref[...].astype(BF16)
        dx2b_ref[...] = dx2b
        dmg = _dot_nt(dx2b, wout_ref[...])
        g0 = jax.nn.sigmoid(g0_ref[...] + b0_ref[...])
        g1 = jax.nn.sigmoid(g1_ref[...] + b1_ref[...])
        dpa = (dmg * g0).astype(BF16)
        dpb = (dmg * g1).astype(BF16)
        dpa_ref[...] = dpa
        dpb_ref[...] = dpb
        dgl0 = dmg * pa_ref[...] * g0 * (1.0 - g0)
        dgl1 = dmg * pb_ref[...] * g1 * (1.0 - g1)
        dgl_ref[:, 0:D_MODEL] = dgl0.astype(BF16)
        dgl_ref[:, D_MODEL:2 * D_MODEL] = dgl1.astype(BF16)
        dbg_ref[:, 0:D_MODEL] += jnp.sum(dgl0, axis=0, keepdims=True)
        dbg_ref[:, D_MODEL:2 * D_MODEL] += jnp.sum(dgl1, axis=0, keepdims=True)
        dya_ref[...] = _dot_nt(dpa, wpa_ref[...])
        dyb_ref[...] = _dot_nt(dpb, wpb_ref[...])

    row = pl.BlockSpec((tm, D_MODEL), lambda i: (i, 0))
    row1 = pl.BlockSpec((tm, D_MODEL), lambda i: (i, 1))
    row2 = pl.BlockSpec((tm, 2 * D_MODEL), lambda i: (i, 0))
    vec = pl.BlockSpec((1, D_MODEL), lambda i: (0, 0))
    vec1 = pl.BlockSpec((1, D_MODEL), lambda i: (0, 1))
    vec2 = pl.BlockSpec((1, 2 * D_MODEL), lambda i: (0, 0))
    wsp = pl.BlockSpec((D_MODEL, D_MODEL), lambda i: (0, 0))
    return pl.pallas_call(
        body, name="merge_bwd", grid=(s // tm,),
        in_specs=[row, row, row, row, row1, vec, vec1, wsp, wsp, wsp],
        out_specs=[row, row, row2, row, row, row, vec2],
        out_shape=[jax.ShapeDtypeStruct((s, D_MODEL), F32), jax.ShapeDtypeStruct((s, D_MODEL), F32),
                   jax.ShapeDtypeStruct((s, 2 * D_MODEL), BF16), jax.ShapeDtypeStruct((s, D_MODEL), BF16),
                   jax.ShapeDtypeStruct((s, D_MODEL), BF16), jax.ShapeDtypeStruct((s, D_MODEL), BF16),
                   jax.ShapeDtypeStruct((1, 2 * D_MODEL), F32)],
        compiler_params=_params(1),
    )(dx2, pa, pb, glog, glog, b_gate, b_gate, wpa, wpb, wout)


def _merge_fused(ya, yb, glog, b_gate, x, tgt, fg, wproj):
    s = x.shape[0]
    tm = min(TM_FUSED, s)

    def body(ya_ref, yb_ref, g0_ref, g1_ref, b0_ref, b1_ref, x_ref, t_ref, fg_ref, w_ref,
             mg_ref, dx2_ref, dx2b_ref, dya_ref, dyb_ref, dgl_ref, dpa_ref, dpb_ref, loss_ref, dfg_ref, dbg_ref):
        @pl.when(pl.program_id(0) == 0)
        def _():
            loss_ref[...] = jnp.zeros_like(loss_ref)
            dfg_ref[...] = jnp.zeros_like(dfg_ref)
            dbg_ref[...] = jnp.zeros_like(dbg_ref)

        pa = _dot(ya_ref[...], w_ref[0])
        pb = _dot(yb_ref[...], w_ref[1])
        g0 = jax.nn.sigmoid(g0_ref[...] + b0_ref[...])
        g1 = jax.nn.sigmoid(g1_ref[...] + b1_ref[...])
        merged = (g0 * pa + g1 * pb).astype(BF16)
        mg_ref[...] = merged
        x2 = x_ref[...] + _dot(merged, w_ref[2])
        fg_v = fg_ref[...]
        err = _rms(x2, fg_v) - t_ref[...]
        loss_ref[...] += 0.5 * jnp.sum(jnp.mean(err * err, axis=-1, keepdims=True), axis=0, keepdims=True)
        dx2, dfg_rows = _rms_bwd(x2, fg_v, err * (1.0 / D_MODEL))
        dx2_ref[...] = dx2
        dfg_ref[...] += jnp.sum(dfg_rows, axis=0, keepdims=True)

        dx2b = dx2.astype(BF16)
        dx2b_ref[...] = dx2b
        dmg = _dot_nt(dx2b, w_ref[2])
        dpa = (dmg * g0).astype(BF16)
        dpb = (dmg * g1).astype(BF16)
        dpa_ref[...] = dpa
        dpb_ref[...] = dpb
        dgl0 = dmg * pa * g0 * (1.0 - g0)
        dgl1 = dmg * pb * g1 * (1.0 - g1)
        dgl_ref[:, 0:D_MODEL] = dgl0.astype(BF16)
        dgl_ref[:, D_MODEL:2 * D_MODEL] = dgl1.astype(BF16)
        dbg_ref[:, 0:D_MODEL] += jnp.sum(dgl0, axis=0, keepdims=True)
        dbg_ref[:, D_MODEL:2 * D_MODEL] += jnp.sum(dgl1, axis=0, keepdims=True)
        dya_ref[...] = _dot_nt(dpa, w_ref[0])
        dyb_ref[...] = _dot_nt(dpb, w_ref[1])

    row = pl.BlockSpec((tm, D_MODEL), lambda i: (i, 0))
    row1 = pl.BlockSpec((tm, D_MODEL), lambda i: (i, 1))
    row2 = pl.BlockSpec((tm, 2 * D_MODEL), lambda i: (i, 0))
    vec = pl.BlockSpec((1, D_MODEL), lambda i: (0, 0))
    vec1 = pl.BlockSpec((1, D_MODEL), lambda i: (0, 1))
    vec2 = pl.BlockSpec((1, 2 * D_MODEL), lambda i: (0, 0))
    f32_rows = jax.ShapeDtypeStruct((s, D_MODEL), F32)
    bf16_rows = jax.ShapeDtypeStruct((s, D_MODEL), BF16)
    return pl.pallas_call(
        body, name="merge_fused", grid=(s // tm,),
        in_specs=[row, row, row, row1, vec, vec1, row, row, vec, pl.BlockSpec((3, D_MODEL, D_MODEL), lambda i: (0, 0, 0))],
        out_specs=[row, row, row, row, row, row2, row, row, pl.BlockSpec((1, 128), lambda i: (0, 0)), vec, vec2],
        out_shape=[bf16_rows, f32_rows, bf16_rows, f32_rows, f32_rows, jax.ShapeDtypeStruct((s, 2 * D_MODEL), BF16),
                   bf16_rows, bf16_rows, jax.ShapeDtypeStruct((1, 128), F32), jax.ShapeDtypeStruct((1, D_MODEL), F32),
                   jax.ShapeDtypeStruct((1, 2 * D_MODEL), F32)],
        compiler_params=_params(1),
    )(ya, yb, glog, glog, b_gate, b_gate, x, tgt, fg, wproj)


def _dh_fused(dhg, dms, dmz, dglog, w_int, w_ms, w_mz, w_gl, x, g, dx2):
    s = x.shape[0]
    tm = min(512, s)
    n_k = 8

    def body(dhg_ref, dms_ref, dmz_ref, dgl_ref, whg_ref, wms_ref, wmz_ref, wgl_ref, x_ref, g_ref, dx2_ref,
             dx_ref, dg_ref, acc_ref):
        i, k = pl.program_id(0), pl.program_id(1)

        @pl.when((i == 0) & (k == 0))
        def _():
            dg_ref[...] = jnp.zeros_like(dg_ref)

        @pl.when(k == 0)
        def _():
            acc_ref[...] = jnp.zeros_like(acc_ref)

        @pl.when(k < 4)
        def _():
            acc_ref[...] += _dot(dhg_ref[...], whg_ref[...])

        @pl.when(k == 4)
        def _():
            acc_ref[...] += _dot(dms_ref[...], wms_ref[...])

        @pl.when(k == 5)
        def _():
            acc_ref[...] += _dot(dmz_ref[...], wmz_ref[...])

        @pl.when(k >= 6)
        def _():
            acc_ref[...] += _dot(dgl_ref[...], wgl_ref[...])

        @pl.when(k == n_k - 1)
        def _():
            dx, dg_rows = _rms_bwd(x_ref[...], g_ref[...], acc_ref[...])
            dx_ref[...] = dx + dx2_ref[...]
            dg_ref[...] += jnp.sum(dg_rows, axis=0, keepdims=True)

    def hg_k(k):
        return jnp.minimum(k, 3)

    def gl_k(k):
        return jnp.clip(k - 6, 0, 1)

    row = pl.BlockSpec((tm, D_MODEL), lambda i, k: (i, 0))
    sq = (D_MODEL, D_MODEL)
    return pl.pallas_call(
        body, name="dh_fused", grid=(s // tm, n_k),
        in_specs=[pl.BlockSpec((None, tm, D_MODEL), lambda i, k: (hg_k(k), i, 0)),
                  pl.BlockSpec((tm, MS_COLS), lambda i, k: (i, 0)), row,
                  pl.BlockSpec((tm, D_MODEL), lambda i, k: (i, gl_k(k))),
                  pl.BlockSpec(sq, lambda i, k: (hg_k(k), 0)), pl.BlockSpec((MS_COLS, D_MODEL), lambda i, k: (0, 0)),
                  pl.BlockSpec(sq, lambda i, k: (0, 0)), pl.BlockSpec(sq, lambda i, k: (gl_k(k), 0)),
                  row, pl.BlockSpec((1, D_MODEL), lambda i, k: (0, 0)), row],
        out_specs=[row, pl.BlockSpec((1, D_MODEL), lambda i, k: (0, 0))],
        out_shape=[jax.ShapeDtypeStruct((s, D_MODEL), F32), jax.ShapeDtypeStruct((1, D_MODEL), F32)],
        scratch_shapes=[pltpu.VMEM((tm, D_MODEL), F32)], compiler_params=_params(2),
    )(dhg, dms, dmz, dglog, w_int, w_ms, w_mz, w_gl, x, g, dx2)


def _local_step(x, tgt, w_int, w_uq, w_ukv, wproj, norm_g, b_gate, lb_logits, hg_norm_g, q_a_g, kv_a_g, fg):
    s = x.shape[0]
    w_ms = jnp.concatenate([w_int[4096:4800], jnp.zeros((64, D_MODEL), BF16)], axis=0)
    w_mz = w_int[4800:5824]
    w_gl = w_int[5824:7872]
    wuq3 = jnp.pad(w_uq.reshape(Q_LORA, HEADS, QK_DIM).transpose(1, 0, 2), ((0, 0), (0, 0), (0, QK_PAD - QK_DIM)))
    wukv3 = w_ukv.reshape(KV_LORA, HEADS, 256).transpose(1, 0, 2)
    tabs = _rope_tables(s)

    h = _norm_in(x, norm_g)
    hg = _mm_slabs_out(h, w_int, 4, name="proj_hg")
    ms = _mm(h, w_ms, trans_b=True, name="proj_ms")
    mz = _mm(h, w_mz, trans_b=True, name="proj_mz")
    glog = _mm(h, w_gl, trans_b=True, name="proj_gate")
    o_pre, ya, st0 = _hgrn_fwd(hg, lb_logits, hg_norm_g)
    q, k, v, cqn, ckvn = _mla_pre(ms, q_a_g, kv_a_g, wuq3, wukv3, tabs)
    o_att, yb, lse = _flash_fwd(q, k, v, mz)
    merged, dx2, dx2b, dya, dyb, dglog, dpa, dpb, loss, dfg, dbg = _merge_fused(ya, yb, glog, b_gate, x, tgt, fg, wproj)

    d_wout = _mm_tn(merged, dx2b, name="dw_out")
    d_wpa = _mm_tn(ya, dpa, name="dw_proj_a")
    d_wpb = _mm_tn(yb, dpb, name="dw_proj_b")
    dhg, dlb, dhgg = _hgrn_bwd(hg, o_pre, dya, st0, lb_logits, hg_norm_g)
    dq, dk, dv, dmz = _flash_bwd(q, k, v, dyb, mz, o_att, lse, tabs)
    dms, d_wuq3, d_wukv3, dqg, dkvg = _mla_bwd_proj(dq, dk, dv, cqn, ckvn, ms, q_a_g, kv_a_g, wuq3, wukv3, tabs)
    d_hg = _mm_tn(dhg, h, name="dw_in_hg")
    d_ms = _mm_tn(dms, h, name="dw_in_ms")
    d_mz = _mm_tn(dmz, h, name="dw_in_mz")
    d_gl = _mm_tn(dglog, h, name="dw_in_gate")
    grad_x, dng = _dh_fused(dhg, dms, dmz, dglog, w_int, w_ms, w_mz, w_gl, x, norm_g, dx2)

    d_w_int = jnp.concatenate([d_hg.reshape(4 * D_MODEL, D_MODEL), d_ms[0:704], d_mz, d_gl], axis=0)
    small = {"norm_g": dng, "b_gate": dbg, "lb": dlb, "hg_norm_g": dhgg, "q_a_g": dqg, "kv_a_g": dkvg,
             "final_norm_g": dfg}
    return loss, grad_x, d_w_int, d_wuq3, d_wukv3, (d_wpa, d_wpb, d_wout), small


def _pack_rest(w_uq_b, w_ukv_b, wpa_b, wpb_b, wout_b):
    return jnp.concatenate([w_uq_b.reshape(144, D_MODEL), w_ukv_b.reshape(128, D_MODEL), wpa_b, wpb_b, wout_b], axis=0)


def _unpack_rest(p):
    return (p[0:144].reshape(Q_LORA, 384), p[144:272].reshape(KV_LORA, 512), p[272:528], p[528:784], p[784:1040])


def _pack_rest_grads(d_wuq3, d_wukv3, d_proj):
    d_wuq = d_wuq3.transpose(1, 0, 2)[:, :, 0:QK_DIM].reshape(Q_LORA, HEADS * QK_DIM)
    d_wukv = d_wukv3.transpose(1, 0, 2).reshape(KV_LORA, HEADS * 256)
    blocks = []
    for b in range(N_CHIPS):
        rows = slice(b * 256, (b + 1) * 256)
        blocks.append(_pack_rest(d_wuq[:, b * 384:(b + 1) * 384], d_wukv[:, b * 512:(b + 1) * 512],
                                 d_proj[0][rows], d_proj[1][rows], d_proj[2][rows]))
    return jnp.stack(blocks, axis=0)


def _unpack_rest_weights(g):
    parts = [_unpack_rest(g[b]) for b in range(N_CHIPS)]
    w_uq, w_ukv = (jnp.concatenate([p[n] for p in parts], axis=1) for n in range(2))
    wproj = jnp.stack([jnp.concatenate([p[n] for p in parts], axis=0) for n in range(2, 5)], axis=0)
    return w_uq, w_ukv, wproj


MESH_ID = pl.DeviceIdType.MESH
ANY = pl.BlockSpec(memory_space=pl.ANY)
HALF_COLS = D_MODEL // 2


def _me():
    return lax.axis_index("x"), lax.axis_index("y"), lax.axis_index("c")


def _other_chips(x, y):
    return [(1 - x, y), (x, 1 - y), (1 - x, 1 - y)]


def _cols(c):
    return pl.ds(c * HALF_COLS, HALF_COLS)


def _gather_weights(w_blk, r_blk):
    def body(w_ref, r_ref, ow_ref, or_ref, send_sems, recv_sems):
        x, y, c = _me()
        chips = _other_chips(x, y)
        me = 2 * x + y
        pairs = [(w_ref, ow_ref), (r_ref, or_ref)]

        def copy(k, src, dst, to):
            return pltpu.make_async_remote_copy(src_ref=src, dst_ref=dst, send_sem=send_sems.at[k],
                                                recv_sem=recv_sems.at[k], device_id=to, device_id_type=MESH_ID)

        first =[copy(6 * a + j, src.at[:, _cols(c)], dst.at[me, :, _cols(c)], (cx, cy, c))
                 for a, (src, dst) in enumerate(pairs) for j, (cx, cy) in enumerate(chips)]
        for cp in first:
            cp.start()
        passed = []
        for a, (src, dst) in enumerate(pairs):
            for j, (cx, cy) in enumerate(chips):
                landed = dst.at[2 * cx + cy, :, _cols(c)]
                copy(6 * a + j, landed, landed, (cx, cy, c)).wait_recv()
                fwd = copy(6 * a + 3 + j, landed, landed, (x, y, 1 - c))
                fwd.start()
                passed.append(fwd)
        for a, (src, dst) in enumerate(pairs):
            for j, (cx, cy) in enumerate(chips):
                theirs = dst.at[2 * cx + cy, :, _cols(1 - c)]
                copy(6 * a + 3 + j, theirs, theirs, (x, y, 1 - c)).wait_recv()
        for cp in first + passed:
            cp.wait_send()

    gw, gr = pl.pallas_call(
        body, name="gather_weights", in_specs=[ANY, ANY], out_specs=[ANY, ANY],
        out_shape=[jax.ShapeDtypeStruct((N_CHIPS,) + w_blk.shape, w_blk.dtype),
                   jax.ShapeDtypeStruct((N_CHIPS,) + r_blk.shape, r_blk.dtype)],
        scratch_shapes=[pltpu.SemaphoreType.DMA((12,)), pltpu.SemaphoreType.DMA((12,))],
    )(w_blk, r_blk)
    chip = 2 * lax.axis_index("x") + lax.axis_index("y")
    return (lax.dynamic_update_slice(gw, w_blk[None], (chip, 0, 0)),
            lax.dynamic_update_slice(gr, r_blk[None], (chip, 0, 0)))


def _swap_halves(gw, gr):
    def body(gw_ref, gr_ref, lw_ref, lr_ref, send_sems, recv_sems):
        x, y, c = _me()
        cps = [pltpu.make_async_remote_copy(
            src_ref=src, dst_ref=dst, send_sem=send_sems.at[a], recv_sem=recv_sems.at[a],
            device_id=(x, y, 1 - c), device_id_type=MESH_ID)
            for a, (src, dst) in enumerate([(gw_ref.at[:, _cols(1 - c)], lw_ref),
                                            (gr_ref.at[:, :, _cols(1 - c)], lr_ref)])]
        for cp in cps:
            cp.start()
        for cp in cps:
            cp.wait()

    return pl.pallas_call(
        body, name="grad_swap_halves", in_specs=[ANY, ANY], out_specs=[ANY, ANY],
        out_shape=[jax.ShapeDtypeStruct((gw.shape[0], HALF_COLS), gw.dtype),
                   jax.ShapeDtypeStruct(gr.shape[:2] + (HALF_COLS,), gr.dtype)],
        scratch_shapes=[pltpu.SemaphoreType.DMA((2,)), pltpu.SemaphoreType.DMA((2,))],
    )(gw, gr)


def _scatter_blocks(hw, hr):
    nw, nr = hw.shape[0] // N_CHIPS, hr.shape[0] // N_CHIPS

    def body(hw_ref, hr_ref, lw_ref, lr_ref, send_sems, recv_sems):
        x, y, c = _me()
        cps = []
        for a, (src, dst, n) in enumerate([(hw_ref, lw_ref, nw), (hr_ref, lr_ref, nr)]):
            for j, (cx, cy) in enumerate(_other_chips(x, y)):
                cps.append(pltpu.make_async_remote_copy(
                    src_ref=src.at[pl.ds((2 * cx + cy) * n, n), :], dst_ref=dst.at[j], send_sem=send_sems.at[3 * a + j],
                    recv_sem=recv_sems.at[3 * a + j], device_id=(cx, cy, c), device_id_type=MESH_ID))
        for cp in cps:
            cp.start()
        for cp in cps:
            cp.wait()

    return pl.pallas_call(
        body, name="grad_scatter_blocks", in_specs=[ANY, ANY], out_specs=[ANY, ANY],
        out_shape=[jax.ShapeDtypeStruct((3, nw, HALF_COLS), hw.dtype), jax.ShapeDtypeStruct((3, nr, HALF_COLS), hr.dtype)],
        scratch_shapes=[pltpu.SemaphoreType.DMA((6,)), pltpu.SemaphoreType.DMA((6,))],
    )(hw, hr)


def _swap_reduced(rw, rr):
    def body(rw_ref, rr_ref, ow_ref, or_ref, send_sems, recv_sems):
        x, y, c = _me()
        cps = [pltpu.make_async_remote_copy(
            src_ref=src, dst_ref=dst, send_sem=send_sems.at[a], recv_sem=recv_sems.at[a],
            device_id=(x, y, 1 - c), device_id_type=MESH_ID)
            for a, (src, dst) in enumerate([(rw_ref, ow_ref), (rr_ref, or_ref)])]
        for cp in cps:
            cp.start()
        for cp in cps:
            cp.wait()

    return pl.pallas_call(
        body, name="grad_swap_reduced", in_specs=[ANY, ANY], out_specs=[ANY, ANY],
        out_shape=[jax.ShapeDtypeStruct(rw.shape, rw.dtype), jax.ShapeDtypeStruct(rr.shape, rr.dtype)],
        scratch_shapes=[pltpu.SemaphoreType.DMA((2,)), pltpu.SemaphoreType.DMA((2,))],
    )(rw, rr)


def _join_cols(mine, theirs):
    first = lax.axis_index("c") == 0
    return jnp.concatenate([jnp.where(first, mine, theirs), jnp.where(first, theirs, mine)], axis=1)


def _gather_small(vec):
    def body(v_ref, out_ref, send_sems, recv_sems, local_sem):
        x, y, c = _me()
        my_id = 4 * x + 2 * y + c
        mine = pltpu.make_async_copy(v_ref, out_ref.at[my_id], local_sem)
        mine.start()
        cps = []
        for r in range(1, N_DEV):
            peer = (x ^ (r >> 2), y ^ ((r >> 1) & 1), c ^ (r & 1))
            cps.append(pltpu.make_async_remote_copy(
                src_ref=v_ref, dst_ref=out_ref.at[my_id], send_sem=send_sems.at[r - 1],
                recv_sem=recv_sems.at[r - 1], device_id=peer, device_id_type=MESH_ID))
        for cp in cps:
            cp.start()
        for cp in cps:
            cp.wait()
        mine.wait()

    return pl.pallas_call(
        body, name="gather_small", in_specs=[ANY], out_specs=ANY,
        out_shape=jax.ShapeDtypeStruct((N_DEV, 1, SMALL_COLS), vec.dtype),
        scratch_shapes=[pltpu.SemaphoreType.DMA((N_DEV - 1,)), pltpu.SemaphoreType.DMA((N_DEV - 1,)),
                        pltpu.SemaphoreType.DMA],
    )(vec)


def _add_cores(c_idx, g, landed, *, tm, name):
    r = g.shape[0]

    def body(c_ref, g_ref, l_ref, o32_ref, o16_ref):
        acc = g_ref[...] + l_ref[...]
        o32_ref[...] = acc
        o16_ref[...] = acc.astype(BF16)

    half = pl.BlockSpec((tm, HALF_COLS), lambda i, c_ref: (i, 0))
    grid_spec = pltpu.PrefetchScalarGridSpec(
        num_scalar_prefetch=1, grid=(r // tm,),
        in_specs=[pl.BlockSpec((tm, HALF_COLS), lambda i, c_ref: (i, c_ref[0])), half], out_specs=[half, half])
    return pl.pallas_call(
        body, name=name, grid_spec=grid_spec,
        out_shape=[jax.ShapeDtypeStruct((r, HALF_COLS), F32), jax.ShapeDtypeStruct((r, HALF_COLS), BF16)],
        compiler_params=_params(1),
    )(c_idx, g, landed)


def _add_chips(chip_idx, h32, landed, *, tm, name):
    n = landed.shape[1]
    per = n // tm

    def body(chip_ref, h_ref, l_ref, o_ref):
        acc = h_ref[...]
        for j in range(3):
            acc = acc + l_ref[j].astype(F32)
        o_ref[...] = acc

    grid_spec = pltpu.PrefetchScalarGridSpec(
        num_scalar_prefetch=1, grid=(per,),
        in_specs=[pl.BlockSpec((tm, HALF_COLS), lambda i, chip_ref: (chip_ref[0] * per + i, 0)),
                  pl.BlockSpec((3, tm, HALF_COLS), lambda i, chip_ref: (0, i, 0))],
        out_specs=pl.BlockSpec((tm, HALF_COLS), lambda i, chip_ref: (i, 0)))
    return pl.pallas_call(
        body, name=name, grid_spec=grid_spec, out_shape=jax.ShapeDtypeStruct((n, HALF_COLS), F32),
        compiler_params=_params(1),
    )(chip_idx, h32, landed)


def _pack_small(small, lb_logits, loss):
    def body(ng_ref, bg_ref, dlb_ref, lbl_ref, hgg_ref, qg_ref, kvg_ref, fg_ref, loss_ref, out_ref):
        out_ref[...] = jnp.zeros_like(out_ref)
        out_ref[:, 0:1024] = ng_ref[...]
        out_ref[:, 1024:3072] = bg_ref[...]
        _, p0p1 = _lower_bound(lbl_ref[...])
        dl0 = dlb_ref[...] * p0p1
        out_ref[:, 3072:4096] = dl0
        out_ref[:, 4096:5120] = -dl0
        hgg = hgg_ref[0]
        for h in range(1, HEADS):
            hgg = hgg + hgg_ref[h]
        out_ref[:, 5120:5248] = hgg
        out_ref[:, 5248:5632] = qg_ref[...]
        out_ref[:, 5632:5888] = kvg_ref[...]
        out_ref[:, 5888:6912] = fg_ref[...]
        out_ref[:, 6912:7040] = loss_ref[...]

    return pl.pallas_call(
        body, name="pack_small", out_shape=jax.ShapeDtypeStruct((1, SMALL_COLS), F32),
    )(small["norm_g"], small["b_gate"], small["lb"], lb_logits, small["hg_norm_g"], small["q_a_g"],
      small["kv_a_g"], small["final_norm_g"], loss)


def _adamw_math(w, g, m, v):
    nm = ADAM_B1 * m + (1.0 - ADAM_B1) * g
    nv = ADAM_B2 * v + (1.0 - ADAM_B2) * (g * g)
    m_hat = nm / (1.0 - ADAM_B1 ** ADAM_STEP)
    v_hat = nv / (1.0 - ADAM_B2 ** ADAM_STEP)
    return -ADAM_LR * (m_hat / (jnp.sqrt(v_hat) + ADAM_EPS) + ADAM_WD * w), nm, nv


def _adamw(w, g, m, v, *, name, tm):
    r, cols = w.shape

    def body(w_ref, g_ref, m_ref, v_ref, d_ref, nm_ref, nv_ref):
        d_ref[...], nm_ref[...], nv_ref[...] = _adamw_math(w_ref[...], g_ref[...], m_ref[...], v_ref[...])

    row = pl.BlockSpec((tm, cols), lambda i: (i, 0))
    shp = jax.ShapeDtypeStruct((r, cols), F32)
    return pl.pallas_call(
        body, name=name, grid=(r // tm,), in_specs=[row] * 4, out_specs=[row] * 3, out_shape=[shp] * 3,
        compiler_params=_params(1),
    )(w, g, m, v)


SMALL_SLOTS = (("norm_g", (0,)), ("b_gate", (1024,)), ("lb_logits", (3072, 4096)), ("hg_norm_g", (5120,)),
               ("q_a_g", (5248,)), ("kv_a_g", (5632,)), ("final_norm_g", (5888,)))
LOSS_SLOT = 6912


def _small_update(gathered, ws, ms, vs):
    n = len(SMALL_SLOTS)

    def body(*refs):
        g_ref = refs[0]
        w_refs, m_refs, v_refs = refs[1:1 + n], refs[1 + n:1 + 2 * n], refs[1 + 2 * n:1 + 3 * n]
        outs = refs[1 + 3 * n:]
        loss_ref = outs[0]
        g_out, d_out, nm_out, nv_out = (outs[1 + k * n:1 + (k + 1) * n] for k in range(4))
        total = g_ref[0]
        for dev in range(1, N_DEV):
            total = total + g_ref[dev]
        loss_ref[...] = total[:, LOSS_SLOT:LOSS_SLOT + 128]
        for p, (_, offsets) in enumerate(SMALL_SLOTS):
            cols = w_refs[p].shape[1]
            for r, off in enumerate(offsets):
                rows = slice(r, r + 1)
                g = total[:, off:off + cols]
                g_out[p][rows, :] = g
                d_out[p][rows, :], nm_out[p][rows, :], nv_out[p][rows, :] = _adamw_math(
                    w_refs[p][rows, :], g, m_refs[p][rows, :], v_refs[p][rows, :])

    shapes = [jax.ShapeDtypeStruct(w.shape, F32) for w in ws]
    res = pl.pallas_call(
        body, name="small_update", out_shape=[jax.ShapeDtypeStruct((1, 128), F32)] + shapes * 4,
    )(gathered, *ws, *ms, *vs)
    return res[0], res[1:1 + n], res[1 + n:1 + 2 * n], res[1 + 2 * n:1 + 3 * n], res[1 + 3 * n:1 + 4 * n]


def kernel(x, norm_g, w_in, b_gate, lb_logits, hg_norm_g, q_a_g, w_uq, kv_a_g, w_ukv, w_proj_a, w_proj_b, w_out, final_norm_g, loss_target, m_norm_g, m_w_in, m_b_gate, m_lb_logits, m_hg_norm_g, m_q_a_g, m_w_uq, m_kv_a_g, m_w_ukv, m_w_proj_a, m_w_proj_b, m_w_out, m_final_norm_g, v_norm_g, v_w_in, v_b_gate, v_lb_logits, v_hg_norm_g, v_q_a_g, v_w_uq, v_kv_a_g, v_w_ukv, v_w_proj_a, v_w_proj_b, v_w_out, v_final_norm_g):
    c_idx = lax.axis_index("c").astype(jnp.int32).reshape(1)
    chip_idx = (2 * lax.axis_index("x") + lax.axis_index("y")).astype(jnp.int32).reshape(1)

    w_blk = w_in[0].T.astype(BF16)
    r_blk = _pack_rest(w_uq[0], w_ukv[0], w_proj_a[0], w_proj_b[0], w_out[0]).astype(BF16)
    gw, gr = _gather_weights(w_blk, r_blk)
    fw_uq, fw_ukv, fwproj = _unpack_rest_weights(gr)

    loss, grad_x, d_w_int, d_wuq3, d_wukv3, d_proj, small = _local_step(
        x[0], loss_target[0], gw.reshape(W_IN_COLS, D_MODEL), fw_uq, fw_ukv, fwproj,
        norm_g, b_gate, lb_logits, hg_norm_g, q_a_g, kv_a_g, final_norm_g.reshape(1, D_MODEL))

    d_rest = _pack_rest_grads(d_wuq3, d_wukv3, d_proj)
    lw, lr = _swap_halves(d_w_int, d_rest)
    hw32, hw16 = _add_cores(c_idx, d_w_int, lw, tm=656, name="grad_add_cores_w")
    hr32, hr16 = _add_cores(c_idx, d_rest.reshape(N_CHIPS * REST_ROWS, D_MODEL), lr.reshape(N_CHIPS * REST_ROWS, HALF_COLS),
                            tm=REST_ROWS, name="grad_add_cores_r")
    landed_w, landed_r = _scatter_blocks(hw16, hr16)
    rw = _add_chips(chip_idx, hw32, landed_w, tm=656, name="grad_add_chips_w")
    rr = _add_chips(chip_idx, hr32, landed_r, tm=208, name="grad_add_chips_r")
    tw, tr = _swap_reduced(rw, rr)
    g_w_in = _join_cols(rw, tw).T
    g_rest = _join_cols(rr, tr)
    g_uq, g_ukv, g_pa, g_pb, g_out = _unpack_rest(g_rest)

    small_all = _gather_small(_pack_small(small, lb_logits, loss))

    upd = {
        "w_in": _adamw(w_in[0], g_w_in, m_w_in[0], v_w_in[0], name="adamw_w_in", tm=128),
        "w_uq": _adamw(w_uq[0], g_uq, m_w_uq[0], v_w_uq[0], name="adamw_w_uq", tm=Q_LORA),
        "w_ukv": _adamw(w_ukv[0], g_ukv, m_w_ukv[0], v_w_ukv[0], name="adamw_w_ukv", tm=KV_LORA),
        "w_proj_a": _adamw(w_proj_a[0], g_pa, m_w_proj_a[0], v_w_proj_a[0], name="adamw_w_proj_a", tm=256),
        "w_proj_b": _adamw(w_proj_b[0], g_pb, m_w_proj_b[0], v_w_proj_b[0], name="adamw_w_proj_b", tm=256),
        "w_out": _adamw(w_out[0], g_out, m_w_out[0], v_w_out[0], name="adamw_w_out", tm=256),
    }
    loss_vec, *small_sets = _small_update(
        small_all,
        [norm_g, b_gate, lb_logits, hg_norm_g, q_a_g, kv_a_g, final_norm_g.reshape(1, D_MODEL)],
        [m_norm_g, m_b_gate, m_lb_logits, m_hg_norm_g, m_q_a_g, m_kv_a_g, m_final_norm_g.reshape(1, D_MODEL)],
        [v_norm_g, v_b_gate, v_lb_logits, v_hg_norm_g, v_q_a_g, v_kv_a_g, v_final_norm_g.reshape(1, D_MODEL)])

    def outputs(big, small_set):
        s_ng, s_bg, s_lb, s_hg, s_qg, s_kvg, s_fg = small_set
        return (s_ng, big["w_in"][None], s_bg, s_lb, s_hg, s_qg, big["w_uq"][None], s_kvg, big["w_ukv"][None],
                big["w_proj_a"][None], big["w_proj_b"][None], big["w_out"][None], s_fg.reshape(D_MODEL))

    grads = {"w_in": g_w_in, "w_uq": g_uq, "w_ukv": g_ukv, "w_proj_a": g_pa, "w_proj_b": g_pb, "w_out": g_out}
    return (loss_vec[0, 0], grad_x[None], *outputs(grads, small_sets[0]),
            *(o for k in range(3) for o in outputs({n: u[k] for n, u in upd.items()}, small_sets[1 + k])))
```

```python
import functools

import jax
import jax.numpy as jnp
from jax import lax
from jax.experimental import pallas as pl
from jax.experimental.pallas import tpu as pltpu

F32 = jnp.float32
BF16 = jnp.bfloat16

D_MODEL = 1024
HEADS = 8
HEAD_DIM = 128
HG_CHUNK = 32
CHUNK_SHIFT = 5
HEAD_SHIFT = 7
QK_NOPE = 128
QK_ROPE = 64
QK_DIM = QK_NOPE + QK_ROPE
QK_PAD = 256
Q_LORA = 384
KV_LORA = 256
MS_COLS = 768
ROPE_THETA = 10000.0
EPS = 1e-6
ATT_SCALE = QK_DIM ** -0.5
LOG2E = 1.4426950408889634
LN2 = 0.6931471805599453
Q_PRESCALE = ATT_SCALE * LOG2E

ADAM_LR = 0.001
ADAM_B1 = 0.9
ADAM_B2 = 0.999
ADAM_EPS = 1e-08
ADAM_WD = 0.01
ADAM_STEP = 10

N_CHIPS = 4
N_DEV = 8
W_IN_COLS = 7872
W_IN_BLK = W_IN_COLS // N_CHIPS
REST_ROWS = 144 + 128 + 3 * 256
SMALL_COLS = 7168

TM_MM = 1024
TM_FUSED = 256
HG_ROWS = 256
TQ = 512
FLASH_HEADS = 2
HG_HEADS = 2
VMEM_LIMIT = 56 * 1024 * 1024


def _dot(a, b):
    return lax.dot_general(a, b, (((1,), (0,)), ((), ())), preferred_element_type=F32)


def _dot_nt(a, b):
    return lax.dot_general(a, b, (((1,), (1,)), ((), ())), preferred_element_type=F32)


def _dot_tn(a, b):
    return lax.dot_general(a, b, (((0,), (0,)), ((), ())), preferred_element_type=F32)


def _params(n_axes):
    return pltpu.CompilerParams(dimension_semantics=("arbitrary",) * n_axes, vmem_limit_bytes=VMEM_LIMIT)


def _rms(x, g):
    r = lax.rsqrt(jnp.mean(x * x, axis=-1, keepdims=True) + EPS)
    return x * r * g


def _rms_bwd(x, g, dy):
    r = lax.rsqrt(jnp.mean(x * x, axis=-1, keepdims=True) + EPS)
    xh = x * r
    dyg = dy * g
    dx = r * (dyg - xh * jnp.mean(dyg * xh, axis=-1, keepdims=True))
    return dx, dy * xh


def _silu_parts(z):
    s = jax.nn.sigmoid(z)
    return z * s, s * (1.0 + z * (1.0 - s))


def _rope(x, c, sa, sb):
    return x * c + pltpu.roll(x, 32, 1) * sa + pltpu.roll(x, 96, 1) * sb


def _rope_bwd(dy, c, sa, sb):
    return dy * c + pltpu.roll(dy * sa, 96, 1) + pltpu.roll(dy * sb, 32, 1)


def _rope_tables(seq):
    inv = ROPE_THETA ** (-jnp.arange(0, QK_ROPE, 2, dtype=F32) / QK_ROPE)
    ang = jnp.arange(seq, dtype=F32)[:, None] * inv[None, :]
    cos, sin = jnp.cos(ang), jnp.sin(ang)
    z32 = jnp.zeros_like(cos)
    z64 = jnp.zeros((seq, 64), F32)
    c = jnp.concatenate([cos, cos, z64], axis=1)
    sa = jnp.concatenate([z32, sin, z64], axis=1)
    sb = jnp.concatenate([-sin, z32, z64], axis=1)
    return c, sa, sb


def _mm_tn(a, b, *, name, tm=TM_MM, tn=1024):
    flat = a.ndim == 2
    if flat:
        a = a[None]
    g, m, k = a.shape
    n = b.shape[1]
    tm, tn = min(tm, m), min(tn, n)
    assert m % tm == 0 and n % tn == 0

    def body(a_ref, b_ref, o_ref):
        @pl.when(pl.program_id(2) == 0)
        def _():
            o_ref[...] = jnp.zeros_like(o_ref)

        o_ref[...] += _dot_tn(a_ref[...], b_ref[...])

    out = pl.pallas_call(
        body, name=name, grid=(g, n // tn, m // tm),
        in_specs=[pl.BlockSpec((None, tm, k), lambda s, j, i: (s, i, 0)),
                  pl.BlockSpec((tm, tn), lambda s, j, i: (i, j))],
        out_specs=pl.BlockSpec((None, k, tn), lambda s, j, i: (s, 0, j)),
        out_shape=jax.ShapeDtypeStruct((g, k, n), F32), compiler_params=_params(3),
    )(a, b)
    return out[0] if flat else out


def _chunk_rows(rows):
    return lax.broadcasted_iota(jnp.int32, (rows, HEAD_DIM), 0) & (HG_CHUNK - 1)


def _chunk_cumsum(x, rows):
    pos = _chunk_rows(rows)
    shift = 1
    while shift < HG_CHUNK:
        x = x + jnp.where(pos >= shift, pltpu.roll(x, shift, 0), 0.0)
        shift *= 2
    return x


def _chunk_revcumsum(x, rows):
    pos = _chunk_rows(rows)
    shift = 1
    while shift < HG_CHUNK:
        x = x + jnp.where(pos + shift < HG_CHUNK, pltpu.roll(x, rows - shift, 0), 0.0)
        shift *= 2
    return x


def _lower_bound(lbl):
    mx = jnp.maximum(lbl[0:1, :], lbl[1:2, :])
    e0 = jnp.exp(lbl[0:1, :] - mx)
    e1 = jnp.exp(lbl[1:2, :] - mx)
    p0 = e0 / (e0 + e1)
    return p0, p0 * (e1 / (e0 + e1))


def _hg_masks(rows, nch, tmask_s, bdmask_s):
    r = lax.broadcasted_iota(jnp.int32, (rows, rows), 0)
    c = lax.broadcasted_iota(jnp.int32, (rows, rows), 1)
    tmask_s[...] = jnp.where(((r >> CHUNK_SHIFT) == (c >> CHUNK_SHIFT)) & (r >= c), 1.0, 0.0)
    r = lax.broadcasted_iota(jnp.int32, (rows, nch * HEAD_DIM), 0)
    c = lax.broadcasted_iota(jnp.int32, (rows, nch * HEAD_DIM), 1)
    bdmask_s[...] = jnp.where((r >> CHUNK_SHIFT) == (c >> HEAD_SHIFT), 1.0, 0.0).astype(BF16)


def _block_diag(x, nch, bdmask):
    return jnp.tile(x, (1, nch)) * bdmask


def _hgrn_fwd(hg, lb_logits, norm_g):
    s = hg.shape[1]
    rows = min(HG_ROWS, s)
    nblk = s // rows
    nch = rows // HG_CHUNK

    def body(hg_ref, lbl_ref, g_ref, o_ref, ya_ref, st0_ref, st_s, stall_s, tmask_s, bdmask_s):
        @pl.when(pl.program_id(1) == 0)
        def _():
            st_s[...] = jnp.zeros_like(st_s)
            _hg_masks(rows, nch, tmask_s, bdmask_s)

        bdmask = bdmask_s[...]
        tmask = tmask_s[...] > 0.5
        for hh in range(HG_HEADS):
            hc = slice(hh * HEAD_DIM, (hh + 1) * HEAD_DIM)
            hq = hg_ref[0, :, hc]
            hf = hg_ref[1, :, hc]
            hi = hg_ref[2, :, hc]
            hz = hg_ref[3, :, hc]
            lb, _ = _lower_bound(lbl_ref[:, hc])
            f = lb + (1.0 - lb) * jax.nn.sigmoid(hf)
            q = hq * jax.nn.sigmoid(hq)
            k = 1.0 - f
            logf = jnp.log(f)
            b = _chunk_cumsum(logf, rows)
            q_in = (q * jnp.exp(b)).astype(BF16)
            k_in = (k * jnp.exp(-b)).astype(BF16)
            k_out = (k * jnp.exp(_chunk_revcumsum(logf, rows) - logf)).astype(BF16)
            vb = hi.astype(BF16)

            sc = jnp.where(tmask, _dot_nt(q_in, k_in), 0.0)
            o_intra = _dot(sc.astype(BF16), vb)
            kvt = _dot_tn(vb, _block_diag(k_out, nch, bdmask))
            st = st_s[hh]
            st0_ref[hh] = st
            for c in range(nch):
                cols = slice(c * HEAD_DIM, (c + 1) * HEAD_DIM)
                last = (c + 1) * HG_CHUNK - 1
                stall_s[hh, :, cols] = st.astype(BF16)
                st = st * jnp.exp(b[last:last + 1, :]) + kvt[:, cols]
            st_s[hh] = st
            o = o_intra + _dot_nt(_block_diag(q_in, nch, bdmask), stall_s[hh])
            o_ref[:, hc] = o
            silu_z, _ = _silu_parts(hz)
            ya_ref[:, hc] = (_rms(o, g_ref[...]) * silu_z).astype(BF16)

    nh = HG_HEADS
    return pl.pallas_call(
        body, name="hgrn_fwd", grid=(HEADS // nh, nblk),
        in_specs=[pl.BlockSpec((4, rows, nh * HEAD_DIM), lambda h, i: (0, i, h)),
                  pl.BlockSpec((2, nh * HEAD_DIM), lambda h, i: (0, h)),
                  pl.BlockSpec((1, HEAD_DIM), lambda h, i: (0, 0))],
        out_specs=[pl.BlockSpec((rows, nh * HEAD_DIM), lambda h, i: (i, h)),
                   pl.BlockSpec((rows, nh * HEAD_DIM), lambda h, i: (i, h)),
                   pl.BlockSpec((nh, None, HEAD_DIM, HEAD_DIM), lambda h, i: (h, i, 0, 0))],
        out_shape=[jax.ShapeDtypeStruct((s, D_MODEL), F32), jax.ShapeDtypeStruct((s, D_MODEL), BF16),
                   jax.ShapeDtypeStruct((HEADS, nblk, HEAD_DIM, HEAD_DIM), F32)],
        scratch_shapes=[pltpu.VMEM((nh, HEAD_DIM, HEAD_DIM), F32), pltpu.VMEM((nh, HEAD_DIM, nch * HEAD_DIM), BF16),
                        pltpu.VMEM((rows, rows), F32), pltpu.VMEM((rows, nch * HEAD_DIM), BF16)],
        compiler_params=_params(2),
    )(hg, lb_logits, norm_g)


def _hgrn_bwd(hg, o_pre, dya, st0, lb_logits, norm_g):
    s = hg.shape[1]
    rows = min(HG_ROWS, s)
    nblk = s // rows
    nch = rows // HG_CHUNK

    def body(hg_ref, o_ref, dya_ref, st0_ref, lbl_ref, g_ref, dhg_ref, dlb_ref, dg_ref,
             dst_s, stp_s, stp_rows_s, dst_rows_s, dst_lane_s, dbl_s, tmask_s, bdmask_s):
        @pl.when(pl.program_id(1) == 0)
        def _():
            dst_s[...] = jnp.zeros_like(dst_s)
            dlb_ref[...] = jnp.zeros_like(dlb_ref)
            dg_ref[...] = jnp.zeros_like(dg_ref)
            _hg_masks(rows, nch, tmask_s, bdmask_s)

        bdmask = bdmask_s[...]
        tmask = tmask_s[...] > 0.5
        g = g_ref[...]
        for hh in range(HG_HEADS):
            hc = slice(hh * HEAD_DIM, (hh + 1) * HEAD_DIM)
            hq = hg_ref[0, :, hc]
            hf = hg_ref[1, :, hc]
            hi = hg_ref[2, :, hc]
            hz = hg_ref[3, :, hc]
            lb, _ = _lower_bound(lbl_ref[:, hc])
            sg = jax.nn.sigmoid(hf)
            f = lb + (1.0 - lb) * sg
            q, dsilu_q = _silu_parts(hq)
            k = 1.0 - f
            logf = jnp.log(f)
            b = _chunk_cumsum(logf, rows)
            eb = jnp.exp(b)
            enb = jnp.exp(-b)
            ebl = jnp.exp(_chunk_revcumsum(logf, rows) - logf)
            q_in32 = q * eb
            k_in32 = k * enb
            k_out32 = k * ebl
            q_in = q_in32.astype(BF16)
            k_in = k_in32.astype(BF16)
            k_out = k_out32.astype(BF16)
            vb = hi.astype(BF16)
            kbd = _block_diag(k_out, nch, bdmask)
            qbd = _block_diag(q_in, nch, bdmask)
            decs = [jnp.exp(b[(c + 1) * HG_CHUNK - 1:(c + 1) * HG_CHUNK, :]) for c in range(nch)]

            kvt = _dot_tn(vb, kbd)
            st = st0_ref[hh]
            for c in range(nch):
                stp_s[hh, c] = st
                stp_rows_s[hh, c * HEAD_DIM:(c + 1) * HEAD_DIM, :] = st.astype(BF16)
                st = st * decs[c] + kvt[:, c * HEAD_DIM:(c + 1) * HEAD_DIM]

            o = o_ref[:, hc]
            rstd = lax.rsqrt(jnp.mean(o * o, axis=-1, keepdims=True) + EPS)
            oh = o * rstd
            silu_z, dsilu_z = _silu_parts(hz)
            dya_v = dya_ref[:, hc]
            dn = dya_v * silu_z
            dhz = dya_v * (oh * g) * dsilu_z
            dg_ref[hh] += jnp.sum(dn * oh, axis=0, keepdims=True)
            doh = dn * g
            do = (rstd * (doh - oh * jnp.mean(doh * oh, axis=-1, keepdims=True))).astype(BF16)

            dq_all = _dot_tn(do, qbd)
            dst = dst_s[hh]
            ddecs = [None] * nch
            for c in reversed(range(nch)):
                dstb = dst.astype(BF16)
                dst_lane_s[hh, :, c * HEAD_DIM:(c + 1) * HEAD_DIM] = dstb
                dst_rows_s[hh, c * HEAD_DIM:(c + 1) * HEAD_DIM, :] = dstb
                ddecs[c] = jnp.sum(dst * stp_s[hh, c], axis=0, keepdims=True) * decs[c]
                dst = dst * decs[c] + dq_all[:, c * HEAD_DIM:(c + 1) * HEAD_DIM]
            dst_s[hh] = dst

            sc = jnp.where(tmask, _dot_nt(q_in, k_in), 0.0).astype(BF16)
            dkout = _dot(_block_diag(vb, nch, bdmask), dst_rows_s[hh])
            dv = _dot_nt(kbd, dst_lane_s[hh]) + _dot_tn(sc, do)
            dsc = jnp.where(tmask, _dot_nt(do, vb), 0.0).astype(BF16)
            dqin = _dot(dsc, k_in) + _dot(_block_diag(do, nch, bdmask), stp_rows_s[hh])
            dkin = _dot_tn(dsc, q_in)

            dko = dkout * k_out32
            for c in range(nch):
                sl = slice(c * HG_CHUNK, (c + 1) * HG_CHUNK)
                dbl = jnp.sum(dko[sl], axis=0, keepdims=True) + ddecs[c]
                dbl_s[hh, sl, :] = jnp.broadcast_to(dbl, (HG_CHUNK, HEAD_DIM))
            dq = dqin * eb
            dk = dkin * enb + dkout * ebl
            db = dqin * q_in32 - dkin * k_in32 - dko
            dlogf = _chunk_revcumsum(db, rows) + dbl_s[hh]
            df = dlogf / f - dk
            dlb_ref[:, hc] += jnp.sum(df * (1.0 - sg), axis=0, keepdims=True)
            dhg_ref[0, :, hc] = (dq * dsilu_q).astype(BF16)
            dhg_ref[1, :, hc] = (df * (1.0 - lb) * sg * (1.0 - sg)).astype(BF16)
            dhg_ref[2, :, hc] = dv.astype(BF16)
            dhg_ref[3, :, hc] = dhz.astype(BF16)

    last = nblk - 1
    nh = HG_HEADS
    wide = nh * HEAD_DIM
    return pl.pallas_call(
        body, name="hgrn_bwd", grid=(HEADS // nh, nblk),
        in_specs=[pl.BlockSpec((4, rows, wide), lambda h, i: (0, last - i, h)),
                  pl.BlockSpec((rows, wide), lambda h, i: (last - i, h)),
                  pl.BlockSpec((rows, wide), lambda h, i: (last - i, h)),
                  pl.BlockSpec((nh, None, HEAD_DIM, HEAD_DIM), lambda h, i: (h, last - i, 0, 0)),
                  pl.BlockSpec((2, wide), lambda h, i: (0, h)),
                  pl.BlockSpec((1, HEAD_DIM), lambda h, i: (0, 0))],
        out_specs=[pl.BlockSpec((4, rows, wide), lambda h, i: (0, last - i, h)),
                   pl.BlockSpec((1, wide), lambda h, i: (0, h)),
                   pl.BlockSpec((nh, 1, HEAD_DIM), lambda h, i: (h, 0, 0))],
        out_shape=[jax.ShapeDtypeStruct((4, s, D_MODEL), BF16), jax.ShapeDtypeStruct((1, D_MODEL), F32),
                   jax.ShapeDtypeStruct((HEADS, 1, HEAD_DIM), F32)],
        scratch_shapes=[pltpu.VMEM((nh, HEAD_DIM, HEAD_DIM), F32), pltpu.VMEM((nh, nch, HEAD_DIM, HEAD_DIM), F32),
                        pltpu.VMEM((nh, nch * HEAD_DIM, HEAD_DIM), BF16), pltpu.VMEM((nh, nch * HEAD_DIM, HEAD_DIM), BF16),
                        pltpu.VMEM((nh, HEAD_DIM, nch * HEAD_DIM), BF16), pltpu.VMEM((nh, rows, HEAD_DIM), F32),
                        pltpu.VMEM((rows, rows), F32), pltpu.VMEM((rows, nch * HEAD_DIM), BF16)],
        compiler_params=_params(2),
    )(hg, o_pre, dya, st0, lb_logits, norm_g)


def _mla_pre(ms, q_a_g, kv_a_g, wuq3, wukv3, tabs):
    s = ms.shape[0]
    tm = min(TM_FUSED, s)

    def body(ms_ref, qg_ref, kvg_ref, wuq_ref, wukv_ref, c_ref, sa_ref, sb_ref,
             q_ref, k_ref, v_ref, cqn_ref, ckvn_ref):
        c, sa, sb = c_ref[...], sa_ref[...], sb_ref[...]
        cqn = _rms(ms_ref[:, 0:Q_LORA], qg_ref[...]).astype(BF16)
        ckvn = _rms(ms_ref[:, Q_LORA:Q_LORA + KV_LORA], kvg_ref[...]).astype(BF16)
        cqn_ref[...] = cqn
        ckvn_ref[...] = ckvn
        k_pe = _rope(ms_ref[:, Q_LORA + KV_LORA:MS_COLS], c, sa, sb).astype(BF16)
        for h in range(HEADS):
            qh = _dot(cqn, wuq_ref[h])
            q_ref[h, :, 0:128] = (qh[:, 0:128] * Q_PRESCALE).astype(BF16)
            q_ref[h, :, 128:256] = (_rope(qh[:, 128:256], c, sa, sb) * Q_PRESCALE).astype(BF16)
            kvh = _dot(ckvn, wukv_ref[h])
            k_ref[h, :, 0:128] = kvh[:, 0:128].astype(BF16)
            k_ref[h, :, 128:256] = k_pe
            v_ref[h] = kvh[:, 128:256].astype(BF16)

    tab = pl.BlockSpec((tm, 128), lambda i: (i, 0))
    return pl.pallas_call(
        body, name="mla_pre", grid=(s // tm,),
        in_specs=[pl.BlockSpec((tm, MS_COLS), lambda i: (i, 0)),
                  pl.BlockSpec((1, Q_LORA), lambda i: (0, 0)), pl.BlockSpec((1, KV_LORA), lambda i: (0, 0)),
                  pl.BlockSpec((HEADS, Q_LORA, QK_PAD), lambda i: (0, 0, 0)),
                  pl.BlockSpec((HEADS, KV_LORA, 256), lambda i: (0, 0, 0)), tab, tab, tab],
        out_specs=[pl.BlockSpec((HEADS, tm, QK_PAD), lambda i: (0, i, 0)),
                   pl.BlockSpec((HEADS, tm, QK_PAD), lambda i: (0, i, 0)),
                   pl.BlockSpec((HEADS, tm, HEAD_DIM), lambda i: (0, i, 0)),
                   pl.BlockSpec((tm, Q_LORA), lambda i: (i, 0)), pl.BlockSpec((tm, KV_LORA), lambda i: (i, 0))],
        out_shape=[jax.ShapeDtypeStruct((HEADS, s, QK_PAD), BF16), jax.ShapeDtypeStruct((HEADS, s, QK_PAD), BF16),
                   jax.ShapeDtypeStruct((HEADS, s, HEAD_DIM), BF16),
                   jax.ShapeDtypeStruct((s, Q_LORA), BF16), jax.ShapeDtypeStruct((s, KV_LORA), BF16)],
        compiler_params=_params(1),
    )(ms, q_a_g, kv_a_g, wuq3, wukv3, *tabs)


def _causal_mask(t):
    r = lax.broadcasted_iota(jnp.int32, (t, t), 0)
    c = lax.broadcasted_iota(jnp.int32, (t, t), 1)
    return r >= c


def _flash_fwd(q, k, v, mz):
    s = q.shape[1]
    t = min(TQ, s)

    def body(q_ref, k_ref, v_ref, mz_ref, o_ref, yb_ref, lse_ref, m_s, l_s, acc_s):
        i = pl.program_id(1)
        m_s[...] = jnp.full_like(m_s, -jnp.inf)
        l_s[...] = jnp.zeros_like(l_s)
        acc_s[...] = jnp.zeros_like(acc_s)

        def step(j, modes):
            rows = pl.ds(pl.multiple_of(j * t, t), t)
            for hh in range(FLASH_HEADS):
                for ch, masked in enumerate(modes):
                    if masked is None:
                        continue
                    r = slice(ch * t, (ch + 1) * t)
                    sc = _dot_nt(q_ref[hh, r, :], k_ref[hh, rows, :])
                    if masked:
                        sc = jnp.where(_causal_mask(t), sc, -jnp.inf)
                    m_prev = m_s[hh, r, :]
                    m_new = jnp.maximum(m_prev, jnp.max(sc, axis=-1, keepdims=True))
                    p = jnp.exp2(sc - jnp.tile(m_new, (1, t // 128)))
                    alpha = jnp.exp2(m_prev - m_new)
                    l_s[hh, r, :] = alpha * l_s[hh, r, :] + jnp.sum(p, axis=-1, keepdims=True)
                    acc_s[hh, r, :] = alpha * acc_s[hh, r, :] + _dot(p.astype(BF16), v_ref[hh, rows, :])
                    m_s[hh, r, :] = m_new

        def loop_body(j, carry):
            step(j, (False, False))
            return carry

        lax.fori_loop(0, 2 * i, loop_body, 0)
        step(2 * i, (True, False))
        step(2 * i + 1, (None, True))
        for hh in range(FLASH_HEADS):
            cols = slice(hh * HEAD_DIM, (hh + 1) * HEAD_DIM)
            out = acc_s[hh] / l_s[hh]
            o_ref[:, cols] = out
            silu_z, _ = _silu_parts(mz_ref[:, cols])
            yb_ref[:, cols] = (out * silu_z).astype(BF16)
            lse_ref[hh] = m_s[hh] + jnp.log2(l_s[hh])

    nh = FLASH_HEADS
    t2 = 2 * t
    col = pl.BlockSpec((t2, nh * HEAD_DIM), lambda h, i: (i, h))
    return pl.pallas_call(
        body, name="flash_fwd", grid=(HEADS // nh, s // t2),
        in_specs=[pl.BlockSpec((nh, t2, QK_PAD), lambda h, i: (h, i, 0)),
                  pl.BlockSpec((nh, s, QK_PAD), lambda h, i: (h, 0, 0)),
                  pl.BlockSpec((nh, s, HEAD_DIM), lambda h, i: (h, 0, 0)), col],
        out_specs=[col, col, pl.BlockSpec((nh, t2, 128), lambda h, i: (h, i, 0))],
        out_shape=[jax.ShapeDtypeStruct((s, D_MODEL), F32), jax.ShapeDtypeStruct((s, D_MODEL), BF16),
                   jax.ShapeDtypeStruct((HEADS, s, 128), F32)],
        scratch_shapes=[pltpu.VMEM((nh, t2, 128), F32), pltpu.VMEM((nh, t2, 128), F32),
                        pltpu.VMEM((nh, t2, HEAD_DIM), F32)],
        compiler_params=_params(2),
    )(q, k, v, mz)


def _flash_bwd(q, k, v, dyb, mz, o_att, lse, tabs):
    s = q.shape[1]
    t = min(TQ, s)

    def body(q_ref, k_ref, v_ref, dyb_ref, mz_ref, o_ref, lse_ref, c_ref, sa_ref, sb_ref,
             dq_ref, dk_ref, dv_ref, dmz_ref, dq_s, delta_s, do_s):
        i = pl.program_id(1)

        @pl.when(i == 0)
        def _():
            dk_ref[...] = jnp.zeros_like(dk_ref)
            dv_ref[...] = jnp.zeros_like(dv_ref)

        silu_z, dsilu_z = _silu_parts(mz_ref[...])
        dyb_v = dyb_ref[...]
        out = o_ref[...]
        do32 = dyb_v * silu_z
        dmz_ref[...] = (dyb_v * out * dsilu_z).astype(BF16)
        delta_s[...] = jnp.broadcast_to(jnp.sum(do32 * out, axis=-1, keepdims=True), (2 * t, 128))
        do_s[...] = do32.astype(BF16)
        dq_s[...] = jnp.zeros_like(dq_s)

        def step(j, modes):
            rows = pl.ds(pl.multiple_of(j * t, t), t)
            kj = k_ref[rows, :]
            vj = v_ref[rows, :]
            dv_acc = None
            dk_acc = None
            for ch, masked in enumerate(modes):
                if masked is None:
                    continue
                r = slice(ch * t, (ch + 1) * t)
                qv = q_ref[r, :]
                do = do_s[r, :]
                sc = _dot_nt(qv, kj)
                if masked:
                    sc = jnp.where(_causal_mask(t), sc, -jnp.inf)
                p = jnp.exp2(sc - jnp.tile(lse_ref[r, :], (1, t // 128)))
                dp = _dot_nt(do, vj)
                ds = (p * (dp - jnp.tile(delta_s[r, :], (1, t // 128)))).astype(BF16)
                dv_c = _dot_tn(p.astype(BF16), do)
                dk_c = _dot_tn(ds, qv)
                dv_acc = dv_c if dv_acc is None else dv_acc + dv_c
                dk_acc = dk_c if dk_acc is None else dk_acc + dk_c
                dq_s[r, :] += _dot(ds, kj)
            dv_ref[rows, :] += dv_acc
            dk_ref[rows, :] += dk_acc

        def loop_body(j, carry):
            step(j, (False, False))
            return carry

        lax.fori_loop(0, 2 * i, loop_body, 0)
        step(2 * i, (True, False))
        step(2 * i + 1, (None, True))
        dq = dq_s[...] * ATT_SCALE
        dq_ref[:, 0:128] = dq[:, 0:128].astype(BF16)
        dq_ref[:, 128:256] = _rope_bwd(dq[:, 128:256], c_ref[...], sa_ref[...], sb_ref[...]).astype(BF16)

    t2 = 2 * t
    col = pl.BlockSpec((t2, HEAD_DIM), lambda h, i: (i, h))
    tab = pl.BlockSpec((t2, 128), lambda h, i: (i, 0))
    return pl.pallas_call(
        body, name="flash_bwd", grid=(HEADS, s // t2),
        in_specs=[pl.BlockSpec((None, t2, QK_PAD), lambda h, i: (h, i, 0)),
                  pl.BlockSpec((None, s, QK_PAD), lambda h, i: (h, 0, 0)),
                  pl.BlockSpec((None, s, HEAD_DIM), lambda h, i: (h, 0, 0)),
                  col, col, col, pl.BlockSpec((None, t2, 128), lambda h, i: (h, i, 0)), tab, tab, tab],
        out_specs=[pl.BlockSpec((None, t2, QK_PAD), lambda h, i: (h, i, 0)),
                   pl.BlockSpec((None, s, QK_PAD), lambda h, i: (h, 0, 0)),
                   pl.BlockSpec((None, s, HEAD_DIM), lambda h, i: (h, 0, 0)), col],
        out_shape=[jax.ShapeDtypeStruct((HEADS, s, QK_PAD), BF16), jax.ShapeDtypeStruct((HEADS, s, QK_PAD), F32),
                   jax.ShapeDtypeStruct((HEADS, s, HEAD_DIM), F32), jax.ShapeDtypeStruct((s, D_MODEL), BF16)],
        scratch_shapes=[pltpu.VMEM((t2, QK_PAD), F32), pltpu.VMEM((t2, 128), F32), pltpu.VMEM((t2, HEAD_DIM), BF16)],
        compiler_params=_params(2),
    )(q, k, v, dyb, mz, o_att, lse, *tabs)


def _mla_bwd_proj(dq, dk, dv, cqn, ckvn, ms, q_a_g, kv_a_g, wuq3, wukv3, tabs):
    s = ms.shape[0]
    tm = min(TM_FUSED, s)

    def body(dq_ref, dk_ref, dv_ref, cqn_ref, ckvn_ref, ms_ref, qg_ref, kvg_ref, wuq_ref, wukv_ref,
             c_ref, sa_ref, sb_ref, dms_ref, dwuq_ref, dwukv_ref, dqg_ref, dkvg_ref):
        @pl.when(pl.program_id(0) == 0)
        def _():
            dwuq_ref[...] = jnp.zeros_like(dwuq_ref)
            dwukv_ref[...] = jnp.zeros_like(dwukv_ref)
            dqg_ref[...] = jnp.zeros_like(dqg_ref)
            dkvg_ref[...] = jnp.zeros_like(dkvg_ref)

        cqn = cqn_ref[...]
        ckvn = ckvn_ref[...]
        dcqn = jnp.zeros((tm, Q_LORA), F32)
        dckvn = jnp.zeros((tm, KV_LORA), F32)
        dkpe = jnp.zeros((tm, 128), F32)
        for h in range(HEADS):
            dqh = dq_ref[h]
            dcqn += _dot_nt(dqh, wuq_ref[h])
            dwuq_ref[h] += _dot_tn(cqn, dqh)
            dkh = dk_ref[h] * LN2
            dkvh = jnp.concatenate([dkh[:, 0:128], dv_ref[h]], axis=1).astype(BF16)
            dckvn += _dot_nt(dkvh, wukv_ref[h])
            dwukv_ref[h] += _dot_tn(ckvn, dkvh)
            dkpe += dkh[:, 128:256]
        dcq, dqg_rows = _rms_bwd(ms_ref[:, 0:Q_LORA], qg_ref[...], dcqn)
        dckv, dkvg_rows = _rms_bwd(ms_ref[:, Q_LORA:Q_LORA + KV_LORA], kvg_ref[...], dckvn)
        dqg_ref[...] += jnp.sum(dqg_rows, axis=0, keepdims=True)
        dkvg_ref[...] += jnp.sum(dkvg_rows, axis=0, keepdims=True)
        dms_ref[:, 0:Q_LORA] = dcq.astype(BF16)
        dms_ref[:, Q_LORA:Q_LORA + KV_LORA] = dckv.astype(BF16)
        dms_ref[:, Q_LORA + KV_LORA:MS_COLS] = _rope_bwd(dkpe, c_ref[...], sa_ref[...], sb_ref[...]).astype(BF16)

    tab = pl.BlockSpec((tm, 128), lambda i: (i, 0))
    wq = pl.BlockSpec((HEADS, Q_LORA, QK_PAD), lambda i: (0, 0, 0))
    wkv = pl.BlockSpec((HEADS, KV_LORA, 256), lambda i: (0, 0, 0))
    qg = pl.BlockSpec((1, Q_LORA), lambda i: (0, 0))
    kvg = pl.BlockSpec((1, KV_LORA), lambda i: (0, 0))
    return pl.pallas_call(
        body, name="mla_bwd_proj", grid=(s // tm,),
        in_specs=[pl.BlockSpec((HEADS, tm, QK_PAD), lambda i: (0, i, 0)),
                  pl.BlockSpec((HEADS, tm, QK_PAD), lambda i: (0, i, 0)),
                  pl.BlockSpec((HEADS, tm, HEAD_DIM), lambda i: (0, i, 0)),
                  pl.BlockSpec((tm, Q_LORA), lambda i: (i, 0)), pl.BlockSpec((tm, KV_LORA), lambda i: (i, 0)),
                  pl.BlockSpec((tm, MS_COLS), lambda i: (i, 0)), qg, kvg, wq, wkv, tab, tab, tab],
        out_specs=[pl.BlockSpec((tm, MS_COLS), lambda i: (i, 0)), wq, wkv, qg, kvg],
        out_shape=[jax.ShapeDtypeStruct((s, MS_COLS), BF16), jax.ShapeDtypeStruct((HEADS, Q_LORA, QK_PAD), F32),
                   jax.ShapeDtypeStruct((HEADS, KV_LORA, 256), F32),
                   jax.ShapeDtypeStruct((1, Q_LORA), F32), jax.ShapeDtypeStruct((1, KV_LORA), F32)],
        compiler_params=_params(1),
    )(dq, dk, dv, cqn, ckvn, ms, q_a_g, kv_a_g, wuq3, wukv3, *tabs)


def _merge_fused(ya, yb, glog, b_gate, x, tgt, fg, wproj):
    s = x.shape[0]
    tm = min(TM_FUSED, s)

    def body(ya_ref, yb_ref, g0_ref, g1_ref, b0_ref, b1_ref, x_ref, t_ref, fg_ref, w_ref,
             mg_ref, dx2_ref, dx2b_ref, dya_ref, dyb_ref, dgl_ref, dpa_ref, dpb_ref, loss_ref, dfg_ref, dbg_ref):
        @pl.when(pl.program_id(0) == 0)
        def _():
            loss_ref[...] = jnp.zeros_like(loss_ref)
            dfg_ref[...] = jnp.zeros_like(dfg_ref)
            dbg_ref[...] = jnp.zeros_like(dbg_ref)

        pa = _dot(ya_ref[...], w_ref[0])
        pb = _dot(yb_ref[...], w_ref[1])
        g0 = jax.nn.sigmoid(g0_ref[...] + b0_ref[...])
        g1 = jax.nn.sigmoid(g1_ref[...] + b1_ref[...])
        merged = (g0 * pa + g1 * pb).astype(BF16)
        mg_ref[...] = merged
        x2 = x_ref[...] + _dot(merged, w_ref[2])
        fg_v = fg_ref[...]
        err = _rms(x2, fg_v) - t_ref[...]
        loss_ref[...] += 0.5 * jnp.sum(jnp.mean(err * err, axis=-1, keepdims=True), axis=0, keepdims=True)
        dx2, dfg_rows = _rms_bwd(x2, fg_v, err * (1.0 / D_MODEL))
        dx2_ref[...] = dx2
        dfg_ref[...] += jnp.sum(dfg_rows, axis=0, keepdims=True)

        dx2b = dx2.astype(BF16)
        dx2b_ref[...] = dx2b
        dmg = _dot_nt(dx2b, w_ref[2])
        dpa = (dmg * g0).astype(BF16)
        dpb = (dmg * g1).astype(BF16)
        dpa_ref[...] = dpa
        dpb_ref[...] = dpb
        dgl0 = dmg * pa * g0 * (1.0 - g0)
        dgl1 = dmg * pb * g1 * (1.0 - g1)
        dgl_ref[:, 0:D_MODEL] = dgl0.astype(BF16)
        dgl_ref[:, D_MODEL:2 * D_MODEL] = dgl1.astype(BF16)
        dbg_ref[:, 0:D_MODEL] += jnp.sum(dgl0, axis=0, keepdims=True)
        dbg_ref[:, D_MODEL:2 * D_MODEL] += jnp.sum(dgl1, axis=0, keepdims=True)
        dya_ref[...] = _dot_nt(dpa, w_ref[0])
        dyb_ref[...] = _dot_nt(dpb, w_ref[1])

    row = pl.BlockSpec((tm, D_MODEL), lambda i: (i, 0))
    row1 = pl.BlockSpec((tm, D_MODEL), lambda i: (i, 1))
    row2 = pl.BlockSpec((tm, 2 * D_MODEL), lambda i: (i, 0))
    vec = pl.BlockSpec((1, D_MODEL), lambda i: (0, 0))
    vec1 = pl.BlockSpec((1, D_MODEL), lambda i: (0, 1))
    vec2 = pl.BlockSpec((1, 2 * D_MODEL), lambda i: (0, 0))
    f32_rows = jax.ShapeDtypeStruct((s, D_MODEL), F32)
    bf16_rows = jax.ShapeDtypeStruct((s, D_MODEL), BF16)
    return pl.pallas_call(
        body, name="merge_fused", grid=(s // tm,),
        in_specs=[row, row, row, row1, vec, vec1, row, row, vec, pl.BlockSpec((3, D_MODEL, D_MODEL), lambda i: (0, 0, 0))],
        out_specs=[row, row, row, row, row, row2, row, row, pl.BlockSpec((1, 128), lambda i: (0, 0)), vec, vec2],
        out_shape=[bf16_rows, f32_rows, bf16_rows, f32_rows, f32_rows, jax.ShapeDtypeStruct((s, 2 * D_MODEL), BF16),
                   bf16_rows, bf16_rows, jax.ShapeDtypeStruct((1, 128), F32), jax.ShapeDtypeStruct((1, D_MODEL), F32),
                   jax.ShapeDtypeStruct((1, 2 * D_MODEL), F32)],
        compiler_params=_params(1),
    )(ya, yb, glog, glog, b_gate, b_gate, x, tgt, fg, wproj)


def _proj_fused(x, g, w_int):
    s = x.shape[0]
    tm = min(TM_FUSED, s)

    def body(x_ref, g_ref, w_hbm, h_ref, hg_ref, ms_ref, mz_ref, gl_ref, w_s, sem):
        @pl.when(pl.program_id(0) == 0)
        def _():
            cp = pltpu.make_async_copy(w_hbm, w_s, sem)
            cp.start()
            cp.wait()

        h = _rms(x_ref[...], g_ref[...]).astype(BF16)
        h_ref[...] = h
        for j in range(4):
            hg_ref[j] = _dot_nt(h, w_s[j * D_MODEL:(j + 1) * D_MODEL, :])
        ms = _dot_nt(h, w_s[4096:4096 + MS_COLS, :])
        lane = lax.broadcasted_iota(jnp.int32, ms.shape, 1)
        ms_ref[...] = jnp.where(lane < 704, ms, 0.0)
        mz_ref[...] = _dot_nt(h, w_s[4800:5824, :])
        for j in range(2):
            gl_ref[:, j * D_MODEL:(j + 1) * D_MODEL] = _dot_nt(h, w_s[5824 + j * D_MODEL:5824 + (j + 1) * D_MODEL, :])

    row = pl.BlockSpec((tm, D_MODEL), lambda i: (i, 0))
    return pl.pallas_call(
        body, name="proj_fused", grid=(s // tm,),
        in_specs=[row, pl.BlockSpec((1, D_MODEL), lambda i: (0, 0)), pl.BlockSpec(memory_space=pl.ANY)],
        out_specs=[row, pl.BlockSpec((4, tm, D_MODEL), lambda i: (0, i, 0)), pl.BlockSpec((tm, MS_COLS), lambda i: (i, 0)),
                   row, pl.BlockSpec((tm, 2 * D_MODEL), lambda i: (i, 0))],
        out_shape=[jax.ShapeDtypeStruct((s, D_MODEL), BF16), jax.ShapeDtypeStruct((4, s, D_MODEL), F32),
                   jax.ShapeDtypeStruct((s, MS_COLS), F32), jax.ShapeDtypeStruct((s, D_MODEL), F32),
                   jax.ShapeDtypeStruct((s, 2 * D_MODEL), F32)],
        scratch_shapes=[pltpu.VMEM(w_int.shape, BF16), pltpu.SemaphoreType.DMA], compiler_params=_params(1),
    )(x, g, w_int)


def _dh_fused(dhg, dms, dmz, dglog, w_int, x, g, dx2):
    s = x.shape[0]
    tm = min(512, s)

    def body(dhg_ref, dms_ref, dmz_ref, dgl_ref, w_hbm, x_ref, g_ref, dx2_ref, dx_ref, dg_ref, w_s, sem):
        @pl.when(pl.program_id(0) == 0)
        def _():
            dg_ref[...] = jnp.zeros_like(dg_ref)
            cp = pltpu.make_async_copy(w_hbm, w_s, sem)
            cp.start()
            cp.wait()

        dh = _dot(dms_ref[...], w_s[4096:4096 + MS_COLS, :]) + _dot(dmz_ref[...], w_s[4800:5824, :])
        for j in range(4):
            dh += _dot(dhg_ref[j], w_s[j * D_MODEL:(j + 1) * D_MODEL, :])
        for j in range(2):
            dh += _dot(dgl_ref[:, j * D_MODEL:(j + 1) * D_MODEL], w_s[5824 + j * D_MODEL:5824 + (j + 1) * D_MODEL, :])
        dx, dg_rows = _rms_bwd(x_ref[...], g_ref[...], dh)
        dx_ref[...] = dx + dx2_ref[...]
        dg_ref[...] += jnp.sum(dg_rows, axis=0, keepdims=True)

    row = pl.BlockSpec((tm, D_MODEL), lambda i: (i, 0))
    vec = pl.BlockSpec((1, D_MODEL), lambda i: (0, 0))
    return pl.pallas_call(
        body, name="dh_fused", grid=(s // tm,),
        in_specs=[pl.BlockSpec((4, tm, D_MODEL), lambda i: (0, i, 0)), pl.BlockSpec((tm, MS_COLS), lambda i: (i, 0)), row,
                  pl.BlockSpec((tm, 2 * D_MODEL), lambda i: (i, 0)), pl.BlockSpec(memory_space=pl.ANY), row, vec, row],
        out_specs=[row, vec],
        out_shape=[jax.ShapeDtypeStruct((s, D_MODEL), F32), jax.ShapeDtypeStruct((1, D_MODEL), F32)],
        scratch_shapes=[pltpu.VMEM(w_int.shape, BF16), pltpu.SemaphoreType.DMA], compiler_params=_params(1),
    )(dhg, dms, dmz, dglog, w_int, x, g, dx2)


def _local_step(x, tgt, w_int, w_uq, w_ukv, wproj, norm_g, b_gate, lb_logits, hg_norm_g, q_a_g, kv_a_g, fg):
    s = x.shape[0]
    wuq3 = jnp.pad(w_uq.reshape(Q_LORA, HEADS, QK_DIM).transpose(1, 0, 2), ((0, 0), (0, 0), (0, QK_PAD - QK_DIM)))
    wukv3 = w_ukv.reshape(KV_LORA, HEADS, 256).transpose(1, 0, 2)
    tabs = _rope_tables(s)

    h, hg, ms, mz, glog = _proj_fused(x, norm_g, w_int)
    o_pre, ya, st0 = _hgrn_fwd(hg, lb_logits, hg_norm_g)
    q, k, v, cqn, ckvn = _mla_pre(ms, q_a_g, kv_a_g, wuq3, wukv3, tabs)
    o_att, yb, lse = _flash_fwd(q, k, v, mz)
    merged, dx2, dx2b, dya, dyb, dglog, dpa, dpb, loss, dfg, dbg = _merge_fused(ya, yb, glog, b_gate, x, tgt, fg, wproj)

    d_wout = _mm_tn(merged, dx2b, name="dw_out")
    d_wpa = _mm_tn(ya, dpa, name="dw_proj_a")
    d_wpb = _mm_tn(yb, dpb, name="dw_proj_b")
    dhg, dlb, dhgg = _hgrn_bwd(hg, o_pre, dya, st0, lb_logits, hg_norm_g)
    dq, dk, dv, dmz = _flash_bwd(q, k, v, dyb, mz, o_att, lse, tabs)
    dms, d_wuq3, d_wukv3, dqg, dkvg = _mla_bwd_proj(dq, dk, dv, cqn, ckvn, ms, q_a_g, kv_a_g, wuq3, wukv3, tabs)
    d_hg = _mm_tn(dhg, h, name="dw_in_hg")
    d_ms = _mm_tn(dms, h, name="dw_in_ms")
    d_mz = _mm_tn(dmz, h, name="dw_in_mz")
    d_gl = _mm_tn(dglog, h, name="dw_in_gate")
    grad_x, dng = _dh_fused(dhg, dms, dmz, dglog, w_int, x, norm_g, dx2)

    d_w_int = jnp.concatenate([d_hg.reshape(4 * D_MODEL, D_MODEL), d_ms[0:704], d_mz, d_gl], axis=0)
    small = {"norm_g": dng, "b_gate": dbg, "lb": dlb, "hg_norm_g": dhgg, "q_a_g": dqg, "kv_a_g": dkvg,
             "final_norm_g": dfg}
    return loss, grad_x, d_w_int, d_wuq3, d_wukv3, (d_wpa, d_wpb, d_wout), small


def _pack_rest(w_uq_b, w_ukv_b, wpa_b, wpb_b, wout_b):
    return jnp.concatenate([w_uq_b.reshape(144, D_MODEL), w_ukv_b.reshape(128, D_MODEL), wpa_b, wpb_b, wout_b], axis=0)


def _unpack_rest(p):
    return (p[0:144].reshape(Q_LORA, 384), p[144:272].reshape(KV_LORA, 512), p[272:528], p[528:784], p[784:1040])


def _pack_rest_grads(d_wuq3, d_wukv3, d_proj):
    d_wuq = d_wuq3.transpose(1, 0, 2)[:, :, 0:QK_DIM].reshape(Q_LORA, HEADS * QK_DIM)
    d_wukv = d_wukv3.transpose(1, 0, 2).reshape(KV_LORA, HEADS * 256)
    blocks = []
    for b in range(N_CHIPS):
        rows = slice(b * 256, (b + 1) * 256)
        blocks.append(_pack_rest(d_wuq[:, b * 384:(b + 1) * 384], d_wukv[:, b * 512:(b + 1) * 512],
                                 d_proj[0][rows], d_proj[1][rows], d_proj[2][rows]))
    return jnp.stack(blocks, axis=0)


def _unpack_rest_weights(g):
    parts = [_unpack_rest(g[b]) for b in range(N_CHIPS)]
    w_uq, w_ukv = (jnp.concatenate([p[n] for p in parts], axis=1) for n in range(2))
    wproj = jnp.stack([jnp.concatenate([p[n] for p in parts], axis=0) for n in range(2, 5)], axis=0)
    return w_uq, w_ukv, wproj


MESH_ID = pl.DeviceIdType.MESH
ANY = pl.BlockSpec(memory_space=pl.ANY)
HALF_COLS = D_MODEL // 2


def _me():
    return lax.axis_index("x"), lax.axis_index("y"), lax.axis_index("c")


def _other_chips(x, y):
    return [(1 - x, y), (x, 1 - y), (1 - x, 1 - y)]


def _cols(c):
    return pl.ds(c * HALF_COLS, HALF_COLS)


def _gather_weights(w_blk, r_blk):
    def body(w_ref, r_ref, ow_ref, or_ref, send_sems, recv_sems):
        x, y, c = _me()
        chips = _other_chips(x, y)
        me = 2 * x + y
        pairs = [(w_ref, ow_ref), (r_ref, or_ref)]

        def copy(k, src, dst, to):
            return pltpu.make_async_remote_copy(src_ref=src, dst_ref=dst, send_sem=send_sems.at[k],
                                                recv_sem=recv_sems.at[k], device_id=to, device_id_type=MESH_ID)

        first =[copy(6 * a + j, src.at[:, _cols(c)], dst.at[me, :, _cols(c)], (cx, cy, c))
                 for a, (src, dst) in enumerate(pairs) for j, (cx, cy) in enumerate(chips)]
        for cp in first:
            cp.start()
        passed = []
        for a, (src, dst) in enumerate(pairs):
            for j, (cx, cy) in enumerate(chips):
                landed = dst.at[2 * cx + cy, :, _cols(c)]
                copy(6 * a + j, landed, landed, (cx, cy, c)).wait_recv()
                fwd = copy(6 * a + 3 + j, landed, landed, (x, y, 1 - c))
                fwd.start()
                passed.append(fwd)
        for a, (src, dst) in enumerate(pairs):
            for j, (cx, cy) in enumerate(chips):
                theirs = dst.at[2 * cx + cy, :, _cols(1 - c)]
                copy(6 * a + 3 + j, theirs, theirs, (x, y, 1 - c)).wait_recv()
        for cp in first + passed:
            cp.wait_send()

    gw, gr = pl.pallas_call(
        body, name="gather_weights", in_specs=[ANY, ANY], out_specs=[ANY, ANY],
        out_shape=[jax.ShapeDtypeStruct((N_CHIPS,) + w_blk.shape, w_blk.dtype),
                   jax.ShapeDtypeStruct((N_CHIPS,) + r_blk.shape, r_blk.dtype)],
        scratch_shapes=[pltpu.SemaphoreType.DMA((12,)), pltpu.SemaphoreType.DMA((12,))],
    )(w_blk, r_blk)
    chip = 2 * lax.axis_index("x") + lax.axis_index("y")
    return (lax.dynamic_update_slice(gw, w_blk[None], (chip, 0, 0)),
            lax.dynamic_update_slice(gr, r_blk[None], (chip, 0, 0)))


def _swap_halves(gw, gr):
    def body(gw_ref, gr_ref, lw_ref, lr_ref, send_sems, recv_sems):
        x, y, c = _me()
        cps = [pltpu.make_async_remote_copy(
            src_ref=src, dst_ref=dst, send_sem=send_sems.at[a], recv_sem=recv_sems.at[a],
            device_id=(x, y, 1 - c), device_id_type=MESH_ID)
            for a, (src, dst) in enumerate([(gw_ref.at[:, _cols(1 - c)], lw_ref),
                                            (gr_ref.at[:, :, _cols(1 - c)], lr_ref)])]
        for cp in cps:
            cp.start()
        for cp in cps:
            cp.wait()

    return pl.pallas_call(
        body, name="grad_swap_halves", in_specs=[ANY, ANY], out_specs=[ANY, ANY],
        out_shape=[jax.ShapeDtypeStruct((gw.shape[0], HALF_COLS), gw.dtype),
                   jax.ShapeDtypeStruct(gr.shape[:2] + (HALF_COLS,), gr.dtype)],
        scratch_shapes=[pltpu.SemaphoreType.DMA((2,)), pltpu.SemaphoreType.DMA((2,))],
    )(gw, gr)


def _scatter_blocks(hw, hr):
    nw, nr = hw.shape[0] // N_CHIPS, hr.shape[0] // N_CHIPS

    def body(hw_ref, hr_ref, lw_ref, lr_ref, send_sems, recv_sems):
        x, y, c = _me()
        cps = []
        for a, (src, dst, n) in enumerate([(hw_ref, lw_ref, nw), (hr_ref, lr_ref, nr)]):
            for j, (cx, cy) in enumerate(_other_chips(x, y)):
                cps.append(pltpu.make_async_remote_copy(
                    src_ref=src.at[pl.ds((2 * cx + cy) * n, n), :], dst_ref=dst.at[j], send_sem=send_sems.at[3 * a + j],
                    recv_sem=recv_sems.at[3 * a + j], device_id=(cx, cy, c), device_id_type=MESH_ID))
        for cp in cps:
            cp.start()
        for cp in cps:
            cp.wait()

    return pl.pallas_call(
        body, name="grad_scatter_blocks", in_specs=[ANY, ANY], out_specs=[ANY, ANY],
        out_shape=[jax.ShapeDtypeStruct((3, nw, HALF_COLS), hw.dtype), jax.ShapeDtypeStruct((3, nr, HALF_COLS), hr.dtype)],
        scratch_shapes=[pltpu.SemaphoreType.DMA((6,)), pltpu.SemaphoreType.DMA((6,))],
    )(hw, hr)


def _swap_reduced(rw, rr):
    def body(rw_ref, rr_ref, ow_ref, or_ref, send_sems, recv_sems):
        x, y, c = _me()
        cps = [pltpu.make_async_remote_copy(
            src_ref=src, dst_ref=dst, send_sem=send_sems.at[a], recv_sem=recv_sems.at[a],
            device_id=(x, y, 1 - c), device_id_type=MESH_ID)
            for a, (src, dst) in enumerate([(rw_ref, ow_ref), (rr_ref, or_ref)])]
        for cp in cps:
            cp.start()
        for cp in cps:
            cp.wait()

    return pl.pallas_call(
        body, name="grad_swap_reduced", in_specs=[ANY, ANY], out_specs=[ANY, ANY],
        out_shape=[jax.ShapeDtypeStruct(rw.shape, rw.dtype), jax.ShapeDtypeStruct(rr.shape, rr.dtype)],
        scratch_shapes=[pltpu.SemaphoreType.DMA((2,)), pltpu.SemaphoreType.DMA((2,))],
    )(rw, rr)


def _join_cols(mine, theirs):
    first = lax.axis_index("c") == 0
    return jnp.concatenate([jnp.where(first, mine, theirs), jnp.where(first, theirs, mine)], axis=1)


def _gather_small(vec):
    def body(v_ref, out_ref, send_sems, recv_sems, local_sem):
        x, y, c = _me()
        my_id = 4 * x + 2 * y + c
        mine = pltpu.make_async_copy(v_ref, out_ref.at[my_id], local_sem)
        mine.start()
        cps = []
        for r in range(1, N_DEV):
            peer = (x ^ (r >> 2), y ^ ((r >> 1) & 1), c ^ (r & 1))
            cps.append(pltpu.make_async_remote_copy(
                src_ref=v_ref, dst_ref=out_ref.at[my_id], send_sem=send_sems.at[r - 1],
                recv_sem=recv_sems.at[r - 1], device_id=peer, device_id_type=MESH_ID))
        for cp in cps:
            cp.start()
        for cp in cps:
            cp.wait()
        mine.wait()

    return pl.pallas_call(
        body, name="gather_small", in_specs=[ANY], out_specs=ANY,
        out_shape=jax.ShapeDtypeStruct((N_DEV, 1, SMALL_COLS), vec.dtype),
        scratch_shapes=[pltpu.SemaphoreType.DMA((N_DEV - 1,)), pltpu.SemaphoreType.DMA((N_DEV - 1,)),
                        pltpu.SemaphoreType.DMA],
    )(vec)


def _add_cores(c_idx, g, landed, *, tm, name):
    r = g.shape[0]

    def body(c_ref, g_ref, l_ref, o32_ref, o16_ref):
        acc = g_ref[...] + l_ref[...]
        o32_ref[...] = acc
        o16_ref[...] = acc.astype(BF16)

    half = pl.BlockSpec((tm, HALF_COLS), lambda i, c_ref: (i, 0))
    grid_spec = pltpu.PrefetchScalarGridSpec(
        num_scalar_prefetch=1, grid=(r // tm,),
        in_specs=[pl.BlockSpec((tm, HALF_COLS), lambda i, c_ref: (i, c_ref[0])), half], out_specs=[half, half])
    return pl.pallas_call(
        body, name=name, grid_spec=grid_spec,
        out_shape=[jax.ShapeDtypeStruct((r, HALF_COLS), F32), jax.ShapeDtypeStruct((r, HALF_COLS), BF16)],
        compiler_params=_params(1),
    )(c_idx, g, landed)


def _add_chips(chip_idx, h32, landed, *, tm, name):
    n = landed.shape[1]
    per = n // tm

    def body(chip_ref, h_ref, l_ref, o_ref):
        acc = h_ref[...]
        for j in range(3):
            acc = acc + l_ref[j].astype(F32)
        o_ref[...] = acc

    grid_spec = pltpu.PrefetchScalarGridSpec(
        num_scalar_prefetch=1, grid=(per,),
        in_specs=[pl.BlockSpec((tm, HALF_COLS), lambda i, chip_ref: (chip_ref[0] * per + i, 0)),
                  pl.BlockSpec((3, tm, HALF_COLS), lambda i, chip_ref: (0, i, 0))],
        out_specs=pl.BlockSpec((tm, HALF_COLS), lambda i, chip_ref: (i, 0)))
    return pl.pallas_call(
        body, name=name, grid_spec=grid_spec, out_shape=jax.ShapeDtypeStruct((n, HALF_COLS), F32),
        compiler_params=_params(1),
    )(chip_idx, h32, landed)


def _pack_small(small, lb_logits, loss):
    def body(ng_ref, bg_ref, dlb_ref, lbl_ref, hgg_ref, qg_ref, kvg_ref, fg_ref, loss_ref, out_ref):
        out_ref[...] = jnp.zeros_like(out_ref)
        out_ref[:, 0:1024] = ng_ref[...]
        out_ref[:, 1024:3072] = bg_ref[...]
        _, p0p1 = _lower_bound(lbl_ref[...])
        dl0 = dlb_ref[...] * p0p1
        out_ref[:, 3072:4096] = dl0
        out_ref[:, 4096:5120] = -dl0
        hgg = hgg_ref[0]
        for h in range(1, HEADS):
            hgg = hgg + hgg_ref[h]
        out_ref[:, 5120:5248] = hgg
        out_ref[:, 5248:5632] = qg_ref[...]
        out_ref[:, 5632:5888] = kvg_ref[...]
        out_ref[:, 5888:6912] = fg_ref[...]
        out_ref[:, 6912:7040] = loss_ref[...]

    return pl.pallas_call(
        body, name="pack_small", out_shape=jax.ShapeDtypeStruct((1, SMALL_COLS), F32),
    )(small["norm_g"], small["b_gate"], small["lb"], lb_logits, small["hg_norm_g"], small["q_a_g"],
      small["kv_a_g"], small["final_norm_g"], loss)


def _adamw_math(w, g, m, v):
    nm = ADAM_B1 * m + (1.0 - ADAM_B1) * g
    nv = ADAM_B2 * v + (1.0 - ADAM_B2) * (g * g)
    m_hat = nm / (1.0 - ADAM_B1 ** ADAM_STEP)
    v_hat = nv / (1.0 - ADAM_B2 ** ADAM_STEP)
    return -ADAM_LR * (m_hat / (jnp.sqrt(v_hat) + ADAM_EPS) + ADAM_WD * w), nm, nv


def _adamw(w, g, m, v, *, name, tm):
    r, cols = w.shape

    def body(w_ref, g_ref, m_ref, v_ref, d_ref, nm_ref, nv_ref):
        d_ref[...], nm_ref[...], nv_ref[...] = _adamw_math(w_ref[...], g_ref[...], m_ref[...], v_ref[...])

    row = pl.BlockSpec((tm, cols), lambda i: (i, 0))
    shp = jax.ShapeDtypeStruct((r, cols), F32)
    return pl.pallas_call(
        body, name=name, grid=(r // tm,), in_specs=[row] * 4, out_specs=[row] * 3, out_shape=[shp] * 3,
        compiler_params=_params(1),
    )(w, g, m, v)


SMALL_SLOTS = (("norm_g", (0,)), ("b_gate", (1024,)), ("lb_logits", (3072, 4096)), ("hg_norm_g", (5120,)),
               ("q_a_g", (5248,)), ("kv_a_g", (5632,)), ("final_norm_g", (5888,)))
LOSS_SLOT = 6912


def _small_update(gathered, ws, ms, vs):
    n = len(SMALL_SLOTS)

    def body(*refs):
        g_ref = refs[0]
        w_refs, m_refs, v_refs = refs[1:1 + n], refs[1 + n:1 + 2 * n], refs[1 + 2 * n:1 + 3 * n]
        outs = refs[1 + 3 * n:]
        loss_ref = outs[0]
        g_out, d_out, nm_out, nv_out = (outs[1 + k * n:1 + (k + 1) * n] for k in range(4))
        total = g_ref[0]
        for dev in range(1, N_DEV):
            total = total + g_ref[dev]
        loss_ref[...] = total[:, LOSS_SLOT:LOSS_SLOT + 128]
        for p, (_, offsets) in enumerate(SMALL_SLOTS):
            cols = w_refs[p].shape[1]
            for r, off in enumerate(offsets):
                rows = slice(r, r + 1)
                g = total[:, off:off + cols]
                g_out[p][rows, :] = g
                d_out[p][rows, :], nm_out[p][rows, :], nv_out[p][rows, :] = _adamw_math(
                    w_refs[p][rows, :], g, m_refs[p][rows, :], v_refs[p][rows, :])

    shapes = [jax.ShapeDtypeStruct(w.shape, F32) for w in ws]
    res = pl.pallas_call(
        body, name="small_update", out_shape=[jax.ShapeDtypeStruct((1, 128), F32)] + shapes * 4,
    )(gathered, *ws, *ms, *vs)
    return res[0], res[1:1 + n], res[1 + n:1 + 2 * n], res[1 + 2 * n:1 + 3 * n], res[1 + 3 * n:1 + 4 * n]


def kernel(x, norm_g, w_in, b_gate, lb_logits, hg_norm_g, q_a_g, w_uq, kv_a_g, w_ukv, w_proj_a, w_proj_b, w_out, final_norm_g, loss_target, m_norm_g, m_w_in, m_b_gate, m_lb_logits, m_hg_norm_g, m_q_a_g, m_w_uq, m_kv_a_g, m_w_ukv, m_w_proj_a, m_w_proj_b, m_w_out, m_final_norm_g, v_norm_g, v_w_in, v_b_gate, v_lb_logits, v_hg_norm_g, v_q_a_g, v_w_uq, v_kv_a_g, v_w_ukv, v_w_proj_a, v_w_proj_b, v_w_out, v_final_norm_g):
    c_idx = lax.axis_index("c").astype(jnp.int32).reshape(1)
    chip_idx = (2 * lax.axis_index("x") + lax.axis_index("y")).astype(jnp.int32).reshape(1)

    w_blk = w_in[0].T.astype(BF16)
    r_blk = _pack_rest(w_uq[0], w_ukv[0], w_proj_a[0], w_proj_b[0], w_out[0]).astype(BF16)
    gw, gr = _gather_weights(w_blk, r_blk)
    fw_uq, fw_ukv, fwproj = _unpack_rest_weights(gr)

    loss, grad_x, d_w_int, d_wuq3, d_wukv3, d_proj, small = _local_step(
        x[0], loss_target[0], gw.reshape(W_IN_COLS, D_MODEL), fw_uq, fw_ukv, fwproj,
        norm_g, b_gate, lb_logits, hg_norm_g, q_a_g, kv_a_g, final_norm_g.reshape(1, D_MODEL))

    d_rest = _pack_rest_grads(d_wuq3, d_wukv3, d_proj)
    lw, lr = _swap_halves(d_w_int, d_rest)
    hw32, hw16 = _add_cores(c_idx, d_w_int, lw, tm=656, name="grad_add_cores_w")
    hr32, hr16 = _add_cores(c_idx, d_rest.reshape(N_CHIPS * REST_ROWS, D_MODEL), lr.reshape(N_CHIPS * REST_ROWS, HALF_COLS),
                            tm=REST_ROWS, name="grad_add_cores_r")
    landed_w, landed_r = _scatter_blocks(hw16, hr16)
    rw = _add_chips(chip_idx, hw32, landed_w, tm=656, name="grad_add_chips_w")
    rr = _add_chips(chip_idx, hr32, landed_r, tm=208, name="grad_add_chips_r")
    tw, tr = _swap_reduced(rw, rr)
    g_w_in = _join_cols(rw, tw).T
    g_rest = _join_cols(rr, tr)
    g_uq, g_ukv, g_pa, g_pb, g_out = _unpack_rest(g_rest)

    small_all = _gather_small(_pack_small(small, lb_logits, loss))

    upd = {
        "w_in": _adamw(w_in[0], g_w_in, m_w_in[0], v_w_in[0], name="adamw_w_in", tm=128),
        "w_uq": _adamw(w_uq[0], g_uq, m_w_uq[0], v_w_uq[0], name="adamw_w_uq", tm=Q_LORA),
        "w_ukv": _adamw(w_ukv[0], g_ukv, m_w_ukv[0], v_w_ukv[0], name="adamw_w_ukv", tm=KV_LORA),
        "w_proj_a": _adamw(w_proj_a[0], g_pa, m_w_proj_a[0], v_w_proj_a[0], name="adamw_w_proj_a", tm=256),
        "w_proj_b": _adamw(w_proj_b[0], g_pb, m_w_proj_b[0], v_w_proj_b[0], name="adamw_w_proj_b", tm=256),
        "w_out": _adamw(w_out[0], g_out, m_w_out[0], v_w_out[0], name="adamw_w_out", tm=256),
    }
    loss_vec, *small_sets = _small_update(
        small_all,
        [norm_g, b_gate, lb_logits, hg_norm_g, q_a_g, kv_a_g, final_norm_g.reshape(1, D_MODEL)],
        [m_norm_g, m_b_gate, m_lb_logits, m_hg_norm_g, m_q_a_g, m_kv_a_g, m_final_norm_g.reshape(1, D_MODEL)],
        [v_norm_g, v_b_gate, v_lb_logits, v_hg_norm_g, v_q_a_g, v_kv_a_g, v_final_norm_g.reshape(1, D_MODEL)])

    def outputs(big, small_set):
        s_ng, s_bg, s_lb, s_hg, s_qg, s_kvg, s_fg = small_set
        return (s_ng, big["w_in"][None], s_bg, s_lb, s_hg, s_qg, big["w_uq"][None], s_kvg, big["w_ukv"][None],
                big["w_proj_a"][None], big["w_proj_b"][None], big["w_out"][None], s_fg.reshape(D_MODEL))

    grads = {"w_in": g_w_in, "w_uq": g_uq, "w_ukv": g_ukv, "w_proj_a": g_pa, "w_proj_b": g_pb, "w_out": g_out}
    return (loss_vec[0, 0], grad_x[None], *outputs(grads, small_sets[0]),
            *(o for k in range(3) for o in outputs({n: u[k] for n, u in upd.items()}, small_sets[1 + k])))
```

```python
import functools

import jax
import jax.numpy as jnp
from jax import lax
from jax.experimental import pallas as pl
from jax.experimental.pallas import tpu as pltpu

F32 = jnp.float32
BF16 = jnp.bfloat16

D_MODEL = 1024
HEADS = 8
HEAD_DIM = 128
HG_CHUNK = 32
CHUNK_SHIFT = 5
HEAD_SHIFT = 7
QK_NOPE = 128
QK_ROPE = 64
QK_DIM = QK_NOPE + QK_ROPE
QK_PAD = 256
Q_LORA = 384
KV_LORA = 256
MS_COLS = 768
ROPE_THETA = 10000.0
EPS = 1e-6
ATT_SCALE = QK_DIM ** -0.5
LOG2E = 1.4426950408889634
LN2 = 0.6931471805599453
Q_PRESCALE = ATT_SCALE * LOG2E

ADAM_LR = 0.001
ADAM_B1 = 0.9
ADAM_B2 = 0.999
ADAM_EPS = 1e-08
ADAM_WD = 0.01
ADAM_STEP = 10

N_CHIPS = 4
N_DEV = 8
W_IN_COLS = 7872
W_IN_BLK = W_IN_COLS // N_CHIPS
REST_ROWS = 144 + 128 + 3 * 256
SMALL_COLS = 7168

TM_MM = 1024
TM_FUSED = 256
HG_ROWS = 128
TQ = 512
FLASH_HEADS = 2
HG_HEADS = 4
VMEM_LIMIT = 56 * 1024 * 1024


def _dot(a, b):
    return lax.dot_general(a, b, (((1,), (0,)), ((), ())), preferred_element_type=F32)


def _dot_nt(a, b):
    return lax.dot_general(a, b, (((1,), (1,)), ((), ())), preferred_element_type=F32)


def _dot_tn(a, b):
    return lax.dot_general(a, b, (((0,), (0,)), ((), ())), preferred_element_type=F32)


def _params(n_axes):
    return pltpu.CompilerParams(dimension_semantics=("arbitrary",) * n_axes, vmem_limit_bytes=VMEM_LIMIT)


def _rms(x, g):
    r = lax.rsqrt(jnp.mean(x * x, axis=-1, keepdims=True) + EPS)
    return x * r * g


def _rms_bwd(x, g, dy):
    r = lax.rsqrt(jnp.mean(x * x, axis=-1, keepdims=True) + EPS)
    xh = x * r
    dyg = dy * g
    dx = r * (dyg - xh * jnp.mean(dyg * xh, axis=-1, keepdims=True))
    return dx, dy * xh


def _silu_parts(z):
    s = jax.nn.sigmoid(z)
    return z * s, s * (1.0 + z * (1.0 - s))


def _rope(x, c, sa, sb):
    return x * c + pltpu.roll(x, 32, 1) * sa + pltpu.roll(x, 96, 1) * sb


def _rope_bwd(dy, c, sa, sb):
    return dy * c + pltpu.roll(dy * sa, 96, 1) + pltpu.roll(dy * sb, 32, 1)


def _rope_tables(seq):
    inv = ROPE_THETA ** (-jnp.arange(0, QK_ROPE, 2, dtype=F32) / QK_ROPE)
    ang = jnp.arange(seq, dtype=F32)[:, None] * inv[None, :]
    cos, sin = jnp.cos(ang), jnp.sin(ang)
    z32 = jnp.zeros_like(cos)
    z64 = jnp.zeros((seq, 64), F32)
    c = jnp.concatenate([cos, cos, z64], axis=1)
    sa = jnp.concatenate([z32, sin, z64], axis=1)
    sb = jnp.concatenate([-sin, z32, z64], axis=1)
    return c, sa, sb


def _mm_tn(a, b, *, name, tm=TM_MM, tn=1024):
    flat = a.ndim == 2
    if flat:
        a = a[None]
    g, m, k = a.shape
    n = b.shape[1]
    tm, tn = min(tm, m), min(tn, n)
    assert m % tm == 0 and n % tn == 0

    def body(a_ref, b_ref, o_ref):
        @pl.when(pl.program_id(2) == 0)
        def _():
            o_ref[...] = jnp.zeros_like(o_ref)

        o_ref[...] += _dot_tn(a_ref[...], b_ref[...])

    out = pl.pallas_call(
        body, name=name, grid=(g, n // tn, m // tm),
        in_specs=[pl.BlockSpec((None, tm, k), lambda s, j, i: (s, i, 0)),
                  pl.BlockSpec((tm, tn), lambda s, j, i: (i, j))],
        out_specs=pl.BlockSpec((None, k, tn), lambda s, j, i: (s, 0, j)),
        out_shape=jax.ShapeDtypeStruct((g, k, n), F32), compiler_params=_params(3),
    )(a, b)
    return out[0] if flat else out


def _chunk_rows(rows):
    return lax.broadcasted_iota(jnp.int32, (rows, HEAD_DIM), 0) & (HG_CHUNK - 1)


def _chunk_cumsum(x, rows):
    pos = _chunk_rows(rows)
    shift = 1
    while shift < HG_CHUNK:
        x = x + jnp.where(pos >= shift, pltpu.roll(x, shift, 0), 0.0)
        shift *= 2
    return x


def _chunk_revcumsum(x, rows):
    pos = _chunk_rows(rows)
    shift = 1
    while shift < HG_CHUNK:
        x = x + jnp.where(pos + shift < HG_CHUNK, pltpu.roll(x, rows - shift, 0), 0.0)
        shift *= 2
    return x


def _lower_bound(lbl):
    mx = jnp.maximum(lbl[0:1, :], lbl[1:2, :])
    e0 = jnp.exp(lbl[0:1, :] - mx)
    e1 = jnp.exp(lbl[1:2, :] - mx)
    p0 = e0 / (e0 + e1)
    return p0, p0 * (e1 / (e0 + e1))


def _hg_masks(rows, nch, tmask_s, bdmask_s):
    r = lax.broadcasted_iota(jnp.int32, (rows, rows), 0)
    c = lax.broadcasted_iota(jnp.int32, (rows, rows), 1)
    tmask_s[...] = jnp.where(((r >> CHUNK_SHIFT) == (c >> CHUNK_SHIFT)) & (r >= c), 1.0, 0.0)
    r = lax.broadcasted_iota(jnp.int32, (rows, nch * HEAD_DIM), 0)
    c = lax.broadcasted_iota(jnp.int32, (rows, nch * HEAD_DIM), 1)
    bdmask_s[...] = jnp.where((r >> CHUNK_SHIFT) == (c >> HEAD_SHIFT), 1.0, 0.0).astype(BF16)


def _block_diag(x, nch, bdmask):
    return jnp.tile(x, (1, nch)) * bdmask


def _hgrn_fwd(hg, lb_logits, norm_g):
    s = hg.shape[1]
    rows = min(HG_ROWS, s)
    nblk = s // rows
    nch = rows // HG_CHUNK

    def body(hg_ref, lbl_ref, g_ref, o_ref, ya_ref, st0_ref, st_s, stall_s, tmask_s, bdmask_s):
        @pl.when(pl.program_id(1) == 0)
        def _():
            st_s[...] = jnp.zeros_like(st_s)
            _hg_masks(rows, nch, tmask_s, bdmask_s)

        bdmask = bdmask_s[...]
        tmask = tmask_s[...] > 0.5
        for hh in range(HG_HEADS):
            hc = slice(hh * HEAD_DIM, (hh + 1) * HEAD_DIM)
            hq = hg_ref[0, :, hc]
            hf = hg_ref[1, :, hc]
            hi = hg_ref[2, :, hc]
            hz = hg_ref[3, :, hc]
            lb, _ = _lower_bound(lbl_ref[:, hc])
            f = lb + (1.0 - lb) * jax.nn.sigmoid(hf)
            q = hq * jax.nn.sigmoid(hq)
            k = 1.0 - f
            logf = jnp.log(f)
            b = _chunk_cumsum(logf, rows)
            q_in = (q * jnp.exp(b)).astype(BF16)
            k_in = (k * jnp.exp(-b)).astype(BF16)
            k_out = (k * jnp.exp(_chunk_revcumsum(logf, rows) - logf)).astype(BF16)
            vb = hi.astype(BF16)

            sc = jnp.where(tmask, _dot_nt(q_in, k_in), 0.0)
            o_intra = _dot(sc.astype(BF16), vb)
            kvt = _dot_tn(vb, _block_diag(k_out, nch, bdmask))
            st = st_s[hh]
            st0_ref[hh] = st
            for c in range(nch):
                cols = slice(c * HEAD_DIM, (c + 1) * HEAD_DIM)
                last = (c + 1) * HG_CHUNK - 1
                stall_s[hh, :, cols] = st.astype(BF16)
                st = st * jnp.exp(b[last:last + 1, :]) + kvt[:, cols]
            st_s[hh] = st
            o = o_intra + _dot_nt(_block_diag(q_in, nch, bdmask), stall_s[hh])
            o_ref[:, hc] = o
            silu_z, _ = _silu_parts(hz)
            ya_ref[:, hc] = (_rms(o, g_ref[...]) * silu_z).astype(BF16)

    nh = HG_HEADS
    return pl.pallas_call(
        body, name="hgrn_fwd", grid=(HEADS // nh, nblk),
        in_specs=[pl.BlockSpec((4, rows, nh * HEAD_DIM), lambda h, i: (0, i, h)),
                  pl.BlockSpec((2, nh * HEAD_DIM), lambda h, i: (0, h)),
                  pl.BlockSpec((1, HEAD_DIM), lambda h, i: (0, 0))],
        out_specs=[pl.BlockSpec((rows, nh * HEAD_DIM), lambda h, i: (i, h)),
                   pl.BlockSpec((rows, nh * HEAD_DIM), lambda h, i: (i, h)),
                   pl.BlockSpec((nh, None, HEAD_DIM, HEAD_DIM), lambda h, i: (h, i, 0, 0))],
        out_shape=[jax.ShapeDtypeStruct((s, D_MODEL), F32), jax.ShapeDtypeStruct((s, D_MODEL), BF16),
                   jax.ShapeDtypeStruct((HEADS, nblk, HEAD_DIM, HEAD_DIM), F32)],
        scratch_shapes=[pltpu.VMEM((nh, HEAD_DIM, HEAD_DIM), F32), pltpu.VMEM((nh, HEAD_DIM, nch * HEAD_DIM), BF16),
                        pltpu.VMEM((rows, rows), F32), pltpu.VMEM((rows, nch * HEAD_DIM), BF16)],
        compiler_params=_params(2),
    )(hg, lb_logits, norm_g)


def _hgrn_bwd(hg, o_pre, dya, st0, lb_logits, norm_g):
    s = hg.shape[1]
    rows = min(HG_ROWS, s)
    nblk = s // rows
    nch = rows // HG_CHUNK

    def body(hg_ref, o_ref, dya_ref, st0_ref, lbl_ref, g_ref, dhg_ref, dlb_ref, dg_ref,
             dst_s, stp_s, stp_rows_s, dst_rows_s, dst_lane_s, dbl_s, tmask_s, bdmask_s):
        @pl.when(pl.program_id(1) == 0)
        def _():
            dst_s[...] = jnp.zeros_like(dst_s)
            dlb_ref[...] = jnp.zeros_like(dlb_ref)
            dg_ref[...] = jnp.zeros_like(dg_ref)
            _hg_masks(rows, nch, tmask_s, bdmask_s)

        bdmask = bdmask_s[...]
        tmask = tmask_s[...] > 0.5
        g = g_ref[...]
        for hh in range(HG_HEADS):
            hc = slice(hh * HEAD_DIM, (hh + 1) * HEAD_DIM)
            hq = hg_ref[0, :, hc]
            hf = hg_ref[1, :, hc]
            hi = hg_ref[2, :, hc]
            hz = hg_ref[3, :, hc]
            lb, _ = _lower_bound(lbl_ref[:, hc])
            sg = jax.nn.sigmoid(hf)
            f = lb + (1.0 - lb) * sg
            q, dsilu_q = _silu_parts(hq)
            k = 1.0 - f
            logf = jnp.log(f)
            b = _chunk_cumsum(logf, rows)
            eb = jnp.exp(b)
            enb = jnp.exp(-b)
            ebl = jnp.exp(_chunk_revcumsum(logf, rows) - logf)
            q_in32 = q * eb
            k_in32 = k * enb
            k_out32 = k * ebl
            q_in = q_in32.astype(BF16)
            k_in = k_in32.astype(BF16)
            k_out = k_out32.astype(BF16)
            vb = hi.astype(BF16)
            kbd = _block_diag(k_out, nch, bdmask)
            qbd = _block_diag(q_in, nch, bdmask)
            decs = [jnp.exp(b[(c + 1) * HG_CHUNK - 1:(c + 1) * HG_CHUNK, :]) for c in range(nch)]

            kvt = _dot_tn(vb, kbd)
            st = st0_ref[hh]
            for c in range(nch):
                stp_s[hh, c] = st
                stp_rows_s[hh, c * HEAD_DIM:(c + 1) * HEAD_DIM, :] = st.astype(BF16)
                st = st * decs[c] + kvt[:, c * HEAD_DIM:(c + 1) * HEAD_DIM]

            o = o_ref[:, hc]
            rstd = lax.rsqrt(jnp.mean(o * o, axis=-1, keepdims=True) + EPS)
            oh = o * rstd
            silu_z, dsilu_z = _silu_parts(hz)
            dya_v = dya_ref[:, hc]
            dn = dya_v * silu_z
            dhz = dya_v * (oh * g) * dsilu_z
            dg_ref[hh] += jnp.sum(dn * oh, axis=0, keepdims=True)
            doh = dn * g
            do = (rstd * (doh - oh * jnp.mean(doh * oh, axis=-1, keepdims=True))).astype(BF16)

            dq_all = _dot_tn(do, qbd)
            dst = dst_s[hh]
            ddecs = [None] * nch
            for c in reversed(range(nch)):
                dstb = dst.astype(BF16)
                dst_lane_s[hh, :, c * HEAD_DIM:(c + 1) * HEAD_DIM] = dstb
                dst_rows_s[hh, c * HEAD_DIM:(c + 1) * HEAD_DIM, :] = dstb
                ddecs[c] = jnp.sum(dst * stp_s[hh, c], axis=0, keepdims=True) * decs[c]
                dst = dst * decs[c] + dq_all[:, c * HEAD_DIM:(c + 1) * HEAD_DIM]
            dst_s[hh] = dst

            sc = jnp.where(tmask, _dot_nt(q_in, k_in), 0.0).astype(BF16)
            dkout = _dot(_block_diag(vb, nch, bdmask), dst_rows_s[hh])
            dv = _dot_nt(kbd, dst_lane_s[hh]) + _dot_tn(sc, do)
            dsc = jnp.where(tmask, _dot_nt(do, vb), 0.0).astype(BF16)
            dqin = _dot(dsc, k_in) + _dot(_block_diag(do, nch, bdmask), stp_rows_s[hh])
            dkin = _dot_tn(dsc, q_in)

            dko = dkout * k_out32
            for c in range(nch):
                sl = slice(c * HG_CHUNK, (c + 1) * HG_CHUNK)
                dbl = jnp.sum(dko[sl], axis=0, keepdims=True) + ddecs[c]
                dbl_s[hh, sl, :] = jnp.broadcast_to(dbl, (HG_CHUNK, HEAD_DIM))
            dq = dqin * eb
            dk = dkin * enb + dkout * ebl
            db = dqin * q_in32 - dkin * k_in32 - dko
            dlogf = _chunk_revcumsum(db, rows) + dbl_s[hh]
            df = dlogf / f - dk
            dlb_ref[:, hc] += jnp.sum(df * (1.0 - sg), axis=0, keepdims=True)
            dhg_ref[0, :, hc] = (dq * dsilu_q).astype(BF16)
            dhg_ref[1, :, hc] = (df * (1.0 - lb) * sg * (1.0 - sg)).astype(BF16)
            dhg_ref[2, :, hc] = dv.astype(BF16)
            dhg_ref[3, :, hc] = dhz.astype(BF16)

    last = nblk - 1
    nh = HG_HEADS
    wide = nh * HEAD_DIM
    return pl.pallas_call(
        body, name="hgrn_bwd", grid=(HEADS // nh, nblk),
        in_specs=[pl.BlockSpec((4, rows, wide), lambda h, i: (0, last - i, h)),
                  pl.BlockSpec((rows, wide), lambda h, i: (last - i, h)),
                  pl.BlockSpec((rows, wide), lambda h, i: (last - i, h)),
                  pl.BlockSpec((nh, None, HEAD_DIM, HEAD_DIM), lambda h, i: (h, last - i, 0, 0)),
                  pl.BlockSpec((2, wide), lambda h, i: (0, h)),
                  pl.BlockSpec((1, HEAD_DIM), lambda h, i: (0, 0))],
        out_specs=[pl.BlockSpec((4, rows, wide), lambda h, i: (0, last - i, h)),
                   pl.BlockSpec((1, wide), lambda h, i: (0, h)),
                   pl.BlockSpec((nh, 1, HEAD_DIM), lambda h, i: (h, 0, 0))],
        out_shape=[jax.ShapeDtypeStruct((4, s, D_MODEL), BF16), jax.ShapeDtypeStruct((1, D_MODEL), F32),
                   jax.ShapeDtypeStruct((HEADS, 1, HEAD_DIM), F32)],
        scratch_shapes=[pltpu.VMEM((nh, HEAD_DIM, HEAD_DIM), F32), pltpu.VMEM((nh, nch, HEAD_DIM, HEAD_DIM), F32),
                        pltpu.VMEM((nh, nch * HEAD_DIM, HEAD_DIM), BF16), pltpu.VMEM((nh, nch * HEAD_DIM, HEAD_DIM), BF16),
                        pltpu.VMEM((nh, HEAD_DIM, nch * HEAD_DIM), BF16), pltpu.VMEM((nh, rows, HEAD_DIM), F32),
                        pltpu.VMEM((rows, rows), F32), pltpu.VMEM((rows, nch * HEAD_DIM), BF16)],
        compiler_params=_params(2),
    )(hg, o_pre, dya, st0, lb_logits, norm_g)


def _mla_pre(ms, q_a_g, kv_a_g, wuq3, wukv3, tabs):
    s = ms.shape[0]
    tm = min(TM_FUSED, s)

    def body(ms_ref, qg_ref, kvg_ref, wuq_ref, wukv_ref, c_ref, sa_ref, sb_ref,
             q_ref, k_ref, v_ref, cqn_ref, ckvn_ref):
        c, sa, sb = c_ref[...], sa_ref[...], sb_ref[...]
        cqn = _rms(ms_ref[:, 0:Q_LORA], qg_ref[...]).astype(BF16)
        ckvn = _rms(ms_ref[:, Q_LORA:Q_LORA + KV_LORA], kvg_ref[...]).astype(BF16)
        cqn_ref[...] = cqn
        ckvn_ref[...] = ckvn
        k_pe = _rope(ms_ref[:, Q_LORA + KV_LORA:MS_COLS], c, sa, sb).astype(BF16)
        for h in range(HEADS):
            qh = _dot(cqn, wuq_ref[h])
            q_ref[h, :, 0:128] = (qh[:, 0:128] * Q_PRESCALE).astype(BF16)
            q_ref[h, :, 128:256] = (_rope(qh[:, 128:256], c, sa, sb) * Q_PRESCALE).astype(BF16)
            kvh = _dot(ckvn, wukv_ref[h])
            k_ref[h, :, 0:128] = kvh[:, 0:128].astype(BF16)
            k_ref[h, :, 128:256] = k_pe
            v_ref[h] = kvh[:, 128:256].astype(BF16)

    tab = pl.BlockSpec((tm, 128), lambda i: (i, 0))
    return pl.pallas_call(
        body, name="mla_pre", grid=(s // tm,),
        in_specs=[pl.BlockSpec((tm, MS_COLS), lambda i: (i, 0)),
                  pl.BlockSpec((1, Q_LORA), lambda i: (0, 0)), pl.BlockSpec((1, KV_LORA), lambda i: (0, 0)),
                  pl.BlockSpec((HEADS, Q_LORA, QK_PAD), lambda i: (0, 0, 0)),
                  pl.BlockSpec((HEADS, KV_LORA, 256), lambda i: (0, 0, 0)), tab, tab, tab],
        out_specs=[pl.BlockSpec((HEADS, tm, QK_PAD), lambda i: (0, i, 0)),
                   pl.BlockSpec((HEADS, tm, QK_PAD), lambda i: (0, i, 0)),
                   pl.BlockSpec((HEADS, tm, HEAD_DIM), lambda i: (0, i, 0)),
                   pl.BlockSpec((tm, Q_LORA), lambda i: (i, 0)), pl.BlockSpec((tm, KV_LORA), lambda i: (i, 0))],
        out_shape=[jax.ShapeDtypeStruct((HEADS, s, QK_PAD), BF16), jax.ShapeDtypeStruct((HEADS, s, QK_PAD), BF16),
                   jax.ShapeDtypeStruct((HEADS, s, HEAD_DIM), BF16),
                   jax.ShapeDtypeStruct((s, Q_LORA), BF16), jax.ShapeDtypeStruct((s, KV_LORA), BF16)],
        compiler_params=_params(1),
    )(ms, q_a_g, kv_a_g, wuq3, wukv3, *tabs)


def _causal_mask(t):
    r = lax.broadcasted_iota(jnp.int32, (t, t), 0)
    c = lax.broadcasted_iota(jnp.int32, (t, t), 1)
    return r >= c


def _flash_fwd(q, k, v, mz):
    s = q.shape[1]
    t = min(TQ, s)

    def body(q_ref, k_ref, v_ref, mz_ref, o_ref, yb_ref, lse_ref, m_s, l_s, acc_s):
        i = pl.program_id(1)
        m_s[...] = jnp.full_like(m_s, -jnp.inf)
        l_s[...] = jnp.zeros_like(l_s)
        acc_s[...] = jnp.zeros_like(acc_s)

        def step(j, groups):
            rows = pl.ds(pl.multiple_of(j * t, t), t)
            for hh in range(FLASH_HEADS):
                for r0, nr, masked in groups:
                    r = slice(r0, r0 + nr)
                    sc = _dot_nt(q_ref[hh, r, :], k_ref[hh, rows, :])
                    if masked:
                        sc = jnp.where(_causal_mask(t), sc, -jnp.inf)
                    m_prev = m_s[hh, r, :]
                    m_new = jnp.maximum(m_prev, jnp.max(sc, axis=-1, keepdims=True))
                    p = jnp.exp2(sc - jnp.tile(m_new, (1, t // 128)))
                    alpha = jnp.exp2(m_prev - m_new)
                    l_s[hh, r, :] = alpha * l_s[hh, r, :] + jnp.sum(p, axis=-1, keepdims=True)
                    acc_s[hh, r, :] = alpha * acc_s[hh, r, :] + _dot(p.astype(BF16), v_ref[hh, rows, :])
                    m_s[hh, r, :] = m_new

        def loop_body(j, carry):
            step(j, ((0, 2 * t, False),))
            return carry

        lax.fori_loop(0, 2 * i, loop_body, 0)
        step(2 * i, ((0, t, True), (t, t, False)))
        step(2 * i + 1, ((t, t, True),))
        for hh in range(FLASH_HEADS):
            cols = slice(hh * HEAD_DIM, (hh + 1) * HEAD_DIM)
            out = acc_s[hh] / l_s[hh]
            o_ref[:, cols] = out
            silu_z, _ = _silu_parts(mz_ref[:, cols])
            yb_ref[:, cols] = (out * silu_z).astype(BF16)
            lse_ref[hh] = m_s[hh] + jnp.log2(l_s[hh])

    nh = FLASH_HEADS
    t2 = 2 * t
    col = pl.BlockSpec((t2, nh * HEAD_DIM), lambda h, i: (i, h))
    return pl.pallas_call(
        body, name="flash_fwd", grid=(HEADS // nh, s // t2),
        in_specs=[pl.BlockSpec((nh, t2, QK_PAD), lambda h, i: (h, i, 0)),
                  pl.BlockSpec((nh, s, QK_PAD), lambda h, i: (h, 0, 0)),
                  pl.BlockSpec((nh, s, HEAD_DIM), lambda h, i: (h, 0, 0)), col],
        out_specs=[col, col, pl.BlockSpec((nh, t2, 128), lambda h, i: (h, i, 0))],
        out_shape=[jax.ShapeDtypeStruct((s, D_MODEL), F32), jax.ShapeDtypeStruct((s, D_MODEL), BF16),
                   jax.ShapeDtypeStruct((HEADS, s, 128), F32)],
        scratch_shapes=[pltpu.VMEM((nh, t2, 128), F32), pltpu.VMEM((nh, t2, 128), F32),
                        pltpu.VMEM((nh, t2, HEAD_DIM), F32)],
        compiler_params=_params(2),
    )(q, k, v, mz)


def _flash_bwd(q, k, v, dyb, mz, o_att, lse, tabs):
    s = q.shape[1]
    t = min(TQ, s)

    def body(q_ref, k_ref, v_ref, dyb_ref, mz_ref, o_ref, lse_ref, c_ref, sa_ref, sb_ref,
             dq_ref, dk_ref, dv_ref, dmz_ref, dq_s, delta_s, do_s):
        i = pl.program_id(1)

        @pl.when(i == 0)
        def _():
            dk_ref[...] = jnp.zeros_like(dk_ref)
            dv_ref[...] = jnp.zeros_like(dv_ref)

        silu_z, dsilu_z = _silu_parts(mz_ref[...])
        dyb_v = dyb_ref[...]
        out = o_ref[...]
        do32 = dyb_v * silu_z
        dmz_ref[...] = (dyb_v * out * dsilu_z).astype(BF16)
        delta_s[...] = jnp.broadcast_to(jnp.sum(do32 * out, axis=-1, keepdims=True), (2 * t, 128))
        do_s[...] = do32.astype(BF16)
        dq_s[...] = jnp.zeros_like(dq_s)

        def step(j, modes):
            rows = pl.ds(pl.multiple_of(j * t, t), t)
            kj = k_ref[rows, :]
            vj = v_ref[rows, :]
            dv_acc = None
            dk_acc = None
            for ch, masked in enumerate(modes):
                if masked is None:
                    continue
                r = slice(ch * t, (ch + 1) * t)
                qv = q_ref[r, :]
                do = do_s[r, :]
                sc = _dot_nt(qv, kj)
                if masked:
                    sc = jnp.where(_causal_mask(t), sc, -jnp.inf)
                p = jnp.exp2(sc - jnp.tile(lse_ref[r, :], (1, t // 128)))
                dp = _dot_nt(do, vj)
                ds = (p * (dp - jnp.tile(delta_s[r, :], (1, t // 128)))).astype(BF16)
                dv_c = _dot_tn(p.astype(BF16), do)
                dk_c = _dot_tn(ds, qv)
                dv_acc = dv_c if dv_acc is None else dv_acc + dv_c
                dk_acc = dk_c if dk_acc is None else dk_acc + dk_c
                dq_s[r, :] += _dot(ds, kj)
            dv_ref[rows, :] += dv_acc
            dk_ref[rows, :] += dk_acc

        def loop_body(j, carry):
            step(j, (False, False))
            return carry

        lax.fori_loop(0, 2 * i, loop_body, 0)
        step(2 * i, (True, False))
        step(2 * i + 1, (None, True))
        dq = dq_s[...] * ATT_SCALE
        dq_ref[:, 0:128] = dq[:, 0:128].astype(BF16)
        dq_ref[:, 128:256] = _rope_bwd(dq[:, 128:256], c_ref[...], sa_ref[...], sb_ref[...]).astype(BF16)

    t2 = 2 * t
    col = pl.BlockSpec((t2, HEAD_DIM), lambda h, i: (i, h))
    tab = pl.BlockSpec((t2, 128), lambda h, i: (i, 0))
    return pl.pallas_call(
        body, name="flash_bwd", grid=(HEADS, s // t2),
        in_specs=[pl.BlockSpec((None, t2, QK_PAD), lambda h, i: (h, i, 0)),
                  pl.BlockSpec((None, s, QK_PAD), lambda h, i: (h, 0, 0)),
                  pl.BlockSpec((None, s, HEAD_DIM), lambda h, i: (h, 0, 0)),
                  col, col, col, pl.BlockSpec((None, t2, 128), lambda h, i: (h, i, 0)), tab, tab, tab],
        out_specs=[pl.BlockSpec((None, t2, QK_PAD), lambda h, i: (h, i, 0)),
                   pl.BlockSpec((None, s, QK_PAD), lambda h, i: (h, 0, 0)),
                   pl.BlockSpec((None, s, HEAD_DIM), lambda h, i: (h, 0, 0)), col],
        out_shape=[jax.ShapeDtypeStruct((HEADS, s, QK_PAD), BF16), jax.ShapeDtypeStruct((HEADS, s, QK_PAD), F32),
                   jax.ShapeDtypeStruct((HEADS, s, HEAD_DIM), F32), jax.ShapeDtypeStruct((s, D_MODEL), BF16)],
        scratch_shapes=[pltpu.VMEM((t2, QK_PAD), F32), pltpu.VMEM((t2, 128), F32), pltpu.VMEM((t2, HEAD_DIM), BF16)],
        compiler_params=_params(2),
    )(q, k, v, dyb, mz, o_att, lse, *tabs)


def _mla_bwd_proj(dq, dk, dv, cqn, ckvn, ms, q_a_g, kv_a_g, wuq3, wukv3, tabs):
    s = ms.shape[0]
    tm = min(TM_FUSED, s)

    def body(dq_ref, dk_ref, dv_ref, cqn_ref, ckvn_ref, ms_ref, qg_ref, kvg_ref, wuq_ref, wukv_ref,
             c_ref, sa_ref, sb_ref, dms_ref, dwuq_ref, dwukv_ref, dqg_ref, dkvg_ref):
        @pl.when(pl.program_id(0) == 0)
        def _():
            dwuq_ref[...] = jnp.zeros_like(dwuq_ref)
            dwukv_ref[...] = jnp.zeros_like(dwukv_ref)
            dqg_ref[...] = jnp.zeros_like(dqg_ref)
            dkvg_ref[...] = jnp.zeros_like(dkvg_ref)

        cqn = cqn_ref[...]
        ckvn = ckvn_ref[...]
        dcqn = jnp.zeros((tm, Q_LORA), F32)
        dckvn = jnp.zeros((tm, KV_LORA), F32)
        dkpe = jnp.zeros((tm, 128), F32)
        for h in range(HEADS):
            dqh = dq_ref[h]
            dcqn += _dot_nt(dqh, wuq_ref[h])
            dwuq_ref[h] += _dot_tn(cqn, dqh)
            dkh = dk_ref[h] * LN2
            dkvh = jnp.concatenate([dkh[:, 0:128], dv_ref[h]], axis=1).astype(BF16)
            dckvn += _dot_nt(dkvh, wukv_ref[h])
            dwukv_ref[h] += _dot_tn(ckvn, dkvh)
            dkpe += dkh[:, 128:256]
        dcq, dqg_rows = _rms_bwd(ms_ref[:, 0:Q_LORA], qg_ref[...], dcqn)
        dckv, dkvg_rows = _rms_bwd(ms_ref[:, Q_LORA:Q_LORA + KV_LORA], kvg_ref[...], dckvn)
        dqg_ref[...] += jnp.sum(dqg_rows, axis=0, keepdims=True)
        dkvg_ref[...] += jnp.sum(dkvg_rows, axis=0, keepdims=True)
        dms_ref[:, 0:Q_LORA] = dcq.astype(BF16)
        dms_ref[:, Q_LORA:Q_LORA + KV_LORA] = dckv.astype(BF16)
        dms_ref[:, Q_LORA + KV_LORA:MS_COLS] = _rope_bwd(dkpe, c_ref[...], sa_ref[...], sb_ref[...]).astype(BF16)

    tab = pl.BlockSpec((tm, 128), lambda i: (i, 0))
    wq = pl.BlockSpec((HEADS, Q_LORA, QK_PAD), lambda i: (0, 0, 0))
    wkv = pl.BlockSpec((HEADS, KV_LORA, 256), lambda i: (0, 0, 0))
    qg = pl.BlockSpec((1, Q_LORA), lambda i: (0, 0))
    kvg = pl.BlockSpec((1, KV_LORA), lambda i: (0, 0))
    return pl.pallas_call(
        body, name="mla_bwd_proj", grid=(s // tm,),
        in_specs=[pl.BlockSpec((HEADS, tm, QK_PAD), lambda i: (0, i, 0)),
                  pl.BlockSpec((HEADS, tm, QK_PAD), lambda i: (0, i, 0)),
                  pl.BlockSpec((HEADS, tm, HEAD_DIM), lambda i: (0, i, 0)),
                  pl.BlockSpec((tm, Q_LORA), lambda i: (i, 0)), pl.BlockSpec((tm, KV_LORA), lambda i: (i, 0)),
                  pl.BlockSpec((tm, MS_COLS), lambda i: (i, 0)), qg, kvg, wq, wkv, tab, tab, tab],
        out_specs=[pl.BlockSpec((tm, MS_COLS), lambda i: (i, 0)), wq, wkv, qg, kvg],
        out_shape=[jax.ShapeDtypeStruct((s, MS_COLS), BF16), jax.ShapeDtypeStruct((HEADS, Q_LORA, QK_PAD), F32),
                   jax.ShapeDtypeStruct((HEADS, KV_LORA, 256), F32),
                   jax.ShapeDtypeStruct((1, Q_LORA), F32), jax.ShapeDtypeStruct((1, KV_LORA), F32)],
        compiler_params=_params(1),
    )(dq, dk, dv, cqn, ckvn, ms, q_a_g, kv_a_g, wuq3, wukv3, *tabs)


def _merge_fused(ya, yb, glog, b_gate, x, tgt, fg, wproj):
    s = x.shape[0]
    tm = min(TM_FUSED, s)

    def body(ya_ref, yb_ref, g0_ref, g1_ref, b0_ref, b1_ref, x_ref, t_ref, fg_ref, w_ref,
             mg_ref, dx2_ref, dx2b_ref, dya_ref, dyb_ref, dgl_ref, dpa_ref, dpb_ref, loss_ref, dfg_ref, dbg_ref):
        @pl.when(pl.program_id(0) == 0)
        def _():
            loss_ref[...] = jnp.zeros_like(loss_ref)
            dfg_ref[...] = jnp.zeros_like(dfg_ref)
            dbg_ref[...] = jnp.zeros_like(dbg_ref)

        pa = _dot(ya_ref[...], w_ref[0])
        pb = _dot(yb_ref[...], w_ref[1])
        g0 = jax.nn.sigmoid(g0_ref[...] + b0_ref[...])
        g1 = jax.nn.sigmoid(g1_ref[...] + b1_ref[...])
        merged = (g0 * pa + g1 * pb).astype(BF16)
        mg_ref[...] = merged
        x2 = x_ref[...] + _dot(merged, w_ref[2])
        fg_v = fg_ref[...]
        err = _rms(x2, fg_v) - t_ref[...]
        loss_ref[...] += 0.5 * jnp.sum(jnp.mean(err * err, axis=-1, keepdims=True), axis=0, keepdims=True)
        dx2, dfg_rows = _rms_bwd(x2, fg_v, err * (1.0 / D_MODEL))
        dx2_ref[...] = dx2
        dfg_ref[...] += jnp.sum(dfg_rows, axis=0, keepdims=True)

        dx2b = dx2.astype(BF16)
        dx2b_ref[...] = dx2b
        dmg = _dot_nt(dx2b, w_ref[2])
        dpa = (dmg * g0).astype(BF16)
        dpb = (dmg * g1).astype(BF16)
        dpa_ref[...] = dpa
        dpb_ref[...] = dpb
        dgl0 = dmg * pa * g0 * (1.0 - g0)
        dgl1 = dmg * pb * g1 * (1.0 - g1)
        dgl_ref[:, 0:D_MODEL] = dgl0.astype(BF16)
        dgl_ref[:, D_MODEL:2 * D_MODEL] = dgl1.astype(BF16)
        dbg_ref[:, 0:D_MODEL] += jnp.sum(dgl0, axis=0, keepdims=True)
        dbg_ref[:, D_MODEL:2 * D_MODEL] += jnp.sum(dgl1, axis=0, keepdims=True)
        dya_ref[...] = _dot_nt(dpa, w_ref[0])
        dyb_ref[...] = _dot_nt(dpb, w_ref[1])

    row = pl.BlockSpec((tm, D_MODEL), lambda i: (i, 0))
    row1 = pl.BlockSpec((tm, D_MODEL), lambda i: (i, 1))
    row2 = pl.BlockSpec((tm, 2 * D_MODEL), lambda i: (i, 0))
    vec = pl.BlockSpec((1, D_MODEL), lambda i: (0, 0))
    vec1 = pl.BlockSpec((1, D_MODEL), lambda i: (0, 1))
    vec2 = pl.BlockSpec((1, 2 * D_MODEL), lambda i: (0, 0))
    f32_rows = jax.ShapeDtypeStruct((s, D_MODEL), F32)
    bf16_rows = jax.ShapeDtypeStruct((s, D_MODEL), BF16)
    return pl.pallas_call(
        body, name="merge_fused", grid=(s // tm,),
        in_specs=[row, row, row, row1, vec, vec1, row, row, vec, pl.BlockSpec((3, D_MODEL, D_MODEL), lambda i: (0, 0, 0))],
        out_specs=[row, row, row, row, row, row2, row, row, pl.BlockSpec((1, 128), lambda i: (0, 0)), vec, vec2],
        out_shape=[bf16_rows, f32_rows, bf16_rows, f32_rows, f32_rows, jax.ShapeDtypeStruct((s, 2 * D_MODEL), BF16),
                   bf16_rows, bf16_rows, jax.ShapeDtypeStruct((1, 128), F32), jax.ShapeDtypeStruct((1, D_MODEL), F32),
                   jax.ShapeDtypeStruct((1, 2 * D_MODEL), F32)],
        compiler_params=_params(1),
    )(ya, yb, glog, glog, b_gate, b_gate, x, tgt, fg, wproj)


def _proj_fused(x, g, w_int):
    s = x.shape[0]
    tm = min(TM_FUSED, s)

    def body(x_ref, g_ref, w_hbm, h_ref, hg_ref, ms_ref, mz_ref, gl_ref, w_s, sem):
        @pl.when(pl.program_id(0) == 0)
        def _():
            cp = pltpu.make_async_copy(w_hbm, w_s, sem)
            cp.start()
            cp.wait()

        h = _rms(x_ref[...], g_ref[...]).astype(BF16)
        h_ref[...] = h
        for j in range(4):
            hg_ref[j] = _dot_nt(h, w_s[j * D_MODEL:(j + 1) * D_MODEL, :])
        ms = _dot_nt(h, w_s[4096:4096 + MS_COLS, :])
        lane = lax.broadcasted_iota(jnp.int32, ms.shape, 1)
        ms_ref[...] = jnp.where(lane < 704, ms, 0.0)
        mz_ref[...] = _dot_nt(h, w_s[4800:5824, :])
        for j in range(2):
            gl_ref[:, j * D_MODEL:(j + 1) * D_MODEL] = _dot_nt(h, w_s[5824 + j * D_MODEL:5824 + (j + 1) * D_MODEL, :])

    row = pl.BlockSpec((tm, D_MODEL), lambda i: (i, 0))
    return pl.pallas_call(
        body, name="proj_fused", grid=(s // tm,),
        in_specs=[row, pl.BlockSpec((1, D_MODEL), lambda i: (0, 0)), pl.BlockSpec(memory_space=pl.ANY)],
        out_specs=[row, pl.BlockSpec((4, tm, D_MODEL), lambda i: (0, i, 0)), pl.BlockSpec((tm, MS_COLS), lambda i: (i, 0)),
                   row, pl.BlockSpec((tm, 2 * D_MODEL), lambda i: (i, 0))],
        out_shape=[jax.ShapeDtypeStruct((s, D_MODEL), BF16), jax.ShapeDtypeStruct((4, s, D_MODEL), F32),
                   jax.ShapeDtypeStruct((s, MS_COLS), F32), jax.ShapeDtypeStruct((s, D_MODEL), F32),
                   jax.ShapeDtypeStruct((s, 2 * D_MODEL), F32)],
        scratch_shapes=[pltpu.VMEM(w_int.shape, BF16), pltpu.SemaphoreType.DMA], compiler_params=_params(1),
    )(x, g, w_int)


def _dh_fused(dhg, dms, dmz, dglog, w_int, x, g, dx2, hw, hr):
    s = x.shape[0]
    tm = min(512, s)
    nw, nr = hw.shape[0] // N_CHIPS, hr.shape[0] // N_CHIPS

    def body(dhg_ref, dms_ref, dmz_ref, dgl_ref, w_hbm, x_ref, g_ref, dx2_ref, hw_ref, hr_ref,
             dx_ref, dg_ref, lw_ref, lr_ref, w_s, sem, send_sems, recv_sems):
        def scatter_copies():
            mx, my, mc = _me()
            return [pltpu.make_async_remote_copy(
                src_ref=src.at[pl.ds((2 * cx + cy) * n, n), :], dst_ref=dst.at[j], send_sem=send_sems.at[3 * a + j],
                recv_sem=recv_sems.at[3 * a + j], device_id=(cx, cy, mc), device_id_type=MESH_ID)
                for a, (src, dst, n) in enumerate([(hw_ref, lw_ref, nw), (hr_ref, lr_ref, nr)])
                for j, (cx, cy) in enumerate(_other_chips(mx, my))]

        @pl.when(pl.program_id(0) == 0)
        def _():
            for cp in scatter_copies():
                cp.start()
            dg_ref[...] = jnp.zeros_like(dg_ref)
            cp = pltpu.make_async_copy(w_hbm, w_s, sem)
            cp.start()
            cp.wait()

        dh = _dot(dms_ref[...], w_s[4096:4096 + MS_COLS, :]) + _dot(dmz_ref[...], w_s[4800:5824, :])
        for j in range(4):
            dh += _dot(dhg_ref[j], w_s[j * D_MODEL:(j + 1) * D_MODEL, :])
        for j in range(2):
            dh += _dot(dgl_ref[:, j * D_MODEL:(j + 1) * D_MODEL], w_s[5824 + j * D_MODEL:5824 + (j + 1) * D_MODEL, :])
        dx, dg_rows = _rms_bwd(x_ref[...], g_ref[...], dh)
        dx_ref[...] = dx + dx2_ref[...]
        dg_ref[...] += jnp.sum(dg_rows, axis=0, keepdims=True)

        @pl.when(pl.program_id(0) == s // tm - 1)
        def _():
            for cp in scatter_copies():
                cp.wait()

    row = pl.BlockSpec((tm, D_MODEL), lambda i: (i, 0))
    vec = pl.BlockSpec((1, D_MODEL), lambda i: (0, 0))
    return pl.pallas_call(
        body, name="dh_fused", grid=(s // tm,),
        in_specs=[pl.BlockSpec((4, tm, D_MODEL), lambda i: (0, i, 0)), pl.BlockSpec((tm, MS_COLS), lambda i: (i, 0)), row,
                  pl.BlockSpec((tm, 2 * D_MODEL), lambda i: (i, 0)), ANY, row, vec, row, ANY, ANY],
        out_specs=[row, vec, ANY, ANY],
        out_shape=[jax.ShapeDtypeStruct((s, D_MODEL), F32), jax.ShapeDtypeStruct((1, D_MODEL), F32),
                   jax.ShapeDtypeStruct((3, nw, HALF_COLS), hw.dtype), jax.ShapeDtypeStruct((3, nr, HALF_COLS), hr.dtype)],
        scratch_shapes=[pltpu.VMEM(w_int.shape, BF16), pltpu.SemaphoreType.DMA,
                        pltpu.SemaphoreType.DMA((6,)), pltpu.SemaphoreType.DMA((6,))],
        compiler_params=_params(1),
    )(dhg, dms, dmz, dglog, w_int, x, g, dx2, hw, hr)


def _local_step(x, tgt, w_int, w_uq, w_ukv, wproj, norm_g, b_gate, lb_logits, hg_norm_g, q_a_g, kv_a_g, fg):
    s = x.shape[0]
    wuq3 = jnp.pad(w_uq.reshape(Q_LORA, HEADS, QK_DIM).transpose(1, 0, 2), ((0, 0), (0, 0), (0, QK_PAD - QK_DIM)))
    wukv3 = w_ukv.reshape(KV_LORA, HEADS, 256).transpose(1, 0, 2)
    tabs = _rope_tables(s)

    h, hg, ms, mz, glog = _proj_fused(x, norm_g, w_int)
    o_pre, ya, st0 = _hgrn_fwd(hg, lb_logits, hg_norm_g)
    q, k, v, cqn, ckvn = _mla_pre(ms, q_a_g, kv_a_g, wuq3, wukv3, tabs)
    o_att, yb, lse = _flash_fwd(q, k, v, mz)
    merged, dx2, dx2b, dya, dyb, dglog, dpa, dpb, loss, dfg, dbg = _merge_fused(ya, yb, glog, b_gate, x, tgt, fg, wproj)

    d_wout = _mm_tn(merged, dx2b, name="dw_out")
    d_wpa = _mm_tn(ya, dpa, name="dw_proj_a")
    d_wpb = _mm_tn(yb, dpb, name="dw_proj_b")
    dhg, dlb, dhgg = _hgrn_bwd(hg, o_pre, dya, st0, lb_logits, hg_norm_g)
    dq, dk, dv, dmz = _flash_bwd(q, k, v, dyb, mz, o_att, lse, tabs)
    dms, d_wuq3, d_wukv3, dqg, dkvg = _mla_bwd_proj(dq, dk, dv, cqn, ckvn, ms, q_a_g, kv_a_g, wuq3, wukv3, tabs)
    d_hg = _mm_tn(dhg, h, name="dw_in_hg")
    d_ms = _mm_tn(dms, h, name="dw_in_ms")
    d_mz = _mm_tn(dmz, h, name="dw_in_mz")
    d_gl = _mm_tn(dglog, h, name="dw_in_gate")
    d_w_int = jnp.concatenate([d_hg.reshape(4 * D_MODEL, D_MODEL), d_ms[0:704], d_mz, d_gl], axis=0)
    small = {"b_gate": dbg, "lb": dlb, "hg_norm_g": dhgg, "q_a_g": dqg, "kv_a_g": dkvg, "final_norm_g": dfg}
    dh_args = (dhg, dms, dmz, dglog, w_int, x, norm_g, dx2)
    return loss, dh_args, d_w_int, d_wuq3, d_wukv3, (d_wpa, d_wpb, d_wout), small


def _pack_rest(w_uq_b, w_ukv_b, wpa_b, wpb_b, wout_b):
    return jnp.concatenate([w_uq_b.reshape(144, D_MODEL), w_ukv_b.reshape(128, D_MODEL), wpa_b, wpb_b, wout_b], axis=0)


def _unpack_rest(p):
    return (p[0:144].reshape(Q_LORA, 384), p[144:272].reshape(KV_LORA, 512), p[272:528], p[528:784], p[784:1040])


def _pack_rest_grads(d_wuq3, d_wukv3, d_proj):
    d_wuq = d_wuq3.transpose(1, 0, 2)[:, :, 0:QK_DIM].reshape(Q_LORA, HEADS * QK_DIM)
    d_wukv = d_wukv3.transpose(1, 0, 2).reshape(KV_LORA, HEADS * 256)
    blocks = []
    for b in range(N_CHIPS):
        rows = slice(b * 256, (b + 1) * 256)
        blocks.append(_pack_rest(d_wuq[:, b * 384:(b + 1) * 384], d_wukv[:, b * 512:(b + 1) * 512],
                                 d_proj[0][rows], d_proj[1][rows], d_proj[2][rows]))
    return jnp.stack(blocks, axis=0)


def _unpack_rest_weights(g):
    parts = [_unpack_rest(g[b]) for b in range(N_CHIPS)]
    w_uq, w_ukv = (jnp.concatenate([p[n] for p in parts], axis=1) for n in range(2))
    wproj = jnp.stack([jnp.concatenate([p[n] for p in parts], axis=0) for n in range(2, 5)], axis=0)
    return w_uq, w_ukv, wproj


MESH_ID = pl.DeviceIdType.MESH
ANY = pl.BlockSpec(memory_space=pl.ANY)
HALF_COLS = D_MODEL // 2


def _me():
    return lax.axis_index("x"), lax.axis_index("y"), lax.axis_index("c")


def _other_chips(x, y):
    return [(1 - x, y), (x, 1 - y), (1 - x, 1 - y)]


def _cols(c):
    return pl.ds(c * HALF_COLS, HALF_COLS)


def _gather_weights(w_blk, r_blk):
    def body(w_ref, r_ref, ow_ref, or_ref, send_sems, recv_sems):
        x, y, c = _me()
        chips = _other_chips(x, y)
        me = 2 * x + y
        pairs = [(w_ref, ow_ref), (r_ref, or_ref)]

        def copy(k, src, dst, to):
            return pltpu.make_async_remote_copy(src_ref=src, dst_ref=dst, send_sem=send_sems.at[k],
                                                recv_sem=recv_sems.at[k], device_id=to, device_id_type=MESH_ID)

        first =[copy(6 * a + j, src.at[:, _cols(c)], dst.at[me, :, _cols(c)], (cx, cy, c))
                 for a, (src, dst) in enumerate(pairs) for j, (cx, cy) in enumerate(chips)]
        for cp in first:
            cp.start()
        passed = []
        for a, (src, dst) in enumerate(pairs):
            for j, (cx, cy) in enumerate(chips):
                landed = dst.at[2 * cx + cy, :, _cols(c)]
                copy(6 * a + j, landed, landed, (cx, cy, c)).wait_recv()
                fwd = copy(6 * a + 3 + j, landed, landed, (x, y, 1 - c))
                fwd.start()
                passed.append(fwd)
        for a, (src, dst) in enumerate(pairs):
            for j, (cx, cy) in enumerate(chips):
                theirs = dst.at[2 * cx + cy, :, _cols(1 - c)]
                copy(6 * a + 3 + j, theirs, theirs, (x, y, 1 - c)).wait_recv()
        for cp in first + passed:
            cp.wait_send()

    gw, gr = pl.pallas_call(
        body, name="gather_weights", in_specs=[ANY, ANY], out_specs=[ANY, ANY],
        out_shape=[jax.ShapeDtypeStruct((N_CHIPS,) + w_blk.shape, w_blk.dtype),
                   jax.ShapeDtypeStruct((N_CHIPS,) + r_blk.shape, r_blk.dtype)],
        scratch_shapes=[pltpu.SemaphoreType.DMA((12,)), pltpu.SemaphoreType.DMA((12,))],
    )(w_blk, r_blk)
    chip = 2 * lax.axis_index("x") + lax.axis_index("y")
    return (lax.dynamic_update_slice(gw, w_blk[None], (chip, 0, 0)),
            lax.dynamic_update_slice(gr, r_blk[None], (chip, 0, 0)))


def _swap_halves(gw, gr):
    def body(gw_ref, gr_ref, lw_ref, lr_ref, send_sems, recv_sems):
        x, y, c = _me()
        cps = [pltpu.make_async_remote_copy(
            src_ref=src, dst_ref=dst, send_sem=send_sems.at[a], recv_sem=recv_sems.at[a],
            device_id=(x, y, 1 - c), device_id_type=MESH_ID)
            for a, (src, dst) in enumerate([(gw_ref.at[:, _cols(1 - c)], lw_ref),
                                            (gr_ref.at[:, :, _cols(1 - c)], lr_ref)])]
        for cp in cps:
            cp.start()
        for cp in cps:
            cp.wait()

    return pl.pallas_call(
        body, name="grad_swap_halves", in_specs=[ANY, ANY], out_specs=[ANY, ANY],
        out_shape=[jax.ShapeDtypeStruct((gw.shape[0], HALF_COLS), gw.dtype),
                   jax.ShapeDtypeStruct(gr.shape[:2] + (HALF_COLS,), gr.dtype)],
        scratch_shapes=[pltpu.SemaphoreType.DMA((2,)), pltpu.SemaphoreType.DMA((2,))],
    )(gw, gr)


def _swap_reduced(rw, rr):
    def body(rw_ref, rr_ref, ow_ref, or_ref, send_sems, recv_sems):
        x, y, c = _me()
        cps = [pltpu.make_async_remote_copy(
            src_ref=src, dst_ref=dst, send_sem=send_sems.at[a], recv_sem=recv_sems.at[a],
            device_id=(x, y, 1 - c), device_id_type=MESH_ID)
            for a, (src, dst) in enumerate([(rw_ref, ow_ref), (rr_ref, or_ref)])]
        for cp in cps:
            cp.start()
        for cp in cps:
            cp.wait()

    return pl.pallas_call(
        body, name="grad_swap_reduced", in_specs=[ANY, ANY], out_specs=[ANY, ANY],
        out_shape=[jax.ShapeDtypeStruct(rw.shape, rw.dtype), jax.ShapeDtypeStruct(rr.shape, rr.dtype)],
        scratch_shapes=[pltpu.SemaphoreType.DMA((2,)), pltpu.SemaphoreType.DMA((2,))],
    )(rw, rr)


def _join_cols(mine, theirs):
    first = lax.axis_index("c") == 0
    return jnp.concatenate([jnp.where(first, mine, theirs), jnp.where(first, theirs, mine)], axis=1)


def _gather_small(vec):
    def body(v_ref, out_ref, send_sems, recv_sems, local_sem):
        x, y, c = _me()
        my_id = 4 * x + 2 * y + c
        mine = pltpu.make_async_copy(v_ref, out_ref.at[my_id], local_sem)
        mine.start()
        cps = []
        for r in range(1, N_DEV):
            peer = (x ^ (r >> 2), y ^ ((r >> 1) & 1), c ^ (r & 1))
            cps.append(pltpu.make_async_remote_copy(
                src_ref=v_ref, dst_ref=out_ref.at[my_id], send_sem=send_sems.at[r - 1],
                recv_sem=recv_sems.at[r - 1], device_id=peer, device_id_type=MESH_ID))
        for cp in cps:
            cp.start()
        for cp in cps:
            cp.wait()
        mine.wait()

    return pl.pallas_call(
        body, name="gather_small", in_specs=[ANY], out_specs=ANY,
        out_shape=jax.ShapeDtypeStruct((N_DEV, 1, SMALL_COLS), vec.dtype),
        scratch_shapes=[pltpu.SemaphoreType.DMA((N_DEV - 1,)), pltpu.SemaphoreType.DMA((N_DEV - 1,)),
                        pltpu.SemaphoreType.DMA],
    )(vec)


def _add_cores(c_idx, g, landed, *, tm, name):
    r = g.shape[0]

    def body(c_ref, g_ref, l_ref, o32_ref, o16_ref):
        acc = g_ref[...] + l_ref[...]
        o32_ref[...] = acc
        o16_ref[...] = acc.astype(BF16)

    half = pl.BlockSpec((tm, HALF_COLS), lambda i, c_ref: (i, 0))
    grid_spec = pltpu.PrefetchScalarGridSpec(
        num_scalar_prefetch=1, grid=(r // tm,),
        in_specs=[pl.BlockSpec((tm, HALF_COLS), lambda i, c_ref: (i, c_ref[0])), half], out_specs=[half, half])
    return pl.pallas_call(
        body, name=name, grid_spec=grid_spec,
        out_shape=[jax.ShapeDtypeStruct((r, HALF_COLS), F32), jax.ShapeDtypeStruct((r, HALF_COLS), BF16)],
        compiler_params=_params(1),
    )(c_idx, g, landed)


def _add_chips(chip_idx, h32, landed, *, tm, name):
    n = landed.shape[1]
    per = n // tm

    def body(chip_ref, h_ref, l_ref, o_ref):
        acc = h_ref[...]
        for j in range(3):
            acc = acc + l_ref[j].astype(F32)
        o_ref[...] = acc

    grid_spec = pltpu.PrefetchScalarGridSpec(
        num_scalar_prefetch=1, grid=(per,),
        in_specs=[pl.BlockSpec((tm, HALF_COLS), lambda i, chip_ref: (chip_ref[0] * per + i, 0)),
                  pl.BlockSpec((3, tm, HALF_COLS), lambda i, chip_ref: (0, i, 0))],
        out_specs=pl.BlockSpec((tm, HALF_COLS), lambda i, chip_ref: (i, 0)))
    return pl.pallas_call(
        body, name=name, grid_spec=grid_spec, out_shape=jax.ShapeDtypeStruct((n, HALF_COLS), F32),
        compiler_params=_params(1),
    )(chip_idx, h32, landed)


def _pack_small(small, lb_logits, loss):
    def body(ng_ref, bg_ref, dlb_ref, lbl_ref, hgg_ref, qg_ref, kvg_ref, fg_ref, loss_ref, out_ref):
        out_ref[...] = jnp.zeros_like(out_ref)
        out_ref[:, 0:1024] = ng_ref[...]
        out_ref[:, 1024:3072] = bg_ref[...]
        _, p0p1 = _lower_bound(lbl_ref[...])
        dl0 = dlb_ref[...] * p0p1
        out_ref[:, 3072:4096] = dl0
        out_ref[:, 4096:5120] = -dl0
        hgg = hgg_ref[0]
        for h in range(1, HEADS):
            hgg = hgg + hgg_ref[h]
        out_ref[:, 5120:5248] = hgg
        out_ref[:, 5248:5632] = qg_ref[...]
        out_ref[:, 5632:5888] = kvg_ref[...]
        out_ref[:, 5888:6912] = fg_ref[...]
        out_ref[:, 6912:7040] = loss_ref[...]

    return pl.pallas_call(
        body, name="pack_small", out_shape=jax.ShapeDtypeStruct((1, SMALL_COLS), F32),
    )(small["norm_g"], small["b_gate"], small["lb"], lb_logits, small["hg_norm_g"], small["q_a_g"],
      small["kv_a_g"], small["final_norm_g"], loss)


def _adamw_math(w, g, m, v):
    nm = ADAM_B1 * m + (1.0 - ADAM_B1) * g
    nv = ADAM_B2 * v + (1.0 - ADAM_B2) * (g * g)
    m_hat = nm / (1.0 - ADAM_B1 ** ADAM_STEP)
    v_hat = nv / (1.0 - ADAM_B2 ** ADAM_STEP)
    return -ADAM_LR * (m_hat / (jnp.sqrt(v_hat) + ADAM_EPS) + ADAM_WD * w), nm, nv


def _adamw(w, g, m, v, *, name, tm):
    r, cols = w.shape

    def body(w_ref, g_ref, m_ref, v_ref, d_ref, nm_ref, nv_ref):
        d_ref[...], nm_ref[...], nv_ref[...] = _adamw_math(w_ref[...], g_ref[...], m_ref[...], v_ref[...])

    row = pl.BlockSpec((tm, cols), lambda i: (i, 0))
    shp = jax.ShapeDtypeStruct((r, cols), F32)
    return pl.pallas_call(
        body, name=name, grid=(r // tm,), in_specs=[row] * 4, out_specs=[row] * 3, out_shape=[shp] * 3,
        compiler_params=_params(1),
    )(w, g, m, v)


SMALL_SLOTS = (("norm_g", (0,)), ("b_gate", (1024,)), ("lb_logits", (3072, 4096)), ("hg_norm_g", (5120,)),
               ("q_a_g", (5248,)), ("kv_a_g", (5632,)), ("final_norm_g", (5888,)))
LOSS_SLOT = 6912


def _small_update(gathered, ws, ms, vs):
    n = len(SMALL_SLOTS)

    def body(*refs):
        g_ref = refs[0]
        w_refs, m_refs, v_refs = refs[1:1 + n], refs[1 + n:1 + 2 * n], refs[1 + 2 * n:1 + 3 * n]
        outs = refs[1 + 3 * n:]
        loss_ref = outs[0]
        g_out, d_out, nm_out, nv_out = (outs[1 + k * n:1 + (k + 1) * n] for k in range(4))
        total = g_ref[0]
        for dev in range(1, N_DEV):
            total = total + g_ref[dev]
        loss_ref[...] = total[:, LOSS_SLOT:LOSS_SLOT + 128]
        for p, (_, offsets) in enumerate(SMALL_SLOTS):
            cols = w_refs[p].shape[1]
            for r, off in enumerate(offsets):
                rows = slice(r, r + 1)
                g = total[:, off:off + cols]
                g_out[p][rows, :] = g
                d_out[p][rows, :], nm_out[p][rows, :], nv_out[p][rows, :] = _adamw_math(
                    w_refs[p][rows, :], g, m_refs[p][rows, :], v_refs[p][rows, :])

    shapes = [jax.ShapeDtypeStruct(w.shape, F32) for w in ws]
    res = pl.pallas_call(
        body, name="small_update", out_shape=[jax.ShapeDtypeStruct((1, 128), F32)] + shapes * 4,
    )(gathered, *ws, *ms, *vs)
    return res[0], res[1:1 + n], res[1 + n:1 + 2 * n], res[1 + 2 * n:1 + 3 * n], res[1 + 3 * n:1 + 4 * n]


def kernel(x, norm_g, w_in, b_gate, lb_logits, hg_norm_g, q_a_g, w_uq, kv_a_g, w_ukv, w_proj_a, w_proj_b, w_out, final_norm_g, loss_target, m_norm_g, m_w_in, m_b_gate, m_lb_logits, m_hg_norm_g, m_q_a_g, m_w_uq, m_kv_a_g, m_w_ukv, m_w_proj_a, m_w_proj_b, m_w_out, m_final_norm_g, v_norm_g, v_w_in, v_b_gate, v_lb_logits, v_hg_norm_g, v_q_a_g, v_w_uq, v_kv_a_g, v_w_ukv, v_w_proj_a, v_w_proj_b, v_w_out, v_final_norm_g):
    c_idx = lax.axis_index("c").astype(jnp.int32).reshape(1)
    chip_idx = (2 * lax.axis_index("x") + lax.axis_index("y")).astype(jnp.int32).reshape(1)

    w_blk = w_in[0].T.astype(BF16)
    r_blk = _pack_rest(w_uq[0], w_ukv[0], w_proj_a[0], w_proj_b[0], w_out[0]).astype(BF16)
    gw, gr = _gather_weights(w_blk, r_blk)
    fw_uq, fw_ukv, fwproj = _unpack_rest_weights(gr)

    loss, dh_args, d_w_int, d_wuq3, d_wukv3, d_proj, small = _local_step(
        x[0], loss_target[0], gw.reshape(W_IN_COLS, D_MODEL), fw_uq, fw_ukv, fwproj,
        norm_g, b_gate, lb_logits, hg_norm_g, q_a_g, kv_a_g, final_norm_g.reshape(1, D_MODEL))

    d_rest = _pack_rest_grads(d_wuq3, d_wukv3, d_proj)
    lw, lr = _swap_halves(d_w_int, d_rest)
    hw32, hw16 = _add_cores(c_idx, d_w_int, lw, tm=656, name="grad_add_cores_w")
    hr32, hr16 = _add_cores(c_idx, d_rest.reshape(N_CHIPS * REST_ROWS, D_MODEL), lr.reshape(N_CHIPS * REST_ROWS, HALF_COLS),
                            tm=REST_ROWS, name="grad_add_cores_r")
    grad_x, small["norm_g"], landed_w, landed_r = _dh_fused(*dh_args, hw16, hr16)
    rw = _add_chips(chip_idx, hw32, landed_w, tm=656, name="grad_add_chips_w")
    rr = _add_chips(chip_idx, hr32, landed_r, tm=208, name="grad_add_chips_r")
    tw, tr = _swap_reduced(rw, rr)
    g_w_in = _join_cols(rw, tw).T
    g_rest = _join_cols(rr, tr)
    g_uq, g_ukv, g_pa, g_pb, g_out = _unpack_rest(g_rest)

    small_all = _gather_small(_pack_small(small, lb_logits, loss))

    upd = {
        "w_in": _adamw(w_in[0], g_w_in, m_w_in[0], v_w_in[0], name="adamw_w_in", tm=128),
        "w_uq": _adamw(w_uq[0], g_uq, m_w_uq[0], v_w_uq[0], name="adamw_w_uq", tm=Q_LORA),
        "w_ukv": _adamw(w_ukv[0], g_ukv, m_w_ukv[0], v_w_ukv[0], name="adamw_w_ukv", tm=KV_LORA),
        "w_proj_a": _adamw(w_proj_a[0], g_pa, m_w_proj_a[0], v_w_proj_a[0], name="adamw_w_proj_a", tm=256),
        "w_proj_b": _adamw(w_proj_b[0], g_pb, m_w_proj_b[0], v_w_proj_b[0], name="adamw_w_proj_b", tm=256),
        "w_out": _adamw(w_out[0], g_out, m_w_out[0], v_w_out[0], name="adamw_w_out", tm=256),
    }
    loss_vec, *small_sets = _small_update(
        small_all,
        [norm_g, b_gate, lb_logits, hg_norm_g, q_a_g, kv_a_g, final_norm_g.reshape(1, D_MODEL)],
        [m_norm_g, m_b_gate, m_lb_logits, m_hg_norm_g, m_q_a_g, m_kv_a_g, m_final_norm_g.reshape(1, D_MODEL)],
        [v_norm_g, v_b_gate, v_lb_logits, v_hg_norm_g, v_q_a_g, v_kv_a_g, v_final_norm_g.reshape(1, D_MODEL)])

    def outputs(big, small_set):
        s_ng, s_bg, s_lb, s_hg, s_qg, s_kvg, s_fg = small_set
        return (s_ng, big["w_in"][None], s_bg, s_lb, s_hg, s_qg, big["w_uq"][None], s_kvg, big["w_ukv"][None],
                big["w_proj_a"][None], big["w_proj_b"][None], big["w_out"][None], s_fg.reshape(D_MODEL))

    grads = {"w_in": g_w_in, "w_uq": g_uq, "w_ukv": g_ukv, "w_proj_a": g_pa, "w_proj_b": g_pb, "w_out": g_out}
    return (loss_vec[0, 0], grad_x[None], *outputs(grads, small_sets[0]),
            *(o for k in range(3) for o in outputs({n: u[k] for n, u in upd.items()}, small_sets[1 + k])))
```

```python
import functools

import jax
import jax.numpy as jnp
from jax import lax
from jax.experimental import pallas as pl
from jax.experimental.pallas import tpu as pltpu

F32 = jnp.float32
BF16 = jnp.bfloat16

D_MODEL = 1024
HEADS = 8
HEAD_DIM = 128
HG_CHUNK = 32
CHUNK_SHIFT = 5
HEAD_SHIFT = 7
QK_NOPE = 128
QK_ROPE = 64
QK_DIM = QK_NOPE + QK_ROPE
QK_PAD = 256
Q_LORA = 384
KV_LORA = 256
MS_COLS = 768
ROPE_THETA = 10000.0
EPS = 1e-6
ATT_SCALE = QK_DIM ** -0.5
LOG2E = 1.4426950408889634
LN2 = 0.6931471805599453
Q_PRESCALE = ATT_SCALE * LOG2E

ADAM_LR = 0.001
ADAM_B1 = 0.9
ADAM_B2 = 0.999
ADAM_EPS = 1e-08
ADAM_WD = 0.01
ADAM_STEP = 10

N_CHIPS = 4
N_DEV = 8
W_IN_COLS = 7872
W_IN_BLK = W_IN_COLS // N_CHIPS
REST_ROWS = 144 + 128 + 3 * 256
SMALL_COLS = 7168

TM_MM = 1024
TM_FUSED = 256
HG_ROWS = 128
TQ = 512
FLASH_HEADS = 2
HG_HEADS = 8
VMEM_LIMIT = 56 * 1024 * 1024


def _dot(a, b):
    return lax.dot_general(a, b, (((1,), (0,)), ((), ())), preferred_element_type=F32)


def _dot_nt(a, b):
    return lax.dot_general(a, b, (((1,), (1,)), ((), ())), preferred_element_type=F32)


def _dot_tn(a, b):
    return lax.dot_general(a, b, (((0,), (0,)), ((), ())), preferred_element_type=F32)


def _params(n_axes):
    return pltpu.CompilerParams(dimension_semantics=("arbitrary",) * n_axes, vmem_limit_bytes=VMEM_LIMIT)


def _rms(x, g):
    r = lax.rsqrt(jnp.mean(x * x, axis=-1, keepdims=True) + EPS)
    return x * r * g


def _rms_bwd(x, g, dy):
    r = lax.rsqrt(jnp.mean(x * x, axis=-1, keepdims=True) + EPS)
    xh = x * r
    dyg = dy * g
    dx = r * (dyg - xh * jnp.mean(dyg * xh, axis=-1, keepdims=True))
    return dx, dy * xh


def _silu_parts(z):
    s = jax.nn.sigmoid(z)
    return z * s, s * (1.0 + z * (1.0 - s))


def _rope(x, c, sa, sb):
    return x * c + pltpu.roll(x, 32, 1) * sa + pltpu.roll(x, 96, 1) * sb


def _rope_bwd(dy, c, sa, sb):
    return dy * c + pltpu.roll(dy * sa, 96, 1) + pltpu.roll(dy * sb, 32, 1)


def _rope_tables(seq):
    inv = ROPE_THETA ** (-jnp.arange(0, QK_ROPE, 2, dtype=F32) / QK_ROPE)
    ang = jnp.arange(seq, dtype=F32)[:, None] * inv[None, :]
    cos, sin = jnp.cos(ang), jnp.sin(ang)
    z32 = jnp.zeros_like(cos)
    z64 = jnp.zeros((seq, 64), F32)
    c = jnp.concatenate([cos, cos, z64], axis=1)
    sa = jnp.concatenate([z32, sin, z64], axis=1)
    sb = jnp.concatenate([-sin, z32, z64], axis=1)
    return c, sa, sb


def _mm_tn(a, b, *, name, tm=TM_MM, tn=1024):
    flat = a.ndim == 2
    if flat:
        a = a[None]
    g, m, k = a.shape
    n = b.shape[1]
    tm, tn = min(tm, m), min(tn, n)
    assert m % tm == 0 and n % tn == 0

    def body(a_ref, b_ref, o_ref):
        @pl.when(pl.program_id(2) == 0)
        def _():
            o_ref[...] = jnp.zeros_like(o_ref)

        o_ref[...] += _dot_tn(a_ref[...], b_ref[...])

    out = pl.pallas_call(
        body, name=name, grid=(g, n // tn, m // tm),
        in_specs=[pl.BlockSpec((None, tm, k), lambda s, j, i: (s, i, 0)),
                  pl.BlockSpec((tm, tn), lambda s, j, i: (i, j))],
        out_specs=pl.BlockSpec((None, k, tn), lambda s, j, i: (s, 0, j)),
        out_shape=jax.ShapeDtypeStruct((g, k, n), F32), compiler_params=_params(3),
    )(a, b)
    return out[0] if flat else out


def _chunk_rows(rows):
    return lax.broadcasted_iota(jnp.int32, (rows, HEAD_DIM), 0) & (HG_CHUNK - 1)


def _chunk_cumsum(x, rows):
    pos = _chunk_rows(rows)
    shift = 1
    while shift < HG_CHUNK:
        x = x + jnp.where(pos >= shift, pltpu.roll(x, shift, 0), 0.0)
        shift *= 2
    return x


def _chunk_revcumsum(x, rows):
    pos = _chunk_rows(rows)
    shift = 1
    while shift < HG_CHUNK:
        x = x + jnp.where(pos + shift < HG_CHUNK, pltpu.roll(x, rows - shift, 0), 0.0)
        shift *= 2
    return x


def _lower_bound(lbl):
    mx = jnp.maximum(lbl[0:1, :], lbl[1:2, :])
    e0 = jnp.exp(lbl[0:1, :] - mx)
    e1 = jnp.exp(lbl[1:2, :] - mx)
    p0 = e0 / (e0 + e1)
    return p0, p0 * (e1 / (e0 + e1))


def _hg_masks(rows, nch, tmask_s, bdmask_s):
    r = lax.broadcasted_iota(jnp.int32, (rows, rows), 0)
    c = lax.broadcasted_iota(jnp.int32, (rows, rows), 1)
    tmask_s[...] = jnp.where(((r >> CHUNK_SHIFT) == (c >> CHUNK_SHIFT)) & (r >= c), 1.0, 0.0)
    r = lax.broadcasted_iota(jnp.int32, (rows, nch * HEAD_DIM), 0)
    c = lax.broadcasted_iota(jnp.int32, (rows, nch * HEAD_DIM), 1)
    bdmask_s[...] = jnp.where((r >> CHUNK_SHIFT) == (c >> HEAD_SHIFT), 1.0, 0.0).astype(BF16)


def _block_diag(x, nch, bdmask):
    return jnp.tile(x, (1, nch)) * bdmask


def _hgrn_fwd(hg, lb_logits, norm_g):
    s = hg.shape[1]
    rows = min(HG_ROWS, s)
    nblk = s // rows
    nch = rows // HG_CHUNK

    def body(hg_ref, lbl_ref, g_ref, o_ref, ya_ref, st0_ref, st_s, stall_s, tmask_s, bdmask_s):
        @pl.when(pl.program_id(1) == 0)
        def _():
            st_s[...] = jnp.zeros_like(st_s)
            _hg_masks(rows, nch, tmask_s, bdmask_s)

        bdmask = bdmask_s[...]
        tmask = tmask_s[...] > 0.5
        for hh in range(HG_HEADS):
            hc = slice(hh * HEAD_DIM, (hh + 1) * HEAD_DIM)
            hq = hg_ref[0, :, hc]
            hf = hg_ref[1, :, hc]
            hi = hg_ref[2, :, hc]
            hz = hg_ref[3, :, hc]
            lb, _ = _lower_bound(lbl_ref[:, hc])
            f = lb + (1.0 - lb) * jax.nn.sigmoid(hf)
            q = hq * jax.nn.sigmoid(hq)
            k = 1.0 - f
            logf = jnp.log(f)
            b = _chunk_cumsum(logf, rows)
            q_in = (q * jnp.exp(b)).astype(BF16)
            k_in = (k * jnp.exp(-b)).astype(BF16)
            k_out = (k * jnp.exp(_chunk_revcumsum(logf, rows) - logf)).astype(BF16)
            vb = hi.astype(BF16)

            sc = jnp.where(tmask, _dot_nt(q_in, k_in), 0.0)
            o_intra = _dot(sc.astype(BF16), vb)
            kvt = _dot_tn(vb, _block_diag(k_out, nch, bdmask))
            st = st_s[hh]
            st0_ref[hh] = st
            for c in range(nch):
                cols = slice(c * HEAD_DIM, (c + 1) * HEAD_DIM)
                last = (c + 1) * HG_CHUNK - 1
                stall_s[hh, :, cols] = st.astype(BF16)
                st = st * jnp.exp(b[last:last + 1, :]) + kvt[:, cols]
            st_s[hh] = st
            o = o_intra + _dot_nt(_block_diag(q_in, nch, bdmask), stall_s[hh])
            o_ref[:, hc] = o
            silu_z, _ = _silu_parts(hz)
            ya_ref[:, hc] = (_rms(o, g_ref[...]) * silu_z).astype(BF16)

    nh = HG_HEADS
    return pl.pallas_call(
        body, name="hgrn_fwd", grid=(HEADS // nh, nblk),
        in_specs=[pl.BlockSpec((4, rows, nh * HEAD_DIM), lambda h, i: (0, i, h)),
                  pl.BlockSpec((2, nh * HEAD_DIM), lambda h, i: (0, h)),
                  pl.BlockSpec((1, HEAD_DIM), lambda h, i: (0, 0))],
        out_specs=[pl.BlockSpec((rows, nh * HEAD_DIM), lambda h, i: (i, h)),
                   pl.BlockSpec((rows, nh * HEAD_DIM), lambda h, i: (i, h)),
                   pl.BlockSpec((nh, None, HEAD_DIM, HEAD_DIM), lambda h, i: (h, i, 0, 0))],
        out_shape=[jax.ShapeDtypeStruct((s, D_MODEL), F32), jax.ShapeDtypeStruct((s, D_MODEL), BF16),
                   jax.ShapeDtypeStruct((HEADS, nblk, HEAD_DIM, HEAD_DIM), F32)],
        scratch_shapes=[pltpu.VMEM((nh, HEAD_DIM, HEAD_DIM), F32), pltpu.VMEM((nh, HEAD_DIM, nch * HEAD_DIM), BF16),
                        pltpu.VMEM((rows, rows), F32), pltpu.VMEM((rows, nch * HEAD_DIM), BF16)],
        compiler_params=_params(2),
    )(hg, lb_logits, norm_g)


def _hgrn_bwd(hg, o_pre, dya, st0, lb_logits, norm_g):
    s = hg.shape[1]
    rows = min(HG_ROWS, s)
    nblk = s // rows
    nch = rows // HG_CHUNK

    def body(hg_ref, o_ref, dya_ref, st0_ref, lbl_ref, g_ref, dhg_ref, dlb_ref, dg_ref,
             dst_s, stp_s, stp_rows_s, dst_rows_s, dst_lane_s, dbl_s, tmask_s, bdmask_s):
        @pl.when(pl.program_id(1) == 0)
        def _():
            dst_s[...] = jnp.zeros_like(dst_s)
            dlb_ref[...] = jnp.zeros_like(dlb_ref)
            dg_ref[...] = jnp.zeros_like(dg_ref)
            _hg_masks(rows, nch, tmask_s, bdmask_s)

        bdmask = bdmask_s[...]
        tmask = tmask_s[...] > 0.5
        g = g_ref[...]
        for hh in range(HG_HEADS):
            hc = slice(hh * HEAD_DIM, (hh + 1) * HEAD_DIM)
            hq = hg_ref[0, :, hc]
            hf = hg_ref[1, :, hc]
            hi = hg_ref[2, :, hc]
            hz = hg_ref[3, :, hc]
            lb, _ = _lower_bound(lbl_ref[:, hc])
            sg = jax.nn.sigmoid(hf)
            f = lb + (1.0 - lb) * sg
            q, dsilu_q = _silu_parts(hq)
            k = 1.0 - f
            logf = jnp.log(f)
            b = _chunk_cumsum(logf, rows)
            eb = jnp.exp(b)
            enb = jnp.exp(-b)
            ebl = jnp.exp(_chunk_revcumsum(logf, rows) - logf)
            q_in32 = q * eb
            k_in32 = k * enb
            k_out32 = k * ebl
            q_in = q_in32.astype(BF16)
            k_in = k_in32.astype(BF16)
            k_out = k_out32.astype(BF16)
            vb = hi.astype(BF16)
            kbd = _block_diag(k_out, nch, bdmask)
            qbd = _block_diag(q_in, nch, bdmask)
            decs = [jnp.exp(b[(c + 1) * HG_CHUNK - 1:(c + 1) * HG_CHUNK, :]) for c in range(nch)]

            kvt = _dot_tn(vb, kbd)
            st = st0_ref[hh]
            for c in range(nch):
                stp_s[hh, c] = st
                stp_rows_s[hh, c * HEAD_DIM:(c + 1) * HEAD_DIM, :] = st.astype(BF16)
                st = st * decs[c] + kvt[:, c * HEAD_DIM:(c + 1) * HEAD_DIM]

            o = o_ref[:, hc]
            rstd = lax.rsqrt(jnp.mean(o * o, axis=-1, keepdims=True) + EPS)
            oh = o * rstd
            silu_z, dsilu_z = _silu_parts(hz)
            dya_v = dya_ref[:, hc]
            dn = dya_v * silu_z
            dhz = dya_v * (oh * g) * dsilu_z
            dg_ref[hh] += jnp.sum(dn * oh, axis=0, keepdims=True)
            doh = dn * g
            do = (rstd * (doh - oh * jnp.mean(doh * oh, axis=-1, keepdims=True))).astype(BF16)

            dq_all = _dot_tn(do, qbd)
            dst = dst_s[hh]
            ddecs = [None] * nch
            for c in reversed(range(nch)):
                dstb = dst.astype(BF16)
                dst_lane_s[hh, :, c * HEAD_DIM:(c + 1) * HEAD_DIM] = dstb
                dst_rows_s[hh, c * HEAD_DIM:(c + 1) * HEAD_DIM, :] = dstb
                ddecs[c] = jnp.sum(dst * stp_s[hh, c], axis=0, keepdims=True) * decs[c]
                dst = dst * decs[c] + dq_all[:, c * HEAD_DIM:(c + 1) * HEAD_DIM]
            dst_s[hh] = dst

            sc = jnp.where(tmask, _dot_nt(q_in, k_in), 0.0).astype(BF16)
            dkout = _dot(_block_diag(vb, nch, bdmask), dst_rows_s[hh])
            dv = _dot_nt(kbd, dst_lane_s[hh]) + _dot_tn(sc, do)
            dsc = jnp.where(tmask, _dot_nt(do, vb), 0.0).astype(BF16)
            dqin = _dot(dsc, k_in) + _dot(_block_diag(do, nch, bdmask), stp_rows_s[hh])
            dkin = _dot_tn(dsc, q_in)

            dko = dkout * k_out32
            for c in range(nch):
                sl = slice(c * HG_CHUNK, (c + 1) * HG_CHUNK)
                dbl = jnp.sum(dko[sl], axis=0, keepdims=True) + ddecs[c]
                dbl_s[hh, sl, :] = jnp.broadcast_to(dbl, (HG_CHUNK, HEAD_DIM))
            dq = dqin * eb
            dk = dkin * enb + dkout * ebl
            db = dqin * q_in32 - dkin * k_in32 - dko
            dlogf = _chunk_revcumsum(db, rows) + dbl_s[hh]
            df = dlogf / f - dk
            dlb_ref[:, hc] += jnp.sum(df * (1.0 - sg), axis=0, keepdims=True)
            dhg_ref[0, :, hc] = (dq * dsilu_q).astype(BF16)
            dhg_ref[1, :, hc] = (df * (1.0 - lb) * sg * (1.0 - sg)).astype(BF16)
            dhg_ref[2, :, hc] = dv.astype(BF16)
            dhg_ref[3, :, hc] = dhz.astype(BF16)

    last = nblk - 1
    nh = HG_HEADS
    wide = nh * HEAD_DIM
    return pl.pallas_call(
        body, name="hgrn_bwd", grid=(HEADS // nh, nblk),
        in_specs=[pl.BlockSpec((4, rows, wide), lambda h, i: (0, last - i, h)),
                  pl.BlockSpec((rows, wide), lambda h, i: (last - i, h)),
                  pl.BlockSpec((rows, wide), lambda h, i: (last - i, h)),
                  pl.BlockSpec((nh, None, HEAD_DIM, HEAD_DIM), lambda h, i: (h, last - i, 0, 0)),
                  pl.BlockSpec((2, wide), lambda h, i: (0, h)),
                  pl.BlockSpec((1, HEAD_DIM), lambda h, i: (0, 0))],
        out_specs=[pl.BlockSpec((4, rows, wide), lambda h, i: (0, last - i, h)),
                   pl.BlockSpec((1, wide), lambda h, i: (0, h)),
                   pl.BlockSpec((nh, 1, HEAD_DIM), lambda h, i: (h, 0, 0))],
        out_shape=[jax.ShapeDtypeStruct((4, s, D_MODEL), BF16), jax.ShapeDtypeStruct((1, D_MODEL), F32),
                   jax.ShapeDtypeStruct((HEADS, 1, HEAD_DIM), F32)],
        scratch_shapes=[pltpu.VMEM((nh, HEAD_DIM, HEAD_DIM), F32), pltpu.VMEM((nh, nch, HEAD_DIM, HEAD_DIM), F32),
                        pltpu.VMEM((nh, nch * HEAD_DIM, HEAD_DIM), BF16), pltpu.VMEM((nh, nch * HEAD_DIM, HEAD_DIM), BF16),
                        pltpu.VMEM((nh, HEAD_DIM, nch * HEAD_DIM), BF16), pltpu.VMEM((nh, rows, HEAD_DIM), F32),
                        pltpu.VMEM((rows, rows), F32), pltpu.VMEM((rows, nch * HEAD_DIM), BF16)],
        compiler_params=_params(2),
    )(hg, o_pre, dya, st0, lb_logits, norm_g)


def _mla_pre(ms, q_a_g, kv_a_g, wuq3, wukv3, tabs):
    s = ms.shape[0]
    tm = min(TM_FUSED, s)

    def body(ms_ref, qg_ref, kvg_ref, wuq_ref, wukv_ref, c_ref, sa_ref, sb_ref,
             q_ref, k_ref, v_ref, cqn_ref, ckvn_ref):
        c, sa, sb = c_ref[...], sa_ref[...], sb_ref[...]
        cqn = _rms(ms_ref[:, 0:Q_LORA], qg_ref[...]).astype(BF16)
        ckvn = _rms(ms_ref[:, Q_LORA:Q_LORA + KV_LORA], kvg_ref[...]).astype(BF16)
        cqn_ref[...] = cqn
        ckvn_ref[...] = ckvn
        k_pe = _rope(ms_ref[:, Q_LORA + KV_LORA:MS_COLS], c, sa, sb).astype(BF16)
        for h in range(HEADS):
            qh = _dot(cqn, wuq_ref[h])
            q_ref[h, :, 0:128] = (qh[:, 0:128] * Q_PRESCALE).astype(BF16)
            q_ref[h, :, 128:256] = (_rope(qh[:, 128:256], c, sa, sb) * Q_PRESCALE).astype(BF16)
            kvh = _dot(ckvn, wukv_ref[h])
            k_ref[h, :, 0:128] = kvh[:, 0:128].astype(BF16)
            k_ref[h, :, 128:256] = k_pe
            v_ref[h] = kvh[:, 128:256].astype(BF16)

    tab = pl.BlockSpec((tm, 128), lambda i: (i, 0))
    return pl.pallas_call(
        body, name="mla_pre", grid=(s // tm,),
        in_specs=[pl.BlockSpec((tm, MS_COLS), lambda i: (i, 0)),
                  pl.BlockSpec((1, Q_LORA), lambda i: (0, 0)), pl.BlockSpec((1, KV_LORA), lambda i: (0, 0)),
                  pl.BlockSpec((HEADS, Q_LORA, QK_PAD), lambda i: (0, 0, 0)),
                  pl.BlockSpec((HEADS, KV_LORA, 256), lambda i: (0, 0, 0)), tab, tab, tab],
        out_specs=[pl.BlockSpec((HEADS, tm, QK_PAD), lambda i: (0, i, 0)),
                   pl.BlockSpec((HEADS, tm, QK_PAD), lambda i: (0, i, 0)),
                   pl.BlockSpec((HEADS, tm, HEAD_DIM), lambda i: (0, i, 0)),
                   pl.BlockSpec((tm, Q_LORA), lambda i: (i, 0)), pl.BlockSpec((tm, KV_LORA), lambda i: (i, 0))],
        out_shape=[jax.ShapeDtypeStruct((HEADS, s, QK_PAD), BF16), jax.ShapeDtypeStruct((HEADS, s, QK_PAD), BF16),
                   jax.ShapeDtypeStruct((HEADS, s, HEAD_DIM), BF16),
                   jax.ShapeDtypeStruct((s, Q_LORA), BF16), jax.ShapeDtypeStruct((s, KV_LORA), BF16)],
        compiler_params=_params(1),
    )(ms, q_a_g, kv_a_g, wuq3, wukv3, *tabs)


def _causal_mask(t):
    r = lax.broadcasted_iota(jnp.int32, (t, t), 0)
    c = lax.broadcasted_iota(jnp.int32, (t, t), 1)
    return r >= c


def _flash_fwd(q, k, v, mz):
    s = q.shape[1]
    t = min(TQ, s)

    def body(q_ref, k_ref, v_ref, mz_ref, o_ref, yb_ref, lse_ref, m_s, l_s, acc_s):
        i = pl.program_id(1)
        m_s[...] = jnp.full_like(m_s, -jnp.inf)
        l_s[...] = jnp.zeros_like(l_s)
        acc_s[...] = jnp.zeros_like(acc_s)

        def step(j, groups):
            rows = pl.ds(pl.multiple_of(j * t, t), t)
            for hh in range(FLASH_HEADS):
                for r0, nr, masked in groups:
                    r = slice(r0, r0 + nr)
                    sc = _dot_nt(q_ref[hh, r, :], k_ref[hh, rows, :])
                    if masked:
                        sc = jnp.where(_causal_mask(t), sc, -jnp.inf)
                    m_prev = m_s[hh, r, :]
                    m_new = jnp.maximum(m_prev, jnp.max(sc, axis=-1, keepdims=True))
                    p = jnp.exp2(sc - jnp.tile(m_new, (1, t // 128)))
                    alpha = jnp.exp2(m_prev - m_new)
                    l_s[hh, r, :] = alpha * l_s[hh, r, :] + jnp.sum(p, axis=-1, keepdims=True)
                    acc_s[hh, r, :] = alpha * acc_s[hh, r, :] + _dot(p.astype(BF16), v_ref[hh, rows, :])
                    m_s[hh, r, :] = m_new

        def loop_body(j, carry):
            step(j, ((0, 2 * t, False),))
            return carry

        lax.fori_loop(0, 2 * i, loop_body, 0)
        step(2 * i, ((0, t, True), (t, t, False)))
        step(2 * i + 1, ((t, t, True),))
        for hh in range(FLASH_HEADS):
            cols = slice(hh * HEAD_DIM, (hh + 1) * HEAD_DIM)
            out = acc_s[hh] / l_s[hh]
            o_ref[:, cols] = out
            silu_z, _ = _silu_parts(mz_ref[:, cols])
            yb_ref[:, cols] = (out * silu_z).astype(BF16)
            lse_ref[hh] = m_s[hh] + jnp.log2(l_s[hh])

    nh = FLASH_HEADS
    t2 = 2 * t
    col = pl.BlockSpec((t2, nh * HEAD_DIM), lambda h, i: (i, h))
    return pl.pallas_call(
        body, name="flash_fwd", grid=(HEADS // nh, s // t2),
        in_specs=[pl.BlockSpec((nh, t2, QK_PAD), lambda h, i: (h, i, 0)),
                  pl.BlockSpec((nh, s, QK_PAD), lambda h, i: (h, 0, 0)),
                  pl.BlockSpec((nh, s, HEAD_DIM), lambda h, i: (h, 0, 0)), col],
        out_specs=[col, col, pl.BlockSpec((nh, t2, 128), lambda h, i: (h, i, 0))],
        out_shape=[jax.ShapeDtypeStruct((s, D_MODEL), F32), jax.ShapeDtypeStruct((s, D_MODEL), BF16),
                   jax.ShapeDtypeStruct((HEADS, s, 128), F32)],
        scratch_shapes=[pltpu.VMEM((nh, t2, 128), F32), pltpu.VMEM((nh, t2, 128), F32),
                        pltpu.VMEM((nh, t2, HEAD_DIM), F32)],
        compiler_params=_params(2),
    )(q, k, v, mz)


def _flash_bwd(q, k, v, dyb, mz, o_att, lse, tabs):
    s = q.shape[1]
    t = min(TQ, s)

    def body(q_ref, k_ref, v_ref, dyb_ref, mz_ref, o_ref, lse_ref, c_ref, sa_ref, sb_ref,
             dq_ref, dk_ref, dv_ref, dmz_ref, dq_s, delta_s, do_s):
        i = pl.program_id(1)

        @pl.when(i == 0)
        def _():
            dk_ref[...] = jnp.zeros_like(dk_ref)
            dv_ref[...] = jnp.zeros_like(dv_ref)

        silu_z, dsilu_z = _silu_parts(mz_ref[...])
        dyb_v = dyb_ref[...]
        out = o_ref[...]
        do32 = dyb_v * silu_z
        dmz_ref[...] = (dyb_v * out * dsilu_z).astype(BF16)
        delta_s[...] = jnp.broadcast_to(jnp.sum(do32 * out, axis=-1, keepdims=True), (2 * t, 128))
        do_s[...] = do32.astype(BF16)
        dq_s[...] = jnp.zeros_like(dq_s)

        def step(j, modes):
            rows = pl.ds(pl.multiple_of(j * t, t), t)
            kj = k_ref[rows, :]
            vj = v_ref[rows, :]
            dv_acc = None
            dk_acc = None
            for ch, masked in enumerate(modes):
                if masked is None:
                    continue
                r = slice(ch * t, (ch + 1) * t)
                qv = q_ref[r, :]
                do = do_s[r, :]
                sc = _dot_nt(qv, kj)
                if masked:
                    sc = jnp.where(_causal_mask(t), sc, -jnp.inf)
                p = jnp.exp2(sc - jnp.tile(lse_ref[r, :], (1, t // 128)))
                dp = _dot_nt(do, vj)
                ds = (p * (dp - jnp.tile(delta_s[r, :], (1, t // 128)))).astype(BF16)
                dv_c = _dot_tn(p.astype(BF16), do)
                dk_c = _dot_tn(ds, qv)
                dv_acc = dv_c if dv_acc is None else dv_acc + dv_c
                dk_acc = dk_c if dk_acc is None else dk_acc + dk_c
                dq_s[r, :] += _dot(ds, kj)
            dv_ref[rows, :] += dv_acc
            dk_ref[rows, :] += dk_acc

        def loop_body(j, carry):
            step(j, (False, False))
            return carry

        lax.fori_loop(0, 2 * i, loop_body, 0)
        step(2 * i, (True, False))
        step(2 * i + 1, (None, True))
        dq = dq_s[...] * ATT_SCALE
        dq_ref[:, 0:128] = dq[:, 0:128].astype(BF16)
        dq_ref[:, 128:256] = _rope_bwd(dq[:, 128:256], c_ref[...], sa_ref[...], sb_ref[...]).astype(BF16)

    t2 = 2 * t
    col = pl.BlockSpec((t2, HEAD_DIM), lambda h, i: (i, h))
    tab = pl.BlockSpec((t2, 128), lambda h, i: (i, 0))
    return pl.pallas_call(
        body, name="flash_bwd", grid=(HEADS, s // t2),
        in_specs=[pl.BlockSpec((None, t2, QK_PAD), lambda h, i: (h, i, 0)),
                  pl.BlockSpec((None, s, QK_PAD), lambda h, i: (h, 0, 0)),
                  pl.BlockSpec((None, s, HEAD_DIM), lambda h, i: (h, 0, 0)),
                  col, col, col, pl.BlockSpec((None, t2, 128), lambda h, i: (h, i, 0)), tab, tab, tab],
        out_specs=[pl.BlockSpec((None, t2, QK_PAD), lambda h, i: (h, i, 0)),
                   pl.BlockSpec((None, s, QK_PAD), lambda h, i: (h, 0, 0)),
                   pl.BlockSpec((None, s, HEAD_DIM), lambda h, i: (h, 0, 0)), col],
        out_shape=[jax.ShapeDtypeStruct((HEADS, s, QK_PAD), BF16), jax.ShapeDtypeStruct((HEADS, s, QK_PAD), F32),
                   jax.ShapeDtypeStruct((HEADS, s, HEAD_DIM), F32), jax.ShapeDtypeStruct((s, D_MODEL), BF16)],
        scratch_shapes=[pltpu.VMEM((t2, QK_PAD), F32), pltpu.VMEM((t2, 128), F32), pltpu.VMEM((t2, HEAD_DIM), BF16)],
        compiler_params=_params(2),
    )(q, k, v, dyb, mz, o_att, lse, *tabs)


def _mla_bwd_proj(dq, dk, dv, cqn, ckvn, ms, q_a_g, kv_a_g, wuq3, wukv3, tabs):
    s = ms.shape[0]
    tm = min(TM_FUSED, s)

    def body(dq_ref, dk_ref, dv_ref, cqn_ref, ckvn_ref, ms_ref, qg_ref, kvg_ref, wuq_ref, wukv_ref,
             c_ref, sa_ref, sb_ref, dms_ref, dwuq_ref, dwukv_ref, dqg_ref, dkvg_ref):
        @pl.when(pl.program_id(0) == 0)
        def _():
            dwuq_ref[...] = jnp.zeros_like(dwuq_ref)
            dwukv_ref[...] = jnp.zeros_like(dwukv_ref)
            dqg_ref[...] = jnp.zeros_like(dqg_ref)
            dkvg_ref[...] = jnp.zeros_like(dkvg_ref)

        cqn = cqn_ref[...]
        ckvn = ckvn_ref[...]
        dcqn = jnp.zeros((tm, Q_LORA), F32)
        dckvn = jnp.zeros((tm, KV_LORA), F32)
        dkpe = jnp.zeros((tm, 128), F32)
        for h in range(HEADS):
            dqh = dq_ref[h]
            dcqn += _dot_nt(dqh, wuq_ref[h])
            dwuq_ref[h] += _dot_tn(cqn, dqh)
            dkh = dk_ref[h] * LN2
            dkvh = jnp.concatenate([dkh[:, 0:128], dv_ref[h]], axis=1).astype(BF16)
            dckvn += _dot_nt(dkvh, wukv_ref[h])
            dwukv_ref[h] += _dot_tn(ckvn, dkvh)
            dkpe += dkh[:, 128:256]
        dcq, dqg_rows = _rms_bwd(ms_ref[:, 0:Q_LORA], qg_ref[...], dcqn)
        dckv, dkvg_rows = _rms_bwd(ms_ref[:, Q_LORA:Q_LORA + KV_LORA], kvg_ref[...], dckvn)
        dqg_ref[...] += jnp.sum(dqg_rows, axis=0, keepdims=True)
        dkvg_ref[...] += jnp.sum(dkvg_rows, axis=0, keepdims=True)
        dms_ref[:, 0:Q_LORA] = dcq.astype(BF16)
        dms_ref[:, Q_LORA:Q_LORA + KV_LORA] = dckv.astype(BF16)
        dms_ref[:, Q_LORA + KV_LORA:MS_COLS] = _rope_bwd(dkpe, c_ref[...], sa_ref[...], sb_ref[...]).astype(BF16)

    tab = pl.BlockSpec((tm, 128), lambda i: (i, 0))
    wq = pl.BlockSpec((HEADS, Q_LORA, QK_PAD), lambda i: (0, 0, 0))
    wkv = pl.BlockSpec((HEADS, KV_LORA, 256), lambda i: (0, 0, 0))
    qg = pl.BlockSpec((1, Q_LORA), lambda i: (0, 0))
    kvg = pl.BlockSpec((1, KV_LORA), lambda i: (0, 0))
    return pl.pallas_call(
        body, name="mla_bwd_proj", grid=(s // tm,),
        in_specs=[pl.BlockSpec((HEADS, tm, QK_PAD), lambda i: (0, i, 0)),
                  pl.BlockSpec((HEADS, tm, QK_PAD), lambda i: (0, i, 0)),
                  pl.BlockSpec((HEADS, tm, HEAD_DIM), lambda i: (0, i, 0)),
                  pl.BlockSpec((tm, Q_LORA), lambda i: (i, 0)), pl.BlockSpec((tm, KV_LORA), lambda i: (i, 0)),
                  pl.BlockSpec((tm, MS_COLS), lambda i: (i, 0)), qg, kvg, wq, wkv, tab, tab, tab],
        out_specs=[pl.BlockSpec((tm, MS_COLS), lambda i: (i, 0)), wq, wkv, qg, kvg],
        out_shape=[jax.ShapeDtypeStruct((s, MS_COLS), BF16), jax.ShapeDtypeStruct((HEADS, Q_LORA, QK_PAD), F32),
                   jax.ShapeDtypeStruct((HEADS, KV_LORA, 256), F32),
                   jax.ShapeDtypeStruct((1, Q_LORA), F32), jax.ShapeDtypeStruct((1, KV_LORA), F32)],
        compiler_params=_params(1),
    )(dq, dk, dv, cqn, ckvn, ms, q_a_g, kv_a_g, wuq3, wukv3, *tabs)


def _merge_fused(ya, yb, glog, b_gate, x, tgt, fg, wproj):
    s = x.shape[0]
    tm = min(TM_FUSED, s)

    def body(ya_ref, yb_ref, g0_ref, g1_ref, b0_ref, b1_ref, x_ref, t_ref, fg_ref, w_ref,
             mg_ref, dx2_ref, dx2b_ref, dya_ref, dyb_ref, dgl_ref, dpa_ref, dpb_ref, loss_ref, dfg_ref, dbg_ref):
        @pl.when(pl.program_id(0) == 0)
        def _():
            loss_ref[...] = jnp.zeros_like(loss_ref)
            dfg_ref[...] = jnp.zeros_like(dfg_ref)
            dbg_ref[...] = jnp.zeros_like(dbg_ref)

        pa = _dot(ya_ref[...], w_ref[0])
        pb = _dot(yb_ref[...], w_ref[1])
        g0 = jax.nn.sigmoid(g0_ref[...] + b0_ref[...])
        g1 = jax.nn.sigmoid(g1_ref[...] + b1_ref[...])
        merged = (g0 * pa + g1 * pb).astype(BF16)
        mg_ref[...] = merged
        x2 = x_ref[...] + _dot(merged, w_ref[2])
        fg_v = fg_ref[...]
        err = _rms(x2, fg_v) - t_ref[...]
        loss_ref[...] += 0.5 * jnp.sum(jnp.mean(err * err, axis=-1, keepdims=True), axis=0, keepdims=True)
        dx2, dfg_rows = _rms_bwd(x2, fg_v, err * (1.0 / D_MODEL))
        dx2_ref[...] = dx2
        dfg_ref[...] += jnp.sum(dfg_rows, axis=0, keepdims=True)

        dx2b = dx2.astype(BF16)
        dx2b_ref[...] = dx2b
        dmg = _dot_nt(dx2b, w_ref[2])
        dpa = (dmg * g0).astype(BF16)
        dpb = (dmg * g1).astype(BF16)
        dpa_ref[...] = dpa
        dpb_ref[...] = dpb
        dgl0 = dmg * pa * g0 * (1.0 - g0)
        dgl1 = dmg * pb * g1 * (1.0 - g1)
        dgl_ref[:, 0:D_MODEL] = dgl0.astype(BF16)
        dgl_ref[:, D_MODEL:2 * D_MODEL] = dgl1.astype(BF16)
        dbg_ref[:, 0:D_MODEL] += jnp.sum(dgl0, axis=0, keepdims=True)
        dbg_ref[:, D_MODEL:2 * D_MODEL] += jnp.sum(dgl1, axis=0, keepdims=True)
        dya_ref[...] = _dot_nt(dpa, w_ref[0])
        dyb_ref[...] = _dot_nt(dpb, w_ref[1])

    row = pl.BlockSpec((tm, D_MODEL), lambda i: (i, 0))
    row1 = pl.BlockSpec((tm, D_MODEL), lambda i: (i, 1))
    row2 = pl.BlockSpec((tm, 2 * D_MODEL), lambda i: (i, 0))
    vec = pl.BlockSpec((1, D_MODEL), lambda i: (0, 0))
    vec1 = pl.BlockSpec((1, D_MODEL), lambda i: (0, 1))
    vec2 = pl.BlockSpec((1, 2 * D_MODEL), lambda i: (0, 0))
    f32_rows = jax.ShapeDtypeStruct((s, D_MODEL), F32)
    bf16_rows = jax.ShapeDtypeStruct((s, D_MODEL), BF16)
    return pl.pallas_call(
        body, name="merge_fused", grid=(s // tm,),
        in_specs=[row, row, row, row1, vec, vec1, row, row, vec, pl.BlockSpec((3, D_MODEL, D_MODEL), lambda i: (0, 0, 0))],
        out_specs=[row, row, row, row, row, row2, row, row, pl.BlockSpec((1, 128), lambda i: (0, 0)), vec, vec2],
        out_shape=[bf16_rows, f32_rows, bf16_rows, f32_rows, f32_rows, jax.ShapeDtypeStruct((s, 2 * D_MODEL), BF16),
                   bf16_rows, bf16_rows, jax.ShapeDtypeStruct((1, 128), F32), jax.ShapeDtypeStruct((1, D_MODEL), F32),
                   jax.ShapeDtypeStruct((1, 2 * D_MODEL), F32)],
        compiler_params=_params(1),
    )(ya, yb, glog, glog, b_gate, b_gate, x, tgt, fg, wproj)


def _proj_fused(x, g, w_int):
    s = x.shape[0]
    tm = min(TM_FUSED, s)

    def body(x_ref, g_ref, w_hbm, h_ref, hg_ref, ms_ref, mz_ref, gl_ref, w_s, sem):
        @pl.when(pl.program_id(0) == 0)
        def _():
            cp = pltpu.make_async_copy(w_hbm, w_s, sem)
            cp.start()
            cp.wait()

        h = _rms(x_ref[...], g_ref[...]).astype(BF16)
        h_ref[...] = h
        for j in range(4):
            hg_ref[j] = _dot_nt(h, w_s[j * D_MODEL:(j + 1) * D_MODEL, :])
        ms = _dot_nt(h, w_s[4096:4096 + MS_COLS, :])
        lane = lax.broadcasted_iota(jnp.int32, ms.shape, 1)
        ms_ref[...] = jnp.where(lane < 704, ms, 0.0)
        mz_ref[...] = _dot_nt(h, w_s[4800:5824, :])
        for j in range(2):
            gl_ref[:, j * D_MODEL:(j + 1) * D_MODEL] = _dot_nt(h, w_s[5824 + j * D_MODEL:5824 + (j + 1) * D_MODEL, :])

    row = pl.BlockSpec((tm, D_MODEL), lambda i: (i, 0))
    return pl.pallas_call(
        body, name="proj_fused", grid=(s // tm,),
        in_specs=[row, pl.BlockSpec((1, D_MODEL), lambda i: (0, 0)), pl.BlockSpec(memory_space=pl.ANY)],
        out_specs=[row, pl.BlockSpec((4, tm, D_MODEL), lambda i: (0, i, 0)), pl.BlockSpec((tm, MS_COLS), lambda i: (i, 0)),
                   row, pl.BlockSpec((tm, 2 * D_MODEL), lambda i: (i, 0))],
        out_shape=[jax.ShapeDtypeStruct((s, D_MODEL), BF16), jax.ShapeDtypeStruct((4, s, D_MODEL), F32),
                   jax.ShapeDtypeStruct((s, MS_COLS), F32), jax.ShapeDtypeStruct((s, D_MODEL), F32),
                   jax.ShapeDtypeStruct((s, 2 * D_MODEL), F32)],
        scratch_shapes=[pltpu.VMEM(w_int.shape, BF16), pltpu.SemaphoreType.DMA], compiler_params=_params(1),
    )(x, g, w_int)


def _dh_fused(dhg, dms, dmz, dglog, w_int, x, g, dx2, hw, hr):
    s = x.shape[0]
    tm = min(512, s)
    nw, nr = hw.shape[0] // N_CHIPS, hr.shape[0] // N_CHIPS

    def body(dhg_ref, dms_ref, dmz_ref, dgl_ref, w_hbm, x_ref, g_ref, dx2_ref, hw_ref, hr_ref,
             dx_ref, dg_ref, lw_ref, lr_ref, w_s, sem, send_sems, recv_sems):
        def scatter_copies():
            mx, my, mc = _me()
            return [pltpu.make_async_remote_copy(
                src_ref=src.at[pl.ds((2 * cx + cy) * n, n), :], dst_ref=dst.at[j], send_sem=send_sems.at[3 * a + j],
                recv_sem=recv_sems.at[3 * a + j], device_id=(cx, cy, mc), device_id_type=MESH_ID)
                for a, (src, dst, n) in enumerate([(hw_ref, lw_ref, nw), (hr_ref, lr_ref, nr)])
                for j, (cx, cy) in enumerate(_other_chips(mx, my))]

        @pl.when(pl.program_id(0) == 0)
        def _():
            for cp in scatter_copies():
                cp.start()
            dg_ref[...] = jnp.zeros_like(dg_ref)
            cp = pltpu.make_async_copy(w_hbm, w_s, sem)
            cp.start()
            cp.wait()

        dh = _dot(dms_ref[...], w_s[4096:4096 + MS_COLS, :]) + _dot(dmz_ref[...], w_s[4800:5824, :])
        for j in range(4):
            dh += _dot(dhg_ref[j], w_s[j * D_MODEL:(j + 1) * D_MODEL, :])
        for j in range(2):
            dh += _dot(dgl_ref[:, j * D_MODEL:(j + 1) * D_MODEL], w_s[5824 + j * D_MODEL:5824 + (j + 1) * D_MODEL, :])
        dx, dg_rows = _rms_bwd(x_ref[...], g_ref[...], dh)
        dx_ref[...] = dx + dx2_ref[...]
        dg_ref[...] += jnp.sum(dg_rows, axis=0, keepdims=True)

        @pl.when(pl.program_id(0) == s // tm - 1)
        def _():
            for cp in scatter_copies():
                cp.wait()

    row = pl.BlockSpec((tm, D_MODEL), lambda i: (i, 0))
    vec = pl.BlockSpec((1, D_MODEL), lambda i: (0, 0))
    return pl.pallas_call(
        body, name="dh_fused", grid=(s // tm,),
        in_specs=[pl.BlockSpec((4, tm, D_MODEL), lambda i: (0, i, 0)), pl.BlockSpec((tm, MS_COLS), lambda i: (i, 0)), row,
                  pl.BlockSpec((tm, 2 * D_MODEL), lambda i: (i, 0)), ANY, row, vec, row, ANY, ANY],
        out_specs=[row, vec, ANY, ANY],
        out_shape=[jax.ShapeDtypeStruct((s, D_MODEL), F32), jax.ShapeDtypeStruct((1, D_MODEL), F32),
                   jax.ShapeDtypeStruct((3, nw, HALF_COLS), hw.dtype), jax.ShapeDtypeStruct((3, nr, HALF_COLS), hr.dtype)],
        scratch_shapes=[pltpu.VMEM(w_int.shape, BF16), pltpu.SemaphoreType.DMA,
                        pltpu.SemaphoreType.DMA((6,)), pltpu.SemaphoreType.DMA((6,))],
        compiler_params=_params(1),
    )(dhg, dms, dmz, dglog, w_int, x, g, dx2, hw, hr)


def _local_step(x, tgt, w_int, w_uq, w_ukv, wproj, norm_g, b_gate, lb_logits, hg_norm_g, q_a_g, kv_a_g, fg):
    s = x.shape[0]
    wuq3 = jnp.pad(w_uq.reshape(Q_LORA, HEADS, QK_DIM).transpose(1, 0, 2), ((0, 0), (0, 0), (0, QK_PAD - QK_DIM)))
    wukv3 = w_ukv.reshape(KV_LORA, HEADS, 256).transpose(1, 0, 2)
    tabs = _rope_tables(s)

    h, hg, ms, mz, glog = _proj_fused(x, norm_g, w_int)
    o_pre, ya, st0 = _hgrn_fwd(hg, lb_logits, hg_norm_g)
    q, k, v, cqn, ckvn = _mla_pre(ms, q_a_g, kv_a_g, wuq3, wukv3, tabs)
    o_att, yb, lse = _flash_fwd(q, k, v, mz)
    merged, dx2, dx2b, dya, dyb, dglog, dpa, dpb, loss, dfg, dbg = _merge_fused(ya, yb, glog, b_gate, x, tgt, fg, wproj)

    d_wout = _mm_tn(merged, dx2b, name="dw_out")
    d_wpa = _mm_tn(ya, dpa, name="dw_proj_a")
    d_wpb = _mm_tn(yb, dpb, name="dw_proj_b")
    dhg, dlb, dhgg = _hgrn_bwd(hg, o_pre, dya, st0, lb_logits, hg_norm_g)
    dq, dk, dv, dmz = _flash_bwd(q, k, v, dyb, mz, o_att, lse, tabs)
    dms, d_wuq3, d_wukv3, dqg, dkvg = _mla_bwd_proj(dq, dk, dv, cqn, ckvn, ms, q_a_g, kv_a_g, wuq3, wukv3, tabs)
    d_hg = _mm_tn(dhg, h, name="dw_in_hg")
    d_ms = _mm_tn(dms, h, name="dw_in_ms")
    d_mz = _mm_tn(dmz, h, name="dw_in_mz")
    d_gl = _mm_tn(dglog, h, name="dw_in_gate")
    d_w_int = jnp.concatenate([d_hg.reshape(4 * D_MODEL, D_MODEL), d_ms[0:704], d_mz, d_gl], axis=0)
    small = {"b_gate": dbg, "lb": dlb, "hg_norm_g": dhgg, "q_a_g": dqg, "kv_a_g": dkvg, "final_norm_g": dfg}
    dh_args = (dhg, dms, dmz, dglog, w_int, x, norm_g, dx2)
    return loss, dh_args, d_w_int, d_wuq3, d_wukv3, (d_wpa, d_wpb, d_wout), small


def _pack_rest(w_uq_b, w_ukv_b, wpa_b, wpb_b, wout_b):
    return jnp.concatenate([w_uq_b.reshape(144, D_MODEL), w_ukv_b.reshape(128, D_MODEL), wpa_b, wpb_b, wout_b], axis=0)


def _unpack_rest(p):
    return (p[0:144].reshape(Q_LORA, 384), p[144:272].reshape(KV_LORA, 512), p[272:528], p[528:784], p[784:1040])


def _pack_rest_grads(d_wuq3, d_wukv3, d_proj):
    d_wuq = d_wuq3.transpose(1, 0, 2)[:, :, 0:QK_DIM].reshape(Q_LORA, HEADS * QK_DIM)
    d_wukv = d_wukv3.transpose(1, 0, 2).reshape(KV_LORA, HEADS * 256)
    blocks = []
    for b in range(N_CHIPS):
        rows = slice(b * 256, (b + 1) * 256)
        blocks.append(_pack_rest(d_wuq[:, b * 384:(b + 1) * 384], d_wukv[:, b * 512:(b + 1) * 512],
                                 d_proj[0][rows], d_proj[1][rows], d_proj[2][rows]))
    return jnp.stack(blocks, axis=0)


def _unpack_rest_weights(g):
    parts = [_unpack_rest(g[b]) for b in range(N_CHIPS)]
    w_uq, w_ukv = (jnp.concatenate([p[n] for p in parts], axis=1) for n in range(2))
    wproj = jnp.stack([jnp.concatenate([p[n] for p in parts], axis=0) for n in range(2, 5)], axis=0)
    return w_uq, w_ukv, wproj


MESH_ID = pl.DeviceIdType.MESH
ANY = pl.BlockSpec(memory_space=pl.ANY)
HALF_COLS = D_MODEL // 2


def _me():
    return lax.axis_index("x"), lax.axis_index("y"), lax.axis_index("c")


def _other_chips(x, y):
    return [(1 - x, y), (x, 1 - y), (1 - x, 1 - y)]


def _cols(c):
    return pl.ds(c * HALF_COLS, HALF_COLS)


def _gather_weights(w_blk, r_blk):
    def body(w_ref, r_ref, ow_ref, or_ref, send_sems, recv_sems):
        x, y, c = _me()
        chips = _other_chips(x, y)
        me = 2 * x + y
        pairs = [(w_ref, ow_ref), (r_ref, or_ref)]

        def copy(k, src, dst, to):
            return pltpu.make_async_remote_copy(src_ref=src, dst_ref=dst, send_sem=send_sems.at[k],
                                                recv_sem=recv_sems.at[k], device_id=to, device_id_type=MESH_ID)

        first =[copy(6 * a + j, src.at[:, _cols(c)], dst.at[me, :, _cols(c)], (cx, cy, c))
                 for a, (src, dst) in enumerate(pairs) for j, (cx, cy) in enumerate(chips)]
        for cp in first:
            cp.start()
        passed = []
        for a, (src, dst) in enumerate(pairs):
            for j, (cx, cy) in enumerate(chips):
                landed = dst.at[2 * cx + cy, :, _cols(c)]
                copy(6 * a + j, landed, landed, (cx, cy, c)).wait_recv()
                fwd = copy(6 * a + 3 + j, landed, landed, (x, y, 1 - c))
                fwd.start()
                passed.append(fwd)
        for a, (src, dst) in enumerate(pairs):
            for j, (cx, cy) in enumerate(chips):
                theirs = dst.at[2 * cx + cy, :, _cols(1 - c)]
                copy(6 * a + 3 + j, theirs, theirs, (x, y, 1 - c)).wait_recv()
        for cp in first + passed:
            cp.wait_send()

    gw, gr = pl.pallas_call(
        body, name="gather_weights", in_specs=[ANY, ANY], out_specs=[ANY, ANY],
        out_shape=[jax.ShapeDtypeStruct((N_CHIPS,) + w_blk.shape, w_blk.dtype),
                   jax.ShapeDtypeStruct((N_CHIPS,) + r_blk.shape, r_blk.dtype)],
        scratch_shapes=[pltpu.SemaphoreType.DMA((12,)), pltpu.SemaphoreType.DMA((12,))],
    )(w_blk, r_blk)
    chip = 2 * lax.axis_index("x") + lax.axis_index("y")
    return (lax.dynamic_update_slice(gw, w_blk[None], (chip, 0, 0)),
            lax.dynamic_update_slice(gr, r_blk[None], (chip, 0, 0)))


def _swap_halves(gw, gr):
    def body(gw_ref, gr_ref, lw_ref, lr_ref, send_sems, recv_sems):
        x, y, c = _me()
        cps = [pltpu.make_async_remote_copy(
            src_ref=src, dst_ref=dst, send_sem=send_sems.at[a], recv_sem=recv_sems.at[a],
            device_id=(x, y, 1 - c), device_id_type=MESH_ID)
            for a, (src, dst) in enumerate([(gw_ref.at[:, _cols(1 - c)], lw_ref),
                                            (gr_ref.at[:, :, _cols(1 - c)], lr_ref)])]
        for cp in cps:
            cp.start()
        for cp in cps:
            cp.wait()

    return pl.pallas_call(
        body, name="grad_swap_halves", in_specs=[ANY, ANY], out_specs=[ANY, ANY],
        out_shape=[jax.ShapeDtypeStruct((gw.shape[0], HALF_COLS), gw.dtype),
                   jax.ShapeDtypeStruct(gr.shape[:2] + (HALF_COLS,), gr.dtype)],
        scratch_shapes=[pltpu.SemaphoreType.DMA((2,)), pltpu.SemaphoreType.DMA((2,))],
    )(gw, gr)


def _swap_reduced(rw, rr):
    def body(rw_ref, rr_ref, ow_ref, or_ref, send_sems, recv_sems):
        x, y, c = _me()
        cps = [pltpu.make_async_remote_copy(
            src_ref=src, dst_ref=dst, send_sem=send_sems.at[a], recv_sem=recv_sems.at[a],
            device_id=(x, y, 1 - c), device_id_type=MESH_ID)
            for a, (src, dst) in enumerate([(rw_ref, ow_ref), (rr_ref, or_ref)])]
        for cp in cps:
            cp.start()
        for cp in cps:
            cp.wait()

    return pl.pallas_call(
        body, name="grad_swap_reduced", in_specs=[ANY, ANY], out_specs=[ANY, ANY],
        out_shape=[jax.ShapeDtypeStruct(rw.shape, rw.dtype), jax.ShapeDtypeStruct(rr.shape, rr.dtype)],
        scratch_shapes=[pltpu.SemaphoreType.DMA((2,)), pltpu.SemaphoreType.DMA((2,))],
    )(rw, rr)


def _join_cols(mine, theirs):
    first = lax.axis_index("c") == 0
    return jnp.concatenate([jnp.where(first, mine, theirs), jnp.where(first, theirs, mine)], axis=1)


def _gather_small(vec):
    def body(v_ref, out_ref, send_sems, recv_sems, local_sem):
        x, y, c = _me()
        my_id = 4 * x + 2 * y + c
        mine = pltpu.make_async_copy(v_ref, out_ref.at[my_id], local_sem)
        mine.start()
        cps = []
        for r in range(1, N_DEV):
            peer = (x ^ (r >> 2), y ^ ((r >> 1) & 1), c ^ (r & 1))
            cps.append(pltpu.make_async_remote_copy(
                src_ref=v_ref, dst_ref=out_ref.at[my_id], send_sem=send_sems.at[r - 1],
                recv_sem=recv_sems.at[r - 1], device_id=peer, device_id_type=MESH_ID))
        for cp in cps:
            cp.start()
        for cp in cps:
            cp.wait()
        mine.wait()

    return pl.pallas_call(
        body, name="gather_small", in_specs=[ANY], out_specs=ANY,
        out_shape=jax.ShapeDtypeStruct((N_DEV, 1, SMALL_COLS), vec.dtype),
        scratch_shapes=[pltpu.SemaphoreType.DMA((N_DEV - 1,)), pltpu.SemaphoreType.DMA((N_DEV - 1,)),
                        pltpu.SemaphoreType.DMA],
    )(vec)


def _add_cores(c_idx, g, landed, *, tm, name):
    r = g.shape[0]

    def body(c_ref, g_ref, l_ref, o32_ref, o16_ref):
        acc = g_ref[...] + l_ref[...]
        o32_ref[...] = acc
        o16_ref[...] = acc.astype(BF16)

    half = pl.BlockSpec((tm, HALF_COLS), lambda i, c_ref: (i, 0))
    grid_spec = pltpu.PrefetchScalarGridSpec(
        num_scalar_prefetch=1, grid=(r // tm,),
        in_specs=[pl.BlockSpec((tm, HALF_COLS), lambda i, c_ref: (i, c_ref[0])), half], out_specs=[half, half])
    return pl.pallas_call(
        body, name=name, grid_spec=grid_spec,
        out_shape=[jax.ShapeDtypeStruct((r, HALF_COLS), F32), jax.ShapeDtypeStruct((r, HALF_COLS), BF16)],
        compiler_params=_params(1),
    )(c_idx, g, landed)


def _add_chips(chip_idx, h32, landed, *, tm, name):
    n = landed.shape[1]
    per = n // tm

    def body(chip_ref, h_ref, l_ref, o_ref):
        acc = h_ref[...]
        for j in range(3):
            acc = acc + l_ref[j].astype(F32)
        o_ref[...] = acc

    grid_spec = pltpu.PrefetchScalarGridSpec(
        num_scalar_prefetch=1, grid=(per,),
        in_specs=[pl.BlockSpec((tm, HALF_COLS), lambda i, chip_ref: (chip_ref[0] * per + i, 0)),
                  pl.BlockSpec((3, tm, HALF_COLS), lambda i, chip_ref: (0, i, 0))],
        out_specs=pl.BlockSpec((tm, HALF_COLS), lambda i, chip_ref: (i, 0)))
    return pl.pallas_call(
        body, name=name, grid_spec=grid_spec, out_shape=jax.ShapeDtypeStruct((n, HALF_COLS), F32),
        compiler_params=_params(1),
    )(chip_idx, h32, landed)


def _pack_small(small, lb_logits, loss):
    def body(ng_ref, bg_ref, dlb_ref, lbl_ref, hgg_ref, qg_ref, kvg_ref, fg_ref, loss_ref, out_ref):
        out_ref[...] = jnp.zeros_like(out_ref)
        out_ref[:, 0:1024] = ng_ref[...]
        out_ref[:, 1024:3072] = bg_ref[...]
        _, p0p1 = _lower_bound(lbl_ref[...])
        dl0 = dlb_ref[...] * p0p1
        out_ref[:, 3072:4096] = dl0
        out_ref[:, 4096:5120] = -dl0
        hgg = hgg_ref[0]
        for h in range(1, HEADS):
            hgg = hgg + hgg_ref[h]
        out_ref[:, 5120:5248] = hgg
        out_ref[:, 5248:5632] = qg_ref[...]
        out_ref[:, 5632:5888] = kvg_ref[...]
        out_ref[:, 5888:6912] = fg_ref[...]
        out_ref[:, 6912:7040] = loss_ref[...]

    return pl.pallas_call(
        body, name="pack_small", out_shape=jax.ShapeDtypeStruct((1, SMALL_COLS), F32),
    )(small["norm_g"], small["b_gate"], small["lb"], lb_logits, small["hg_norm_g"], small["q_a_g"],
      small["kv_a_g"], small["final_norm_g"], loss)


def _adamw_math(w, g, m, v):
    nm = ADAM_B1 * m + (1.0 - ADAM_B1) * g
    nv = ADAM_B2 * v + (1.0 - ADAM_B2) * (g * g)
    m_hat = nm / (1.0 - ADAM_B1 ** ADAM_STEP)
    v_hat = nv / (1.0 - ADAM_B2 ** ADAM_STEP)
    return -ADAM_LR * (m_hat / (jnp.sqrt(v_hat) + ADAM_EPS) + ADAM_WD * w), nm, nv


def _adamw(w, g, m, v, *, name, tm):
    r, cols = w.shape

    def body(w_ref, g_ref, m_ref, v_ref, d_ref, nm_ref, nv_ref):
        d_ref[...], nm_ref[...], nv_ref[...] = _adamw_math(w_ref[...], g_ref[...], m_ref[...], v_ref[...])

    row = pl.BlockSpec((tm, cols), lambda i: (i, 0))
    shp = jax.ShapeDtypeStruct((r, cols), F32)
    return pl.pallas_call(
        body, name=name, grid=(r // tm,), in_specs=[row] * 4, out_specs=[row] * 3, out_shape=[shp] * 3,
        compiler_params=_params(1),
    )(w, g, m, v)


SMALL_SLOTS = (("norm_g", (0,)), ("b_gate", (1024,)), ("lb_logits", (3072, 4096)), ("hg_norm_g", (5120,)),
               ("q_a_g", (5248,)), ("kv_a_g", (5632,)), ("final_norm_g", (5888,)))
LOSS_SLOT = 6912


def _small_update(gathered, ws, ms, vs):
    n = len(SMALL_SLOTS)

    def body(*refs):
        g_ref = refs[0]
        w_refs, m_refs, v_refs = refs[1:1 + n], refs[1 + n:1 + 2 * n], refs[1 + 2 * n:1 + 3 * n]
        outs = refs[1 + 3 * n:]
        loss_ref = outs[0]
        g_out, d_out, nm_out, nv_out = (outs[1 + k * n:1 + (k + 1) * n] for k in range(4))
        total = g_ref[0]
        for dev in range(1, N_DEV):
            total = total + g_ref[dev]
        loss_ref[...] = total[:, LOSS_SLOT:LOSS_SLOT + 128]
        for p, (_, offsets) in enumerate(SMALL_SLOTS):
            cols = w_refs[p].shape[1]
            for r, off in enumerate(offsets):
                rows = slice(r, r + 1)
                g = total[:, off:off + cols]
                g_out[p][rows, :] = g
                d_out[p][rows, :], nm_out[p][rows, :], nv_out[p][rows, :] = _adamw_math(
                    w_refs[p][rows, :], g, m_refs[p][rows, :], v_refs[p][rows, :])

    shapes = [jax.ShapeDtypeStruct(w.shape, F32) for w in ws]
    res = pl.pallas_call(
        body, name="small_update", out_shape=[jax.ShapeDtypeStruct((1, 128), F32)] + shapes * 4,
    )(gathered, *ws, *ms, *vs)
    return res[0], res[1:1 + n], res[1 + n:1 + 2 * n], res[1 + 2 * n:1 + 3 * n], res[1 + 3 * n:1 + 4 * n]


def kernel(x, norm_g, w_in, b_gate, lb_logits, hg_norm_g, q_a_g, w_uq, kv_a_g, w_ukv, w_proj_a, w_proj_b, w_out, final_norm_g, loss_target, m_norm_g, m_w_in, m_b_gate, m_lb_logits, m_hg_norm_g, m_q_a_g, m_w_uq, m_kv_a_g, m_w_ukv, m_w_proj_a, m_w_proj_b, m_w_out, m_final_norm_g, v_norm_g, v_w_in, v_b_gate, v_lb_logits, v_hg_norm_g, v_q_a_g, v_w_uq, v_kv_a_g, v_w_ukv, v_w_proj_a, v_w_proj_b, v_w_out, v_final_norm_g):
    c_idx = lax.axis_index("c").astype(jnp.int32).reshape(1)
    chip_idx = (2 * lax.axis_index("x") + lax.axis_index("y")).astype(jnp.int32).reshape(1)

    w_blk = w_in[0].T.astype(BF16)
    r_blk = _pack_rest(w_uq[0], w_ukv[0], w_proj_a[0], w_proj_b[0], w_out[0]).astype(BF16)
    gw, gr = _gather_weights(w_blk, r_blk)
    fw_uq, fw_ukv, fwproj = _unpack_rest_weights(gr)

    loss, dh_args, d_w_int, d_wuq3, d_wukv3, d_proj, small = _local_step(
        x[0], loss_target[0], gw.reshape(W_IN_COLS, D_MODEL), fw_uq, fw_ukv, fwproj,
        norm_g, b_gate, lb_logits, hg_norm_g, q_a_g, kv_a_g, final_norm_g.reshape(1, D_MODEL))

    d_rest = _pack_rest_grads(d_wuq3, d_wukv3, d_proj)
    lw, lr = _swap_halves(d_w_int, d_rest)
    hw32, hw16 = _add_cores(c_idx, d_w_int, lw, tm=656, name="grad_add_cores_w")
    hr32, hr16 = _add_cores(c_idx, d_rest.reshape(N_CHIPS * REST_ROWS, D_MODEL), lr.reshape(N_CHIPS * REST_ROWS, HALF_COLS),
                            tm=REST_ROWS, name="grad_add_cores_r")
    grad_x, small["norm_g"], landed_w, landed_r = _dh_fused(*dh_args, hw16, hr16)
    rw = _add_chips(chip_idx, hw32, landed_w, tm=656, name="grad_add_chips_w")
    rr = _add_chips(chip_idx, hr32, landed_r, tm=208, name="grad_add_chips_r")
    tw, tr = _swap_reduced(rw, rr)
    g_w_in = _join_cols(rw, tw).T
    g_rest = _join_cols(rr, tr)
    g_uq, g_ukv, g_pa, g_pb, g_out = _unpack_rest(g_rest)

    small_all = _gather_small(_pack_small(small, lb_logits, loss))

    upd = {
        "w_in": _adamw(w_in[0], g_w_in, m_w_in[0], v_w_in[0], name="adamw_w_in", tm=128),
        "w_uq": _adamw(w_uq[0], g_uq, m_w_uq[0], v_w_uq[0], name="adamw_w_uq", tm=Q_LORA),
        "w_ukv": _adamw(w_ukv[0], g_ukv, m_w_ukv[0], v_w_ukv[0], name="adamw_w_ukv", tm=KV_LORA),
        "w_proj_a": _adamw(w_proj_a[0], g_pa, m_w_proj_a[0], v_w_proj_a[0], name="adamw_w_proj_a", tm=256),
        "w_proj_b": _adamw(w_proj_b[0], g_pb, m_w_proj_b[0], v_w_proj_b[0], name="adamw_w_proj_b", tm=256),
        "w_out": _adamw(w_out[0], g_out, m_w_out[0], v_w_out[0], name="adamw_w_out", tm=256),
    }
    loss_vec, *small_sets = _small_update(
        small_all,
        [norm_g, b_gate, lb_logits, hg_norm_g, q_a_g, kv_a_g, final_norm_g.reshape(1, D_MODEL)],
        [m_norm_g, m_b_gate, m_lb_logits, m_hg_norm_g, m_q_a_g, m_kv_a_g, m_final_norm_g.reshape(1, D_MODEL)],
        [v_norm_g, v_b_gate, v_lb_logits, v_hg_norm_g, v_q_a_g, v_kv_a_g, v_final_norm_g.reshape(1, D_MODEL)])

    def outputs(big, small_set):
        s_ng, s_bg, s_lb, s_hg, s_qg, s_kvg, s_fg = small_set
        return (s_ng, big["w_in"][None], s_bg, s_lb, s_hg, s_qg, big["w_uq"][None], s_kvg, big["w_ukv"][None],
                big["w_proj_a"][None], big["w_proj_b"][None], big["w_out"][None], s_fg.reshape(D_MODEL))

    grads = {"w_in": g_w_in, "w_uq": g_uq, "w_ukv": g_ukv, "w_proj_a": g_pa, "w_proj_b": g_pb, "w_out": g_out}
    return (loss_vec[0, 0], grad_x[None], *outputs(grads, small_sets[0]),
            *(o for k in range(3) for o in outputs({n: u[k] for n, u in upd.items()}, small_sets[1 + k])))
```

```python
import functools

import jax
import jax.numpy as jnp
from jax import lax
from jax.experimental import pallas as pl
from jax.experimental.pallas import tpu as pltpu

F32 = jnp.float32
BF16 = jnp.bfloat16

D_MODEL = 1024
HEADS = 8
HEAD_DIM = 128
HG_CHUNK = 32
CHUNK_SHIFT = 5
HEAD_SHIFT = 7
QK_NOPE = 128
QK_ROPE = 64
QK_DIM = QK_NOPE + QK_ROPE
QK_PAD = 256
Q_LORA = 384
KV_LORA = 256
MS_COLS = 768
ROPE_THETA = 10000.0
EPS = 1e-6
ATT_SCALE = QK_DIM ** -0.5
LOG2E = 1.4426950408889634
LN2 = 0.6931471805599453
Q_PRESCALE = ATT_SCALE * LOG2E

ADAM_LR = 0.001
ADAM_B1 = 0.9
ADAM_B2 = 0.999
ADAM_EPS = 1e-08
ADAM_WD = 0.01
ADAM_STEP = 10

N_CHIPS = 4
N_DEV = 8
W_IN_COLS = 7872
W_IN_BLK = W_IN_COLS // N_CHIPS
REST_ROWS = 144 + 128 + 3 * 256
SMALL_COLS = 7168

TM_MM = 1024
TM_FUSED = 256
HG_ROWS = 128
TQ = 512
FLASH_HEADS = 2
HG_HEADS = 8
VMEM_LIMIT = 56 * 1024 * 1024


def _dot(a, b):
    return lax.dot_general(a, b, (((1,), (0,)), ((), ())), preferred_element_type=F32)


def _dot_nt(a, b):
    return lax.dot_general(a, b, (((1,), (1,)), ((), ())), preferred_element_type=F32)


def _dot_tn(a, b):
    return lax.dot_general(a, b, (((0,), (0,)), ((), ())), preferred_element_type=F32)


def _params(n_axes):
    return pltpu.CompilerParams(dimension_semantics=("arbitrary",) * n_axes, vmem_limit_bytes=VMEM_LIMIT)


def _rms(x, g):
    r = lax.rsqrt(jnp.mean(x * x, axis=-1, keepdims=True) + EPS)
    return x * r * g


def _rms_bwd(x, g, dy):
    r = lax.rsqrt(jnp.mean(x * x, axis=-1, keepdims=True) + EPS)
    xh = x * r
    dyg = dy * g
    dx = r * (dyg - xh * jnp.mean(dyg * xh, axis=-1, keepdims=True))
    return dx, dy * xh


def _silu_parts(z):
    s = jax.nn.sigmoid(z)
    return z * s, s * (1.0 + z * (1.0 - s))


def _rope(x, c, sa, sb):
    return x * c + pltpu.roll(x, 32, 1) * sa + pltpu.roll(x, 96, 1) * sb


def _rope_bwd(dy, c, sa, sb):
    return dy * c + pltpu.roll(dy * sa, 96, 1) + pltpu.roll(dy * sb, 32, 1)


def _rope_tables(seq):
    inv = ROPE_THETA ** (-jnp.arange(0, QK_ROPE, 2, dtype=F32) / QK_ROPE)
    ang = jnp.arange(seq, dtype=F32)[:, None] * inv[None, :]
    cos, sin = jnp.cos(ang), jnp.sin(ang)
    z32 = jnp.zeros_like(cos)
    z64 = jnp.zeros((seq, 64), F32)
    c = jnp.concatenate([cos, cos, z64], axis=1)
    sa = jnp.concatenate([z32, sin, z64], axis=1)
    sb = jnp.concatenate([-sin, z32, z64], axis=1)
    return c, sa, sb


def _mm_tn(a, b, *, name, tm=TM_MM, tn=1024):
    flat = a.ndim == 2
    if flat:
        a = a[None]
    g, m, k = a.shape
    n = b.shape[1]
    tm, tn = min(tm, m), min(tn, n)
    assert m % tm == 0 and n % tn == 0

    def body(a_ref, b_ref, o_ref):
        @pl.when(pl.program_id(2) == 0)
        def _():
            o_ref[...] = jnp.zeros_like(o_ref)

        o_ref[...] += _dot_tn(a_ref[...], b_ref[...])

    out = pl.pallas_call(
        body, name=name, grid=(g, n // tn, m // tm),
        in_specs=[pl.BlockSpec((None, tm, k), lambda s, j, i: (s, i, 0)),
                  pl.BlockSpec((tm, tn), lambda s, j, i: (i, j))],
        out_specs=pl.BlockSpec((None, k, tn), lambda s, j, i: (s, 0, j)),
        out_shape=jax.ShapeDtypeStruct((g, k, n), F32), compiler_params=_params(3),
    )(a, b)
    return out[0] if flat else out


def _chunk_rows(rows):
    return lax.broadcasted_iota(jnp.int32, (rows, HEAD_DIM), 0) & (HG_CHUNK - 1)


def _chunk_cumsum(x, rows):
    pos = _chunk_rows(rows)
    shift = 1
    while shift < HG_CHUNK:
        x = x + jnp.where(pos >= shift, pltpu.roll(x, shift, 0), 0.0)
        shift *= 2
    return x


def _chunk_revcumsum(x, rows):
    pos = _chunk_rows(rows)
    shift = 1
    while shift < HG_CHUNK:
        x = x + jnp.where(pos + shift < HG_CHUNK, pltpu.roll(x, rows - shift, 0), 0.0)
        shift *= 2
    return x


def _lower_bound(lbl):
    mx = jnp.maximum(lbl[0:1, :], lbl[1:2, :])
    e0 = jnp.exp(lbl[0:1, :] - mx)
    e1 = jnp.exp(lbl[1:2, :] - mx)
    p0 = e0 / (e0 + e1)
    return p0, p0 * (e1 / (e0 + e1))


def _hg_masks(rows, nch, tmask_s, bdmask_s):
    r = lax.broadcasted_iota(jnp.int32, (rows, rows), 0)
    c = lax.broadcasted_iota(jnp.int32, (rows, rows), 1)
    tmask_s[...] = jnp.where(((r >> CHUNK_SHIFT) == (c >> CHUNK_SHIFT)) & (r >= c), 1.0, 0.0)
    r = lax.broadcasted_iota(jnp.int32, (rows, nch * HEAD_DIM), 0)
    c = lax.broadcasted_iota(jnp.int32, (rows, nch * HEAD_DIM), 1)
    bdmask_s[...] = jnp.where((r >> CHUNK_SHIFT) == (c >> HEAD_SHIFT), 1.0, 0.0).astype(BF16)


def _block_diag(x, nch, bdmask):
    return jnp.tile(x, (1, nch)) * bdmask


def _hgrn_fwd(hg, lb_logits, norm_g):
    s = hg.shape[1]
    rows = min(HG_ROWS, s)
    nblk = s // rows
    nch = rows // HG_CHUNK

    def body(hg_ref, lbl_ref, g_ref, o_ref, ya_ref, st0_ref, st_s, stall_s, tmask_s, bdmask_s):
        @pl.when(pl.program_id(1) == 0)
        def _():
            st_s[...] = jnp.zeros_like(st_s)
            _hg_masks(rows, nch, tmask_s, bdmask_s)

        bdmask = bdmask_s[...]
        tmask = tmask_s[...] > 0.5
        for hh in range(HG_HEADS):
            hc = slice(hh * HEAD_DIM, (hh + 1) * HEAD_DIM)
            hq = hg_ref[0, :, hc]
            hf = hg_ref[1, :, hc]
            hi = hg_ref[2, :, hc]
            hz = hg_ref[3, :, hc]
            lb, _ = _lower_bound(lbl_ref[:, hc])
            f = lb + (1.0 - lb) * jax.nn.sigmoid(hf)
            q = hq * jax.nn.sigmoid(hq)
            k = 1.0 - f
            logf = jnp.log(f)
            b = _chunk_cumsum(logf, rows)
            q_in = (q * jnp.exp(b)).astype(BF16)
            k_in = (k * jnp.exp(-b)).astype(BF16)
            k_out = (k * jnp.exp(_chunk_revcumsum(logf, rows) - logf)).astype(BF16)
            vb = hi.astype(BF16)

            sc = jnp.where(tmask, _dot_nt(q_in, k_in), 0.0)
            o_intra = _dot(sc.astype(BF16), vb)
            kvt = _dot_tn(vb, _block_diag(k_out, nch, bdmask))
            st = st_s[hh]
            st0_ref[hh] = st
            for c in range(nch):
                cols = slice(c * HEAD_DIM, (c + 1) * HEAD_DIM)
                last = (c + 1) * HG_CHUNK - 1
                stall_s[hh, :, cols] = st.astype(BF16)
                st = st * jnp.exp(b[last:last + 1, :]) + kvt[:, cols]
            st_s[hh] = st
            o = o_intra + _dot_nt(_block_diag(q_in, nch, bdmask), stall_s[hh])
            o_ref[:, hc] = o
            silu_z, _ = _silu_parts(hz)
            ya_ref[:, hc] = (_rms(o, g_ref[...]) * silu_z).astype(BF16)

    nh = HG_HEADS
    return pl.pallas_call(
        body, name="hgrn_fwd", grid=(HEADS // nh, nblk),
        in_specs=[pl.BlockSpec((4, rows, nh * HEAD_DIM), lambda h, i: (0, i, h)),
                  pl.BlockSpec((2, nh * HEAD_DIM), lambda h, i: (0, h)),
                  pl.BlockSpec((1, HEAD_DIM), lambda h, i: (0, 0))],
        out_specs=[pl.BlockSpec((rows, nh * HEAD_DIM), lambda h, i: (i, h)),
                   pl.BlockSpec((rows, nh * HEAD_DIM), lambda h, i: (i, h)),
                   pl.BlockSpec((nh, None, HEAD_DIM, HEAD_DIM), lambda h, i: (h, i, 0, 0))],
        out_shape=[jax.ShapeDtypeStruct((s, D_MODEL), F32), jax.ShapeDtypeStruct((s, D_MODEL), BF16),
                   jax.ShapeDtypeStruct((HEADS, nblk, HEAD_DIM, HEAD_DIM), F32)],
        scratch_shapes=[pltpu.VMEM((nh, HEAD_DIM, HEAD_DIM), F32), pltpu.VMEM((nh, HEAD_DIM, nch * HEAD_DIM), BF16),
                        pltpu.VMEM((rows, rows), F32), pltpu.VMEM((rows, nch * HEAD_DIM), BF16)],
        compiler_params=_params(2),
    )(hg, lb_logits, norm_g)


def _hgrn_bwd(hg, o_pre, dya, st0, lb_logits, norm_g):
    s = hg.shape[1]
    rows = min(HG_ROWS, s)
    nblk = s // rows
    nch = rows // HG_CHUNK

    def body(hg_ref, o_ref, dya_ref, st0_ref, lbl_ref, g_ref, dhg_ref, dlb_ref, dg_ref,
             dst_s, stp_s, stp_rows_s, dst_rows_s, dst_lane_s, dbl_s, tmask_s, bdmask_s):
        @pl.when(pl.program_id(1) == 0)
        def _():
            dst_s[...] = jnp.zeros_like(dst_s)
            dlb_ref[...] = jnp.zeros_like(dlb_ref)
            dg_ref[...] = jnp.zeros_like(dg_ref)
            _hg_masks(rows, nch, tmask_s, bdmask_s)

        bdmask = bdmask_s[...]
        tmask = tmask_s[...] > 0.5
        g = g_ref[...]
        for hh in range(HG_HEADS):
            hc = slice(hh * HEAD_DIM, (hh + 1) * HEAD_DIM)
            hq = hg_ref[0, :, hc]
            hf = hg_ref[1, :, hc]
            hi = hg_ref[2, :, hc]
            hz = hg_ref[3, :, hc]
            lb, _ = _lower_bound(lbl_ref[:, hc])
            sg = jax.nn.sigmoid(hf)
            f = lb + (1.0 - lb) * sg
            q, dsilu_q = _silu_parts(hq)
            k = 1.0 - f
            logf = jnp.log(f)
            b = _chunk_cumsum(logf, rows)
            eb = jnp.exp(b)
            enb = jnp.exp(-b)
            ebl = jnp.exp(_chunk_revcumsum(logf, rows) - logf)
            q_in32 = q * eb
            k_in32 = k * enb
            k_out32 = k * ebl
            q_in = q_in32.astype(BF16)
            k_in = k_in32.astype(BF16)
            k_out = k_out32.astype(BF16)
            vb = hi.astype(BF16)
            kbd = _block_diag(k_out, nch, bdmask)
            qbd = _block_diag(q_in, nch, bdmask)
            decs = [jnp.exp(b[(c + 1) * HG_CHUNK - 1:(c + 1) * HG_CHUNK, :]) for c in range(nch)]

            kvt = _dot_tn(vb, kbd)
            st = st0_ref[hh]
            for c in range(nch):
                stp_s[hh, c] = st
                stp_rows_s[hh, c * HEAD_DIM:(c + 1) * HEAD_DIM, :] = st.astype(BF16)
                st = st * decs[c] + kvt[:, c * HEAD_DIM:(c + 1) * HEAD_DIM]

            o = o_ref[:, hc]
            rstd = lax.rsqrt(jnp.mean(o * o, axis=-1, keepdims=True) + EPS)
            oh = o * rstd
            silu_z, dsilu_z = _silu_parts(hz)
            dya_v = dya_ref[:, hc]
            dn = dya_v * silu_z
            dhz = dya_v * (oh * g) * dsilu_z
            dg_ref[hh] += jnp.sum(dn * oh, axis=0, keepdims=True)
            doh = dn * g
            do = (rstd * (doh - oh * jnp.mean(doh * oh, axis=-1, keepdims=True))).astype(BF16)

            dq_all = _dot_tn(do, qbd)
            dst = dst_s[hh]
            ddecs = [None] * nch
            for c in reversed(range(nch)):
                dstb = dst.astype(BF16)
                dst_lane_s[hh, :, c * HEAD_DIM:(c + 1) * HEAD_DIM] = dstb
                dst_rows_s[hh, c * HEAD_DIM:(c + 1) * HEAD_DIM, :] = dstb
                ddecs[c] = jnp.sum(dst * stp_s[hh, c], axis=0, keepdims=True) * decs[c]
                dst = dst * decs[c] + dq_all[:, c * HEAD_DIM:(c + 1) * HEAD_DIM]
            dst_s[hh] = dst

            sc = jnp.where(tmask, _dot_nt(q_in, k_in), 0.0).astype(BF16)
            dkout = _dot(_block_diag(vb, nch, bdmask), dst_rows_s[hh])
            dv = _dot_nt(kbd, dst_lane_s[hh]) + _dot_tn(sc, do)
            dsc = jnp.where(tmask, _dot_nt(do, vb), 0.0).astype(BF16)
            dqin = _dot(dsc, k_in) + _dot(_block_diag(do, nch, bdmask), stp_rows_s[hh])
            dkin = _dot_tn(dsc, q_in)

            dko = dkout * k_out32
            for c in range(nch):
                sl = slice(c * HG_CHUNK, (c + 1) * HG_CHUNK)
                dbl = jnp.sum(dko[sl], axis=0, keepdims=True) + ddecs[c]
                dbl_s[hh, sl, :] = jnp.broadcast_to(dbl, (HG_CHUNK, HEAD_DIM))
            dq = dqin * eb
            dk = dkin * enb + dkout * ebl
            db = dqin * q_in32 - dkin * k_in32 - dko
            dlogf = _chunk_revcumsum(db, rows) + dbl_s[hh]
            df = dlogf / f - dk
            dlb_ref[:, hc] += jnp.sum(df * (1.0 - sg), axis=0, keepdims=True)
            dhg_ref[0, :, hc] = (dq * dsilu_q).astype(BF16)
            dhg_ref[1, :, hc] = (df * (1.0 - lb) * sg * (1.0 - sg)).astype(BF16)
            dhg_ref[2, :, hc] = dv.astype(BF16)
            dhg_ref[3, :, hc] = dhz.astype(BF16)

    last = nblk - 1
    nh = HG_HEADS
    wide = nh * HEAD_DIM
    return pl.pallas_call(
        body, name="hgrn_bwd", grid=(HEADS // nh, nblk),
        in_specs=[pl.BlockSpec((4, rows, wide), lambda h, i: (0, last - i, h)),
                  pl.BlockSpec((rows, wide), lambda h, i: (last - i, h)),
                  pl.BlockSpec((rows, wide), lambda h, i: (last - i, h)),
                  pl.BlockSpec((nh, None, HEAD_DIM, HEAD_DIM), lambda h, i: (h, last - i, 0, 0)),
                  pl.BlockSpec((2, wide), lambda h, i: (0, h)),
                  pl.BlockSpec((1, HEAD_DIM), lambda h, i: (0, 0))],
        out_specs=[pl.BlockSpec((4, rows, wide), lambda h, i: (0, last - i, h)),
                   pl.BlockSpec((1, wide), lambda h, i: (0, h)),
                   pl.BlockSpec((nh, 1, HEAD_DIM), lambda h, i: (h, 0, 0))],
        out_shape=[jax.ShapeDtypeStruct((4, s, D_MODEL), BF16), jax.ShapeDtypeStruct((1, D_MODEL), F32),
                   jax.ShapeDtypeStruct((HEADS, 1, HEAD_DIM), F32)],
        scratch_shapes=[pltpu.VMEM((nh, HEAD_DIM, HEAD_DIM), F32), pltpu.VMEM((nh, nch, HEAD_DIM, HEAD_DIM), F32),
                        pltpu.VMEM((nh, nch * HEAD_DIM, HEAD_DIM), BF16), pltpu.VMEM((nh, nch * HEAD_DIM, HEAD_DIM), BF16),
                        pltpu.VMEM((nh, HEAD_DIM, nch * HEAD_DIM), BF16), pltpu.VMEM((nh, rows, HEAD_DIM), F32),
                        pltpu.VMEM((rows, rows), F32), pltpu.VMEM((rows, nch * HEAD_DIM), BF16)],
        compiler_params=_params(2),
    )(hg, o_pre, dya, st0, lb_logits, norm_g)


def _mla_pre(ms, q_a_g, kv_a_g, wuq3, wukv3, tabs):
    s = ms.shape[0]
    tm = min(TM_FUSED, s)

    def body(ms_ref, qg_ref, kvg_ref, wuq_ref, wukv_ref, c_ref, sa_ref, sb_ref,
             q_ref, k_ref, v_ref, cqn_ref, ckvn_ref):
        c, sa, sb = c_ref[...], sa_ref[...], sb_ref[...]
        cqn = _rms(ms_ref[:, 0:Q_LORA], qg_ref[...]).astype(BF16)
        ckvn = _rms(ms_ref[:, Q_LORA:Q_LORA + KV_LORA], kvg_ref[...]).astype(BF16)
        cqn_ref[...] = cqn
        ckvn_ref[...] = ckvn
        k_pe = _rope(ms_ref[:, Q_LORA + KV_LORA:MS_COLS], c, sa, sb).astype(BF16)
        for h in range(HEADS):
            qh = _dot(cqn, wuq_ref[h])
            q_ref[h, :, 0:128] = (qh[:, 0:128] * Q_PRESCALE).astype(BF16)
            q_ref[h, :, 128:256] = (_rope(qh[:, 128:256], c, sa, sb) * Q_PRESCALE).astype(BF16)
            kvh = _dot(ckvn, wukv_ref[h])
            k_ref[h, :, 0:128] = kvh[:, 0:128].astype(BF16)
            k_ref[h, :, 128:256] = k_pe
            v_ref[h] = kvh[:, 128:256].astype(BF16)

    tab = pl.BlockSpec((tm, 128), lambda i: (i, 0))
    return pl.pallas_call(
        body, name="mla_pre", grid=(s // tm,),
        in_specs=[pl.BlockSpec((tm, MS_COLS), lambda i: (i, 0)),
                  pl.BlockSpec((1, Q_LORA), lambda i: (0, 0)), pl.BlockSpec((1, KV_LORA), lambda i: (0, 0)),
                  pl.BlockSpec((HEADS, Q_LORA, QK_PAD), lambda i: (0, 0, 0)),
                  pl.BlockSpec((HEADS, KV_LORA, 256), lambda i: (0, 0, 0)), tab, tab, tab],
        out_specs=[pl.BlockSpec((HEADS, tm, QK_PAD), lambda i: (0, i, 0)),
                   pl.BlockSpec((HEADS, tm, QK_PAD), lambda i: (0, i, 0)),
                   pl.BlockSpec((HEADS, tm, HEAD_DIM), lambda i: (0, i, 0)),
                   pl.BlockSpec((tm, Q_LORA), lambda i: (i, 0)), pl.BlockSpec((tm, KV_LORA), lambda i: (i, 0))],
        out_shape=[jax.ShapeDtypeStruct((HEADS, s, QK_PAD), BF16), jax.ShapeDtypeStruct((HEADS, s, QK_PAD), BF16),
                   jax.ShapeDtypeStruct((HEADS, s, HEAD_DIM), BF16),
                   jax.ShapeDtypeStruct((s, Q_LORA), BF16), jax.ShapeDtypeStruct((s, KV_LORA), BF16)],
        compiler_params=_params(1),
    )(ms, q_a_g, kv_a_g, wuq3, wukv3, *tabs)


def _causal_mask(t):
    r = lax.broadcasted_iota(jnp.int32, (t, t), 0)
    c = lax.broadcasted_iota(jnp.int32, (t, t), 1)
    return r >= c


def _flash_fwd(q, k, v, mz):
    s = q.shape[1]
    t = min(TQ, s)

    def body(q_ref, k_ref, v_ref, mz_ref, o_ref, yb_ref, lse_ref, m_s, l_s, acc_s):
        i = pl.program_id(1)
        m_s[...] = jnp.full_like(m_s, -jnp.inf)
        l_s[...] = jnp.zeros_like(l_s)
        acc_s[...] = jnp.zeros_like(acc_s)

        def step(j, groups):
            rows = pl.ds(pl.multiple_of(j * t, t), t)
            for hh in range(FLASH_HEADS):
                for r0, nr, masked in groups:
                    r = slice(r0, r0 + nr)
                    sc = _dot_nt(q_ref[hh, r, :], k_ref[hh, rows, :])
                    if masked:
                        sc = jnp.where(_causal_mask(t), sc, -jnp.inf)
                    m_prev = m_s[hh, r, :]
                    m_new = jnp.maximum(m_prev, jnp.max(sc, axis=-1, keepdims=True))
                    p = jnp.exp2(sc - jnp.tile(m_new, (1, t // 128)))
                    alpha = jnp.exp2(m_prev - m_new)
                    l_s[hh, r, :] = alpha * l_s[hh, r, :] + jnp.sum(p, axis=-1, keepdims=True)
                    acc_s[hh, r, :] = alpha * acc_s[hh, r, :] + _dot(p.astype(BF16), v_ref[hh, rows, :])
                    m_s[hh, r, :] = m_new

        def loop_body(j, carry):
            step(j, ((0, 2 * t, False),))
            return carry

        lax.fori_loop(0, 2 * i, loop_body, 0)
        step(2 * i, ((0, t, True), (t, t, False)))
        step(2 * i + 1, ((t, t, True),))
        for hh in range(FLASH_HEADS):
            cols = slice(hh * HEAD_DIM, (hh + 1) * HEAD_DIM)
            out = acc_s[hh] / l_s[hh]
            o_ref[:, cols] = out
            silu_z, _ = _silu_parts(mz_ref[:, cols])
            yb_ref[:, cols] = (out * silu_z).astype(BF16)
            lse_ref[hh] = m_s[hh] + jnp.log2(l_s[hh])

    nh = FLASH_HEADS
    t2 = 2 * t
    col = pl.BlockSpec((t2, nh * HEAD_DIM), lambda h, i: (i, h))
    return pl.pallas_call(
        body, name="flash_fwd", grid=(HEADS // nh, s // t2),
        in_specs=[pl.BlockSpec((nh, t2, QK_PAD), lambda h, i: (h, i, 0)),
                  pl.BlockSpec((nh, s, QK_PAD), lambda h, i: (h, 0, 0)),
                  pl.BlockSpec((nh, s, HEAD_DIM), lambda h, i: (h, 0, 0)), col],
        out_specs=[col, col, pl.BlockSpec((nh, t2, 128), lambda h, i: (h, i, 0))],
        out_shape=[jax.ShapeDtypeStruct((s, D_MODEL), F32), jax.ShapeDtypeStruct((s, D_MODEL), BF16),
                   jax.ShapeDtypeStruct((HEADS, s, 128), F32)],
        scratch_shapes=[pltpu.VMEM((nh, t2, 128), F32), pltpu.VMEM((nh, t2, 128), F32),
                        pltpu.VMEM((nh, t2, HEAD_DIM), F32)],
        compiler_params=_params(2),
    )(q, k, v, mz)


def _flash_bwd(q, k, v, dyb, mz, o_att, lse, tabs):
    s = q.shape[1]
    t = min(TQ, s)

    def body(q_ref, k_ref, v_ref, dyb_ref, mz_ref, o_ref, lse_ref, c_ref, sa_ref, sb_ref,
             dq_ref, dk_ref, dv_ref, dmz_ref, dq_s, delta_s, do_s):
        i = pl.program_id(1)

        @pl.when(i == 0)
        def _():
            dk_ref[...] = jnp.zeros_like(dk_ref)
            dv_ref[...] = jnp.zeros_like(dv_ref)

        silu_z, dsilu_z = _silu_parts(mz_ref[...])
        dyb_v = dyb_ref[...]
        out = o_ref[...]
        do32 = dyb_v * silu_z
        dmz_ref[...] = (dyb_v * out * dsilu_z).astype(BF16)
        delta_s[...] = jnp.broadcast_to(jnp.sum(do32 * out, axis=-1, keepdims=True), (2 * t, 128))
        do_s[...] = do32.astype(BF16)
        dq_s[...] = jnp.zeros_like(dq_s)

        def step(j, modes):
            rows = pl.ds(pl.multiple_of(j * t, t), t)
            kj = k_ref[rows, :]
            vj = v_ref[rows, :]
            dv_acc = None
            dk_acc = None
            for ch, masked in enumerate(modes):
                if masked is None:
                    continue
                r = slice(ch * t, (ch + 1) * t)
                qv = q_ref[r, :]
                do = do_s[r, :]
                sc = _dot_nt(qv, kj)
                if masked:
                    sc = jnp.where(_causal_mask(t), sc, -jnp.inf)
                p = jnp.exp2(sc - jnp.tile(lse_ref[r, :], (1, t // 128)))
                dp = _dot_nt(do, vj)
                ds = (p * (dp - jnp.tile(delta_s[r, :], (1, t // 128)))).astype(BF16)
                dv_c = _dot_tn(p.astype(BF16), do)
                dk_c = _dot_tn(ds, qv)
                dv_acc = dv_c if dv_acc is None else dv_acc + dv_c
                dk_acc = dk_c if dk_acc is None else dk_acc + dk_c
                dq_s[r, :] += _dot(ds, kj)
            dv_ref[rows, :] += dv_acc
            dk_ref[rows, :] += dk_acc

        def loop_body(j, carry):
            step(j, (False, False))
            return carry

        lax.fori_loop(0, 2 * i, loop_body, 0)
        step(2 * i, (True, False))
        step(2 * i + 1, (None, True))
        dq = dq_s[...] * ATT_SCALE
        dq_ref[:, 0:128] = dq[:, 0:128].astype(BF16)
        dq_ref[:, 128:256] = _rope_bwd(dq[:, 128:256], c_ref[...], sa_ref[...], sb_ref[...]).astype(BF16)

    t2 = 2 * t
    col = pl.BlockSpec((t2, HEAD_DIM), lambda h, i: (i, h))
    tab = pl.BlockSpec((t2, 128), lambda h, i: (i, 0))
    return pl.pallas_call(
        body, name="flash_bwd", grid=(HEADS, s // t2),
        in_specs=[pl.BlockSpec((None, t2, QK_PAD), lambda h, i: (h, i, 0)),
                  pl.BlockSpec((None, s, QK_PAD), lambda h, i: (h, 0, 0)),
                  pl.BlockSpec((None, s, HEAD_DIM), lambda h, i: (h, 0, 0)),
                  col, col, col, pl.BlockSpec((None, t2, 128), lambda h, i: (h, i, 0)), tab, tab, tab],
        out_specs=[pl.BlockSpec((None, t2, QK_PAD), lambda h, i: (h, i, 0)),
                   pl.BlockSpec((None, s, QK_PAD), lambda h, i: (h, 0, 0)),
                   pl.BlockSpec((None, s, HEAD_DIM), lambda h, i: (h, 0, 0)), col],
        out_shape=[jax.ShapeDtypeStruct((HEADS, s, QK_PAD), BF16), jax.ShapeDtypeStruct((HEADS, s, QK_PAD), F32),
                   jax.ShapeDtypeStruct((HEADS, s, HEAD_DIM), F32), jax.ShapeDtypeStruct((s, D_MODEL), BF16)],
        scratch_shapes=[pltpu.VMEM((t2, QK_PAD), F32), pltpu.VMEM((t2, 128), F32), pltpu.VMEM((t2, HEAD_DIM), BF16)],
        compiler_params=_params(2),
    )(q, k, v, dyb, mz, o_att, lse, *tabs)


def _mla_bwd_proj(dq, dk, dv, cqn, ckvn, ms, q_a_g, kv_a_g, wuq3, wukv3, tabs):
    s = ms.shape[0]
    tm = min(TM_FUSED, s)

    def body(dq_ref, dk_ref, dv_ref, cqn_ref, ckvn_ref, ms_ref, qg_ref, kvg_ref, wuq_ref, wukv_ref,
             c_ref, sa_ref, sb_ref, dms_ref, dwuq_ref, dwukv_ref, dqg_ref, dkvg_ref):
        @pl.when(pl.program_id(0) == 0)
        def _():
            dwuq_ref[...] = jnp.zeros_like(dwuq_ref)
            dwukv_ref[...] = jnp.zeros_like(dwukv_ref)
            dqg_ref[...] = jnp.zeros_like(dqg_ref)
            dkvg_ref[...] = jnp.zeros_like(dkvg_ref)

        cqn = cqn_ref[...]
        ckvn = ckvn_ref[...]
        dcqn = jnp.zeros((tm, Q_LORA), F32)
        dckvn = jnp.zeros((tm, KV_LORA), F32)
        dkpe = jnp.zeros((tm, 128), F32)
        for h in range(HEADS):
            dqh = dq_ref[h]
            dcqn += _dot_nt(dqh, wuq_ref[h])
            dwuq_ref[h] += _dot_tn(cqn, dqh)
            dkh = dk_ref[h] * LN2
            dkvh = jnp.concatenate([dkh[:, 0:128], dv_ref[h]], axis=1).astype(BF16)
            dckvn += _dot_nt(dkvh, wukv_ref[h])
            dwukv_ref[h] += _dot_tn(ckvn, dkvh)
            dkpe += dkh[:, 128:256]
        dcq, dqg_rows = _rms_bwd(ms_ref[:, 0:Q_LORA], qg_ref[...], dcqn)
        dckv, dkvg_rows = _rms_bwd(ms_ref[:, Q_LORA:Q_LORA + KV_LORA], kvg_ref[...], dckvn)
        dqg_ref[...] += jnp.sum(dqg_rows, axis=0, keepdims=True)
        dkvg_ref[...] += jnp.sum(dkvg_rows, axis=0, keepdims=True)
        dms_ref[:, 0:Q_LORA] = dcq.astype(BF16)
        dms_ref[:, Q_LORA:Q_LORA + KV_LORA] = dckv.astype(BF16)
        dms_ref[:, Q_LORA + KV_LORA:MS_COLS] = _rope_bwd(dkpe, c_ref[...], sa_ref[...], sb_ref[...]).astype(BF16)

    tab = pl.BlockSpec((tm, 128), lambda i: (i, 0))
    wq = pl.BlockSpec((HEADS, Q_LORA, QK_PAD), lambda i: (0, 0, 0))
    wkv = pl.BlockSpec((HEADS, KV_LORA, 256), lambda i: (0, 0, 0))
    qg = pl.BlockSpec((1, Q_LORA), lambda i: (0, 0))
    kvg = pl.BlockSpec((1, KV_LORA), lambda i: (0, 0))
    return pl.pallas_call(
        body, name="mla_bwd_proj", grid=(s // tm,),
        in_specs=[pl.BlockSpec((HEADS, tm, QK_PAD), lambda i: (0, i, 0)),
                  pl.BlockSpec((HEADS, tm, QK_PAD), lambda i: (0, i, 0)),
                  pl.BlockSpec((HEADS, tm, HEAD_DIM), lambda i: (0, i, 0)),
                  pl.BlockSpec((tm, Q_LORA), lambda i: (i, 0)), pl.BlockSpec((tm, KV_LORA), lambda i: (i, 0)),
                  pl.BlockSpec((tm, MS_COLS), lambda i: (i, 0)), qg, kvg, wq, wkv, tab, tab, tab],
        out_specs=[pl.BlockSpec((tm, MS_COLS), lambda i: (i, 0)), wq, wkv, qg, kvg],
        out_shape=[jax.ShapeDtypeStruct((s, MS_COLS), BF16), jax.ShapeDtypeStruct((HEADS, Q_LORA, QK_PAD), F32),
                   jax.ShapeDtypeStruct((HEADS, KV_LORA, 256), F32),
                   jax.ShapeDtypeStruct((1, Q_LORA), F32), jax.ShapeDtypeStruct((1, KV_LORA), F32)],
        compiler_params=_params(1),
    )(dq, dk, dv, cqn, ckvn, ms, q_a_g, kv_a_g, wuq3, wukv3, *tabs)


def _merge_fused(ya, yb, glog, b_gate, x, tgt, fg, wproj):
    s = x.shape[0]
    tm = min(TM_FUSED, s)

    def body(ya_ref, yb_ref, g0_ref, g1_ref, b0_ref, b1_ref, x_ref, t_ref, fg_ref, w_ref,
             mg_ref, dx2_ref, dx2b_ref, dya_ref, dyb_ref, dgl_ref, dpa_ref, dpb_ref, loss_ref, dfg_ref, dbg_ref):
        @pl.when(pl.program_id(0) == 0)
        def _():
            loss_ref[...] = jnp.zeros_like(loss_ref)
            dfg_ref[...] = jnp.zeros_like(dfg_ref)
            dbg_ref[...] = jnp.zeros_like(dbg_ref)

        pa = _dot(ya_ref[...], w_ref[0])
        pb = _dot(yb_ref[...], w_ref[1])
        g0 = jax.nn.sigmoid(g0_ref[...] + b0_ref[...])
        g1 = jax.nn.sigmoid(g1_ref[...] + b1_ref[...])
        merged = (g0 * pa + g1 * pb).astype(BF16)
        mg_ref[...] = merged
        x2 = x_ref[...] + _dot(merged, w_ref[2])
        fg_v = fg_ref[...]
        err = _rms(x2, fg_v) - t_ref[...]
        loss_ref[...] += 0.5 * jnp.sum(jnp.mean(err * err, axis=-1, keepdims=True), axis=0, keepdims=True)
        dx2, dfg_rows = _rms_bwd(x2, fg_v, err * (1.0 / D_MODEL))
        dx2_ref[...] = dx2
        dfg_ref[...] += jnp.sum(dfg_rows, axis=0, keepdims=True)

        dx2b = dx2.astype(BF16)
        dx2b_ref[...] = dx2b
        dmg = _dot_nt(dx2b, w_ref[2])
        dpa = (dmg * g0).astype(BF16)
        dpb = (dmg * g1).astype(BF16)
        dpa_ref[...] = dpa
        dpb_ref[...] = dpb
        dgl0 = dmg * pa * g0 * (1.0 - g0)
        dgl1 = dmg * pb * g1 * (1.0 - g1)
        dgl_ref[:, 0:D_MODEL] = dgl0.astype(BF16)
        dgl_ref[:, D_MODEL:2 * D_MODEL] = dgl1.astype(BF16)
        dbg_ref[:, 0:D_MODEL] += jnp.sum(dgl0, axis=0, keepdims=True)
        dbg_ref[:, D_MODEL:2 * D_MODEL] += jnp.sum(dgl1, axis=0, keepdims=True)
        dya_ref[...] = _dot_nt(dpa, w_ref[0])
        dyb_ref[...] = _dot_nt(dpb, w_ref[1])

    row = pl.BlockSpec((tm, D_MODEL), lambda i: (i, 0))
    row1 = pl.BlockSpec((tm, D_MODEL), lambda i: (i, 1))
    row2 = pl.BlockSpec((tm, 2 * D_MODEL), lambda i: (i, 0))
    vec = pl.BlockSpec((1, D_MODEL), lambda i: (0, 0))
    vec1 = pl.BlockSpec((1, D_MODEL), lambda i: (0, 1))
    vec2 = pl.BlockSpec((1, 2 * D_MODEL), lambda i: (0, 0))
    f32_rows = jax.ShapeDtypeStruct((s, D_MODEL), F32)
    bf16_rows = jax.ShapeDtypeStruct((s, D_MODEL), BF16)
    return pl.pallas_call(
        body, name="merge_fused", grid=(s // tm,),
        in_specs=[row, row, row, row1, vec, vec1, row, row, vec, pl.BlockSpec((3, D_MODEL, D_MODEL), lambda i: (0, 0, 0))],
        out_specs=[row, row, row, row, row, row2, row, row, pl.BlockSpec((1, 128), lambda i: (0, 0)), vec, vec2],
        out_shape=[bf16_rows, f32_rows, bf16_rows, f32_rows, f32_rows, jax.ShapeDtypeStruct((s, 2 * D_MODEL), BF16),
                   bf16_rows, bf16_rows, jax.ShapeDtypeStruct((1, 128), F32), jax.ShapeDtypeStruct((1, D_MODEL), F32),
                   jax.ShapeDtypeStruct((1, 2 * D_MODEL), F32)],
        compiler_params=_params(1),
    )(ya, yb, glog, glog, b_gate, b_gate, x, tgt, fg, wproj)


def _proj_fused(x, g, w_int, r_blk):
    s = x.shape[0]
    tm = min(TM_FUSED, s)

    def body(x_ref, g_ref, w_hbm, r_ref, h_ref, hg_ref, ms_ref, mz_ref, gl_ref, or_ref, w_s, sem, send_sems, recv_sems):
        mx, my, mc = _me()
        chips = _other_chips(mx, my)
        mine, theirs = _cols(mc), _cols(1 - mc)

        def copy(k, src, dst, to):
            return pltpu.make_async_remote_copy(src_ref=src, dst_ref=dst, send_sem=send_sems.at[k],
                                                recv_sem=recv_sems.at[k], device_id=to, device_id_type=MESH_ID)

        def sends():
            return [copy(j, r_ref.at[:, mine], or_ref.at[2 * mx + my, :, mine], (cx, cy, mc))
                    for j, (cx, cy) in enumerate(chips)]

        @pl.when(pl.program_id(0) == 0)
        def _():
            for cp in sends():
                cp.start()
            cp = pltpu.make_async_copy(w_hbm, w_s, sem)
            cp.start()
            cp.wait()

        h = _rms(x_ref[...], g_ref[...]).astype(BF16)
        h_ref[...] = h
        for j in range(4):
            hg_ref[j] = _dot_nt(h, w_s[j * D_MODEL:(j + 1) * D_MODEL, :])
        ms = _dot_nt(h, w_s[4096:4096 + MS_COLS, :])
        lane = lax.broadcasted_iota(jnp.int32, ms.shape, 1)
        ms_ref[...] = jnp.where(lane < 704, ms, 0.0)
        mz_ref[...] = _dot_nt(h, w_s[4800:5824, :])
        for j in range(2):
            gl_ref[:, j * D_MODEL:(j + 1) * D_MODEL] = _dot_nt(h, w_s[5824 + j * D_MODEL:5824 + (j + 1) * D_MODEL, :])

        @pl.when(pl.program_id(0) == s // tm - 1)
        def _():
            passed = []
            for j, (cx, cy) in enumerate(chips):
                landed = or_ref.at[2 * cx + cy, :, mine]
                copy(j, landed, landed, (cx, cy, mc)).wait_recv()
                fwd = copy(3 + j, landed, landed, (mx, my, 1 - mc))
                fwd.start()
                passed.append(fwd)
            for j, (cx, cy) in enumerate(chips):
                other = or_ref.at[2 * cx + cy, :, theirs]
                copy(3 + j, other, other, (mx, my, 1 - mc)).wait_recv()
            for cp in sends() + passed:
                cp.wait_send()

    row = pl.BlockSpec((tm, D_MODEL), lambda i: (i, 0))
    outs = pl.pallas_call(
        body, name="proj_fused", grid=(s // tm,),
        in_specs=[row, pl.BlockSpec((1, D_MODEL), lambda i: (0, 0)), ANY, ANY],
        out_specs=[row, pl.BlockSpec((4, tm, D_MODEL), lambda i: (0, i, 0)), pl.BlockSpec((tm, MS_COLS), lambda i: (i, 0)),
                   row, pl.BlockSpec((tm, 2 * D_MODEL), lambda i: (i, 0)), ANY],
        out_shape=[jax.ShapeDtypeStruct((s, D_MODEL), BF16), jax.ShapeDtypeStruct((4, s, D_MODEL), F32),
                   jax.ShapeDtypeStruct((s, MS_COLS), F32), jax.ShapeDtypeStruct((s, D_MODEL), F32),
                   jax.ShapeDtypeStruct((s, 2 * D_MODEL), F32), jax.ShapeDtypeStruct((N_CHIPS,) + r_blk.shape, r_blk.dtype)],
        scratch_shapes=[pltpu.VMEM(w_int.shape, BF16), pltpu.SemaphoreType.DMA,
                        pltpu.SemaphoreType.DMA((6,)), pltpu.SemaphoreType.DMA((6,))],
        compiler_params=_params(1),
    )(x, g, w_int, r_blk)
    gr = lax.dynamic_update_slice(outs[5], r_blk[None], (2 * lax.axis_index("x") + lax.axis_index("y"), 0, 0))
    return (*outs[:5], gr)


def _dh_fused(dhg, dms, dmz, dglog, w_int, x, g, dx2, hw, hr):
    s = x.shape[0]
    tm = min(512, s)
    nw, nr = hw.shape[0] // N_CHIPS, hr.shape[0] // N_CHIPS

    def body(dhg_ref, dms_ref, dmz_ref, dgl_ref, w_hbm, x_ref, g_ref, dx2_ref, hw_ref, hr_ref,
             dx_ref, dg_ref, lw_ref, lr_ref, w_s, sem, send_sems, recv_sems):
        def scatter_copies():
            mx, my, mc = _me()
            return [pltpu.make_async_remote_copy(
                src_ref=src.at[pl.ds((2 * cx + cy) * n, n), :], dst_ref=dst.at[j], send_sem=send_sems.at[3 * a + j],
                recv_sem=recv_sems.at[3 * a + j], device_id=(cx, cy, mc), device_id_type=MESH_ID)
                for a, (src, dst, n) in enumerate([(hw_ref, lw_ref, nw), (hr_ref, lr_ref, nr)])
                for j, (cx, cy) in enumerate(_other_chips(mx, my))]

        @pl.when(pl.program_id(0) == 0)
        def _():
            for cp in scatter_copies():
                cp.start()
            dg_ref[...] = jnp.zeros_like(dg_ref)
            cp = pltpu.make_async_copy(w_hbm, w_s, sem)
            cp.start()
            cp.wait()

        dh = _dot(dms_ref[...], w_s[4096:4096 + MS_COLS, :]) + _dot(dmz_ref[...], w_s[4800:5824, :])
        for j in range(4):
            dh += _dot(dhg_ref[j], w_s[j * D_MODEL:(j + 1) * D_MODEL, :])
        for j in range(2):
            dh += _dot(dgl_ref[:, j * D_MODEL:(j + 1) * D_MODEL], w_s[5824 + j * D_MODEL:5824 + (j + 1) * D_MODEL, :])
        dx, dg_rows = _rms_bwd(x_ref[...], g_ref[...], dh)
        dx_ref[...] = dx + dx2_ref[...]
        dg_ref[...] += jnp.sum(dg_rows, axis=0, keepdims=True)

        @pl.when(pl.program_id(0) == s // tm - 1)
        def _():
            for cp in scatter_copies():
                cp.wait()

    row = pl.BlockSpec((tm, D_MODEL), lambda i: (i, 0))
    vec = pl.BlockSpec((1, D_MODEL), lambda i: (0, 0))
    return pl.pallas_call(
        body, name="dh_fused", grid=(s // tm,),
        in_specs=[pl.BlockSpec((4, tm, D_MODEL), lambda i: (0, i, 0)), pl.BlockSpec((tm, MS_COLS), lambda i: (i, 0)), row,
                  pl.BlockSpec((tm, 2 * D_MODEL), lambda i: (i, 0)), ANY, row, vec, row, ANY, ANY],
        out_specs=[row, vec, ANY, ANY],
        out_shape=[jax.ShapeDtypeStruct((s, D_MODEL), F32), jax.ShapeDtypeStruct((1, D_MODEL), F32),
                   jax.ShapeDtypeStruct((3, nw, HALF_COLS), hw.dtype), jax.ShapeDtypeStruct((3, nr, HALF_COLS), hr.dtype)],
        scratch_shapes=[pltpu.VMEM(w_int.shape, BF16), pltpu.SemaphoreType.DMA,
                        pltpu.SemaphoreType.DMA((6,)), pltpu.SemaphoreType.DMA((6,))],
        compiler_params=_params(1),
    )(dhg, dms, dmz, dglog, w_int, x, g, dx2, hw, hr)


def _local_step(x, tgt, w_int, r_blk, norm_g, b_gate, lb_logits, hg_norm_g, q_a_g, kv_a_g, fg):
    s = x.shape[0]
    tabs = _rope_tables(s)

    h, hg, ms, mz, glog, gr = _proj_fused(x, norm_g, w_int, r_blk)
    w_uq, w_ukv, wproj = _unpack_rest_weights(gr)
    wuq3 = jnp.pad(w_uq.reshape(Q_LORA, HEADS, QK_DIM).transpose(1, 0, 2), ((0, 0), (0, 0), (0, QK_PAD - QK_DIM)))
    wukv3 = w_ukv.reshape(KV_LORA, HEADS, 256).transpose(1, 0, 2)
    o_pre, ya, st0 = _hgrn_fwd(hg, lb_logits, hg_norm_g)
    q, k, v, cqn, ckvn = _mla_pre(ms, q_a_g, kv_a_g, wuq3, wukv3, tabs)
    o_att, yb, lse = _flash_fwd(q, k, v, mz)
    merged, dx2, dx2b, dya, dyb, dglog, dpa, dpb, loss, dfg, dbg = _merge_fused(ya, yb, glog, b_gate, x, tgt, fg, wproj)

    d_wout = _mm_tn(merged, dx2b, name="dw_out")
    d_wpa = _mm_tn(ya, dpa, name="dw_proj_a")
    d_wpb = _mm_tn(yb, dpb, name="dw_proj_b")
    dhg, dlb, dhgg = _hgrn_bwd(hg, o_pre, dya, st0, lb_logits, hg_norm_g)
    dq, dk, dv, dmz = _flash_bwd(q, k, v, dyb, mz, o_att, lse, tabs)
    dms, d_wuq3, d_wukv3, dqg, dkvg = _mla_bwd_proj(dq, dk, dv, cqn, ckvn, ms, q_a_g, kv_a_g, wuq3, wukv3, tabs)
    d_hg = _mm_tn(dhg, h, name="dw_in_hg")
    d_ms = _mm_tn(dms, h, name="dw_in_ms")
    d_mz = _mm_tn(dmz, h, name="dw_in_mz")
    d_gl = _mm_tn(dglog, h, name="dw_in_gate")
    d_w_int = jnp.concatenate([d_hg.reshape(4 * D_MODEL, D_MODEL), d_ms[0:704], d_mz, d_gl], axis=0)
    small = {"b_gate": dbg, "lb": dlb, "hg_norm_g": dhgg, "q_a_g": dqg, "kv_a_g": dkvg, "final_norm_g": dfg}
    dh_args = (dhg, dms, dmz, dglog, w_int, x, norm_g, dx2)
    return loss, dh_args, d_w_int, d_wuq3, d_wukv3, (d_wpa, d_wpb, d_wout), small


def _pack_rest(w_uq_b, w_ukv_b, wpa_b, wpb_b, wout_b):
    return jnp.concatenate([w_uq_b.reshape(144, D_MODEL), w_ukv_b.reshape(128, D_MODEL), wpa_b, wpb_b, wout_b], axis=0)


def _unpack_rest(p):
    return (p[0:144].reshape(Q_LORA, 384), p[144:272].reshape(KV_LORA, 512), p[272:528], p[528:784], p[784:1040])


def _pack_rest_grads(d_wuq3, d_wukv3, d_proj):
    d_wuq = d_wuq3.transpose(1, 0, 2)[:, :, 0:QK_DIM].reshape(Q_LORA, HEADS * QK_DIM)
    d_wukv = d_wukv3.transpose(1, 0, 2).reshape(KV_LORA, HEADS * 256)
    blocks = []
    for b in range(N_CHIPS):
        rows = slice(b * 256, (b + 1) * 256)
        blocks.append(_pack_rest(d_wuq[:, b * 384:(b + 1) * 384], d_wukv[:, b * 512:(b + 1) * 512],
                                 d_proj[0][rows], d_proj[1][rows], d_proj[2][rows]))
    return jnp.stack(blocks, axis=0)


def _unpack_rest_weights(g):
    parts = [_unpack_rest(g[b]) for b in range(N_CHIPS)]
    w_uq, w_ukv = (jnp.concatenate([p[n] for p in parts], axis=1) for n in range(2))
    wproj = jnp.stack([jnp.concatenate([p[n] for p in parts], axis=0) for n in range(2, 5)], axis=0)
    return w_uq, w_ukv, wproj


MESH_ID = pl.DeviceIdType.MESH
ANY = pl.BlockSpec(memory_space=pl.ANY)
HALF_COLS = D_MODEL // 2


def _me():
    return lax.axis_index("x"), lax.axis_index("y"), lax.axis_index("c")


def _other_chips(x, y):
    return [(1 - x, y), (x, 1 - y), (1 - x, 1 - y)]


def _cols(c):
    return pl.ds(c * HALF_COLS, HALF_COLS)


RELAY_TOP = 992


def _gather_weights(w_blk):
    bot = W_IN_BLK - RELAY_TOP

    def body(w_ref, ow_ref, send_sems, recv_sems):
        x, y, c = _me()
        me, xn, yn, dg = 2 * x + y, 2 * (1 - x) + y, 2 * x + (1 - y), 2 * (1 - x) + (1 - y)
        to_x, to_y, to_sib = (1 - x, y, c), (x, 1 - y, c), (x, y, 1 - c)
        mine, theirs = _cols(c), _cols(1 - c)
        top, low = pl.ds(0, RELAY_TOP), pl.ds(RELAY_TOP, bot)

        def copy(k, src, dst, to):
            return pltpu.make_async_remote_copy(src_ref=src, dst_ref=dst, send_sem=send_sems.at[k],
                                                recv_sem=recv_sems.at[k], device_id=to, device_id_type=MESH_ID)

        def same(k, ref, to):
            return copy(k, ref, ref, to)

        own = [copy(0, w_ref.at[:, mine], ow_ref.at[me, :, mine], to_x),
               copy(1, w_ref.at[:, mine], ow_ref.at[me, :, mine], to_y)]
        for cp in own:
            cp.start()
        from_x, from_y = ow_ref.at[xn, :, mine], ow_ref.at[yn, :, mine]
        same(0, from_x, to_x).wait_recv()
        relay_y = same(2, ow_ref.at[xn, top, mine], to_y)
        pass_x = same(4, from_x, to_sib)
        relay_y.start()
        pass_x.start()
        same(1, from_y, to_y).wait_recv()
        relay_x = same(3, ow_ref.at[yn, low, mine], to_x)
        pass_y = same(5, from_y, to_sib)
        relay_x.start()
        pass_y.start()
        same(2, ow_ref.at[dg, top, mine], to_y).wait_recv()
        same(3, ow_ref.at[dg, low, mine], to_x).wait_recv()
        pass_d = same(6, ow_ref.at[dg, :, mine], to_sib)
        pass_d.start()
        for k, blk in ((4, xn), (5, yn), (6, dg)):
            same(k, ow_ref.at[blk, :, theirs], to_sib).wait_recv()
        for cp in own + [relay_y, relay_x, pass_x, pass_y, pass_d]:
            cp.wait_send()

    gw = pl.pallas_call(
        body, name="gather_weights", in_specs=[ANY], out_specs=ANY,
        out_shape=jax.ShapeDtypeStruct((N_CHIPS,) + w_blk.shape, w_blk.dtype),
        scratch_shapes=[pltpu.SemaphoreType.DMA((7,)), pltpu.SemaphoreType.DMA((7,))],
    )(w_blk)
    return lax.dynamic_update_slice(gw, w_blk[None], (2 * lax.axis_index("x") + lax.axis_index("y"), 0, 0))


def _swap_halves(gw, gr):
    def body(gw_ref, gr_ref, lw_ref, lr_ref, send_sems, recv_sems):
        x, y, c = _me()
        cps = [pltpu.make_async_remote_copy(
            src_ref=src, dst_ref=dst, send_sem=send_sems.at[a], recv_sem=recv_sems.at[a],
            device_id=(x, y, 1 - c), device_id_type=MESH_ID)
            for a, (src, dst) in enumerate([(gw_ref.at[:, _cols(1 - c)], lw_ref),
                                            (gr_ref.at[:, :, _cols(1 - c)], lr_ref)])]
        for cp in cps:
            cp.start()
        for cp in cps:
            cp.wait()

    return pl.pallas_call(
        body, name="grad_swap_halves", in_specs=[ANY, ANY], out_specs=[ANY, ANY],
        out_shape=[jax.ShapeDtypeStruct((gw.shape[0], HALF_COLS), gw.dtype),
                   jax.ShapeDtypeStruct(gr.shape[:2] + (HALF_COLS,), gr.dtype)],
        scratch_shapes=[pltpu.SemaphoreType.DMA((2,)), pltpu.SemaphoreType.DMA((2,))],
    )(gw, gr)


def _swap_reduced(rw, rr):
    def body(rw_ref, rr_ref, ow_ref, or_ref, send_sems, recv_sems):
        x, y, c = _me()
        cps = [pltpu.make_async_remote_copy(
            src_ref=src, dst_ref=dst, send_sem=send_sems.at[a], recv_sem=recv_sems.at[a],
            device_id=(x, y, 1 - c), device_id_type=MESH_ID)
            for a, (src, dst) in enumerate([(rw_ref, ow_ref), (rr_ref, or_ref)])]
        for cp in cps:
            cp.start()
        for cp in cps:
            cp.wait()

    return pl.pallas_call(
        body, name="grad_swap_reduced", in_specs=[ANY, ANY], out_specs=[ANY, ANY],
        out_shape=[jax.ShapeDtypeStruct(rw.shape, rw.dtype), jax.ShapeDtypeStruct(rr.shape, rr.dtype)],
        scratch_shapes=[pltpu.SemaphoreType.DMA((2,)), pltpu.SemaphoreType.DMA((2,))],
    )(rw, rr)


def _join_cols(mine, theirs):
    first = lax.axis_index("c") == 0
    return jnp.concatenate([jnp.where(first, mine, theirs), jnp.where(first, theirs, mine)], axis=1)


def _gather_small(vec):
    def body(v_ref, out_ref, send_sems, recv_sems, local_sem):
        x, y, c = _me()
        my_id = 4 * x + 2 * y + c
        mine = pltpu.make_async_copy(v_ref, out_ref.at[my_id], local_sem)
        mine.start()
        cps = []
        for r in range(1, N_DEV):
            peer = (x ^ (r >> 2), y ^ ((r >> 1) & 1), c ^ (r & 1))
            cps.append(pltpu.make_async_remote_copy(
                src_ref=v_ref, dst_ref=out_ref.at[my_id], send_sem=send_sems.at[r - 1],
                recv_sem=recv_sems.at[r - 1], device_id=peer, device_id_type=MESH_ID))
        for cp in cps:
            cp.start()
        for cp in cps:
            cp.wait()
        mine.wait()

    return pl.pallas_call(
        body, name="gather_small", in_specs=[ANY], out_specs=ANY,
        out_shape=jax.ShapeDtypeStruct((N_DEV, 1, SMALL_COLS), vec.dtype),
        scratch_shapes=[pltpu.SemaphoreType.DMA((N_DEV - 1,)), pltpu.SemaphoreType.DMA((N_DEV - 1,)),
                        pltpu.SemaphoreType.DMA],
    )(vec)


def _add_cores(c_idx, g, landed, *, tm, name):
    r = g.shape[0]

    def body(c_ref, g_ref, l_ref, o32_ref, o16_ref):
        acc = g_ref[...] + l_ref[...]
        o32_ref[...] = acc
        o16_ref[...] = acc.astype(BF16)

    half = pl.BlockSpec((tm, HALF_COLS), lambda i, c_ref: (i, 0))
    grid_spec = pltpu.PrefetchScalarGridSpec(
        num_scalar_prefetch=1, grid=(r // tm,),
        in_specs=[pl.BlockSpec((tm, HALF_COLS), lambda i, c_ref: (i, c_ref[0])), half], out_specs=[half, half])
    return pl.pallas_call(
        body, name=name, grid_spec=grid_spec,
        out_shape=[jax.ShapeDtypeStruct((r, HALF_COLS), F32), jax.ShapeDtypeStruct((r, HALF_COLS), BF16)],
        compiler_params=_params(1),
    )(c_idx, g, landed)


def _add_chips(chip_idx, h32, landed, *, tm, name):
    n = landed.shape[1]
    per = n // tm

    def body(chip_ref, h_ref, l_ref, o_ref):
        acc = h_ref[...]
        for j in range(3):
            acc = acc + l_ref[j].astype(F32)
        o_ref[...] = acc

    grid_spec = pltpu.PrefetchScalarGridSpec(
        num_scalar_prefetch=1, grid=(per,),
        in_specs=[pl.BlockSpec((tm, HALF_COLS), lambda i, chip_ref: (chip_ref[0] * per + i, 0)),
                  pl.BlockSpec((3, tm, HALF_COLS), lambda i, chip_ref: (0, i, 0))],
        out_specs=pl.BlockSpec((tm, HALF_COLS), lambda i, chip_ref: (i, 0)))
    return pl.pallas_call(
        body, name=name, grid_spec=grid_spec, out_shape=jax.ShapeDtypeStruct((n, HALF_COLS), F32),
        compiler_params=_params(1),
    )(chip_idx, h32, landed)


def _pack_small(small, lb_logits, loss):
    def body(ng_ref, bg_ref, dlb_ref, lbl_ref, hgg_ref, qg_ref, kvg_ref, fg_ref, loss_ref, out_ref):
        out_ref[...] = jnp.zeros_like(out_ref)
        out_ref[:, 0:1024] = ng_ref[...]
        out_ref[:, 1024:3072] = bg_ref[...]
        _, p0p1 = _lower_bound(lbl_ref[...])
        dl0 = dlb_ref[...] * p0p1
        out_ref[:, 3072:4096] = dl0
        out_ref[:, 4096:5120] = -dl0
        hgg = hgg_ref[0]
        for h in range(1, HEADS):
            hgg = hgg + hgg_ref[h]
        out_ref[:, 5120:5248] = hgg
        out_ref[:, 5248:5632] = qg_ref[...]
        out_ref[:, 5632:5888] = kvg_ref[...]
        out_ref[:, 5888:6912] = fg_ref[...]
        out_ref[:, 6912:7040] = loss_ref[...]

    return pl.pallas_call(
        body, name="pack_small", out_shape=jax.ShapeDtypeStruct((1, SMALL_COLS), F32),
    )(small["norm_g"], small["b_gate"], small["lb"], lb_logits, small["hg_norm_g"], small["q_a_g"],
      small["kv_a_g"], small["final_norm_g"], loss)


def _adamw_math(w, g, m, v):
    nm = ADAM_B1 * m + (1.0 - ADAM_B1) * g
    nv = ADAM_B2 * v + (1.0 - ADAM_B2) * (g * g)
    m_hat = nm / (1.0 - ADAM_B1 ** ADAM_STEP)
    v_hat = nv / (1.0 - ADAM_B2 ** ADAM_STEP)
    return -ADAM_LR * (m_hat / (jnp.sqrt(v_hat) + ADAM_EPS) + ADAM_WD * w), nm, nv


def _adamw(w, g, m, v, *, name, tm):
    r, cols = w.shape

    def body(w_ref, g_ref, m_ref, v_ref, d_ref, nm_ref, nv_ref):
        d_ref[...], nm_ref[...], nv_ref[...] = _adamw_math(w_ref[...], g_ref[...], m_ref[...], v_ref[...])

    row = pl.BlockSpec((tm, cols), lambda i: (i, 0))
    shp = jax.ShapeDtypeStruct((r, cols), F32)
    return pl.pallas_call(
        body, name=name, grid=(r // tm,), in_specs=[row] * 4, out_specs=[row] * 3, out_shape=[shp] * 3,
        compiler_params=_params(1),
    )(w, g, m, v)


SMALL_SLOTS = (("norm_g", (0,)), ("b_gate", (1024,)), ("lb_logits", (3072, 4096)), ("hg_norm_g", (5120,)),
               ("q_a_g", (5248,)), ("kv_a_g", (5632,)), ("final_norm_g", (5888,)))
LOSS_SLOT = 6912


def _small_update(gathered, ws, ms, vs):
    n = len(SMALL_SLOTS)

    def body(*refs):
        g_ref = refs[0]
        w_refs, m_refs, v_refs = refs[1:1 + n], refs[1 + n:1 + 2 * n], refs[1 + 2 * n:1 + 3 * n]
        outs = refs[1 + 3 * n:]
        loss_ref = outs[0]
        g_out, d_out, nm_out, nv_out = (outs[1 + k * n:1 + (k + 1) * n] for k in range(4))
        total = g_ref[0]
        for dev in range(1, N_DEV):
            total = total + g_ref[dev]
        loss_ref[...] = total[:, LOSS_SLOT:LOSS_SLOT + 128]
        for p, (_, offsets) in enumerate(SMALL_SLOTS):
            cols = w_refs[p].shape[1]
            for r, off in enumerate(offsets):
                rows = slice(r, r + 1)
                g = total[:, off:off + cols]
                g_out[p][rows, :] = g
                d_out[p][rows, :], nm_out[p][rows, :], nv_out[p][rows, :] = _adamw_math(
                    w_refs[p][rows, :], g, m_refs[p][rows, :], v_refs[p][rows, :])

    shapes = [jax.ShapeDtypeStruct(w.shape, F32) for w in ws]
    res = pl.pallas_call(
        body, name="small_update", out_shape=[jax.ShapeDtypeStruct((1, 128), F32)] + shapes * 4,
    )(gathered, *ws, *ms, *vs)
    return res[0], res[1:1 + n], res[1 + n:1 + 2 * n], res[1 + 2 * n:1 + 3 * n], res[1 + 3 * n:1 + 4 * n]


def kernel(x, norm_g, w_in, b_gate, lb_logits, hg_norm_g, q_a_g, w_uq, kv_a_g, w_ukv, w_proj_a, w_proj_b, w_out, final_norm_g, loss_target, m_norm_g, m_w_in, m_b_gate, m_lb_logits, m_hg_norm_g, m_q_a_g, m_w_uq, m_kv_a_g, m_w_ukv, m_w_proj_a, m_w_proj_b, m_w_out, m_final_norm_g, v_norm_g, v_w_in, v_b_gate, v_lb_logits, v_hg_norm_g, v_q_a_g, v_w_uq, v_kv_a_g, v_w_ukv, v_w_proj_a, v_w_proj_b, v_w_out, v_final_norm_g):
    c_idx = lax.axis_index("c").astype(jnp.int32).reshape(1)
    chip_idx = (2 * lax.axis_index("x") + lax.axis_index("y")).astype(jnp.int32).reshape(1)

    w_blk = w_in[0].T.astype(BF16)
    r_blk = _pack_rest(w_uq[0], w_ukv[0], w_proj_a[0], w_proj_b[0], w_out[0]).astype(BF16)
    gw = _gather_weights(w_blk)

    loss, dh_args, d_w_int, d_wuq3, d_wukv3, d_proj, small = _local_step(
        x[0], loss_target[0], gw.reshape(W_IN_COLS, D_MODEL), r_blk,
        norm_g, b_gate, lb_logits, hg_norm_g, q_a_g, kv_a_g, final_norm_g.reshape(1, D_MODEL))

    d_rest = _pack_rest_grads(d_wuq3, d_wukv3, d_proj)
    lw, lr = _swap_halves(d_w_int, d_rest)
    hw32, hw16 = _add_cores(c_idx, d_w_int, lw, tm=656, name="grad_add_cores_w")
    hr32, hr16 = _add_cores(c_idx, d_rest.reshape(N_CHIPS * REST_ROWS, D_MODEL), lr.reshape(N_CHIPS * REST_ROWS, HALF_COLS),
                            tm=REST_ROWS, name="grad_add_cores_r")
    grad_x, small["norm_g"], landed_w, landed_r = _dh_fused(*dh_args, hw16, hr16)
    rw = _add_chips(chip_idx, hw32, landed_w, tm=656, name="grad_add_chips_w")
    rr = _add_chips(chip_idx, hr32, landed_r, tm=208, name="grad_add_chips_r")
    tw, tr = _swap_reduced(rw, rr)
    g_w_in = _join_cols(rw, tw).T
    g_rest = _join_cols(rr, tr)
    g_uq, g_ukv, g_pa, g_pb, g_out = _unpack_rest(g_rest)

    small_all = _gather_small(_pack_small(small, lb_logits, loss))

    upd = {
        "w_in": _adamw(w_in[0], g_w_in, m_w_in[0], v_w_in[0], name="adamw_w_in", tm=128),
        "w_uq": _adamw(w_uq[0], g_uq, m_w_uq[0], v_w_uq[0], name="adamw_w_uq", tm=Q_LORA),
        "w_ukv": _adamw(w_ukv[0], g_ukv, m_w_ukv[0], v_w_ukv[0], name="adamw_w_ukv", tm=KV_LORA),
        "w_proj_a": _adamw(w_proj_a[0], g_pa, m_w_proj_a[0], v_w_proj_a[0], name="adamw_w_proj_a", tm=256),
        "w_proj_b": _adamw(w_proj_b[0], g_pb, m_w_proj_b[0], v_w_proj_b[0], name="adamw_w_proj_b", tm=256),
        "w_out": _adamw(w_out[0], g_out, m_w_out[0], v_w_out[0], name="adamw_w_out", tm=256),
    }
    loss_vec, *small_sets = _small_update(
        small_all,
        [norm_g, b_gate, lb_logits, hg_norm_g, q_a_g, kv_a_g, final_norm_g.reshape(1, D_MODEL)],
        [m_norm_g, m_b_gate, m_lb_logits, m_hg_norm_g, m_q_a_g, m_kv_a_g, m_final_norm_g.reshape(1, D_MODEL)],
        [v_norm_g, v_b_gate, v_lb_logits, v_hg_norm_g, v_q_a_g, v_kv_a_g, v_final_norm_g.reshape(1, D_MODEL)])

    def outputs(big, small_set):
        s_ng, s_bg, s_lb, s_hg, s_qg, s_kvg, s_fg = small_set
        return (s_ng, big["w_in"][None], s_bg, s_lb, s_hg, s_qg, big["w_uq"][None], s_kvg, big["w_ukv"][None],
                big["w_proj_a"][None], big["w_proj_b"][None], big["w_out"][None], s_fg.reshape(D_MODEL))

    grads = {"w_in": g_w_in, "w_uq": g_uq, "w_ukv": g_ukv, "w_proj_a": g_pa, "w_proj_b": g_pb, "w_out": g_out}
    return (loss_vec[0, 0], grad_x[None], *outputs(grads, small_sets[0]),
            *(o for k in range(3) for o in outputs({n: u[k] for n, u in upd.items()}, small_sets[1 + k])))
```

```python
import functools

import jax
import jax.numpy as jnp
from jax import lax
from jax.experimental import pallas as pl
from jax.experimental.pallas import tpu as pltpu

F32 = jnp.float32
BF16 = jnp.bfloat16

D_MODEL = 1024
HEADS = 8
HEAD_DIM = 128
HG_CHUNK = 32
CHUNK_SHIFT = 5
HEAD_SHIFT = 7
QK_NOPE = 128
QK_ROPE = 64
QK_DIM = QK_NOPE + QK_ROPE
QK_PAD = 256
Q_LORA = 384
KV_LORA = 256
MS_COLS = 768
ROPE_THETA = 10000.0
EPS = 1e-6
ATT_SCALE = QK_DIM ** -0.5
LOG2E = 1.4426950408889634
LN2 = 0.6931471805599453
Q_PRESCALE = ATT_SCALE * LOG2E

ADAM_LR = 0.001
ADAM_B1 = 0.9
ADAM_B2 = 0.999
ADAM_EPS = 1e-08
ADAM_WD = 0.01
ADAM_STEP = 10

N_CHIPS = 4
N_DEV = 8
W_IN_COLS = 7872
W_IN_BLK = W_IN_COLS // N_CHIPS
REST_ROWS = 144 + 128 + 3 * 256
SMALL_COLS = 7168

TM_MM = 1024
TM_FUSED = 256
HG_ROWS = 128
TQ = 512
FLASH_HEADS = 2
HG_HEADS = 8
VMEM_LIMIT = 56 * 1024 * 1024


def _dot(a, b):
    return lax.dot_general(a, b, (((1,), (0,)), ((), ())), preferred_element_type=F32)


def _dot_nt(a, b):
    return lax.dot_general(a, b, (((1,), (1,)), ((), ())), preferred_element_type=F32)


def _dot_tn(a, b):
    return lax.dot_general(a, b, (((0,), (0,)), ((), ())), preferred_element_type=F32)


def _params(n_axes):
    return pltpu.CompilerParams(dimension_semantics=("arbitrary",) * n_axes, vmem_limit_bytes=VMEM_LIMIT)


def _rms(x, g):
    r = lax.rsqrt(jnp.mean(x * x, axis=-1, keepdims=True) + EPS)
    return x * r * g


def _rms_bwd(x, g, dy):
    r = lax.rsqrt(jnp.mean(x * x, axis=-1, keepdims=True) + EPS)
    xh = x * r
    dyg = dy * g
    dx = r * (dyg - xh * jnp.mean(dyg * xh, axis=-1, keepdims=True))
    return dx, dy * xh


def _silu_parts(z):
    s = jax.nn.sigmoid(z)
    return z * s, s * (1.0 + z * (1.0 - s))


def _rope(x, c, sa, sb):
    return x * c + pltpu.roll(x, 32, 1) * sa + pltpu.roll(x, 96, 1) * sb


def _rope_bwd(dy, c, sa, sb):
    return dy * c + pltpu.roll(dy * sa, 96, 1) + pltpu.roll(dy * sb, 32, 1)


def _rope_tables(seq):
    inv = ROPE_THETA ** (-jnp.arange(0, QK_ROPE, 2, dtype=F32) / QK_ROPE)
    ang = jnp.arange(seq, dtype=F32)[:, None] * inv[None, :]
    cos, sin = jnp.cos(ang), jnp.sin(ang)
    z32 = jnp.zeros_like(cos)
    z64 = jnp.zeros((seq, 64), F32)
    c = jnp.concatenate([cos, cos, z64], axis=1)
    sa = jnp.concatenate([z32, sin, z64], axis=1)
    sb = jnp.concatenate([-sin, z32, z64], axis=1)
    return c, sa, sb


def _mm_tn(a, b, *, name, tm=TM_MM, tn=1024):
    flat = a.ndim == 2
    if flat:
        a = a[None]
    g, m, k = a.shape
    n = b.shape[1]
    tm, tn = min(tm, m), min(tn, n)
    assert m % tm == 0 and n % tn == 0

    def body(a_ref, b_ref, o_ref):
        @pl.when(pl.program_id(2) == 0)
        def _():
            o_ref[...] = jnp.zeros_like(o_ref)

        o_ref[...] += _dot_tn(a_ref[...], b_ref[...])

    out = pl.pallas_call(
        body, name=name, grid=(g, n // tn, m // tm),
        in_specs=[pl.BlockSpec((None, tm, k), lambda s, j, i: (s, i, 0)),
                  pl.BlockSpec((tm, tn), lambda s, j, i: (i, j))],
        out_specs=pl.BlockSpec((None, k, tn), lambda s, j, i: (s, 0, j)),
        out_shape=jax.ShapeDtypeStruct((g, k, n), F32), compiler_params=_params(3),
    )(a, b)
    return out[0] if flat else out


def _chunk_rows(rows):
    return lax.broadcasted_iota(jnp.int32, (rows, HEAD_DIM), 0) & (HG_CHUNK - 1)


def _chunk_cumsum(x, rows):
    pos = _chunk_rows(rows)
    shift = 1
    while shift < HG_CHUNK:
        x = x + jnp.where(pos >= shift, pltpu.roll(x, shift, 0), 0.0)
        shift *= 2
    return x


def _chunk_revcumsum(x, rows):
    pos = _chunk_rows(rows)
    shift = 1
    while shift < HG_CHUNK:
        x = x + jnp.where(pos + shift < HG_CHUNK, pltpu.roll(x, rows - shift, 0), 0.0)
        shift *= 2
    return x


def _lower_bound(lbl):
    mx = jnp.maximum(lbl[0:1, :], lbl[1:2, :])
    e0 = jnp.exp(lbl[0:1, :] - mx)
    e1 = jnp.exp(lbl[1:2, :] - mx)
    p0 = e0 / (e0 + e1)
    return p0, p0 * (e1 / (e0 + e1))


def _hg_masks(rows, nch, tmask_s, bdmask_s):
    r = lax.broadcasted_iota(jnp.int32, (rows, rows), 0)
    c = lax.broadcasted_iota(jnp.int32, (rows, rows), 1)
    tmask_s[...] = jnp.where(((r >> CHUNK_SHIFT) == (c >> CHUNK_SHIFT)) & (r >= c), 1.0, 0.0)
    r = lax.broadcasted_iota(jnp.int32, (rows, nch * HEAD_DIM), 0)
    c = lax.broadcasted_iota(jnp.int32, (rows, nch * HEAD_DIM), 1)
    bdmask_s[...] = jnp.where((r >> CHUNK_SHIFT) == (c >> HEAD_SHIFT), 1.0, 0.0).astype(BF16)


def _block_diag(x, nch, bdmask):
    return jnp.tile(x, (1, nch)) * bdmask


def _hgrn_fwd(hg, lb_logits, norm_g):
    s = hg.shape[1]
    rows = min(HG_ROWS, s)
    nblk = s // rows
    nch = rows // HG_CHUNK

    def body(hg_ref, lbl_ref, g_ref, o_ref, ya_ref, st0_ref, st_s, stall_s, tmask_s, bdmask_s):
        @pl.when(pl.program_id(1) == 0)
        def _():
            st_s[...] = jnp.zeros_like(st_s)
            _hg_masks(rows, nch, tmask_s, bdmask_s)

        bdmask = bdmask_s[...]
        tmask = tmask_s[...] > 0.5
        for hh in range(HG_HEADS):
            hc = slice(hh * HEAD_DIM, (hh + 1) * HEAD_DIM)
            hq = hg_ref[0, :, hc]
            hf = hg_ref[1, :, hc]
            hi = hg_ref[2, :, hc]
            hz = hg_ref[3, :, hc]
            lb, _ = _lower_bound(lbl_ref[:, hc])
            f = lb + (1.0 - lb) * jax.nn.sigmoid(hf)
            q = hq * jax.nn.sigmoid(hq)
            k = 1.0 - f
            logf = jnp.log(f)
            b = _chunk_cumsum(logf, rows)
            q_in = (q * jnp.exp(b)).astype(BF16)
            k_in = (k * jnp.exp(-b)).astype(BF16)
            k_out = (k * jnp.exp(_chunk_revcumsum(logf, rows) - logf)).astype(BF16)
            vb = hi.astype(BF16)

            sc = jnp.where(tmask, _dot_nt(q_in, k_in), 0.0)
            o_intra = _dot(sc.astype(BF16), vb)
            kvt = _dot_tn(vb, _block_diag(k_out, nch, bdmask))
            st = st_s[hh]
            st0_ref[hh] = st
            for c in range(nch):
                cols = slice(c * HEAD_DIM, (c + 1) * HEAD_DIM)
                last = (c + 1) * HG_CHUNK - 1
                stall_s[hh, :, cols] = st.astype(BF16)
                st = st * jnp.exp(b[last:last + 1, :]) + kvt[:, cols]
            st_s[hh] = st
            o = o_intra + _dot_nt(_block_diag(q_in, nch, bdmask), stall_s[hh])
            o_ref[:, hc] = o
            silu_z, _ = _silu_parts(hz)
            ya_ref[:, hc] = (_rms(o, g_ref[...]) * silu_z).astype(BF16)

    nh = HG_HEADS
    return pl.pallas_call(
        body, name="hgrn_fwd", grid=(HEADS // nh, nblk),
        in_specs=[pl.BlockSpec((4, rows, nh * HEAD_DIM), lambda h, i: (0, i, h)),
                  pl.BlockSpec((2, nh * HEAD_DIM), lambda h, i: (0, h)),
                  pl.BlockSpec((1, HEAD_DIM), lambda h, i: (0, 0))],
        out_specs=[pl.BlockSpec((rows, nh * HEAD_DIM), lambda h, i: (i, h)),
                   pl.BlockSpec((rows, nh * HEAD_DIM), lambda h, i: (i, h)),
                   pl.BlockSpec((nh, None, HEAD_DIM, HEAD_DIM), lambda h, i: (h, i, 0, 0))],
        out_shape=[jax.ShapeDtypeStruct((s, D_MODEL), F32), jax.ShapeDtypeStruct((s, D_MODEL), BF16),
                   jax.ShapeDtypeStruct((HEADS, nblk, HEAD_DIM, HEAD_DIM), F32)],
        scratch_shapes=[pltpu.VMEM((nh, HEAD_DIM, HEAD_DIM), F32), pltpu.VMEM((nh, HEAD_DIM, nch * HEAD_DIM), BF16),
                        pltpu.VMEM((rows, rows), F32), pltpu.VMEM((rows, nch * HEAD_DIM), BF16)],
        compiler_params=_params(2),
    )(hg, lb_logits, norm_g)


def _hgrn_bwd(hg, o_pre, dya, st0, lb_logits, norm_g):
    s = hg.shape[1]
    rows = min(HG_ROWS, s)
    nblk = s // rows
    nch = rows // HG_CHUNK

    def body(hg_ref, o_ref, dya_ref, st0_ref, lbl_ref, g_ref, dhg_ref, dlb_ref, dg_ref,
             dst_s, stp_s, stp_rows_s, dst_rows_s, dst_lane_s, dbl_s, tmask_s, bdmask_s):
        @pl.when(pl.program_id(1) == 0)
        def _():
            dst_s[...] = jnp.zeros_like(dst_s)
            dlb_ref[...] = jnp.zeros_like(dlb_ref)
            dg_ref[...] = jnp.zeros_like(dg_ref)
            _hg_masks(rows, nch, tmask_s, bdmask_s)

        bdmask = bdmask_s[...]
        tmask = tmask_s[...] > 0.5
        g = g_ref[...]
        for hh in range(HG_HEADS):
            hc = slice(hh * HEAD_DIM, (hh + 1) * HEAD_DIM)
            hq = hg_ref[0, :, hc]
            hf = hg_ref[1, :, hc]
            hi = hg_ref[2, :, hc]
            hz = hg_ref[3, :, hc]
            lb, _ = _lower_bound(lbl_ref[:, hc])
            sg = jax.nn.sigmoid(hf)
            f = lb + (1.0 - lb) * sg
            q, dsilu_q = _silu_parts(hq)
            k = 1.0 - f
            logf = jnp.log(f)
            b = _chunk_cumsum(logf, rows)
            eb = jnp.exp(b)
            enb = jnp.exp(-b)
            ebl = jnp.exp(_chunk_revcumsum(logf, rows) - logf)
            q_in32 = q * eb
            k_in32 = k * enb
            k_out32 = k * ebl
            q_in = q_in32.astype(BF16)
            k_in = k_in32.astype(BF16)
            k_out = k_out32.astype(BF16)
            vb = hi.astype(BF16)
            kbd = _block_diag(k_out, nch, bdmask)
            qbd = _block_diag(q_in, nch, bdmask)
            decs = [jnp.exp(b[(c + 1) * HG_CHUNK - 1:(c + 1) * HG_CHUNK, :]) for c in range(nch)]

            kvt = _dot_tn(vb, kbd)
            st = st0_ref[hh]
            for c in range(nch):
                stp_s[hh, c] = st
                stp_rows_s[hh, c * HEAD_DIM:(c + 1) * HEAD_DIM, :] = st.astype(BF16)
                st = st * decs[c] + kvt[:, c * HEAD_DIM:(c + 1) * HEAD_DIM]

            o = o_ref[:, hc]
            rstd = lax.rsqrt(jnp.mean(o * o, axis=-1, keepdims=True) + EPS)
            oh = o * rstd
            silu_z, dsilu_z = _silu_parts(hz)
            dya_v = dya_ref[:, hc]
            dn = dya_v * silu_z
            dhz = dya_v * (oh * g) * dsilu_z
            dg_ref[hh] += jnp.sum(dn * oh, axis=0, keepdims=True)
            doh = dn * g
            do = (rstd * (doh - oh * jnp.mean(doh * oh, axis=-1, keepdims=True))).astype(BF16)

            dq_all = _dot_tn(do, qbd)
            dst = dst_s[hh]
            ddecs = [None] * nch
            for c in reversed(range(nch)):
                dstb = dst.astype(BF16)
                dst_lane_s[hh, :, c * HEAD_DIM:(c + 1) * HEAD_DIM] = dstb
                dst_rows_s[hh, c * HEAD_DIM:(c + 1) * HEAD_DIM, :] = dstb
                ddecs[c] = jnp.sum(dst * stp_s[hh, c], axis=0, keepdims=True) * decs[c]
                dst = dst * decs[c] + dq_all[:, c * HEAD_DIM:(c + 1) * HEAD_DIM]
            dst_s[hh] = dst

            sc = jnp.where(tmask, _dot_nt(q_in, k_in), 0.0).astype(BF16)
            dkout = _dot(_block_diag(vb, nch, bdmask), dst_rows_s[hh])
            dv = _dot_nt(kbd, dst_lane_s[hh]) + _dot_tn(sc, do)
            dsc = jnp.where(tmask, _dot_nt(do, vb), 0.0).astype(BF16)
            dqin = _dot(dsc, k_in) + _dot(_block_diag(do, nch, bdmask), stp_rows_s[hh])
            dkin = _dot_tn(dsc, q_in)

            dko = dkout * k_out32
            for c in range(nch):
                sl = slice(c * HG_CHUNK, (c + 1) * HG_CHUNK)
                dbl = jnp.sum(dko[sl], axis=0, keepdims=True) + ddecs[c]
                dbl_s[hh, sl, :] = jnp.broadcast_to(dbl, (HG_CHUNK, HEAD_DIM))
            dq = dqin * eb
            dk = dkin * enb + dkout * ebl
            db = dqin * q_in32 - dkin * k_in32 - dko
            dlogf = _chunk_revcumsum(db, rows) + dbl_s[hh]
            df = dlogf / f - dk
            dlb_ref[:, hc] += jnp.sum(df * (1.0 - sg), axis=0, keepdims=True)
            dhg_ref[0, :, hc] = (dq * dsilu_q).astype(BF16)
            dhg_ref[1, :, hc] = (df * (1.0 - lb) * sg * (1.0 - sg)).astype(BF16)
            dhg_ref[2, :, hc] = dv.astype(BF16)
            dhg_ref[3, :, hc] = dhz.astype(BF16)

    last = nblk - 1
    nh = HG_HEADS
    wide = nh * HEAD_DIM
    return pl.pallas_call(
        body, name="hgrn_bwd", grid=(HEADS // nh, nblk),
        in_specs=[pl.BlockSpec((4, rows, wide), lambda h, i: (0, last - i, h)),
                  pl.BlockSpec((rows, wide), lambda h, i: (last - i, h)),
                  pl.BlockSpec((rows, wide), lambda h, i: (last - i, h)),
                  pl.BlockSpec((nh, None, HEAD_DIM, HEAD_DIM), lambda h, i: (h, last - i, 0, 0)),
                  pl.BlockSpec((2, wide), lambda h, i: (0, h)),
                  pl.BlockSpec((1, HEAD_DIM), lambda h, i: (0, 0))],
        out_specs=[pl.BlockSpec((4, rows, wide), lambda h, i: (0, last - i, h)),
                   pl.BlockSpec((1, wide), lambda h, i: (0, h)),
                   pl.BlockSpec((nh, 1, HEAD_DIM), lambda h, i: (h, 0, 0))],
        out_shape=[jax.ShapeDtypeStruct((4, s, D_MODEL), BF16), jax.ShapeDtypeStruct((1, D_MODEL), F32),
                   jax.ShapeDtypeStruct((HEADS, 1, HEAD_DIM), F32)],
        scratch_shapes=[pltpu.VMEM((nh, HEAD_DIM, HEAD_DIM), F32), pltpu.VMEM((nh, nch, HEAD_DIM, HEAD_DIM), F32),
                        pltpu.VMEM((nh, nch * HEAD_DIM, HEAD_DIM), BF16), pltpu.VMEM((nh, nch * HEAD_DIM, HEAD_DIM), BF16),
                        pltpu.VMEM((nh, HEAD_DIM, nch * HEAD_DIM), BF16), pltpu.VMEM((nh, rows, HEAD_DIM), F32),
                        pltpu.VMEM((rows, rows), F32), pltpu.VMEM((rows, nch * HEAD_DIM), BF16)],
        compiler_params=_params(2),
    )(hg, o_pre, dya, st0, lb_logits, norm_g)


def _mla_pre(ms, q_a_g, kv_a_g, wuq3, wukv3, tabs):
    s = ms.shape[0]
    tm = min(TM_FUSED, s)

    def body(ms_ref, qg_ref, kvg_ref, wuq_ref, wukv_ref, c_ref, sa_ref, sb_ref,
             q_ref, k_ref, v_ref, cqn_ref, ckvn_ref):
        c, sa, sb = c_ref[...], sa_ref[...], sb_ref[...]
        cqn = _rms(ms_ref[:, 0:Q_LORA], qg_ref[...]).astype(BF16)
        ckvn = _rms(ms_ref[:, Q_LORA:Q_LORA + KV_LORA], kvg_ref[...]).astype(BF16)
        cqn_ref[...] = cqn
        ckvn_ref[...] = ckvn
        k_pe = _rope(ms_ref[:, Q_LORA + KV_LORA:MS_COLS], c, sa, sb).astype(BF16)
        for h in range(HEADS):
            qh = _dot(cqn, wuq_ref[h])
            q_ref[h, :, 0:128] = (qh[:, 0:128] * Q_PRESCALE).astype(BF16)
            q_ref[h, :, 128:256] = (_rope(qh[:, 128:256], c, sa, sb) * Q_PRESCALE).astype(BF16)
            kvh = _dot(ckvn, wukv_ref[h])
            k_ref[h, :, 0:128] = kvh[:, 0:128].astype(BF16)
            k_ref[h, :, 128:256] = k_pe
            v_ref[h] = kvh[:, 128:256].astype(BF16)

    tab = pl.BlockSpec((tm, 128), lambda i: (i, 0))
    return pl.pallas_call(
        body, name="mla_pre", grid=(s // tm,),
        in_specs=[pl.BlockSpec((tm, MS_COLS), lambda i: (i, 0)),
                  pl.BlockSpec((1, Q_LORA), lambda i: (0, 0)), pl.BlockSpec((1, KV_LORA), lambda i: (0, 0)),
                  pl.BlockSpec((HEADS, Q_LORA, QK_PAD), lambda i: (0, 0, 0)),
                  pl.BlockSpec((HEADS, KV_LORA, 256), lambda i: (0, 0, 0)), tab, tab, tab],
        out_specs=[pl.BlockSpec((HEADS, tm, QK_PAD), lambda i: (0, i, 0)),
                   pl.BlockSpec((HEADS, tm, QK_PAD), lambda i: (0, i, 0)),
                   pl.BlockSpec((HEADS, tm, HEAD_DIM), lambda i: (0, i, 0)),
                   pl.BlockSpec((tm, Q_LORA), lambda i: (i, 0)), pl.BlockSpec((tm, KV_LORA), lambda i: (i, 0))],
        out_shape=[jax.ShapeDtypeStruct((HEADS, s, QK_PAD), BF16), jax.ShapeDtypeStruct((HEADS, s, QK_PAD), BF16),
                   jax.ShapeDtypeStruct((HEADS, s, HEAD_DIM), BF16),
                   jax.ShapeDtypeStruct((s, Q_LORA), BF16), jax.ShapeDtypeStruct((s, KV_LORA), BF16)],
        compiler_params=_params(1),
    )(ms, q_a_g, kv_a_g, wuq3, wukv3, *tabs)


def _causal_mask(t):
    r = lax.broadcasted_iota(jnp.int32, (t, t), 0)
    c = lax.broadcasted_iota(jnp.int32, (t, t), 1)
    return r >= c


def _flash_fwd(q, k, v, mz):
    s = q.shape[1]
    t = min(TQ, s)

    def body(q_ref, k_ref, v_ref, mz_ref, o_ref, yb_ref, lse_ref, m_s, l_s, acc_s):
        i = pl.program_id(1)
        m_s[...] = jnp.full_like(m_s, -jnp.inf)
        l_s[...] = jnp.zeros_like(l_s)
        acc_s[...] = jnp.zeros_like(acc_s)

        def step(j, groups):
            rows = pl.ds(pl.multiple_of(j * t, t), t)
            for hh in range(FLASH_HEADS):
                for r0, nr, masked in groups:
                    r = slice(r0, r0 + nr)
                    sc = _dot_nt(q_ref[hh, r, :], k_ref[hh, rows, :])
                    if masked:
                        sc = jnp.where(_causal_mask(t), sc, -jnp.inf)
                    m_prev = m_s[hh, r, :]
                    m_new = jnp.maximum(m_prev, jnp.max(sc, axis=-1, keepdims=True))
                    p = jnp.exp2(sc - jnp.tile(m_new, (1, t // 128)))
                    alpha = jnp.exp2(m_prev - m_new)
                    l_s[hh, r, :] = alpha * l_s[hh, r, :] + jnp.sum(p, axis=-1, keepdims=True)
                    acc_s[hh, r, :] = alpha * acc_s[hh, r, :] + _dot(p.astype(BF16), v_ref[hh, rows, :])
                    m_s[hh, r, :] = m_new

        def loop_body(jj, carry):
            step(2 * jj, ((0, 2 * t, False),))
            step(2 * jj + 1, ((0, 2 * t, False),))
            return carry

        lax.fori_loop(0, i, loop_body, 0)
        step(2 * i, ((0, t, True), (t, t, False)))
        step(2 * i + 1, ((t, t, True),))
        for hh in range(FLASH_HEADS):
            cols = slice(hh * HEAD_DIM, (hh + 1) * HEAD_DIM)
            out = acc_s[hh] / l_s[hh]
            o_ref[:, cols] = out
            silu_z, _ = _silu_parts(mz_ref[:, cols])
            yb_ref[:, cols] = (out * silu_z).astype(BF16)
            lse_ref[hh] = m_s[hh] + jnp.log2(l_s[hh])

    nh = FLASH_HEADS
    t2 = 2 * t
    col = pl.BlockSpec((t2, nh * HEAD_DIM), lambda h, i: (i, h))
    return pl.pallas_call(
        body, name="flash_fwd", grid=(HEADS // nh, s // t2),
        in_specs=[pl.BlockSpec((nh, t2, QK_PAD), lambda h, i: (h, i, 0)),
                  pl.BlockSpec((nh, s, QK_PAD), lambda h, i: (h, 0, 0)),
                  pl.BlockSpec((nh, s, HEAD_DIM), lambda h, i: (h, 0, 0)), col],
        out_specs=[col, col, pl.BlockSpec((nh, t2, 128), lambda h, i: (h, i, 0))],
        out_shape=[jax.ShapeDtypeStruct((s, D_MODEL), F32), jax.ShapeDtypeStruct((s, D_MODEL), BF16),
                   jax.ShapeDtypeStruct((HEADS, s, 128), F32)],
        scratch_shapes=[pltpu.VMEM((nh, t2, 128), F32), pltpu.VMEM((nh, t2, 128), F32),
                        pltpu.VMEM((nh, t2, HEAD_DIM), F32)],
        compiler_params=_params(2),
    )(q, k, v, mz)


def _flash_bwd(q, k, v, dyb, mz, o_att, lse, tabs):
    s = q.shape[1]
    t = min(TQ, s)

    def body(q_ref, k_ref, v_ref, dyb_ref, mz_ref, o_ref, lse_ref, c_ref, sa_ref, sb_ref,
             dq_ref, dk_ref, dv_ref, dmz_ref, dq_s, delta_s, do_s):
        i = pl.program_id(1)

        @pl.when(i == 0)
        def _():
            dk_ref[...] = jnp.zeros_like(dk_ref)
            dv_ref[...] = jnp.zeros_like(dv_ref)

        silu_z, dsilu_z = _silu_parts(mz_ref[...])
        dyb_v = dyb_ref[...]
        out = o_ref[...]
        do32 = dyb_v * silu_z
        dmz_ref[...] = (dyb_v * out * dsilu_z).astype(BF16)
        delta_s[...] = jnp.broadcast_to(jnp.sum(do32 * out, axis=-1, keepdims=True), (2 * t, 128))
        do_s[...] = do32.astype(BF16)
        dq_s[...] = jnp.zeros_like(dq_s)

        def step(j, modes):
            rows = pl.ds(pl.multiple_of(j * t, t), t)
            kj = k_ref[rows, :]
            vj = v_ref[rows, :]
            dv_acc = None
            dk_acc = None
            for ch, masked in enumerate(modes):
                if masked is None:
                    continue
                r = slice(ch * t, (ch + 1) * t)
                qv = q_ref[r, :]
                do = do_s[r, :]
                sc = _dot_nt(qv, kj)
                if masked:
                    sc = jnp.where(_causal_mask(t), sc, -jnp.inf)
                p = jnp.exp2(sc - jnp.tile(lse_ref[r, :], (1, t // 128)))
                dp = _dot_nt(do, vj)
                ds = (p * (dp - jnp.tile(delta_s[r, :], (1, t // 128)))).astype(BF16)
                dv_c = _dot_tn(p.astype(BF16), do)
                dk_c = _dot_tn(ds, qv)
                dv_acc = dv_c if dv_acc is None else dv_acc + dv_c
                dk_acc = dk_c if dk_acc is None else dk_acc + dk_c
                dq_s[r, :] += _dot(ds, kj)
            dv_ref[rows, :] += dv_acc
            dk_ref[rows, :] += dk_acc

        def loop_body(jj, carry):
            step(2 * jj, (False, False))
            step(2 * jj + 1, (False, False))
            return carry

        lax.fori_loop(0, i, loop_body, 0)
        step(2 * i, (True, False))
        step(2 * i + 1, (None, True))
        dq = dq_s[...] * ATT_SCALE
        dq_ref[:, 0:128] = dq[:, 0:128].astype(BF16)
        dq_ref[:, 128:256] = _rope_bwd(dq[:, 128:256], c_ref[...], sa_ref[...], sb_ref[...]).astype(BF16)

    t2 = 2 * t
    col = pl.BlockSpec((t2, HEAD_DIM), lambda h, i: (i, h))
    tab = pl.BlockSpec((t2, 128), lambda h, i: (i, 0))
    return pl.pallas_call(
        body, name="flash_bwd", grid=(HEADS, s // t2),
        in_specs=[pl.BlockSpec((None, t2, QK_PAD), lambda h, i: (h, i, 0)),
                  pl.BlockSpec((None, s, QK_PAD), lambda h, i: (h, 0, 0)),
                  pl.BlockSpec((None, s, HEAD_DIM), lambda h, i: (h, 0, 0)),
                  col, col, col, pl.BlockSpec((None, t2, 128), lambda h, i: (h, i, 0)), tab, tab, tab],
        out_specs=[pl.BlockSpec((None, t2, QK_PAD), lambda h, i: (h, i, 0)),
                   pl.BlockSpec((None, s, QK_PAD), lambda h, i: (h, 0, 0)),
                   pl.BlockSpec((None, s, HEAD_DIM), lambda h, i: (h, 0, 0)), col],
        out_shape=[jax.ShapeDtypeStruct((HEADS, s, QK_PAD), BF16), jax.ShapeDtypeStruct((HEADS, s, QK_PAD), F32),
                   jax.ShapeDtypeStruct((HEADS, s, HEAD_DIM), F32), jax.ShapeDtypeStruct((s, D_MODEL), BF16)],
        scratch_shapes=[pltpu.VMEM((t2, QK_PAD), F32), pltpu.VMEM((t2, 128), F32), pltpu.VMEM((t2, HEAD_DIM), BF16)],
        compiler_params=_params(2),
    )(q, k, v, dyb, mz, o_att, lse, *tabs)


def _mla_bwd_proj(dq, dk, dv, cqn, ckvn, ms, q_a_g, kv_a_g, wuq3, wukv3, tabs):
    s = ms.shape[0]
    tm = min(TM_FUSED, s)

    def body(dq_ref, dk_ref, dv_ref, cqn_ref, ckvn_ref, ms_ref, qg_ref, kvg_ref, wuq_ref, wukv_ref,
             c_ref, sa_ref, sb_ref, dms_ref, dwuq_ref, dwukv_ref, dqg_ref, dkvg_ref):
        @pl.when(pl.program_id(0) == 0)
        def _():
            dwuq_ref[...] = jnp.zeros_like(dwuq_ref)
            dwukv_ref[...] = jnp.zeros_like(dwukv_ref)
            dqg_ref[...] = jnp.zeros_like(dqg_ref)
            dkvg_ref[...] = jnp.zeros_like(dkvg_ref)

        cqn = cqn_ref[...]
        ckvn = ckvn_ref[...]
        dcqn = jnp.zeros((tm, Q_LORA), F32)
        dckvn = jnp.zeros((tm, KV_LORA), F32)
        dkpe = jnp.zeros((tm, 128), F32)
        for h in range(HEADS):
            dqh = dq_ref[h]
            dcqn += _dot_nt(dqh, wuq_ref[h])
            dwuq_ref[h] += _dot_tn(cqn, dqh)
            dkh = dk_ref[h] * LN2
            dkvh = jnp.concatenate([dkh[:, 0:128], dv_ref[h]], axis=1).astype(BF16)
            dckvn += _dot_nt(dkvh, wukv_ref[h])
            dwukv_ref[h] += _dot_tn(ckvn, dkvh)
            dkpe += dkh[:, 128:256]
        dcq, dqg_rows = _rms_bwd(ms_ref[:, 0:Q_LORA], qg_ref[...], dcqn)
        dckv, dkvg_rows = _rms_bwd(ms_ref[:, Q_LORA:Q_LORA + KV_LORA], kvg_ref[...], dckvn)
        dqg_ref[...] += jnp.sum(dqg_rows, axis=0, keepdims=True)
        dkvg_ref[...] += jnp.sum(dkvg_rows, axis=0, keepdims=True)
        dms_ref[:, 0:Q_LORA] = dcq.astype(BF16)
        dms_ref[:, Q_LORA:Q_LORA + KV_LORA] = dckv.astype(BF16)
        dms_ref[:, Q_LORA + KV_LORA:MS_COLS] = _rope_bwd(dkpe, c_ref[...], sa_ref[...], sb_ref[...]).astype(BF16)

    tab = pl.BlockSpec((tm, 128), lambda i: (i, 0))
    wq = pl.BlockSpec((HEADS, Q_LORA, QK_PAD), lambda i: (0, 0, 0))
    wkv = pl.BlockSpec((HEADS, KV_LORA, 256), lambda i: (0, 0, 0))
    qg = pl.BlockSpec((1, Q_LORA), lambda i: (0, 0))
    kvg = pl.BlockSpec((1, KV_LORA), lambda i: (0, 0))
    return pl.pallas_call(
        body, name="mla_bwd_proj", grid=(s // tm,),
        in_specs=[pl.BlockSpec((HEADS, tm, QK_PAD), lambda i: (0, i, 0)),
                  pl.BlockSpec((HEADS, tm, QK_PAD), lambda i: (0, i, 0)),
                  pl.BlockSpec((HEADS, tm, HEAD_DIM), lambda i: (0, i, 0)),
                  pl.BlockSpec((tm, Q_LORA), lambda i: (i, 0)), pl.BlockSpec((tm, KV_LORA), lambda i: (i, 0)),
                  pl.BlockSpec((tm, MS_COLS), lambda i: (i, 0)), qg, kvg, wq, wkv, tab, tab, tab],
        out_specs=[pl.BlockSpec((tm, MS_COLS), lambda i: (i, 0)), wq, wkv, qg, kvg],
        out_shape=[jax.ShapeDtypeStruct((s, MS_COLS), BF16), jax.ShapeDtypeStruct((HEADS, Q_LORA, QK_PAD), F32),
                   jax.ShapeDtypeStruct((HEADS, KV_LORA, 256), F32),
                   jax.ShapeDtypeStruct((1, Q_LORA), F32), jax.ShapeDtypeStruct((1, KV_LORA), F32)],
        compiler_params=_params(1),
    )(dq, dk, dv, cqn, ckvn, ms, q_a_g, kv_a_g, wuq3, wukv3, *tabs)


def _merge_fused(ya, yb, glog, b_gate, x, tgt, fg, wproj):
    s = x.shape[0]
    tm = min(TM_FUSED, s)

    def body(ya_ref, yb_ref, g0_ref, g1_ref, b0_ref, b1_ref, x_ref, t_ref, fg_ref, w_ref,
             mg_ref, dx2_ref, dx2b_ref, dya_ref, dyb_ref, dgl_ref, dpa_ref, dpb_ref, loss_ref, dfg_ref, dbg_ref):
        @pl.when(pl.program_id(0) == 0)
        def _():
            loss_ref[...] = jnp.zeros_like(loss_ref)
            dfg_ref[...] = jnp.zeros_like(dfg_ref)
            dbg_ref[...] = jnp.zeros_like(dbg_ref)

        pa = _dot(ya_ref[...], w_ref[0])
        pb = _dot(yb_ref[...], w_ref[1])
        g0 = jax.nn.sigmoid(g0_ref[...] + b0_ref[...])
        g1 = jax.nn.sigmoid(g1_ref[...] + b1_ref[...])
        merged = (g0 * pa + g1 * pb).astype(BF16)
        mg_ref[...] = merged
        x2 = x_ref[...] + _dot(merged, w_ref[2])
        fg_v = fg_ref[...]
        err = _rms(x2, fg_v) - t_ref[...]
        loss_ref[...] += 0.5 * jnp.sum(jnp.mean(err * err, axis=-1, keepdims=True), axis=0, keepdims=True)
        dx2, dfg_rows = _rms_bwd(x2, fg_v, err * (1.0 / D_MODEL))
        dx2_ref[...] = dx2
        dfg_ref[...] += jnp.sum(dfg_rows, axis=0, keepdims=True)

        dx2b = dx2.astype(BF16)
        dx2b_ref[...] = dx2b
        dmg = _dot_nt(dx2b, w_ref[2])
        dpa = (dmg * g0).astype(BF16)
        dpb = (dmg * g1).astype(BF16)
        dpa_ref[...] = dpa
        dpb_ref[...] = dpb
        dgl0 = dmg * pa * g0 * (1.0 - g0)
        dgl1 = dmg * pb * g1 * (1.0 - g1)
        dgl_ref[:, 0:D_MODEL] = dgl0.astype(BF16)
        dgl_ref[:, D_MODEL:2 * D_MODEL] = dgl1.astype(BF16)
        dbg_ref[:, 0:D_MODEL] += jnp.sum(dgl0, axis=0, keepdims=True)
        dbg_ref[:, D_MODEL:2 * D_MODEL] += jnp.sum(dgl1, axis=0, keepdims=True)
        dya_ref[...] = _dot_nt(dpa, w_ref[0])
        dyb_ref[...] = _dot_nt(dpb, w_ref[1])

    row = pl.BlockSpec((tm, D_MODEL), lambda i: (i, 0))
    row1 = pl.BlockSpec((tm, D_MODEL), lambda i: (i, 1))
    row2 = pl.BlockSpec((tm, 2 * D_MODEL), lambda i: (i, 0))
    vec = pl.BlockSpec((1, D_MODEL), lambda i: (0, 0))
    vec1 = pl.BlockSpec((1, D_MODEL), lambda i: (0, 1))
    vec2 = pl.BlockSpec((1, 2 * D_MODEL), lambda i: (0, 0))
    f32_rows = jax.ShapeDtypeStruct((s, D_MODEL), F32)
    bf16_rows = jax.ShapeDtypeStruct((s, D_MODEL), BF16)
    return pl.pallas_call(
        body, name="merge_fused", grid=(s // tm,),
        in_specs=[row, row, row, row1, vec, vec1, row, row, vec, pl.BlockSpec((3, D_MODEL, D_MODEL), lambda i: (0, 0, 0))],
        out_specs=[row, row, row, row, row, row2, row, row, pl.BlockSpec((1, 128), lambda i: (0, 0)), vec, vec2],
        out_shape=[bf16_rows, f32_rows, bf16_rows, f32_rows, f32_rows, jax.ShapeDtypeStruct((s, 2 * D_MODEL), BF16),
                   bf16_rows, bf16_rows, jax.ShapeDtypeStruct((1, 128), F32), jax.ShapeDtypeStruct((1, D_MODEL), F32),
                   jax.ShapeDtypeStruct((1, 2 * D_MODEL), F32)],
        compiler_params=_params(1),
    )(ya, yb, glog, glog, b_gate, b_gate, x, tgt, fg, wproj)


def _proj_fused(x, g, w_int, r_blk):
    s = x.shape[0]
    tm = min(TM_FUSED, s)

    def body(x_ref, g_ref, w_hbm, r_ref, h_ref, hg_ref, ms_ref, mz_ref, gl_ref, or_ref, w_s, sem, send_sems, recv_sems):
        mx, my, mc = _me()
        chips = _other_chips(mx, my)
        mine, theirs = _cols(mc), _cols(1 - mc)

        def copy(k, src, dst, to):
            return pltpu.make_async_remote_copy(src_ref=src, dst_ref=dst, send_sem=send_sems.at[k],
                                                recv_sem=recv_sems.at[k], device_id=to, device_id_type=MESH_ID)

        def sends():
            return [copy(j, r_ref.at[:, mine], or_ref.at[2 * mx + my, :, mine], (cx, cy, mc))
                    for j, (cx, cy) in enumerate(chips)]

        @pl.when(pl.program_id(0) == 0)
        def _():
            for cp in sends():
                cp.start()
            cp = pltpu.make_async_copy(w_hbm, w_s, sem)
            cp.start()
            cp.wait()

        h = _rms(x_ref[...], g_ref[...]).astype(BF16)
        h_ref[...] = h
        for j in range(4):
            hg_ref[j] = _dot_nt(h, w_s[j * D_MODEL:(j + 1) * D_MODEL, :])
        ms = _dot_nt(h, w_s[4096:4096 + MS_COLS, :])
        lane = lax.broadcasted_iota(jnp.int32, ms.shape, 1)
        ms_ref[...] = jnp.where(lane < 704, ms, 0.0)
        mz_ref[...] = _dot_nt(h, w_s[4800:5824, :])
        for j in range(2):
            gl_ref[:, j * D_MODEL:(j + 1) * D_MODEL] = _dot_nt(h, w_s[5824 + j * D_MODEL:5824 + (j + 1) * D_MODEL, :])

        @pl.when(pl.program_id(0) == s // tm - 1)
        def _():
            passed = []
            for j, (cx, cy) in enumerate(chips):
                landed = or_ref.at[2 * cx + cy, :, mine]
                copy(j, landed, landed, (cx, cy, mc)).wait_recv()
                fwd = copy(3 + j, landed, landed, (mx, my, 1 - mc))
                fwd.start()
                passed.append(fwd)
            for j, (cx, cy) in enumerate(chips):
                other = or_ref.at[2 * cx + cy, :, theirs]
                copy(3 + j, other, other, (mx, my, 1 - mc)).wait_recv()
            for cp in sends() + passed:
                cp.wait_send()

    row = pl.BlockSpec((tm, D_MODEL), lambda i: (i, 0))
    outs = pl.pallas_call(
        body, name="proj_fused", grid=(s // tm,),
        in_specs=[row, pl.BlockSpec((1, D_MODEL), lambda i: (0, 0)), ANY, ANY],
        out_specs=[row, pl.BlockSpec((4, tm, D_MODEL), lambda i: (0, i, 0)), pl.BlockSpec((tm, MS_COLS), lambda i: (i, 0)),
                   row, pl.BlockSpec((tm, 2 * D_MODEL), lambda i: (i, 0)), ANY],
        out_shape=[jax.ShapeDtypeStruct((s, D_MODEL), BF16), jax.ShapeDtypeStruct((4, s, D_MODEL), F32),
                   jax.ShapeDtypeStruct((s, MS_COLS), F32), jax.ShapeDtypeStruct((s, D_MODEL), F32),
                   jax.ShapeDtypeStruct((s, 2 * D_MODEL), F32), jax.ShapeDtypeStruct((N_CHIPS,) + r_blk.shape, r_blk.dtype)],
        scratch_shapes=[pltpu.VMEM(w_int.shape, BF16), pltpu.SemaphoreType.DMA,
                        pltpu.SemaphoreType.DMA((6,)), pltpu.SemaphoreType.DMA((6,))],
        compiler_params=_params(1),
    )(x, g, w_int, r_blk)
    gr = lax.dynamic_update_slice(outs[5], r_blk[None], (2 * lax.axis_index("x") + lax.axis_index("y"), 0, 0))
    return (*outs[:5], gr)


def _dh_fused(dhg, dms, dmz, dglog, w_int, x, g, dx2, hw, hr):
    s = x.shape[0]
    tm = min(512, s)
    nw, nr = hw.shape[0] // N_CHIPS, hr.shape[0] // N_CHIPS

    def body(dhg_ref, dms_ref, dmz_ref, dgl_ref, w_hbm, x_ref, g_ref, dx2_ref, hw_ref, hr_ref,
             dx_ref, dg_ref, lw_ref, lr_ref, w_s, sem, send_sems, recv_sems):
        def scatter_copies():
            mx, my, mc = _me()
            return [pltpu.make_async_remote_copy(
                src_ref=src.at[pl.ds((2 * cx + cy) * n, n), :], dst_ref=dst.at[j], send_sem=send_sems.at[3 * a + j],
                recv_sem=recv_sems.at[3 * a + j], device_id=(cx, cy, mc), device_id_type=MESH_ID)
                for a, (src, dst, n) in enumerate([(hw_ref, lw_ref, nw), (hr_ref, lr_ref, nr)])
                for j, (cx, cy) in enumerate(_other_chips(mx, my))]

        @pl.when(pl.program_id(0) == 0)
        def _():
            for cp in scatter_copies():
                cp.start()
            dg_ref[...] = jnp.zeros_like(dg_ref)
            cp = pltpu.make_async_copy(w_hbm, w_s, sem)
            cp.start()
            cp.wait()

        dh = _dot(dms_ref[...], w_s[4096:4096 + MS_COLS, :]) + _dot(dmz_ref[...], w_s[4800:5824, :])
        for j in range(4):
            dh += _dot(dhg_ref[j], w_s[j * D_MODEL:(j + 1) * D_MODEL, :])
        for j in range(2):
            dh += _dot(dgl_ref[:, j * D_MODEL:(j + 1) * D_MODEL], w_s[5824 + j * D_MODEL:5824 + (j + 1) * D_MODEL, :])
        dx, dg_rows = _rms_bwd(x_ref[...], g_ref[...], dh)
        dx_ref[...] = dx + dx2_ref[...]
        dg_ref[...] += jnp.sum(dg_rows, axis=0, keepdims=True)

        @pl.when(pl.program_id(0) == s // tm - 1)
        def _():
            for cp in scatter_copies():
                cp.wait()

    row = pl.BlockSpec((tm, D_MODEL), lambda i: (i, 0))
    vec = pl.BlockSpec((1, D_MODEL), lambda i: (0, 0))
    return pl.pallas_call(
        body, name="dh_fused", grid=(s // tm,),
        in_specs=[pl.BlockSpec((4, tm, D_MODEL), lambda i: (0, i, 0)), pl.BlockSpec((tm, MS_COLS), lambda i: (i, 0)), row,
                  pl.BlockSpec((tm, 2 * D_MODEL), lambda i: (i, 0)), ANY, row, vec, row, ANY, ANY],
        out_specs=[row, vec, ANY, ANY],
        out_shape=[jax.ShapeDtypeStruct((s, D_MODEL), F32), jax.ShapeDtypeStruct((1, D_MODEL), F32),
                   jax.ShapeDtypeStruct((3, nw, HALF_COLS), hw.dtype), jax.ShapeDtypeStruct((3, nr, HALF_COLS), hr.dtype)],
        scratch_shapes=[pltpu.VMEM(w_int.shape, BF16), pltpu.SemaphoreType.DMA,
                        pltpu.SemaphoreType.DMA((6,)), pltpu.SemaphoreType.DMA((6,))],
        compiler_params=_params(1),
    )(dhg, dms, dmz, dglog, w_int, x, g, dx2, hw, hr)


def _local_step(x, tgt, w_int, r_blk, norm_g, b_gate, lb_logits, hg_norm_g, q_a_g, kv_a_g, fg):
    s = x.shape[0]
    tabs = _rope_tables(s)

    h, hg, ms, mz, glog, gr = _proj_fused(x, norm_g, w_int, r_blk)
    w_uq, w_ukv, wproj = _unpack_rest_weights(gr)
    wuq3 = jnp.pad(w_uq.reshape(Q_LORA, HEADS, QK_DIM).transpose(1, 0, 2), ((0, 0), (0, 0), (0, QK_PAD - QK_DIM)))
    wukv3 = w_ukv.reshape(KV_LORA, HEADS, 256).transpose(1, 0, 2)
    o_pre, ya, st0 = _hgrn_fwd(hg, lb_logits, hg_norm_g)
    q, k, v, cqn, ckvn = _mla_pre(ms, q_a_g, kv_a_g, wuq3, wukv3, tabs)
    o_att, yb, lse = _flash_fwd(q, k, v, mz)
    merged, dx2, dx2b, dya, dyb, dglog, dpa, dpb, loss, dfg, dbg = _merge_fused(ya, yb, glog, b_gate, x, tgt, fg, wproj)

    d_wout = _mm_tn(merged, dx2b, name="dw_out")
    d_wpa = _mm_tn(ya, dpa, name="dw_proj_a")
    d_wpb = _mm_tn(yb, dpb, name="dw_proj_b")
    dhg, dlb, dhgg = _hgrn_bwd(hg, o_pre, dya, st0, lb_logits, hg_norm_g)
    dq, dk, dv, dmz = _flash_bwd(q, k, v, dyb, mz, o_att, lse, tabs)
    dms, d_wuq3, d_wukv3, dqg, dkvg = _mla_bwd_proj(dq, dk, dv, cqn, ckvn, ms, q_a_g, kv_a_g, wuq3, wukv3, tabs)
    d_hg = _mm_tn(dhg, h, name="dw_in_hg")
    d_ms = _mm_tn(dms, h, name="dw_in_ms")
    d_mz = _mm_tn(dmz, h, name="dw_in_mz")
    d_gl = _mm_tn(dglog, h, name="dw_in_gate")
    d_w_int = jnp.concatenate([d_hg.reshape(4 * D_MODEL, D_MODEL), d_ms[0:704], d_mz, d_gl], axis=0)
    small = {"b_gate": dbg, "lb": dlb, "hg_norm_g": dhgg, "q_a_g": dqg, "kv_a_g": dkvg, "final_norm_g": dfg}
    dh_args = (dhg, dms, dmz, dglog, w_int, x, norm_g, dx2)
    return loss, dh_args, d_w_int, d_wuq3, d_wukv3, (d_wpa, d_wpb, d_wout), small


def _pack_rest(w_uq_b, w_ukv_b, wpa_b, wpb_b, wout_b):
    return jnp.concatenate([w_uq_b.reshape(144, D_MODEL), w_ukv_b.reshape(128, D_MODEL), wpa_b, wpb_b, wout_b], axis=0)


def _unpack_rest(p):
    return (p[0:144].reshape(Q_LORA, 384), p[144:272].reshape(KV_LORA, 512), p[272:528], p[528:784], p[784:1040])


def _pack_rest_grads(d_wuq3, d_wukv3, d_proj):
    d_wuq = d_wuq3.transpose(1, 0, 2)[:, :, 0:QK_DIM].reshape(Q_LORA, HEADS * QK_DIM)
    d_wukv = d_wukv3.transpose(1, 0, 2).reshape(KV_LORA, HEADS * 256)
    blocks = []
    for b in range(N_CHIPS):
        rows = slice(b * 256, (b + 1) * 256)
        blocks.append(_pack_rest(d_wuq[:, b * 384:(b + 1) * 384], d_wukv[:, b * 512:(b + 1) * 512],
                                 d_proj[0][rows], d_proj[1][rows], d_proj[2][rows]))
    return jnp.stack(blocks, axis=0)


def _unpack_rest_weights(g):
    parts = [_unpack_rest(g[b]) for b in range(N_CHIPS)]
    w_uq, w_ukv = (jnp.concatenate([p[n] for p in parts], axis=1) for n in range(2))
    wproj = jnp.stack([jnp.concatenate([p[n] for p in parts], axis=0) for n in range(2, 5)], axis=0)
    return w_uq, w_ukv, wproj


MESH_ID = pl.DeviceIdType.MESH
ANY = pl.BlockSpec(memory_space=pl.ANY)
HALF_COLS = D_MODEL // 2


def _me():
    return lax.axis_index("x"), lax.axis_index("y"), lax.axis_index("c")


def _other_chips(x, y):
    return [(1 - x, y), (x, 1 - y), (1 - x, 1 - y)]


def _cols(c):
    return pl.ds(c * HALF_COLS, HALF_COLS)


RELAY_TOP = 992


def _gather_weights(w_blk):
    bot = W_IN_BLK - RELAY_TOP

    def body(w_ref, ow_ref, send_sems, recv_sems):
        x, y, c = _me()
        me, xn, yn, dg = 2 * x + y, 2 * (1 - x) + y, 2 * x + (1 - y), 2 * (1 - x) + (1 - y)
        to_x, to_y, to_sib = (1 - x, y, c), (x, 1 - y, c), (x, y, 1 - c)
        mine, theirs = _cols(c), _cols(1 - c)
        top, low = pl.ds(0, RELAY_TOP), pl.ds(RELAY_TOP, bot)

        def copy(k, src, dst, to):
            return pltpu.make_async_remote_copy(src_ref=src, dst_ref=dst, send_sem=send_sems.at[k],
                                                recv_sem=recv_sems.at[k], device_id=to, device_id_type=MESH_ID)

        def same(k, ref, to):
            return copy(k, ref, ref, to)

        own = [copy(0, w_ref.at[:, mine], ow_ref.at[me, :, mine], to_x),
               copy(1, w_ref.at[:, mine], ow_ref.at[me, :, mine], to_y)]
        for cp in own:
            cp.start()
        from_x, from_y = ow_ref.at[xn, :, mine], ow_ref.at[yn, :, mine]
        same(0, from_x, to_x).wait_recv()
        relay_y = same(2, ow_ref.at[xn, top, mine], to_y)
        pass_x = same(4, from_x, to_sib)
        relay_y.start()
        pass_x.start()
        same(1, from_y, to_y).wait_recv()
        relay_x = same(3, ow_ref.at[yn, low, mine], to_x)
        pass_y = same(5, from_y, to_sib)
        relay_x.start()
        pass_y.start()
        same(2, ow_ref.at[dg, top, mine], to_y).wait_recv()
        same(3, ow_ref.at[dg, low, mine], to_x).wait_recv()
        pass_d = same(6, ow_ref.at[dg, :, mine], to_sib)
        pass_d.start()
        for k, blk in ((4, xn), (5, yn), (6, dg)):
            same(k, ow_ref.at[blk, :, theirs], to_sib).wait_recv()
        for cp in own + [relay_y, relay_x, pass_x, pass_y, pass_d]:
            cp.wait_send()

    gw = pl.pallas_call(
        body, name="gather_weights", in_specs=[ANY], out_specs=ANY,
        out_shape=jax.ShapeDtypeStruct((N_CHIPS,) + w_blk.shape, w_blk.dtype),
        scratch_shapes=[pltpu.SemaphoreType.DMA((7,)), pltpu.SemaphoreType.DMA((7,))],
    )(w_blk)
    return lax.dynamic_update_slice(gw, w_blk[None], (2 * lax.axis_index("x") + lax.axis_index("y"), 0, 0))


def _swap_halves(gw, gr):
    def body(gw_ref, gr_ref, lw_ref, lr_ref, send_sems, recv_sems):
        x, y, c = _me()
        cps = [pltpu.make_async_remote_copy(
            src_ref=src, dst_ref=dst, send_sem=send_sems.at[a], recv_sem=recv_sems.at[a],
            device_id=(x, y, 1 - c), device_id_type=MESH_ID)
            for a, (src, dst) in enumerate([(gw_ref.at[:, _cols(1 - c)], lw_ref),
                                            (gr_ref.at[:, :, _cols(1 - c)], lr_ref)])]
        for cp in cps:
            cp.start()
        for cp in cps:
            cp.wait()

    return pl.pallas_call(
        body, name="grad_swap_halves", in_specs=[ANY, ANY], out_specs=[ANY, ANY],
        out_shape=[jax.ShapeDtypeStruct((gw.shape[0], HALF_COLS), gw.dtype),
                   jax.ShapeDtypeStruct(gr.shape[:2] + (HALF_COLS,), gr.dtype)],
        scratch_shapes=[pltpu.SemaphoreType.DMA((2,)), pltpu.SemaphoreType.DMA((2,))],
    )(gw, gr)


def _swap_reduced(rw, rr):
    def body(rw_ref, rr_ref, ow_ref, or_ref, send_sems, recv_sems):
        x, y, c = _me()
        cps = [pltpu.make_async_remote_copy(
            src_ref=src, dst_ref=dst, send_sem=send_sems.at[a], recv_sem=recv_sems.at[a],
            device_id=(x, y, 1 - c), device_id_type=MESH_ID)
            for a, (src, dst) in enumerate([(rw_ref, ow_ref), (rr_ref, or_ref)])]
        for cp in cps:
            cp.start()
        for cp in cps:
            cp.wait()

    return pl.pallas_call(
        body, name="grad_swap_reduced", in_specs=[ANY, ANY], out_specs=[ANY, ANY],
        out_shape=[jax.ShapeDtypeStruct(rw.shape, rw.dtype), jax.ShapeDtypeStruct(rr.shape, rr.dtype)],
        scratch_shapes=[pltpu.SemaphoreType.DMA((2,)), pltpu.SemaphoreType.DMA((2,))],
    )(rw, rr)


def _join_cols(mine, theirs):
    first = lax.axis_index("c") == 0
    return jnp.concatenate([jnp.where(first, mine, theirs), jnp.where(first, theirs, mine)], axis=1)


def _gather_small(vec):
    def body(v_ref, out_ref, send_sems, recv_sems, local_sem):
        x, y, c = _me()
        my_id = 4 * x + 2 * y + c
        mine = pltpu.make_async_copy(v_ref, out_ref.at[my_id], local_sem)
        mine.start()
        cps = []
        for r in range(1, N_DEV):
            peer = (x ^ (r >> 2), y ^ ((r >> 1) & 1), c ^ (r & 1))
            cps.append(pltpu.make_async_remote_copy(
                src_ref=v_ref, dst_ref=out_ref.at[my_id], send_sem=send_sems.at[r - 1],
                recv_sem=recv_sems.at[r - 1], device_id=peer, device_id_type=MESH_ID))
        for cp in cps:
            cp.start()
        for cp in cps:
            cp.wait()
        mine.wait()

    return pl.pallas_call(
        body, name="gather_small", in_specs=[ANY], out_specs=ANY,
        out_shape=jax.ShapeDtypeStruct((N_DEV, 1, SMALL_COLS), vec.dtype),
        scratch_shapes=[pltpu.SemaphoreType.DMA((N_DEV - 1,)), pltpu.SemaphoreType.DMA((N_DEV - 1,)),
                        pltpu.SemaphoreType.DMA],
    )(vec)


def _add_cores(c_idx, g, landed, *, tm, name):
    r = g.shape[0]

    def body(c_ref, g_ref, l_ref, o32_ref, o16_ref):
        acc = g_ref[...] + l_ref[...]
        o32_ref[...] = acc
        o16_ref[...] = acc.astype(BF16)

    half = pl.BlockSpec((tm, HALF_COLS), lambda i, c_ref: (i, 0))
    grid_spec = pltpu.PrefetchScalarGridSpec(
        num_scalar_prefetch=1, grid=(r // tm,),
        in_specs=[pl.BlockSpec((tm, HALF_COLS), lambda i, c_ref: (i, c_ref[0])), half], out_specs=[half, half])
    return pl.pallas_call(
        body, name=name, grid_spec=grid_spec,
        out_shape=[jax.ShapeDtypeStruct((r, HALF_COLS), F32), jax.ShapeDtypeStruct((r, HALF_COLS), BF16)],
        compiler_params=_params(1),
    )(c_idx, g, landed)


def _add_chips(chip_idx, h32, landed, *, tm, name):
    n = landed.shape[1]
    per = n // tm

    def body(chip_ref, h_ref, l_ref, o_ref):
        acc = h_ref[...]
        for j in range(3):
            acc = acc + l_ref[j].astype(F32)
        o_ref[...] = acc

    grid_spec = pltpu.PrefetchScalarGridSpec(
        num_scalar_prefetch=1, grid=(per,),
        in_specs=[pl.BlockSpec((tm, HALF_COLS), lambda i, chip_ref: (chip_ref[0] * per + i, 0)),
                  pl.BlockSpec((3, tm, HALF_COLS), lambda i, chip_ref: (0, i, 0))],
        out_specs=pl.BlockSpec((tm, HALF_COLS), lambda i, chip_ref: (i, 0)))
    return pl.pallas_call(
        body, name=name, grid_spec=grid_spec, out_shape=jax.ShapeDtypeStruct((n, HALF_COLS), F32),
        compiler_params=_params(1),
    )(chip_idx, h32, landed)


def _pack_small(small, lb_logits, loss):
    def body(ng_ref, bg_ref, dlb_ref, lbl_ref, hgg_ref, qg_ref, kvg_ref, fg_ref, loss_ref, out_ref):
        out_ref[...] = jnp.zeros_like(out_ref)
        out_ref[:, 0:1024] = ng_ref[...]
        out_ref[:, 1024:3072] = bg_ref[...]
        _, p0p1 = _lower_bound(lbl_ref[...])
        dl0 = dlb_ref[...] * p0p1
        out_ref[:, 3072:4096] = dl0
        out_ref[:, 4096:5120] = -dl0
        hgg = hgg_ref[0]
        for h in range(1, HEADS):
            hgg = hgg + hgg_ref[h]
        out_ref[:, 5120:5248] = hgg
        out_ref[:, 5248:5632] = qg_ref[...]
        out_ref[:, 5632:5888] = kvg_ref[...]
        out_ref[:, 5888:6912] = fg_ref[...]
        out_ref[:, 6912:7040] = loss_ref[...]

    return pl.pallas_call(
        body, name="pack_small", out_shape=jax.ShapeDtypeStruct((1, SMALL_COLS), F32),
    )(small["norm_g"], small["b_gate"], small["lb"], lb_logits, small["hg_norm_g"], small["q_a_g"],
      small["kv_a_g"], small["final_norm_g"], loss)


def _adamw_math(w, g, m, v):
    nm = ADAM_B1 * m + (1.0 - ADAM_B1) * g
    nv = ADAM_B2 * v + (1.0 - ADAM_B2) * (g * g)
    m_hat = nm / (1.0 - ADAM_B1 ** ADAM_STEP)
    v_hat = nv / (1.0 - ADAM_B2 ** ADAM_STEP)
    return -ADAM_LR * (m_hat / (jnp.sqrt(v_hat) + ADAM_EPS) + ADAM_WD * w), nm, nv


def _adamw(w, g, m, v, *, name, tm):
    r, cols = w.shape

    def body(w_ref, g_ref, m_ref, v_ref, d_ref, nm_ref, nv_ref):
        d_ref[...], nm_ref[...], nv_ref[...] = _adamw_math(w_ref[...], g_ref[...], m_ref[...], v_ref[...])

    row = pl.BlockSpec((tm, cols), lambda i: (i, 0))
    shp = jax.ShapeDtypeStruct((r, cols), F32)
    return pl.pallas_call(
        body, name=name, grid=(r // tm,), in_specs=[row] * 4, out_specs=[row] * 3, out_shape=[shp] * 3,
        compiler_params=_params(1),
    )(w, g, m, v)


SMALL_SLOTS = (("norm_g", (0,)), ("b_gate", (1024,)), ("lb_logits", (3072, 4096)), ("hg_norm_g", (5120,)),
               ("q_a_g", (5248,)), ("kv_a_g", (5632,)), ("final_norm_g", (5888,)))
LOSS_SLOT = 6912


def _small_update(gathered, ws, ms, vs):
    n = len(SMALL_SLOTS)

    def body(*refs):
        g_ref = refs[0]
        w_refs, m_refs, v_refs = refs[1:1 + n], refs[1 + n:1 + 2 * n], refs[1 + 2 * n:1 + 3 * n]
        outs = refs[1 + 3 * n:]
        loss_ref = outs[0]
        g_out, d_out, nm_out, nv_out = (outs[1 + k * n:1 + (k + 1) * n] for k in range(4))
        total = g_ref[0]
        for dev in range(1, N_DEV):
            total = total + g_ref[dev]
        loss_ref[...] = total[:, LOSS_SLOT:LOSS_SLOT + 128]
        for p, (_, offsets) in enumerate(SMALL_SLOTS):
            cols = w_refs[p].shape[1]
            for r, off in enumerate(offsets):
                rows = slice(r, r + 1)
                g = total[:, off:off + cols]
                g_out[p][rows, :] = g
                d_out[p][rows, :], nm_out[p][rows, :], nv_out[p][rows, :] = _adamw_math(
                    w_refs[p][rows, :], g, m_refs[p][rows, :], v_refs[p][rows, :])

    shapes = [jax.ShapeDtypeStruct(w.shape, F32) for w in ws]
    res = pl.pallas_call(
        body, name="small_update", out_shape=[jax.ShapeDtypeStruct((1, 128), F32)] + shapes * 4,
    )(gathered, *ws, *ms, *vs)
    return res[0], res[1:1 + n], res[1 + n:1 + 2 * n], res[1 + 2 * n:1 + 3 * n], res[1 + 3 * n:1 + 4 * n]


def kernel(x, norm_g, w_in, b_gate, lb_logits, hg_norm_g, q_a_g, w_uq, kv_a_g, w_ukv, w_proj_a, w_proj_b, w_out, final_norm_g, loss_target, m_norm_g, m_w_in, m_b_gate, m_lb_logits, m_hg_norm_g, m_q_a_g, m_w_uq, m_kv_a_g, m_w_ukv, m_w_proj_a, m_w_proj_b, m_w_out, m_final_norm_g, v_norm_g, v_w_in, v_b_gate, v_lb_logits, v_hg_norm_g, v_q_a_g, v_w_uq, v_kv_a_g, v_w_ukv, v_w_proj_a, v_w_proj_b, v_w_out, v_final_norm_g):
    c_idx = lax.axis_index("c").astype(jnp.int32).reshape(1)
    chip_idx = (2 * lax.axis_index("x") + lax.axis_index("y")).astype(jnp.int32).reshape(1)

    w_blk = w_in[0].T.astype(BF16)
    r_blk = _pack_rest(w_uq[0], w_ukv[0], w_proj_a[0], w_proj_b[0], w_out[0]).astype(BF16)
    gw = _gather_weights(w_blk)

    loss, dh_args, d_w_int, d_wuq3, d_wukv3, d_proj, small = _local_step(
        x[0], loss_target[0], gw.reshape(W_IN_COLS, D_MODEL), r_blk,
        norm_g, b_gate, lb_logits, hg_norm_g, q_a_g, kv_a_g, final_norm_g.reshape(1, D_MODEL))

    d_rest = _pack_rest_grads(d_wuq3, d_wukv3, d_proj)
    lw, lr = _swap_halves(d_w_int, d_rest)
    hw32, hw16 = _add_cores(c_idx, d_w_int, lw, tm=656, name="grad_add_cores_w")
    hr32, hr16 = _add_cores(c_idx, d_rest.reshape(N_CHIPS * REST_ROWS, D_MODEL), lr.reshape(N_CHIPS * REST_ROWS, HALF_COLS),
                            tm=REST_ROWS, name="grad_add_cores_r")
    grad_x, small["norm_g"], landed_w, landed_r = _dh_fused(*dh_args, hw16, hr16)
    rw = _add_chips(chip_idx, hw32, landed_w, tm=656, name="grad_add_chips_w")
    rr = _add_chips(chip_idx, hr32, landed_r, tm=208, name="grad_add_chips_r")
    tw, tr = _swap_reduced(rw, rr)
    g_w_in = _join_cols(rw, tw).T
    g_rest = _join_cols(rr, tr)
    g_uq, g_ukv, g_pa, g_pb, g_out = _unpack_rest(g_rest)

    small_all = _gather_small(_pack_small(small, lb_logits, loss))

    upd = {
        "w_in": _adamw(w_in[0], g_w_in, m_w_in[0], v_w_in[0], name="adamw_w_in", tm=128),
        "w_uq": _adamw(w_uq[0], g_uq, m_w_uq[0], v_w_uq[0], name="adamw_w_uq", tm=Q_LORA),
        "w_ukv": _adamw(w_ukv[0], g_ukv, m_w_ukv[0], v_w_ukv[0], name="adamw_w_ukv", tm=KV_LORA),
        "w_proj_a": _adamw(w_proj_a[0], g_pa, m_w_proj_a[0], v_w_proj_a[0], name="adamw_w_proj_a", tm=256),
        "w_proj_b": _adamw(w_proj_b[0], g_pb, m_w_proj_b[0], v_w_proj_b[0], name="adamw_w_proj_b", tm=256),
        "w_out": _adamw(w_out[0], g_out, m_w_out[0], v_w_out[0], name="adamw_w_out", tm=256),
    }
    loss_vec, *small_sets = _small_update(
        small_all,
        [norm_g, b_gate, lb_logits, hg_norm_g, q_a_g, kv_a_g, final_norm_g.reshape(1, D_MODEL)],
        [m_norm_g, m_b_gate, m_lb_logits, m_hg_norm_g, m_q_a_g, m_kv_a_g, m_final_norm_g.reshape(1, D_MODEL)],
        [v_norm_g, v_b_gate, v_lb_logits, v_hg_norm_g, v_q_a_g, v_kv_a_g, v_final_norm_g.reshape(1, D_MODEL)])

    def outputs(big, small_set):
        s_ng, s_bg, s_lb, s_hg, s_qg, s_kvg, s_fg = small_set
        return (s_ng, big["w_in"][None], s_bg, s_lb, s_hg, s_qg, big["w_uq"][None], s_kvg, big["w_ukv"][None],
                big["w_proj_a"][None], big["w_proj_b"][None], big["w_out"][None], s_fg.reshape(D_MODEL))

    grads = {"w_in": g_w_in, "w_uq": g_uq, "w_ukv": g_ukv, "w_proj_a": g_pa, "w_proj_b": g_pb, "w_out": g_out}
    return (loss_vec[0, 0], grad_x[None], *outputs(grads, small_sets[0]),
            *(o for k in range(3) for o in outputs({n: u[k] for n, u in upd.items()}, small_sets[1 + k])))
```

```python
import functools

import jax
import jax.numpy as jnp
from jax import lax
from jax.experimental import pallas as pl
from jax.experimental.pallas import tpu as pltpu

F32 = jnp.float32
BF16 = jnp.bfloat16

D_MODEL = 1024
HEADS = 8
HEAD_DIM = 128
HG_CHUNK = 32
CHUNK_SHIFT = 5
HEAD_SHIFT = 7
QK_NOPE = 128
QK_ROPE = 64
QK_DIM = QK_NOPE + QK_ROPE
QK_PAD = 256
Q_LORA = 384
KV_LORA = 256
MS_COLS = 768
ROPE_THETA = 10000.0
EPS = 1e-6
ATT_SCALE = QK_DIM ** -0.5
LOG2E = 1.4426950408889634
LN2 = 0.6931471805599453
Q_PRESCALE = ATT_SCALE * LOG2E

ADAM_LR = 0.001
ADAM_B1 = 0.9
ADAM_B2 = 0.999
ADAM_EPS = 1e-08
ADAM_WD = 0.01
ADAM_STEP = 10

N_CHIPS = 4
N_DEV = 8
W_IN_COLS = 7872
W_IN_BLK = W_IN_COLS // N_CHIPS
REST_ROWS = 144 + 128 + 3 * 256
SMALL_COLS = 7168

TM_MM = 1024
TM_FUSED = 256
TM_MLA = 512
HG_ROWS = 128
TQ = 512
FLASH_HEADS = 2
HG_HEADS = 8
VMEM_LIMIT = 56 * 1024 * 1024


def _dot(a, b):
    return lax.dot_general(a, b, (((1,), (0,)), ((), ())), preferred_element_type=F32)


def _dot_nt(a, b):
    return lax.dot_general(a, b, (((1,), (1,)), ((), ())), preferred_element_type=F32)


def _dot_tn(a, b):
    return lax.dot_general(a, b, (((0,), (0,)), ((), ())), preferred_element_type=F32)


def _params(n_axes):
    return pltpu.CompilerParams(dimension_semantics=("arbitrary",) * n_axes, vmem_limit_bytes=VMEM_LIMIT)


def _rms(x, g):
    r = lax.rsqrt(jnp.mean(x * x, axis=-1, keepdims=True) + EPS)
    return x * r * g


def _rms_bwd(x, g, dy):
    r = lax.rsqrt(jnp.mean(x * x, axis=-1, keepdims=True) + EPS)
    xh = x * r
    dyg = dy * g
    dx = r * (dyg - xh * jnp.mean(dyg * xh, axis=-1, keepdims=True))
    return dx, dy * xh


def _silu_parts(z):
    s = jax.nn.sigmoid(z)
    return z * s, s * (1.0 + z * (1.0 - s))


def _rope(x, c, sa, sb):
    return x * c + pltpu.roll(x, 32, 1) * sa + pltpu.roll(x, 96, 1) * sb


def _rope_bwd(dy, c, sa, sb):
    return dy * c + pltpu.roll(dy * sa, 96, 1) + pltpu.roll(dy * sb, 32, 1)


def _rope_tables(seq):
    inv = ROPE_THETA ** (-jnp.arange(0, QK_ROPE, 2, dtype=F32) / QK_ROPE)
    ang = jnp.arange(seq, dtype=F32)[:, None] * inv[None, :]
    cos, sin = jnp.cos(ang), jnp.sin(ang)
    z32 = jnp.zeros_like(cos)
    z64 = jnp.zeros((seq, 64), F32)
    c = jnp.concatenate([cos, cos, z64], axis=1)
    sa = jnp.concatenate([z32, sin, z64], axis=1)
    sb = jnp.concatenate([-sin, z32, z64], axis=1)
    return c, sa, sb


def _mm_tn(a, b, *, name, tm=TM_MM, tn=1024):
    flat = a.ndim == 2
    if flat:
        a = a[None]
    g, m, k = a.shape
    n = b.shape[1]
    tm, tn = min(tm, m), min(tn, n)
    assert m % tm == 0 and n % tn == 0

    def body(a_ref, b_ref, o_ref):
        @pl.when(pl.program_id(2) == 0)
        def _():
            o_ref[...] = jnp.zeros_like(o_ref)

        o_ref[...] += _dot_tn(a_ref[...], b_ref[...])

    out = pl.pallas_call(
        body, name=name, grid=(g, n // tn, m // tm),
        in_specs=[pl.BlockSpec((None, tm, k), lambda s, j, i: (s, i, 0)),
                  pl.BlockSpec((tm, tn), lambda s, j, i: (i, j))],
        out_specs=pl.BlockSpec((None, k, tn), lambda s, j, i: (s, 0, j)),
        out_shape=jax.ShapeDtypeStruct((g, k, n), F32), compiler_params=_params(3),
    )(a, b)
    return out[0] if flat else out


def _chunk_rows(rows):
    return lax.broadcasted_iota(jnp.int32, (rows, HEAD_DIM), 0) & (HG_CHUNK - 1)


def _chunk_cumsum(x, rows):
    pos = _chunk_rows(rows)
    shift = 1
    while shift < HG_CHUNK:
        x = x + jnp.where(pos >= shift, pltpu.roll(x, shift, 0), 0.0)
        shift *= 2
    return x


def _chunk_revcumsum(x, rows):
    pos = _chunk_rows(rows)
    shift = 1
    while shift < HG_CHUNK:
        x = x + jnp.where(pos + shift < HG_CHUNK, pltpu.roll(x, rows - shift, 0), 0.0)
        shift *= 2
    return x


def _lower_bound(lbl):
    mx = jnp.maximum(lbl[0:1, :], lbl[1:2, :])
    e0 = jnp.exp(lbl[0:1, :] - mx)
    e1 = jnp.exp(lbl[1:2, :] - mx)
    p0 = e0 / (e0 + e1)
    return p0, p0 * (e1 / (e0 + e1))


def _hg_masks(rows, nch, tmask_s, bdmask_s):
    r = lax.broadcasted_iota(jnp.int32, (rows, rows), 0)
    c = lax.broadcasted_iota(jnp.int32, (rows, rows), 1)
    tmask_s[...] = jnp.where(((r >> CHUNK_SHIFT) == (c >> CHUNK_SHIFT)) & (r >= c), 1.0, 0.0)
    r = lax.broadcasted_iota(jnp.int32, (rows, nch * HEAD_DIM), 0)
    c = lax.broadcasted_iota(jnp.int32, (rows, nch * HEAD_DIM), 1)
    bdmask_s[...] = jnp.where((r >> CHUNK_SHIFT) == (c >> HEAD_SHIFT), 1.0, 0.0).astype(BF16)


def _block_diag(x, nch, bdmask):
    return jnp.tile(x, (1, nch)) * bdmask


def _hgrn_fwd(hg, lb_logits, norm_g):
    s = hg.shape[1]
    rows = min(HG_ROWS, s)
    nblk = s // rows
    nch = rows // HG_CHUNK

    def body(hg_ref, lbl_ref, g_ref, o_ref, ya_ref, st0_ref, st_s, stall_s, tmask_s, bdmask_s):
        @pl.when(pl.program_id(1) == 0)
        def _():
            st_s[...] = jnp.zeros_like(st_s)
            _hg_masks(rows, nch, tmask_s, bdmask_s)

        bdmask = bdmask_s[...]
        tmask = tmask_s[...] > 0.5
        for hh in range(HG_HEADS):
            hc = slice(hh * HEAD_DIM, (hh + 1) * HEAD_DIM)
            hq = hg_ref[0, :, hc]
            hf = hg_ref[1, :, hc]
            hi = hg_ref[2, :, hc]
            hz = hg_ref[3, :, hc]
            lb, _ = _lower_bound(lbl_ref[:, hc])
            f = lb + (1.0 - lb) * jax.nn.sigmoid(hf)
            q = hq * jax.nn.sigmoid(hq)
            k = 1.0 - f
            logf = jnp.log(f)
            b = _chunk_cumsum(logf, rows)
            q_in = (q * jnp.exp(b)).astype(BF16)
            k_in = (k * jnp.exp(-b)).astype(BF16)
            k_out = (k * jnp.exp(_chunk_revcumsum(logf, rows) - logf)).astype(BF16)
            vb = hi.astype(BF16)

            sc = jnp.where(tmask, _dot_nt(q_in, k_in), 0.0)
            o_intra = _dot(sc.astype(BF16), vb)
            kvt = _dot_tn(vb, _block_diag(k_out, nch, bdmask))
            st = st_s[hh]
            st0_ref[hh] = st
            for c in range(nch):
                cols = slice(c * HEAD_DIM, (c + 1) * HEAD_DIM)
                last = (c + 1) * HG_CHUNK - 1
                stall_s[hh, :, cols] = st.astype(BF16)
                st = st * jnp.exp(b[last:last + 1, :]) + kvt[:, cols]
            st_s[hh] = st
            o = o_intra + _dot_nt(_block_diag(q_in, nch, bdmask), stall_s[hh])
            o_ref[:, hc] = o
            silu_z, _ = _silu_parts(hz)
            ya_ref[:, hc] = (_rms(o, g_ref[...]) * silu_z).astype(BF16)

    nh = HG_HEADS
    return pl.pallas_call(
        body, name="hgrn_fwd", grid=(HEADS // nh, nblk),
        in_specs=[pl.BlockSpec((4, rows, nh * HEAD_DIM), lambda h, i: (0, i, h)),
                  pl.BlockSpec((2, nh * HEAD_DIM), lambda h, i: (0, h)),
                  pl.BlockSpec((1, HEAD_DIM), lambda h, i: (0, 0))],
        out_specs=[pl.BlockSpec((rows, nh * HEAD_DIM), lambda h, i: (i, h)),
                   pl.BlockSpec((rows, nh * HEAD_DIM), lambda h, i: (i, h)),
                   pl.BlockSpec((nh, None, HEAD_DIM, HEAD_DIM), lambda h, i: (h, i, 0, 0))],
        out_shape=[jax.ShapeDtypeStruct((s, D_MODEL), F32), jax.ShapeDtypeStruct((s, D_MODEL), BF16),
                   jax.ShapeDtypeStruct((HEADS, nblk, HEAD_DIM, HEAD_DIM), F32)],
        scratch_shapes=[pltpu.VMEM((nh, HEAD_DIM, HEAD_DIM), F32), pltpu.VMEM((nh, HEAD_DIM, nch * HEAD_DIM), BF16),
                        pltpu.VMEM((rows, rows), F32), pltpu.VMEM((rows, nch * HEAD_DIM), BF16)],
        compiler_params=_params(2),
    )(hg, lb_logits, norm_g)


def _hgrn_bwd(hg, o_pre, dya, st0, lb_logits, norm_g):
    s = hg.shape[1]
    rows = min(HG_ROWS, s)
    nblk = s // rows
    nch = rows // HG_CHUNK

    def body(hg_ref, o_ref, dya_ref, st0_ref, lbl_ref, g_ref, dhg_ref, dlb_ref, dg_ref,
             dst_s, stp_s, stp_rows_s, dst_rows_s, dst_lane_s, dbl_s, tmask_s, bdmask_s):
        @pl.when(pl.program_id(1) == 0)
        def _():
            dst_s[...] = jnp.zeros_like(dst_s)
            dlb_ref[...] = jnp.zeros_like(dlb_ref)
            dg_ref[...] = jnp.zeros_like(dg_ref)
            _hg_masks(rows, nch, tmask_s, bdmask_s)

        bdmask = bdmask_s[...]
        tmask = tmask_s[...] > 0.5
        g = g_ref[...]
        for hh in range(HG_HEADS):
            hc = slice(hh * HEAD_DIM, (hh + 1) * HEAD_DIM)
            hq = hg_ref[0, :, hc]
            hf = hg_ref[1, :, hc]
            hi = hg_ref[2, :, hc]
            hz = hg_ref[3, :, hc]
            lb, _ = _lower_bound(lbl_ref[:, hc])
            sg = jax.nn.sigmoid(hf)
            f = lb + (1.0 - lb) * sg
            q, dsilu_q = _silu_parts(hq)
            k = 1.0 - f
            logf = jnp.log(f)
            b = _chunk_cumsum(logf, rows)
            eb = jnp.exp(b)
            enb = jnp.exp(-b)
            ebl = jnp.exp(_chunk_revcumsum(logf, rows) - logf)
            q_in32 = q * eb
            k_in32 = k * enb
            k_out32 = k * ebl
            q_in = q_in32.astype(BF16)
            k_in = k_in32.astype(BF16)
            k_out = k_out32.astype(BF16)
            vb = hi.astype(BF16)
            kbd = _block_diag(k_out, nch, bdmask)
            qbd = _block_diag(q_in, nch, bdmask)
            decs = [jnp.exp(b[(c + 1) * HG_CHUNK - 1:(c + 1) * HG_CHUNK, :]) for c in range(nch)]

            kvt = _dot_tn(vb, kbd)
            st = st0_ref[hh]
            for c in range(nch):
                stp_s[hh, c] = st
                stp_rows_s[hh, c * HEAD_DIM:(c + 1) * HEAD_DIM, :] = st.astype(BF16)
                st = st * decs[c] + kvt[:, c * HEAD_DIM:(c + 1) * HEAD_DIM]

            o = o_ref[:, hc]
            rstd = lax.rsqrt(jnp.mean(o * o, axis=-1, keepdims=True) + EPS)
            oh = o * rstd
            silu_z, dsilu_z = _silu_parts(hz)
            dya_v = dya_ref[:, hc]
            dn = dya_v * silu_z
            dhz = dya_v * (oh * g) * dsilu_z
            dg_ref[hh] += jnp.sum(dn * oh, axis=0, keepdims=True)
            doh = dn * g
            do = (rstd * (doh - oh * jnp.mean(doh * oh, axis=-1, keepdims=True))).astype(BF16)

            dq_all = _dot_tn(do, qbd)
            dst = dst_s[hh]
            ddecs = [None] * nch
            for c in reversed(range(nch)):
                dstb = dst.astype(BF16)
                dst_lane_s[hh, :, c * HEAD_DIM:(c + 1) * HEAD_DIM] = dstb
                dst_rows_s[hh, c * HEAD_DIM:(c + 1) * HEAD_DIM, :] = dstb
                ddecs[c] = jnp.sum(dst * stp_s[hh, c], axis=0, keepdims=True) * decs[c]
                dst = dst * decs[c] + dq_all[:, c * HEAD_DIM:(c + 1) * HEAD_DIM]
            dst_s[hh] = dst

            sc = jnp.where(tmask, _dot_nt(q_in, k_in), 0.0).astype(BF16)
            dkout = _dot(_block_diag(vb, nch, bdmask), dst_rows_s[hh])
            dv = _dot_nt(kbd, dst_lane_s[hh]) + _dot_tn(sc, do)
            dsc = jnp.where(tmask, _dot_nt(do, vb), 0.0).astype(BF16)
            dqin = _dot(dsc, k_in) + _dot(_block_diag(do, nch, bdmask), stp_rows_s[hh])
            dkin = _dot_tn(dsc, q_in)

            dko = dkout * k_out32
            for c in range(nch):
                sl = slice(c * HG_CHUNK, (c + 1) * HG_CHUNK)
                dbl = jnp.sum(dko[sl], axis=0, keepdims=True) + ddecs[c]
                dbl_s[hh, sl, :] = jnp.broadcast_to(dbl, (HG_CHUNK, HEAD_DIM))
            dq = dqin * eb
            dk = dkin * enb + dkout * ebl
            db = dqin * q_in32 - dkin * k_in32 - dko
            dlogf = _chunk_revcumsum(db, rows) + dbl_s[hh]
            df = dlogf / f - dk
            dlb_ref[:, hc] += jnp.sum(df * (1.0 - sg), axis=0, keepdims=True)
            dhg_ref[0, :, hc] = (dq * dsilu_q).astype(BF16)
            dhg_ref[1, :, hc] = (df * (1.0 - lb) * sg * (1.0 - sg)).astype(BF16)
            dhg_ref[2, :, hc] = dv.astype(BF16)
            dhg_ref[3, :, hc] = dhz.astype(BF16)

    last = nblk - 1
    nh = HG_HEADS
    wide = nh * HEAD_DIM
    return pl.pallas_call(
        body, name="hgrn_bwd", grid=(HEADS // nh, nblk),
        in_specs=[pl.BlockSpec((4, rows, wide), lambda h, i: (0, last - i, h)),
                  pl.BlockSpec((rows, wide), lambda h, i: (last - i, h)),
                  pl.BlockSpec((rows, wide), lambda h, i: (last - i, h)),
                  pl.BlockSpec((nh, None, HEAD_DIM, HEAD_DIM), lambda h, i: (h, last - i, 0, 0)),
                  pl.BlockSpec((2, wide), lambda h, i: (0, h)),
                  pl.BlockSpec((1, HEAD_DIM), lambda h, i: (0, 0))],
        out_specs=[pl.BlockSpec((4, rows, wide), lambda h, i: (0, last - i, h)),
                   pl.BlockSpec((1, wide), lambda h, i: (0, h)),
                   pl.BlockSpec((nh, 1, HEAD_DIM), lambda h, i: (h, 0, 0))],
        out_shape=[jax.ShapeDtypeStruct((4, s, D_MODEL), BF16), jax.ShapeDtypeStruct((1, D_MODEL), F32),
                   jax.ShapeDtypeStruct((HEADS, 1, HEAD_DIM), F32)],
        scratch_shapes=[pltpu.VMEM((nh, HEAD_DIM, HEAD_DIM), F32), pltpu.VMEM((nh, nch, HEAD_DIM, HEAD_DIM), F32),
                        pltpu.VMEM((nh, nch * HEAD_DIM, HEAD_DIM), BF16), pltpu.VMEM((nh, nch * HEAD_DIM, HEAD_DIM), BF16),
                        pltpu.VMEM((nh, HEAD_DIM, nch * HEAD_DIM), BF16), pltpu.VMEM((nh, rows, HEAD_DIM), F32),
                        pltpu.VMEM((rows, rows), F32), pltpu.VMEM((rows, nch * HEAD_DIM), BF16)],
        compiler_params=_params(2),
    )(hg, o_pre, dya, st0, lb_logits, norm_g)


def _mla_pre(ms, q_a_g, kv_a_g, wuq3, wukv3, tabs):
    s = ms.shape[0]
    tm = min(TM_MLA, s)

    def body(ms_ref, qg_ref, kvg_ref, wuq_ref, wukv_ref, c_ref, sa_ref, sb_ref,
             q_ref, k_ref, v_ref, cqn_ref, ckvn_ref):
        c, sa, sb = c_ref[...], sa_ref[...], sb_ref[...]
        cqn = _rms(ms_ref[:, 0:Q_LORA], qg_ref[...]).astype(BF16)
        ckvn = _rms(ms_ref[:, Q_LORA:Q_LORA + KV_LORA], kvg_ref[...]).astype(BF16)
        cqn_ref[...] = cqn
        ckvn_ref[...] = ckvn
        k_pe = _rope(ms_ref[:, Q_LORA + KV_LORA:MS_COLS], c, sa, sb).astype(BF16)
        for h in range(HEADS):
            qh = _dot(cqn, wuq_ref[h])
            q_ref[h, :, 0:128] = (qh[:, 0:128] * Q_PRESCALE).astype(BF16)
            q_ref[h, :, 128:256] = (_rope(qh[:, 128:256], c, sa, sb) * Q_PRESCALE).astype(BF16)
            kvh = _dot(ckvn, wukv_ref[h])
            k_ref[h, :, 0:128] = kvh[:, 0:128].astype(BF16)
            k_ref[h, :, 128:256] = k_pe
            v_ref[h] = kvh[:, 128:256].astype(BF16)

    tab = pl.BlockSpec((tm, 128), lambda i: (i, 0))
    return pl.pallas_call(
        body, name="mla_pre", grid=(s // tm,),
        in_specs=[pl.BlockSpec((tm, MS_COLS), lambda i: (i, 0)),
                  pl.BlockSpec((1, Q_LORA), lambda i: (0, 0)), pl.BlockSpec((1, KV_LORA), lambda i: (0, 0)),
                  pl.BlockSpec((HEADS, Q_LORA, QK_PAD), lambda i: (0, 0, 0)),
                  pl.BlockSpec((HEADS, KV_LORA, 256), lambda i: (0, 0, 0)), tab, tab, tab],
        out_specs=[pl.BlockSpec((HEADS, tm, QK_PAD), lambda i: (0, i, 0)),
                   pl.BlockSpec((HEADS, tm, QK_PAD), lambda i: (0, i, 0)),
                   pl.BlockSpec((HEADS, tm, HEAD_DIM), lambda i: (0, i, 0)),
                   pl.BlockSpec((tm, Q_LORA), lambda i: (i, 0)), pl.BlockSpec((tm, KV_LORA), lambda i: (i, 0))],
        out_shape=[jax.ShapeDtypeStruct((HEADS, s, QK_PAD), BF16), jax.ShapeDtypeStruct((HEADS, s, QK_PAD), BF16),
                   jax.ShapeDtypeStruct((HEADS, s, HEAD_DIM), BF16),
                   jax.ShapeDtypeStruct((s, Q_LORA), BF16), jax.ShapeDtypeStruct((s, KV_LORA), BF16)],
        compiler_params=_params(1),
    )(ms, q_a_g, kv_a_g, wuq3, wukv3, *tabs)


def _causal_mask(t):
    r = lax.broadcasted_iota(jnp.int32, (t, t), 0)
    c = lax.broadcasted_iota(jnp.int32, (t, t), 1)
    return r >= c


def _flash_fwd(q, k, v, mz):
    s = q.shape[1]
    t = min(TQ, s)

    def body(q_ref, k_ref, v_ref, mz_ref, o_ref, yb_ref, lse_ref, m_s, l_s, acc_s):
        i = pl.program_id(1)
        m_s[...] = jnp.full_like(m_s, -jnp.inf)
        l_s[...] = jnp.zeros_like(l_s)
        acc_s[...] = jnp.zeros_like(acc_s)

        def step(j, groups):
            rows = pl.ds(pl.multiple_of(j * t, t), t)
            for hh in range(FLASH_HEADS):
                for r0, nr, masked in groups:
                    r = slice(r0, r0 + nr)
                    sc = _dot_nt(q_ref[hh, r, :], k_ref[hh, rows, :])
                    if masked:
                        sc = jnp.where(_causal_mask(t), sc, -jnp.inf)
                    m_prev = m_s[hh, r, :]
                    m_new = jnp.maximum(m_prev, jnp.max(sc, axis=-1, keepdims=True))
                    p = jnp.exp2(sc - jnp.tile(m_new, (1, t // 128)))
                    alpha = jnp.exp2(m_prev - m_new)
                    l_s[hh, r, :] = alpha * l_s[hh, r, :] + jnp.sum(p, axis=-1, keepdims=True)
                    acc_s[hh, r, :] = alpha * acc_s[hh, r, :] + _dot(p.astype(BF16), v_ref[hh, rows, :])
                    m_s[hh, r, :] = m_new

        def loop_body(jj, carry):
            step(2 * jj, ((0, 2 * t, False),))
            step(2 * jj + 1, ((0, 2 * t, False),))
            return carry

        lax.fori_loop(0, i, loop_body, 0)
        step(2 * i, ((0, t, True), (t, t, False)))
        step(2 * i + 1, ((t, t, True),))
        for hh in range(FLASH_HEADS):
            cols = slice(hh * HEAD_DIM, (hh + 1) * HEAD_DIM)
            out = acc_s[hh] / l_s[hh]
            o_ref[:, cols] = out
            silu_z, _ = _silu_parts(mz_ref[:, cols])
            yb_ref[:, cols] = (out * silu_z).astype(BF16)
            lse_ref[hh] = m_s[hh] + jnp.log2(l_s[hh])

    nh = FLASH_HEADS
    t2 = 2 * t
    col = pl.BlockSpec((t2, nh * HEAD_DIM), lambda h, i: (i, h))
    return pl.pallas_call(
        body, name="flash_fwd", grid=(HEADS // nh, s // t2),
        in_specs=[pl.BlockSpec((nh, t2, QK_PAD), lambda h, i: (h, i, 0)),
                  pl.BlockSpec((nh, s, QK_PAD), lambda h, i: (h, 0, 0)),
                  pl.BlockSpec((nh, s, HEAD_DIM), lambda h, i: (h, 0, 0)), col],
        out_specs=[col, col, pl.BlockSpec((nh, t2, 128), lambda h, i: (h, i, 0))],
        out_shape=[jax.ShapeDtypeStruct((s, D_MODEL), F32), jax.ShapeDtypeStruct((s, D_MODEL), BF16),
                   jax.ShapeDtypeStruct((HEADS, s, 128), F32)],
        scratch_shapes=[pltpu.VMEM((nh, t2, 128), F32), pltpu.VMEM((nh, t2, 128), F32),
                        pltpu.VMEM((nh, t2, HEAD_DIM), F32)],
        compiler_params=_params(2),
    )(q, k, v, mz)


def _flash_bwd(q, k, v, dyb, mz, o_att, lse, tabs):
    s = q.shape[1]
    t = min(TQ, s)

    def body(q_ref, k_ref, v_ref, dyb_ref, mz_ref, o_ref, lse_ref, c_ref, sa_ref, sb_ref,
             dq_ref, dk_ref, dv_ref, dmz_ref, dq_s, delta_s, do_s):
        i = pl.program_id(1)

        @pl.when(i == 0)
        def _():
            dk_ref[...] = jnp.zeros_like(dk_ref)
            dv_ref[...] = jnp.zeros_like(dv_ref)

        silu_z, dsilu_z = _silu_parts(mz_ref[...])
        dyb_v = dyb_ref[...]
        out = o_ref[...]
        do32 = dyb_v * silu_z
        dmz_ref[...] = (dyb_v * out * dsilu_z).astype(BF16)
        delta_s[...] = jnp.broadcast_to(jnp.sum(do32 * out, axis=-1, keepdims=True), (2 * t, 128))
        do_s[...] = do32.astype(BF16)
        dq_s[...] = jnp.zeros_like(dq_s)

        def step(j, modes):
            rows = pl.ds(pl.multiple_of(j * t, t), t)
            kj = k_ref[rows, :]
            vj = v_ref[rows, :]
            dv_acc = None
            dk_acc = None
            for ch, masked in enumerate(modes):
                if masked is None:
                    continue
                r = slice(ch * t, (ch + 1) * t)
                qv = q_ref[r, :]
                do = do_s[r, :]
                sc = _dot_nt(qv, kj)
                if masked:
                    sc = jnp.where(_causal_mask(t), sc, -jnp.inf)
                p = jnp.exp2(sc - jnp.tile(lse_ref[r, :], (1, t // 128)))
                dp = _dot_nt(do, vj)
                ds = (p * (dp - jnp.tile(delta_s[r, :], (1, t // 128)))).astype(BF16)
                dv_c = _dot_tn(p.astype(BF16), do)
                dk_c = _dot_tn(ds, qv)
                dv_acc = dv_c if dv_acc is None else dv_acc + dv_c
                dk_acc = dk_c if dk_acc is None else dk_acc + dk_c
                dq_s[r, :] += _dot(ds, kj)
            dv_ref[rows, :] += dv_acc
            dk_ref[rows, :] += dk_acc

        def loop_body(jj, carry):
            step(2 * jj, (False, False))
            step(2 * jj + 1, (False, False))
            return carry

        lax.fori_loop(0, i, loop_body, 0)
        step(2 * i, (True, False))
        step(2 * i + 1, (None, True))
        dq = dq_s[...] * ATT_SCALE
        dq_ref[:, 0:128] = dq[:, 0:128].astype(BF16)
        dq_ref[:, 128:256] = _rope_bwd(dq[:, 128:256], c_ref[...], sa_ref[...], sb_ref[...]).astype(BF16)

    t2 = 2 * t
    col = pl.BlockSpec((t2, HEAD_DIM), lambda h, i: (i, h))
    tab = pl.BlockSpec((t2, 128), lambda h, i: (i, 0))
    return pl.pallas_call(
        body, name="flash_bwd", grid=(HEADS, s // t2),
        in_specs=[pl.BlockSpec((None, t2, QK_PAD), lambda h, i: (h, i, 0)),
                  pl.BlockSpec((None, s, QK_PAD), lambda h, i: (h, 0, 0)),
                  pl.BlockSpec((None, s, HEAD_DIM), lambda h, i: (h, 0, 0)),
                  col, col, col, pl.BlockSpec((None, t2, 128), lambda h, i: (h, i, 0)), tab, tab, tab],
        out_specs=[pl.BlockSpec((None, t2, QK_PAD), lambda h, i: (h, i, 0)),
                   pl.BlockSpec((None, s, QK_PAD), lambda h, i: (h, 0, 0)),
                   pl.BlockSpec((None, s, HEAD_DIM), lambda h, i: (h, 0, 0)), col],
        out_shape=[jax.ShapeDtypeStruct((HEADS, s, QK_PAD), BF16), jax.ShapeDtypeStruct((HEADS, s, QK_PAD), F32),
                   jax.ShapeDtypeStruct((HEADS, s, HEAD_DIM), F32), jax.ShapeDtypeStruct((s, D_MODEL), BF16)],
        scratch_shapes=[pltpu.VMEM((t2, QK_PAD), F32), pltpu.VMEM((t2, 128), F32), pltpu.VMEM((t2, HEAD_DIM), BF16)],
        compiler_params=_params(2),
    )(q, k, v, dyb, mz, o_att, lse, *tabs)


def _mla_bwd_proj(dq, dk, dv, cqn, ckvn, ms, q_a_g, kv_a_g, wuq3, wukv3, tabs):
    s = ms.shape[0]
    tm = min(TM_MLA, s)

    def body(dq_ref, dk_ref, dv_ref, cqn_ref, ckvn_ref, ms_ref, qg_ref, kvg_ref, wuq_ref, wukv_ref,
             c_ref, sa_ref, sb_ref, dms_ref, dwuq_ref, dwukv_ref, dqg_ref, dkvg_ref):
        @pl.when(pl.program_id(0) == 0)
        def _():
            dwuq_ref[...] = jnp.zeros_like(dwuq_ref)
            dwukv_ref[...] = jnp.zeros_like(dwukv_ref)
            dqg_ref[...] = jnp.zeros_like(dqg_ref)
            dkvg_ref[...] = jnp.zeros_like(dkvg_ref)

        cqn = cqn_ref[...]
        ckvn = ckvn_ref[...]
        dcqn = jnp.zeros((tm, Q_LORA), F32)
        dckvn = jnp.zeros((tm, KV_LORA), F32)
        dkpe = jnp.zeros((tm, 128), F32)
        for h in range(HEADS):
            dqh = dq_ref[h]
            dcqn += _dot_nt(dqh, wuq_ref[h])
            dwuq_ref[h] += _dot_tn(cqn, dqh)
            dkh = dk_ref[h] * LN2
            dkvh = jnp.concatenate([dkh[:, 0:128], dv_ref[h]], axis=1).astype(BF16)
            dckvn += _dot_nt(dkvh, wukv_ref[h])
            dwukv_ref[h] += _dot_tn(ckvn, dkvh)
            dkpe += dkh[:, 128:256]
        dcq, dqg_rows = _rms_bwd(ms_ref[:, 0:Q_LORA], qg_ref[...], dcqn)
        dckv, dkvg_rows = _rms_bwd(ms_ref[:, Q_LORA:Q_LORA + KV_LORA], kvg_ref[...], dckvn)
        dqg_ref[...] += jnp.sum(dqg_rows, axis=0, keepdims=True)
        dkvg_ref[...] += jnp.sum(dkvg_rows, axis=0, keepdims=True)
        dms_ref[:, 0:Q_LORA] = dcq.astype(BF16)
        dms_ref[:, Q_LORA:Q_LORA + KV_LORA] = dckv.astype(BF16)
        dms_ref[:, Q_LORA + KV_LORA:MS_COLS] = _rope_bwd(dkpe, c_ref[...], sa_ref[...], sb_ref[...]).astype(BF16)

    tab = pl.BlockSpec((tm, 128), lambda i: (i, 0))
    wq = pl.BlockSpec((HEADS, Q_LORA, QK_PAD), lambda i: (0, 0, 0))
    wkv = pl.BlockSpec((HEADS, KV_LORA, 256), lambda i: (0, 0, 0))
    qg = pl.BlockSpec((1, Q_LORA), lambda i: (0, 0))
    kvg = pl.BlockSpec((1, KV_LORA), lambda i: (0, 0))
    return pl.pallas_call(
        body, name="mla_bwd_proj", grid=(s // tm,),
        in_specs=[pl.BlockSpec((HEADS, tm, QK_PAD), lambda i: (0, i, 0)),
                  pl.BlockSpec((HEADS, tm, QK_PAD), lambda i: (0, i, 0)),
                  pl.BlockSpec((HEADS, tm, HEAD_DIM), lambda i: (0, i, 0)),
                  pl.BlockSpec((tm, Q_LORA), lambda i: (i, 0)), pl.BlockSpec((tm, KV_LORA), lambda i: (i, 0)),
                  pl.BlockSpec((tm, MS_COLS), lambda i: (i, 0)), qg, kvg, wq, wkv, tab, tab, tab],
        out_specs=[pl.BlockSpec((tm, MS_COLS), lambda i: (i, 0)), wq, wkv, qg, kvg],
        out_shape=[jax.ShapeDtypeStruct((s, MS_COLS), BF16), jax.ShapeDtypeStruct((HEADS, Q_LORA, QK_PAD), F32),
                   jax.ShapeDtypeStruct((HEADS, KV_LORA, 256), F32),
                   jax.ShapeDtypeStruct((1, Q_LORA), F32), jax.ShapeDtypeStruct((1, KV_LORA), F32)],
        compiler_params=_params(1),
    )(dq, dk, dv, cqn, ckvn, ms, q_a_g, kv_a_g, wuq3, wukv3, *tabs)


def _merge_fused(ya, yb, glog, b_gate, x, tgt, fg, wproj):
    s = x.shape[0]
    tm = min(TM_FUSED, s)

    def body(ya_ref, yb_ref, g0_ref, g1_ref, b0_ref, b1_ref, x_ref, t_ref, fg_ref, w_ref,
             mg_ref, dx2_ref, dx2b_ref, dya_ref, dyb_ref, dgl_ref, dpa_ref, dpb_ref, loss_ref, dfg_ref, dbg_ref):
        @pl.when(pl.program_id(0) == 0)
        def _():
            loss_ref[...] = jnp.zeros_like(loss_ref)
            dfg_ref[...] = jnp.zeros_like(dfg_ref)
            dbg_ref[...] = jnp.zeros_like(dbg_ref)

        pa = _dot(ya_ref[...], w_ref[0])
        pb = _dot(yb_ref[...], w_ref[1])
        g0 = jax.nn.sigmoid(g0_ref[...] + b0_ref[...])
        g1 = jax.nn.sigmoid(g1_ref[...] + b1_ref[...])
        merged = (g0 * pa + g1 * pb).astype(BF16)
        mg_ref[...] = merged
        x2 = x_ref[...] + _dot(merged, w_ref[2])
        fg_v = fg_ref[...]
        err = _rms(x2, fg_v) - t_ref[...]
        loss_ref[...] += 0.5 * jnp.sum(jnp.mean(err * err, axis=-1, keepdims=True), axis=0, keepdims=True)
        dx2, dfg_rows = _rms_bwd(x2, fg_v, err * (1.0 / D_MODEL))
        dx2_ref[...] = dx2
        dfg_ref[...] += jnp.sum(dfg_rows, axis=0, keepdims=True)

        dx2b = dx2.astype(BF16)
        dx2b_ref[...] = dx2b
        dmg = _dot_nt(dx2b, w_ref[2])
        dpa = (dmg * g0).astype(BF16)
        dpb = (dmg * g1).astype(BF16)
        dpa_ref[...] = dpa
        dpb_ref[...] = dpb
        dgl0 = dmg * pa * g0 * (1.0 - g0)
        dgl1 = dmg * pb * g1 * (1.0 - g1)
        dgl_ref[:, 0:D_MODEL] = dgl0.astype(BF16)
        dgl_ref[:, D_MODEL:2 * D_MODEL] = dgl1.astype(BF16)
        dbg_ref[:, 0:D_MODEL] += jnp.sum(dgl0, axis=0, keepdims=True)
        dbg_ref[:, D_MODEL:2 * D_MODEL] += jnp.sum(dgl1, axis=0, keepdims=True)
        dya_ref[...] = _dot_nt(dpa, w_ref[0])
        dyb_ref[...] = _dot_nt(dpb, w_ref[1])

    row = pl.BlockSpec((tm, D_MODEL), lambda i: (i, 0))
    row1 = pl.BlockSpec((tm, D_MODEL), lambda i: (i, 1))
    row2 = pl.BlockSpec((tm, 2 * D_MODEL), lambda i: (i, 0))
    vec = pl.BlockSpec((1, D_MODEL), lambda i: (0, 0))
    vec1 = pl.BlockSpec((1, D_MODEL), lambda i: (0, 1))
    vec2 = pl.BlockSpec((1, 2 * D_MODEL), lambda i: (0, 0))
    f32_rows = jax.ShapeDtypeStruct((s, D_MODEL), F32)
    bf16_rows = jax.ShapeDtypeStruct((s, D_MODEL), BF16)
    return pl.pallas_call(
        body, name="merge_fused", grid=(s // tm,),
        in_specs=[row, row, row, row1, vec, vec1, row, row, vec, pl.BlockSpec((3, D_MODEL, D_MODEL), lambda i: (0, 0, 0))],
        out_specs=[row, row, row, row, row, row2, row, row, pl.BlockSpec((1, 128), lambda i: (0, 0)), vec, vec2],
        out_shape=[bf16_rows, f32_rows, bf16_rows, f32_rows, f32_rows, jax.ShapeDtypeStruct((s, 2 * D_MODEL), BF16),
                   bf16_rows, bf16_rows, jax.ShapeDtypeStruct((1, 128), F32), jax.ShapeDtypeStruct((1, D_MODEL), F32),
                   jax.ShapeDtypeStruct((1, 2 * D_MODEL), F32)],
        compiler_params=_params(1),
    )(ya, yb, glog, glog, b_gate, b_gate, x, tgt, fg, wproj)


def _proj_fused(x, g, w_int, r_blk):
    s = x.shape[0]
    tm = min(TM_FUSED, s)

    def body(x_ref, g_ref, w_hbm, r_ref, h_ref, hg_ref, ms_ref, mz_ref, gl_ref, or_ref, w_s, sem, send_sems, recv_sems):
        mx, my, mc = _me()
        chips = _other_chips(mx, my)
        mine, theirs = _cols(mc), _cols(1 - mc)

        def copy(k, src, dst, to):
            return pltpu.make_async_remote_copy(src_ref=src, dst_ref=dst, send_sem=send_sems.at[k],
                                                recv_sem=recv_sems.at[k], device_id=to, device_id_type=MESH_ID)

        def sends():
            return [copy(j, r_ref.at[:, mine], or_ref.at[2 * mx + my, :, mine], (cx, cy, mc))
                    for j, (cx, cy) in enumerate(chips)]

        @pl.when(pl.program_id(0) == 0)
        def _():
            for cp in sends():
                cp.start()
            cp = pltpu.make_async_copy(w_hbm, w_s, sem)
            cp.start()
            cp.wait()

        h = _rms(x_ref[...], g_ref[...]).astype(BF16)
        h_ref[...] = h
        for j in range(4):
            hg_ref[j] = _dot_nt(h, w_s[j * D_MODEL:(j + 1) * D_MODEL, :])
        ms = _dot_nt(h, w_s[4096:4096 + MS_COLS, :])
        lane = lax.broadcasted_iota(jnp.int32, ms.shape, 1)
        ms_ref[...] = jnp.where(lane < 704, ms, 0.0)
        mz_ref[...] = _dot_nt(h, w_s[4800:5824, :])
        for j in range(2):
            gl_ref[:, j * D_MODEL:(j + 1) * D_MODEL] = _dot_nt(h, w_s[5824 + j * D_MODEL:5824 + (j + 1) * D_MODEL, :])

        @pl.when(pl.program_id(0) == s // tm - 1)
        def _():
            passed = []
            for j, (cx, cy) in enumerate(chips):
                landed = or_ref.at[2 * cx + cy, :, mine]
                copy(j, landed, landed, (cx, cy, mc)).wait_recv()
                fwd = copy(3 + j, landed, landed, (mx, my, 1 - mc))
                fwd.start()
                passed.append(fwd)
            for j, (cx, cy) in enumerate(chips):
                other = or_ref.at[2 * cx + cy, :, theirs]
                copy(3 + j, other, other, (mx, my, 1 - mc)).wait_recv()
            for cp in sends() + passed:
                cp.wait_send()

    row = pl.BlockSpec((tm, D_MODEL), lambda i: (i, 0))
    outs = pl.pallas_call(
        body, name="proj_fused", grid=(s // tm,),
        in_specs=[row, pl.BlockSpec((1, D_MODEL), lambda i: (0, 0)), ANY, ANY],
        out_specs=[row, pl.BlockSpec((4, tm, D_MODEL), lambda i: (0, i, 0)), pl.BlockSpec((tm, MS_COLS), lambda i: (i, 0)),
                   row, pl.BlockSpec((tm, 2 * D_MODEL), lambda i: (i, 0)), ANY],
        out_shape=[jax.ShapeDtypeStruct((s, D_MODEL), BF16), jax.ShapeDtypeStruct((4, s, D_MODEL), F32),
                   jax.ShapeDtypeStruct((s, MS_COLS), F32), jax.ShapeDtypeStruct((s, D_MODEL), F32),
                   jax.ShapeDtypeStruct((s, 2 * D_MODEL), F32), jax.ShapeDtypeStruct((N_CHIPS,) + r_blk.shape, r_blk.dtype)],
        scratch_shapes=[pltpu.VMEM(w_int.shape, BF16), pltpu.SemaphoreType.DMA,
                        pltpu.SemaphoreType.DMA((6,)), pltpu.SemaphoreType.DMA((6,))],
        compiler_params=_params(1),
    )(x, g, w_int, r_blk)
    gr = lax.dynamic_update_slice(outs[5], r_blk[None], (2 * lax.axis_index("x") + lax.axis_index("y"), 0, 0))
    return (*outs[:5], gr)


def _dh_fused(dhg, dms, dmz, dglog, w_int, x, g, dx2, hw, hr):
    s = x.shape[0]
    tm = min(512, s)
    nw, nr = hw.shape[0] // N_CHIPS, hr.shape[0] // N_CHIPS

    def body(dhg_ref, dms_ref, dmz_ref, dgl_ref, w_hbm, x_ref, g_ref, dx2_ref, hw_ref, hr_ref,
             dx_ref, dg_ref, lw_ref, lr_ref, w_s, sem, send_sems, recv_sems):
        def scatter_copies():
            mx, my, mc = _me()
            return [pltpu.make_async_remote_copy(
                src_ref=src.at[pl.ds((2 * cx + cy) * n, n), :], dst_ref=dst.at[j], send_sem=send_sems.at[3 * a + j],
                recv_sem=recv_sems.at[3 * a + j], device_id=(cx, cy, mc), device_id_type=MESH_ID)
                for a, (src, dst, n) in enumerate([(hw_ref, lw_ref, nw), (hr_ref, lr_ref, nr)])
                for j, (cx, cy) in enumerate(_other_chips(mx, my))]

        @pl.when(pl.program_id(0) == 0)
        def _():
            for cp in scatter_copies():
                cp.start()
            dg_ref[...] = jnp.zeros_like(dg_ref)
            cp = pltpu.make_async_copy(w_hbm, w_s, sem)
            cp.start()
            cp.wait()

        dh = _dot(dms_ref[...], w_s[4096:4096 + MS_COLS, :]) + _dot(dmz_ref[...], w_s[4800:5824, :])
        for j in range(4):
            dh += _dot(dhg_ref[j], w_s[j * D_MODEL:(j + 1) * D_MODEL, :])
        for j in range(2):
            dh += _dot(dgl_ref[:, j * D_MODEL:(j + 1) * D_MODEL], w_s[5824 + j * D_MODEL:5824 + (j + 1) * D_MODEL, :])
        dx, dg_rows = _rms_bwd(x_ref[...], g_ref[...], dh)
        dx_ref[...] = dx + dx2_ref[...]
        dg_ref[...] += jnp.sum(dg_rows, axis=0, keepdims=True)

        @pl.when(pl.program_id(0) == s // tm - 1)
        def _():
            for cp in scatter_copies():
                cp.wait()

    row = pl.BlockSpec((tm, D_MODEL), lambda i: (i, 0))
    vec = pl.BlockSpec((1, D_MODEL), lambda i: (0, 0))
    return pl.pallas_call(
        body, name="dh_fused", grid=(s // tm,),
        in_specs=[pl.BlockSpec((4, tm, D_MODEL), lambda i: (0, i, 0)), pl.BlockSpec((tm, MS_COLS), lambda i: (i, 0)), row,
                  pl.BlockSpec((tm, 2 * D_MODEL), lambda i: (i, 0)), ANY, row, vec, row, ANY, ANY],
        out_specs=[row, vec, ANY, ANY],
        out_shape=[jax.ShapeDtypeStruct((s, D_MODEL), F32), jax.ShapeDtypeStruct((1, D_MODEL), F32),
                   jax.ShapeDtypeStruct((3, nw, HALF_COLS), hw.dtype), jax.ShapeDtypeStruct((3, nr, HALF_COLS), hr.dtype)],
        scratch_shapes=[pltpu.VMEM(w_int.shape, BF16), pltpu.SemaphoreType.DMA,
                        pltpu.SemaphoreType.DMA((6,)), pltpu.SemaphoreType.DMA((6,))],
        compiler_params=_params(1),
    )(dhg, dms, dmz, dglog, w_int, x, g, dx2, hw, hr)


def _local_step(x, tgt, w_int, r_blk, norm_g, b_gate, lb_logits, hg_norm_g, q_a_g, kv_a_g, fg):
    s = x.shape[0]
    tabs = _rope_tables(s)

    h, hg, ms, mz, glog, gr = _proj_fused(x, norm_g, w_int, r_blk)
    w_uq, w_ukv, wproj = _unpack_rest_weights(gr)
    wuq3 = jnp.pad(w_uq.reshape(Q_LORA, HEADS, QK_DIM).transpose(1, 0, 2), ((0, 0), (0, 0), (0, QK_PAD - QK_DIM)))
    wukv3 = w_ukv.reshape(KV_LORA, HEADS, 256).transpose(1, 0, 2)
    o_pre, ya, st0 = _hgrn_fwd(hg, lb_logits, hg_norm_g)
    q, k, v, cqn, ckvn = _mla_pre(ms, q_a_g, kv_a_g, wuq3, wukv3, tabs)
    o_att, yb, lse = _flash_fwd(q, k, v, mz)
    merged, dx2, dx2b, dya, dyb, dglog, dpa, dpb, loss, dfg, dbg = _merge_fused(ya, yb, glog, b_gate, x, tgt, fg, wproj)

    d_wout = _mm_tn(merged, dx2b, name="dw_out")
    d_wpa = _mm_tn(ya, dpa, name="dw_proj_a")
    d_wpb = _mm_tn(yb, dpb, name="dw_proj_b")
    dhg, dlb, dhgg = _hgrn_bwd(hg, o_pre, dya, st0, lb_logits, hg_norm_g)
    dq, dk, dv, dmz = _flash_bwd(q, k, v, dyb, mz, o_att, lse, tabs)
    dms, d_wuq3, d_wukv3, dqg, dkvg = _mla_bwd_proj(dq, dk, dv, cqn, ckvn, ms, q_a_g, kv_a_g, wuq3, wukv3, tabs)
    d_hg = _mm_tn(dhg, h, name="dw_in_hg")
    d_ms = _mm_tn(dms, h, name="dw_in_ms")
    d_mz = _mm_tn(dmz, h, name="dw_in_mz")
    d_gl = _mm_tn(dglog, h, name="dw_in_gate")
    d_w_int = jnp.concatenate([d_hg.reshape(4 * D_MODEL, D_MODEL), d_ms[0:704], d_mz, d_gl], axis=0)
    small = {"b_gate": dbg, "lb": dlb, "hg_norm_g": dhgg, "q_a_g": dqg, "kv_a_g": dkvg, "final_norm_g": dfg}
    dh_args = (dhg, dms, dmz, dglog, w_int, x, norm_g, dx2)
    return loss, dh_args, d_w_int, d_wuq3, d_wukv3, (d_wpa, d_wpb, d_wout), small


def _pack_rest(w_uq_b, w_ukv_b, wpa_b, wpb_b, wout_b):
    return jnp.concatenate([w_uq_b.reshape(144, D_MODEL), w_ukv_b.reshape(128, D_MODEL), wpa_b, wpb_b, wout_b], axis=0)


def _unpack_rest(p):
    return (p[0:144].reshape(Q_LORA, 384), p[144:272].reshape(KV_LORA, 512), p[272:528], p[528:784], p[784:1040])


def _pack_rest_grads(d_wuq3, d_wukv3, d_proj):
    d_wuq = d_wuq3.transpose(1, 0, 2)[:, :, 0:QK_DIM].reshape(Q_LORA, HEADS * QK_DIM)
    d_wukv = d_wukv3.transpose(1, 0, 2).reshape(KV_LORA, HEADS * 256)
    blocks = []
    for b in range(N_CHIPS):
        rows = slice(b * 256, (b + 1) * 256)
        blocks.append(_pack_rest(d_wuq[:, b * 384:(b + 1) * 384], d_wukv[:, b * 512:(b + 1) * 512],
                                 d_proj[0][rows], d_proj[1][rows], d_proj[2][rows]))
    return jnp.stack(blocks, axis=0)


def _unpack_rest_weights(g):
    parts = [_unpack_rest(g[b]) for b in range(N_CHIPS)]
    w_uq, w_ukv = (jnp.concatenate([p[n] for p in parts], axis=1) for n in range(2))
    wproj = jnp.stack([jnp.concatenate([p[n] for p in parts], axis=0) for n in range(2, 5)], axis=0)
    return w_uq, w_ukv, wproj


MESH_ID = pl.DeviceIdType.MESH
ANY = pl.BlockSpec(memory_space=pl.ANY)
HALF_COLS = D_MODEL // 2


def _me():
    return lax.axis_index("x"), lax.axis_index("y"), lax.axis_index("c")


def _other_chips(x, y):
    return [(1 - x, y), (x, 1 - y), (1 - x, 1 - y)]


def _cols(c):
    return pl.ds(c * HALF_COLS, HALF_COLS)


RELAY_TOP = 992


def _gather_weights(w_blk):
    bot = W_IN_BLK - RELAY_TOP

    def body(w_ref, ow_ref, send_sems, recv_sems):
        x, y, c = _me()
        me, xn, yn, dg = 2 * x + y, 2 * (1 - x) + y, 2 * x + (1 - y), 2 * (1 - x) + (1 - y)
        to_x, to_y, to_sib = (1 - x, y, c), (x, 1 - y, c), (x, y, 1 - c)
        mine, theirs = _cols(c), _cols(1 - c)
        top, low = pl.ds(0, RELAY_TOP), pl.ds(RELAY_TOP, bot)

        def copy(k, src, dst, to):
            return pltpu.make_async_remote_copy(src_ref=src, dst_ref=dst, send_sem=send_sems.at[k],
                                                recv_sem=recv_sems.at[k], device_id=to, device_id_type=MESH_ID)

        def same(k, ref, to):
            return copy(k, ref, ref, to)

        own = [copy(0, w_ref.at[:, mine], ow_ref.at[me, :, mine], to_x),
               copy(1, w_ref.at[:, mine], ow_ref.at[me, :, mine], to_y)]
        for cp in own:
            cp.start()
        from_x, from_y = ow_ref.at[xn, :, mine], ow_ref.at[yn, :, mine]
        same(0, from_x, to_x).wait_recv()
        relay_y = same(2, ow_ref.at[xn, top, mine], to_y)
        pass_x = same(4, from_x, to_sib)
        relay_y.start()
        pass_x.start()
        same(1, from_y, to_y).wait_recv()
        relay_x = same(3, ow_ref.at[yn, low, mine], to_x)
        pass_y = same(5, from_y, to_sib)
        relay_x.start()
        pass_y.start()
        same(2, ow_ref.at[dg, top, mine], to_y).wait_recv()
        same(3, ow_ref.at[dg, low, mine], to_x).wait_recv()
        pass_d = same(6, ow_ref.at[dg, :, mine], to_sib)
        pass_d.start()
        for k, blk in ((4, xn), (5, yn), (6, dg)):
            same(k, ow_ref.at[blk, :, theirs], to_sib).wait_recv()
        for cp in own + [relay_y, relay_x, pass_x, pass_y, pass_d]:
            cp.wait_send()

    gw = pl.pallas_call(
        body, name="gather_weights", in_specs=[ANY], out_specs=ANY,
        out_shape=jax.ShapeDtypeStruct((N_CHIPS,) + w_blk.shape, w_blk.dtype),
        scratch_shapes=[pltpu.SemaphoreType.DMA((7,)), pltpu.SemaphoreType.DMA((7,))],
    )(w_blk)
    return lax.dynamic_update_slice(gw, w_blk[None], (2 * lax.axis_index("x") + lax.axis_index("y"), 0, 0))


def _swap_halves(gw, gr):
    def body(gw_ref, gr_ref, lw_ref, lr_ref, send_sems, recv_sems):
        x, y, c = _me()
        cps = [pltpu.make_async_remote_copy(
            src_ref=src, dst_ref=dst, send_sem=send_sems.at[a], recv_sem=recv_sems.at[a],
            device_id=(x, y, 1 - c), device_id_type=MESH_ID)
            for a, (src, dst) in enumerate([(gw_ref.at[:, _cols(1 - c)], lw_ref),
                                            (gr_ref.at[:, :, _cols(1 - c)], lr_ref)])]
        for cp in cps:
            cp.start()
        for cp in cps:
            cp.wait()

    return pl.pallas_call(
        body, name="grad_swap_halves", in_specs=[ANY, ANY], out_specs=[ANY, ANY],
        out_shape=[jax.ShapeDtypeStruct((gw.shape[0], HALF_COLS), gw.dtype),
                   jax.ShapeDtypeStruct(gr.shape[:2] + (HALF_COLS,), gr.dtype)],
        scratch_shapes=[pltpu.SemaphoreType.DMA((2,)), pltpu.SemaphoreType.DMA((2,))],
    )(gw, gr)


def _swap_reduced(rw, rr):
    def body(rw_ref, rr_ref, ow_ref, or_ref, send_sems, recv_sems):
        x, y, c = _me()
        cps = [pltpu.make_async_remote_copy(
            src_ref=src, dst_ref=dst, send_sem=send_sems.at[a], recv_sem=recv_sems.at[a],
            device_id=(x, y, 1 - c), device_id_type=MESH_ID)
            for a, (src, dst) in enumerate([(rw_ref, ow_ref), (rr_ref, or_ref)])]
        for cp in cps:
            cp.start()
        for cp in cps:
            cp.wait()

    return pl.pallas_call(
        body, name="grad_swap_reduced", in_specs=[ANY, ANY], out_specs=[ANY, ANY],
        out_shape=[jax.ShapeDtypeStruct(rw.shape, rw.dtype), jax.ShapeDtypeStruct(rr.shape, rr.dtype)],
        scratch_shapes=[pltpu.SemaphoreType.DMA((2,)), pltpu.SemaphoreType.DMA((2,))],
    )(rw, rr)


def _join_cols(mine, theirs):
    first = lax.axis_index("c") == 0
    return jnp.concatenate([jnp.where(first, mine, theirs), jnp.where(first, theirs, mine)], axis=1)


def _gather_small(vec):
    def body(v_ref, out_ref, send_sems, recv_sems, local_sem):
        x, y, c = _me()
        my_id = 4 * x + 2 * y + c
        mine = pltpu.make_async_copy(v_ref, out_ref.at[my_id], local_sem)
        mine.start()
        cps = []
        for r in range(1, N_DEV):
            peer = (x ^ (r >> 2), y ^ ((r >> 1) & 1), c ^ (r & 1))
            cps.append(pltpu.make_async_remote_copy(
                src_ref=v_ref, dst_ref=out_ref.at[my_id], send_sem=send_sems.at[r - 1],
                recv_sem=recv_sems.at[r - 1], device_id=peer, device_id_type=MESH_ID))
        for cp in cps:
            cp.start()
        for cp in cps:
            cp.wait()
        mine.wait()

    return pl.pallas_call(
        body, name="gather_small", in_specs=[ANY], out_specs=ANY,
        out_shape=jax.ShapeDtypeStruct((N_DEV, 1, SMALL_COLS), vec.dtype),
        scratch_shapes=[pltpu.SemaphoreType.DMA((N_DEV - 1,)), pltpu.SemaphoreType.DMA((N_DEV - 1,)),
                        pltpu.SemaphoreType.DMA],
    )(vec)


def _add_cores(c_idx, g, landed, *, tm, name):
    r = g.shape[0]

    def body(c_ref, g_ref, l_ref, o32_ref, o16_ref):
        acc = g_ref[...] + l_ref[...]
        o32_ref[...] = acc
        o16_ref[...] = acc.astype(BF16)

    half = pl.BlockSpec((tm, HALF_COLS), lambda i, c_ref: (i, 0))
    grid_spec = pltpu.PrefetchScalarGridSpec(
        num_scalar_prefetch=1, grid=(r // tm,),
        in_specs=[pl.BlockSpec((tm, HALF_COLS), lambda i, c_ref: (i, c_ref[0])), half], out_specs=[half, half])
    return pl.pallas_call(
        body, name=name, grid_spec=grid_spec,
        out_shape=[jax.ShapeDtypeStruct((r, HALF_COLS), F32), jax.ShapeDtypeStruct((r, HALF_COLS), BF16)],
        compiler_params=_params(1),
    )(c_idx, g, landed)


def _add_chips(chip_idx, h32, landed, *, tm, name):
    n = landed.shape[1]
    per = n // tm

    def body(chip_ref, h_ref, l_ref, o_ref):
        acc = h_ref[...]
        for j in range(3):
            acc = acc + l_ref[j].astype(F32)
        o_ref[...] = acc

    grid_spec = pltpu.PrefetchScalarGridSpec(
        num_scalar_prefetch=1, grid=(per,),
        in_specs=[pl.BlockSpec((tm, HALF_COLS), lambda i, chip_ref: (chip_ref[0] * per + i, 0)),
                  pl.BlockSpec((3, tm, HALF_COLS), lambda i, chip_ref: (0, i, 0))],
        out_specs=pl.BlockSpec((tm, HALF_COLS), lambda i, chip_ref: (i, 0)))
    return pl.pallas_call(
        body, name=name, grid_spec=grid_spec, out_shape=jax.ShapeDtypeStruct((n, HALF_COLS), F32),
        compiler_params=_params(1),
    )(chip_idx, h32, landed)


def _pack_small(small, lb_logits, loss):
    def body(ng_ref, bg_ref, dlb_ref, lbl_ref, hgg_ref, qg_ref, kvg_ref, fg_ref, loss_ref, out_ref):
        out_ref[...] = jnp.zeros_like(out_ref)
        out_ref[:, 0:1024] = ng_ref[...]
        out_ref[:, 1024:3072] = bg_ref[...]
        _, p0p1 = _lower_bound(lbl_ref[...])
        dl0 = dlb_ref[...] * p0p1
        out_ref[:, 3072:4096] = dl0
        out_ref[:, 4096:5120] = -dl0
        hgg = hgg_ref[0]
        for h in range(1, HEADS):
            hgg = hgg + hgg_ref[h]
        out_ref[:, 5120:5248] = hgg
        out_ref[:, 5248:5632] = qg_ref[...]
        out_ref[:, 5632:5888] = kvg_ref[...]
        out_ref[:, 5888:6912] = fg_ref[...]
        out_ref[:, 6912:7040] = loss_ref[...]

    return pl.pallas_call(
        body, name="pack_small", out_shape=jax.ShapeDtypeStruct((1, SMALL_COLS), F32),
    )(small["norm_g"], small["b_gate"], small["lb"], lb_logits, small["hg_norm_g"], small["q_a_g"],
      small["kv_a_g"], small["final_norm_g"], loss)


def _adamw_math(w, g, m, v):
    nm = ADAM_B1 * m + (1.0 - ADAM_B1) * g
    nv = ADAM_B2 * v + (1.0 - ADAM_B2) * (g * g)
    m_hat = nm / (1.0 - ADAM_B1 ** ADAM_STEP)
    v_hat = nv / (1.0 - ADAM_B2 ** ADAM_STEP)
    return -ADAM_LR * (m_hat / (jnp.sqrt(v_hat) + ADAM_EPS) + ADAM_WD * w), nm, nv


def _adamw(w, g, m, v, *, name, tm):
    r, cols = w.shape

    def body(w_ref, g_ref, m_ref, v_ref, d_ref, nm_ref, nv_ref):
        d_ref[...], nm_ref[...], nv_ref[...] = _adamw_math(w_ref[...], g_ref[...], m_ref[...], v_ref[...])

    row = pl.BlockSpec((tm, cols), lambda i: (i, 0))
    shp = jax.ShapeDtypeStruct((r, cols), F32)
    return pl.pallas_call(
        body, name=name, grid=(r // tm,), in_specs=[row] * 4, out_specs=[row] * 3, out_shape=[shp] * 3,
        compiler_params=_params(1),
    )(w, g, m, v)


SMALL_SLOTS = (("norm_g", (0,)), ("b_gate", (1024,)), ("lb_logits", (3072, 4096)), ("hg_norm_g", (5120,)),
               ("q_a_g", (5248,)), ("kv_a_g", (5632,)), ("final_norm_g", (5888,)))
LOSS_SLOT = 6912


def _small_update(gathered, ws, ms, vs):
    n = len(SMALL_SLOTS)

    def body(*refs):
        g_ref = refs[0]
        w_refs, m_refs, v_refs = refs[1:1 + n], refs[1 + n:1 + 2 * n], refs[1 + 2 * n:1 + 3 * n]
        outs = refs[1 + 3 * n:]
        loss_ref = outs[0]
        g_out, d_out, nm_out, nv_out = (outs[1 + k * n:1 + (k + 1) * n] for k in range(4))
        total = g_ref[0]
        for dev in range(1, N_DEV):
            total = total + g_ref[dev]
        loss_ref[...] = total[:, LOSS_SLOT:LOSS_SLOT + 128]
        for p, (_, offsets) in enumerate(SMALL_SLOTS):
            cols = w_refs[p].shape[1]
            for r, off in enumerate(offsets):
                rows = slice(r, r + 1)
                g = total[:, off:off + cols]
                g_out[p][rows, :] = g
                d_out[p][rows, :], nm_out[p][rows, :], nv_out[p][rows, :] = _adamw_math(
                    w_refs[p][rows, :], g, m_refs[p][rows, :], v_refs[p][rows, :])

    shapes = [jax.ShapeDtypeStruct(w.shape, F32) for w in ws]
    res = pl.pallas_call(
        body, name="small_update", out_shape=[jax.ShapeDtypeStruct((1, 128), F32)] + shapes * 4,
    )(gathered, *ws, *ms, *vs)
    return res[0], res[1:1 + n], res[1 + n:1 + 2 * n], res[1 + 2 * n:1 + 3 * n], res[1 + 3 * n:1 + 4 * n]


def kernel(x, norm_g, w_in, b_gate, lb_logits, hg_norm_g, q_a_g, w_uq, kv_a_g, w_ukv, w_proj_a, w_proj_b, w_out, final_norm_g, loss_target, m_norm_g, m_w_in, m_b_gate, m_lb_logits, m_hg_norm_g, m_q_a_g, m_w_uq, m_kv_a_g, m_w_ukv, m_w_proj_a, m_w_proj_b, m_w_out, m_final_norm_g, v_norm_g, v_w_in, v_b_gate, v_lb_logits, v_hg_norm_g, v_q_a_g, v_w_uq, v_kv_a_g, v_w_ukv, v_w_proj_a, v_w_proj_b, v_w_out, v_final_norm_g):
    c_idx = lax.axis_index("c").astype(jnp.int32).reshape(1)
    chip_idx = (2 * lax.axis_index("x") + lax.axis_index("y")).astype(jnp.int32).reshape(1)

    w_blk = w_in[0].T.astype(BF16)
    r_blk = _pack_rest(w_uq[0], w_ukv[0], w_proj_a[0], w_proj_b[0], w_out[0]).astype(BF16)
    gw = _gather_weights(w_blk)

    loss, dh_args, d_w_int, d_wuq3, d_wukv3, d_proj, small = _local_step(
        x[0], loss_target[0], gw.reshape(W_IN_COLS, D_MODEL), r_blk,
        norm_g, b_gate, lb_logits, hg_norm_g, q_a_g, kv_a_g, final_norm_g.reshape(1, D_MODEL))

    d_rest = _pack_rest_grads(d_wuq3, d_wukv3, d_proj)
    lw, lr = _swap_halves(d_w_int, d_rest)
    hw32, hw16 = _add_cores(c_idx, d_w_int, lw, tm=656, name="grad_add_cores_w")
    hr32, hr16 = _add_cores(c_idx, d_rest.reshape(N_CHIPS * REST_ROWS, D_MODEL), lr.reshape(N_CHIPS * REST_ROWS, HALF_COLS),
                            tm=REST_ROWS, name="grad_add_cores_r")
    grad_x, small["norm_g"], landed_w, landed_r = _dh_fused(*dh_args, hw16, hr16)
    rw = _add_chips(chip_idx, hw32, landed_w, tm=656, name="grad_add_chips_w")
    rr = _add_chips(chip_idx, hr32, landed_r, tm=208, name="grad_add_chips_r")
    tw, tr = _swap_reduced(rw, rr)
    g_w_in = _join_cols(rw, tw).T
    g_rest = _join_cols(rr, tr)
    g_uq, g_ukv, g_pa, g_pb, g_out = _unpack_rest(g_rest)

    small_all = _gather_small(_pack_small(small, lb_logits, loss))

    upd = {
        "w_in": _adamw(w_in[0], g_w_in, m_w_in[0], v_w_in[0], name="adamw_w_in", tm=128),
        "w_uq": _adamw(w_uq[0], g_uq, m_w_uq[0], v_w_uq[0], name="adamw_w_uq", tm=Q_LORA),
        "w_ukv": _adamw(w_ukv[0], g_ukv, m_w_ukv[0], v_w_ukv[0], name="adamw_w_ukv", tm=KV_LORA),
        "w_proj_a": _adamw(w_proj_a[0], g_pa, m_w_proj_a[0], v_w_proj_a[0], name="adamw_w_proj_a", tm=256),
        "w_proj_b": _adamw(w_proj_b[0], g_pb, m_w_proj_b[0], v_w_proj_b[0], name="adamw_w_proj_b", tm=256),
        "w_out": _adamw(w_out[0], g_out, m_w_out[0], v_w_out[0], name="adamw_w_out", tm=256),
    }
    loss_vec, *small_sets = _small_update(
        small_all,
        [norm_g, b_gate, lb_logits, hg_norm_g, q_a_g, kv_a_g, final_norm_g.reshape(1, D_MODEL)],
        [m_norm_g, m_b_gate, m_lb_logits, m_hg_norm_g, m_q_a_g, m_kv_a_g, m_final_norm_g.reshape(1, D_MODEL)],
        [v_norm_g, v_b_gate, v_lb_logits, v_hg_norm_g, v_q_a_g, v_kv_a_g, v_final_norm_g.reshape(1, D_MODEL)])

    def outputs(big, small_set):
        s_ng, s_bg, s_lb, s_hg, s_qg, s_kvg, s_fg = small_set
        return (s_ng, big["w_in"][None], s_bg, s_lb, s_hg, s_qg, big["w_uq"][None], s_kvg, big["w_ukv"][None],
                big["w_proj_a"][None], big["w_proj_b"][None], big["w_out"][None], s_fg.reshape(D_MODEL))

    grads = {"w_in": g_w_in, "w_uq": g_uq, "w_ukv": g_ukv, "w_proj_a": g_pa, "w_proj_b": g_pb, "w_out": g_out}
    return (loss_vec[0, 0], grad_x[None], *outputs(grads, small_sets[0]),
            *(o for k in range(3) for o in outputs({n: u[k] for n, u in upd.items()}, small_sets[1 + k])))
```

```python
import functools

import jax
import jax.numpy as jnp
from jax import lax
from jax.experimental import pallas as pl
from jax.experimental.pallas import tpu as pltpu

F32 = jnp.float32
BF16 = jnp.bfloat16

D_MODEL = 1024
HEADS = 8
HEAD_DIM = 128
HG_CHUNK = 32
CHUNK_SHIFT = 5
HEAD_SHIFT = 7
QK_NOPE = 128
QK_ROPE = 64
QK_DIM = QK_NOPE + QK_ROPE
QK_PAD = 256
Q_LORA = 384
KV_LORA = 256
MS_COLS = 768
ROPE_THETA = 10000.0
EPS = 1e-6
ATT_SCALE = QK_DIM ** -0.5
LOG2E = 1.4426950408889634
LN2 = 0.6931471805599453
Q_PRESCALE = ATT_SCALE * LOG2E

ADAM_LR = 0.001
ADAM_B1 = 0.9
ADAM_B2 = 0.999
ADAM_EPS = 1e-08
ADAM_WD = 0.01
ADAM_STEP = 10

N_CHIPS = 4
N_DEV = 8
W_IN_COLS = 7872
W_IN_BLK = W_IN_COLS // N_CHIPS
REST_ROWS = 144 + 128 + 3 * 256
SMALL_COLS = 7168

TM_MM = 1024
TM_FUSED = 256
TM_MLA = 512
HG_ROWS = 128
TQ = 512
FLASH_HEADS = 2
HG_HEADS = 8
VMEM_LIMIT = 56 * 1024 * 1024


def _dot(a, b):
    return lax.dot_general(a, b, (((1,), (0,)), ((), ())), preferred_element_type=F32)


def _dot_nt(a, b):
    return lax.dot_general(a, b, (((1,), (1,)), ((), ())), preferred_element_type=F32)


def _dot_tn(a, b):
    return lax.dot_general(a, b, (((0,), (0,)), ((), ())), preferred_element_type=F32)


def _params(n_axes):
    return pltpu.CompilerParams(dimension_semantics=("arbitrary",) * n_axes, vmem_limit_bytes=VMEM_LIMIT)


def _rms(x, g):
    r = lax.rsqrt(jnp.mean(x * x, axis=-1, keepdims=True) + EPS)
    return x * r * g


def _rms_bwd(x, g, dy):
    r = lax.rsqrt(jnp.mean(x * x, axis=-1, keepdims=True) + EPS)
    xh = x * r
    dyg = dy * g
    dx = r * (dyg - xh * jnp.mean(dyg * xh, axis=-1, keepdims=True))
    return dx, dy * xh


def _silu_parts(z):
    s = jax.nn.sigmoid(z)
    return z * s, s * (1.0 + z * (1.0 - s))


def _rope(x, c, sa, sb):
    return x * c + pltpu.roll(x, 32, 1) * sa + pltpu.roll(x, 96, 1) * sb


def _rope_bwd(dy, c, sa, sb):
    return dy * c + pltpu.roll(dy * sa, 96, 1) + pltpu.roll(dy * sb, 32, 1)


def _rope_tables(seq):
    inv = ROPE_THETA ** (-jnp.arange(0, QK_ROPE, 2, dtype=F32) / QK_ROPE)
    ang = jnp.arange(seq, dtype=F32)[:, None] * inv[None, :]
    cos, sin = jnp.cos(ang), jnp.sin(ang)
    z32 = jnp.zeros_like(cos)
    z64 = jnp.zeros((seq, 64), F32)
    c = jnp.concatenate([cos, cos, z64], axis=1)
    sa = jnp.concatenate([z32, sin, z64], axis=1)
    sb = jnp.concatenate([-sin, z32, z64], axis=1)
    return c, sa, sb


def _mm_tn(a, b, *, name, tm=TM_MM, tn=1024):
    flat = a.ndim == 2
    if flat:
        a = a[None]
    g, m, k = a.shape
    n = b.shape[1]
    tm, tn = min(tm, m), min(tn, n)
    assert m % tm == 0 and n % tn == 0

    def body(a_ref, b_ref, o_ref):
        @pl.when(pl.program_id(2) == 0)
        def _():
            o_ref[...] = jnp.zeros_like(o_ref)

        o_ref[...] += _dot_tn(a_ref[...], b_ref[...])

    out = pl.pallas_call(
        body, name=name, grid=(g, n // tn, m // tm),
        in_specs=[pl.BlockSpec((None, tm, k), lambda s, j, i: (s, i, 0)),
                  pl.BlockSpec((tm, tn), lambda s, j, i: (i, j))],
        out_specs=pl.BlockSpec((None, k, tn), lambda s, j, i: (s, 0, j)),
        out_shape=jax.ShapeDtypeStruct((g, k, n), F32), compiler_params=_params(3),
    )(a, b)
    return out[0] if flat else out


def _chunk_rows(rows):
    return lax.broadcasted_iota(jnp.int32, (rows, HEAD_DIM), 0) & (HG_CHUNK - 1)


def _chunk_cumsum(x, rows):
    pos = _chunk_rows(rows)
    shift = 1
    while shift < HG_CHUNK:
        x = x + jnp.where(pos >= shift, pltpu.roll(x, shift, 0), 0.0)
        shift *= 2
    return x


def _chunk_revcumsum(x, rows):
    pos = _chunk_rows(rows)
    shift = 1
    while shift < HG_CHUNK:
        x = x + jnp.where(pos + shift < HG_CHUNK, pltpu.roll(x, rows - shift, 0), 0.0)
        shift *= 2
    return x


def _chunk_last(x, rows):
    x3 = x.reshape(rows // HG_CHUNK, HG_CHUNK, HEAD_DIM)
    return jnp.broadcast_to(x3[:, HG_CHUNK - 1:HG_CHUNK, :], x3.shape).reshape(rows, HEAD_DIM)


def _lower_bound(lbl):
    mx = jnp.maximum(lbl[0:1, :], lbl[1:2, :])
    e0 = jnp.exp(lbl[0:1, :] - mx)
    e1 = jnp.exp(lbl[1:2, :] - mx)
    p0 = e0 / (e0 + e1)
    return p0, p0 * (e1 / (e0 + e1))


def _hg_masks(rows, nch, tmask_s, bdmask_s):
    r = lax.broadcasted_iota(jnp.int32, (rows, rows), 0)
    c = lax.broadcasted_iota(jnp.int32, (rows, rows), 1)
    tmask_s[...] = jnp.where(((r >> CHUNK_SHIFT) == (c >> CHUNK_SHIFT)) & (r >= c), 1.0, 0.0)
    r = lax.broadcasted_iota(jnp.int32, (rows, nch * HEAD_DIM), 0)
    c = lax.broadcasted_iota(jnp.int32, (rows, nch * HEAD_DIM), 1)
    bdmask_s[...] = jnp.where((r >> CHUNK_SHIFT) == (c >> HEAD_SHIFT), 1.0, 0.0).astype(BF16)


def _block_diag(x, nch, bdmask):
    return jnp.tile(x, (1, nch)) * bdmask


def _hgrn_fwd(hg, lb_logits, norm_g):
    s = hg.shape[1]
    rows = min(HG_ROWS, s)
    nblk = s // rows
    nch = rows // HG_CHUNK

    def body(hg_ref, lbl_ref, g_ref, o_ref, ya_ref, st0_ref, st_s, stall_s, tmask_s, bdmask_s):
        @pl.when(pl.program_id(1) == 0)
        def _():
            st_s[...] = jnp.zeros_like(st_s)
            _hg_masks(rows, nch, tmask_s, bdmask_s)

        bdmask = bdmask_s[...]
        tmask = tmask_s[...] > 0.5
        for hh in range(HG_HEADS):
            hc = slice(hh * HEAD_DIM, (hh + 1) * HEAD_DIM)
            hq = hg_ref[0, :, hc]
            hf = hg_ref[1, :, hc]
            hi = hg_ref[2, :, hc]
            hz = hg_ref[3, :, hc]
            lb, _ = _lower_bound(lbl_ref[:, hc])
            f = lb + (1.0 - lb) * jax.nn.sigmoid(hf)
            q = hq * jax.nn.sigmoid(hq)
            k = 1.0 - f
            logf = jnp.log(f)
            b = _chunk_cumsum(logf, rows)
            q_in = (q * jnp.exp(b)).astype(BF16)
            k_in = (k * jnp.exp(-b)).astype(BF16)
            k_out = (k * jnp.exp(_chunk_last(b, rows) - b)).astype(BF16)
            vb = hi.astype(BF16)

            sc = jnp.where(tmask, _dot_nt(q_in, k_in), 0.0)
            o_intra = _dot(sc.astype(BF16), vb)
            kvt = _dot_tn(vb, _block_diag(k_out, nch, bdmask))
            st = st_s[hh]
            st0_ref[hh] = st
            for c in range(nch):
                cols = slice(c * HEAD_DIM, (c + 1) * HEAD_DIM)
                last = (c + 1) * HG_CHUNK - 1
                stall_s[hh, :, cols] = st.astype(BF16)
                st = st * jnp.exp(b[last:last + 1, :]) + kvt[:, cols]
            st_s[hh] = st
            o = o_intra + _dot_nt(_block_diag(q_in, nch, bdmask), stall_s[hh])
            o_ref[:, hc] = o
            silu_z, _ = _silu_parts(hz)
            ya_ref[:, hc] = (_rms(o, g_ref[...]) * silu_z).astype(BF16)

    nh = HG_HEADS
    return pl.pallas_call(
        body, name="hgrn_fwd", grid=(HEADS // nh, nblk),
        in_specs=[pl.BlockSpec((4, rows, nh * HEAD_DIM), lambda h, i: (0, i, h)),
                  pl.BlockSpec((2, nh * HEAD_DIM), lambda h, i: (0, h)),
                  pl.BlockSpec((1, HEAD_DIM), lambda h, i: (0, 0))],
        out_specs=[pl.BlockSpec((rows, nh * HEAD_DIM), lambda h, i: (i, h)),
                   pl.BlockSpec((rows, nh * HEAD_DIM), lambda h, i: (i, h)),
                   pl.BlockSpec((nh, None, HEAD_DIM, HEAD_DIM), lambda h, i: (h, i, 0, 0))],
        out_shape=[jax.ShapeDtypeStruct((s, D_MODEL), F32), jax.ShapeDtypeStruct((s, D_MODEL), BF16),
                   jax.ShapeDtypeStruct((HEADS, nblk, HEAD_DIM, HEAD_DIM), F32)],
        scratch_shapes=[pltpu.VMEM((nh, HEAD_DIM, HEAD_DIM), F32), pltpu.VMEM((nh, HEAD_DIM, nch * HEAD_DIM), BF16),
                        pltpu.VMEM((rows, rows), F32), pltpu.VMEM((rows, nch * HEAD_DIM), BF16)],
        compiler_params=_params(2),
    )(hg, lb_logits, norm_g)


def _hgrn_bwd(hg, o_pre, dya, st0, lb_logits, norm_g):
    s = hg.shape[1]
    rows = min(HG_ROWS, s)
    nblk = s // rows
    nch = rows // HG_CHUNK

    def body(hg_ref, o_ref, dya_ref, st0_ref, lbl_ref, g_ref, dhg_ref, dlb_ref, dg_ref,
             dst_s, stp_s, stp_rows_s, dst_rows_s, dst_lane_s, dbl_s, tmask_s, bdmask_s):
        @pl.when(pl.program_id(1) == 0)
        def _():
            dst_s[...] = jnp.zeros_like(dst_s)
            dlb_ref[...] = jnp.zeros_like(dlb_ref)
            dg_ref[...] = jnp.zeros_like(dg_ref)
            _hg_masks(rows, nch, tmask_s, bdmask_s)

        bdmask = bdmask_s[...]
        tmask = tmask_s[...] > 0.5
        g = g_ref[...]
        for hh in range(HG_HEADS):
            hc = slice(hh * HEAD_DIM, (hh + 1) * HEAD_DIM)
            hq = hg_ref[0, :, hc]
            hf = hg_ref[1, :, hc]
            hi = hg_ref[2, :, hc]
            hz = hg_ref[3, :, hc]
            lb, _ = _lower_bound(lbl_ref[:, hc])
            sg = jax.nn.sigmoid(hf)
            f = lb + (1.0 - lb) * sg
            q, dsilu_q = _silu_parts(hq)
            k = 1.0 - f
            logf = jnp.log(f)
            b = _chunk_cumsum(logf, rows)
            eb = jnp.exp(b)
            enb = jnp.exp(-b)
            ebl = jnp.exp(_chunk_last(b, rows) - b)
            q_in32 = q * eb
            k_in32 = k * enb
            k_out32 = k * ebl
            q_in = q_in32.astype(BF16)
            k_in = k_in32.astype(BF16)
            k_out = k_out32.astype(BF16)
            vb = hi.astype(BF16)
            kbd = _block_diag(k_out, nch, bdmask)
            qbd = _block_diag(q_in, nch, bdmask)
            decs = [jnp.exp(b[(c + 1) * HG_CHUNK - 1:(c + 1) * HG_CHUNK, :]) for c in range(nch)]

            kvt = _dot_tn(vb, kbd)
            st = st0_ref[hh]
            for c in range(nch):
                stp_s[hh, c] = st
                stp_rows_s[hh, c * HEAD_DIM:(c + 1) * HEAD_DIM, :] = st.astype(BF16)
                st = st * decs[c] + kvt[:, c * HEAD_DIM:(c + 1) * HEAD_DIM]

            o = o_ref[:, hc]
            rstd = lax.rsqrt(jnp.mean(o * o, axis=-1, keepdims=True) + EPS)
            oh = o * rstd
            silu_z, dsilu_z = _silu_parts(hz)
            dya_v = dya_ref[:, hc]
            dn = dya_v * silu_z
            dhz = dya_v * (oh * g) * dsilu_z
            dg_ref[hh] += jnp.sum(dn * oh, axis=0, keepdims=True)
            doh = dn * g
            do = (rstd * (doh - oh * jnp.mean(doh * oh, axis=-1, keepdims=True))).astype(BF16)

            dq_all = _dot_tn(do, qbd)
            dst = dst_s[hh]
            ddecs = [None] * nch
            for c in reversed(range(nch)):
                dstb = dst.astype(BF16)
                dst_lane_s[hh, :, c * HEAD_DIM:(c + 1) * HEAD_DIM] = dstb
                dst_rows_s[hh, c * HEAD_DIM:(c + 1) * HEAD_DIM, :] = dstb
                ddecs[c] = jnp.sum(dst * stp_s[hh, c], axis=0, keepdims=True) * decs[c]
                dst = dst * decs[c] + dq_all[:, c * HEAD_DIM:(c + 1) * HEAD_DIM]
            dst_s[hh] = dst

            sc = jnp.where(tmask, _dot_nt(q_in, k_in), 0.0).astype(BF16)
            dkout = _dot(_block_diag(vb, nch, bdmask), dst_rows_s[hh])
            dv = _dot_nt(kbd, dst_lane_s[hh]) + _dot_tn(sc, do)
            dsc = jnp.where(tmask, _dot_nt(do, vb), 0.0).astype(BF16)
            dqin = _dot(dsc, k_in) + _dot(_block_diag(do, nch, bdmask), stp_rows_s[hh])
            dkin = _dot_tn(dsc, q_in)

            dko = dkout * k_out32
            for c in range(nch):
                sl = slice(c * HG_CHUNK, (c + 1) * HG_CHUNK)
                dbl = jnp.sum(dko[sl], axis=0, keepdims=True) + ddecs[c]
                dbl_s[hh, sl, :] = jnp.broadcast_to(dbl, (HG_CHUNK, HEAD_DIM))
            dq = dqin * eb
            dk = dkin * enb + dkout * ebl
            db = dqin * q_in32 - dkin * k_in32 - dko
            dlogf = _chunk_revcumsum(db, rows) + dbl_s[hh]
            df = dlogf / f - dk
            dlb_ref[:, hc] += jnp.sum(df * (1.0 - sg), axis=0, keepdims=True)
            dhg_ref[0, :, hc] = (dq * dsilu_q).astype(BF16)
            dhg_ref[1, :, hc] = (df * (1.0 - lb) * sg * (1.0 - sg)).astype(BF16)
            dhg_ref[2, :, hc] = dv.astype(BF16)
            dhg_ref[3, :, hc] = dhz.astype(BF16)

    last = nblk - 1
    nh = HG_HEADS
    wide = nh * HEAD_DIM
    return pl.pallas_call(
        body, name="hgrn_bwd", grid=(HEADS // nh, nblk),
        in_specs=[pl.BlockSpec((4, rows, wide), lambda h, i: (0, last - i, h)),
                  pl.BlockSpec((rows, wide), lambda h, i: (last - i, h)),
                  pl.BlockSpec((rows, wide), lambda h, i: (last - i, h)),
                  pl.BlockSpec((nh, None, HEAD_DIM, HEAD_DIM), lambda h, i: (h, last - i, 0, 0)),
                  pl.BlockSpec((2, wide), lambda h, i: (0, h)),
                  pl.BlockSpec((1, HEAD_DIM), lambda h, i: (0, 0))],
        out_specs=[pl.BlockSpec((4, rows, wide), lambda h, i: (0, last - i, h)),
                   pl.BlockSpec((1, wide), lambda h, i: (0, h)),
                   pl.BlockSpec((nh, 1, HEAD_DIM), lambda h, i: (h, 0, 0))],
        out_shape=[jax.ShapeDtypeStruct((4, s, D_MODEL), BF16), jax.ShapeDtypeStruct((1, D_MODEL), F32),
                   jax.ShapeDtypeStruct((HEADS, 1, HEAD_DIM), F32)],
        scratch_shapes=[pltpu.VMEM((nh, HEAD_DIM, HEAD_DIM), F32), pltpu.VMEM((nh, nch, HEAD_DIM, HEAD_DIM), F32),
                        pltpu.VMEM((nh, nch * HEAD_DIM, HEAD_DIM), BF16), pltpu.VMEM((nh, nch * HEAD_DIM, HEAD_DIM), BF16),
                        pltpu.VMEM((nh, HEAD_DIM, nch * HEAD_DIM), BF16), pltpu.VMEM((nh, rows, HEAD_DIM), F32),
                        pltpu.VMEM((rows, rows), F32), pltpu.VMEM((rows, nch * HEAD_DIM), BF16)],
        compiler_params=_params(2),
    )(hg, o_pre, dya, st0, lb_logits, norm_g)


def _mla_pre(ms, q_a_g, kv_a_g, wuq3, wukv3, tabs):
    s = ms.shape[0]
    tm = min(TM_MLA, s)

    def body(ms_ref, qg_ref, kvg_ref, wuq_ref, wukv_ref, c_ref, sa_ref, sb_ref,
             q_ref, k_ref, v_ref, cqn_ref, ckvn_ref):
        c, sa, sb = c_ref[...], sa_ref[...], sb_ref[...]
        cqn = _rms(ms_ref[:, 0:Q_LORA], qg_ref[...]).astype(BF16)
        ckvn = _rms(ms_ref[:, Q_LORA:Q_LORA + KV_LORA], kvg_ref[...]).astype(BF16)
        cqn_ref[...] = cqn
        ckvn_ref[...] = ckvn
        k_pe = _rope(ms_ref[:, Q_LORA + KV_LORA:MS_COLS], c, sa, sb).astype(BF16)
        for h in range(HEADS):
            qh = _dot(cqn, wuq_ref[h])
            q_ref[h, :, 0:128] = (qh[:, 0:128] * Q_PRESCALE).astype(BF16)
            q_ref[h, :, 128:256] = (_rope(qh[:, 128:256], c, sa, sb) * Q_PRESCALE).astype(BF16)
            kvh = _dot(ckvn, wukv_ref[h])
            k_ref[h, :, 0:128] = kvh[:, 0:128].astype(BF16)
            k_ref[h, :, 128:256] = k_pe
            v_ref[h] = kvh[:, 128:256].astype(BF16)

    tab = pl.BlockSpec((tm, 128), lambda i: (i, 0))
    return pl.pallas_call(
        body, name="mla_pre", grid=(s // tm,),
        in_specs=[pl.BlockSpec((tm, MS_COLS), lambda i: (i, 0)),
                  pl.BlockSpec((1, Q_LORA), lambda i: (0, 0)), pl.BlockSpec((1, KV_LORA), lambda i: (0, 0)),
                  pl.BlockSpec((HEADS, Q_LORA, QK_PAD), lambda i: (0, 0, 0)),
                  pl.BlockSpec((HEADS, KV_LORA, 256), lambda i: (0, 0, 0)), tab, tab, tab],
        out_specs=[pl.BlockSpec((HEADS, tm, QK_PAD), lambda i: (0, i, 0)),
                   pl.BlockSpec((HEADS, tm, QK_PAD), lambda i: (0, i, 0)),
                   pl.BlockSpec((HEADS, tm, HEAD_DIM), lambda i: (0, i, 0)),
                   pl.BlockSpec((tm, Q_LORA), lambda i: (i, 0)), pl.BlockSpec((tm, KV_LORA), lambda i: (i, 0))],
        out_shape=[jax.ShapeDtypeStruct((HEADS, s, QK_PAD), BF16), jax.ShapeDtypeStruct((HEADS, s, QK_PAD), BF16),
                   jax.ShapeDtypeStruct((HEADS, s, HEAD_DIM), BF16),
                   jax.ShapeDtypeStruct((s, Q_LORA), BF16), jax.ShapeDtypeStruct((s, KV_LORA), BF16)],
        compiler_params=_params(1),
    )(ms, q_a_g, kv_a_g, wuq3, wukv3, *tabs)


def _causal_mask(t):
    r = lax.broadcasted_iota(jnp.int32, (t, t), 0)
    c = lax.broadcasted_iota(jnp.int32, (t, t), 1)
    return r >= c


def _flash_fwd(q, k, v, mz):
    s = q.shape[1]
    t = min(TQ, s)

    def body(q_ref, k_ref, v_ref, mz_ref, o_ref, yb_ref, lse_ref, m_s, l_s, acc_s):
        i = pl.program_id(1)
        m_s[...] = jnp.full_like(m_s, -jnp.inf)
        l_s[...] = jnp.zeros_like(l_s)
        acc_s[...] = jnp.zeros_like(acc_s)

        def step(j, groups):
            rows = pl.ds(pl.multiple_of(j * t, t), t)
            for hh in range(FLASH_HEADS):
                for r0, nr, masked in groups:
                    r = slice(r0, r0 + nr)
                    sc = _dot_nt(q_ref[hh, r, :], k_ref[hh, rows, :])
                    if masked:
                        sc = jnp.where(_causal_mask(t), sc, -jnp.inf)
                    m_prev = m_s[hh, r, :]
                    m_new = jnp.maximum(m_prev, jnp.max(sc, axis=-1, keepdims=True))
                    p = jnp.exp2(sc - jnp.tile(m_new, (1, t // 128)))
                    alpha = jnp.exp2(m_prev - m_new)
                    l_s[hh, r, :] = alpha * l_s[hh, r, :] + jnp.sum(p, axis=-1, keepdims=True)
                    acc_s[hh, r, :] = alpha * acc_s[hh, r, :] + _dot(p.astype(BF16), v_ref[hh, rows, :])
                    m_s[hh, r, :] = m_new

        def loop_body(jj, carry):
            step(2 * jj, ((0, 2 * t, False),))
            step(2 * jj + 1, ((0, 2 * t, False),))
            return carry

        lax.fori_loop(0, i, loop_body, 0)
        step(2 * i, ((0, t, True), (t, t, False)))
        step(2 * i + 1, ((t, t, True),))
        for hh in range(FLASH_HEADS):
            cols = slice(hh * HEAD_DIM, (hh + 1) * HEAD_DIM)
            out = acc_s[hh] / l_s[hh]
            o_ref[:, cols] = out
            silu_z, _ = _silu_parts(mz_ref[:, cols])
            yb_ref[:, cols] = (out * silu_z).astype(BF16)
            lse_ref[hh] = m_s[hh] + jnp.log2(l_s[hh])

    nh = FLASH_HEADS
    t2 = 2 * t
    col = pl.BlockSpec((t2, nh * HEAD_DIM), lambda h, i: (i, h))
    return pl.pallas_call(
        body, name="flash_fwd", grid=(HEADS // nh, s // t2),
        in_specs=[pl.BlockSpec((nh, t2, QK_PAD), lambda h, i: (h, i, 0)),
                  pl.BlockSpec((nh, s, QK_PAD), lambda h, i: (h, 0, 0)),
                  pl.BlockSpec((nh, s, HEAD_DIM), lambda h, i: (h, 0, 0)), col],
        out_specs=[col, col, pl.BlockSpec((nh, t2, 128), lambda h, i: (h, i, 0))],
        out_shape=[jax.ShapeDtypeStruct((s, D_MODEL), F32), jax.ShapeDtypeStruct((s, D_MODEL), BF16),
                   jax.ShapeDtypeStruct((HEADS, s, 128), F32)],
        scratch_shapes=[pltpu.VMEM((nh, t2, 128), F32), pltpu.VMEM((nh, t2, 128), F32),
                        pltpu.VMEM((nh, t2, HEAD_DIM), F32)],
        compiler_params=_params(2),
    )(q, k, v, mz)


def _flash_bwd(q, k, v, dyb, mz, o_att, lse, tabs):
    s = q.shape[1]
    t = min(TQ, s)

    def body(q_ref, k_ref, v_ref, dyb_ref, mz_ref, o_ref, lse_ref, c_ref, sa_ref, sb_ref,
             dq_ref, dk_ref, dv_ref, dmz_ref, dq_s, delta_s, do_s):
        i = pl.program_id(1)

        @pl.when(i == 0)
        def _():
            dk_ref[...] = jnp.zeros_like(dk_ref)
            dv_ref[...] = jnp.zeros_like(dv_ref)

        silu_z, dsilu_z = _silu_parts(mz_ref[...])
        dyb_v = dyb_ref[...]
        out = o_ref[...]
        do32 = dyb_v * silu_z
        dmz_ref[...] = (dyb_v * out * dsilu_z).astype(BF16)
        delta_s[...] = jnp.broadcast_to(jnp.sum(do32 * out, axis=-1, keepdims=True), (2 * t, 128))
        do_s[...] = do32.astype(BF16)
        dq_s[...] = jnp.zeros_like(dq_s)

        def step(j, modes):
            rows = pl.ds(pl.multiple_of(j * t, t), t)
            kj = k_ref[rows, :]
            vj = v_ref[rows, :]
            dv_acc = None
            dk_acc = None
            for ch, masked in enumerate(modes):
                if masked is None:
                    continue
                r = slice(ch * t, (ch + 1) * t)
                qv = q_ref[r, :]
                do = do_s[r, :]
                sc = _dot_nt(qv, kj)
                if masked:
                    sc = jnp.where(_causal_mask(t), sc, -jnp.inf)
                p = jnp.exp2(sc - jnp.tile(lse_ref[r, :], (1, t // 128)))
                dp = _dot_nt(do, vj)
                ds = (p * (dp - jnp.tile(delta_s[r, :], (1, t // 128)))).astype(BF16)
                dv_c = _dot_tn(p.astype(BF16), do)
                dk_c = _dot_tn(ds, qv)
                dv_acc = dv_c if dv_acc is None else dv_acc + dv_c
                dk_acc = dk_c if dk_acc is None else dk_acc + dk_c
                dq_s[r, :] += _dot(ds, kj)
            dv_ref[rows, :] += dv_acc
            dk_ref[rows, :] += dk_acc

        def loop_body(jj, carry):
            step(2 * jj, (False, False))
            step(2 * jj + 1, (False, False))
            return carry

        lax.fori_loop(0, i, loop_body, 0)
        step(2 * i, (True, False))
        step(2 * i + 1, (None, True))
        dq = dq_s[...] * ATT_SCALE
        dq_ref[:, 0:128] = dq[:, 0:128].astype(BF16)
        dq_ref[:, 128:256] = _rope_bwd(dq[:, 128:256], c_ref[...], sa_ref[...], sb_ref[...]).astype(BF16)

    t2 = 2 * t
    col = pl.BlockSpec((t2, HEAD_DIM), lambda h, i: (i, h))
    tab = pl.BlockSpec((t2, 128), lambda h, i: (i, 0))
    return pl.pallas_call(
        body, name="flash_bwd", grid=(HEADS, s // t2),
        in_specs=[pl.BlockSpec((None, t2, QK_PAD), lambda h, i: (h, i, 0)),
                  pl.BlockSpec((None, s, QK_PAD), lambda h, i: (h, 0, 0)),
                  pl.BlockSpec((None, s, HEAD_DIM), lambda h, i: (h, 0, 0)),
                  col, col, col, pl.BlockSpec((None, t2, 128), lambda h, i: (h, i, 0)), tab, tab, tab],
        out_specs=[pl.BlockSpec((None, t2, QK_PAD), lambda h, i: (h, i, 0)),
                   pl.BlockSpec((None, s, QK_PAD), lambda h, i: (h, 0, 0)),
                   pl.BlockSpec((None, s, HEAD_DIM), lambda h, i: (h, 0, 0)), col],
        out_shape=[jax.ShapeDtypeStruct((HEADS, s, QK_PAD), BF16), jax.ShapeDtypeStruct((HEADS, s, QK_PAD), F32),
                   jax.ShapeDtypeStruct((HEADS, s, HEAD_DIM), F32), jax.ShapeDtypeStruct((s, D_MODEL), BF16)],
        scratch_shapes=[pltpu.VMEM((t2, QK_PAD), F32), pltpu.VMEM((t2, 128), F32), pltpu.VMEM((t2, HEAD_DIM), BF16)],
        compiler_params=_params(2),
    )(q, k, v, dyb, mz, o_att, lse, *tabs)


def _mla_bwd_proj(dq, dk, dv, cqn, ckvn, ms, q_a_g, kv_a_g, wuq3, wukv3, tabs):
    s = ms.shape[0]
    tm = min(TM_MLA, s)

    def body(dq_ref, dk_ref, dv_ref, cqn_ref, ckvn_ref, ms_ref, qg_ref, kvg_ref, wuq_ref, wukv_ref,
             c_ref, sa_ref, sb_ref, dms_ref, dwuq_ref, dwukv_ref, dqg_ref, dkvg_ref):
        @pl.when(pl.program_id(0) == 0)
        def _():
            dwuq_ref[...] = jnp.zeros_like(dwuq_ref)
            dwukv_ref[...] = jnp.zeros_like(dwukv_ref)
            dqg_ref[...] = jnp.zeros_like(dqg_ref)
            dkvg_ref[...] = jnp.zeros_like(dkvg_ref)

        cqn = cqn_ref[...]
        ckvn = ckvn_ref[...]
        dcqn = jnp.zeros((tm, Q_LORA), F32)
        dckvn = jnp.zeros((tm, KV_LORA), F32)
        dkpe = jnp.zeros((tm, 128), F32)
        for h in range(HEADS):
            dqh = dq_ref[h]
            dcqn += _dot_nt(dqh, wuq_ref[h])
            dwuq_ref[h] += _dot_tn(cqn, dqh)
            dkh = dk_ref[h] * LN2
            dkvh = jnp.concatenate([dkh[:, 0:128], dv_ref[h]], axis=1).astype(BF16)
            dckvn += _dot_nt(dkvh, wukv_ref[h])
            dwukv_ref[h] += _dot_tn(ckvn, dkvh)
            dkpe += dkh[:, 128:256]
        dcq, dqg_rows = _rms_bwd(ms_ref[:, 0:Q_LORA], qg_ref[...], dcqn)
        dckv, dkvg_rows = _rms_bwd(ms_ref[:, Q_LORA:Q_LORA + KV_LORA], kvg_ref[...], dckvn)
        dqg_ref[...] += jnp.sum(dqg_rows, axis=0, keepdims=True)
        dkvg_ref[...] += jnp.sum(dkvg_rows, axis=0, keepdims=True)
        dms_ref[:, 0:Q_LORA] = dcq.astype(BF16)
        dms_ref[:, Q_LORA:Q_LORA + KV_LORA] = dckv.astype(BF16)
        dms_ref[:, Q_LORA + KV_LORA:MS_COLS] = _rope_bwd(dkpe, c_ref[...], sa_ref[...], sb_ref[...]).astype(BF16)

    tab = pl.BlockSpec((tm, 128), lambda i: (i, 0))
    wq = pl.BlockSpec((HEADS, Q_LORA, QK_PAD), lambda i: (0, 0, 0))
    wkv = pl.BlockSpec((HEADS, KV_LORA, 256), lambda i: (0, 0, 0))
    qg = pl.BlockSpec((1, Q_LORA), lambda i: (0, 0))
    kvg = pl.BlockSpec((1, KV_LORA), lambda i: (0, 0))
    return pl.pallas_call(
        body, name="mla_bwd_proj", grid=(s // tm,),
        in_specs=[pl.BlockSpec((HEADS, tm, QK_PAD), lambda i: (0, i, 0)),
                  pl.BlockSpec((HEADS, tm, QK_PAD), lambda i: (0, i, 0)),
                  pl.BlockSpec((HEADS, tm, HEAD_DIM), lambda i: (0, i, 0)),
                  pl.BlockSpec((tm, Q_LORA), lambda i: (i, 0)), pl.BlockSpec((tm, KV_LORA), lambda i: (i, 0)),
                  pl.BlockSpec((tm, MS_COLS), lambda i: (i, 0)), qg, kvg, wq, wkv, tab, tab, tab],
        out_specs=[pl.BlockSpec((tm, MS_COLS), lambda i: (i, 0)), wq, wkv, qg, kvg],
        out_shape=[jax.ShapeDtypeStruct((s, MS_COLS), BF16), jax.ShapeDtypeStruct((HEADS, Q_LORA, QK_PAD), F32),
                   jax.ShapeDtypeStruct((HEADS, KV_LORA, 256), F32),
                   jax.ShapeDtypeStruct((1, Q_LORA), F32), jax.ShapeDtypeStruct((1, KV_LORA), F32)],
        compiler_params=_params(1),
    )(dq, dk, dv, cqn, ckvn, ms, q_a_g, kv_a_g, wuq3, wukv3, *tabs)


def _merge_fused(ya, yb, glog, b_gate, x, tgt, fg, wproj):
    s = x.shape[0]
    tm = min(TM_FUSED, s)

    def body(ya_ref, yb_ref, g0_ref, g1_ref, b0_ref, b1_ref, x_ref, t_ref, fg_ref, w_ref,
             mg_ref, dx2_ref, dx2b_ref, dya_ref, dyb_ref, dgl_ref, dpa_ref, dpb_ref, loss_ref, dfg_ref, dbg_ref):
        @pl.when(pl.program_id(0) == 0)
        def _():
            loss_ref[...] = jnp.zeros_like(loss_ref)
            dfg_ref[...] = jnp.zeros_like(dfg_ref)
            dbg_ref[...] = jnp.zeros_like(dbg_ref)

        pa = _dot(ya_ref[...], w_ref[0])
        pb = _dot(yb_ref[...], w_ref[1])
        g0 = jax.nn.sigmoid(g0_ref[...] + b0_ref[...])
        g1 = jax.nn.sigmoid(g1_ref[...] + b1_ref[...])
        merged = (g0 * pa + g1 * pb).astype(BF16)
        mg_ref[...] = merged
        x2 = x_ref[...] + _dot(merged, w_ref[2])
        fg_v = fg_ref[...]
        err = _rms(x2, fg_v) - t_ref[...]
        loss_ref[...] += 0.5 * jnp.sum(jnp.mean(err * err, axis=-1, keepdims=True), axis=0, keepdims=True)
        dx2, dfg_rows = _rms_bwd(x2, fg_v, err * (1.0 / D_MODEL))
        dx2_ref[...] = dx2
        dfg_ref[...] += jnp.sum(dfg_rows, axis=0, keepdims=True)

        dx2b = dx2.astype(BF16)
        dx2b_ref[...] = dx2b
        dmg = _dot_nt(dx2b, w_ref[2])
        dpa = (dmg * g0).astype(BF16)
        dpb = (dmg * g1).astype(BF16)
        dpa_ref[...] = dpa
        dpb_ref[...] = dpb
        dgl0 = dmg * pa * g0 * (1.0 - g0)
        dgl1 = dmg * pb * g1 * (1.0 - g1)
        dgl_ref[:, 0:D_MODEL] = dgl0.astype(BF16)
        dgl_ref[:, D_MODEL:2 * D_MODEL] = dgl1.astype(BF16)
        dbg_ref[:, 0:D_MODEL] += jnp.sum(dgl0, axis=0, keepdims=True)
        dbg_ref[:, D_MODEL:2 * D_MODEL] += jnp.sum(dgl1, axis=0, keepdims=True)
        dya_ref[...] = _dot_nt(dpa, w_ref[0])
        dyb_ref[...] = _dot_nt(dpb, w_ref[1])

    row = pl.BlockSpec((tm, D_MODEL), lambda i: (i, 0))
    row1 = pl.BlockSpec((tm, D_MODEL), lambda i: (i, 1))
    row2 = pl.BlockSpec((tm, 2 * D_MODEL), lambda i: (i, 0))
    vec = pl.BlockSpec((1, D_MODEL), lambda i: (0, 0))
    vec1 = pl.BlockSpec((1, D_MODEL), lambda i: (0, 1))
    vec2 = pl.BlockSpec((1, 2 * D_MODEL), lambda i: (0, 0))
    f32_rows = jax.ShapeDtypeStruct((s, D_MODEL), F32)
    bf16_rows = jax.ShapeDtypeStruct((s, D_MODEL), BF16)
    return pl.pallas_call(
        body, name="merge_fused", grid=(s // tm,),
        in_specs=[row, row, row, row1, vec, vec1, row, row, vec, pl.BlockSpec((3, D_MODEL, D_MODEL), lambda i: (0, 0, 0))],
        out_specs=[row, row, row, row, row, row2, row, row, pl.BlockSpec((1, 128), lambda i: (0, 0)), vec, vec2],
        out_shape=[bf16_rows, f32_rows, bf16_rows, f32_rows, f32_rows, jax.ShapeDtypeStruct((s, 2 * D_MODEL), BF16),
                   bf16_rows, bf16_rows, jax.ShapeDtypeStruct((1, 128), F32), jax.ShapeDtypeStruct((1, D_MODEL), F32),
                   jax.ShapeDtypeStruct((1, 2 * D_MODEL), F32)],
        compiler_params=_params(1),
    )(ya, yb, glog, glog, b_gate, b_gate, x, tgt, fg, wproj)


def _proj_fused(x, g, w_int, r_blk):
    s = x.shape[0]
    tm = min(TM_FUSED, s)

    def body(x_ref, g_ref, w_hbm, r_ref, h_ref, hg_ref, ms_ref, mz_ref, gl_ref, or_ref, w_s, sem, send_sems, recv_sems):
        mx, my, mc = _me()
        chips = _other_chips(mx, my)
        mine, theirs = _cols(mc), _cols(1 - mc)

        def copy(k, src, dst, to):
            return pltpu.make_async_remote_copy(src_ref=src, dst_ref=dst, send_sem=send_sems.at[k],
                                                recv_sem=recv_sems.at[k], device_id=to, device_id_type=MESH_ID)

        def sends():
            return [copy(j, r_ref.at[:, mine], or_ref.at[2 * mx + my, :, mine], (cx, cy, mc))
                    for j, (cx, cy) in enumerate(chips)]

        @pl.when(pl.program_id(0) == 0)
        def _():
            for cp in sends():
                cp.start()
            cp = pltpu.make_async_copy(w_hbm, w_s, sem)
            cp.start()
            cp.wait()

        h = _rms(x_ref[...], g_ref[...]).astype(BF16)
        h_ref[...] = h
        for j in range(4):
            hg_ref[j] = _dot_nt(h, w_s[j * D_MODEL:(j + 1) * D_MODEL, :])
        ms = _dot_nt(h, w_s[4096:4096 + MS_COLS, :])
        lane = lax.broadcasted_iota(jnp.int32, ms.shape, 1)
        ms_ref[...] = jnp.where(lane < 704, ms, 0.0)
        mz_ref[...] = _dot_nt(h, w_s[4800:5824, :])
        for j in range(2):
            gl_ref[:, j * D_MODEL:(j + 1) * D_MODEL] = _dot_nt(h, w_s[5824 + j * D_MODEL:5824 + (j + 1) * D_MODEL, :])

        @pl.when(pl.program_id(0) == s // tm - 1)
        def _():
            passed = []
            for j, (cx, cy) in enumerate(chips):
                landed = or_ref.at[2 * cx + cy, :, mine]
                copy(j, landed, landed, (cx, cy, mc)).wait_recv()
                fwd = copy(3 + j, landed, landed, (mx, my, 1 - mc))
                fwd.start()
                passed.append(fwd)
            for j, (cx, cy) in enumerate(chips):
                other = or_ref.at[2 * cx + cy, :, theirs]
                copy(3 + j, other, other, (mx, my, 1 - mc)).wait_recv()
            for cp in sends() + passed:
                cp.wait_send()

    row = pl.BlockSpec((tm, D_MODEL), lambda i: (i, 0))
    outs = pl.pallas_call(
        body, name="proj_fused", grid=(s // tm,),
        in_specs=[row, pl.BlockSpec((1, D_MODEL), lambda i: (0, 0)), ANY, ANY],
        out_specs=[row, pl.BlockSpec((4, tm, D_MODEL), lambda i: (0, i, 0)), pl.BlockSpec((tm, MS_COLS), lambda i: (i, 0)),
                   row, pl.BlockSpec((tm, 2 * D_MODEL), lambda i: (i, 0)), ANY],
        out_shape=[jax.ShapeDtypeStruct((s, D_MODEL), BF16), jax.ShapeDtypeStruct((4, s, D_MODEL), F32),
                   jax.ShapeDtypeStruct((s, MS_COLS), F32), jax.ShapeDtypeStruct((s, D_MODEL), F32),
                   jax.ShapeDtypeStruct((s, 2 * D_MODEL), F32), jax.ShapeDtypeStruct((N_CHIPS,) + r_blk.shape, r_blk.dtype)],
        scratch_shapes=[pltpu.VMEM(w_int.shape, BF16), pltpu.SemaphoreType.DMA,
                        pltpu.SemaphoreType.DMA((6,)), pltpu.SemaphoreType.DMA((6,))],
        compiler_params=_params(1),
    )(x, g, w_int, r_blk)
    gr = lax.dynamic_update_slice(outs[5], r_blk[None], (2 * lax.axis_index("x") + lax.axis_index("y"), 0, 0))
    return (*outs[:5], gr)


def _dh_fused(dhg, dms, dmz, dglog, w_int, x, g, dx2, hw, hr):
    s = x.shape[0]
    tm = min(512, s)
    nw, nr = hw.shape[0] // N_CHIPS, hr.shape[0] // N_CHIPS

    def body(dhg_ref, dms_ref, dmz_ref, dgl_ref, w_hbm, x_ref, g_ref, dx2_ref, hw_ref, hr_ref,
             dx_ref, dg_ref, lw_ref, lr_ref, w_s, sem, send_sems, recv_sems):
        def scatter_copies():
            mx, my, mc = _me()
            return [pltpu.make_async_remote_copy(
                src_ref=src.at[pl.ds((2 * cx + cy) * n, n), :], dst_ref=dst.at[j], send_sem=send_sems.at[3 * a + j],
                recv_sem=recv_sems.at[3 * a + j], device_id=(cx, cy, mc), device_id_type=MESH_ID)
                for a, (src, dst, n) in enumerate([(hw_ref, lw_ref, nw), (hr_ref, lr_ref, nr)])
                for j, (cx, cy) in enumerate(_other_chips(mx, my))]

        @pl.when(pl.program_id(0) == 0)
        def _():
            for cp in scatter_copies():
                cp.start()
            dg_ref[...] = jnp.zeros_like(dg_ref)
            cp = pltpu.make_async_copy(w_hbm, w_s, sem)
            cp.start()
            cp.wait()

        dh = _dot(dms_ref[...], w_s[4096:4096 + MS_COLS, :]) + _dot(dmz_ref[...], w_s[4800:5824, :])
        for j in range(4):
            dh += _dot(dhg_ref[j], w_s[j * D_MODEL:(j + 1) * D_MODEL, :])
        for j in range(2):
            dh += _dot(dgl_ref[:, j * D_MODEL:(j + 1) * D_MODEL], w_s[5824 + j * D_MODEL:5824 + (j + 1) * D_MODEL, :])
        dx, dg_rows = _rms_bwd(x_ref[...], g_ref[...], dh)
        dx_ref[...] = dx + dx2_ref[...]
        dg_ref[...] += jnp.sum(dg_rows, axis=0, keepdims=True)

        @pl.when(pl.program_id(0) == s // tm - 1)
        def _():
            for cp in scatter_copies():
                cp.wait()

    row = pl.BlockSpec((tm, D_MODEL), lambda i: (i, 0))
    vec = pl.BlockSpec((1, D_MODEL), lambda i: (0, 0))
    return pl.pallas_call(
        body, name="dh_fused", grid=(s // tm,),
        in_specs=[pl.BlockSpec((4, tm, D_MODEL), lambda i: (0, i, 0)), pl.BlockSpec((tm, MS_COLS), lambda i: (i, 0)), row,
                  pl.BlockSpec((tm, 2 * D_MODEL), lambda i: (i, 0)), ANY, row, vec, row, ANY, ANY],
        out_specs=[row, vec, ANY, ANY],
        out_shape=[jax.ShapeDtypeStruct((s, D_MODEL), F32), jax.ShapeDtypeStruct((1, D_MODEL), F32),
                   jax.ShapeDtypeStruct((3, nw, HALF_COLS), hw.dtype), jax.ShapeDtypeStruct((3, nr, HALF_COLS), hr.dtype)],
        scratch_shapes=[pltpu.VMEM(w_int.shape, BF16), pltpu.SemaphoreType.DMA,
                        pltpu.SemaphoreType.DMA((6,)), pltpu.SemaphoreType.DMA((6,))],
        compiler_params=_params(1),
    )(dhg, dms, dmz, dglog, w_int, x, g, dx2, hw, hr)


def _local_step(x, tgt, w_int, r_blk, norm_g, b_gate, lb_logits, hg_norm_g, q_a_g, kv_a_g, fg):
    s = x.shape[0]
    tabs = _rope_tables(s)

    h, hg, ms, mz, glog, gr = _proj_fused(x, norm_g, w_int, r_blk)
    w_uq, w_ukv, wproj = _unpack_rest_weights(gr)
    wuq3 = jnp.pad(w_uq.reshape(Q_LORA, HEADS, QK_DIM).transpose(1, 0, 2), ((0, 0), (0, 0), (0, QK_PAD - QK_DIM)))
    wukv3 = w_ukv.reshape(KV_LORA, HEADS, 256).transpose(1, 0, 2)
    o_pre, ya, st0 = _hgrn_fwd(hg, lb_logits, hg_norm_g)
    q, k, v, cqn, ckvn = _mla_pre(ms, q_a_g, kv_a_g, wuq3, wukv3, tabs)
    o_att, yb, lse = _flash_fwd(q, k, v, mz)
    merged, dx2, dx2b, dya, dyb, dglog, dpa, dpb, loss, dfg, dbg = _merge_fused(ya, yb, glog, b_gate, x, tgt, fg, wproj)

    d_wout = _mm_tn(merged, dx2b, name="dw_out")
    d_wpa = _mm_tn(ya, dpa, name="dw_proj_a")
    d_wpb = _mm_tn(yb, dpb, name="dw_proj_b")
    dhg, dlb, dhgg = _hgrn_bwd(hg, o_pre, dya, st0, lb_logits, hg_norm_g)
    dq, dk, dv, dmz = _flash_bwd(q, k, v, dyb, mz, o_att, lse, tabs)
    dms, d_wuq3, d_wukv3, dqg, dkvg = _mla_bwd_proj(dq, dk, dv, cqn, ckvn, ms, q_a_g, kv_a_g, wuq3, wukv3, tabs)
    d_hg = _mm_tn(dhg, h, name="dw_in_hg")
    d_ms = _mm_tn(dms, h, name="dw_in_ms")
    d_mz = _mm_tn(dmz, h, name="dw_in_mz")
    d_gl = _mm_tn(dglog, h, name="dw_in_gate")
    d_w_int = jnp.concatenate([d_hg.reshape(4 * D_MODEL, D_MODEL), d_ms[0:704], d_mz, d_gl], axis=0)
    small = {"b_gate": dbg, "lb": dlb, "hg_norm_g": dhgg, "q_a_g": dqg, "kv_a_g": dkvg, "final_norm_g": dfg}
    dh_args = (dhg, dms, dmz, dglog, w_int, x, norm_g, dx2)
    return loss, dh_args, d_w_int, d_wuq3, d_wukv3, (d_wpa, d_wpb, d_wout), small


def _pack_rest(w_uq_b, w_ukv_b, wpa_b, wpb_b, wout_b):
    return jnp.concatenate([w_uq_b.reshape(144, D_MODEL), w_ukv_b.reshape(128, D_MODEL), wpa_b, wpb_b, wout_b], axis=0)


def _unpack_rest(p):
    return (p[0:144].reshape(Q_LORA, 384), p[144:272].reshape(KV_LORA, 512), p[272:528], p[528:784], p[784:1040])


def _pack_rest_grads(d_wuq3, d_wukv3, d_proj):
    d_wuq = d_wuq3.transpose(1, 0, 2)[:, :, 0:QK_DIM].reshape(Q_LORA, HEADS * QK_DIM)
    d_wukv = d_wukv3.transpose(1, 0, 2).reshape(KV_LORA, HEADS * 256)
    blocks = []
    for b in range(N_CHIPS):
        rows = slice(b * 256, (b + 1) * 256)
        blocks.append(_pack_rest(d_wuq[:, b * 384:(b + 1) * 384], d_wukv[:, b * 512:(b + 1) * 512],
                                 d_proj[0][rows], d_proj[1][rows], d_proj[2][rows]))
    return jnp.stack(blocks, axis=0)


def _unpack_rest_weights(g):
    parts = [_unpack_rest(g[b]) for b in range(N_CHIPS)]
    w_uq, w_ukv = (jnp.concatenate([p[n] for p in parts], axis=1) for n in range(2))
    wproj = jnp.stack([jnp.concatenate([p[n] for p in parts], axis=0) for n in range(2, 5)], axis=0)
    return w_uq, w_ukv, wproj


MESH_ID = pl.DeviceIdType.MESH
ANY = pl.BlockSpec(memory_space=pl.ANY)
HALF_COLS = D_MODEL // 2


def _me():
    return lax.axis_index("x"), lax.axis_index("y"), lax.axis_index("c")


def _other_chips(x, y):
    return [(1 - x, y), (x, 1 - y), (1 - x, 1 - y)]


def _cols(c):
    return pl.ds(c * HALF_COLS, HALF_COLS)


RELAY_TOP = 992


def _gather_weights(w_blk):
    bot = W_IN_BLK - RELAY_TOP

    def body(w_ref, ow_ref, send_sems, recv_sems):
        x, y, c = _me()
        me, xn, yn, dg = 2 * x + y, 2 * (1 - x) + y, 2 * x + (1 - y), 2 * (1 - x) + (1 - y)
        to_x, to_y, to_sib = (1 - x, y, c), (x, 1 - y, c), (x, y, 1 - c)
        mine, theirs = _cols(c), _cols(1 - c)
        top, low = pl.ds(0, RELAY_TOP), pl.ds(RELAY_TOP, bot)

        def copy(k, src, dst, to):
            return pltpu.make_async_remote_copy(src_ref=src, dst_ref=dst, send_sem=send_sems.at[k],
                                                recv_sem=recv_sems.at[k], device_id=to, device_id_type=MESH_ID)

        def same(k, ref, to):
            return copy(k, ref, ref, to)

        own = [copy(0, w_ref.at[:, mine], ow_ref.at[me, :, mine], to_x),
               copy(1, w_ref.at[:, mine], ow_ref.at[me, :, mine], to_y)]
        for cp in own:
            cp.start()
        from_x, from_y = ow_ref.at[xn, :, mine], ow_ref.at[yn, :, mine]
        same(0, from_x, to_x).wait_recv()
        relay_y = same(2, ow_ref.at[xn, top, mine], to_y)
        pass_x = same(4, from_x, to_sib)
        relay_y.start()
        pass_x.start()
        same(1, from_y, to_y).wait_recv()
        relay_x = same(3, ow_ref.at[yn, low, mine], to_x)
        pass_y = same(5, from_y, to_sib)
        relay_x.start()
        pass_y.start()
        same(2, ow_ref.at[dg, top, mine], to_y).wait_recv()
        same(3, ow_ref.at[dg, low, mine], to_x).wait_recv()
        pass_d = same(6, ow_ref.at[dg, :, mine], to_sib)
        pass_d.start()
        for k, blk in ((4, xn), (5, yn), (6, dg)):
            same(k, ow_ref.at[blk, :, theirs], to_sib).wait_recv()
        for cp in own + [relay_y, relay_x, pass_x, pass_y, pass_d]:
            cp.wait_send()

    gw = pl.pallas_call(
        body, name="gather_weights", in_specs=[ANY], out_specs=ANY,
        out_shape=jax.ShapeDtypeStruct((N_CHIPS,) + w_blk.shape, w_blk.dtype),
        scratch_shapes=[pltpu.SemaphoreType.DMA((7,)), pltpu.SemaphoreType.DMA((7,))],
    )(w_blk)
    return lax.dynamic_update_slice(gw, w_blk[None], (2 * lax.axis_index("x") + lax.axis_index("y"), 0, 0))


def _swap_halves(gw, gr):
    def body(gw_ref, gr_ref, lw_ref, lr_ref, send_sems, recv_sems):
        x, y, c = _me()
        cps = [pltpu.make_async_remote_copy(
            src_ref=src, dst_ref=dst, send_sem=send_sems.at[a], recv_sem=recv_sems.at[a],
            device_id=(x, y, 1 - c), device_id_type=MESH_ID)
            for a, (src, dst) in enumerate([(gw_ref.at[:, _cols(1 - c)], lw_ref),
                                            (gr_ref.at[:, :, _cols(1 - c)], lr_ref)])]
        for cp in cps:
            cp.start()
        for cp in cps:
            cp.wait()

    return pl.pallas_call(
        body, name="grad_swap_halves", in_specs=[ANY, ANY], out_specs=[ANY, ANY],
        out_shape=[jax.ShapeDtypeStruct((gw.shape[0], HALF_COLS), gw.dtype),
                   jax.ShapeDtypeStruct(gr.shape[:2] + (HALF_COLS,), gr.dtype)],
        scratch_shapes=[pltpu.SemaphoreType.DMA((2,)), pltpu.SemaphoreType.DMA((2,))],
    )(gw, gr)


def _swap_reduced(rw, rr):
    def body(rw_ref, rr_ref, ow_ref, or_ref, send_sems, recv_sems):
        x, y, c = _me()
        cps = [pltpu.make_async_remote_copy(
            src_ref=src, dst_ref=dst, send_sem=send_sems.at[a], recv_sem=recv_sems.at[a],
            device_id=(x, y, 1 - c), device_id_type=MESH_ID)
            for a, (src, dst) in enumerate([(rw_ref, ow_ref), (rr_ref, or_ref)])]
        for cp in cps:
            cp.start()
        for cp in cps:
            cp.wait()

    return pl.pallas_call(
        body, name="grad_swap_reduced", in_specs=[ANY, ANY], out_specs=[ANY, ANY],
        out_shape=[jax.ShapeDtypeStruct(rw.shape, rw.dtype), jax.ShapeDtypeStruct(rr.shape, rr.dtype)],
        scratch_shapes=[pltpu.SemaphoreType.DMA((2,)), pltpu.SemaphoreType.DMA((2,))],
    )(rw, rr)


def _join_cols(mine, theirs):
    first = lax.axis_index("c") == 0
    return jnp.concatenate([jnp.where(first, mine, theirs), jnp.where(first, theirs, mine)], axis=1)


def _gather_small(vec):
    def body(v_ref, out_ref, send_sems, recv_sems, local_sem):
        x, y, c = _me()
        my_id = 4 * x + 2 * y + c
        mine = pltpu.make_async_copy(v_ref, out_ref.at[my_id], local_sem)
        mine.start()
        cps = []
        for r in range(1, N_DEV):
            peer = (x ^ (r >> 2), y ^ ((r >> 1) & 1), c ^ (r & 1))
            cps.append(pltpu.make_async_remote_copy(
                src_ref=v_ref, dst_ref=out_ref.at[my_id], send_sem=send_sems.at[r - 1],
                recv_sem=recv_sems.at[r - 1], device_id=peer, device_id_type=MESH_ID))
        for cp in cps:
            cp.start()
        for cp in cps:
            cp.wait()
        mine.wait()

    return pl.pallas_call(
        body, name="gather_small", in_specs=[ANY], out_specs=ANY,
        out_shape=jax.ShapeDtypeStruct((N_DEV, 1, SMALL_COLS), vec.dtype),
        scratch_shapes=[pltpu.SemaphoreType.DMA((N_DEV - 1,)), pltpu.SemaphoreType.DMA((N_DEV - 1,)),
                        pltpu.SemaphoreType.DMA],
    )(vec)


def _add_cores(c_idx, g, landed, *, tm, name):
    r = g.shape[0]

    def body(c_ref, g_ref, l_ref, o32_ref, o16_ref):
        acc = g_ref[...] + l_ref[...]
        o32_ref[...] = acc
        o16_ref[...] = acc.astype(BF16)

    half = pl.BlockSpec((tm, HALF_COLS), lambda i, c_ref: (i, 0))
    grid_spec = pltpu.PrefetchScalarGridSpec(
        num_scalar_prefetch=1, grid=(r // tm,),
        in_specs=[pl.BlockSpec((tm, HALF_COLS), lambda i, c_ref: (i, c_ref[0])), half], out_specs=[half, half])
    return pl.pallas_call(
        body, name=name, grid_spec=grid_spec,
        out_shape=[jax.ShapeDtypeStruct((r, HALF_COLS), F32), jax.ShapeDtypeStruct((r, HALF_COLS), BF16)],
        compiler_params=_params(1),
    )(c_idx, g, landed)


def _add_chips(chip_idx, h32, landed, *, tm, name):
    n = landed.shape[1]
    per = n // tm

    def body(chip_ref, h_ref, l_ref, o_ref):
        acc = h_ref[...]
        for j in range(3):
            acc = acc + l_ref[j].astype(F32)
        o_ref[...] = acc

    grid_spec = pltpu.PrefetchScalarGridSpec(
        num_scalar_prefetch=1, grid=(per,),
        in_specs=[pl.BlockSpec((tm, HALF_COLS), lambda i, chip_ref: (chip_ref[0] * per + i, 0)),
                  pl.BlockSpec((3, tm, HALF_COLS), lambda i, chip_ref: (0, i, 0))],
        out_specs=pl.BlockSpec((tm, HALF_COLS), lambda i, chip_ref: (i, 0)))
    return pl.pallas_call(
        body, name=name, grid_spec=grid_spec, out_shape=jax.ShapeDtypeStruct((n, HALF_COLS), F32),
        compiler_params=_params(1),
    )(chip_idx, h32, landed)


def _pack_small(small, lb_logits, loss):
    def body(ng_ref, bg_ref, dlb_ref, lbl_ref, hgg_ref, qg_ref, kvg_ref, fg_ref, loss_ref, out_ref):
        out_ref[...] = jnp.zeros_like(out_ref)
        out_ref[:, 0:1024] = ng_ref[...]
        out_ref[:, 1024:3072] = bg_ref[...]
        _, p0p1 = _lower_bound(lbl_ref[...])
        dl0 = dlb_ref[...] * p0p1
        out_ref[:, 3072:4096] = dl0
        out_ref[:, 4096:5120] = -dl0
        hgg = hgg_ref[0]
        for h in range(1, HEADS):
            hgg = hgg + hgg_ref[h]
        out_ref[:, 5120:5248] = hgg
        out_ref[:, 5248:5632] = qg_ref[...]
        out_ref[:, 5632:5888] = kvg_ref[...]
        out_ref[:, 5888:6912] = fg_ref[...]
        out_ref[:, 6912:7040] = loss_ref[...]

    return pl.pallas_call(
        body, name="pack_small", out_shape=jax.ShapeDtypeStruct((1, SMALL_COLS), F32),
    )(small["norm_g"], small["b_gate"], small["lb"], lb_logits, small["hg_norm_g"], small["q_a_g"],
      small["kv_a_g"], small["final_norm_g"], loss)


def _adamw_math(w, g, m, v):
    nm = ADAM_B1 * m + (1.0 - ADAM_B1) * g
    nv = ADAM_B2 * v + (1.0 - ADAM_B2) * (g * g)
    m_hat = nm / (1.0 - ADAM_B1 ** ADAM_STEP)
    v_hat = nv / (1.0 - ADAM_B2 ** ADAM_STEP)
    return -ADAM_LR * (m_hat / (jnp.sqrt(v_hat) + ADAM_EPS) + ADAM_WD * w), nm, nv


def _adamw(w, g, m, v, *, name, tm):
    r, cols = w.shape

    def body(w_ref, g_ref, m_ref, v_ref, d_ref, nm_ref, nv_ref):
        d_ref[...], nm_ref[...], nv_ref[...] = _adamw_math(w_ref[...], g_ref[...], m_ref[...], v_ref[...])

    row = pl.BlockSpec((tm, cols), lambda i: (i, 0))
    shp = jax.ShapeDtypeStruct((r, cols), F32)
    return pl.pallas_call(
        body, name=name, grid=(r // tm,), in_specs=[row] * 4, out_specs=[row] * 3, out_shape=[shp] * 3,
        compiler_params=_params(1),
    )(w, g, m, v)


SMALL_SLOTS = (("norm_g", (0,)), ("b_gate", (1024,)), ("lb_logits", (3072, 4096)), ("hg_norm_g", (5120,)),
               ("q_a_g", (5248,)), ("kv_a_g", (5632,)), ("final_norm_g", (5888,)))
LOSS_SLOT = 6912


def _small_update(gathered, ws, ms, vs):
    n = len(SMALL_SLOTS)

    def body(*refs):
        g_ref = refs[0]
        w_refs, m_refs, v_refs = refs[1:1 + n], refs[1 + n:1 + 2 * n], refs[1 + 2 * n:1 + 3 * n]
        outs = refs[1 + 3 * n:]
        loss_ref = outs[0]
        g_out, d_out, nm_out, nv_out = (outs[1 + k * n:1 + (k + 1) * n] for k in range(4))
        total = g_ref[0]
        for dev in range(1, N_DEV):
            total = total + g_ref[dev]
        loss_ref[...] = total[:, LOSS_SLOT:LOSS_SLOT + 128]
        for p, (_, offsets) in enumerate(SMALL_SLOTS):
            cols = w_refs[p].shape[1]
            for r, off in enumerate(offsets):
                rows = slice(r, r + 1)
                g = total[:, off:off + cols]
                g_out[p][rows, :] = g
                d_out[p][rows, :], nm_out[p][rows, :], nv_out[p][rows, :] = _adamw_math(
                    w_refs[p][rows, :], g, m_refs[p][rows, :], v_refs[p][rows, :])

    shapes = [jax.ShapeDtypeStruct(w.shape, F32) for w in ws]
    res = pl.pallas_call(
        body, name="small_update", out_shape=[jax.ShapeDtypeStruct((1, 128), F32)] + shapes * 4,
    )(gathered, *ws, *ms, *vs)
    return res[0], res[1:1 + n], res[1 + n:1 + 2 * n], res[1 + 2 * n:1 + 3 * n], res[1 + 3 * n:1 + 4 * n]


def kernel(x, norm_g, w_in, b_gate, lb_logits, hg_norm_g, q_a_g, w_uq, kv_a_g, w_ukv, w_proj_a, w_proj_b, w_out, final_norm_g, loss_target, m_norm_g, m_w_in, m_b_gate, m_lb_logits, m_hg_norm_g, m_q_a_g, m_w_uq, m_kv_a_g, m_w_ukv, m_w_proj_a, m_w_proj_b, m_w_out, m_final_norm_g, v_norm_g, v_w_in, v_b_gate, v_lb_logits, v_hg_norm_g, v_q_a_g, v_w_uq, v_kv_a_g, v_w_ukv, v_w_proj_a, v_w_proj_b, v_w_out, v_final_norm_g):
    c_idx = lax.axis_index("c").astype(jnp.int32).reshape(1)
    chip_idx = (2 * lax.axis_index("x") + lax.axis_index("y")).astype(jnp.int32).reshape(1)

    w_blk = w_in[0].T.astype(BF16)
    r_blk = _pack_rest(w_uq[0], w_ukv[0], w_proj_a[0], w_proj_b[0], w_out[0]).astype(BF16)
    gw = _gather_weights(w_blk)

    loss, dh_args, d_w_int, d_wuq3, d_wukv3, d_proj, small = _local_step(
        x[0], loss_target[0], gw.reshape(W_IN_COLS, D_MODEL), r_blk,
        norm_g, b_gate, lb_logits, hg_norm_g, q_a_g, kv_a_g, final_norm_g.reshape(1, D_MODEL))

    d_rest = _pack_rest_grads(d_wuq3, d_wukv3, d_proj)
    lw, lr = _swap_halves(d_w_int, d_rest)
    hw32, hw16 = _add_cores(c_idx, d_w_int, lw, tm=656, name="grad_add_cores_w")
    hr32, hr16 = _add_cores(c_idx, d_rest.reshape(N_CHIPS * REST_ROWS, D_MODEL), lr.reshape(N_CHIPS * REST_ROWS, HALF_COLS),
                            tm=REST_ROWS, name="grad_add_cores_r")
    grad_x, small["norm_g"], landed_w, landed_r = _dh_fused(*dh_args, hw16, hr16)
    rw = _add_chips(chip_idx, hw32, landed_w, tm=656, name="grad_add_chips_w")
    rr = _add_chips(chip_idx, hr32, landed_r, tm=208, name="grad_add_chips_r")
    tw, tr = _swap_reduced(rw, rr)
    g_w_in = _join_cols(rw, tw).T
    g_rest = _join_cols(rr, tr)
    g_uq, g_ukv, g_pa, g_pb, g_out = _unpack_rest(g_rest)

    small_all = _gather_small(_pack_small(small, lb_logits, loss))

    upd = {
        "w_in": _adamw(w_in[0], g_w_in, m_w_in[0], v_w_in[0], name="adamw_w_in", tm=128),
        "w_uq": _adamw(w_uq[0], g_uq, m_w_uq[0], v_w_uq[0], name="adamw_w_uq", tm=Q_LORA),
        "w_ukv": _adamw(w_ukv[0], g_ukv, m_w_ukv[0], v_w_ukv[0], name="adamw_w_ukv", tm=KV_LORA),
        "w_proj_a": _adamw(w_proj_a[0], g_pa, m_w_proj_a[0], v_w_proj_a[0], name="adamw_w_proj_a", tm=256),
        "w_proj_b": _adamw(w_proj_b[0], g_pb, m_w_proj_b[0], v_w_proj_b[0], name="adamw_w_proj_b", tm=256),
        "w_out": _adamw(w_out[0], g_out, m_w_out[0], v_w_out[0], name="adamw_w_out", tm=256),
    }
    loss_vec, *small_sets = _small_update(
        small_all,
        [norm_g, b_gate, lb_logits, hg_norm_g, q_a_g, kv_a_g, final_norm_g.reshape(1, D_MODEL)],
        [m_norm_g, m_b_gate, m_lb_logits, m_hg_norm_g, m_q_a_g, m_kv_a_g, m_final_norm_g.reshape(1, D_MODEL)],
        [v_norm_g, v_b_gate, v_lb_logits, v_hg_norm_g, v_q_a_g, v_kv_a_g, v_final_norm_g.reshape(1, D_MODEL)])

    def outputs(big, small_set):
        s_ng, s_bg, s_lb, s_hg, s_qg, s_kvg, s_fg = small_set
        return (s_ng, big["w_in"][None], s_bg, s_lb, s_hg, s_qg, big["w_uq"][None], s_kvg, big["w_ukv"][None],
                big["w_proj_a"][None], big["w_proj_b"][None], big["w_out"][None], s_fg.reshape(D_MODEL))

    grads = {"w_in": g_w_in, "w_uq": g_uq, "w_ukv": g_ukv, "w_proj_a": g_pa, "w_proj_b": g_pb, "w_out": g_out}
    return (loss_vec[0, 0], grad_x[None], *outputs(grads, small_sets[0]),
            *(o for k in range(3) for o in outputs({n: u[k] for n, u in upd.items()}, small_sets[1 + k])))
```

```python
import functools

import jax
import jax.numpy as jnp
from jax import lax
from jax.experimental import pallas as pl
from jax.experimental.pallas import tpu as pltpu

F32 = jnp.float32
BF16 = jnp.bfloat16

D_MODEL = 1024
HEADS = 8
HEAD_DIM = 128
HG_CHUNK = 32
CHUNK_SHIFT = 5
HEAD_SHIFT = 7
QK_NOPE = 128
QK_ROPE = 64
QK_DIM = QK_NOPE + QK_ROPE
QK_PAD = 256
Q_LORA = 384
KV_LORA = 256
MS_COLS = 768
ROPE_THETA = 10000.0
EPS = 1e-6
ATT_SCALE = QK_DIM ** -0.5
LOG2E = 1.4426950408889634
LN2 = 0.6931471805599453
Q_PRESCALE = ATT_SCALE * LOG2E

ADAM_LR = 0.001
ADAM_B1 = 0.9
ADAM_B2 = 0.999
ADAM_EPS = 1e-08
ADAM_WD = 0.01
ADAM_STEP = 10

N_CHIPS = 4
N_DEV = 8
W_IN_COLS = 7872
W_IN_BLK = W_IN_COLS // N_CHIPS
REST_ROWS = 144 + 128 + 3 * 256
SMALL_COLS = 7168

TM_MM = 1024
TM_FUSED = 256
TM_MLA = 512
HG_ROWS = 128
TQ = 512
FLASH_HEADS = 2
HG_HEADS = 8
VMEM_LIMIT = 56 * 1024 * 1024


def _dot(a, b):
    return lax.dot_general(a, b, (((1,), (0,)), ((), ())), preferred_element_type=F32)


def _dot_nt(a, b):
    return lax.dot_general(a, b, (((1,), (1,)), ((), ())), preferred_element_type=F32)


def _dot_tn(a, b):
    return lax.dot_general(a, b, (((0,), (0,)), ((), ())), preferred_element_type=F32)


def _params(n_axes):
    return pltpu.CompilerParams(dimension_semantics=("arbitrary",) * n_axes, vmem_limit_bytes=VMEM_LIMIT)


def _rms(x, g):
    r = lax.rsqrt(jnp.mean(x * x, axis=-1, keepdims=True) + EPS)
    return x * r * g


def _rms_bwd(x, g, dy):
    r = lax.rsqrt(jnp.mean(x * x, axis=-1, keepdims=True) + EPS)
    xh = x * r
    dyg = dy * g
    dx = r * (dyg - xh * jnp.mean(dyg * xh, axis=-1, keepdims=True))
    return dx, dy * xh


def _silu_parts(z):
    s = jax.nn.sigmoid(z)
    return z * s, s * (1.0 + z * (1.0 - s))


def _rope(x, c, sa, sb):
    return x * c + pltpu.roll(x, 32, 1) * sa + pltpu.roll(x, 96, 1) * sb


def _rope_bwd(dy, c, sa, sb):
    return dy * c + pltpu.roll(dy * sa, 96, 1) + pltpu.roll(dy * sb, 32, 1)


def _rope_tables(seq):
    inv = ROPE_THETA ** (-jnp.arange(0, QK_ROPE, 2, dtype=F32) / QK_ROPE)
    ang = jnp.arange(seq, dtype=F32)[:, None] * inv[None, :]
    cos, sin = jnp.cos(ang), jnp.sin(ang)
    z32 = jnp.zeros_like(cos)
    z64 = jnp.zeros((seq, 64), F32)
    c = jnp.concatenate([cos, cos, z64], axis=1)
    sa = jnp.concatenate([z32, sin, z64], axis=1)
    sb = jnp.concatenate([-sin, z32, z64], axis=1)
    return c, sa, sb


def _mm_tn(a, b, *, name, tm=TM_MM, tn=1024):
    flat = a.ndim == 2
    if flat:
        a = a[None]
    g, m, k = a.shape
    n = b.shape[1]
    tm, tn = min(tm, m), min(tn, n)
    assert m % tm == 0 and n % tn == 0

    def body(a_ref, b_ref, o_ref):
        @pl.when(pl.program_id(2) == 0)
        def _():
            o_ref[...] = jnp.zeros_like(o_ref)

        o_ref[...] += _dot_tn(a_ref[...], b_ref[...])

    out = pl.pallas_call(
        body, name=name, grid=(g, n // tn, m // tm),
        in_specs=[pl.BlockSpec((None, tm, k), lambda s, j, i: (s, i, 0)),
                  pl.BlockSpec((tm, tn), lambda s, j, i: (i, j))],
        out_specs=pl.BlockSpec((None, k, tn), lambda s, j, i: (s, 0, j)),
        out_shape=jax.ShapeDtypeStruct((g, k, n), F32), compiler_params=_params(3),
    )(a, b)
    return out[0] if flat else out


def _chunk_rows(rows):
    return lax.broadcasted_iota(jnp.int32, (rows, HEAD_DIM), 0) & (HG_CHUNK - 1)


def _chunk_cumsum(x, rows):
    pos = _chunk_rows(rows)
    shift = 1
    while shift < HG_CHUNK:
        x = x + jnp.where(pos >= shift, pltpu.roll(x, shift, 0), 0.0)
        shift *= 2
    return x


def _chunk_revcumsum(x, rows):
    pos = _chunk_rows(rows)
    shift = 1
    while shift < HG_CHUNK:
        x = x + jnp.where(pos + shift < HG_CHUNK, pltpu.roll(x, rows - shift, 0), 0.0)
        shift *= 2
    return x


def _chunk_last(x, rows):
    x3 = x.reshape(rows // HG_CHUNK, HG_CHUNK, HEAD_DIM)
    return jnp.broadcast_to(x3[:, HG_CHUNK - 1:HG_CHUNK, :], x3.shape).reshape(rows, HEAD_DIM)


def _lower_bound(lbl):
    mx = jnp.maximum(lbl[0:1, :], lbl[1:2, :])
    e0 = jnp.exp(lbl[0:1, :] - mx)
    e1 = jnp.exp(lbl[1:2, :] - mx)
    p0 = e0 / (e0 + e1)
    return p0, p0 * (e1 / (e0 + e1))


def _hg_masks(rows, nch, tmask_s, bdmask_s):
    r = lax.broadcasted_iota(jnp.int32, (rows, rows), 0)
    c = lax.broadcasted_iota(jnp.int32, (rows, rows), 1)
    tmask_s[...] = jnp.where(((r >> CHUNK_SHIFT) == (c >> CHUNK_SHIFT)) & (r >= c), 1.0, 0.0)
    r = lax.broadcasted_iota(jnp.int32, (rows, nch * HEAD_DIM), 0)
    c = lax.broadcasted_iota(jnp.int32, (rows, nch * HEAD_DIM), 1)
    bdmask_s[...] = jnp.where((r >> CHUNK_SHIFT) == (c >> HEAD_SHIFT), 1.0, 0.0).astype(BF16)


def _block_diag(x, nch, bdmask):
    return jnp.tile(x, (1, nch)) * bdmask


def _hgrn_fwd(hg, lb_logits, norm_g):
    s = hg.shape[1]
    rows = min(HG_ROWS, s)
    nblk = s // rows
    nch = rows // HG_CHUNK

    def body(hg_ref, lbl_ref, g_ref, o_ref, ya_ref, st0_ref, st_s, stall_s, tmask_s, bdmask_s):
        @pl.when(pl.program_id(1) == 0)
        def _():
            st_s[...] = jnp.zeros_like(st_s)
            _hg_masks(rows, nch, tmask_s, bdmask_s)

        bdmask = bdmask_s[...]
        tmask = tmask_s[...] > 0.5
        for hh in range(HG_HEADS):
            hc = slice(hh * HEAD_DIM, (hh + 1) * HEAD_DIM)
            hq = hg_ref[0, :, hc]
            hf = hg_ref[1, :, hc]
            hi = hg_ref[2, :, hc]
            hz = hg_ref[3, :, hc]
            lb, _ = _lower_bound(lbl_ref[:, hc])
            f = lb + (1.0 - lb) * jax.nn.sigmoid(hf)
            q = hq * jax.nn.sigmoid(hq)
            k = 1.0 - f
            logf = jnp.log(f)
            b = _chunk_cumsum(logf, rows)
            q_in = (q * jnp.exp(b)).astype(BF16)
            k_in = (k * jnp.exp(-b)).astype(BF16)
            k_out = (k * jnp.exp(_chunk_last(b, rows) - b)).astype(BF16)
            vb = hi.astype(BF16)

            sc = jnp.where(tmask, _dot_nt(q_in, k_in), 0.0)
            o_intra = _dot(sc.astype(BF16), vb)
            kvt = _dot_tn(vb, _block_diag(k_out, nch, bdmask))
            st = st_s[hh]
            st0_ref[hh] = st
            for c in range(nch):
                cols = slice(c * HEAD_DIM, (c + 1) * HEAD_DIM)
                last = (c + 1) * HG_CHUNK - 1
                stall_s[hh, :, cols] = st.astype(BF16)
                st = st * jnp.exp(b[last:last + 1, :]) + kvt[:, cols]
            st_s[hh] = st
            o = o_intra + _dot_nt(_block_diag(q_in, nch, bdmask), stall_s[hh])
            o_ref[:, hc] = o
            silu_z, _ = _silu_parts(hz)
            ya_ref[:, hc] = (_rms(o, g_ref[...]) * silu_z).astype(BF16)

    nh = HG_HEADS
    return pl.pallas_call(
        body, name="hgrn_fwd", grid=(HEADS // nh, nblk),
        in_specs=[pl.BlockSpec((4, rows, nh * HEAD_DIM), lambda h, i: (0, i, h)),
                  pl.BlockSpec((2, nh * HEAD_DIM), lambda h, i: (0, h)),
                  pl.BlockSpec((1, HEAD_DIM), lambda h, i: (0, 0))],
        out_specs=[pl.BlockSpec((rows, nh * HEAD_DIM), lambda h, i: (i, h)),
                   pl.BlockSpec((rows, nh * HEAD_DIM), lambda h, i: (i, h)),
                   pl.BlockSpec((nh, None, HEAD_DIM, HEAD_DIM), lambda h, i: (h, i, 0, 0))],
        out_shape=[jax.ShapeDtypeStruct((s, D_MODEL), F32), jax.ShapeDtypeStruct((s, D_MODEL), BF16),
                   jax.ShapeDtypeStruct((HEADS, nblk, HEAD_DIM, HEAD_DIM), F32)],
        scratch_shapes=[pltpu.VMEM((nh, HEAD_DIM, HEAD_DIM), F32), pltpu.VMEM((nh, HEAD_DIM, nch * HEAD_DIM), BF16),
                        pltpu.VMEM((rows, rows), F32), pltpu.VMEM((rows, nch * HEAD_DIM), BF16)],
        compiler_params=_params(2),
    )(hg, lb_logits, norm_g)


def _hgrn_bwd(hg, o_pre, dya, st0, lb_logits, norm_g):
    s = hg.shape[1]
    rows = min(HG_ROWS, s)
    nblk = s // rows
    nch = rows // HG_CHUNK

    def body(hg_ref, o_ref, dya_ref, st0_ref, lbl_ref, g_ref, dhg_ref, dlb_ref, dg_ref,
             dst_s, stp_s, stp_rows_s, dst_rows_s, dst_lane_s, dbl_s, tmask_s, bdmask_s):
        @pl.when(pl.program_id(1) == 0)
        def _():
            dst_s[...] = jnp.zeros_like(dst_s)
            dlb_ref[...] = jnp.zeros_like(dlb_ref)
            dg_ref[...] = jnp.zeros_like(dg_ref)
            _hg_masks(rows, nch, tmask_s, bdmask_s)

        bdmask = bdmask_s[...]
        tmask = tmask_s[...] > 0.5
        g = g_ref[...]
        for hh in range(HG_HEADS):
            hc = slice(hh * HEAD_DIM, (hh + 1) * HEAD_DIM)
            hq = hg_ref[0, :, hc]
            hf = hg_ref[1, :, hc]
            hi = hg_ref[2, :, hc]
            hz = hg_ref[3, :, hc]
            lb, _ = _lower_bound(lbl_ref[:, hc])
            sg = jax.nn.sigmoid(hf)
            f = lb + (1.0 - lb) * sg
            q, dsilu_q = _silu_parts(hq)
            k = 1.0 - f
            logf = jnp.log(f)
            b = _chunk_cumsum(logf, rows)
            eb = jnp.exp(b)
            enb = jnp.exp(-b)
            ebl = jnp.exp(_chunk_last(b, rows) - b)
            q_in32 = q * eb
            k_in32 = k * enb
            k_out32 = k * ebl
            q_in = q_in32.astype(BF16)
            k_in = k_in32.astype(BF16)
            k_out = k_out32.astype(BF16)
            vb = hi.astype(BF16)
            kbd = _block_diag(k_out, nch, bdmask)
            qbd = _block_diag(q_in, nch, bdmask)
            decs = [jnp.exp(b[(c + 1) * HG_CHUNK - 1:(c + 1) * HG_CHUNK, :]) for c in range(nch)]

            kvt = _dot_tn(vb, kbd)
            st = st0_ref[hh]
            for c in range(nch):
                stp_s[hh, c] = st
                stp_rows_s[hh, c * HEAD_DIM:(c + 1) * HEAD_DIM, :] = st.astype(BF16)
                st = st * decs[c] + kvt[:, c * HEAD_DIM:(c + 1) * HEAD_DIM]

            o = o_ref[:, hc]
            rstd = lax.rsqrt(jnp.mean(o * o, axis=-1, keepdims=True) + EPS)
            oh = o * rstd
            silu_z, dsilu_z = _silu_parts(hz)
            dya_v = dya_ref[:, hc]
            dn = dya_v * silu_z
            dhz = dya_v * (oh * g) * dsilu_z
            dg_ref[hh] += jnp.sum(dn * oh, axis=0, keepdims=True)
            doh = dn * g
            do = (rstd * (doh - oh * jnp.mean(doh * oh, axis=-1, keepdims=True))).astype(BF16)

            dq_all = _dot_tn(do, qbd)
            dst = dst_s[hh]
            ddecs = [None] * nch
            for c in reversed(range(nch)):
                dstb = dst.astype(BF16)
                dst_lane_s[hh, :, c * HEAD_DIM:(c + 1) * HEAD_DIM] = dstb
                dst_rows_s[hh, c * HEAD_DIM:(c + 1) * HEAD_DIM, :] = dstb
                ddecs[c] = jnp.sum(dst * stp_s[hh, c], axis=0, keepdims=True) * decs[c]
                dst = dst * decs[c] + dq_all[:, c * HEAD_DIM:(c + 1) * HEAD_DIM]
            dst_s[hh] = dst

            sc = jnp.where(tmask, _dot_nt(q_in, k_in), 0.0).astype(BF16)
            dkout = _dot(_block_diag(vb, nch, bdmask), dst_rows_s[hh])
            dv = _dot_nt(kbd, dst_lane_s[hh]) + _dot_tn(sc, do)
            dsc = jnp.where(tmask, _dot_nt(do, vb), 0.0).astype(BF16)
            dqin = _dot(dsc, k_in) + _dot(_block_diag(do, nch, bdmask), stp_rows_s[hh])
            dkin = _dot_tn(dsc, q_in)

            dko = dkout * k_out32
            for c in range(nch):
                sl = slice(c * HG_CHUNK, (c + 1) * HG_CHUNK)
                dbl = jnp.sum(dko[sl], axis=0, keepdims=True) + ddecs[c]
                dbl_s[hh, sl, :] = jnp.broadcast_to(dbl, (HG_CHUNK, HEAD_DIM))
            dq = dqin * eb
            dk = dkin * enb + dkout * ebl
            db = dqin * q_in32 - dkin * k_in32 - dko
            dlogf = _chunk_revcumsum(db, rows) + dbl_s[hh]
            df = dlogf / f - dk
            dlb_ref[:, hc] += jnp.sum(df * (1.0 - sg), axis=0, keepdims=True)
            dhg_ref[0, :, hc] = (dq * dsilu_q).astype(BF16)
            dhg_ref[1, :, hc] = (df * (1.0 - lb) * sg * (1.0 - sg)).astype(BF16)
            dhg_ref[2, :, hc] = dv.astype(BF16)
            dhg_ref[3, :, hc] = dhz.astype(BF16)

    last = nblk - 1
    nh = HG_HEADS
    wide = nh * HEAD_DIM
    return pl.pallas_call(
        body, name="hgrn_bwd", grid=(HEADS // nh, nblk),
        in_specs=[pl.BlockSpec((4, rows, wide), lambda h, i: (0, last - i, h)),
                  pl.BlockSpec((rows, wide), lambda h, i: (last - i, h)),
                  pl.BlockSpec((rows, wide), lambda h, i: (last - i, h)),
                  pl.BlockSpec((nh, None, HEAD_DIM, HEAD_DIM), lambda h, i: (h, last - i, 0, 0)),
                  pl.BlockSpec((2, wide), lambda h, i: (0, h)),
                  pl.BlockSpec((1, HEAD_DIM), lambda h, i: (0, 0))],
        out_specs=[pl.BlockSpec((4, rows, wide), lambda h, i: (0, last - i, h)),
                   pl.BlockSpec((1, wide), lambda h, i: (0, h)),
                   pl.BlockSpec((nh, 1, HEAD_DIM), lambda h, i: (h, 0, 0))],
        out_shape=[jax.ShapeDtypeStruct((4, s, D_MODEL), BF16), jax.ShapeDtypeStruct((1, D_MODEL), F32),
                   jax.ShapeDtypeStruct((HEADS, 1, HEAD_DIM), F32)],
        scratch_shapes=[pltpu.VMEM((nh, HEAD_DIM, HEAD_DIM), F32), pltpu.VMEM((nh, nch, HEAD_DIM, HEAD_DIM), F32),
                        pltpu.VMEM((nh, nch * HEAD_DIM, HEAD_DIM), BF16), pltpu.VMEM((nh, nch * HEAD_DIM, HEAD_DIM), BF16),
                        pltpu.VMEM((nh, HEAD_DIM, nch * HEAD_DIM), BF16), pltpu.VMEM((nh, rows, HEAD_DIM), F32),
                        pltpu.VMEM((rows, rows), F32), pltpu.VMEM((rows, nch * HEAD_DIM), BF16)],
        compiler_params=_params(2),
    )(hg, o_pre, dya, st0, lb_logits, norm_g)


def _mla_pre(ms, q_a_g, kv_a_g, wuq3, wukv3, tabs):
    s = ms.shape[0]
    tm = min(TM_MLA, s)

    def body(ms_ref, qg_ref, kvg_ref, wuq_ref, wukv_ref, c_ref, sa_ref, sb_ref,
             q_ref, k_ref, v_ref, cqn_ref, ckvn_ref):
        c, sa, sb = c_ref[...], sa_ref[...], sb_ref[...]
        cqn = _rms(ms_ref[:, 0:Q_LORA], qg_ref[...]).astype(BF16)
        ckvn = _rms(ms_ref[:, Q_LORA:Q_LORA + KV_LORA], kvg_ref[...]).astype(BF16)
        cqn_ref[...] = cqn
        ckvn_ref[...] = ckvn
        k_pe = _rope(ms_ref[:, Q_LORA + KV_LORA:MS_COLS], c, sa, sb).astype(BF16)
        for h in range(HEADS):
            qh = _dot(cqn, wuq_ref[h])
            q_ref[h, :, 0:128] = (qh[:, 0:128] * Q_PRESCALE).astype(BF16)
            q_ref[h, :, 128:256] = (_rope(qh[:, 128:256], c, sa, sb) * Q_PRESCALE).astype(BF16)
            kvh = _dot(ckvn, wukv_ref[h])
            k_ref[h, :, 0:128] = kvh[:, 0:128].astype(BF16)
            k_ref[h, :, 128:256] = k_pe
            v_ref[h] = kvh[:, 128:256].astype(BF16)

    tab = pl.BlockSpec((tm, 128), lambda i: (i, 0))
    return pl.pallas_call(
        body, name="mla_pre", grid=(s // tm,),
        in_specs=[pl.BlockSpec((tm, MS_COLS), lambda i: (i, 0)),
                  pl.BlockSpec((1, Q_LORA), lambda i: (0, 0)), pl.BlockSpec((1, KV_LORA), lambda i: (0, 0)),
                  pl.BlockSpec((HEADS, Q_LORA, QK_PAD), lambda i: (0, 0, 0)),
                  pl.BlockSpec((HEADS, KV_LORA, 256), lambda i: (0, 0, 0)), tab, tab, tab],
        out_specs=[pl.BlockSpec((HEADS, tm, QK_PAD), lambda i: (0, i, 0)),
                   pl.BlockSpec((HEADS, tm, QK_PAD), lambda i: (0, i, 0)),
                   pl.BlockSpec((HEADS, tm, HEAD_DIM), lambda i: (0, i, 0)),
                   pl.BlockSpec((tm, Q_LORA), lambda i: (i, 0)), pl.BlockSpec((tm, KV_LORA), lambda i: (i, 0))],
        out_shape=[jax.ShapeDtypeStruct((HEADS, s, QK_PAD), BF16), jax.ShapeDtypeStruct((HEADS, s, QK_PAD), BF16),
                   jax.ShapeDtypeStruct((HEADS, s, HEAD_DIM), BF16),
                   jax.ShapeDtypeStruct((s, Q_LORA), BF16), jax.ShapeDtypeStruct((s, KV_LORA), BF16)],
        compiler_params=_params(1),
    )(ms, q_a_g, kv_a_g, wuq3, wukv3, *tabs)


def _causal_mask(t):
    r = lax.broadcasted_iota(jnp.int32, (t, t), 0)
    c = lax.broadcasted_iota(jnp.int32, (t, t), 1)
    return r >= c


def _flash_fwd(q, k, v, mz):
    s = q.shape[1]
    t = min(TQ, s)

    def body(q_ref, k_ref, v_ref, mz_ref, o_ref, yb_ref, lse_ref, m_s, l_s, acc_s):
        i = pl.program_id(1)
        m_s[...] = jnp.full_like(m_s, -jnp.inf)
        l_s[...] = jnp.zeros_like(l_s)
        acc_s[...] = jnp.zeros_like(acc_s)

        def step(j, groups):
            rows = pl.ds(pl.multiple_of(j * t, t), t)
            for hh in range(FLASH_HEADS):
                for r0, nr, masked in groups:
                    r = slice(r0, r0 + nr)
                    sc = _dot_nt(q_ref[hh, r, :], k_ref[hh, rows, :])
                    if masked:
                        sc = jnp.where(_causal_mask(t), sc, -jnp.inf)
                    m_prev = m_s[hh, r, :]
                    m_new = jnp.maximum(m_prev, jnp.max(sc, axis=-1, keepdims=True))
                    p = jnp.exp2(sc - jnp.tile(m_new, (1, t // 128)))
                    alpha = jnp.exp2(m_prev - m_new)
                    l_s[hh, r, :] = alpha * l_s[hh, r, :] + jnp.sum(p, axis=-1, keepdims=True)
                    acc_s[hh, r, :] = alpha * acc_s[hh, r, :] + _dot(p.astype(BF16), v_ref[hh, rows, :])
                    m_s[hh, r, :] = m_new

        def loop_body(jj, carry):
            for u in range(4):
                step(4 * jj + u, ((0, 2 * t, False),))
            return carry

        lax.fori_loop(0, i // 2, loop_body, 0)

        @pl.when(i % 2 == 1)
        def _():
            step(2 * i - 2, ((0, 2 * t, False),))
            step(2 * i - 1, ((0, 2 * t, False),))

        step(2 * i, ((0, t, True), (t, t, False)))
        step(2 * i + 1, ((t, t, True),))
        for hh in range(FLASH_HEADS):
            cols = slice(hh * HEAD_DIM, (hh + 1) * HEAD_DIM)
            out = acc_s[hh] / l_s[hh]
            o_ref[:, cols] = out
            silu_z, _ = _silu_parts(mz_ref[:, cols])
            yb_ref[:, cols] = (out * silu_z).astype(BF16)
            lse_ref[hh] = m_s[hh] + jnp.log2(l_s[hh])

    nh = FLASH_HEADS
    t2 = 2 * t
    col = pl.BlockSpec((t2, nh * HEAD_DIM), lambda h, i: (i, h))
    return pl.pallas_call(
        body, name="flash_fwd", grid=(HEADS // nh, s // t2),
        in_specs=[pl.BlockSpec((nh, t2, QK_PAD), lambda h, i: (h, i, 0)),
                  pl.BlockSpec((nh, s, QK_PAD), lambda h, i: (h, 0, 0)),
                  pl.BlockSpec((nh, s, HEAD_DIM), lambda h, i: (h, 0, 0)), col],
        out_specs=[col, col, pl.BlockSpec((nh, t2, 128), lambda h, i: (h, i, 0))],
        out_shape=[jax.ShapeDtypeStruct((s, D_MODEL), F32), jax.ShapeDtypeStruct((s, D_MODEL), BF16),
                   jax.ShapeDtypeStruct((HEADS, s, 128), F32)],
        scratch_shapes=[pltpu.VMEM((nh, t2, 128), F32), pltpu.VMEM((nh, t2, 128), F32),
                        pltpu.VMEM((nh, t2, HEAD_DIM), F32)],
        compiler_params=_params(2),
    )(q, k, v, mz)


def _flash_bwd(q, k, v, dyb, mz, o_att, lse, tabs):
    s = q.shape[1]
    t = min(TQ, s)

    def body(q_ref, k_ref, v_ref, dyb_ref, mz_ref, o_ref, lse_ref, c_ref, sa_ref, sb_ref,
             dq_ref, dk_ref, dv_ref, dmz_ref, dq_s, delta_s, do_s):
        i = pl.program_id(1)

        @pl.when(i == 0)
        def _():
            dk_ref[...] = jnp.zeros_like(dk_ref)
            dv_ref[...] = jnp.zeros_like(dv_ref)

        silu_z, dsilu_z = _silu_parts(mz_ref[...])
        dyb_v = dyb_ref[...]
        out = o_ref[...]
        do32 = dyb_v * silu_z
        dmz_ref[...] = (dyb_v * out * dsilu_z).astype(BF16)
        delta_s[...] = jnp.broadcast_to(jnp.sum(do32 * out, axis=-1, keepdims=True), (2 * t, 128))
        do_s[...] = do32.astype(BF16)
        dq_s[...] = jnp.zeros_like(dq_s)

        def step(j, modes):
            rows = pl.ds(pl.multiple_of(j * t, t), t)
            kj = k_ref[rows, :]
            vj = v_ref[rows, :]
            dv_acc = None
            dk_acc = None
            for ch, masked in enumerate(modes):
                if masked is None:
                    continue
                r = slice(ch * t, (ch + 1) * t)
                qv = q_ref[r, :]
                do = do_s[r, :]
                sc = _dot_nt(qv, kj)
                if masked:
                    sc = jnp.where(_causal_mask(t), sc, -jnp.inf)
                p = jnp.exp2(sc - jnp.tile(lse_ref[r, :], (1, t // 128)))
                dp = _dot_nt(do, vj)
                ds = (p * (dp - jnp.tile(delta_s[r, :], (1, t // 128)))).astype(BF16)
                dv_c = _dot_tn(p.astype(BF16), do)
                dk_c = _dot_tn(ds, qv)
                dv_acc = dv_c if dv_acc is None else dv_acc + dv_c
                dk_acc = dk_c if dk_acc is None else dk_acc + dk_c
                dq_s[r, :] += _dot(ds, kj)
            dv_ref[rows, :] += dv_acc
            dk_ref[rows, :] += dk_acc

        def loop_body(jj, carry):
            for u in range(4):
                step(4 * jj + u, (False, False))
            return carry

        lax.fori_loop(0, i // 2, loop_body, 0)

        @pl.when(i % 2 == 1)
        def _():
            step(2 * i - 2, (False, False))
            step(2 * i - 1, (False, False))

        step(2 * i, (True, False))
        step(2 * i + 1, (None, True))
        dq = dq_s[...] * ATT_SCALE
        dq_ref[:, 0:128] = dq[:, 0:128].astype(BF16)
        dq_ref[:, 128:256] = _rope_bwd(dq[:, 128:256], c_ref[...], sa_ref[...], sb_ref[...]).astype(BF16)

    t2 = 2 * t
    col = pl.BlockSpec((t2, HEAD_DIM), lambda h, i: (i, h))
    tab = pl.BlockSpec((t2, 128), lambda h, i: (i, 0))
    return pl.pallas_call(
        body, name="flash_bwd", grid=(HEADS, s // t2),
        in_specs=[pl.BlockSpec((None, t2, QK_PAD), lambda h, i: (h, i, 0)),
                  pl.BlockSpec((None, s, QK_PAD), lambda h, i: (h, 0, 0)),
                  pl.BlockSpec((None, s, HEAD_DIM), lambda h, i: (h, 0, 0)),
                  col, col, col, pl.BlockSpec((None, t2, 128), lambda h, i: (h, i, 0)), tab, tab, tab],
        out_specs=[pl.BlockSpec((None, t2, QK_PAD), lambda h, i: (h, i, 0)),
                   pl.BlockSpec((None, s, QK_PAD), lambda h, i: (h, 0, 0)),
                   pl.BlockSpec((None, s, HEAD_DIM), lambda h, i: (h, 0, 0)), col],
        out_shape=[jax.ShapeDtypeStruct((HEADS, s, QK_PAD), BF16), jax.ShapeDtypeStruct((HEADS, s, QK_PAD), F32),
                   jax.ShapeDtypeStruct((HEADS, s, HEAD_DIM), F32), jax.ShapeDtypeStruct((s, D_MODEL), BF16)],
        scratch_shapes=[pltpu.VMEM((t2, QK_PAD), F32), pltpu.VMEM((t2, 128), F32), pltpu.VMEM((t2, HEAD_DIM), BF16)],
        compiler_params=_params(2),
    )(q, k, v, dyb, mz, o_att, lse, *tabs)


def _mla_bwd_proj(dq, dk, dv, cqn, ckvn, ms, q_a_g, kv_a_g, wuq3, wukv3, tabs):
    s = ms.shape[0]
    tm = min(TM_MLA, s)

    def body(dq_ref, dk_ref, dv_ref, cqn_ref, ckvn_ref, ms_ref, qg_ref, kvg_ref, wuq_ref, wukv_ref,
             c_ref, sa_ref, sb_ref, dms_ref, dwuq_ref, dwukv_ref, dqg_ref, dkvg_ref):
        @pl.when(pl.program_id(0) == 0)
        def _():
            dwuq_ref[...] = jnp.zeros_like(dwuq_ref)
            dwukv_ref[...] = jnp.zeros_like(dwukv_ref)
            dqg_ref[...] = jnp.zeros_like(dqg_ref)
            dkvg_ref[...] = jnp.zeros_like(dkvg_ref)

        cqn = cqn_ref[...]
        ckvn = ckvn_ref[...]
        dcqn = jnp.zeros((tm, Q_LORA), F32)
        dckvn = jnp.zeros((tm, KV_LORA), F32)
        dkpe = jnp.zeros((tm, 128), F32)
        for h in range(HEADS):
            dqh = dq_ref[h]
            dcqn += _dot_nt(dqh, wuq_ref[h])
            dwuq_ref[h] += _dot_tn(cqn, dqh)
            dkh = dk_ref[h] * LN2
            dkvh = jnp.concatenate([dkh[:, 0:128], dv_ref[h]], axis=1).astype(BF16)
            dckvn += _dot_nt(dkvh, wukv_ref[h])
            dwukv_ref[h] += _dot_tn(ckvn, dkvh)
            dkpe += dkh[:, 128:256]
        dcq, dqg_rows = _rms_bwd(ms_ref[:, 0:Q_LORA], qg_ref[...], dcqn)
        dckv, dkvg_rows = _rms_bwd(ms_ref[:, Q_LORA:Q_LORA + KV_LORA], kvg_ref[...], dckvn)
        dqg_ref[...] += jnp.sum(dqg_rows, axis=0, keepdims=True)
        dkvg_ref[...] += jnp.sum(dkvg_rows, axis=0, keepdims=True)
        dms_ref[:, 0:Q_LORA] = dcq.astype(BF16)
        dms_ref[:, Q_LORA:Q_LORA + KV_LORA] = dckv.astype(BF16)
        dms_ref[:, Q_LORA + KV_LORA:MS_COLS] = _rope_bwd(dkpe, c_ref[...], sa_ref[...], sb_ref[...]).astype(BF16)

    tab = pl.BlockSpec((tm, 128), lambda i: (i, 0))
    wq = pl.BlockSpec((HEADS, Q_LORA, QK_PAD), lambda i: (0, 0, 0))
    wkv = pl.BlockSpec((HEADS, KV_LORA, 256), lambda i: (0, 0, 0))
    qg = pl.BlockSpec((1, Q_LORA), lambda i: (0, 0))
    kvg = pl.BlockSpec((1, KV_LORA), lambda i: (0, 0))
    return pl.pallas_call(
        body, name="mla_bwd_proj", grid=(s // tm,),
        in_specs=[pl.BlockSpec((HEADS, tm, QK_PAD), lambda i: (0, i, 0)),
                  pl.BlockSpec((HEADS, tm, QK_PAD), lambda i: (0, i, 0)),
                  pl.BlockSpec((HEADS, tm, HEAD_DIM), lambda i: (0, i, 0)),
                  pl.BlockSpec((tm, Q_LORA), lambda i: (i, 0)), pl.BlockSpec((tm, KV_LORA), lambda i: (i, 0)),
                  pl.BlockSpec((tm, MS_COLS), lambda i: (i, 0)), qg, kvg, wq, wkv, tab, tab, tab],
        out_specs=[pl.BlockSpec((tm, MS_COLS), lambda i: (i, 0)), wq, wkv, qg, kvg],
        out_shape=[jax.ShapeDtypeStruct((s, MS_COLS), BF16), jax.ShapeDtypeStruct((HEADS, Q_LORA, QK_PAD), F32),
                   jax.ShapeDtypeStruct((HEADS, KV_LORA, 256), F32),
                   jax.ShapeDtypeStruct((1, Q_LORA), F32), jax.ShapeDtypeStruct((1, KV_LORA), F32)],
        compiler_params=_params(1),
    )(dq, dk, dv, cqn, ckvn, ms, q_a_g, kv_a_g, wuq3, wukv3, *tabs)


def _merge_fused(ya, yb, glog, b_gate, x, tgt, fg, wproj):
    s = x.shape[0]
    tm = min(TM_FUSED, s)

    def body(ya_ref, yb_ref, g0_ref, g1_ref, b0_ref, b1_ref, x_ref, t_ref, fg_ref, w_ref,
             mg_ref, dx2_ref, dx2b_ref, dya_ref, dyb_ref, dgl_ref, dpa_ref, dpb_ref, loss_ref, dfg_ref, dbg_ref):
        @pl.when(pl.program_id(0) == 0)
        def _():
            loss_ref[...] = jnp.zeros_like(loss_ref)
            dfg_ref[...] = jnp.zeros_like(dfg_ref)
            dbg_ref[...] = jnp.zeros_like(dbg_ref)

        pa = _dot(ya_ref[...], w_ref[0])
        pb = _dot(yb_ref[...], w_ref[1])
        g0 = jax.nn.sigmoid(g0_ref[...] + b0_ref[...])
        g1 = jax.nn.sigmoid(g1_ref[...] + b1_ref[...])
        merged = (g0 * pa + g1 * pb).astype(BF16)
        mg_ref[...] = merged
        x2 = x_ref[...] + _dot(merged, w_ref[2])
        fg_v = fg_ref[...]
        err = _rms(x2, fg_v) - t_ref[...]
        loss_ref[...] += 0.5 * jnp.sum(jnp.mean(err * err, axis=-1, keepdims=True), axis=0, keepdims=True)
        dx2, dfg_rows = _rms_bwd(x2, fg_v, err * (1.0 / D_MODEL))
        dx2_ref[...] = dx2
        dfg_ref[...] += jnp.sum(dfg_rows, axis=0, keepdims=True)

        dx2b = dx2.astype(BF16)
        dx2b_ref[...] = dx2b
        dmg = _dot_nt(dx2b, w_ref[2])
        dpa = (dmg * g0).astype(BF16)
        dpb = (dmg * g1).astype(BF16)
        dpa_ref[...] = dpa
        dpb_ref[...] = dpb
        dgl0 = dmg * pa * g0 * (1.0 - g0)
        dgl1 = dmg * pb * g1 * (1.0 - g1)
        dgl_ref[:, 0:D_MODEL] = dgl0.astype(BF16)
        dgl_ref[:, D_MODEL:2 * D_MODEL] = dgl1.astype(BF16)
        dbg_ref[:, 0:D_MODEL] += jnp.sum(dgl0, axis=0, keepdims=True)
        dbg_ref[:, D_MODEL:2 * D_MODEL] += jnp.sum(dgl1, axis=0, keepdims=True)
        dya_ref[...] = _dot_nt(dpa, w_ref[0])
        dyb_ref[...] = _dot_nt(dpb, w_ref[1])

    row = pl.BlockSpec((tm, D_MODEL), lambda i: (i, 0))
    row1 = pl.BlockSpec((tm, D_MODEL), lambda i: (i, 1))
    row2 = pl.BlockSpec((tm, 2 * D_MODEL), lambda i: (i, 0))
    vec = pl.BlockSpec((1, D_MODEL), lambda i: (0, 0))
    vec1 = pl.BlockSpec((1, D_MODEL), lambda i: (0, 1))
    vec2 = pl.BlockSpec((1, 2 * D_MODEL), lambda i: (0, 0))
    f32_rows = jax.ShapeDtypeStruct((s, D_MODEL), F32)
    bf16_rows = jax.ShapeDtypeStruct((s, D_MODEL), BF16)
    return pl.pallas_call(
        body, name="merge_fused", grid=(s // tm,),
        in_specs=[row, row, row, row1, vec, vec1, row, row, vec, pl.BlockSpec((3, D_MODEL, D_MODEL), lambda i: (0, 0, 0))],
        out_specs=[row, row, row, row, row, row2, row, row, pl.BlockSpec((1, 128), lambda i: (0, 0)), vec, vec2],
        out_shape=[bf16_rows, f32_rows, bf16_rows, f32_rows, f32_rows, jax.ShapeDtypeStruct((s, 2 * D_MODEL), BF16),
                   bf16_rows, bf16_rows, jax.ShapeDtypeStruct((1, 128), F32), jax.ShapeDtypeStruct((1, D_MODEL), F32),
                   jax.ShapeDtypeStruct((1, 2 * D_MODEL), F32)],
        compiler_params=_params(1),
    )(ya, yb, glog, glog, b_gate, b_gate, x, tgt, fg, wproj)


def _proj_fused(x, g, w_int, r_blk):
    s = x.shape[0]
    tm = min(TM_FUSED, s)

    def body(x_ref, g_ref, w_hbm, r_ref, h_ref, hg_ref, ms_ref, mz_ref, gl_ref, or_ref, w_s, sem, send_sems, recv_sems):
        mx, my, mc = _me()
        chips = _other_chips(mx, my)
        mine, theirs = _cols(mc), _cols(1 - mc)

        def copy(k, src, dst, to):
            return pltpu.make_async_remote_copy(src_ref=src, dst_ref=dst, send_sem=send_sems.at[k],
                                                recv_sem=recv_sems.at[k], device_id=to, device_id_type=MESH_ID)

        def sends():
            return [copy(j, r_ref.at[:, mine], or_ref.at[2 * mx + my, :, mine], (cx, cy, mc))
                    for j, (cx, cy) in enumerate(chips)]

        @pl.when(pl.program_id(0) == 0)
        def _():
            for cp in sends():
                cp.start()
            cp = pltpu.make_async_copy(w_hbm, w_s, sem)
            cp.start()
            cp.wait()

        h = _rms(x_ref[...], g_ref[...]).astype(BF16)
        h_ref[...] = h
        for j in range(4):
            hg_ref[j] = _dot_nt(h, w_s[j * D_MODEL:(j + 1) * D_MODEL, :])
        ms = _dot_nt(h, w_s[4096:4096 + MS_COLS, :])
        lane = lax.broadcasted_iota(jnp.int32, ms.shape, 1)
        ms_ref[...] = jnp.where(lane < 704, ms, 0.0)
        mz_ref[...] = _dot_nt(h, w_s[4800:5824, :])
        for j in range(2):
            gl_ref[:, j * D_MODEL:(j + 1) * D_MODEL] = _dot_nt(h, w_s[5824 + j * D_MODEL:5824 + (j + 1) * D_MODEL, :])

        @pl.when(pl.program_id(0) == s // tm - 1)
        def _():
            passed = []
            for j, (cx, cy) in enumerate(chips):
                landed = or_ref.at[2 * cx + cy, :, mine]
                copy(j, landed, landed, (cx, cy, mc)).wait_recv()
                fwd = copy(3 + j, landed, landed, (mx, my, 1 - mc))
                fwd.start()
                passed.append(fwd)
            for j, (cx, cy) in enumerate(chips):
                other = or_ref.at[2 * cx + cy, :, theirs]
                copy(3 + j, other, other, (mx, my, 1 - mc)).wait_recv()
            for cp in sends() + passed:
                cp.wait_send()

    row = pl.BlockSpec((tm, D_MODEL), lambda i: (i, 0))
    outs = pl.pallas_call(
        body, name="proj_fused", grid=(s // tm,),
        in_specs=[row, pl.BlockSpec((1, D_MODEL), lambda i: (0, 0)), ANY, ANY],
        out_specs=[row, pl.BlockSpec((4, tm, D_MODEL), lambda i: (0, i, 0)), pl.BlockSpec((tm, MS_COLS), lambda i: (i, 0)),
                   row, pl.BlockSpec((tm, 2 * D_MODEL), lambda i: (i, 0)), ANY],
        out_shape=[jax.ShapeDtypeStruct((s, D_MODEL), BF16), jax.ShapeDtypeStruct((4, s, D_MODEL), F32),
                   jax.ShapeDtypeStruct((s, MS_COLS), F32), jax.ShapeDtypeStruct((s, D_MODEL), F32),
                   jax.ShapeDtypeStruct((s, 2 * D_MODEL), F32), jax.ShapeDtypeStruct((N_CHIPS,) + r_blk.shape, r_blk.dtype)],
        scratch_shapes=[pltpu.VMEM(w_int.shape, BF16), pltpu.SemaphoreType.DMA,
                        pltpu.SemaphoreType.DMA((6,)), pltpu.SemaphoreType.DMA((6,))],
        compiler_params=_params(1),
    )(x, g, w_int, r_blk)
    gr = lax.dynamic_update_slice(outs[5], r_blk[None], (2 * lax.axis_index("x") + lax.axis_index("y"), 0, 0))
    return (*outs[:5], gr)


def _dh_fused(dhg, dms, dmz, dglog, w_int, x, g, dx2, hw, hr):
    s = x.shape[0]
    tm = min(512, s)
    nw, nr = hw.shape[0] // N_CHIPS, hr.shape[0] // N_CHIPS

    def body(dhg_ref, dms_ref, dmz_ref, dgl_ref, w_hbm, x_ref, g_ref, dx2_ref, hw_ref, hr_ref,
             dx_ref, dg_ref, lw_ref, lr_ref, w_s, sem, send_sems, recv_sems):
        def scatter_copies():
            mx, my, mc = _me()
            return [pltpu.make_async_remote_copy(
                src_ref=src.at[pl.ds((2 * cx + cy) * n, n), :], dst_ref=dst.at[j], send_sem=send_sems.at[3 * a + j],
                recv_sem=recv_sems.at[3 * a + j], device_id=(cx, cy, mc), device_id_type=MESH_ID)
                for a, (src, dst, n) in enumerate([(hw_ref, lw_ref, nw), (hr_ref, lr_ref, nr)])
                for j, (cx, cy) in enumerate(_other_chips(mx, my))]

        @pl.when(pl.program_id(0) == 0)
        def _():
            for cp in scatter_copies():
                cp.start()
            dg_ref[...] = jnp.zeros_like(dg_ref)
            cp = pltpu.make_async_copy(w_hbm, w_s, sem)
            cp.start()
            cp.wait()

        dh = _dot(dms_ref[...], w_s[4096:4096 + MS_COLS, :]) + _dot(dmz_ref[...], w_s[4800:5824, :])
        for j in range(4):
            dh += _dot(dhg_ref[j], w_s[j * D_MODEL:(j + 1) * D_MODEL, :])
        for j in range(2):
            dh += _dot(dgl_ref[:, j * D_MODEL:(j + 1) * D_MODEL], w_s[5824 + j * D_MODEL:5824 + (j + 1) * D_MODEL, :])
        dx, dg_rows = _rms_bwd(x_ref[...], g_ref[...], dh)
        dx_ref[...] = dx + dx2_ref[...]
        dg_ref[...] += jnp.sum(dg_rows, axis=0, keepdims=True)

        @pl.when(pl.program_id(0) == s // tm - 1)
        def _():
            for cp in scatter_copies():
                cp.wait()

    row = pl.BlockSpec((tm, D_MODEL), lambda i: (i, 0))
    vec = pl.BlockSpec((1, D_MODEL), lambda i: (0, 0))
    return pl.pallas_call(
        body, name="dh_fused", grid=(s // tm,),
        in_specs=[pl.BlockSpec((4, tm, D_MODEL), lambda i: (0, i, 0)), pl.BlockSpec((tm, MS_COLS), lambda i: (i, 0)), row,
                  pl.BlockSpec((tm, 2 * D_MODEL), lambda i: (i, 0)), ANY, row, vec, row, ANY, ANY],
        out_specs=[row, vec, ANY, ANY],
        out_shape=[jax.ShapeDtypeStruct((s, D_MODEL), F32), jax.ShapeDtypeStruct((1, D_MODEL), F32),
                   jax.ShapeDtypeStruct((3, nw, HALF_COLS), hw.dtype), jax.ShapeDtypeStruct((3, nr, HALF_COLS), hr.dtype)],
        scratch_shapes=[pltpu.VMEM(w_int.shape, BF16), pltpu.SemaphoreType.DMA,
                        pltpu.SemaphoreType.DMA((6,)), pltpu.SemaphoreType.DMA((6,))],
        compiler_params=_params(1),
    )(dhg, dms, dmz, dglog, w_int, x, g, dx2, hw, hr)


def _local_step(x, tgt, w_int, r_blk, norm_g, b_gate, lb_logits, hg_norm_g, q_a_g, kv_a_g, fg):
    s = x.shape[0]
    tabs = _rope_tables(s)

    h, hg, ms, mz, glog, gr = _proj_fused(x, norm_g, w_int, r_blk)
    w_uq, w_ukv, wproj = _unpack_rest_weights(gr)
    wuq3 = jnp.pad(w_uq.reshape(Q_LORA, HEADS, QK_DIM).transpose(1, 0, 2), ((0, 0), (0, 0), (0, QK_PAD - QK_DIM)))
    wukv3 = w_ukv.reshape(KV_LORA, HEADS, 256).transpose(1, 0, 2)
    o_pre, ya, st0 = _hgrn_fwd(hg, lb_logits, hg_norm_g)
    q, k, v, cqn, ckvn = _mla_pre(ms, q_a_g, kv_a_g, wuq3, wukv3, tabs)
    o_att, yb, lse = _flash_fwd(q, k, v, mz)
    merged, dx2, dx2b, dya, dyb, dglog, dpa, dpb, loss, dfg, dbg = _merge_fused(ya, yb, glog, b_gate, x, tgt, fg, wproj)

    d_wout = _mm_tn(merged, dx2b, name="dw_out")
    d_wpa = _mm_tn(ya, dpa, name="dw_proj_a")
    d_wpb = _mm_tn(yb, dpb, name="dw_proj_b")
    dhg, dlb, dhgg = _hgrn_bwd(hg, o_pre, dya, st0, lb_logits, hg_norm_g)
    dq, dk, dv, dmz = _flash_bwd(q, k, v, dyb, mz, o_att, lse, tabs)
    dms, d_wuq3, d_wukv3, dqg, dkvg = _mla_bwd_proj(dq, dk, dv, cqn, ckvn, ms, q_a_g, kv_a_g, wuq3, wukv3, tabs)
    d_hg = _mm_tn(dhg, h, name="dw_in_hg")
    d_ms = _mm_tn(dms, h, name="dw_in_ms")
    d_mz = _mm_tn(dmz, h, name="dw_in_mz")
    d_gl = _mm_tn(dglog, h, name="dw_in_gate")
    d_w_int = jnp.concatenate([d_hg.reshape(4 * D_MODEL, D_MODEL), d_ms[0:704], d_mz, d_gl], axis=0)
    small = {"b_gate": dbg, "lb": dlb, "hg_norm_g": dhgg, "q_a_g": dqg, "kv_a_g": dkvg, "final_norm_g": dfg}
    dh_args = (dhg, dms, dmz, dglog, w_int, x, norm_g, dx2)
    return loss, dh_args, d_w_int, d_wuq3, d_wukv3, (d_wpa, d_wpb, d_wout), small


def _pack_rest(w_uq_b, w_ukv_b, wpa_b, wpb_b, wout_b):
    return jnp.concatenate([w_uq_b.reshape(144, D_MODEL), w_ukv_b.reshape(128, D_MODEL), wpa_b, wpb_b, wout_b], axis=0)


def _unpack_rest(p):
    return (p[0:144].reshape(Q_LORA, 384), p[144:272].reshape(KV_LORA, 512), p[272:528], p[528:784], p[784:1040])


def _pack_rest_grads(d_wuq3, d_wukv3, d_proj):
    d_wuq = d_wuq3.transpose(1, 0, 2)[:, :, 0:QK_DIM].reshape(Q_LORA, HEADS * QK_DIM)
    d_wukv = d_wukv3.transpose(1, 0, 2).reshape(KV_LORA, HEADS * 256)
    blocks = []
    for b in range(N_CHIPS):
        rows = slice(b * 256, (b + 1) * 256)
        blocks.append(_pack_rest(d_wuq[:, b * 384:(b + 1) * 384], d_wukv[:, b * 512:(b + 1) * 512],
                                 d_proj[0][rows], d_proj[1][rows], d_proj[2][rows]))
    return jnp.stack(blocks, axis=0)


def _unpack_rest_weights(g):
    parts = [_unpack_rest(g[b]) for b in range(N_CHIPS)]
    w_uq, w_ukv = (jnp.concatenate([p[n] for p in parts], axis=1) for n in range(2))
    wproj = jnp.stack([jnp.concatenate([p[n] for p in parts], axis=0) for n in range(2, 5)], axis=0)
    return w_uq, w_ukv, wproj


MESH_ID = pl.DeviceIdType.MESH
ANY = pl.BlockSpec(memory_space=pl.ANY)
HALF_COLS = D_MODEL // 2


def _me():
    return lax.axis_index("x"), lax.axis_index("y"), lax.axis_index("c")


def _other_chips(x, y):
    return [(1 - x, y), (x, 1 - y), (1 - x, 1 - y)]


def _cols(c):
    return pl.ds(c * HALF_COLS, HALF_COLS)


RELAY_TOP = 992


def _gather_weights(w_blk):
    bot = W_IN_BLK - RELAY_TOP

    def body(w_ref, ow_ref, send_sems, recv_sems):
        x, y, c = _me()
        me, xn, yn, dg = 2 * x + y, 2 * (1 - x) + y, 2 * x + (1 - y), 2 * (1 - x) + (1 - y)
        to_x, to_y, to_sib = (1 - x, y, c), (x, 1 - y, c), (x, y, 1 - c)
        mine, theirs = _cols(c), _cols(1 - c)
        top, low = pl.ds(0, RELAY_TOP), pl.ds(RELAY_TOP, bot)

        def copy(k, src, dst, to):
            return pltpu.make_async_remote_copy(src_ref=src, dst_ref=dst, send_sem=send_sems.at[k],
                                                recv_sem=recv_sems.at[k], device_id=to, device_id_type=MESH_ID)

        def same(k, ref, to):
            return copy(k, ref, ref, to)

        own = [copy(0, w_ref.at[:, mine], ow_ref.at[me, :, mine], to_x),
               copy(1, w_ref.at[:, mine], ow_ref.at[me, :, mine], to_y)]
        for cp in own:
            cp.start()
        from_x, from_y = ow_ref.at[xn, :, mine], ow_ref.at[yn, :, mine]
        same(0, from_x, to_x).wait_recv()
        relay_y = same(2, ow_ref.at[xn, top, mine], to_y)
        pass_x = same(4, from_x, to_sib)
        relay_y.start()
        pass_x.start()
        same(1, from_y, to_y).wait_recv()
        relay_x = same(3, ow_ref.at[yn, low, mine], to_x)
        pass_y = same(5, from_y, to_sib)
        relay_x.start()
        pass_y.start()
        same(2, ow_ref.at[dg, top, mine], to_y).wait_recv()
        same(3, ow_ref.at[dg, low, mine], to_x).wait_recv()
        pass_d = same(6, ow_ref.at[dg, :, mine], to_sib)
        pass_d.start()
        for k, blk in ((4, xn), (5, yn), (6, dg)):
            same(k, ow_ref.at[blk, :, theirs], to_sib).wait_recv()
        for cp in own + [relay_y, relay_x, pass_x, pass_y, pass_d]:
            cp.wait_send()

    gw = pl.pallas_call(
        body, name="gather_weights", in_specs=[ANY], out_specs=ANY,
        out_shape=jax.ShapeDtypeStruct((N_CHIPS,) + w_blk.shape, w_blk.dtype),
        scratch_shapes=[pltpu.SemaphoreType.DMA((7,)), pltpu.SemaphoreType.DMA((7,))],
    )(w_blk)
    return lax.dynamic_update_slice(gw, w_blk[None], (2 * lax.axis_index("x") + lax.axis_index("y"), 0, 0))


def _swap_halves(gw, gr):
    def body(gw_ref, gr_ref, lw_ref, lr_ref, send_sems, recv_sems):
        x, y, c = _me()
        cps = [pltpu.make_async_remote_copy(
            src_ref=src, dst_ref=dst, send_sem=send_sems.at[a], recv_sem=recv_sems.at[a],
            device_id=(x, y, 1 - c), device_id_type=MESH_ID)
            for a, (src, dst) in enumerate([(gw_ref.at[:, _cols(1 - c)], lw_ref),
                                            (gr_ref.at[:, :, _cols(1 - c)], lr_ref)])]
        for cp in cps:
            cp.start()
        for cp in cps:
            cp.wait()

    return pl.pallas_call(
        body, name="grad_swap_halves", in_specs=[ANY, ANY], out_specs=[ANY, ANY],
        out_shape=[jax.ShapeDtypeStruct((gw.shape[0], HALF_COLS), gw.dtype),
                   jax.ShapeDtypeStruct(gr.shape[:2] + (HALF_COLS,), gr.dtype)],
        scratch_shapes=[pltpu.SemaphoreType.DMA((2,)), pltpu.SemaphoreType.DMA((2,))],
    )(gw, gr)


def _swap_reduced(rw, rr):
    def body(rw_ref, rr_ref, ow_ref, or_ref, send_sems, recv_sems):
        x, y, c = _me()
        cps = [pltpu.make_async_remote_copy(
            src_ref=src, dst_ref=dst, send_sem=send_sems.at[a], recv_sem=recv_sems.at[a],
            device_id=(x, y, 1 - c), device_id_type=MESH_ID)
            for a, (src, dst) in enumerate([(rw_ref, ow_ref), (rr_ref, or_ref)])]
        for cp in cps:
            cp.start()
        for cp in cps:
            cp.wait()

    return pl.pallas_call(
        body, name="grad_swap_reduced", in_specs=[ANY, ANY], out_specs=[ANY, ANY],
        out_shape=[jax.ShapeDtypeStruct(rw.shape, rw.dtype), jax.ShapeDtypeStruct(rr.shape, rr.dtype)],
        scratch_shapes=[pltpu.SemaphoreType.DMA((2,)), pltpu.SemaphoreType.DMA((2,))],
    )(rw, rr)


def _join_cols(mine, theirs):
    first = lax.axis_index("c") == 0
    return jnp.concatenate([jnp.where(first, mine, theirs), jnp.where(first, theirs, mine)], axis=1)


def _gather_small(vec):
    def body(v_ref, out_ref, send_sems, recv_sems, local_sem):
        x, y, c = _me()
        my_id = 4 * x + 2 * y + c
        mine = pltpu.make_async_copy(v_ref, out_ref.at[my_id], local_sem)
        mine.start()
        cps = []
        for r in range(1, N_DEV):
            peer = (x ^ (r >> 2), y ^ ((r >> 1) & 1), c ^ (r & 1))
            cps.append(pltpu.make_async_remote_copy(
                src_ref=v_ref, dst_ref=out_ref.at[my_id], send_sem=send_sems.at[r - 1],
                recv_sem=recv_sems.at[r - 1], device_id=peer, device_id_type=MESH_ID))
        for cp in cps:
            cp.start()
        for cp in cps:
            cp.wait()
        mine.wait()

    return pl.pallas_call(
        body, name="gather_small", in_specs=[ANY], out_specs=ANY,
        out_shape=jax.ShapeDtypeStruct((N_DEV, 1, SMALL_COLS), vec.dtype),
        scratch_shapes=[pltpu.SemaphoreType.DMA((N_DEV - 1,)), pltpu.SemaphoreType.DMA((N_DEV - 1,)),
                        pltpu.SemaphoreType.DMA],
    )(vec)


def _add_cores(c_idx, g, landed, *, tm, name):
    r = g.shape[0]

    def body(c_ref, g_ref, l_ref, o32_ref, o16_ref):
        acc = g_ref[...] + l_ref[...]
        o32_ref[...] = acc
        o16_ref[...] = acc.astype(BF16)

    half = pl.BlockSpec((tm, HALF_COLS), lambda i, c_ref: (i, 0))
    grid_spec = pltpu.PrefetchScalarGridSpec(
        num_scalar_prefetch=1, grid=(r // tm,),
        in_specs=[pl.BlockSpec((tm, HALF_COLS), lambda i, c_ref: (i, c_ref[0])), half], out_specs=[half, half])
    return pl.pallas_call(
        body, name=name, grid_spec=grid_spec,
        out_shape=[jax.ShapeDtypeStruct((r, HALF_COLS), F32), jax.ShapeDtypeStruct((r, HALF_COLS), BF16)],
        compiler_params=_params(1),
    )(c_idx, g, landed)


def _add_chips(chip_idx, h32, landed, *, tm, name):
    n = landed.shape[1]
    per = n // tm

    def body(chip_ref, h_ref, l_ref, o_ref):
        acc = h_ref[...]
        for j in range(3):
            acc = acc + l_ref[j].astype(F32)
        o_ref[...] = acc

    grid_spec = pltpu.PrefetchScalarGridSpec(
        num_scalar_prefetch=1, grid=(per,),
        in_specs=[pl.BlockSpec((tm, HALF_COLS), lambda i, chip_ref: (chip_ref[0] * per + i, 0)),
                  pl.BlockSpec((3, tm, HALF_COLS), lambda i, chip_ref: (0, i, 0))],
        out_specs=pl.BlockSpec((tm, HALF_COLS), lambda i, chip_ref: (i, 0)))
    return pl.pallas_call(
        body, name=name, grid_spec=grid_spec, out_shape=jax.ShapeDtypeStruct((n, HALF_COLS), F32),
        compiler_params=_params(1),
    )(chip_idx, h32, landed)


def _pack_small(small, lb_logits, loss):
    def body(ng_ref, bg_ref, dlb_ref, lbl_ref, hgg_ref, qg_ref, kvg_ref, fg_ref, loss_ref, out_ref):
        out_ref[...] = jnp.zeros_like(out_ref)
        out_ref[:, 0:1024] = ng_ref[...]
        out_ref[:, 1024:3072] = bg_ref[...]
        _, p0p1 = _lower_bound(lbl_ref[...])
        dl0 = dlb_ref[...] * p0p1
        out_ref[:, 3072:4096] = dl0
        out_ref[:, 4096:5120] = -dl0
        hgg = hgg_ref[0]
        for h in range(1, HEADS):
            hgg = hgg + hgg_ref[h]
        out_ref[:, 5120:5248] = hgg
        out_ref[:, 5248:5632] = qg_ref[...]
        out_ref[:, 5632:5888] = kvg_ref[...]
        out_ref[:, 5888:6912] = fg_ref[...]
        out_ref[:, 6912:7040] = loss_ref[...]

    return pl.pallas_call(
        body, name="pack_small", out_shape=jax.ShapeDtypeStruct((1, SMALL_COLS), F32),
    )(small["norm_g"], small["b_gate"], small["lb"], lb_logits, small["hg_norm_g"], small["q_a_g"],
      small["kv_a_g"], small["final_norm_g"], loss)


def _adamw_math(w, g, m, v):
    nm = ADAM_B1 * m + (1.0 - ADAM_B1) * g
    nv = ADAM_B2 * v + (1.0 - ADAM_B2) * (g * g)
    m_hat = nm / (1.0 - ADAM_B1 ** ADAM_STEP)
    v_hat = nv / (1.0 - ADAM_B2 ** ADAM_STEP)
    return -ADAM_LR * (m_hat / (jnp.sqrt(v_hat) + ADAM_EPS) + ADAM_WD * w), nm, nv


def _adamw(w, g, m, v, *, name, tm):
    r, cols = w.shape

    def body(w_ref, g_ref, m_ref, v_ref, d_ref, nm_ref, nv_ref):
        d_ref[...], nm_ref[...], nv_ref[...] = _adamw_math(w_ref[...], g_ref[...], m_ref[...], v_ref[...])

    row = pl.BlockSpec((tm, cols), lambda i: (i, 0))
    shp = jax.ShapeDtypeStruct((r, cols), F32)
    return pl.pallas_call(
        body, name=name, grid=(r // tm,), in_specs=[row] * 4, out_specs=[row] * 3, out_shape=[shp] * 3,
        compiler_params=_params(1),
    )(w, g, m, v)


SMALL_SLOTS = (("norm_g", (0,)), ("b_gate", (1024,)), ("lb_logits", (3072, 4096)), ("hg_norm_g", (5120,)),
               ("q_a_g", (5248,)), ("kv_a_g", (5632,)), ("final_norm_g", (5888,)))
LOSS_SLOT = 6912


def _small_update(gathered, ws, ms, vs):
    n = len(SMALL_SLOTS)

    def body(*refs):
        g_ref = refs[0]
        w_refs, m_refs, v_refs = refs[1:1 + n], refs[1 + n:1 + 2 * n], refs[1 + 2 * n:1 + 3 * n]
        outs = refs[1 + 3 * n:]
        loss_ref = outs[0]
        g_out, d_out, nm_out, nv_out = (outs[1 + k * n:1 + (k + 1) * n] for k in range(4))
        total = g_ref[0]
        for dev in range(1, N_DEV):
            total = total + g_ref[dev]
        loss_ref[...] = total[:, LOSS_SLOT:LOSS_SLOT + 128]
        for p, (_, offsets) in enumerate(SMALL_SLOTS):
            cols = w_refs[p].shape[1]
            for r, off in enumerate(offsets):
                rows = slice(r, r + 1)
                g = total[:, off:off + cols]
                g_out[p][rows, :] = g
                d_out[p][rows, :], nm_out[p][rows, :], nv_out[p][rows, :] = _adamw_math(
                    w_refs[p][rows, :], g, m_refs[p][rows, :], v_refs[p][rows, :])

    shapes = [jax.ShapeDtypeStruct(w.shape, F32) for w in ws]
    res = pl.pallas_call(
        body, name="small_update", out_shape=[jax.ShapeDtypeStruct((1, 128), F32)] + shapes * 4,
    )(gathered, *ws, *ms, *vs)
    return res[0], res[1:1 + n], res[1 + n:1 + 2 * n], res[1 + 2 * n:1 + 3 * n], res[1 + 3 * n:1 + 4 * n]


def kernel(x, norm_g, w_in, b_gate, lb_logits, hg_norm_g, q_a_g, w_uq, kv_a_g, w_ukv, w_proj_a, w_proj_b, w_out, final_norm_g, loss_target, m_norm_g, m_w_in, m_b_gate, m_lb_logits, m_hg_norm_g, m_q_a_g, m_w_uq, m_kv_a_g, m_w_ukv, m_w_proj_a, m_w_proj_b, m_w_out, m_final_norm_g, v_norm_g, v_w_in, v_b_gate, v_lb_logits, v_hg_norm_g, v_q_a_g, v_w_uq, v_kv_a_g, v_w_ukv, v_w_proj_a, v_w_proj_b, v_w_out, v_final_norm_g):
    c_idx = lax.axis_index("c").astype(jnp.int32).reshape(1)
    chip_idx = (2 * lax.axis_index("x") + lax.axis_index("y")).astype(jnp.int32).reshape(1)

    w_blk = w_in[0].T.astype(BF16)
    r_blk = _pack_rest(w_uq[0], w_ukv[0], w_proj_a[0], w_proj_b[0], w_out[0]).astype(BF16)
    gw = _gather_weights(w_blk)

    loss, dh_args, d_w_int, d_wuq3, d_wukv3, d_proj, small = _local_step(
        x[0], loss_target[0], gw.reshape(W_IN_COLS, D_MODEL), r_blk,
        norm_g, b_gate, lb_logits, hg_norm_g, q_a_g, kv_a_g, final_norm_g.reshape(1, D_MODEL))

    d_rest = _pack_rest_grads(d_wuq3, d_wukv3, d_proj)
    lw, lr = _swap_halves(d_w_int, d_rest)
    hw32, hw16 = _add_cores(c_idx, d_w_int, lw, tm=656, name="grad_add_cores_w")
    hr32, hr16 = _add_cores(c_idx, d_rest.reshape(N_CHIPS * REST_ROWS, D_MODEL), lr.reshape(N_CHIPS * REST_ROWS, HALF_COLS),
                            tm=REST_ROWS, name="grad_add_cores_r")
    grad_x, small["norm_g"], landed_w, landed_r = _dh_fused(*dh_args, hw16, hr16)
    rw = _add_chips(chip_idx, hw32, landed_w, tm=656, name="grad_add_chips_w")
    rr = _add_chips(chip_idx, hr32, landed_r, tm=208, name="grad_add_chips_r")
    tw, tr = _swap_reduced(rw, rr)
    g_w_in = _join_cols(rw, tw).T
    g_rest = _join_cols(rr, tr)
    g_uq, g_ukv, g_pa, g_pb, g_out = _unpack_rest(g_rest)

    small_all = _gather_small(_pack_small(small, lb_logits, loss))

    upd = {
        "w_in": _adamw(w_in[0], g_w_in, m_w_in[0], v_w_in[0], name="adamw_w_in", tm=128),
        "w_uq": _adamw(w_uq[0], g_uq, m_w_uq[0], v_w_uq[0], name="adamw_w_uq", tm=Q_LORA),
        "w_ukv": _adamw(w_ukv[0], g_ukv, m_w_ukv[0], v_w_ukv[0], name="adamw_w_ukv", tm=KV_LORA),
        "w_proj_a": _adamw(w_proj_a[0], g_pa, m_w_proj_a[0], v_w_proj_a[0], name="adamw_w_proj_a", tm=256),
        "w_proj_b": _adamw(w_proj_b[0], g_pb, m_w_proj_b[0], v_w_proj_b[0], name="adamw_w_proj_b", tm=256),
        "w_out": _adamw(w_out[0], g_out, m_w_out[0], v_w_out[0], name="adamw_w_out", tm=256),
    }
    loss_vec, *small_sets = _small_update(
        small_all,
        [norm_g, b_gate, lb_logits, hg_norm_g, q_a_g, kv_a_g, final_norm_g.reshape(1, D_MODEL)],
        [m_norm_g, m_b_gate, m_lb_logits, m_hg_norm_g, m_q_a_g, m_kv_a_g, m_final_norm_g.reshape(1, D_MODEL)],
        [v_norm_g, v_b_gate, v_lb_logits, v_hg_norm_g, v_q_a_g, v_kv_a_g, v_final_norm_g.reshape(1, D_MODEL)])

    def outputs(big, small_set):
        s_ng, s_bg, s_lb, s_hg, s_qg, s_kvg, s_fg = small_set
        return (s_ng, big["w_in"][None], s_bg, s_lb, s_hg, s_qg, big["w_uq"][None], s_kvg, big["w_ukv"][None],
                big["w_proj_a"][None], big["w_proj_b"][None], big["w_out"][None], s_fg.reshape(D_MODEL))

    grads = {"w_in": g_w_in, "w_uq": g_uq, "w_ukv": g_ukv, "w_proj_a": g_pa, "w_proj_b": g_pb, "w_out": g_out}
    return (loss_vec[0, 0], grad_x[None], *outputs(grads, small_sets[0]),
            *(o for k in range(3) for o in outputs({n: u[k] for n, u in upd.items()}, small_sets[1 + k])))
```

```python
import functools

import jax
import jax.numpy as jnp
from jax import lax
from jax.experimental import pallas as pl
from jax.experimental.pallas import tpu as pltpu

F32 = jnp.float32
BF16 = jnp.bfloat16

D_MODEL = 1024
HEADS = 8
HEAD_DIM = 128
HG_CHUNK = 32
CHUNK_SHIFT = 5
HEAD_SHIFT = 7
QK_NOPE = 128
QK_ROPE = 64
QK_DIM = QK_NOPE + QK_ROPE
QK_PAD = 256
Q_LORA = 384
KV_LORA = 256
MS_COLS = 768
ROPE_THETA = 10000.0
EPS = 1e-6
ATT_SCALE = QK_DIM ** -0.5
LOG2E = 1.4426950408889634
LN2 = 0.6931471805599453
Q_PRESCALE = ATT_SCALE * LOG2E

ADAM_LR = 0.001
ADAM_B1 = 0.9
ADAM_B2 = 0.999
ADAM_EPS = 1e-08
ADAM_WD = 0.01
ADAM_STEP = 10

N_CHIPS = 4
N_DEV = 8
W_IN_COLS = 7872
W_IN_BLK = W_IN_COLS // N_CHIPS
REST_ROWS = 144 + 128 + 3 * 256
SMALL_COLS = 7168

TM_MM = 2048
TM_FUSED = 256
TM_MLA = 512
HG_ROWS = 128
TQ = 512
FLASH_HEADS = 2
HG_HEADS = 8
VMEM_LIMIT = 56 * 1024 * 1024


def _dot(a, b):
    return lax.dot_general(a, b, (((1,), (0,)), ((), ())), preferred_element_type=F32)


def _dot_nt(a, b):
    return lax.dot_general(a, b, (((1,), (1,)), ((), ())), preferred_element_type=F32)


def _dot_tn(a, b):
    return lax.dot_general(a, b, (((0,), (0,)), ((), ())), preferred_element_type=F32)


def _params(n_axes):
    return pltpu.CompilerParams(dimension_semantics=("arbitrary",) * n_axes, vmem_limit_bytes=VMEM_LIMIT)


def _rms(x, g):
    r = lax.rsqrt(jnp.mean(x * x, axis=-1, keepdims=True) + EPS)
    return x * r * g


def _rms_bwd(x, g, dy):
    r = lax.rsqrt(jnp.mean(x * x, axis=-1, keepdims=True) + EPS)
    xh = x * r
    dyg = dy * g
    dx = r * (dyg - xh * jnp.mean(dyg * xh, axis=-1, keepdims=True))
    return dx, dy * xh


def _silu_parts(z):
    s = jax.nn.sigmoid(z)
    return z * s, s * (1.0 + z * (1.0 - s))


def _rope(x, c, sa, sb):
    return x * c + pltpu.roll(x, 32, 1) * sa + pltpu.roll(x, 96, 1) * sb


def _rope_bwd(dy, c, sa, sb):
    return dy * c + pltpu.roll(dy * sa, 96, 1) + pltpu.roll(dy * sb, 32, 1)


def _rope_tables(seq):
    inv = ROPE_THETA ** (-jnp.arange(0, QK_ROPE, 2, dtype=F32) / QK_ROPE)
    ang = jnp.arange(seq, dtype=F32)[:, None] * inv[None, :]
    cos, sin = jnp.cos(ang), jnp.sin(ang)
    z32 = jnp.zeros_like(cos)
    z64 = jnp.zeros((seq, 64), F32)
    c = jnp.concatenate([cos, cos, z64], axis=1)
    sa = jnp.concatenate([z32, sin, z64], axis=1)
    sb = jnp.concatenate([-sin, z32, z64], axis=1)
    return c, sa, sb


def _mm_tn(a, b, *, name, tm=TM_MM, tn=1024):
    flat = a.ndim == 2
    if flat:
        a = a[None]
    g, m, k = a.shape
    n = b.shape[1]
    tm, tn = min(tm, m), min(tn, n)
    assert m % tm == 0 and n % tn == 0

    def body(a_ref, b_ref, o_ref):
        @pl.when(pl.program_id(2) == 0)
        def _():
            o_ref[...] = jnp.zeros_like(o_ref)

        o_ref[...] += _dot_tn(a_ref[...], b_ref[...])

    out = pl.pallas_call(
        body, name=name, grid=(g, n // tn, m // tm),
        in_specs=[pl.BlockSpec((None, tm, k), lambda s, j, i: (s, i, 0)),
                  pl.BlockSpec((tm, tn), lambda s, j, i: (i, j))],
        out_specs=pl.BlockSpec((None, k, tn), lambda s, j, i: (s, 0, j)),
        out_shape=jax.ShapeDtypeStruct((g, k, n), F32), compiler_params=_params(3),
    )(a, b)
    return out[0] if flat else out


def _chunk_rows(rows):
    return lax.broadcasted_iota(jnp.int32, (rows, HEAD_DIM), 0) & (HG_CHUNK - 1)


def _chunk_cumsum(x, rows):
    pos = _chunk_rows(rows)
    shift = 1
    while shift < HG_CHUNK:
        x = x + jnp.where(pos >= shift, pltpu.roll(x, shift, 0), 0.0)
        shift *= 2
    return x


def _chunk_revcumsum(x, rows):
    pos = _chunk_rows(rows)
    shift = 1
    while shift < HG_CHUNK:
        x = x + jnp.where(pos + shift < HG_CHUNK, pltpu.roll(x, rows - shift, 0), 0.0)
        shift *= 2
    return x


def _chunk_last(x, rows):
    x3 = x.reshape(rows // HG_CHUNK, HG_CHUNK, HEAD_DIM)
    return jnp.broadcast_to(x3[:, HG_CHUNK - 1:HG_CHUNK, :], x3.shape).reshape(rows, HEAD_DIM)


def _lower_bound(lbl):
    mx = jnp.maximum(lbl[0:1, :], lbl[1:2, :])
    e0 = jnp.exp(lbl[0:1, :] - mx)
    e1 = jnp.exp(lbl[1:2, :] - mx)
    p0 = e0 / (e0 + e1)
    return p0, p0 * (e1 / (e0 + e1))


def _hg_masks(rows, nch, tmask_s, bdmask_s):
    r = lax.broadcasted_iota(jnp.int32, (rows, rows), 0)
    c = lax.broadcasted_iota(jnp.int32, (rows, rows), 1)
    tmask_s[...] = jnp.where(((r >> CHUNK_SHIFT) == (c >> CHUNK_SHIFT)) & (r >= c), 1.0, 0.0)
    r = lax.broadcasted_iota(jnp.int32, (rows, nch * HEAD_DIM), 0)
    c = lax.broadcasted_iota(jnp.int32, (rows, nch * HEAD_DIM), 1)
    bdmask_s[...] = jnp.where((r >> CHUNK_SHIFT) == (c >> HEAD_SHIFT), 1.0, 0.0).astype(BF16)


def _block_diag(x, nch, bdmask):
    return jnp.tile(x, (1, nch)) * bdmask


def _hgrn_fwd(hg, lb_logits, norm_g):
    s = hg.shape[1]
    rows = min(HG_ROWS, s)
    nblk = s // rows
    nch = rows // HG_CHUNK

    def body(hg_ref, lbl_ref, g_ref, o_ref, ya_ref, st0_ref, st_s, stall_s, tmask_s, bdmask_s):
        @pl.when(pl.program_id(1) == 0)
        def _():
            st_s[...] = jnp.zeros_like(st_s)
            _hg_masks(rows, nch, tmask_s, bdmask_s)

        bdmask = bdmask_s[...]
        tmask = tmask_s[...] > 0.5
        for hh in range(HG_HEADS):
            hc = slice(hh * HEAD_DIM, (hh + 1) * HEAD_DIM)
            hq = hg_ref[0, :, hc]
            hf = hg_ref[1, :, hc]
            hi = hg_ref[2, :, hc]
            hz = hg_ref[3, :, hc]
            lb, _ = _lower_bound(lbl_ref[:, hc])
            f = lb + (1.0 - lb) * jax.nn.sigmoid(hf)
            q = hq * jax.nn.sigmoid(hq)
            k = 1.0 - f
            logf = jnp.log(f)
            b = _chunk_cumsum(logf, rows)
            q_in = (q * jnp.exp(b)).astype(BF16)
            k_in = (k * jnp.exp(-b)).astype(BF16)
            k_out = (k * jnp.exp(_chunk_last(b, rows) - b)).astype(BF16)
            vb = hi.astype(BF16)

            sc = jnp.where(tmask, _dot_nt(q_in, k_in), 0.0)
            o_intra = _dot(sc.astype(BF16), vb)
            kvt = _dot_tn(vb, _block_diag(k_out, nch, bdmask))
            st = st_s[hh]
            st0_ref[hh] = st
            for c in range(nch):
                cols = slice(c * HEAD_DIM, (c + 1) * HEAD_DIM)
                last = (c + 1) * HG_CHUNK - 1
                stall_s[hh, :, cols] = st.astype(BF16)
                st = st * jnp.exp(b[last:last + 1, :]) + kvt[:, cols]
            st_s[hh] = st
            o = o_intra + _dot_nt(_block_diag(q_in, nch, bdmask), stall_s[hh])
            o_ref[:, hc] = o
            silu_z, _ = _silu_parts(hz)
            ya_ref[:, hc] = (_rms(o, g_ref[...]) * silu_z).astype(BF16)

    nh = HG_HEADS
    return pl.pallas_call(
        body, name="hgrn_fwd", grid=(HEADS // nh, nblk),
        in_specs=[pl.BlockSpec((4, rows, nh * HEAD_DIM), lambda h, i: (0, i, h)),
                  pl.BlockSpec((2, nh * HEAD_DIM), lambda h, i: (0, h)),
                  pl.BlockSpec((1, HEAD_DIM), lambda h, i: (0, 0))],
        out_specs=[pl.BlockSpec((rows, nh * HEAD_DIM), lambda h, i: (i, h)),
                   pl.BlockSpec((rows, nh * HEAD_DIM), lambda h, i: (i, h)),
                   pl.BlockSpec((nh, None, HEAD_DIM, HEAD_DIM), lambda h, i: (h, i, 0, 0))],
        out_shape=[jax.ShapeDtypeStruct((s, D_MODEL), F32), jax.ShapeDtypeStruct((s, D_MODEL), BF16),
                   jax.ShapeDtypeStruct((HEADS, nblk, HEAD_DIM, HEAD_DIM), F32)],
        scratch_shapes=[pltpu.VMEM((nh, HEAD_DIM, HEAD_DIM), F32), pltpu.VMEM((nh, HEAD_DIM, nch * HEAD_DIM), BF16),
                        pltpu.VMEM((rows, rows), F32), pltpu.VMEM((rows, nch * HEAD_DIM), BF16)],
        compiler_params=_params(2),
    )(hg, lb_logits, norm_g)


def _hgrn_bwd(hg, o_pre, dya, st0, lb_logits, norm_g):
    s = hg.shape[1]
    rows = min(HG_ROWS, s)
    nblk = s // rows
    nch = rows // HG_CHUNK

    def body(hg_ref, o_ref, dya_ref, st0_ref, lbl_ref, g_ref, dhg_ref, dlb_ref, dg_ref,
             dst_s, stp_s, stp_rows_s, dst_rows_s, dst_lane_s, dbl_s, tmask_s, bdmask_s):
        @pl.when(pl.program_id(1) == 0)
        def _():
            dst_s[...] = jnp.zeros_like(dst_s)
            dlb_ref[...] = jnp.zeros_like(dlb_ref)
            dg_ref[...] = jnp.zeros_like(dg_ref)
            _hg_masks(rows, nch, tmask_s, bdmask_s)

        bdmask = bdmask_s[...]
        tmask = tmask_s[...] > 0.5
        g = g_ref[...]
        for hh in range(HG_HEADS):
            hc = slice(hh * HEAD_DIM, (hh + 1) * HEAD_DIM)
            hq = hg_ref[0, :, hc]
            hf = hg_ref[1, :, hc]
            hi = hg_ref[2, :, hc]
            hz = hg_ref[3, :, hc]
            lb, _ = _lower_bound(lbl_ref[:, hc])
            sg = jax.nn.sigmoid(hf)
            f = lb + (1.0 - lb) * sg
            q, dsilu_q = _silu_parts(hq)
            k = 1.0 - f
            logf = jnp.log(f)
            b = _chunk_cumsum(logf, rows)
            eb = jnp.exp(b)
            enb = jnp.exp(-b)
            ebl = jnp.exp(_chunk_last(b, rows) - b)
            q_in32 = q * eb
            k_in32 = k * enb
            k_out32 = k * ebl
            q_in = q_in32.astype(BF16)
            k_in = k_in32.astype(BF16)
            k_out = k_out32.astype(BF16)
            vb = hi.astype(BF16)
            kbd = _block_diag(k_out, nch, bdmask)
            qbd = _block_diag(q_in, nch, bdmask)
            decs = [jnp.exp(b[(c + 1) * HG_CHUNK - 1:(c + 1) * HG_CHUNK, :]) for c in range(nch)]

            kvt = _dot_tn(vb, kbd)
            st = st0_ref[hh]
            for c in range(nch):
                stp_s[hh, c] = st
                stp_rows_s[hh, c * HEAD_DIM:(c + 1) * HEAD_DIM, :] = st.astype(BF16)
                st = st * decs[c] + kvt[:, c * HEAD_DIM:(c + 1) * HEAD_DIM]

            o = o_ref[:, hc]
            rstd = lax.rsqrt(jnp.mean(o * o, axis=-1, keepdims=True) + EPS)
            oh = o * rstd
            silu_z, dsilu_z = _silu_parts(hz)
            dya_v = dya_ref[:, hc]
            dn = dya_v * silu_z
            dhz = dya_v * (oh * g) * dsilu_z
            dg_ref[hh] += jnp.sum(dn * oh, axis=0, keepdims=True)
            doh = dn * g
            do = (rstd * (doh - oh * jnp.mean(doh * oh, axis=-1, keepdims=True))).astype(BF16)

            dq_all = _dot_tn(do, qbd)
            dst = dst_s[hh]
            ddecs = [None] * nch
            for c in reversed(range(nch)):
                dstb = dst.astype(BF16)
                dst_lane_s[hh, :, c * HEAD_DIM:(c + 1) * HEAD_DIM] = dstb
                dst_rows_s[hh, c * HEAD_DIM:(c + 1) * HEAD_DIM, :] = dstb
                ddecs[c] = jnp.sum(dst * stp_s[hh, c], axis=0, keepdims=True) * decs[c]
                dst = dst * decs[c] + dq_all[:, c * HEAD_DIM:(c + 1) * HEAD_DIM]
            dst_s[hh] = dst

            sc = jnp.where(tmask, _dot_nt(q_in, k_in), 0.0).astype(BF16)
            dkout = _dot(_block_diag(vb, nch, bdmask), dst_rows_s[hh])
            dv = _dot_nt(kbd, dst_lane_s[hh]) + _dot_tn(sc, do)
            dsc = jnp.where(tmask, _dot_nt(do, vb), 0.0).astype(BF16)
            dqin = _dot(dsc, k_in) + _dot(_block_diag(do, nch, bdmask), stp_rows_s[hh])
            dkin = _dot_tn(dsc, q_in)

            dko = dkout * k_out32
            for c in range(nch):
                sl = slice(c * HG_CHUNK, (c + 1) * HG_CHUNK)
                dbl = jnp.sum(dko[sl], axis=0, keepdims=True) + ddecs[c]
                dbl_s[hh, sl, :] = jnp.broadcast_to(dbl, (HG_CHUNK, HEAD_DIM))
            dq = dqin * eb
            dk = dkin * enb + dkout * ebl
            db = dqin * q_in32 - dkin * k_in32 - dko
            dlogf = _chunk_revcumsum(db, rows) + dbl_s[hh]
            df = dlogf / f - dk
            dlb_ref[:, hc] += jnp.sum(df * (1.0 - sg), axis=0, keepdims=True)
            dhg_ref[0, :, hc] = (dq * dsilu_q).astype(BF16)
            dhg_ref[1, :, hc] = (df * (1.0 - lb) * sg * (1.0 - sg)).astype(BF16)
            dhg_ref[2, :, hc] = dv.astype(BF16)
            dhg_ref[3, :, hc] = dhz.astype(BF16)

    last = nblk - 1
    nh = HG_HEADS
    wide = nh * HEAD_DIM
    return pl.pallas_call(
        body, name="hgrn_bwd", grid=(HEADS // nh, nblk),
        in_specs=[pl.BlockSpec((4, rows, wide), lambda h, i: (0, last - i, h)),
                  pl.BlockSpec((rows, wide), lambda h, i: (last - i, h)),
                  pl.BlockSpec((rows, wide), lambda h, i: (last - i, h)),
                  pl.BlockSpec((nh, None, HEAD_DIM, HEAD_DIM), lambda h, i: (h, last - i, 0, 0)),
                  pl.BlockSpec((2, wide), lambda h, i: (0, h)),
                  pl.BlockSpec((1, HEAD_DIM), lambda h, i: (0, 0))],
        out_specs=[pl.BlockSpec((4, rows, wide), lambda h, i: (0, last - i, h)),
                   pl.BlockSpec((1, wide), lambda h, i: (0, h)),
                   pl.BlockSpec((nh, 1, HEAD_DIM), lambda h, i: (h, 0, 0))],
        out_shape=[jax.ShapeDtypeStruct((4, s, D_MODEL), BF16), jax.ShapeDtypeStruct((1, D_MODEL), F32),
                   jax.ShapeDtypeStruct((HEADS, 1, HEAD_DIM), F32)],
        scratch_shapes=[pltpu.VMEM((nh, HEAD_DIM, HEAD_DIM), F32), pltpu.VMEM((nh, nch, HEAD_DIM, HEAD_DIM), F32),
                        pltpu.VMEM((nh, nch * HEAD_DIM, HEAD_DIM), BF16), pltpu.VMEM((nh, nch * HEAD_DIM, HEAD_DIM), BF16),
                        pltpu.VMEM((nh, HEAD_DIM, nch * HEAD_DIM), BF16), pltpu.VMEM((nh, rows, HEAD_DIM), F32),
                        pltpu.VMEM((rows, rows), F32), pltpu.VMEM((rows, nch * HEAD_DIM), BF16)],
        compiler_params=_params(2),
    )(hg, o_pre, dya, st0, lb_logits, norm_g)


def _mla_pre(ms, q_a_g, kv_a_g, wuq3, wukv3, tabs):
    s = ms.shape[0]
    tm = min(TM_MLA, s)

    def body(ms_ref, qg_ref, kvg_ref, wuq_ref, wukv_ref, c_ref, sa_ref, sb_ref,
             q_ref, k_ref, v_ref, cqn_ref, ckvn_ref):
        c, sa, sb = c_ref[...], sa_ref[...], sb_ref[...]
        cqn = _rms(ms_ref[:, 0:Q_LORA], qg_ref[...]).astype(BF16)
        ckvn = _rms(ms_ref[:, Q_LORA:Q_LORA + KV_LORA], kvg_ref[...]).astype(BF16)
        cqn_ref[...] = cqn
        ckvn_ref[...] = ckvn
        k_pe = _rope(ms_ref[:, Q_LORA + KV_LORA:MS_COLS], c, sa, sb).astype(BF16)
        for h in range(HEADS):
            qh = _dot(cqn, wuq_ref[h])
            q_ref[h, :, 0:128] = (qh[:, 0:128] * Q_PRESCALE).astype(BF16)
            q_ref[h, :, 128:256] = (_rope(qh[:, 128:256], c, sa, sb) * Q_PRESCALE).astype(BF16)
            kvh = _dot(ckvn, wukv_ref[h])
            k_ref[h, :, 0:128] = kvh[:, 0:128].astype(BF16)
            k_ref[h, :, 128:256] = k_pe
            v_ref[h] = kvh[:, 128:256].astype(BF16)

    tab = pl.BlockSpec((tm, 128), lambda i: (i, 0))
    return pl.pallas_call(
        body, name="mla_pre", grid=(s // tm,),
        in_specs=[pl.BlockSpec((tm, MS_COLS), lambda i: (i, 0)),
                  pl.BlockSpec((1, Q_LORA), lambda i: (0, 0)), pl.BlockSpec((1, KV_LORA), lambda i: (0, 0)),
                  pl.BlockSpec((HEADS, Q_LORA, QK_PAD), lambda i: (0, 0, 0)),
                  pl.BlockSpec((HEADS, KV_LORA, 256), lambda i: (0, 0, 0)), tab, tab, tab],
        out_specs=[pl.BlockSpec((HEADS, tm, QK_PAD), lambda i: (0, i, 0)),
                   pl.BlockSpec((HEADS, tm, QK_PAD), lambda i: (0, i, 0)),
                   pl.BlockSpec((HEADS, tm, HEAD_DIM), lambda i: (0, i, 0)),
                   pl.BlockSpec((tm, Q_LORA), lambda i: (i, 0)), pl.BlockSpec((tm, KV_LORA), lambda i: (i, 0))],
        out_shape=[jax.ShapeDtypeStruct((HEADS, s, QK_PAD), BF16), jax.ShapeDtypeStruct((HEADS, s, QK_PAD), BF16),
                   jax.ShapeDtypeStruct((HEADS, s, HEAD_DIM), BF16),
                   jax.ShapeDtypeStruct((s, Q_LORA), BF16), jax.ShapeDtypeStruct((s, KV_LORA), BF16)],
        compiler_params=_params(1),
    )(ms, q_a_g, kv_a_g, wuq3, wukv3, *tabs)


def _causal_mask(t):
    r = lax.broadcasted_iota(jnp.int32, (t, t), 0)
    c = lax.broadcasted_iota(jnp.int32, (t, t), 1)
    return r >= c


def _flash_fwd(q, k, v, mz):
    s = q.shape[1]
    t = min(TQ, s)

    def body(q_ref, k_ref, v_ref, mz_ref, o_ref, yb_ref, lse_ref, m_s, l_s, acc_s):
        i = pl.program_id(1)
        m_s[...] = jnp.full_like(m_s, -jnp.inf)
        l_s[...] = jnp.zeros_like(l_s)
        acc_s[...] = jnp.zeros_like(acc_s)

        def step(j, groups):
            rows = pl.ds(pl.multiple_of(j * t, t), t)
            for hh in range(FLASH_HEADS):
                for r0, nr, masked in groups:
                    r = slice(r0, r0 + nr)
                    sc = _dot_nt(q_ref[hh, r, :], k_ref[hh, rows, :])
                    if masked:
                        sc = jnp.where(_causal_mask(t), sc, -jnp.inf)
                    m_prev = m_s[hh, r, :]
                    m_new = jnp.maximum(m_prev, jnp.max(sc, axis=-1, keepdims=True))
                    p = jnp.exp2(sc - jnp.tile(m_new, (1, t // 128))).astype(BF16)
                    alpha = jnp.exp2(m_prev - m_new)
                    l_s[hh, r, :] = alpha * l_s[hh, r, :] + jnp.sum(p.astype(F32), axis=-1, keepdims=True)
                    acc_s[hh, r, :] = alpha * acc_s[hh, r, :] + _dot(p, v_ref[hh, rows, :])
                    m_s[hh, r, :] = m_new

        def loop_body(jj, carry):
            for u in range(4):
                step(4 * jj + u, ((0, 2 * t, False),))
            return carry

        lax.fori_loop(0, i // 2, loop_body, 0)

        @pl.when(i % 2 == 1)
        def _():
            step(2 * i - 2, ((0, 2 * t, False),))
            step(2 * i - 1, ((0, 2 * t, False),))

        step(2 * i, ((0, t, True), (t, t, False)))
        step(2 * i + 1, ((t, t, True),))
        for hh in range(FLASH_HEADS):
            cols = slice(hh * HEAD_DIM, (hh + 1) * HEAD_DIM)
            out = acc_s[hh] / l_s[hh]
            o_ref[:, cols] = out
            silu_z, _ = _silu_parts(mz_ref[:, cols])
            yb_ref[:, cols] = (out * silu_z).astype(BF16)
            lse_ref[hh] = m_s[hh] + jnp.log2(l_s[hh])

    nh = FLASH_HEADS
    t2 = 2 * t
    col = pl.BlockSpec((t2, nh * HEAD_DIM), lambda h, i: (i, h))
    return pl.pallas_call(
        body, name="flash_fwd", grid=(HEADS // nh, s // t2),
        in_specs=[pl.BlockSpec((nh, t2, QK_PAD), lambda h, i: (h, i, 0)),
                  pl.BlockSpec((nh, s, QK_PAD), lambda h, i: (h, 0, 0)),
                  pl.BlockSpec((nh, s, HEAD_DIM), lambda h, i: (h, 0, 0)), col],
        out_specs=[col, col, pl.BlockSpec((nh, t2, 128), lambda h, i: (h, i, 0))],
        out_shape=[jax.ShapeDtypeStruct((s, D_MODEL), F32), jax.ShapeDtypeStruct((s, D_MODEL), BF16),
                   jax.ShapeDtypeStruct((HEADS, s, 128), F32)],
        scratch_shapes=[pltpu.VMEM((nh, t2, 128), F32), pltpu.VMEM((nh, t2, 128), F32),
                        pltpu.VMEM((nh, t2, HEAD_DIM), F32)],
        compiler_params=_params(2),
    )(q, k, v, mz)


def _flash_bwd(q, k, v, dyb, mz, o_att, lse, tabs):
    s = q.shape[1]
    t = min(TQ, s)

    def body(q_ref, k_ref, v_ref, dyb_ref, mz_ref, o_ref, lse_ref, c_ref, sa_ref, sb_ref,
             dq_ref, dk_ref, dv_ref, dmz_ref, dq_s, delta_s, do_s):
        i = pl.program_id(1)

        @pl.when(i == 0)
        def _():
            dk_ref[...] = jnp.zeros_like(dk_ref)
            dv_ref[...] = jnp.zeros_like(dv_ref)

        silu_z, dsilu_z = _silu_parts(mz_ref[...])
        dyb_v = dyb_ref[...]
        out = o_ref[...]
        do32 = dyb_v * silu_z
        dmz_ref[...] = (dyb_v * out * dsilu_z).astype(BF16)
        delta_s[...] = jnp.broadcast_to(jnp.sum(do32 * out, axis=-1, keepdims=True), (2 * t, 128))
        do_s[...] = do32.astype(BF16)
        dq_s[...] = jnp.zeros_like(dq_s)

        def step(j, modes):
            rows = pl.ds(pl.multiple_of(j * t, t), t)
            kj = k_ref[rows, :]
            vj = v_ref[rows, :]
            dv_acc = None
            dk_acc = None
            for ch, masked in enumerate(modes):
                if masked is None:
                    continue
                r = slice(ch * t, (ch + 1) * t)
                qv = q_ref[r, :]
                do = do_s[r, :]
                sc = _dot_nt(qv, kj)
                if masked:
                    sc = jnp.where(_causal_mask(t), sc, -jnp.inf)
                p = jnp.exp2(sc - jnp.tile(lse_ref[r, :], (1, t // 128)))
                dp = _dot_nt(do, vj)
                ds = (p * (dp - jnp.tile(delta_s[r, :], (1, t // 128)))).astype(BF16)
                dv_c = _dot_tn(p.astype(BF16), do)
                dk_c = _dot_tn(ds, qv)
                dv_acc = dv_c if dv_acc is None else dv_acc + dv_c
                dk_acc = dk_c if dk_acc is None else dk_acc + dk_c
                dq_s[r, :] += _dot(ds, kj)
            dv_ref[rows, :] += dv_acc
            dk_ref[rows, :] += dk_acc

        def loop_body(jj, carry):
            for u in range(4):
                step(4 * jj + u, (False, False))
            return carry

        lax.fori_loop(0, i // 2, loop_body, 0)

        @pl.when(i % 2 == 1)
        def _():
            step(2 * i - 2, (False, False))
            step(2 * i - 1, (False, False))

        step(2 * i, (True, False))
        step(2 * i + 1, (None, True))
        dq = dq_s[...] * ATT_SCALE
        dq_ref[:, 0:128] = dq[:, 0:128].astype(BF16)
        dq_ref[:, 128:256] = _rope_bwd(dq[:, 128:256], c_ref[...], sa_ref[...], sb_ref[...]).astype(BF16)

    t2 = 2 * t
    col = pl.BlockSpec((t2, HEAD_DIM), lambda h, i: (i, h))
    tab = pl.BlockSpec((t2, 128), lambda h, i: (i, 0))
    return pl.pallas_call(
        body, name="flash_bwd", grid=(HEADS, s // t2),
        in_specs=[pl.BlockSpec((None, t2, QK_PAD), lambda h, i: (h, i, 0)),
                  pl.BlockSpec((None, s, QK_PAD), lambda h, i: (h, 0, 0)),
                  pl.BlockSpec((None, s, HEAD_DIM), lambda h, i: (h, 0, 0)),
                  col, col, col, pl.BlockSpec((None, t2, 128), lambda h, i: (h, i, 0)), tab, tab, tab],
        out_specs=[pl.BlockSpec((None, t2, QK_PAD), lambda h, i: (h, i, 0)),
                   pl.BlockSpec((None, s, QK_PAD), lambda h, i: (h, 0, 0)),
                   pl.BlockSpec((None, s, HEAD_DIM), lambda h, i: (h, 0, 0)), col],
        out_shape=[jax.ShapeDtypeStruct((HEADS, s, QK_PAD), BF16), jax.ShapeDtypeStruct((HEADS, s, QK_PAD), F32),
                   jax.ShapeDtypeStruct((HEADS, s, HEAD_DIM), F32), jax.ShapeDtypeStruct((s, D_MODEL), BF16)],
        scratch_shapes=[pltpu.VMEM((t2, QK_PAD), F32), pltpu.VMEM((t2, 128), F32), pltpu.VMEM((t2, HEAD_DIM), BF16)],
        compiler_params=_params(2),
    )(q, k, v, dyb, mz, o_att, lse, *tabs)


def _mla_bwd_proj(dq, dk, dv, cqn, ckvn, ms, q_a_g, kv_a_g, wuq3, wukv3, tabs):
    s = ms.shape[0]
    tm = min(TM_MLA, s)

    def body(dq_ref, dk_ref, dv_ref, cqn_ref, ckvn_ref, ms_ref, qg_ref, kvg_ref, wuq_ref, wukv_ref,
             c_ref, sa_ref, sb_ref, dms_ref, dwuq_ref, dwukv_ref, dqg_ref, dkvg_ref):
        @pl.when(pl.program_id(0) == 0)
        def _():
            dwuq_ref[...] = jnp.zeros_like(dwuq_ref)
            dwukv_ref[...] = jnp.zeros_like(dwukv_ref)
            dqg_ref[...] = jnp.zeros_like(dqg_ref)
            dkvg_ref[...] = jnp.zeros_like(dkvg_ref)

        cqn = cqn_ref[...]
        ckvn = ckvn_ref[...]
        dcqn = jnp.zeros((tm, Q_LORA), F32)
        dckvn = jnp.zeros((tm, KV_LORA), F32)
        dkpe = jnp.zeros((tm, 128), F32)
        for h in range(HEADS):
            dqh = dq_ref[h]
            dcqn += _dot_nt(dqh, wuq_ref[h])
            dwuq_ref[h] += _dot_tn(cqn, dqh)
            dkh = dk_ref[h] * LN2
            dkvh = jnp.concatenate([dkh[:, 0:128], dv_ref[h]], axis=1).astype(BF16)
            dckvn += _dot_nt(dkvh, wukv_ref[h])
            dwukv_ref[h] += _dot_tn(ckvn, dkvh)
            dkpe += dkh[:, 128:256]
        dcq, dqg_rows = _rms_bwd(ms_ref[:, 0:Q_LORA], qg_ref[...], dcqn)
        dckv, dkvg_rows = _rms_bwd(ms_ref[:, Q_LORA:Q_LORA + KV_LORA], kvg_ref[...], dckvn)
        dqg_ref[...] += jnp.sum(dqg_rows, axis=0, keepdims=True)
        dkvg_ref[...] += jnp.sum(dkvg_rows, axis=0, keepdims=True)
        dms_ref[:, 0:Q_LORA] = dcq.astype(BF16)
        dms_ref[:, Q_LORA:Q_LORA + KV_LORA] = dckv.astype(BF16)
        dms_ref[:, Q_LORA + KV_LORA:MS_COLS] = _rope_bwd(dkpe, c_ref[...], sa_ref[...], sb_ref[...]).astype(BF16)

    tab = pl.BlockSpec((tm, 128), lambda i: (i, 0))
    wq = pl.BlockSpec((HEADS, Q_LORA, QK_PAD), lambda i: (0, 0, 0))
    wkv = pl.BlockSpec((HEADS, KV_LORA, 256), lambda i: (0, 0, 0))
    qg = pl.BlockSpec((1, Q_LORA), lambda i: (0, 0))
    kvg = pl.BlockSpec((1, KV_LORA), lambda i: (0, 0))
    return pl.pallas_call(
        body, name="mla_bwd_proj", grid=(s // tm,),
        in_specs=[pl.BlockSpec((HEADS, tm, QK_PAD), lambda i: (0, i, 0)),
                  pl.BlockSpec((HEADS, tm, QK_PAD), lambda i: (0, i, 0)),
                  pl.BlockSpec((HEADS, tm, HEAD_DIM), lambda i: (0, i, 0)),
                  pl.BlockSpec((tm, Q_LORA), lambda i: (i, 0)), pl.BlockSpec((tm, KV_LORA), lambda i: (i, 0)),
                  pl.BlockSpec((tm, MS_COLS), lambda i: (i, 0)), qg, kvg, wq, wkv, tab, tab, tab],
        out_specs=[pl.BlockSpec((tm, MS_COLS), lambda i: (i, 0)), wq, wkv, qg, kvg],
        out_shape=[jax.ShapeDtypeStruct((s, MS_COLS), BF16), jax.ShapeDtypeStruct((HEADS, Q_LORA, QK_PAD), F32),
                   jax.ShapeDtypeStruct((HEADS, KV_LORA, 256), F32),
                   jax.ShapeDtypeStruct((1, Q_LORA), F32), jax.ShapeDtypeStruct((1, KV_LORA), F32)],
        compiler_params=_params(1),
    )(dq, dk, dv, cqn, ckvn, ms, q_a_g, kv_a_g, wuq3, wukv3, *tabs)


def _merge_fused(ya, yb, glog, b_gate, x, tgt, fg, wproj):
    s = x.shape[0]
    tm = min(TM_FUSED, s)

    def body(ya_ref, yb_ref, g0_ref, g1_ref, b0_ref, b1_ref, x_ref, t_ref, fg_ref, w_ref,
             mg_ref, dx2_ref, dx2b_ref, dya_ref, dyb_ref, dgl_ref, dpa_ref, dpb_ref, loss_ref, dfg_ref, dbg_ref):
        @pl.when(pl.program_id(0) == 0)
        def _():
            loss_ref[...] = jnp.zeros_like(loss_ref)
            dfg_ref[...] = jnp.zeros_like(dfg_ref)
            dbg_ref[...] = jnp.zeros_like(dbg_ref)

        pa = _dot(ya_ref[...], w_ref[0])
        pb = _dot(yb_ref[...], w_ref[1])
        g0 = jax.nn.sigmoid(g0_ref[...] + b0_ref[...])
        g1 = jax.nn.sigmoid(g1_ref[...] + b1_ref[...])
        merged = (g0 * pa + g1 * pb).astype(BF16)
        mg_ref[...] = merged
        x2 = x_ref[...] + _dot(merged, w_ref[2])
        fg_v = fg_ref[...]
        err = _rms(x2, fg_v) - t_ref[...]
        loss_ref[...] += 0.5 * jnp.sum(jnp.mean(err * err, axis=-1, keepdims=True), axis=0, keepdims=True)
        dx2, dfg_rows = _rms_bwd(x2, fg_v, err * (1.0 / D_MODEL))
        dx2_ref[...] = dx2
        dfg_ref[...] += jnp.sum(dfg_rows, axis=0, keepdims=True)

        dx2b = dx2.astype(BF16)
        dx2b_ref[...] = dx2b
        dmg = _dot_nt(dx2b, w_ref[2])
        dpa = (dmg * g0).astype(BF16)
        dpb = (dmg * g1).astype(BF16)
        dpa_ref[...] = dpa
        dpb_ref[...] = dpb
        dgl0 = dmg * pa * g0 * (1.0 - g0)
        dgl1 = dmg * pb * g1 * (1.0 - g1)
        dgl_ref[:, 0:D_MODEL] = dgl0.astype(BF16)
        dgl_ref[:, D_MODEL:2 * D_MODEL] = dgl1.astype(BF16)
        dbg_ref[:, 0:D_MODEL] += jnp.sum(dgl0, axis=0, keepdims=True)
        dbg_ref[:, D_MODEL:2 * D_MODEL] += jnp.sum(dgl1, axis=0, keepdims=True)
        dya_ref[...] = _dot_nt(dpa, w_ref[0])
        dyb_ref[...] = _dot_nt(dpb, w_ref[1])

    row = pl.BlockSpec((tm, D_MODEL), lambda i: (i, 0))
    row1 = pl.BlockSpec((tm, D_MODEL), lambda i: (i, 1))
    row2 = pl.BlockSpec((tm, 2 * D_MODEL), lambda i: (i, 0))
    vec = pl.BlockSpec((1, D_MODEL), lambda i: (0, 0))
    vec1 = pl.BlockSpec((1, D_MODEL), lambda i: (0, 1))
    vec2 = pl.BlockSpec((1, 2 * D_MODEL), lambda i: (0, 0))
    f32_rows = jax.ShapeDtypeStruct((s, D_MODEL), F32)
    bf16_rows = jax.ShapeDtypeStruct((s, D_MODEL), BF16)
    return pl.pallas_call(
        body, name="merge_fused", grid=(s // tm,),
        in_specs=[row, row, row, row1, vec, vec1, row, row, vec, pl.BlockSpec((3, D_MODEL, D_MODEL), lambda i: (0, 0, 0))],
        out_specs=[row, row, row, row, row, row2, row, row, pl.BlockSpec((1, 128), lambda i: (0, 0)), vec, vec2],
        out_shape=[bf16_rows, f32_rows, bf16_rows, f32_rows, f32_rows, jax.ShapeDtypeStruct((s, 2 * D_MODEL), BF16),
                   bf16_rows, bf16_rows, jax.ShapeDtypeStruct((1, 128), F32), jax.ShapeDtypeStruct((1, D_MODEL), F32),
                   jax.ShapeDtypeStruct((1, 2 * D_MODEL), F32)],
        compiler_params=_params(1),
    )(ya, yb, glog, glog, b_gate, b_gate, x, tgt, fg, wproj)


def _proj_fused(x, g, w_int, r_blk):
    s = x.shape[0]
    tm = min(TM_FUSED, s)

    def body(x_ref, g_ref, w_hbm, r_ref, h_ref, hg_ref, ms_ref, mz_ref, gl_ref, or_ref, w_s, sem, send_sems, recv_sems):
        mx, my, mc = _me()
        chips = _other_chips(mx, my)
        mine, theirs = _cols(mc), _cols(1 - mc)

        def copy(k, src, dst, to):
            return pltpu.make_async_remote_copy(src_ref=src, dst_ref=dst, send_sem=send_sems.at[k],
                                                recv_sem=recv_sems.at[k], device_id=to, device_id_type=MESH_ID)

        def sends():
            return [copy(j, r_ref.at[:, mine], or_ref.at[2 * mx + my, :, mine], (cx, cy, mc))
                    for j, (cx, cy) in enumerate(chips)]

        @pl.when(pl.program_id(0) == 0)
        def _():
            for cp in sends():
                cp.start()
            cp = pltpu.make_async_copy(w_hbm, w_s, sem)
            cp.start()
            cp.wait()

        h = _rms(x_ref[...], g_ref[...]).astype(BF16)
        h_ref[...] = h
        for j in range(4):
            hg_ref[j] = _dot_nt(h, w_s[j * D_MODEL:(j + 1) * D_MODEL, :])
        ms = _dot_nt(h, w_s[4096:4096 + MS_COLS, :])
        lane = lax.broadcasted_iota(jnp.int32, ms.shape, 1)
        ms_ref[...] = jnp.where(lane < 704, ms, 0.0)
        mz_ref[...] = _dot_nt(h, w_s[4800:5824, :])
        for j in range(2):
            gl_ref[:, j * D_MODEL:(j + 1) * D_MODEL] = _dot_nt(h, w_s[5824 + j * D_MODEL:5824 + (j + 1) * D_MODEL, :])

        @pl.when(pl.program_id(0) == s // tm - 1)
        def _():
            passed = []
            for j, (cx, cy) in enumerate(chips):
                landed = or_ref.at[2 * cx + cy, :, mine]
                copy(j, landed, landed, (cx, cy, mc)).wait_recv()
                fwd = copy(3 + j, landed, landed, (mx, my, 1 - mc))
                fwd.start()
                passed.append(fwd)
            for j, (cx, cy) in enumerate(chips):
                other = or_ref.at[2 * cx + cy, :, theirs]
                copy(3 + j, other, other, (mx, my, 1 - mc)).wait_recv()
            for cp in sends() + passed:
                cp.wait_send()

    row = pl.BlockSpec((tm, D_MODEL), lambda i: (i, 0))
    outs = pl.pallas_call(
        body, name="proj_fused", grid=(s // tm,),
        in_specs=[row, pl.BlockSpec((1, D_MODEL), lambda i: (0, 0)), ANY, ANY],
        out_specs=[row, pl.BlockSpec((4, tm, D_MODEL), lambda i: (0, i, 0)), pl.BlockSpec((tm, MS_COLS), lambda i: (i, 0)),
                   row, pl.BlockSpec((tm, 2 * D_MODEL), lambda i: (i, 0)), ANY],
        out_shape=[jax.ShapeDtypeStruct((s, D_MODEL), BF16), jax.ShapeDtypeStruct((4, s, D_MODEL), F32),
                   jax.ShapeDtypeStruct((s, MS_COLS), F32), jax.ShapeDtypeStruct((s, D_MODEL), F32),
                   jax.ShapeDtypeStruct((s, 2 * D_MODEL), F32), jax.ShapeDtypeStruct((N_CHIPS,) + r_blk.shape, r_blk.dtype)],
        scratch_shapes=[pltpu.VMEM(w_int.shape, BF16), pltpu.SemaphoreType.DMA,
                        pltpu.SemaphoreType.DMA((6,)), pltpu.SemaphoreType.DMA((6,))],
        compiler_params=_params(1),
    )(x, g, w_int, r_blk)
    gr = lax.dynamic_update_slice(outs[5], r_blk[None], (2 * lax.axis_index("x") + lax.axis_index("y"), 0, 0))
    return (*outs[:5], gr)


def _dh_fused(dhg, dms, dmz, dglog, w_int, x, g, dx2, hw, hr):
    s = x.shape[0]
    tm = min(512, s)
    nw, nr = hw.shape[0] // N_CHIPS, hr.shape[0] // N_CHIPS

    def body(dhg_ref, dms_ref, dmz_ref, dgl_ref, w_hbm, x_ref, g_ref, dx2_ref, hw_ref, hr_ref,
             dx_ref, dg_ref, lw_ref, lr_ref, w_s, sem, send_sems, recv_sems):
        def scatter_copies():
            mx, my, mc = _me()
            return [pltpu.make_async_remote_copy(
                src_ref=src.at[pl.ds((2 * cx + cy) * n, n), :], dst_ref=dst.at[j], send_sem=send_sems.at[3 * a + j],
                recv_sem=recv_sems.at[3 * a + j], device_id=(cx, cy, mc), device_id_type=MESH_ID)
                for a, (src, dst, n) in enumerate([(hw_ref, lw_ref, nw), (hr_ref, lr_ref, nr)])
                for j, (cx, cy) in enumerate(_other_chips(mx, my))]

        @pl.when(pl.program_id(0) == 0)
        def _():
            for cp in scatter_copies():
                cp.start()
            dg_ref[...] = jnp.zeros_like(dg_ref)
            cp = pltpu.make_async_copy(w_hbm, w_s, sem)
            cp.start()
            cp.wait()

        dh = _dot(dms_ref[...], w_s[4096:4096 + MS_COLS, :]) + _dot(dmz_ref[...], w_s[4800:5824, :])
        for j in range(4):
            dh += _dot(dhg_ref[j], w_s[j * D_MODEL:(j + 1) * D_MODEL, :])
        for j in range(2):
            dh += _dot(dgl_ref[:, j * D_MODEL:(j + 1) * D_MODEL], w_s[5824 + j * D_MODEL:5824 + (j + 1) * D_MODEL, :])
        dx, dg_rows = _rms_bwd(x_ref[...], g_ref[...], dh)
        dx_ref[...] = dx + dx2_ref[...]
        dg_ref[...] += jnp.sum(dg_rows, axis=0, keepdims=True)

        @pl.when(pl.program_id(0) == s // tm - 1)
        def _():
            for cp in scatter_copies():
                cp.wait()

    row = pl.BlockSpec((tm, D_MODEL), lambda i: (i, 0))
    vec = pl.BlockSpec((1, D_MODEL), lambda i: (0, 0))
    return pl.pallas_call(
        body, name="dh_fused", grid=(s // tm,),
        in_specs=[pl.BlockSpec((4, tm, D_MODEL), lambda i: (0, i, 0)), pl.BlockSpec((tm, MS_COLS), lambda i: (i, 0)), row,
                  pl.BlockSpec((tm, 2 * D_MODEL), lambda i: (i, 0)), ANY, row, vec, row, ANY, ANY],
        out_specs=[row, vec, ANY, ANY],
        out_shape=[jax.ShapeDtypeStruct((s, D_MODEL), F32), jax.ShapeDtypeStruct((1, D_MODEL), F32),
                   jax.ShapeDtypeStruct((3, nw, HALF_COLS), hw.dtype), jax.ShapeDtypeStruct((3, nr, HALF_COLS), hr.dtype)],
        scratch_shapes=[pltpu.VMEM(w_int.shape, BF16), pltpu.SemaphoreType.DMA,
                        pltpu.SemaphoreType.DMA((6,)), pltpu.SemaphoreType.DMA((6,))],
        compiler_params=_params(1),
    )(dhg, dms, dmz, dglog, w_int, x, g, dx2, hw, hr)


def _local_step(x, tgt, w_int, r_blk, norm_g, b_gate, lb_logits, hg_norm_g, q_a_g, kv_a_g, fg):
    s = x.shape[0]
    tabs = _rope_tables(s)

    h, hg, ms, mz, glog, gr = _proj_fused(x, norm_g, w_int, r_blk)
    w_uq, w_ukv, wproj = _unpack_rest_weights(gr)
    wuq3 = jnp.pad(w_uq.reshape(Q_LORA, HEADS, QK_DIM).transpose(1, 0, 2), ((0, 0), (0, 0), (0, QK_PAD - QK_DIM)))
    wukv3 = w_ukv.reshape(KV_LORA, HEADS, 256).transpose(1, 0, 2)
    o_pre, ya, st0 = _hgrn_fwd(hg, lb_logits, hg_norm_g)
    q, k, v, cqn, ckvn = _mla_pre(ms, q_a_g, kv_a_g, wuq3, wukv3, tabs)
    o_att, yb, lse = _flash_fwd(q, k, v, mz)
    merged, dx2, dx2b, dya, dyb, dglog, dpa, dpb, loss, dfg, dbg = _merge_fused(ya, yb, glog, b_gate, x, tgt, fg, wproj)

    d_wout = _mm_tn(merged, dx2b, name="dw_out")
    d_wpa = _mm_tn(ya, dpa, name="dw_proj_a")
    d_wpb = _mm_tn(yb, dpb, name="dw_proj_b")
    dhg, dlb, dhgg = _hgrn_bwd(hg, o_pre, dya, st0, lb_logits, hg_norm_g)
    dq, dk, dv, dmz = _flash_bwd(q, k, v, dyb, mz, o_att, lse, tabs)
    dms, d_wuq3, d_wukv3, dqg, dkvg = _mla_bwd_proj(dq, dk, dv, cqn, ckvn, ms, q_a_g, kv_a_g, wuq3, wukv3, tabs)
    d_hg = _mm_tn(dhg, h, name="dw_in_hg")
    d_ms = _mm_tn(dms, h, name="dw_in_ms")
    d_mz = _mm_tn(dmz, h, name="dw_in_mz")
    d_gl = _mm_tn(dglog, h, name="dw_in_gate")
    d_w_int = jnp.concatenate([d_hg.reshape(4 * D_MODEL, D_MODEL), d_ms[0:704], d_mz, d_gl], axis=0)
    small = {"b_gate": dbg, "lb": dlb, "hg_norm_g": dhgg, "q_a_g": dqg, "kv_a_g": dkvg, "final_norm_g": dfg}
    dh_args = (dhg, dms, dmz, dglog, w_int, x, norm_g, dx2)
    return loss, dh_args, d_w_int, d_wuq3, d_wukv3, (d_wpa, d_wpb, d_wout), small


def _pack_rest(w_uq_b, w_ukv_b, wpa_b, wpb_b, wout_b):
    return jnp.concatenate([w_uq_b.reshape(144, D_MODEL), w_ukv_b.reshape(128, D_MODEL), wpa_b, wpb_b, wout_b], axis=0)


def _unpack_rest(p):
    return (p[0:144].reshape(Q_LORA, 384), p[144:272].reshape(KV_LORA, 512), p[272:528], p[528:784], p[784:1040])


def _pack_rest_grads(d_wuq3, d_wukv3, d_proj):
    d_wuq = d_wuq3.transpose(1, 0, 2)[:, :, 0:QK_DIM].reshape(Q_LORA, HEADS * QK_DIM)
    d_wukv = d_wukv3.transpose(1, 0, 2).reshape(KV_LORA, HEADS * 256)
    blocks = []
    for b in range(N_CHIPS):
        rows = slice(b * 256, (b + 1) * 256)
        blocks.append(_pack_rest(d_wuq[:, b * 384:(b + 1) * 384], d_wukv[:, b * 512:(b + 1) * 512],
                                 d_proj[0][rows], d_proj[1][rows], d_proj[2][rows]))
    return jnp.stack(blocks, axis=0)


def _unpack_rest_weights(g):
    parts = [_unpack_rest(g[b]) for b in range(N_CHIPS)]
    w_uq, w_ukv = (jnp.concatenate([p[n] for p in parts], axis=1) for n in range(2))
    wproj = jnp.stack([jnp.concatenate([p[n] for p in parts], axis=0) for n in range(2, 5)], axis=0)
    return w_uq, w_ukv, wproj


MESH_ID = pl.DeviceIdType.MESH
ANY = pl.BlockSpec(memory_space=pl.ANY)
HALF_COLS = D_MODEL // 2


def _me():
    return lax.axis_index("x"), lax.axis_index("y"), lax.axis_index("c")


def _other_chips(x, y):
    return [(1 - x, y), (x, 1 - y), (1 - x, 1 - y)]


def _cols(c):
    return pl.ds(c * HALF_COLS, HALF_COLS)


RELAY_TOP = 992


def _gather_weights(w_blk):
    bot = W_IN_BLK - RELAY_TOP

    def body(w_ref, ow_ref, send_sems, recv_sems):
        x, y, c = _me()
        me, xn, yn, dg = 2 * x + y, 2 * (1 - x) + y, 2 * x + (1 - y), 2 * (1 - x) + (1 - y)
        to_x, to_y, to_sib = (1 - x, y, c), (x, 1 - y, c), (x, y, 1 - c)
        mine, theirs = _cols(c), _cols(1 - c)
        top, low = pl.ds(0, RELAY_TOP), pl.ds(RELAY_TOP, bot)

        def copy(k, src, dst, to):
            return pltpu.make_async_remote_copy(src_ref=src, dst_ref=dst, send_sem=send_sems.at[k],
                                                recv_sem=recv_sems.at[k], device_id=to, device_id_type=MESH_ID)

        def same(k, ref, to):
            return copy(k, ref, ref, to)

        own = [copy(0, w_ref.at[:, mine], ow_ref.at[me, :, mine], to_x),
               copy(1, w_ref.at[:, mine], ow_ref.at[me, :, mine], to_y)]
        for cp in own:
            cp.start()
        from_x, from_y = ow_ref.at[xn, :, mine], ow_ref.at[yn, :, mine]
        same(0, from_x, to_x).wait_recv()
        relay_y = same(2, ow_ref.at[xn, top, mine], to_y)
        pass_x = same(4, from_x, to_sib)
        relay_y.start()
        pass_x.start()
        same(1, from_y, to_y).wait_recv()
        relay_x = same(3, ow_ref.at[yn, low, mine], to_x)
        pass_y = same(5, from_y, to_sib)
        relay_x.start()
        pass_y.start()
        same(2, ow_ref.at[dg, top, mine], to_y).wait_recv()
        same(3, ow_ref.at[dg, low, mine], to_x).wait_recv()
        pass_d = same(6, ow_ref.at[dg, :, mine], to_sib)
        pass_d.start()
        for k, blk in ((4, xn), (5, yn), (6, dg)):
            same(k, ow_ref.at[blk, :, theirs], to_sib).wait_recv()
        for cp in own + [relay_y, relay_x, pass_x, pass_y, pass_d]:
            cp.wait_send()

    gw = pl.pallas_call(
        body, name="gather_weights", in_specs=[ANY], out_specs=ANY,
        out_shape=jax.ShapeDtypeStruct((N_CHIPS,) + w_blk.shape, w_blk.dtype),
        scratch_shapes=[pltpu.SemaphoreType.DMA((7,)), pltpu.SemaphoreType.DMA((7,))],
    )(w_blk)
    return lax.dynamic_update_slice(gw, w_blk[None], (2 * lax.axis_index("x") + lax.axis_index("y"), 0, 0))


def _swap_halves(gw, gr):
    def body(gw_ref, gr_ref, lw_ref, lr_ref, send_sems, recv_sems):
        x, y, c = _me()
        cps = [pltpu.make_async_remote_copy(
            src_ref=src, dst_ref=dst, send_sem=send_sems.at[a], recv_sem=recv_sems.at[a],
            device_id=(x, y, 1 - c), device_id_type=MESH_ID)
            for a, (src, dst) in enumerate([(gw_ref.at[:, _cols(1 - c)], lw_ref),
                                            (gr_ref.at[:, :, _cols(1 - c)], lr_ref)])]
        for cp in cps:
            cp.start()
        for cp in cps:
            cp.wait()

    return pl.pallas_call(
        body, name="grad_swap_halves", in_specs=[ANY, ANY], out_specs=[ANY, ANY],
        out_shape=[jax.ShapeDtypeStruct((gw.shape[0], HALF_COLS), gw.dtype),
                   jax.ShapeDtypeStruct(gr.shape[:2] + (HALF_COLS,), gr.dtype)],
        scratch_shapes=[pltpu.SemaphoreType.DMA((2,)), pltpu.SemaphoreType.DMA((2,))],
    )(gw, gr)


def _swap_reduced(rw, rr):
    def body(rw_ref, rr_ref, ow_ref, or_ref, send_sems, recv_sems):
        x, y, c = _me()
        cps = [pltpu.make_async_remote_copy(
            src_ref=src, dst_ref=dst, send_sem=send_sems.at[a], recv_sem=recv_sems.at[a],
            device_id=(x, y, 1 - c), device_id_type=MESH_ID)
            for a, (src, dst) in enumerate([(rw_ref, ow_ref), (rr_ref, or_ref)])]
        for cp in cps:
            cp.start()
        for cp in cps:
            cp.wait()

    return pl.pallas_call(
        body, name="grad_swap_reduced", in_specs=[ANY, ANY], out_specs=[ANY, ANY],
        out_shape=[jax.ShapeDtypeStruct(rw.shape, rw.dtype), jax.ShapeDtypeStruct(rr.shape, rr.dtype)],
        scratch_shapes=[pltpu.SemaphoreType.DMA((2,)), pltpu.SemaphoreType.DMA((2,))],
    )(rw, rr)


def _join_cols(mine, theirs):
    first = lax.axis_index("c") == 0
    return jnp.concatenate([jnp.where(first, mine, theirs), jnp.where(first, theirs, mine)], axis=1)


def _gather_small(vec):
    def body(v_ref, out_ref, send_sems, recv_sems, local_sem):
        x, y, c = _me()
        my_id = 4 * x + 2 * y + c
        mine = pltpu.make_async_copy(v_ref, out_ref.at[my_id], local_sem)
        mine.start()
        cps = []
        for r in range(1, N_DEV):
            peer = (x ^ (r >> 2), y ^ ((r >> 1) & 1), c ^ (r & 1))
            cps.append(pltpu.make_async_remote_copy(
                src_ref=v_ref, dst_ref=out_ref.at[my_id], send_sem=send_sems.at[r - 1],
                recv_sem=recv_sems.at[r - 1], device_id=peer, device_id_type=MESH_ID))
        for cp in cps:
            cp.start()
        for cp in cps:
            cp.wait()
        mine.wait()

    return pl.pallas_call(
        body, name="gather_small", in_specs=[ANY], out_specs=ANY,
        out_shape=jax.ShapeDtypeStruct((N_DEV, 1, SMALL_COLS), vec.dtype),
        scratch_shapes=[pltpu.SemaphoreType.DMA((N_DEV - 1,)), pltpu.SemaphoreType.DMA((N_DEV - 1,)),
                        pltpu.SemaphoreType.DMA],
    )(vec)


def _add_cores(c_idx, g, landed, *, tm, name):
    r = g.shape[0]

    def body(c_ref, g_ref, l_ref, o32_ref, o16_ref):
        acc = g_ref[...] + l_ref[...]
        o32_ref[...] = acc
        o16_ref[...] = acc.astype(BF16)

    half = pl.BlockSpec((tm, HALF_COLS), lambda i, c_ref: (i, 0))
    grid_spec = pltpu.PrefetchScalarGridSpec(
        num_scalar_prefetch=1, grid=(r // tm,),
        in_specs=[pl.BlockSpec((tm, HALF_COLS), lambda i, c_ref: (i, c_ref[0])), half], out_specs=[half, half])
    return pl.pallas_call(
        body, name=name, grid_spec=grid_spec,
        out_shape=[jax.ShapeDtypeStruct((r, HALF_COLS), F32), jax.ShapeDtypeStruct((r, HALF_COLS), BF16)],
        compiler_params=_params(1),
    )(c_idx, g, landed)


def _add_chips(chip_idx, h32, landed, *, tm, name):
    n = landed.shape[1]
    per = n // tm

    def body(chip_ref, h_ref, l_ref, o_ref):
        acc = h_ref[...]
        for j in range(3):
            acc = acc + l_ref[j].astype(F32)
        o_ref[...] = acc

    grid_spec = pltpu.PrefetchScalarGridSpec(
        num_scalar_prefetch=1, grid=(per,),
        in_specs=[pl.BlockSpec((tm, HALF_COLS), lambda i, chip_ref: (chip_ref[0] * per + i, 0)),
                  pl.BlockSpec((3, tm, HALF_COLS), lambda i, chip_ref: (0, i, 0))],
        out_specs=pl.BlockSpec((tm, HALF_COLS), lambda i, chip_ref: (i, 0)))
    return pl.pallas_call(
        body, name=name, grid_spec=grid_spec, out_shape=jax.ShapeDtypeStruct((n, HALF_COLS), F32),
        compiler_params=_params(1),
    )(chip_idx, h32, landed)


def _pack_small(small, lb_logits, loss):
    def body(ng_ref, bg_ref, dlb_ref, lbl_ref, hgg_ref, qg_ref, kvg_ref, fg_ref, loss_ref, out_ref):
        out_ref[...] = jnp.zeros_like(out_ref)
        out_ref[:, 0:1024] = ng_ref[...]
        out_ref[:, 1024:3072] = bg_ref[...]
        _, p0p1 = _lower_bound(lbl_ref[...])
        dl0 = dlb_ref[...] * p0p1
        out_ref[:, 3072:4096] = dl0
        out_ref[:, 4096:5120] = -dl0
        hgg = hgg_ref[0]
        for h in range(1, HEADS):
            hgg = hgg + hgg_ref[h]
        out_ref[:, 5120:5248] = hgg
        out_ref[:, 5248:5632] = qg_ref[...]
        out_ref[:, 5632:5888] = kvg_ref[...]
        out_ref[:, 5888:6912] = fg_ref[...]
        out_ref[:, 6912:7040] = loss_ref[...]

    return pl.pallas_call(
        body, name="pack_small", out_shape=jax.ShapeDtypeStruct((1, SMALL_COLS), F32),
    )(small["norm_g"], small["b_gate"], small["lb"], lb_logits, small["hg_norm_g"], small["q_a_g"],
      small["kv_a_g"], small["final_norm_g"], loss)


def _adamw_math(w, g, m, v):
    nm = ADAM_B1 * m + (1.0 - ADAM_B1) * g
    nv = ADAM_B2 * v + (1.0 - ADAM_B2) * (g * g)
    m_hat = nm / (1.0 - ADAM_B1 ** ADAM_STEP)
    v_hat = nv / (1.0 - ADAM_B2 ** ADAM_STEP)
    return -ADAM_LR * (m_hat / (jnp.sqrt(v_hat) + ADAM_EPS) + ADAM_WD * w), nm, nv


def _adamw(w, g, m, v, *, name, tm):
    r, cols = w.shape

    def body(w_ref, g_ref, m_ref, v_ref, d_ref, nm_ref, nv_ref):
        d_ref[...], nm_ref[...], nv_ref[...] = _adamw_math(w_ref[...], g_ref[...], m_ref[...], v_ref[...])

    row = pl.BlockSpec((tm, cols), lambda i: (i, 0))
    shp = jax.ShapeDtypeStruct((r, cols), F32)
    return pl.pallas_call(
        body, name=name, grid=(r // tm,), in_specs=[row] * 4, out_specs=[row] * 3, out_shape=[shp] * 3,
        compiler_params=_params(1),
    )(w, g, m, v)


SMALL_SLOTS = (("norm_g", (0,)), ("b_gate", (1024,)), ("lb_logits", (3072, 4096)), ("hg_norm_g", (5120,)),
               ("q_a_g", (5248,)), ("kv_a_g", (5632,)), ("final_norm_g", (5888,)))
LOSS_SLOT = 6912


def _small_update(gathered, ws, ms, vs):
    n = len(SMALL_SLOTS)

    def body(*refs):
        g_ref = refs[0]
        w_refs, m_refs, v_refs = refs[1:1 + n], refs[1 + n:1 + 2 * n], refs[1 + 2 * n:1 + 3 * n]
        outs = refs[1 + 3 * n:]
        loss_ref = outs[0]
        g_out, d_out, nm_out, nv_out = (outs[1 + k * n:1 + (k + 1) * n] for k in range(4))
        total = g_ref[0]
        for dev in range(1, N_DEV):
            total = total + g_ref[dev]
        loss_ref[...] = total[:, LOSS_SLOT:LOSS_SLOT + 128]
        for p, (_, offsets) in enumerate(SMALL_SLOTS):
            cols = w_refs[p].shape[1]
            for r, off in enumerate(offsets):
                rows = slice(r, r + 1)
                g = total[:, off:off + cols]
                g_out[p][rows, :] = g
                d_out[p][rows, :], nm_out[p][rows, :], nv_out[p][rows, :] = _adamw_math(
                    w_refs[p][rows, :], g, m_refs[p][rows, :], v_refs[p][rows, :])

    shapes = [jax.ShapeDtypeStruct(w.shape, F32) for w in ws]
    res = pl.pallas_call(
        body, name="small_update", out_shape=[jax.ShapeDtypeStruct((1, 128), F32)] + shapes * 4,
    )(gathered, *ws, *ms, *vs)
    return res[0], res[1:1 + n], res[1 + n:1 + 2 * n], res[1 + 2 * n:1 + 3 * n], res[1 + 3 * n:1 + 4 * n]


def kernel(x, norm_g, w_in, b_gate, lb_logits, hg_norm_g, q_a_g, w_uq, kv_a_g, w_ukv, w_proj_a, w_proj_b, w_out, final_norm_g, loss_target, m_norm_g, m_w_in, m_b_gate, m_lb_logits, m_hg_norm_g, m_q_a_g, m_w_uq, m_kv_a_g, m_w_ukv, m_w_proj_a, m_w_proj_b, m_w_out, m_final_norm_g, v_norm_g, v_w_in, v_b_gate, v_lb_logits, v_hg_norm_g, v_q_a_g, v_w_uq, v_kv_a_g, v_w_ukv, v_w_proj_a, v_w_proj_b, v_w_out, v_final_norm_g):
    c_idx = lax.axis_index("c").astype(jnp.int32).reshape(1)
    chip_idx = (2 * lax.axis_index("x") + lax.axis_index("y")).astype(jnp.int32).reshape(1)

    w_blk = w_in[0].T.astype(BF16)
    r_blk = _pack_rest(w_uq[0], w_ukv[0], w_proj_a[0], w_proj_b[0], w_out[0]).astype(BF16)
    gw = _gather_weights(w_blk)

    loss, dh_args, d_w_int, d_wuq3, d_wukv3, d_proj, small = _local_step(
        x[0], loss_target[0], gw.reshape(W_IN_COLS, D_MODEL), r_blk,
        norm_g, b_gate, lb_logits, hg_norm_g, q_a_g, kv_a_g, final_norm_g.reshape(1, D_MODEL))

    d_rest = _pack_rest_grads(d_wuq3, d_wukv3, d_proj)
    lw, lr = _swap_halves(d_w_int, d_rest)
    hw32, hw16 = _add_cores(c_idx, d_w_int, lw, tm=656, name="grad_add_cores_w")
    hr32, hr16 = _add_cores(c_idx, d_rest.reshape(N_CHIPS * REST_ROWS, D_MODEL), lr.reshape(N_CHIPS * REST_ROWS, HALF_COLS),
                            tm=REST_ROWS, name="grad_add_cores_r")
    grad_x, small["norm_g"], landed_w, landed_r = _dh_fused(*dh_args, hw16, hr16)
    rw = _add_chips(chip_idx, hw32, landed_w, tm=656, name="grad_add_chips_w")
    rr = _add_chips(chip_idx, hr32, landed_r, tm=208, name="grad_add_chips_r")
    tw, tr = _swap_reduced(rw, rr)
    g_w_in = _join_cols(rw, tw).T
    g_rest = _join_cols(rr, tr)
    g_uq, g_ukv, g_pa, g_pb, g_out = _unpack_rest(g_rest)

    small_all = _gather_small(_pack_small(small, lb_logits, loss))

    upd = {
        "w_in": _adamw(w_in[0], g_w_in, m_w_in[0], v_w_in[0], name="adamw_w_in", tm=128),
        "w_uq": _adamw(w_uq[0], g_uq, m_w_uq[0], v_w_uq[0], name="adamw_w_uq", tm=Q_LORA),
        "w_ukv": _adamw(w_ukv[0], g_ukv, m_w_ukv[0], v_w_ukv[0], name="adamw_w_ukv", tm=KV_LORA),
        "w_proj_a": _adamw(w_proj_a[0], g_pa, m_w_proj_a[0], v_w_proj_a[0], name="adamw_w_proj_a", tm=256),
        "w_proj_b": _adamw(w_proj_b[0], g_pb, m_w_proj_b[0], v_w_proj_b[0], name="adamw_w_proj_b", tm=256),
        "w_out": _adamw(w_out[0], g_out, m_w_out[0], v_w_out[0], name="adamw_w_out", tm=256),
    }
    loss_vec, *small_sets = _small_update(
        small_all,
        [norm_g, b_gate, lb_logits, hg_norm_g, q_a_g, kv_a_g, final_norm_g.reshape(1, D_MODEL)],
        [m_norm_g, m_b_gate, m_lb_logits, m_hg_norm_g, m_q_a_g, m_kv_a_g, m_final_norm_g.reshape(1, D_MODEL)],
        [v_norm_g, v_b_gate, v_lb_logits, v_hg_norm_g, v_q_a_g, v_kv_a_g, v_final_norm_g.reshape(1, D_MODEL)])

    def outputs(big, small_set):
        s_ng, s_bg, s_lb, s_hg, s_qg, s_kvg, s_fg = small_set
        return (s_ng, big["w_in"][None], s_bg, s_lb, s_hg, s_qg, big["w_uq"][None], s_kvg, big["w_ukv"][None],
                big["w_proj_a"][None], big["w_proj_b"][None], big["w_out"][None], s_fg.reshape(D_MODEL))

    grads = {"w_in": g_w_in, "w_uq": g_uq, "w_ukv": g_ukv, "w_proj_a": g_pa, "w_proj_b": g_pb, "w_out": g_out}
    return (loss_vec[0, 0], grad_x[None], *outputs(grads, small_sets[0]),
            *(o for k in range(3) for o in outputs({n: u[k] for n, u in upd.items()}, small_sets[1 + k])))
```

```python
import jax
import jax.numpy as jnp
from jax import lax
from jax.experimental import pallas as pl
from jax.experimental.pallas import tpu as pltpu

F32 = jnp.float32
BF16 = jnp.bfloat16

D_MODEL = 1024
HEADS = 8
HEAD_DIM = 128
HG_CHUNK = 32
CHUNK_SHIFT = 5
HEAD_SHIFT = 7
QK_NOPE = 128
QK_ROPE = 64
QK_DIM = QK_NOPE + QK_ROPE
QK_PAD = 256
Q_LORA = 384
KV_LORA = 256
MS_COLS = 768
ROPE_THETA = 10000.0
EPS = 1e-6
ATT_SCALE = QK_DIM ** -0.5
LOG2E = 1.4426950408889634
LN2 = 0.6931471805599453
Q_PRESCALE = ATT_SCALE * LOG2E

ADAM_LR = 0.001
ADAM_B1 = 0.9
ADAM_B2 = 0.999
ADAM_EPS = 1e-08
ADAM_WD = 0.01
ADAM_STEP = 10

N_CHIPS = 4
N_DEV = 8
W_IN_COLS = 7872
W_IN_BLK = W_IN_COLS // N_CHIPS
REST_ROWS = 144 + 128 + 3 * 256
SMALL_COLS = 7168

TM_MM = 2048
TM_FUSED = 256
TM_MLA = 512
HG_ROWS = 128
TQ = 512
FLASH_HEADS = 2
HG_HEADS = 8
VMEM_LIMIT = 56 * 1024 * 1024


def _dot(a, b):
    return lax.dot_general(a, b, (((1,), (0,)), ((), ())), preferred_element_type=F32)


def _dot_nt(a, b):
    return lax.dot_general(a, b, (((1,), (1,)), ((), ())), preferred_element_type=F32)


def _dot_tn(a, b):
    return lax.dot_general(a, b, (((0,), (0,)), ((), ())), preferred_element_type=F32)


def _params(n_axes):
    return pltpu.CompilerParams(dimension_semantics=("arbitrary",) * n_axes, vmem_limit_bytes=VMEM_LIMIT)


def _rms(x, g):
    r = lax.rsqrt(jnp.mean(x * x, axis=-1, keepdims=True) + EPS)
    return x * r * g


def _rms_bwd(x, g, dy):
    r = lax.rsqrt(jnp.mean(x * x, axis=-1, keepdims=True) + EPS)
    xh = x * r
    dyg = dy * g
    dx = r * (dyg - xh * jnp.mean(dyg * xh, axis=-1, keepdims=True))
    return dx, dy * xh


def _silu_parts(z):
    s = jax.nn.sigmoid(z)
    return z * s, s * (1.0 + z * (1.0 - s))


def _rope(x, c, sa, sb):
    return x * c + pltpu.roll(x, 32, 1) * sa + pltpu.roll(x, 96, 1) * sb


def _rope_bwd(dy, c, sa, sb):
    return dy * c + pltpu.roll(dy * sa, 96, 1) + pltpu.roll(dy * sb, 32, 1)


def _rope_tables(seq):
    inv = ROPE_THETA ** (-jnp.arange(0, QK_ROPE, 2, dtype=F32) / QK_ROPE)
    ang = jnp.arange(seq, dtype=F32)[:, None] * inv[None, :]
    cos, sin = jnp.cos(ang), jnp.sin(ang)
    z32 = jnp.zeros_like(cos)
    z64 = jnp.zeros((seq, 64), F32)
    c = jnp.concatenate([cos, cos, z64], axis=1)
    sa = jnp.concatenate([z32, sin, z64], axis=1)
    sb = jnp.concatenate([-sin, z32, z64], axis=1)
    return c, sa, sb


def _mm_tn(a, b, *, name, tm=TM_MM, tn=1024):
    flat = a.ndim == 2
    if flat:
        a = a[None]
    g, m, k = a.shape
    n = b.shape[1]
    tm, tn = min(tm, m), min(tn, n)
    assert m % tm == 0 and n % tn == 0

    def body(a_ref, b_ref, o_ref):
        @pl.when(pl.program_id(2) == 0)
        def _():
            o_ref[...] = jnp.zeros_like(o_ref)

        o_ref[...] += _dot_tn(a_ref[...], b_ref[...])

    out = pl.pallas_call(
        body, name=name, grid=(g, n // tn, m // tm),
        in_specs=[pl.BlockSpec((None, tm, k), lambda s, j, i: (s, i, 0)),
                  pl.BlockSpec((tm, tn), lambda s, j, i: (i, j))],
        out_specs=pl.BlockSpec((None, k, tn), lambda s, j, i: (s, 0, j)),
        out_shape=jax.ShapeDtypeStruct((g, k, n), F32), compiler_params=_params(3),
    )(a, b)
    return out[0] if flat else out


def _chunk_rows(rows):
    return lax.broadcasted_iota(jnp.int32, (rows, HEAD_DIM), 0) & (HG_CHUNK - 1)


def _chunk_cumsum(x, rows):
    pos = _chunk_rows(rows)
    shift = 1
    while shift < HG_CHUNK:
        x = x + jnp.where(pos >= shift, pltpu.roll(x, shift, 0), 0.0)
        shift *= 2
    return x


def _chunk_revcumsum(x, rows):
    pos = _chunk_rows(rows)
    shift = 1
    while shift < HG_CHUNK:
        x = x + jnp.where(pos + shift < HG_CHUNK, pltpu.roll(x, rows - shift, 0), 0.0)
        shift *= 2
    return x


def _chunk_last(x, rows):
    x3 = x.reshape(rows // HG_CHUNK, HG_CHUNK, HEAD_DIM)
    return jnp.broadcast_to(x3[:, HG_CHUNK - 1:HG_CHUNK, :], x3.shape).reshape(rows, HEAD_DIM)


def _lower_bound(lbl):
    mx = jnp.maximum(lbl[0:1, :], lbl[1:2, :])
    e0 = jnp.exp(lbl[0:1, :] - mx)
    e1 = jnp.exp(lbl[1:2, :] - mx)
    p0 = e0 / (e0 + e1)
    return p0, p0 * (e1 / (e0 + e1))


def _hg_masks(rows, nch, tmask_s, bdmask_s):
    r = lax.broadcasted_iota(jnp.int32, (rows, rows), 0)
    c = lax.broadcasted_iota(jnp.int32, (rows, rows), 1)
    tmask_s[...] = jnp.where(((r >> CHUNK_SHIFT) == (c >> CHUNK_SHIFT)) & (r >= c), 1.0, 0.0)
    r = lax.broadcasted_iota(jnp.int32, (rows, nch * HEAD_DIM), 0)
    c = lax.broadcasted_iota(jnp.int32, (rows, nch * HEAD_DIM), 1)
    bdmask_s[...] = jnp.where((r >> CHUNK_SHIFT) == (c >> HEAD_SHIFT), 1.0, 0.0).astype(BF16)


def _block_diag(x, nch, bdmask):
    return jnp.tile(x, (1, nch)) * bdmask


def _hgrn_fwd(hg, lb_logits, norm_g):
    s = hg.shape[1]
    rows = min(HG_ROWS, s)
    nblk = s // rows
    nch = rows // HG_CHUNK

    def body(hg_ref, lbl_ref, g_ref, o_ref, ya_ref, st0_ref, st_s, stall_s, tmask_s, bdmask_s):
        @pl.when(pl.program_id(1) == 0)
        def _():
            st_s[...] = jnp.zeros_like(st_s)
            _hg_masks(rows, nch, tmask_s, bdmask_s)

        bdmask = bdmask_s[...]
        tmask = tmask_s[...] > 0.5
        for hh in range(HG_HEADS):
            hc = slice(hh * HEAD_DIM, (hh + 1) * HEAD_DIM)
            hq = hg_ref[0, :, hc]
            hf = hg_ref[1, :, hc]
            hi = hg_ref[2, :, hc]
            hz = hg_ref[3, :, hc]
            lb, _ = _lower_bound(lbl_ref[:, hc])
            f = lb + (1.0 - lb) * jax.nn.sigmoid(hf)
            q = hq * jax.nn.sigmoid(hq)
            k = 1.0 - f
            logf = jnp.log(f)
            b = _chunk_cumsum(logf, rows)
            q_in = (q * jnp.exp(b)).astype(BF16)
            k_in = (k * jnp.exp(-b)).astype(BF16)
            k_out = (k * jnp.exp(_chunk_last(b, rows) - b)).astype(BF16)
            vb = hi.astype(BF16)

            sc = jnp.where(tmask, _dot_nt(q_in, k_in), 0.0)
            o_intra = _dot(sc.astype(BF16), vb)
            kvt = _dot_tn(vb, _block_diag(k_out, nch, bdmask))
            st = st_s[hh]
            st0_ref[hh] = st
            for c in range(nch):
                cols = slice(c * HEAD_DIM, (c + 1) * HEAD_DIM)
                last = (c + 1) * HG_CHUNK - 1
                stall_s[hh, :, cols] = st.astype(BF16)
                st = st * jnp.exp(b[last:last + 1, :]) + kvt[:, cols]
            st_s[hh] = st
            o = o_intra + _dot_nt(_block_diag(q_in, nch, bdmask), stall_s[hh])
            o_ref[:, hc] = o
            silu_z, _ = _silu_parts(hz)
            ya_ref[:, hc] = (_rms(o, g_ref[...]) * silu_z).astype(BF16)

    nh = HG_HEADS
    return pl.pallas_call(
        body, name="hgrn_fwd", grid=(HEADS // nh, nblk),
        in_specs=[pl.BlockSpec((4, rows, nh * HEAD_DIM), lambda h, i: (0, i, h)),
                  pl.BlockSpec((2, nh * HEAD_DIM), lambda h, i: (0, h)),
                  pl.BlockSpec((1, HEAD_DIM), lambda h, i: (0, 0))],
        out_specs=[pl.BlockSpec((rows, nh * HEAD_DIM), lambda h, i: (i, h)),
                   pl.BlockSpec((rows, nh * HEAD_DIM), lambda h, i: (i, h)),
                   pl.BlockSpec((nh, None, HEAD_DIM, HEAD_DIM), lambda h, i: (h, i, 0, 0))],
        out_shape=[jax.ShapeDtypeStruct((s, D_MODEL), F32), jax.ShapeDtypeStruct((s, D_MODEL), BF16),
                   jax.ShapeDtypeStruct((HEADS, nblk, HEAD_DIM, HEAD_DIM), F32)],
        scratch_shapes=[pltpu.VMEM((nh, HEAD_DIM, HEAD_DIM), F32), pltpu.VMEM((nh, HEAD_DIM, nch * HEAD_DIM), BF16),
                        pltpu.VMEM((rows, rows), F32), pltpu.VMEM((rows, nch * HEAD_DIM), BF16)],
        compiler_params=_params(2),
    )(hg, lb_logits, norm_g)


def _hgrn_bwd(hg, o_pre, dya, st0, lb_logits, norm_g):
    s = hg.shape[1]
    rows = min(HG_ROWS, s)
    nblk = s // rows
    nch = rows // HG_CHUNK

    def body(hg_ref, o_ref, dya_ref, st0_ref, lbl_ref, g_ref, dhg_ref, dlb_ref, dg_ref,
             dst_s, stp_s, stp_rows_s, dst_rows_s, dst_lane_s, dbl_s, tmask_s, bdmask_s):
        @pl.when(pl.program_id(1) == 0)
        def _():
            dst_s[...] = jnp.zeros_like(dst_s)
            dlb_ref[...] = jnp.zeros_like(dlb_ref)
            dg_ref[...] = jnp.zeros_like(dg_ref)
            _hg_masks(rows, nch, tmask_s, bdmask_s)

        bdmask = bdmask_s[...]
        tmask = tmask_s[...] > 0.5
        g = g_ref[...]
        for hh in range(HG_HEADS):
            hc = slice(hh * HEAD_DIM, (hh + 1) * HEAD_DIM)
            hq = hg_ref[0, :, hc]
            hf = hg_ref[1, :, hc]
            hi = hg_ref[2, :, hc]
            hz = hg_ref[3, :, hc]
            lb, _ = _lower_bound(lbl_ref[:, hc])
            sg = jax.nn.sigmoid(hf)
            f = lb + (1.0 - lb) * sg
            q, dsilu_q = _silu_parts(hq)
            k = 1.0 - f
            logf = jnp.log(f)
            b = _chunk_cumsum(logf, rows)
            eb = jnp.exp(b)
            enb = jnp.exp(-b)
            ebl = jnp.exp(_chunk_last(b, rows) - b)
            q_in32 = q * eb
            k_in32 = k * enb
            k_out32 = k * ebl
            q_in = q_in32.astype(BF16)
            k_in = k_in32.astype(BF16)
            k_out = k_out32.astype(BF16)
            vb = hi.astype(BF16)
            kbd = _block_diag(k_out, nch, bdmask)
            qbd = _block_diag(q_in, nch, bdmask)
            decs = [jnp.exp(b[(c + 1) * HG_CHUNK - 1:(c + 1) * HG_CHUNK, :]) for c in range(nch)]

            kvt = _dot_tn(vb, kbd)
            st = st0_ref[hh]
            for c in range(nch):
                stp_s[hh, c] = st
                stp_rows_s[hh, c * HEAD_DIM:(c + 1) * HEAD_DIM, :] = st.astype(BF16)
                st = st * decs[c] + kvt[:, c * HEAD_DIM:(c + 1) * HEAD_DIM]

            o = o_ref[:, hc]
            rstd = lax.rsqrt(jnp.mean(o * o, axis=-1, keepdims=True) + EPS)
            oh = o * rstd
            silu_z, dsilu_z = _silu_parts(hz)
            dya_v = dya_ref[:, hc]
            dn = dya_v * silu_z
            dhz = dya_v * (oh * g) * dsilu_z
            dg_ref[hh] += jnp.sum(dn * oh, axis=0, keepdims=True)
            doh = dn * g
            do = (rstd * (doh - oh * jnp.mean(doh * oh, axis=-1, keepdims=True))).astype(BF16)

            dq_all = _dot_tn(do, qbd)
            dst = dst_s[hh]
            ddecs = [None] * nch
            for c in reversed(range(nch)):
                dstb = dst.astype(BF16)
                dst_lane_s[hh, :, c * HEAD_DIM:(c + 1) * HEAD_DIM] = dstb
                dst_rows_s[hh, c * HEAD_DIM:(c + 1) * HEAD_DIM, :] = dstb
                ddecs[c] = jnp.sum(dst * stp_s[hh, c], axis=0, keepdims=True) * decs[c]
                dst = dst * decs[c] + dq_all[:, c * HEAD_DIM:(c + 1) * HEAD_DIM]
            dst_s[hh] = dst

            sc = jnp.where(tmask, _dot_nt(q_in, k_in), 0.0).astype(BF16)
            dkout = _dot(_block_diag(vb, nch, bdmask), dst_rows_s[hh])
            dv = _dot_nt(kbd, dst_lane_s[hh]) + _dot_tn(sc, do)
            dsc = jnp.where(tmask, _dot_nt(do, vb), 0.0).astype(BF16)
            dqin = _dot(dsc, k_in) + _dot(_block_diag(do, nch, bdmask), stp_rows_s[hh])
            dkin = _dot_tn(dsc, q_in)

            dko = dkout * k_out32
            for c in range(nch):
                sl = slice(c * HG_CHUNK, (c + 1) * HG_CHUNK)
                dbl = jnp.sum(dko[sl], axis=0, keepdims=True) + ddecs[c]
                dbl_s[hh, sl, :] = jnp.broadcast_to(dbl, (HG_CHUNK, HEAD_DIM))
            dq = dqin * eb
            dk = dkin * enb + dkout * ebl
            db = dqin * q_in32 - dkin * k_in32 - dko
            dlogf = _chunk_revcumsum(db, rows) + dbl_s[hh]
            df = dlogf / f - dk
            dlb_ref[:, hc] += jnp.sum(df * (1.0 - sg), axis=0, keepdims=True)
            dhg_ref[0, :, hc] = (dq * dsilu_q).astype(BF16)
            dhg_ref[1, :, hc] = (df * (1.0 - lb) * sg * (1.0 - sg)).astype(BF16)
            dhg_ref[2, :, hc] = dv.astype(BF16)
            dhg_ref[3, :, hc] = dhz.astype(BF16)

    last = nblk - 1
    nh = HG_HEADS
    wide = nh * HEAD_DIM
    return pl.pallas_call(
        body, name="hgrn_bwd", grid=(HEADS // nh, nblk),
        in_specs=[pl.BlockSpec((4, rows, wide), lambda h, i: (0, last - i, h)),
                  pl.BlockSpec((rows, wide), lambda h, i: (last - i, h)),
                  pl.BlockSpec((rows, wide), lambda h, i: (last - i, h)),
                  pl.BlockSpec((nh, None, HEAD_DIM, HEAD_DIM), lambda h, i: (h, last - i, 0, 0)),
                  pl.BlockSpec((2, wide), lambda h, i: (0, h)),
                  pl.BlockSpec((1, HEAD_DIM), lambda h, i: (0, 0))],
        out_specs=[pl.BlockSpec((4, rows, wide), lambda h, i: (0, last - i, h)),
                   pl.BlockSpec((1, wide), lambda h, i: (0, h)),
                   pl.BlockSpec((nh, 1, HEAD_DIM), lambda h, i: (h, 0, 0))],
        out_shape=[jax.ShapeDtypeStruct((4, s, D_MODEL), BF16), jax.ShapeDtypeStruct((1, D_MODEL), F32),
                   jax.ShapeDtypeStruct((HEADS, 1, HEAD_DIM), F32)],
        scratch_shapes=[pltpu.VMEM((nh, HEAD_DIM, HEAD_DIM), F32), pltpu.VMEM((nh, nch, HEAD_DIM, HEAD_DIM), F32),
                        pltpu.VMEM((nh, nch * HEAD_DIM, HEAD_DIM), BF16), pltpu.VMEM((nh, nch * HEAD_DIM, HEAD_DIM), BF16),
                        pltpu.VMEM((nh, HEAD_DIM, nch * HEAD_DIM), BF16), pltpu.VMEM((nh, rows, HEAD_DIM), F32),
                        pltpu.VMEM((rows, rows), F32), pltpu.VMEM((rows, nch * HEAD_DIM), BF16)],
        compiler_params=_params(2),
    )(hg, o_pre, dya, st0, lb_logits, norm_g)


def _mla_pre(ms, q_a_g, kv_a_g, wuq3, wukv3, tabs):
    s = ms.shape[0]
    tm = min(TM_MLA, s)

    def body(ms_ref, qg_ref, kvg_ref, wuq_ref, wukv_ref, c_ref, sa_ref, sb_ref,
             q_ref, k_ref, v_ref, cqn_ref, ckvn_ref):
        c, sa, sb = c_ref[...], sa_ref[...], sb_ref[...]
        cqn = _rms(ms_ref[:, 0:Q_LORA], qg_ref[...]).astype(BF16)
        ckvn = _rms(ms_ref[:, Q_LORA:Q_LORA + KV_LORA], kvg_ref[...]).astype(BF16)
        cqn_ref[...] = cqn
        ckvn_ref[...] = ckvn
        k_pe = _rope(ms_ref[:, Q_LORA + KV_LORA:MS_COLS], c, sa, sb).astype(BF16)
        for h in range(HEADS):
            qh = _dot(cqn, wuq_ref[h])
            q_ref[h, :, 0:128] = (qh[:, 0:128] * Q_PRESCALE).astype(BF16)
            q_ref[h, :, 128:256] = (_rope(qh[:, 128:256], c, sa, sb) * Q_PRESCALE).astype(BF16)
            kvh = _dot(ckvn, wukv_ref[h])
            k_ref[h, :, 0:128] = kvh[:, 0:128].astype(BF16)
            k_ref[h, :, 128:256] = k_pe
            v_ref[h] = kvh[:, 128:256].astype(BF16)

    tab = pl.BlockSpec((tm, 128), lambda i: (i, 0))
    return pl.pallas_call(
        body, name="mla_pre", grid=(s // tm,),
        in_specs=[pl.BlockSpec((tm, MS_COLS), lambda i: (i, 0)),
                  pl.BlockSpec((1, Q_LORA), lambda i: (0, 0)), pl.BlockSpec((1, KV_LORA), lambda i: (0, 0)),
                  pl.BlockSpec((HEADS, Q_LORA, QK_PAD), lambda i: (0, 0, 0)),
                  pl.BlockSpec((HEADS, KV_LORA, 256), lambda i: (0, 0, 0)), tab, tab, tab],
        out_specs=[pl.BlockSpec((HEADS, tm, QK_PAD), lambda i: (0, i, 0)),
                   pl.BlockSpec((HEADS, tm, QK_PAD), lambda i: (0, i, 0)),
                   pl.BlockSpec((HEADS, tm, HEAD_DIM), lambda i: (0, i, 0)),
                   pl.BlockSpec((tm, Q_LORA), lambda i: (i, 0)), pl.BlockSpec((tm, KV_LORA), lambda i: (i, 0))],
        out_shape=[jax.ShapeDtypeStruct((HEADS, s, QK_PAD), BF16), jax.ShapeDtypeStruct((HEADS, s, QK_PAD), BF16),
                   jax.ShapeDtypeStruct((HEADS, s, HEAD_DIM), BF16),
                   jax.ShapeDtypeStruct((s, Q_LORA), BF16), jax.ShapeDtypeStruct((s, KV_LORA), BF16)],
        compiler_params=_params(1),
    )(ms, q_a_g, kv_a_g, wuq3, wukv3, *tabs)


def _causal_mask(t):
    r = lax.broadcasted_iota(jnp.int32, (t, t), 0)
    c = lax.broadcasted_iota(jnp.int32, (t, t), 1)
    return r >= c


def _flash_fwd(q, k, v, mz):
    s = q.shape[1]
    t = min(TQ, s)

    def body(q_ref, k_ref, v_ref, mz_ref, o_ref, yb_ref, lse_ref, m_s, l_s, acc_s):
        i = pl.program_id(1)
        m_s[...] = jnp.full_like(m_s, -jnp.inf)
        l_s[...] = jnp.zeros_like(l_s)
        acc_s[...] = jnp.zeros_like(acc_s)

        def step(j, groups):
            rows = pl.ds(pl.multiple_of(j * t, t), t)
            for hh in range(FLASH_HEADS):
                for r0, nr, masked in groups:
                    r = slice(r0, r0 + nr)
                    sc = _dot_nt(q_ref[hh, r, :], k_ref[hh, rows, :])
                    if masked:
                        sc = jnp.where(_causal_mask(t), sc, -jnp.inf)
                    m_prev = m_s[hh, r, :]
                    m_new = jnp.maximum(m_prev, jnp.max(sc, axis=-1, keepdims=True))
                    p = jnp.exp2(sc - jnp.tile(m_new, (1, t // 128))).astype(BF16)
                    alpha = jnp.exp2(m_prev - m_new)
                    l_s[hh, r, :] = alpha * l_s[hh, r, :] + jnp.sum(p.astype(F32), axis=-1, keepdims=True)
                    acc_s[hh, r, :] = alpha * acc_s[hh, r, :] + _dot(p, v_ref[hh, rows, :])
                    m_s[hh, r, :] = m_new

        def loop_body(jj, carry):
            for u in range(4):
                step(4 * jj + u, ((0, 2 * t, False),))
            return carry

        lax.fori_loop(0, i // 2, loop_body, 0)

        @pl.when(i % 2 == 1)
        def _():
            step(2 * i - 2, ((0, 2 * t, False),))
            step(2 * i - 1, ((0, 2 * t, False),))

        step(2 * i, ((0, t, True), (t, t, False)))
        step(2 * i + 1, ((t, t, True),))
        for hh in range(FLASH_HEADS):
            cols = slice(hh * HEAD_DIM, (hh + 1) * HEAD_DIM)
            out = acc_s[hh] / l_s[hh]
            o_ref[:, cols] = out
            silu_z, _ = _silu_parts(mz_ref[:, cols])
            yb_ref[:, cols] = (out * silu_z).astype(BF16)
            lse_ref[hh] = m_s[hh] + jnp.log2(l_s[hh])

    nh = FLASH_HEADS
    t2 = 2 * t
    col = pl.BlockSpec((t2, nh * HEAD_DIM), lambda h, i: (i, h))
    return pl.pallas_call(
        body, name="flash_fwd", grid=(HEADS // nh, s // t2),
        in_specs=[pl.BlockSpec((nh, t2, QK_PAD), lambda h, i: (h, i, 0)),
                  pl.BlockSpec((nh, s, QK_PAD), lambda h, i: (h, 0, 0)),
                  pl.BlockSpec((nh, s, HEAD_DIM), lambda h, i: (h, 0, 0)), col],
        out_specs=[col, col, pl.BlockSpec((nh, t2, 128), lambda h, i: (h, i, 0))],
        out_shape=[jax.ShapeDtypeStruct((s, D_MODEL), F32), jax.ShapeDtypeStruct((s, D_MODEL), BF16),
                   jax.ShapeDtypeStruct((HEADS, s, 128), F32)],
        scratch_shapes=[pltpu.VMEM((nh, t2, 128), F32), pltpu.VMEM((nh, t2, 128), F32),
                        pltpu.VMEM((nh, t2, HEAD_DIM), F32)],
        compiler_params=_params(2),
    )(q, k, v, mz)


def _flash_bwd(q, k, v, dyb, mz, o_att, lse, tabs):
    s = q.shape[1]
    t = min(TQ, s)

    def body(q_ref, k_ref, v_ref, dyb_ref, mz_ref, o_ref, lse_ref, c_ref, sa_ref, sb_ref,
             dq_ref, dk_ref, dv_ref, dmz_ref, dq_s, delta_s, do_s):
        i = pl.program_id(1)

        @pl.when(i == 0)
        def _():
            dk_ref[...] = jnp.zeros_like(dk_ref)
            dv_ref[...] = jnp.zeros_like(dv_ref)

        silu_z, dsilu_z = _silu_parts(mz_ref[...])
        dyb_v = dyb_ref[...]
        out = o_ref[...]
        do32 = dyb_v * silu_z
        dmz_ref[...] = (dyb_v * out * dsilu_z).astype(BF16)
        delta_s[...] = jnp.broadcast_to(jnp.sum(do32 * out, axis=-1, keepdims=True), (2 * t, 128))
        do_s[...] = do32.astype(BF16)
        dq_s[...] = jnp.zeros_like(dq_s)

        def step(j, modes):
            rows = pl.ds(pl.multiple_of(j * t, t), t)
            kj = k_ref[rows, :]
            vj = v_ref[rows, :]
            dv_acc = None
            dk_acc = None
            for ch, masked in enumerate(modes):
                if masked is None:
                    continue
                r = slice(ch * t, (ch + 1) * t)
                qv = q_ref[r, :]
                do = do_s[r, :]
                sc = _dot_nt(qv, kj)
                if masked:
                    sc = jnp.where(_causal_mask(t), sc, -jnp.inf)
                p = jnp.exp2(sc - jnp.tile(lse_ref[r, :], (1, t // 128)))
                dp = _dot_nt(do, vj)
                ds = (p * (dp - jnp.tile(delta_s[r, :], (1, t // 128)))).astype(BF16)
                dv_c = _dot_tn(p.astype(BF16), do)
                dk_c = _dot_tn(ds, qv)
                dv_acc = dv_c if dv_acc is None else dv_acc + dv_c
                dk_acc = dk_c if dk_acc is None else dk_acc + dk_c
                dq_s[r, :] += _dot(ds, kj)
            dv_ref[rows, :] += dv_acc
            dk_ref[rows, :] += dk_acc

        def loop_body(jj, carry):
            for u in range(4):
                step(4 * jj + u, (False, False))
            return carry

        lax.fori_loop(0, i // 2, loop_body, 0)

        @pl.when(i % 2 == 1)
        def _():
            step(2 * i - 2, (False, False))
            step(2 * i - 1, (False, False))

        step(2 * i, (True, False))
        step(2 * i + 1, (None, True))
        dq = dq_s[...] * ATT_SCALE
        dq_ref[:, 0:128] = dq[:, 0:128].astype(BF16)
        dq_ref[:, 128:256] = _rope_bwd(dq[:, 128:256], c_ref[...], sa_ref[...], sb_ref[...]).astype(BF16)

    t2 = 2 * t
    col = pl.BlockSpec((t2, HEAD_DIM), lambda h, i: (i, h))
    tab = pl.BlockSpec((t2, 128), lambda h, i: (i, 0))
    return pl.pallas_call(
        body, name="flash_bwd", grid=(HEADS, s // t2),
        in_specs=[pl.BlockSpec((None, t2, QK_PAD), lambda h, i: (h, i, 0)),
                  pl.BlockSpec((None, s, QK_PAD), lambda h, i: (h, 0, 0)),
                  pl.BlockSpec((None, s, HEAD_DIM), lambda h, i: (h, 0, 0)),
                  col, col, col, pl.BlockSpec((None, t2, 128), lambda h, i: (h, i, 0)), tab, tab, tab],
        out_specs=[pl.BlockSpec((None, t2, QK_PAD), lambda h, i: (h, i, 0)),
                   pl.BlockSpec((None, s, QK_PAD), lambda h, i: (h, 0, 0)),
                   pl.BlockSpec((None, s, HEAD_DIM), lambda h, i: (h, 0, 0)), col],
        out_shape=[jax.ShapeDtypeStruct((HEADS, s, QK_PAD), BF16), jax.ShapeDtypeStruct((HEADS, s, QK_PAD), F32),
                   jax.ShapeDtypeStruct((HEADS, s, HEAD_DIM), F32), jax.ShapeDtypeStruct((s, D_MODEL), BF16)],
        scratch_shapes=[pltpu.VMEM((t2, QK_PAD), F32), pltpu.VMEM((t2, 128), F32), pltpu.VMEM((t2, HEAD_DIM), BF16)],
        compiler_params=_params(2),
    )(q, k, v, dyb, mz, o_att, lse, *tabs)


def _mla_bwd_proj(dq, dk, dv, cqn, ckvn, ms, q_a_g, kv_a_g, wuq3, wukv3, tabs):
    s = ms.shape[0]
    tm = min(TM_MLA, s)

    def body(dq_ref, dk_ref, dv_ref, cqn_ref, ckvn_ref, ms_ref, qg_ref, kvg_ref, wuq_ref, wukv_ref,
             c_ref, sa_ref, sb_ref, dms_ref, dwuq_ref, dwukv_ref, dqg_ref, dkvg_ref):
        @pl.when(pl.program_id(0) == 0)
        def _():
            dwuq_ref[...] = jnp.zeros_like(dwuq_ref)
            dwukv_ref[...] = jnp.zeros_like(dwukv_ref)
            dqg_ref[...] = jnp.zeros_like(dqg_ref)
            dkvg_ref[...] = jnp.zeros_like(dkvg_ref)

        cqn = cqn_ref[...]
        ckvn = ckvn_ref[...]
        dcqn = jnp.zeros((tm, Q_LORA), F32)
        dckvn = jnp.zeros((tm, KV_LORA), F32)
        dkpe = jnp.zeros((tm, 128), F32)
        for h in range(HEADS):
            dqh = dq_ref[h]
            dcqn += _dot_nt(dqh, wuq_ref[h])
            dwuq_ref[h] += _dot_tn(cqn, dqh)
            dkh = dk_ref[h] * LN2
            dkvh = jnp.concatenate([dkh[:, 0:128], dv_ref[h]], axis=1).astype(BF16)
            dckvn += _dot_nt(dkvh, wukv_ref[h])
            dwukv_ref[h] += _dot_tn(ckvn, dkvh)
            dkpe += dkh[:, 128:256]
        dcq, dqg_rows = _rms_bwd(ms_ref[:, 0:Q_LORA], qg_ref[...], dcqn)
        dckv, dkvg_rows = _rms_bwd(ms_ref[:, Q_LORA:Q_LORA + KV_LORA], kvg_ref[...], dckvn)
        dqg_ref[...] += jnp.sum(dqg_rows, axis=0, keepdims=True)
        dkvg_ref[...] += jnp.sum(dkvg_rows, axis=0, keepdims=True)
        dms_ref[:, 0:Q_LORA] = dcq.astype(BF16)
        dms_ref[:, Q_LORA:Q_LORA + KV_LORA] = dckv.astype(BF16)
        dms_ref[:, Q_LORA + KV_LORA:MS_COLS] = _rope_bwd(dkpe, c_ref[...], sa_ref[...], sb_ref[...]).astype(BF16)

    tab = pl.BlockSpec((tm, 128), lambda i: (i, 0))
    wq = pl.BlockSpec((HEADS, Q_LORA, QK_PAD), lambda i: (0, 0, 0))
    wkv = pl.BlockSpec((HEADS, KV_LORA, 256), lambda i: (0, 0, 0))
    qg = pl.BlockSpec((1, Q_LORA), lambda i: (0, 0))
    kvg = pl.BlockSpec((1, KV_LORA), lambda i: (0, 0))
    return pl.pallas_call(
        body, name="mla_bwd_proj", grid=(s // tm,),
        in_specs=[pl.BlockSpec((HEADS, tm, QK_PAD), lambda i: (0, i, 0)),
                  pl.BlockSpec((HEADS, tm, QK_PAD), lambda i: (0, i, 0)),
                  pl.BlockSpec((HEADS, tm, HEAD_DIM), lambda i: (0, i, 0)),
                  pl.BlockSpec((tm, Q_LORA), lambda i: (i, 0)), pl.BlockSpec((tm, KV_LORA), lambda i: (i, 0)),
                  pl.BlockSpec((tm, MS_COLS), lambda i: (i, 0)), qg, kvg, wq, wkv, tab, tab, tab],
        out_specs=[pl.BlockSpec((tm, MS_COLS), lambda i: (i, 0)), wq, wkv, qg, kvg],
        out_shape=[jax.ShapeDtypeStruct((s, MS_COLS), BF16), jax.ShapeDtypeStruct((HEADS, Q_LORA, QK_PAD), F32),
                   jax.ShapeDtypeStruct((HEADS, KV_LORA, 256), F32),
                   jax.ShapeDtypeStruct((1, Q_LORA), F32), jax.ShapeDtypeStruct((1, KV_LORA), F32)],
        compiler_params=_params(1),
    )(dq, dk, dv, cqn, ckvn, ms, q_a_g, kv_a_g, wuq3, wukv3, *tabs)


def _merge_fused(ya, yb, glog, b_gate, x, tgt, fg, wproj):
    s = x.shape[0]
    tm = min(TM_FUSED, s)

    def body(ya_ref, yb_ref, g0_ref, g1_ref, b0_ref, b1_ref, x_ref, t_ref, fg_ref, w_ref,
             mg_ref, dx2_ref, dx2b_ref, dya_ref, dyb_ref, dgl_ref, dpa_ref, dpb_ref, loss_ref, dfg_ref, dbg_ref):
        @pl.when(pl.program_id(0) == 0)
        def _():
            loss_ref[...] = jnp.zeros_like(loss_ref)
            dfg_ref[...] = jnp.zeros_like(dfg_ref)
            dbg_ref[...] = jnp.zeros_like(dbg_ref)

        pa = _dot(ya_ref[...], w_ref[0])
        pb = _dot(yb_ref[...], w_ref[1])
        g0 = jax.nn.sigmoid(g0_ref[...] + b0_ref[...])
        g1 = jax.nn.sigmoid(g1_ref[...] + b1_ref[...])
        merged = (g0 * pa + g1 * pb).astype(BF16)
        mg_ref[...] = merged
        x2 = x_ref[...] + _dot(merged, w_ref[2])
        fg_v = fg_ref[...]
        err = _rms(x2, fg_v) - t_ref[...]
        loss_ref[...] += 0.5 * jnp.sum(jnp.mean(err * err, axis=-1, keepdims=True), axis=0, keepdims=True)
        dx2, dfg_rows = _rms_bwd(x2, fg_v, err * (1.0 / D_MODEL))
        dx2_ref[...] = dx2
        dfg_ref[...] += jnp.sum(dfg_rows, axis=0, keepdims=True)

        dx2b = dx2.astype(BF16)
        dx2b_ref[...] = dx2b
        dmg = _dot_nt(dx2b, w_ref[2])
        dpa = (dmg * g0).astype(BF16)
        dpb = (dmg * g1).astype(BF16)
        dpa_ref[...] = dpa
        dpb_ref[...] = dpb
        dgl0 = dmg * pa * g0 * (1.0 - g0)
        dgl1 = dmg * pb * g1 * (1.0 - g1)
        dgl_ref[:, 0:D_MODEL] = dgl0.astype(BF16)
        dgl_ref[:, D_MODEL:2 * D_MODEL] = dgl1.astype(BF16)
        dbg_ref[:, 0:D_MODEL] += jnp.sum(dgl0, axis=0, keepdims=True)
        dbg_ref[:, D_MODEL:2 * D_MODEL] += jnp.sum(dgl1, axis=0, keepdims=True)
        dya_ref[...] = _dot_nt(dpa, w_ref[0])
        dyb_ref[...] = _dot_nt(dpb, w_ref[1])

    row = pl.BlockSpec((tm, D_MODEL), lambda i: (i, 0))
    row1 = pl.BlockSpec((tm, D_MODEL), lambda i: (i, 1))
    row2 = pl.BlockSpec((tm, 2 * D_MODEL), lambda i: (i, 0))
    vec = pl.BlockSpec((1, D_MODEL), lambda i: (0, 0))
    vec1 = pl.BlockSpec((1, D_MODEL), lambda i: (0, 1))
    vec2 = pl.BlockSpec((1, 2 * D_MODEL), lambda i: (0, 0))
    f32_rows = jax.ShapeDtypeStruct((s, D_MODEL), F32)
    bf16_rows = jax.ShapeDtypeStruct((s, D_MODEL), BF16)
    return pl.pallas_call(
        body, name="merge_fused", grid=(s // tm,),
        in_specs=[row, row, row, row1, vec, vec1, row, row, vec, pl.BlockSpec((3, D_MODEL, D_MODEL), lambda i: (0, 0, 0))],
        out_specs=[row, row, row, row, row, row2, row, row, pl.BlockSpec((1, 128), lambda i: (0, 0)), vec, vec2],
        out_shape=[bf16_rows, f32_rows, bf16_rows, f32_rows, f32_rows, jax.ShapeDtypeStruct((s, 2 * D_MODEL), BF16),
                   bf16_rows, bf16_rows, jax.ShapeDtypeStruct((1, 128), F32), jax.ShapeDtypeStruct((1, D_MODEL), F32),
                   jax.ShapeDtypeStruct((1, 2 * D_MODEL), F32)],
        compiler_params=_params(1),
    )(ya, yb, glog, glog, b_gate, b_gate, x, tgt, fg, wproj)


def _proj_fused(x, g, w_int, r_blk):
    s = x.shape[0]
    tm = min(TM_FUSED, s)

    def body(x_ref, g_ref, w_hbm, r_ref, h_ref, hg_ref, ms_ref, mz_ref, gl_ref, or_ref, w_s, sem, send_sems, recv_sems):
        mx, my, mc = _me()
        chips = _other_chips(mx, my)
        mine, theirs = _cols(mc), _cols(1 - mc)

        def copy(k, src, dst, to):
            return pltpu.make_async_remote_copy(src_ref=src, dst_ref=dst, send_sem=send_sems.at[k],
                                                recv_sem=recv_sems.at[k], device_id=to, device_id_type=MESH_ID)

        def sends():
            return [copy(j, r_ref.at[:, mine], or_ref.at[2 * mx + my, :, mine], (cx, cy, mc))
                    for j, (cx, cy) in enumerate(chips)]

        @pl.when(pl.program_id(0) == 0)
        def _():
            for cp in sends():
                cp.start()
            cp = pltpu.make_async_copy(w_hbm, w_s, sem)
            cp.start()
            cp.wait()

        h = _rms(x_ref[...], g_ref[...]).astype(BF16)
        h_ref[...] = h
        for j in range(4):
            hg_ref[j] = _dot_nt(h, w_s[j * D_MODEL:(j + 1) * D_MODEL, :])
        ms = _dot_nt(h, w_s[4096:4096 + MS_COLS, :])
        lane = lax.broadcasted_iota(jnp.int32, ms.shape, 1)
        ms_ref[...] = jnp.where(lane < 704, ms, 0.0)
        mz_ref[...] = _dot_nt(h, w_s[4800:5824, :])
        for j in range(2):
            gl_ref[:, j * D_MODEL:(j + 1) * D_MODEL] = _dot_nt(h, w_s[5824 + j * D_MODEL:5824 + (j + 1) * D_MODEL, :])

        def passes():
            return [copy(3 + j, or_ref.at[2 * cx + cy, :, mine], or_ref.at[2 * cx + cy, :, mine], (mx, my, 1 - mc))
                    for j, (cx, cy) in enumerate(chips)]

        @pl.when(pl.program_id(0) == (3 * (s // tm)) // 4)
        def _():
            for j, (cx, cy) in enumerate(chips):
                landed = or_ref.at[2 * cx + cy, :, mine]
                copy(j, landed, landed, (cx, cy, mc)).wait_recv()
            for cp in passes():
                cp.start()

        @pl.when(pl.program_id(0) == s // tm - 1)
        def _():
            for j, (cx, cy) in enumerate(chips):
                other = or_ref.at[2 * cx + cy, :, theirs]
                copy(3 + j, other, other, (mx, my, 1 - mc)).wait_recv()
            for cp in sends() + passes():
                cp.wait_send()

    row = pl.BlockSpec((tm, D_MODEL), lambda i: (i, 0))
    outs = pl.pallas_call(
        body, name="proj_fused", grid=(s // tm,),
        in_specs=[row, pl.BlockSpec((1, D_MODEL), lambda i: (0, 0)), ANY, ANY],
        out_specs=[row, pl.BlockSpec((4, tm, D_MODEL), lambda i: (0, i, 0)), pl.BlockSpec((tm, MS_COLS), lambda i: (i, 0)),
                   row, pl.BlockSpec((tm, 2 * D_MODEL), lambda i: (i, 0)), ANY],
        out_shape=[jax.ShapeDtypeStruct((s, D_MODEL), BF16), jax.ShapeDtypeStruct((4, s, D_MODEL), F32),
                   jax.ShapeDtypeStruct((s, MS_COLS), F32), jax.ShapeDtypeStruct((s, D_MODEL), F32),
                   jax.ShapeDtypeStruct((s, 2 * D_MODEL), F32), jax.ShapeDtypeStruct((N_CHIPS,) + r_blk.shape, r_blk.dtype)],
        scratch_shapes=[pltpu.VMEM(w_int.shape, BF16), pltpu.SemaphoreType.DMA,
                        pltpu.SemaphoreType.DMA((6,)), pltpu.SemaphoreType.DMA((6,))],
        compiler_params=_params(1),
    )(x, g, w_int, r_blk)
    gr = lax.dynamic_update_slice(outs[5], r_blk[None], (2 * lax.axis_index("x") + lax.axis_index("y"), 0, 0))
    return (*outs[:5], gr)


def _dh_fused(dhg, dms, dmz, dglog, w_int, x, g, dx2, hw, hr):
    s = x.shape[0]
    tm = min(512, s)
    nw, nr = hw.shape[0] // N_CHIPS, hr.shape[0] // N_CHIPS

    def body(dhg_ref, dms_ref, dmz_ref, dgl_ref, w_hbm, x_ref, g_ref, dx2_ref, hw_ref, hr_ref,
             dx_ref, dg_ref, lw_ref, lr_ref, w_s, sem, send_sems, recv_sems):
        def scatter_copies():
            mx, my, mc = _me()
            return [pltpu.make_async_remote_copy(
                src_ref=src.at[pl.ds((2 * cx + cy) * n, n), :], dst_ref=dst.at[j], send_sem=send_sems.at[3 * a + j],
                recv_sem=recv_sems.at[3 * a + j], device_id=(cx, cy, mc), device_id_type=MESH_ID)
                for a, (src, dst, n) in enumerate([(hw_ref, lw_ref, nw), (hr_ref, lr_ref, nr)])
                for j, (cx, cy) in enumerate(_other_chips(mx, my))]

        @pl.when(pl.program_id(0) == 0)
        def _():
            for cp in scatter_copies():
                cp.start()
            dg_ref[...] = jnp.zeros_like(dg_ref)
            cp = pltpu.make_async_copy(w_hbm, w_s, sem)
            cp.start()
            cp.wait()

        dh = _dot(dms_ref[...], w_s[4096:4096 + MS_COLS, :]) + _dot(dmz_ref[...], w_s[4800:5824, :])
        for j in range(4):
            dh += _dot(dhg_ref[j], w_s[j * D_MODEL:(j + 1) * D_MODEL, :])
        for j in range(2):
            dh += _dot(dgl_ref[:, j * D_MODEL:(j + 1) * D_MODEL], w_s[5824 + j * D_MODEL:5824 + (j + 1) * D_MODEL, :])
        dx, dg_rows = _rms_bwd(x_ref[...], g_ref[...], dh)
        dx_ref[...] = dx + dx2_ref[...]
        dg_ref[...] += jnp.sum(dg_rows, axis=0, keepdims=True)

        @pl.when(pl.program_id(0) == s // tm - 1)
        def _():
            for cp in scatter_copies():
                cp.wait()

    row = pl.BlockSpec((tm, D_MODEL), lambda i: (i, 0))
    vec = pl.BlockSpec((1, D_MODEL), lambda i: (0, 0))
    return pl.pallas_call(
        body, name="dh_fused", grid=(s // tm,),
        in_specs=[pl.BlockSpec((4, tm, D_MODEL), lambda i: (0, i, 0)), pl.BlockSpec((tm, MS_COLS), lambda i: (i, 0)), row,
                  pl.BlockSpec((tm, 2 * D_MODEL), lambda i: (i, 0)), ANY, row, vec, row, ANY, ANY],
        out_specs=[row, vec, ANY, ANY],
        out_shape=[jax.ShapeDtypeStruct((s, D_MODEL), F32), jax.ShapeDtypeStruct((1, D_MODEL), F32),
                   jax.ShapeDtypeStruct((3, nw, HALF_COLS), hw.dtype), jax.ShapeDtypeStruct((3, nr, HALF_COLS), hr.dtype)],
        scratch_shapes=[pltpu.VMEM(w_int.shape, BF16), pltpu.SemaphoreType.DMA,
                        pltpu.SemaphoreType.DMA((6,)), pltpu.SemaphoreType.DMA((6,))],
        compiler_params=_params(1),
    )(dhg, dms, dmz, dglog, w_int, x, g, dx2, hw, hr)


def _local_step(x, tgt, w_int, r_blk, norm_g, b_gate, lb_logits, hg_norm_g, q_a_g, kv_a_g, fg):
    s = x.shape[0]
    tabs = _rope_tables(s)

    h, hg, ms, mz, glog, gr = _proj_fused(x, norm_g, w_int, r_blk)
    w_uq, w_ukv, wproj = _unpack_rest_weights(gr)
    wuq3 = jnp.pad(w_uq.reshape(Q_LORA, HEADS, QK_DIM).transpose(1, 0, 2), ((0, 0), (0, 0), (0, QK_PAD - QK_DIM)))
    wukv3 = w_ukv.reshape(KV_LORA, HEADS, 256).transpose(1, 0, 2)
    o_pre, ya, st0 = _hgrn_fwd(hg, lb_logits, hg_norm_g)
    q, k, v, cqn, ckvn = _mla_pre(ms, q_a_g, kv_a_g, wuq3, wukv3, tabs)
    o_att, yb, lse = _flash_fwd(q, k, v, mz)
    merged, dx2, dx2b, dya, dyb, dglog, dpa, dpb, loss, dfg, dbg = _merge_fused(ya, yb, glog, b_gate, x, tgt, fg, wproj)

    d_wout = _mm_tn(merged, dx2b, name="dw_out")
    d_wpa = _mm_tn(ya, dpa, name="dw_proj_a")
    d_wpb = _mm_tn(yb, dpb, name="dw_proj_b")
    dhg, dlb, dhgg = _hgrn_bwd(hg, o_pre, dya, st0, lb_logits, hg_norm_g)
    dq, dk, dv, dmz = _flash_bwd(q, k, v, dyb, mz, o_att, lse, tabs)
    dms, d_wuq3, d_wukv3, dqg, dkvg = _mla_bwd_proj(dq, dk, dv, cqn, ckvn, ms, q_a_g, kv_a_g, wuq3, wukv3, tabs)
    d_hg = _mm_tn(dhg, h, name="dw_in_hg")
    d_ms = _mm_tn(dms, h, name="dw_in_ms")
    d_mz = _mm_tn(dmz, h, name="dw_in_mz")
    d_gl = _mm_tn(dglog, h, name="dw_in_gate")
    d_w_int = jnp.concatenate([d_hg.reshape(4 * D_MODEL, D_MODEL), d_ms[0:704], d_mz, d_gl], axis=0)
    small = {"b_gate": dbg, "lb": dlb, "hg_norm_g": dhgg, "q_a_g": dqg, "kv_a_g": dkvg, "final_norm_g": dfg}
    dh_args = (dhg, dms, dmz, dglog, w_int, x, norm_g, dx2)
    return loss, dh_args, d_w_int, d_wuq3, d_wukv3, (d_wpa, d_wpb, d_wout), small


def _pack_rest(w_uq_b, w_ukv_b, wpa_b, wpb_b, wout_b):
    return jnp.concatenate([w_uq_b.reshape(144, D_MODEL), w_ukv_b.reshape(128, D_MODEL), wpa_b, wpb_b, wout_b], axis=0)


def _unpack_rest(p):
    return (p[0:144].reshape(Q_LORA, 384), p[144:272].reshape(KV_LORA, 512), p[272:528], p[528:784], p[784:1040])


def _pack_rest_grads(d_wuq3, d_wukv3, d_proj):
    d_wuq = d_wuq3.transpose(1, 0, 2)[:, :, 0:QK_DIM].reshape(Q_LORA, HEADS * QK_DIM)
    d_wukv = d_wukv3.transpose(1, 0, 2).reshape(KV_LORA, HEADS * 256)
    blocks = []
    for b in range(N_CHIPS):
        rows = slice(b * 256, (b + 1) * 256)
        blocks.append(_pack_rest(d_wuq[:, b * 384:(b + 1) * 384], d_wukv[:, b * 512:(b + 1) * 512],
                                 d_proj[0][rows], d_proj[1][rows], d_proj[2][rows]))
    return jnp.stack(blocks, axis=0)


def _unpack_rest_weights(g):
    parts = [_unpack_rest(g[b]) for b in range(N_CHIPS)]
    w_uq, w_ukv = (jnp.concatenate([p[n] for p in parts], axis=1) for n in range(2))
    wproj = jnp.stack([jnp.concatenate([p[n] for p in parts], axis=0) for n in range(2, 5)], axis=0)
    return w_uq, w_ukv, wproj


MESH_ID = pl.DeviceIdType.MESH
ANY = pl.BlockSpec(memory_space=pl.ANY)
HALF_COLS = D_MODEL // 2


def _me():
    return lax.axis_index("x"), lax.axis_index("y"), lax.axis_index("c")


def _other_chips(x, y):
    return [(1 - x, y), (x, 1 - y), (1 - x, 1 - y)]


def _cols(c):
    return pl.ds(c * HALF_COLS, HALF_COLS)


RELAY_TOP = 992


def _gather_weights(w_blk):
    bot = W_IN_BLK - RELAY_TOP

    def body(w_ref, ow_ref, send_sems, recv_sems):
        x, y, c = _me()
        me, xn, yn, dg = 2 * x + y, 2 * (1 - x) + y, 2 * x + (1 - y), 2 * (1 - x) + (1 - y)
        to_x, to_y, to_sib = (1 - x, y, c), (x, 1 - y, c), (x, y, 1 - c)
        mine, theirs = _cols(c), _cols(1 - c)
        top, low = pl.ds(0, RELAY_TOP), pl.ds(RELAY_TOP, bot)

        def copy(k, src, dst, to):
            return pltpu.make_async_remote_copy(src_ref=src, dst_ref=dst, send_sem=send_sems.at[k],
                                                recv_sem=recv_sems.at[k], device_id=to, device_id_type=MESH_ID)

        def same(k, ref, to):
            return copy(k, ref, ref, to)

        own = [copy(0, w_ref.at[:, mine], ow_ref.at[me, :, mine], to_x),
               copy(1, w_ref.at[:, mine], ow_ref.at[me, :, mine], to_y)]
        for cp in own:
            cp.start()
        from_x, from_y = ow_ref.at[xn, :, mine], ow_ref.at[yn, :, mine]
        same(0, from_x, to_x).wait_recv()
        relay_y = same(2, ow_ref.at[xn, top, mine], to_y)
        pass_x = same(4, from_x, to_sib)
        relay_y.start()
        pass_x.start()
        same(1, from_y, to_y).wait_recv()
        relay_x = same(3, ow_ref.at[yn, low, mine], to_x)
        pass_y = same(5, from_y, to_sib)
        relay_x.start()
        pass_y.start()
        same(2, ow_ref.at[dg, top, mine], to_y).wait_recv()
        same(3, ow_ref.at[dg, low, mine], to_x).wait_recv()
        pass_d = same(6, ow_ref.at[dg, :, mine], to_sib)
        pass_d.start()
        for k, blk in ((4, xn), (5, yn), (6, dg)):
            same(k, ow_ref.at[blk, :, theirs], to_sib).wait_recv()
        for cp in own + [relay_y, relay_x, pass_x, pass_y, pass_d]:
            cp.wait_send()

    gw = pl.pallas_call(
        body, name="gather_weights", in_specs=[ANY], out_specs=ANY,
        out_shape=jax.ShapeDtypeStruct((N_CHIPS,) + w_blk.shape, w_blk.dtype),
        scratch_shapes=[pltpu.SemaphoreType.DMA((7,)), pltpu.SemaphoreType.DMA((7,))],
    )(w_blk)
    return lax.dynamic_update_slice(gw, w_blk[None], (2 * lax.axis_index("x") + lax.axis_index("y"), 0, 0))


def _swap_halves(gw, gr):
    def body(gw_ref, gr_ref, lw_ref, lr_ref, send_sems, recv_sems):
        x, y, c = _me()
        cps = [pltpu.make_async_remote_copy(
            src_ref=src, dst_ref=dst, send_sem=send_sems.at[a], recv_sem=recv_sems.at[a],
            device_id=(x, y, 1 - c), device_id_type=MESH_ID)
            for a, (src, dst) in enumerate([(gw_ref.at[:, _cols(1 - c)], lw_ref),
                                            (gr_ref.at[:, :, _cols(1 - c)], lr_ref)])]
        for cp in cps:
            cp.start()
        for cp in cps:
            cp.wait()

    return pl.pallas_call(
        body, name="grad_swap_halves", in_specs=[ANY, ANY], out_specs=[ANY, ANY],
        out_shape=[jax.ShapeDtypeStruct((gw.shape[0], HALF_COLS), gw.dtype),
                   jax.ShapeDtypeStruct(gr.shape[:2] + (HALF_COLS,), gr.dtype)],
        scratch_shapes=[pltpu.SemaphoreType.DMA((2,)), pltpu.SemaphoreType.DMA((2,))],
    )(gw, gr)


def _swap_reduced(rw, rr):
    def body(rw_ref, rr_ref, ow_ref, or_ref, send_sems, recv_sems):
        x, y, c = _me()
        cps = [pltpu.make_async_remote_copy(
            src_ref=src, dst_ref=dst, send_sem=send_sems.at[a], recv_sem=recv_sems.at[a],
            device_id=(x, y, 1 - c), device_id_type=MESH_ID)
            for a, (src, dst) in enumerate([(rw_ref, ow_ref), (rr_ref, or_ref)])]
        for cp in cps:
            cp.start()
        for cp in cps:
            cp.wait()

    return pl.pallas_call(
        body, name="grad_swap_reduced", in_specs=[ANY, ANY], out_specs=[ANY, ANY],
        out_shape=[jax.ShapeDtypeStruct(rw.shape, rw.dtype), jax.ShapeDtypeStruct(rr.shape, rr.dtype)],
        scratch_shapes=[pltpu.SemaphoreType.DMA((2,)), pltpu.SemaphoreType.DMA((2,))],
    )(rw, rr)


def _join_cols(mine, theirs):
    first = lax.axis_index("c") == 0
    return jnp.concatenate([jnp.where(first, mine, theirs), jnp.where(first, theirs, mine)], axis=1)


def _gather_small(vec):
    def body(v_ref, out_ref, send_sems, recv_sems, local_sem):
        x, y, c = _me()
        my_id = 4 * x + 2 * y + c
        mine = pltpu.make_async_copy(v_ref, out_ref.at[my_id], local_sem)
        mine.start()
        cps = []
        for r in range(1, N_DEV):
            peer = (x ^ (r >> 2), y ^ ((r >> 1) & 1), c ^ (r & 1))
            cps.append(pltpu.make_async_remote_copy(
                src_ref=v_ref, dst_ref=out_ref.at[my_id], send_sem=send_sems.at[r - 1],
                recv_sem=recv_sems.at[r - 1], device_id=peer, device_id_type=MESH_ID))
        for cp in cps:
            cp.start()
        for cp in cps:
            cp.wait()
        mine.wait()

    return pl.pallas_call(
        body, name="gather_small", in_specs=[ANY], out_specs=ANY,
        out_shape=jax.ShapeDtypeStruct((N_DEV, 1, SMALL_COLS), vec.dtype),
        scratch_shapes=[pltpu.SemaphoreType.DMA((N_DEV - 1,)), pltpu.SemaphoreType.DMA((N_DEV - 1,)),
                        pltpu.SemaphoreType.DMA],
    )(vec)


def _add_cores(c_idx, g, landed, *, tm, name):
    r = g.shape[0]

    def body(c_ref, g_ref, l_ref, o32_ref, o16_ref):
        acc = g_ref[...] + l_ref[...]
        o32_ref[...] = acc
        o16_ref[...] = acc.astype(BF16)

    half = pl.BlockSpec((tm, HALF_COLS), lambda i, c_ref: (i, 0))
    grid_spec = pltpu.PrefetchScalarGridSpec(
        num_scalar_prefetch=1, grid=(r // tm,),
        in_specs=[pl.BlockSpec((tm, HALF_COLS), lambda i, c_ref: (i, c_ref[0])), half], out_specs=[half, half])
    return pl.pallas_call(
        body, name=name, grid_spec=grid_spec,
        out_shape=[jax.ShapeDtypeStruct((r, HALF_COLS), F32), jax.ShapeDtypeStruct((r, HALF_COLS), BF16)],
        compiler_params=_params(1),
    )(c_idx, g, landed)


def _add_chips(chip_idx, h32, landed, *, tm, name):
    n = landed.shape[1]
    per = n // tm

    def body(chip_ref, h_ref, l_ref, o_ref):
        acc = h_ref[...]
        for j in range(3):
            acc = acc + l_ref[j].astype(F32)
        o_ref[...] = acc

    grid_spec = pltpu.PrefetchScalarGridSpec(
        num_scalar_prefetch=1, grid=(per,),
        in_specs=[pl.BlockSpec((tm, HALF_COLS), lambda i, chip_ref: (chip_ref[0] * per + i, 0)),
                  pl.BlockSpec((3, tm, HALF_COLS), lambda i, chip_ref: (0, i, 0))],
        out_specs=pl.BlockSpec((tm, HALF_COLS), lambda i, chip_ref: (i, 0)))
    return pl.pallas_call(
        body, name=name, grid_spec=grid_spec, out_shape=jax.ShapeDtypeStruct((n, HALF_COLS), F32),
        compiler_params=_params(1),
    )(chip_idx, h32, landed)


def _pack_small(small, lb_logits, loss):
    def body(ng_ref, bg_ref, dlb_ref, lbl_ref, hgg_ref, qg_ref, kvg_ref, fg_ref, loss_ref, out_ref):
        out_ref[...] = jnp.zeros_like(out_ref)
        out_ref[:, 0:1024] = ng_ref[...]
        out_ref[:, 1024:3072] = bg_ref[...]
        _, p0p1 = _lower_bound(lbl_ref[...])
        dl0 = dlb_ref[...] * p0p1
        out_ref[:, 3072:4096] = dl0
        out_ref[:, 4096:5120] = -dl0
        hgg = hgg_ref[0]
        for h in range(1, HEADS):
            hgg = hgg + hgg_ref[h]
        out_ref[:, 5120:5248] = hgg
        out_ref[:, 5248:5632] = qg_ref[...]
        out_ref[:, 5632:5888] = kvg_ref[...]
        out_ref[:, 5888:6912] = fg_ref[...]
        out_ref[:, 6912:7040] = loss_ref[...]

    return pl.pallas_call(
        body, name="pack_small", out_shape=jax.ShapeDtypeStruct((1, SMALL_COLS), F32),
    )(small["norm_g"], small["b_gate"], small["lb"], lb_logits, small["hg_norm_g"], small["q_a_g"],
      small["kv_a_g"], small["final_norm_g"], loss)


def _adamw_math(w, g, m, v):
    nm = ADAM_B1 * m + (1.0 - ADAM_B1) * g
    nv = ADAM_B2 * v + (1.0 - ADAM_B2) * (g * g)
    m_hat = nm / (1.0 - ADAM_B1 ** ADAM_STEP)
    v_hat = nv / (1.0 - ADAM_B2 ** ADAM_STEP)
    return -ADAM_LR * (m_hat / (jnp.sqrt(v_hat) + ADAM_EPS) + ADAM_WD * w), nm, nv


def _adamw(w, g, m, v, *, name, tm):
    r, cols = w.shape

    def body(w_ref, g_ref, m_ref, v_ref, d_ref, nm_ref, nv_ref):
        d_ref[...], nm_ref[...], nv_ref[...] = _adamw_math(w_ref[...], g_ref[...], m_ref[...], v_ref[...])

    row = pl.BlockSpec((tm, cols), lambda i: (i, 0))
    shp = jax.ShapeDtypeStruct((r, cols), F32)
    return pl.pallas_call(
        body, name=name, grid=(r // tm,), in_specs=[row] * 4, out_specs=[row] * 3, out_shape=[shp] * 3,
        compiler_params=_params(1),
    )(w, g, m, v)


SMALL_SLOTS = (("norm_g", (0,)), ("b_gate", (1024,)), ("lb_logits", (3072, 4096)), ("hg_norm_g", (5120,)),
               ("q_a_g", (5248,)), ("kv_a_g", (5632,)), ("final_norm_g", (5888,)))
LOSS_SLOT = 6912


def _small_update(gathered, ws, ms, vs):
    n = len(SMALL_SLOTS)

    def body(*refs):
        g_ref = refs[0]
        w_refs, m_refs, v_refs = refs[1:1 + n], refs[1 + n:1 + 2 * n], refs[1 + 2 * n:1 + 3 * n]
        outs = refs[1 + 3 * n:]
        loss_ref = outs[0]
        g_out, d_out, nm_out, nv_out = (outs[1 + k * n:1 + (k + 1) * n] for k in range(4))
        total = g_ref[0]
        for dev in range(1, N_DEV):
            total = total + g_ref[dev]
        loss_ref[...] = total[:, LOSS_SLOT:LOSS_SLOT + 128]
        for p, (_, offsets) in enumerate(SMALL_SLOTS):
            cols = w_refs[p].shape[1]
            for r, off in enumerate(offsets):
                rows = slice(r, r + 1)
                g = total[:, off:off + cols]
                g_out[p][rows, :] = g
                d_out[p][rows, :], nm_out[p][rows, :], nv_out[p][rows, :] = _adamw_math(
                    w_refs[p][rows, :], g, m_refs[p][rows, :], v_refs[p][rows, :])

    shapes = [jax.ShapeDtypeStruct(w.shape, F32) for w in ws]
    res = pl.pallas_call(
        body, name="small_update", out_shape=[jax.ShapeDtypeStruct((1, 128), F32)] + shapes * 4,
    )(gathered, *ws, *ms, *vs)
    return res[0], res[1:1 + n], res[1 + n:1 + 2 * n], res[1 + 2 * n:1 + 3 * n], res[1 + 3 * n:1 + 4 * n]


def kernel(x, norm_g, w_in, b_gate, lb_logits, hg_norm_g, q_a_g, w_uq, kv_a_g, w_ukv, w_proj_a, w_proj_b, w_out, final_norm_g, loss_target, m_norm_g, m_w_in, m_b_gate, m_lb_logits, m_hg_norm_g, m_q_a_g, m_w_uq, m_kv_a_g, m_w_ukv, m_w_proj_a, m_w_proj_b, m_w_out, m_final_norm_g, v_norm_g, v_w_in, v_b_gate, v_lb_logits, v_hg_norm_g, v_q_a_g, v_w_uq, v_kv_a_g, v_w_ukv, v_w_proj_a, v_w_proj_b, v_w_out, v_final_norm_g):
    c_idx = lax.axis_index("c").astype(jnp.int32).reshape(1)
    chip_idx = (2 * lax.axis_index("x") + lax.axis_index("y")).astype(jnp.int32).reshape(1)

    w_blk = w_in[0].T.astype(BF16)
    r_blk = _pack_rest(w_uq[0], w_ukv[0], w_proj_a[0], w_proj_b[0], w_out[0]).astype(BF16)
    gw = _gather_weights(w_blk)

    loss, dh_args, d_w_int, d_wuq3, d_wukv3, d_proj, small = _local_step(
        x[0], loss_target[0], gw.reshape(W_IN_COLS, D_MODEL), r_blk,
        norm_g, b_gate, lb_logits, hg_norm_g, q_a_g, kv_a_g, final_norm_g.reshape(1, D_MODEL))

    d_rest = _pack_rest_grads(d_wuq3, d_wukv3, d_proj)
    lw, lr = _swap_halves(d_w_int, d_rest)
    hw32, hw16 = _add_cores(c_idx, d_w_int, lw, tm=656, name="grad_add_cores_w")
    hr32, hr16 = _add_cores(c_idx, d_rest.reshape(N_CHIPS * REST_ROWS, D_MODEL), lr.reshape(N_CHIPS * REST_ROWS, HALF_COLS),
                            tm=REST_ROWS, name="grad_add_cores_r")
    grad_x, small["norm_g"], landed_w, landed_r = _dh_fused(*dh_args, hw16, hr16)
    rw = _add_chips(chip_idx, hw32, landed_w, tm=656, name="grad_add_chips_w")
    rr = _add_chips(chip_idx, hr32, landed_r, tm=208, name="grad_add_chips_r")
    tw, tr = _swap_reduced(rw, rr)
    g_w_in = _join_cols(rw, tw).T
    g_rest = _join_cols(rr, tr)
    g_uq, g_ukv, g_pa, g_pb, g_out = _unpack_rest(g_rest)

    small_all = _gather_small(_pack_small(small, lb_logits, loss))

    upd = {
        "w_in": _adamw(w_in[0], g_w_in, m_w_in[0], v_w_in[0], name="adamw_w_in", tm=128),
        "w_uq": _adamw(w_uq[0], g_uq, m_w_uq[0], v_w_uq[0], name="adamw_w_uq", tm=Q_LORA),
        "w_ukv": _adamw(w_ukv[0], g_ukv, m_w_ukv[0], v_w_ukv[0], name="adamw_w_ukv", tm=KV_LORA),
        "w_proj_a": _adamw(w_proj_a[0], g_pa, m_w_proj_a[0], v_w_proj_a[0], name="adamw_w_proj_a", tm=256),
        "w_proj_b": _adamw(w_proj_b[0], g_pb, m_w_proj_b[0], v_w_proj_b[0], name="adamw_w_proj_b", tm=256),
        "w_out": _adamw(w_out[0], g_out, m_w_out[0], v_w_out[0], name="adamw_w_out", tm=256),
    }
    loss_vec, *small_sets = _small_update(
        small_all,
        [norm_g, b_gate, lb_logits, hg_norm_g, q_a_g, kv_a_g, final_norm_g.reshape(1, D_MODEL)],
        [m_norm_g, m_b_gate, m_lb_logits, m_hg_norm_g, m_q_a_g, m_kv_a_g, m_final_norm_g.reshape(1, D_MODEL)],
        [v_norm_g, v_b_gate, v_lb_logits, v_hg_norm_g, v_q_a_g, v_kv_a_g, v_final_norm_g.reshape(1, D_MODEL)])

    def outputs(big, small_set):
        s_ng, s_bg, s_lb, s_hg, s_qg, s_kvg, s_fg = small_set
        return (s_ng, big["w_in"][None], s_bg, s_lb, s_hg, s_qg, big["w_uq"][None], s_kvg, big["w_ukv"][None],
                big["w_proj_a"][None], big["w_proj_b"][None], big["w_out"][None], s_fg.reshape(D_MODEL))

    grads = {"w_in": g_w_in, "w_uq": g_uq, "w_ukv": g_ukv, "w_proj_a": g_pa, "w_proj_b": g_pb, "w_out": g_out}
    return (loss_vec[0, 0], grad_x[None], *outputs(grads, small_sets[0]),
            *(o for k in range(3) for o in outputs({n: u[k] for n, u in upd.items()}, small_sets[1 + k])))
```

```python
import jax
import jax.numpy as jnp
from jax import lax
from jax.experimental import pallas as pl
from jax.experimental.pallas import tpu as pltpu

F32 = jnp.float32
BF16 = jnp.bfloat16

D_MODEL = 1024
HEADS = 8
HEAD_DIM = 128
HG_CHUNK = 32
CHUNK_SHIFT = 5
HEAD_SHIFT = 7
QK_NOPE = 128
QK_ROPE = 64
QK_DIM = QK_NOPE + QK_ROPE
QK_PAD = 256
Q_LORA = 384
KV_LORA = 256
MS_COLS = 768
ROPE_THETA = 10000.0
EPS = 1e-6
ATT_SCALE = QK_DIM ** -0.5
LOG2E = 1.4426950408889634
LN2 = 0.6931471805599453
Q_PRESCALE = ATT_SCALE * LOG2E

ADAM_LR = 0.001
ADAM_B1 = 0.9
ADAM_B2 = 0.999
ADAM_EPS = 1e-08
ADAM_WD = 0.01
ADAM_STEP = 10

N_CHIPS = 4
N_DEV = 8
W_IN_COLS = 7872
W_IN_BLK = W_IN_COLS // N_CHIPS
REST_ROWS = 144 + 128 + 3 * 256
SMALL_COLS = 7168

TM_MM = 2048
TM_FUSED = 256
TM_MLA = 512
HG_ROWS = 128
TQ = 512
FLASH_HEADS = 2
HG_HEADS = 8
VMEM_LIMIT = 56 * 1024 * 1024


def _dot(a, b):
    return lax.dot_general(a, b, (((1,), (0,)), ((), ())), preferred_element_type=F32)


def _dot_nt(a, b):
    return lax.dot_general(a, b, (((1,), (1,)), ((), ())), preferred_element_type=F32)


def _dot_tn(a, b):
    return lax.dot_general(a, b, (((0,), (0,)), ((), ())), preferred_element_type=F32)


def _params(n_axes):
    return pltpu.CompilerParams(dimension_semantics=("arbitrary",) * n_axes, vmem_limit_bytes=VMEM_LIMIT)


def _rms(x, g):
    r = lax.rsqrt(jnp.mean(x * x, axis=-1, keepdims=True) + EPS)
    return x * r * g


def _rms_bwd(x, g, dy):
    r = lax.rsqrt(jnp.mean(x * x, axis=-1, keepdims=True) + EPS)
    xh = x * r
    dyg = dy * g
    dx = r * (dyg - xh * jnp.mean(dyg * xh, axis=-1, keepdims=True))
    return dx, dy * xh


def _silu_parts(z):
    s = jax.nn.sigmoid(z)
    return z * s, s * (1.0 + z * (1.0 - s))


def _rope(x, c, sa, sb):
    return x * c + pltpu.roll(x, 32, 1) * sa + pltpu.roll(x, 96, 1) * sb


def _rope_bwd(dy, c, sa, sb):
    return dy * c + pltpu.roll(dy * sa, 96, 1) + pltpu.roll(dy * sb, 32, 1)


def _rope_tables(seq):
    inv = ROPE_THETA ** (-jnp.arange(0, QK_ROPE, 2, dtype=F32) / QK_ROPE)
    ang = jnp.arange(seq, dtype=F32)[:, None] * inv[None, :]
    cos, sin = jnp.cos(ang), jnp.sin(ang)
    z32 = jnp.zeros_like(cos)
    z64 = jnp.zeros((seq, 64), F32)
    c = jnp.concatenate([cos, cos, z64], axis=1)
    sa = jnp.concatenate([z32, sin, z64], axis=1)
    sb = jnp.concatenate([-sin, z32, z64], axis=1)
    return c, sa, sb


def _mm_tn(a, b, *, name, tm=TM_MM, tn=1024, swap=None):
    flat = a.ndim == 2
    if flat:
        a = a[None]
    g, m, k = a.shape
    n = b.shape[1]
    tm, tn = min(tm, m), min(tn, n)
    assert m % tm == 0 and n % tn == 0
    grid = (g, n // tn, m // tm)

    def body(a_ref, b_ref, *rest):
        o_ref = rest[1] if swap is not None else rest[0]

        @pl.when(pl.program_id(2) == 0)
        def _():
            o_ref[...] = jnp.zeros_like(o_ref)

        o_ref[...] += _dot_tn(a_ref[...], b_ref[...])

        if swap is not None:
            src_ref, _, land_ref, send_sem, recv_sem = rest
            mx, my, mc = _me()
            step = (pl.program_id(0) * grid[1] + pl.program_id(1)) * grid[2] + pl.program_id(2)
            cp = pltpu.make_async_remote_copy(
                src_ref=src_ref.at[:, :, _cols(1 - mc)], dst_ref=land_ref, send_sem=send_sem, recv_sem=recv_sem,
                device_id=(mx, my, 1 - mc), device_id_type=MESH_ID)

            @pl.when(step == 0)
            def _():
                cp.start()

            @pl.when(step == grid[0] * grid[1] * grid[2] - 1)
            def _():
                cp.wait()

    in_specs = [pl.BlockSpec((None, tm, k), lambda s, j, i: (s, i, 0)), pl.BlockSpec((tm, tn), lambda s, j, i: (i, j))]
    out_specs = pl.BlockSpec((None, k, tn), lambda s, j, i: (s, 0, j))
    out_shape = jax.ShapeDtypeStruct((g, k, n), F32)
    if swap is None:
        out = pl.pallas_call(body, name=name, grid=grid, in_specs=in_specs, out_specs=out_specs, out_shape=out_shape,
                             compiler_params=_params(3))(a, b)
        return out[0] if flat else out
    out, landed = pl.pallas_call(
        body, name=name, grid=grid, in_specs=in_specs + [ANY], out_specs=[out_specs, ANY],
        out_shape=[out_shape, jax.ShapeDtypeStruct(swap.shape[:2] + (HALF_COLS,), swap.dtype)],
        scratch_shapes=[pltpu.SemaphoreType.DMA, pltpu.SemaphoreType.DMA], compiler_params=_params(3),
    )(a, b, swap)
    return (out[0] if flat else out), landed


def _chunk_rows(rows):
    return lax.broadcasted_iota(jnp.int32, (rows, HEAD_DIM), 0) & (HG_CHUNK - 1)


def _chunk_cumsum(x, rows):
    pos = _chunk_rows(rows)
    shift = 1
    while shift < HG_CHUNK:
        x = x + jnp.where(pos >= shift, pltpu.roll(x, shift, 0), 0.0)
        shift *= 2
    return x


def _chunk_revcumsum(x, rows):
    pos = _chunk_rows(rows)
    shift = 1
    while shift < HG_CHUNK:
        x = x + jnp.where(pos + shift < HG_CHUNK, pltpu.roll(x, rows - shift, 0), 0.0)
        shift *= 2
    return x


def _chunk_last(x, rows):
    x3 = x.reshape(rows // HG_CHUNK, HG_CHUNK, HEAD_DIM)
    return jnp.broadcast_to(x3[:, HG_CHUNK - 1:HG_CHUNK, :], x3.shape).reshape(rows, HEAD_DIM)


def _lower_bound(lbl):
    mx = jnp.maximum(lbl[0:1, :], lbl[1:2, :])
    e0 = jnp.exp(lbl[0:1, :] - mx)
    e1 = jnp.exp(lbl[1:2, :] - mx)
    p0 = e0 / (e0 + e1)
    return p0, p0 * (e1 / (e0 + e1))


def _hg_masks(rows, nch, tmask_s, bdmask_s):
    r = lax.broadcasted_iota(jnp.int32, (rows, rows), 0)
    c = lax.broadcasted_iota(jnp.int32, (rows, rows), 1)
    tmask_s[...] = jnp.where(((r >> CHUNK_SHIFT) == (c >> CHUNK_SHIFT)) & (r >= c), 1.0, 0.0)
    r = lax.broadcasted_iota(jnp.int32, (rows, nch * HEAD_DIM), 0)
    c = lax.broadcasted_iota(jnp.int32, (rows, nch * HEAD_DIM), 1)
    bdmask_s[...] = jnp.where((r >> CHUNK_SHIFT) == (c >> HEAD_SHIFT), 1.0, 0.0).astype(BF16)


def _block_diag(x, nch, bdmask):
    return jnp.tile(x, (1, nch)) * bdmask


def _hgrn_fwd(hg, lb_logits, norm_g):
    s = hg.shape[1]
    rows = min(HG_ROWS, s)
    nblk = s // rows
    nch = rows // HG_CHUNK

    def body(hg_ref, lbl_ref, g_ref, o_ref, ya_ref, st0_ref, st_s, stall_s, tmask_s, bdmask_s):
        @pl.when(pl.program_id(1) == 0)
        def _():
            st_s[...] = jnp.zeros_like(st_s)
            _hg_masks(rows, nch, tmask_s, bdmask_s)

        bdmask = bdmask_s[...]
        tmask = tmask_s[...] > 0.5
        for hh in range(HG_HEADS):
            hc = slice(hh * HEAD_DIM, (hh + 1) * HEAD_DIM)
            hq = hg_ref[0, :, hc]
            hf = hg_ref[1, :, hc]
            hi = hg_ref[2, :, hc]
            hz = hg_ref[3, :, hc]
            lb, _ = _lower_bound(lbl_ref[:, hc])
            f = lb + (1.0 - lb) * jax.nn.sigmoid(hf)
            q = hq * jax.nn.sigmoid(hq)
            k = 1.0 - f
            logf = jnp.log(f)
            b = _chunk_cumsum(logf, rows)
            q_in = (q * jnp.exp(b)).astype(BF16)
            k_in = (k * jnp.exp(-b)).astype(BF16)
            k_out = (k * jnp.exp(_chunk_last(b, rows) - b)).astype(BF16)
            vb = hi.astype(BF16)

            sc = jnp.where(tmask, _dot_nt(q_in, k_in), 0.0)
            o_intra = _dot(sc.astype(BF16), vb)
            kvt = _dot_tn(vb, _block_diag(k_out, nch, bdmask))
            st = st_s[hh]
            st0_ref[hh] = st
            for c in range(nch):
                cols = slice(c * HEAD_DIM, (c + 1) * HEAD_DIM)
                last = (c + 1) * HG_CHUNK - 1
                stall_s[hh, :, cols] = st.astype(BF16)
                st = st * jnp.exp(b[last:last + 1, :]) + kvt[:, cols]
            st_s[hh] = st
            o = o_intra + _dot_nt(_block_diag(q_in, nch, bdmask), stall_s[hh])
            o_ref[:, hc] = o
            silu_z, _ = _silu_parts(hz)
            ya_ref[:, hc] = (_rms(o, g_ref[...]) * silu_z).astype(BF16)

    nh = HG_HEADS
    return pl.pallas_call(
        body, name="hgrn_fwd", grid=(HEADS // nh, nblk),
        in_specs=[pl.BlockSpec((4, rows, nh * HEAD_DIM), lambda h, i: (0, i, h)),
                  pl.BlockSpec((2, nh * HEAD_DIM), lambda h, i: (0, h)),
                  pl.BlockSpec((1, HEAD_DIM), lambda h, i: (0, 0))],
        out_specs=[pl.BlockSpec((rows, nh * HEAD_DIM), lambda h, i: (i, h)),
                   pl.BlockSpec((rows, nh * HEAD_DIM), lambda h, i: (i, h)),
                   pl.BlockSpec((nh, None, HEAD_DIM, HEAD_DIM), lambda h, i: (h, i, 0, 0))],
        out_shape=[jax.ShapeDtypeStruct((s, D_MODEL), F32), jax.ShapeDtypeStruct((s, D_MODEL), BF16),
                   jax.ShapeDtypeStruct((HEADS, nblk, HEAD_DIM, HEAD_DIM), F32)],
        scratch_shapes=[pltpu.VMEM((nh, HEAD_DIM, HEAD_DIM), F32), pltpu.VMEM((nh, HEAD_DIM, nch * HEAD_DIM), BF16),
                        pltpu.VMEM((rows, rows), F32), pltpu.VMEM((rows, nch * HEAD_DIM), BF16)],
        compiler_params=_params(2),
    )(hg, lb_logits, norm_g)


def _hgrn_bwd(hg, o_pre, dya, st0, lb_logits, norm_g):
    s = hg.shape[1]
    rows = min(HG_ROWS, s)
    nblk = s // rows
    nch = rows // HG_CHUNK

    def body(hg_ref, o_ref, dya_ref, st0_ref, lbl_ref, g_ref, dhg_ref, dlb_ref, dg_ref,
             dst_s, stp_s, stp_rows_s, dst_rows_s, dst_lane_s, dbl_s, tmask_s, bdmask_s):
        @pl.when(pl.program_id(1) == 0)
        def _():
            dst_s[...] = jnp.zeros_like(dst_s)
            dlb_ref[...] = jnp.zeros_like(dlb_ref)
            dg_ref[...] = jnp.zeros_like(dg_ref)
            _hg_masks(rows, nch, tmask_s, bdmask_s)

        bdmask = bdmask_s[...]
        tmask = tmask_s[...] > 0.5
        g = g_ref[...]
        for hh in range(HG_HEADS):
            hc = slice(hh * HEAD_DIM, (hh + 1) * HEAD_DIM)
            hq = hg_ref[0, :, hc]
            hf = hg_ref[1, :, hc]
            hi = hg_ref[2, :, hc]
            hz = hg_ref[3, :, hc]
            lb, _ = _lower_bound(lbl_ref[:, hc])
            sg = jax.nn.sigmoid(hf)
            f = lb + (1.0 - lb) * sg
            q, dsilu_q = _silu_parts(hq)
            k = 1.0 - f
            logf = jnp.log(f)
            b = _chunk_cumsum(logf, rows)
            eb = jnp.exp(b)
            enb = jnp.exp(-b)
            ebl = jnp.exp(_chunk_last(b, rows) - b)
            q_in32 = q * eb
            k_in32 = k * enb
            k_out32 = k * ebl
            q_in = q_in32.astype(BF16)
            k_in = k_in32.astype(BF16)
            k_out = k_out32.astype(BF16)
            vb = hi.astype(BF16)
            kbd = _block_diag(k_out, nch, bdmask)
            qbd = _block_diag(q_in, nch, bdmask)
            decs = [jnp.exp(b[(c + 1) * HG_CHUNK - 1:(c + 1) * HG_CHUNK, :]) for c in range(nch)]

            kvt = _dot_tn(vb, kbd)
            st = st0_ref[hh]
            for c in range(nch):
                stp_s[hh, c] = st
                stp_rows_s[hh, c * HEAD_DIM:(c + 1) * HEAD_DIM, :] = st.astype(BF16)
                st = st * decs[c] + kvt[:, c * HEAD_DIM:(c + 1) * HEAD_DIM]

            o = o_ref[:, hc]
            rstd = lax.rsqrt(jnp.mean(o * o, axis=-1, keepdims=True) + EPS)
            oh = o * rstd
            silu_z, dsilu_z = _silu_parts(hz)
            dya_v = dya_ref[:, hc]
            dn = dya_v * silu_z
            dhz = dya_v * (oh * g) * dsilu_z
            dg_ref[hh] += jnp.sum(dn * oh, axis=0, keepdims=True)
            doh = dn * g
            do = (rstd * (doh - oh * jnp.mean(doh * oh, axis=-1, keepdims=True))).astype(BF16)

            dq_all = _dot_tn(do, qbd)
            dst = dst_s[hh]
            ddecs = [None] * nch
            for c in reversed(range(nch)):
                dstb = dst.astype(BF16)
                dst_lane_s[hh, :, c * HEAD_DIM:(c + 1) * HEAD_DIM] = dstb
                dst_rows_s[hh, c * HEAD_DIM:(c + 1) * HEAD_DIM, :] = dstb
                ddecs[c] = jnp.sum(dst * stp_s[hh, c], axis=0, keepdims=True) * decs[c]
                dst = dst * decs[c] + dq_all[:, c * HEAD_DIM:(c + 1) * HEAD_DIM]
            dst_s[hh] = dst

            sc = jnp.where(tmask, _dot_nt(q_in, k_in), 0.0).astype(BF16)
            dkout = _dot(_block_diag(vb, nch, bdmask), dst_rows_s[hh])
            dv = _dot_nt(kbd, dst_lane_s[hh]) + _dot_tn(sc, do)
            dsc = jnp.where(tmask, _dot_nt(do, vb), 0.0).astype(BF16)
            dqin = _dot(dsc, k_in) + _dot(_block_diag(do, nch, bdmask), stp_rows_s[hh])
            dkin = _dot_tn(dsc, q_in)

            dko = dkout * k_out32
            for c in range(nch):
                sl = slice(c * HG_CHUNK, (c + 1) * HG_CHUNK)
                dbl = jnp.sum(dko[sl], axis=0, keepdims=True) + ddecs[c]
                dbl_s[hh, sl, :] = jnp.broadcast_to(dbl, (HG_CHUNK, HEAD_DIM))
            dq = dqin * eb
            dk = dkin * enb + dkout * ebl
            db = dqin * q_in32 - dkin * k_in32 - dko
            dlogf = _chunk_revcumsum(db, rows) + dbl_s[hh]
            df = dlogf / f - dk
            dlb_ref[:, hc] += jnp.sum(df * (1.0 - sg), axis=0, keepdims=True)
            dhg_ref[0, :, hc] = (dq * dsilu_q).astype(BF16)
            dhg_ref[1, :, hc] = (df * (1.0 - lb) * sg * (1.0 - sg)).astype(BF16)
            dhg_ref[2, :, hc] = dv.astype(BF16)
            dhg_ref[3, :, hc] = dhz.astype(BF16)

    last = nblk - 1
    nh = HG_HEADS
    wide = nh * HEAD_DIM
    return pl.pallas_call(
        body, name="hgrn_bwd", grid=(HEADS // nh, nblk),
        in_specs=[pl.BlockSpec((4, rows, wide), lambda h, i: (0, last - i, h)),
                  pl.BlockSpec((rows, wide), lambda h, i: (last - i, h)),
                  pl.BlockSpec((rows, wide), lambda h, i: (last - i, h)),
                  pl.BlockSpec((nh, None, HEAD_DIM, HEAD_DIM), lambda h, i: (h, last - i, 0, 0)),
                  pl.BlockSpec((2, wide), lambda h, i: (0, h)),
                  pl.BlockSpec((1, HEAD_DIM), lambda h, i: (0, 0))],
        out_specs=[pl.BlockSpec((4, rows, wide), lambda h, i: (0, last - i, h)),
                   pl.BlockSpec((1, wide), lambda h, i: (0, h)),
                   pl.BlockSpec((nh, 1, HEAD_DIM), lambda h, i: (h, 0, 0))],
        out_shape=[jax.ShapeDtypeStruct((4, s, D_MODEL), BF16), jax.ShapeDtypeStruct((1, D_MODEL), F32),
                   jax.ShapeDtypeStruct((HEADS, 1, HEAD_DIM), F32)],
        scratch_shapes=[pltpu.VMEM((nh, HEAD_DIM, HEAD_DIM), F32), pltpu.VMEM((nh, nch, HEAD_DIM, HEAD_DIM), F32),
                        pltpu.VMEM((nh, nch * HEAD_DIM, HEAD_DIM), BF16), pltpu.VMEM((nh, nch * HEAD_DIM, HEAD_DIM), BF16),
                        pltpu.VMEM((nh, HEAD_DIM, nch * HEAD_DIM), BF16), pltpu.VMEM((nh, rows, HEAD_DIM), F32),
                        pltpu.VMEM((rows, rows), F32), pltpu.VMEM((rows, nch * HEAD_DIM), BF16)],
        compiler_params=_params(2),
    )(hg, o_pre, dya, st0, lb_logits, norm_g)


def _mla_pre(ms, q_a_g, kv_a_g, wuq3, wukv3, tabs):
    s = ms.shape[0]
    tm = min(TM_MLA, s)

    def body(ms_ref, qg_ref, kvg_ref, wuq_ref, wukv_ref, c_ref, sa_ref, sb_ref,
             q_ref, k_ref, v_ref, cqn_ref, ckvn_ref):
        c, sa, sb = c_ref[...], sa_ref[...], sb_ref[...]
        cqn = _rms(ms_ref[:, 0:Q_LORA], qg_ref[...]).astype(BF16)
        ckvn = _rms(ms_ref[:, Q_LORA:Q_LORA + KV_LORA], kvg_ref[...]).astype(BF16)
        cqn_ref[...] = cqn
        ckvn_ref[...] = ckvn
        k_pe = _rope(ms_ref[:, Q_LORA + KV_LORA:MS_COLS], c, sa, sb).astype(BF16)
        for h in range(HEADS):
            qh = _dot(cqn, wuq_ref[h])
            q_ref[h, :, 0:128] = (qh[:, 0:128] * Q_PRESCALE).astype(BF16)
            q_ref[h, :, 128:256] = (_rope(qh[:, 128:256], c, sa, sb) * Q_PRESCALE).astype(BF16)
            kvh = _dot(ckvn, wukv_ref[h])
            k_ref[h, :, 0:128] = kvh[:, 0:128].astype(BF16)
            k_ref[h, :, 128:256] = k_pe
            v_ref[h] = kvh[:, 128:256].astype(BF16)

    tab = pl.BlockSpec((tm, 128), lambda i: (i, 0))
    return pl.pallas_call(
        body, name="mla_pre", grid=(s // tm,),
        in_specs=[pl.BlockSpec((tm, MS_COLS), lambda i: (i, 0)),
                  pl.BlockSpec((1, Q_LORA), lambda i: (0, 0)), pl.BlockSpec((1, KV_LORA), lambda i: (0, 0)),
                  pl.BlockSpec((HEADS, Q_LORA, QK_PAD), lambda i: (0, 0, 0)),
                  pl.BlockSpec((HEADS, KV_LORA, 256), lambda i: (0, 0, 0)), tab, tab, tab],
        out_specs=[pl.BlockSpec((HEADS, tm, QK_PAD), lambda i: (0, i, 0)),
                   pl.BlockSpec((HEADS, tm, QK_PAD), lambda i: (0, i, 0)),
                   pl.BlockSpec((HEADS, tm, HEAD_DIM), lambda i: (0, i, 0)),
                   pl.BlockSpec((tm, Q_LORA), lambda i: (i, 0)), pl.BlockSpec((tm, KV_LORA), lambda i: (i, 0))],
        out_shape=[jax.ShapeDtypeStruct((HEADS, s, QK_PAD), BF16), jax.ShapeDtypeStruct((HEADS, s, QK_PAD), BF16),
                   jax.ShapeDtypeStruct((HEADS, s, HEAD_DIM), BF16),
                   jax.ShapeDtypeStruct((s, Q_LORA), BF16), jax.ShapeDtypeStruct((s, KV_LORA), BF16)],
        compiler_params=_params(1),
    )(ms, q_a_g, kv_a_g, wuq3, wukv3, *tabs)


def _causal_mask(t):
    r = lax.broadcasted_iota(jnp.int32, (t, t), 0)
    c = lax.broadcasted_iota(jnp.int32, (t, t), 1)
    return r >= c


def _flash_fwd(q, k, v, mz):
    s = q.shape[1]
    t = min(TQ, s)

    def body(q_ref, k_ref, v_ref, mz_ref, o_ref, yb_ref, lse_ref, m_s, l_s, acc_s):
        i = pl.program_id(1)
        m_s[...] = jnp.full_like(m_s, -jnp.inf)
        l_s[...] = jnp.zeros_like(l_s)
        acc_s[...] = jnp.zeros_like(acc_s)

        def step(j, groups):
            rows = pl.ds(pl.multiple_of(j * t, t), t)
            for hh in range(FLASH_HEADS):
                for r0, nr, masked in groups:
                    r = slice(r0, r0 + nr)
                    sc = _dot_nt(q_ref[hh, r, :], k_ref[hh, rows, :])
                    if masked:
                        sc = jnp.where(_causal_mask(t), sc, -jnp.inf)
                    m_prev = m_s[hh, r, :]
                    m_new = jnp.maximum(m_prev, jnp.max(sc, axis=-1, keepdims=True))
                    p = jnp.exp2(sc - jnp.tile(m_new, (1, t // 128))).astype(BF16)
                    alpha = jnp.exp2(m_prev - m_new)
                    l_s[hh, r, :] = alpha * l_s[hh, r, :] + jnp.sum(p.astype(F32), axis=-1, keepdims=True)
                    acc_s[hh, r, :] = alpha * acc_s[hh, r, :] + _dot(p, v_ref[hh, rows, :])
                    m_s[hh, r, :] = m_new

        def loop_body(jj, carry):
            for u in range(4):
                step(4 * jj + u, ((0, 2 * t, False),))
            return carry

        lax.fori_loop(0, i // 2, loop_body, 0)

        @pl.when(i % 2 == 1)
        def _():
            step(2 * i - 2, ((0, 2 * t, False),))
            step(2 * i - 1, ((0, 2 * t, False),))

        step(2 * i, ((0, t, True), (t, t, False)))
        step(2 * i + 1, ((t, t, True),))
        for hh in range(FLASH_HEADS):
            cols = slice(hh * HEAD_DIM, (hh + 1) * HEAD_DIM)
            out = acc_s[hh] / l_s[hh]
            o_ref[:, cols] = out
            silu_z, _ = _silu_parts(mz_ref[:, cols])
            yb_ref[:, cols] = (out * silu_z).astype(BF16)
            lse_ref[hh] = m_s[hh] + jnp.log2(l_s[hh])

    nh = FLASH_HEADS
    t2 = 2 * t
    col = pl.BlockSpec((t2, nh * HEAD_DIM), lambda h, i: (i, h))
    return pl.pallas_call(
        body, name="flash_fwd", grid=(HEADS // nh, s // t2),
        in_specs=[pl.BlockSpec((nh, t2, QK_PAD), lambda h, i: (h, i, 0)),
                  pl.BlockSpec((nh, s, QK_PAD), lambda h, i: (h, 0, 0)),
                  pl.BlockSpec((nh, s, HEAD_DIM), lambda h, i: (h, 0, 0)), col],
        out_specs=[col, col, pl.BlockSpec((nh, t2, 128), lambda h, i: (h, i, 0))],
        out_shape=[jax.ShapeDtypeStruct((s, D_MODEL), F32), jax.ShapeDtypeStruct((s, D_MODEL), BF16),
                   jax.ShapeDtypeStruct((HEADS, s, 128), F32)],
        scratch_shapes=[pltpu.VMEM((nh, t2, 128), F32), pltpu.VMEM((nh, t2, 128), F32),
                        pltpu.VMEM((nh, t2, HEAD_DIM), F32)],
        compiler_params=_params(2),
    )(q, k, v, mz)


def _flash_bwd(q, k, v, dyb, mz, o_att, lse, tabs):
    s = q.shape[1]
    t = min(TQ, s)

    def body(q_ref, k_ref, v_ref, dyb_ref, mz_ref, o_ref, lse_ref, c_ref, sa_ref, sb_ref,
             dq_ref, dk_ref, dv_ref, dmz_ref, dq_s, delta_s, do_s):
        i = pl.program_id(1)

        @pl.when(i == 0)
        def _():
            dk_ref[...] = jnp.zeros_like(dk_ref)
            dv_ref[...] = jnp.zeros_like(dv_ref)

        silu_z, dsilu_z = _silu_parts(mz_ref[...])
        dyb_v = dyb_ref[...]
        out = o_ref[...]
        do32 = dyb_v * silu_z
        dmz_ref[...] = (dyb_v * out * dsilu_z).astype(BF16)
        delta_s[...] = jnp.broadcast_to(jnp.sum(do32 * out, axis=-1, keepdims=True), (2 * t, 128))
        do_s[...] = do32.astype(BF16)
        dq_s[...] = jnp.zeros_like(dq_s)

        def step(j, modes):
            rows = pl.ds(pl.multiple_of(j * t, t), t)
            kj = k_ref[rows, :]
            vj = v_ref[rows, :]
            dv_acc = None
            dk_acc = None
            for ch, masked in enumerate(modes):
                if masked is None:
                    continue
                r = slice(ch * t, (ch + 1) * t)
                qv = q_ref[r, :]
                do = do_s[r, :]
                sc = _dot_nt(qv, kj)
                if masked:
                    sc = jnp.where(_causal_mask(t), sc, -jnp.inf)
                p = jnp.exp2(sc - jnp.tile(lse_ref[r, :], (1, t // 128)))
                dp = _dot_nt(do, vj)
                ds = (p * (dp - jnp.tile(delta_s[r, :], (1, t // 128)))).astype(BF16)
                dv_c = _dot_tn(p.astype(BF16), do)
                dk_c = _dot_tn(ds, qv)
                dv_acc = dv_c if dv_acc is None else dv_acc + dv_c
                dk_acc = dk_c if dk_acc is None else dk_acc + dk_c
                dq_s[r, :] += _dot(ds, kj)
            dv_ref[rows, :] += dv_acc
            dk_ref[rows, :] += dk_acc

        def loop_body(jj, carry):
            for u in range(4):
                step(4 * jj + u, (False, False))
            return carry

        lax.fori_loop(0, i // 2, loop_body, 0)

        @pl.when(i % 2 == 1)
        def _():
            step(2 * i - 2, (False, False))
            step(2 * i - 1, (False, False))

        step(2 * i, (True, False))
        step(2 * i + 1, (None, True))
        dq = dq_s[...] * ATT_SCALE
        dq_ref[:, 0:128] = dq[:, 0:128].astype(BF16)
        dq_ref[:, 128:256] = _rope_bwd(dq[:, 128:256], c_ref[...], sa_ref[...], sb_ref[...]).astype(BF16)

    t2 = 2 * t
    col = pl.BlockSpec((t2, HEAD_DIM), lambda h, i: (i, h))
    tab = pl.BlockSpec((t2, 128), lambda h, i: (i, 0))
    return pl.pallas_call(
        body, name="flash_bwd", grid=(HEADS, s // t2),
        in_specs=[pl.BlockSpec((None, t2, QK_PAD), lambda h, i: (h, i, 0)),
                  pl.BlockSpec((None, s, QK_PAD), lambda h, i: (h, 0, 0)),
                  pl.BlockSpec((None, s, HEAD_DIM), lambda h, i: (h, 0, 0)),
                  col, col, col, pl.BlockSpec((None, t2, 128), lambda h, i: (h, i, 0)), tab, tab, tab],
        out_specs=[pl.BlockSpec((None, t2, QK_PAD), lambda h, i: (h, i, 0)),
                   pl.BlockSpec((None, s, QK_PAD), lambda h, i: (h, 0, 0)),
                   pl.BlockSpec((None, s, HEAD_DIM), lambda h, i: (h, 0, 0)), col],
        out_shape=[jax.ShapeDtypeStruct((HEADS, s, QK_PAD), BF16), jax.ShapeDtypeStruct((HEADS, s, QK_PAD), F32),
                   jax.ShapeDtypeStruct((HEADS, s, HEAD_DIM), F32), jax.ShapeDtypeStruct((s, D_MODEL), BF16)],
        scratch_shapes=[pltpu.VMEM((t2, QK_PAD), F32), pltpu.VMEM((t2, 128), F32), pltpu.VMEM((t2, HEAD_DIM), BF16)],
        compiler_params=_params(2),
    )(q, k, v, dyb, mz, o_att, lse, *tabs)


def _mla_bwd_proj(dq, dk, dv, cqn, ckvn, ms, q_a_g, kv_a_g, wuq3, wukv3, tabs):
    s = ms.shape[0]
    tm = min(TM_MLA, s)

    def body(dq_ref, dk_ref, dv_ref, cqn_ref, ckvn_ref, ms_ref, qg_ref, kvg_ref, wuq_ref, wukv_ref,
             c_ref, sa_ref, sb_ref, dms_ref, dwuq_ref, dwukv_ref, dqg_ref, dkvg_ref):
        @pl.when(pl.program_id(0) == 0)
        def _():
            dwuq_ref[...] = jnp.zeros_like(dwuq_ref)
            dwukv_ref[...] = jnp.zeros_like(dwukv_ref)
            dqg_ref[...] = jnp.zeros_like(dqg_ref)
            dkvg_ref[...] = jnp.zeros_like(dkvg_ref)

        cqn = cqn_ref[...]
        ckvn = ckvn_ref[...]
        dcqn = jnp.zeros((tm, Q_LORA), F32)
        dckvn = jnp.zeros((tm, KV_LORA), F32)
        dkpe = jnp.zeros((tm, 128), F32)
        for h in range(HEADS):
            dqh = dq_ref[h]
            dcqn += _dot_nt(dqh, wuq_ref[h])
            dwuq_ref[h] += _dot_tn(cqn, dqh)
            dkh = dk_ref[h] * LN2
            dkvh = jnp.concatenate([dkh[:, 0:128], dv_ref[h]], axis=1).astype(BF16)
            dckvn += _dot_nt(dkvh, wukv_ref[h])
            dwukv_ref[h] += _dot_tn(ckvn, dkvh)
            dkpe += dkh[:, 128:256]
        dcq, dqg_rows = _rms_bwd(ms_ref[:, 0:Q_LORA], qg_ref[...], dcqn)
        dckv, dkvg_rows = _rms_bwd(ms_ref[:, Q_LORA:Q_LORA + KV_LORA], kvg_ref[...], dckvn)
        dqg_ref[...] += jnp.sum(dqg_rows, axis=0, keepdims=True)
        dkvg_ref[...] += jnp.sum(dkvg_rows, axis=0, keepdims=True)
        dms_ref[:, 0:Q_LORA] = dcq.astype(BF16)
        dms_ref[:, Q_LORA:Q_LORA + KV_LORA] = dckv.astype(BF16)
        dms_ref[:, Q_LORA + KV_LORA:MS_COLS] = _rope_bwd(dkpe, c_ref[...], sa_ref[...], sb_ref[...]).astype(BF16)

    tab = pl.BlockSpec((tm, 128), lambda i: (i, 0))
    wq = pl.BlockSpec((HEADS, Q_LORA, QK_PAD), lambda i: (0, 0, 0))
    wkv = pl.BlockSpec((HEADS, KV_LORA, 256), lambda i: (0, 0, 0))
    qg = pl.BlockSpec((1, Q_LORA), lambda i: (0, 0))
    kvg = pl.BlockSpec((1, KV_LORA), lambda i: (0, 0))
    return pl.pallas_call(
        body, name="mla_bwd_proj", grid=(s // tm,),
        in_specs=[pl.BlockSpec((HEADS, tm, QK_PAD), lambda i: (0, i, 0)),
                  pl.BlockSpec((HEADS, tm, QK_PAD), lambda i: (0, i, 0)),
                  pl.BlockSpec((HEADS, tm, HEAD_DIM), lambda i: (0, i, 0)),
                  pl.BlockSpec((tm, Q_LORA), lambda i: (i, 0)), pl.BlockSpec((tm, KV_LORA), lambda i: (i, 0)),
                  pl.BlockSpec((tm, MS_COLS), lambda i: (i, 0)), qg, kvg, wq, wkv, tab, tab, tab],
        out_specs=[pl.BlockSpec((tm, MS_COLS), lambda i: (i, 0)), wq, wkv, qg, kvg],
        out_shape=[jax.ShapeDtypeStruct((s, MS_COLS), BF16), jax.ShapeDtypeStruct((HEADS, Q_LORA, QK_PAD), F32),
                   jax.ShapeDtypeStruct((HEADS, KV_LORA, 256), F32),
                   jax.ShapeDtypeStruct((1, Q_LORA), F32), jax.ShapeDtypeStruct((1, KV_LORA), F32)],
        compiler_params=_params(1),
    )(dq, dk, dv, cqn, ckvn, ms, q_a_g, kv_a_g, wuq3, wukv3, *tabs)


def _merge_fused(ya, yb, glog, b_gate, x, tgt, fg, wproj):
    s = x.shape[0]
    tm = min(TM_FUSED, s)

    def body(ya_ref, yb_ref, g0_ref, g1_ref, b0_ref, b1_ref, x_ref, t_ref, fg_ref, w_ref,
             mg_ref, dx2_ref, dx2b_ref, dya_ref, dyb_ref, dgl_ref, dpa_ref, dpb_ref, loss_ref, dfg_ref, dbg_ref):
        @pl.when(pl.program_id(0) == 0)
        def _():
            loss_ref[...] = jnp.zeros_like(loss_ref)
            dfg_ref[...] = jnp.zeros_like(dfg_ref)
            dbg_ref[...] = jnp.zeros_like(dbg_ref)

        pa = _dot(ya_ref[...], w_ref[0])
        pb = _dot(yb_ref[...], w_ref[1])
        g0 = jax.nn.sigmoid(g0_ref[...] + b0_ref[...])
        g1 = jax.nn.sigmoid(g1_ref[...] + b1_ref[...])
        merged = (g0 * pa + g1 * pb).astype(BF16)
        mg_ref[...] = merged
        x2 = x_ref[...] + _dot(merged, w_ref[2])
        fg_v = fg_ref[...]
        err = _rms(x2, fg_v) - t_ref[...]
        loss_ref[...] += 0.5 * jnp.sum(jnp.mean(err * err, axis=-1, keepdims=True), axis=0, keepdims=True)
        dx2, dfg_rows = _rms_bwd(x2, fg_v, err * (1.0 / D_MODEL))
        dx2_ref[...] = dx2
        dfg_ref[...] += jnp.sum(dfg_rows, axis=0, keepdims=True)

        dx2b = dx2.astype(BF16)
        dx2b_ref[...] = dx2b
        dmg = _dot_nt(dx2b, w_ref[2])
        dpa = (dmg * g0).astype(BF16)
        dpb = (dmg * g1).astype(BF16)
        dpa_ref[...] = dpa
        dpb_ref[...] = dpb
        dgl0 = dmg * pa * g0 * (1.0 - g0)
        dgl1 = dmg * pb * g1 * (1.0 - g1)
        dgl_ref[:, 0:D_MODEL] = dgl0.astype(BF16)
        dgl_ref[:, D_MODEL:2 * D_MODEL] = dgl1.astype(BF16)
        dbg_ref[:, 0:D_MODEL] += jnp.sum(dgl0, axis=0, keepdims=True)
        dbg_ref[:, D_MODEL:2 * D_MODEL] += jnp.sum(dgl1, axis=0, keepdims=True)
        dya_ref[...] = _dot_nt(dpa, w_ref[0])
        dyb_ref[...] = _dot_nt(dpb, w_ref[1])

    row = pl.BlockSpec((tm, D_MODEL), lambda i: (i, 0))
    row1 = pl.BlockSpec((tm, D_MODEL), lambda i: (i, 1))
    row2 = pl.BlockSpec((tm, 2 * D_MODEL), lambda i: (i, 0))
    vec = pl.BlockSpec((1, D_MODEL), lambda i: (0, 0))
    vec1 = pl.BlockSpec((1, D_MODEL), lambda i: (0, 1))
    vec2 = pl.BlockSpec((1, 2 * D_MODEL), lambda i: (0, 0))
    f32_rows = jax.ShapeDtypeStruct((s, D_MODEL), F32)
    bf16_rows = jax.ShapeDtypeStruct((s, D_MODEL), BF16)
    return pl.pallas_call(
        body, name="merge_fused", grid=(s // tm,),
        in_specs=[row, row, row, row1, vec, vec1, row, row, vec, pl.BlockSpec((3, D_MODEL, D_MODEL), lambda i: (0, 0, 0))],
        out_specs=[row, row, row, row, row, row2, row, row, pl.BlockSpec((1, 128), lambda i: (0, 0)), vec, vec2],
        out_shape=[bf16_rows, f32_rows, bf16_rows, f32_rows, f32_rows, jax.ShapeDtypeStruct((s, 2 * D_MODEL), BF16),
                   bf16_rows, bf16_rows, jax.ShapeDtypeStruct((1, 128), F32), jax.ShapeDtypeStruct((1, D_MODEL), F32),
                   jax.ShapeDtypeStruct((1, 2 * D_MODEL), F32)],
        compiler_params=_params(1),
    )(ya, yb, glog, glog, b_gate, b_gate, x, tgt, fg, wproj)


def _proj_fused(x, g, w_int, r_blk):
    s = x.shape[0]
    tm = min(TM_FUSED, s)

    def body(x_ref, g_ref, w_hbm, r_ref, h_ref, hg_ref, ms_ref, mz_ref, gl_ref, or_ref, w_s, sem, send_sems, recv_sems):
        mx, my, mc = _me()
        chips = _other_chips(mx, my)
        mine, theirs = _cols(mc), _cols(1 - mc)

        def copy(k, src, dst, to):
            return pltpu.make_async_remote_copy(src_ref=src, dst_ref=dst, send_sem=send_sems.at[k],
                                                recv_sem=recv_sems.at[k], device_id=to, device_id_type=MESH_ID)

        def sends():
            return [copy(j, r_ref.at[:, mine], or_ref.at[2 * mx + my, :, mine], (cx, cy, mc))
                    for j, (cx, cy) in enumerate(chips)]

        @pl.when(pl.program_id(0) == 0)
        def _():
            for cp in sends():
                cp.start()
            cp = pltpu.make_async_copy(w_hbm, w_s, sem)
            cp.start()
            cp.wait()

        h = _rms(x_ref[...], g_ref[...]).astype(BF16)
        h_ref[...] = h
        for j in range(4):
            hg_ref[j] = _dot_nt(h, w_s[j * D_MODEL:(j + 1) * D_MODEL, :])
        ms = _dot_nt(h, w_s[4096:4096 + MS_COLS, :])
        lane = lax.broadcasted_iota(jnp.int32, ms.shape, 1)
        ms_ref[...] = jnp.where(lane < 704, ms, 0.0)
        mz_ref[...] = _dot_nt(h, w_s[4800:5824, :])
        for j in range(2):
            gl_ref[:, j * D_MODEL:(j + 1) * D_MODEL] = _dot_nt(h, w_s[5824 + j * D_MODEL:5824 + (j + 1) * D_MODEL, :])

        def passes():
            return [copy(3 + j, or_ref.at[2 * cx + cy, :, mine], or_ref.at[2 * cx + cy, :, mine], (mx, my, 1 - mc))
                    for j, (cx, cy) in enumerate(chips)]

        @pl.when(pl.program_id(0) == (3 * (s // tm)) // 4)
        def _():
            for j, (cx, cy) in enumerate(chips):
                landed = or_ref.at[2 * cx + cy, :, mine]
                copy(j, landed, landed, (cx, cy, mc)).wait_recv()
            for cp in passes():
                cp.start()

        @pl.when(pl.program_id(0) == s // tm - 1)
        def _():
            for j, (cx, cy) in enumerate(chips):
                other = or_ref.at[2 * cx + cy, :, theirs]
                copy(3 + j, other, other, (mx, my, 1 - mc)).wait_recv()
            for cp in sends() + passes():
                cp.wait_send()

    row = pl.BlockSpec((tm, D_MODEL), lambda i: (i, 0))
    outs = pl.pallas_call(
        body, name="proj_fused", grid=(s // tm,),
        in_specs=[row, pl.BlockSpec((1, D_MODEL), lambda i: (0, 0)), ANY, ANY],
        out_specs=[row, pl.BlockSpec((4, tm, D_MODEL), lambda i: (0, i, 0)), pl.BlockSpec((tm, MS_COLS), lambda i: (i, 0)),
                   row, pl.BlockSpec((tm, 2 * D_MODEL), lambda i: (i, 0)), ANY],
        out_shape=[jax.ShapeDtypeStruct((s, D_MODEL), BF16), jax.ShapeDtypeStruct((4, s, D_MODEL), F32),
                   jax.ShapeDtypeStruct((s, MS_COLS), F32), jax.ShapeDtypeStruct((s, D_MODEL), F32),
                   jax.ShapeDtypeStruct((s, 2 * D_MODEL), F32), jax.ShapeDtypeStruct((N_CHIPS,) + r_blk.shape, r_blk.dtype)],
        scratch_shapes=[pltpu.VMEM(w_int.shape, BF16), pltpu.SemaphoreType.DMA,
                        pltpu.SemaphoreType.DMA((6,)), pltpu.SemaphoreType.DMA((6,))],
        compiler_params=_params(1),
    )(x, g, w_int, r_blk)
    gr = lax.dynamic_update_slice(outs[5], r_blk[None], (2 * lax.axis_index("x") + lax.axis_index("y"), 0, 0))
    return (*outs[:5], gr)


def _dh_fused(dhg, dms, dmz, dglog, w_int, x, g, dx2, hw, hr):
    s = x.shape[0]
    tm = min(512, s)
    nw, nr = hw.shape[0] // N_CHIPS, hr.shape[0] // N_CHIPS

    def body(dhg_ref, dms_ref, dmz_ref, dgl_ref, w_hbm, x_ref, g_ref, dx2_ref, hw_ref, hr_ref,
             dx_ref, dg_ref, lw_ref, lr_ref, w_s, sem, send_sems, recv_sems):
        def scatter_copies():
            mx, my, mc = _me()
            return [pltpu.make_async_remote_copy(
                src_ref=src.at[pl.ds((2 * cx + cy) * n, n), :], dst_ref=dst.at[j], send_sem=send_sems.at[3 * a + j],
                recv_sem=recv_sems.at[3 * a + j], device_id=(cx, cy, mc), device_id_type=MESH_ID)
                for a, (src, dst, n) in enumerate([(hw_ref, lw_ref, nw), (hr_ref, lr_ref, nr)])
                for j, (cx, cy) in enumerate(_other_chips(mx, my))]

        @pl.when(pl.program_id(0) == 0)
        def _():
            for cp in scatter_copies():
                cp.start()
            dg_ref[...] = jnp.zeros_like(dg_ref)
            cp = pltpu.make_async_copy(w_hbm, w_s, sem)
            cp.start()
            cp.wait()

        dh = _dot(dms_ref[...], w_s[4096:4096 + MS_COLS, :]) + _dot(dmz_ref[...], w_s[4800:5824, :])
        for j in range(4):
            dh += _dot(dhg_ref[j], w_s[j * D_MODEL:(j + 1) * D_MODEL, :])
        for j in range(2):
            dh += _dot(dgl_ref[:, j * D_MODEL:(j + 1) * D_MODEL], w_s[5824 + j * D_MODEL:5824 + (j + 1) * D_MODEL, :])
        dx, dg_rows = _rms_bwd(x_ref[...], g_ref[...], dh)
        dx_ref[...] = dx + dx2_ref[...]
        dg_ref[...] += jnp.sum(dg_rows, axis=0, keepdims=True)

        @pl.when(pl.program_id(0) == s // tm - 1)
        def _():
            for cp in scatter_copies():
                cp.wait()

    row = pl.BlockSpec((tm, D_MODEL), lambda i: (i, 0))
    vec = pl.BlockSpec((1, D_MODEL), lambda i: (0, 0))
    return pl.pallas_call(
        body, name="dh_fused", grid=(s // tm,),
        in_specs=[pl.BlockSpec((4, tm, D_MODEL), lambda i: (0, i, 0)), pl.BlockSpec((tm, MS_COLS), lambda i: (i, 0)), row,
                  pl.BlockSpec((tm, 2 * D_MODEL), lambda i: (i, 0)), ANY, row, vec, row, ANY, ANY],
        out_specs=[row, vec, ANY, ANY],
        out_shape=[jax.ShapeDtypeStruct((s, D_MODEL), F32), jax.ShapeDtypeStruct((1, D_MODEL), F32),
                   jax.ShapeDtypeStruct((3, nw, HALF_COLS), hw.dtype), jax.ShapeDtypeStruct((3, nr, HALF_COLS), hr.dtype)],
        scratch_shapes=[pltpu.VMEM(w_int.shape, BF16), pltpu.SemaphoreType.DMA,
                        pltpu.SemaphoreType.DMA((6,)), pltpu.SemaphoreType.DMA((6,))],
        compiler_params=_params(1),
    )(dhg, dms, dmz, dglog, w_int, x, g, dx2, hw, hr)


def _local_step(x, tgt, w_int, r_blk, norm_g, b_gate, lb_logits, hg_norm_g, q_a_g, kv_a_g, fg):
    s = x.shape[0]
    tabs = _rope_tables(s)

    h, hg, ms, mz, glog, gr = _proj_fused(x, norm_g, w_int, r_blk)
    w_uq, w_ukv, wproj = _unpack_rest_weights(gr)
    wuq3 = jnp.pad(w_uq.reshape(Q_LORA, HEADS, QK_DIM).transpose(1, 0, 2), ((0, 0), (0, 0), (0, QK_PAD - QK_DIM)))
    wukv3 = w_ukv.reshape(KV_LORA, HEADS, 256).transpose(1, 0, 2)
    o_pre, ya, st0 = _hgrn_fwd(hg, lb_logits, hg_norm_g)
    q, k, v, cqn, ckvn = _mla_pre(ms, q_a_g, kv_a_g, wuq3, wukv3, tabs)
    o_att, yb, lse = _flash_fwd(q, k, v, mz)
    merged, dx2, dx2b, dya, dyb, dglog, dpa, dpb, loss, dfg, dbg = _merge_fused(ya, yb, glog, b_gate, x, tgt, fg, wproj)

    d_wout = _mm_tn(merged, dx2b, name="dw_out")
    d_wpa = _mm_tn(ya, dpa, name="dw_proj_a")
    d_wpb = _mm_tn(yb, dpb, name="dw_proj_b")
    dhg, dlb, dhgg = _hgrn_bwd(hg, o_pre, dya, st0, lb_logits, hg_norm_g)
    dq, dk, dv, dmz = _flash_bwd(q, k, v, dyb, mz, o_att, lse, tabs)
    dms, d_wuq3, d_wukv3, dqg, dkvg = _mla_bwd_proj(dq, dk, dv, cqn, ckvn, ms, q_a_g, kv_a_g, wuq3, wukv3, tabs)
    d_rest = _pack_rest_grads(d_wuq3, d_wukv3, (d_wpa, d_wpb, d_wout))
    d_hg, l_rest = _mm_tn(dhg, h, name="dw_in_hg", swap=d_rest)
    d_ms = _mm_tn(dms, h, name="dw_in_ms")
    d_mz = _mm_tn(dmz, h, name="dw_in_mz")
    d_gl = _mm_tn(dglog, h, name="dw_in_gate")
    d_w_int = jnp.concatenate([d_hg.reshape(4 * D_MODEL, D_MODEL), d_ms[0:704], d_mz, d_gl], axis=0)
    small = {"b_gate": dbg, "lb": dlb, "hg_norm_g": dhgg, "q_a_g": dqg, "kv_a_g": dkvg, "final_norm_g": dfg}
    dh_args = (dhg, dms, dmz, dglog, w_int, x, norm_g, dx2)
    return loss, dh_args, d_w_int, d_rest, l_rest, small


def _pack_rest(w_uq_b, w_ukv_b, wpa_b, wpb_b, wout_b):
    return jnp.concatenate([w_uq_b.reshape(144, D_MODEL), w_ukv_b.reshape(128, D_MODEL), wpa_b, wpb_b, wout_b], axis=0)


def _unpack_rest(p):
    return (p[0:144].reshape(Q_LORA, 384), p[144:272].reshape(KV_LORA, 512), p[272:528], p[528:784], p[784:1040])


def _pack_rest_grads(d_wuq3, d_wukv3, d_proj):
    d_wuq = d_wuq3.transpose(1, 0, 2)[:, :, 0:QK_DIM].reshape(Q_LORA, HEADS * QK_DIM)
    d_wukv = d_wukv3.transpose(1, 0, 2).reshape(KV_LORA, HEADS * 256)
    blocks = []
    for b in range(N_CHIPS):
        rows = slice(b * 256, (b + 1) * 256)
        blocks.append(_pack_rest(d_wuq[:, b * 384:(b + 1) * 384], d_wukv[:, b * 512:(b + 1) * 512],
                                 d_proj[0][rows], d_proj[1][rows], d_proj[2][rows]))
    return jnp.stack(blocks, axis=0)


def _unpack_rest_weights(g):
    parts = [_unpack_rest(g[b]) for b in range(N_CHIPS)]
    w_uq, w_ukv = (jnp.concatenate([p[n] for p in parts], axis=1) for n in range(2))
    wproj = jnp.stack([jnp.concatenate([p[n] for p in parts], axis=0) for n in range(2, 5)], axis=0)
    return w_uq, w_ukv, wproj


MESH_ID = pl.DeviceIdType.MESH
ANY = pl.BlockSpec(memory_space=pl.ANY)
HALF_COLS = D_MODEL // 2


def _me():
    return lax.axis_index("x"), lax.axis_index("y"), lax.axis_index("c")


def _other_chips(x, y):
    return [(1 - x, y), (x, 1 - y), (1 - x, 1 - y)]


def _cols(c):
    return pl.ds(c * HALF_COLS, HALF_COLS)


RELAY_TOP = 992


def _gather_weights(w_blk):
    bot = W_IN_BLK - RELAY_TOP

    def body(w_ref, ow_ref, send_sems, recv_sems):
        x, y, c = _me()
        me, xn, yn, dg = 2 * x + y, 2 * (1 - x) + y, 2 * x + (1 - y), 2 * (1 - x) + (1 - y)
        to_x, to_y, to_sib = (1 - x, y, c), (x, 1 - y, c), (x, y, 1 - c)
        mine, theirs = _cols(c), _cols(1 - c)
        top, low = pl.ds(0, RELAY_TOP), pl.ds(RELAY_TOP, bot)

        def copy(k, src, dst, to):
            return pltpu.make_async_remote_copy(src_ref=src, dst_ref=dst, send_sem=send_sems.at[k],
                                                recv_sem=recv_sems.at[k], device_id=to, device_id_type=MESH_ID)

        def same(k, ref, to):
            return copy(k, ref, ref, to)

        own = [copy(0, w_ref.at[:, mine], ow_ref.at[me, :, mine], to_x),
               copy(1, w_ref.at[:, mine], ow_ref.at[me, :, mine], to_y)]
        for cp in own:
            cp.start()
        from_x, from_y = ow_ref.at[xn, :, mine], ow_ref.at[yn, :, mine]
        same(0, from_x, to_x).wait_recv()
        relay_y = same(2, ow_ref.at[xn, top, mine], to_y)
        pass_x = same(4, from_x, to_sib)
        relay_y.start()
        pass_x.start()
        same(1, from_y, to_y).wait_recv()
        relay_x = same(3, ow_ref.at[yn, low, mine], to_x)
        pass_y = same(5, from_y, to_sib)
        relay_x.start()
        pass_y.start()
        same(2, ow_ref.at[dg, top, mine], to_y).wait_recv()
        same(3, ow_ref.at[dg, low, mine], to_x).wait_recv()
        pass_d = same(6, ow_ref.at[dg, :, mine], to_sib)
        pass_d.start()
        for k, blk in ((4, xn), (5, yn), (6, dg)):
            same(k, ow_ref.at[blk, :, theirs], to_sib).wait_recv()
        for cp in own + [relay_y, relay_x, pass_x, pass_y, pass_d]:
            cp.wait_send()

    gw = pl.pallas_call(
        body, name="gather_weights", in_specs=[ANY], out_specs=ANY,
        out_shape=jax.ShapeDtypeStruct((N_CHIPS,) + w_blk.shape, w_blk.dtype),
        scratch_shapes=[pltpu.SemaphoreType.DMA((7,)), pltpu.SemaphoreType.DMA((7,))],
    )(w_blk)
    return lax.dynamic_update_slice(gw, w_blk[None], (2 * lax.axis_index("x") + lax.axis_index("y"), 0, 0))


def _swap_halves(gw):
    def body(gw_ref, lw_ref, send_sem, recv_sem):
        x, y, c = _me()
        cp = pltpu.make_async_remote_copy(
            src_ref=gw_ref.at[:, _cols(1 - c)], dst_ref=lw_ref, send_sem=send_sem, recv_sem=recv_sem,
            device_id=(x, y, 1 - c), device_id_type=MESH_ID)
        cp.start()
        cp.wait()

    return pl.pallas_call(
        body, name="grad_swap_halves", in_specs=[ANY], out_specs=ANY,
        out_shape=jax.ShapeDtypeStruct((gw.shape[0], HALF_COLS), gw.dtype),
        scratch_shapes=[pltpu.SemaphoreType.DMA, pltpu.SemaphoreType.DMA],
    )(gw)


def _swap_reduced(rw, rr):
    def body(rw_ref, rr_ref, ow_ref, or_ref, send_sems, recv_sems):
        x, y, c = _me()
        cps = [pltpu.make_async_remote_copy(
            src_ref=src, dst_ref=dst, send_sem=send_sems.at[a], recv_sem=recv_sems.at[a],
            device_id=(x, y, 1 - c), device_id_type=MESH_ID)
            for a, (src, dst) in enumerate([(rw_ref, ow_ref), (rr_ref, or_ref)])]
        for cp in cps:
            cp.start()
        for cp in cps:
            cp.wait()

    return pl.pallas_call(
        body, name="grad_swap_reduced", in_specs=[ANY, ANY], out_specs=[ANY, ANY],
        out_shape=[jax.ShapeDtypeStruct(rw.shape, rw.dtype), jax.ShapeDtypeStruct(rr.shape, rr.dtype)],
        scratch_shapes=[pltpu.SemaphoreType.DMA((2,)), pltpu.SemaphoreType.DMA((2,))],
    )(rw, rr)


def _join_cols(mine, theirs):
    first = lax.axis_index("c") == 0
    return jnp.concatenate([jnp.where(first, mine, theirs), jnp.where(first, theirs, mine)], axis=1)


def _gather_small(vec):
    def body(v_ref, out_ref, send_sems, recv_sems, local_sem):
        x, y, c = _me()
        my_id = 4 * x + 2 * y + c
        mine = pltpu.make_async_copy(v_ref, out_ref.at[my_id], local_sem)
        mine.start()
        cps = []
        for r in range(1, N_DEV):
            peer = (x ^ (r >> 2), y ^ ((r >> 1) & 1), c ^ (r & 1))
            cps.append(pltpu.make_async_remote_copy(
                src_ref=v_ref, dst_ref=out_ref.at[my_id], send_sem=send_sems.at[r - 1],
                recv_sem=recv_sems.at[r - 1], device_id=peer, device_id_type=MESH_ID))
        for cp in cps:
            cp.start()
        for cp in cps:
            cp.wait()
        mine.wait()

    return pl.pallas_call(
        body, name="gather_small", in_specs=[ANY], out_specs=ANY,
        out_shape=jax.ShapeDtypeStruct((N_DEV, 1, SMALL_COLS), vec.dtype),
        scratch_shapes=[pltpu.SemaphoreType.DMA((N_DEV - 1,)), pltpu.SemaphoreType.DMA((N_DEV - 1,)),
                        pltpu.SemaphoreType.DMA],
    )(vec)


def _add_cores(c_idx, g, landed, *, tm, name):
    r = g.shape[0]

    def body(c_ref, g_ref, l_ref, o32_ref, o16_ref):
        acc = g_ref[...] + l_ref[...]
        o32_ref[...] = acc
        o16_ref[...] = acc.astype(BF16)

    half = pl.BlockSpec((tm, HALF_COLS), lambda i, c_ref: (i, 0))
    grid_spec = pltpu.PrefetchScalarGridSpec(
        num_scalar_prefetch=1, grid=(r // tm,),
        in_specs=[pl.BlockSpec((tm, HALF_COLS), lambda i, c_ref: (i, c_ref[0])), half], out_specs=[half, half])
    return pl.pallas_call(
        body, name=name, grid_spec=grid_spec,
        out_shape=[jax.ShapeDtypeStruct((r, HALF_COLS), F32), jax.ShapeDtypeStruct((r, HALF_COLS), BF16)],
        compiler_params=_params(1),
    )(c_idx, g, landed)


def _add_chips(chip_idx, h32, landed, *, tm, name):
    n = landed.shape[1]
    per = n // tm

    def body(chip_ref, h_ref, l_ref, o_ref):
        acc = h_ref[...]
        for j in range(3):
            acc = acc + l_ref[j].astype(F32)
        o_ref[...] = acc

    grid_spec = pltpu.PrefetchScalarGridSpec(
        num_scalar_prefetch=1, grid=(per,),
        in_specs=[pl.BlockSpec((tm, HALF_COLS), lambda i, chip_ref: (chip_ref[0] * per + i, 0)),
                  pl.BlockSpec((3, tm, HALF_COLS), lambda i, chip_ref: (0, i, 0))],
        out_specs=pl.BlockSpec((tm, HALF_COLS), lambda i, chip_ref: (i, 0)))
    return pl.pallas_call(
        body, name=name, grid_spec=grid_spec, out_shape=jax.ShapeDtypeStruct((n, HALF_COLS), F32),
        compiler_params=_params(1),
    )(chip_idx, h32, landed)


def _pack_small(small, lb_logits, loss):
    def body(ng_ref, bg_ref, dlb_ref, lbl_ref, hgg_ref, qg_ref, kvg_ref, fg_ref, loss_ref, out_ref):
        out_ref[...] = jnp.zeros_like(out_ref)
        out_ref[:, 0:1024] = ng_ref[...]
        out_ref[:, 1024:3072] = bg_ref[...]
        _, p0p1 = _lower_bound(lbl_ref[...])
        dl0 = dlb_ref[...] * p0p1
        out_ref[:, 3072:4096] = dl0
        out_ref[:, 4096:5120] = -dl0
        hgg = hgg_ref[0]
        for h in range(1, HEADS):
            hgg = hgg + hgg_ref[h]
        out_ref[:, 5120:5248] = hgg
        out_ref[:, 5248:5632] = qg_ref[...]
        out_ref[:, 5632:5888] = kvg_ref[...]
        out_ref[:, 5888:6912] = fg_ref[...]
        out_ref[:, 6912:7040] = loss_ref[...]

    return pl.pallas_call(
        body, name="pack_small", out_shape=jax.ShapeDtypeStruct((1, SMALL_COLS), F32),
    )(small["norm_g"], small["b_gate"], small["lb"], lb_logits, small["hg_norm_g"], small["q_a_g"],
      small["kv_a_g"], small["final_norm_g"], loss)


def _adamw_math(w, g, m, v):
    nm = ADAM_B1 * m + (1.0 - ADAM_B1) * g
    nv = ADAM_B2 * v + (1.0 - ADAM_B2) * (g * g)
    m_hat = nm / (1.0 - ADAM_B1 ** ADAM_STEP)
    v_hat = nv / (1.0 - ADAM_B2 ** ADAM_STEP)
    return -ADAM_LR * (m_hat / (jnp.sqrt(v_hat) + ADAM_EPS) + ADAM_WD * w), nm, nv


def _adamw(w, g, m, v, *, name, tm):
    r, cols = w.shape

    def body(w_ref, g_ref, m_ref, v_ref, d_ref, nm_ref, nv_ref):
        d_ref[...], nm_ref[...], nv_ref[...] = _adamw_math(w_ref[...], g_ref[...], m_ref[...], v_ref[...])

    row = pl.BlockSpec((tm, cols), lambda i: (i, 0))
    shp = jax.ShapeDtypeStruct((r, cols), F32)
    return pl.pallas_call(
        body, name=name, grid=(r // tm,), in_specs=[row] * 4, out_specs=[row] * 3, out_shape=[shp] * 3,
        compiler_params=_params(1),
    )(w, g, m, v)


SMALL_SLOTS = (("norm_g", (0,)), ("b_gate", (1024,)), ("lb_logits", (3072, 4096)), ("hg_norm_g", (5120,)),
               ("q_a_g", (5248,)), ("kv_a_g", (5632,)), ("final_norm_g", (5888,)))
LOSS_SLOT = 6912


def _small_update(gathered, ws, ms, vs):
    n = len(SMALL_SLOTS)

    def body(*refs):
        g_ref = refs[0]
        w_refs, m_refs, v_refs = refs[1:1 + n], refs[1 + n:1 + 2 * n], refs[1 + 2 * n:1 + 3 * n]
        outs = refs[1 + 3 * n:]
        loss_ref = outs[0]
        g_out, d_out, nm_out, nv_out = (outs[1 + k * n:1 + (k + 1) * n] for k in range(4))
        total = g_ref[0]
        for dev in range(1, N_DEV):
            total = total + g_ref[dev]
        loss_ref[...] = total[:, LOSS_SLOT:LOSS_SLOT + 128]
        for p, (_, offsets) in enumerate(SMALL_SLOTS):
            cols = w_refs[p].shape[1]
            for r, off in enumerate(offsets):
                rows = slice(r, r + 1)
                g = total[:, off:off + cols]
                g_out[p][rows, :] = g
                d_out[p][rows, :], nm_out[p][rows, :], nv_out[p][rows, :] = _adamw_math(
                    w_refs[p][rows, :], g, m_refs[p][rows, :], v_refs[p][rows, :])

    shapes = [jax.ShapeDtypeStruct(w.shape, F32) for w in ws]
    res = pl.pallas_call(
        body, name="small_update", out_shape=[jax.ShapeDtypeStruct((1, 128), F32)] + shapes * 4,
    )(gathered, *ws, *ms, *vs)
    return res[0], res[1:1 + n], res[1 + n:1 + 2 * n], res[1 + 2 * n:1 + 3 * n], res[1 + 3 * n:1 + 4 * n]


def kernel(x, norm_g, w_in, b_gate, lb_logits, hg_norm_g, q_a_g, w_uq, kv_a_g, w_ukv, w_proj_a, w_proj_b, w_out, final_norm_g, loss_target, m_norm_g, m_w_in, m_b_gate, m_lb_logits, m_hg_norm_g, m_q_a_g, m_w_uq, m_kv_a_g, m_w_ukv, m_w_proj_a, m_w_proj_b, m_w_out, m_final_norm_g, v_norm_g, v_w_in, v_b_gate, v_lb_logits, v_hg_norm_g, v_q_a_g, v_w_uq, v_kv_a_g, v_w_ukv, v_w_proj_a, v_w_proj_b, v_w_out, v_final_norm_g):
    c_idx = lax.axis_index("c").astype(jnp.int32).reshape(1)
    chip_idx = (2 * lax.axis_index("x") + lax.axis_index("y")).astype(jnp.int32).reshape(1)

    w_blk = w_in[0].T.astype(BF16)
    r_blk = _pack_rest(w_uq[0], w_ukv[0], w_proj_a[0], w_proj_b[0], w_out[0]).astype(BF16)
    gw = _gather_weights(w_blk)

    loss, dh_args, d_w_int, d_rest, lr, small = _local_step(
        x[0], loss_target[0], gw.reshape(W_IN_COLS, D_MODEL), r_blk,
        norm_g, b_gate, lb_logits, hg_norm_g, q_a_g, kv_a_g, final_norm_g.reshape(1, D_MODEL))

    lw = _swap_halves(d_w_int)
    hw32, hw16 = _add_cores(c_idx, d_w_int, lw, tm=656, name="grad_add_cores_w")
    hr32, hr16 = _add_cores(c_idx, d_rest.reshape(N_CHIPS * REST_ROWS, D_MODEL), lr.reshape(N_CHIPS * REST_ROWS, HALF_COLS),
                            tm=REST_ROWS, name="grad_add_cores_r")
    grad_x, small["norm_g"], landed_w, landed_r = _dh_fused(*dh_args, hw16, hr16)
    rw = _add_chips(chip_idx, hw32, landed_w, tm=656, name="grad_add_chips_w")
    rr = _add_chips(chip_idx, hr32, landed_r, tm=208, name="grad_add_chips_r")
    tw, tr = _swap_reduced(rw, rr)
    g_w_in = _join_cols(rw, tw).T
    g_rest = _join_cols(rr, tr)
    g_uq, g_ukv, g_pa, g_pb, g_out = _unpack_rest(g_rest)

    small_all = _gather_small(_pack_small(small, lb_logits, loss))

    upd = {
        "w_in": _adamw(w_in[0], g_w_in, m_w_in[0], v_w_in[0], name="adamw_w_in", tm=128),
        "w_uq": _adamw(w_uq[0], g_uq, m_w_uq[0], v_w_uq[0], name="adamw_w_uq", tm=Q_LORA),
        "w_ukv": _adamw(w_ukv[0], g_ukv, m_w_ukv[0], v_w_ukv[0], name="adamw_w_ukv", tm=KV_LORA),
        "w_proj_a": _adamw(w_proj_a[0], g_pa, m_w_proj_a[0], v_w_proj_a[0], name="adamw_w_proj_a", tm=256),
        "w_proj_b": _adamw(w_proj_b[0], g_pb, m_w_proj_b[0], v_w_proj_b[0], name="adamw_w_proj_b", tm=256),
        "w_out": _adamw(w_out[0], g_out, m_w_out[0], v_w_out[0], name="adamw_w_out", tm=256),
    }
    loss_vec, *small_sets = _small_update(
        small_all,
        [norm_g, b_gate, lb_logits, hg_norm_g, q_a_g, kv_a_g, final_norm_g.reshape(1, D_MODEL)],
        [m_norm_g, m_b_gate, m_lb_logits, m_hg_norm_g, m_q_a_g, m_kv_a_g, m_final_norm_g.reshape(1, D_MODEL)],
        [v_norm_g, v_b_gate, v_lb_logits, v_hg_norm_g, v_q_a_g, v_kv_a_g, v_final_norm_g.reshape(1, D_MODEL)])

    def outputs(big, small_set):
        s_ng, s_bg, s_lb, s_hg, s_qg, s_kvg, s_fg = small_set
        return (s_ng, big["w_in"][None], s_bg, s_lb, s_hg, s_qg, big["w_uq"][None], s_kvg, big["w_ukv"][None],
                big["w_proj_a"][None], big["w_proj_b"][None], big["w_out"][None], s_fg.reshape(D_MODEL))

    grads = {"w_in": g_w_in, "w_uq": g_uq, "w_ukv": g_ukv, "w_proj_a": g_pa, "w_proj_b": g_pb, "w_out": g_out}
    return (loss_vec[0, 0], grad_x[None], *outputs(grads, small_sets[0]),
            *(o for k in range(3) for o in outputs({n: u[k] for n, u in upd.items()}, small_sets[1 + k])))
```
